```python
import jax, jax.numpy as jnp
from jax import lax
import numpy as np

D_MODEL = 1024
BATCH = 8
SEQ = 4096
DEPTH = 2

N_MIXERS = 2
N_ATTN_LAYERS = (DEPTH + 1) // 2
N_HGRN_LAYERS = DEPTH // 2
DILATED_GROUPS = ((128, 1), (512, 4), (2048, 16))
N_GROUPS = len(DILATED_GROUPS)
ATTN_HEADS = 8
ATTN_HEAD_DIM = D_MODEL // ATTN_HEADS
ATTN_WIDTH = ATTN_HEADS * ATTN_HEAD_DIM
NUM_BUCKETS = 32
MAX_DISTANCE = 2048
HGRN_HEADS = 8
HGRN_KDIM = 128
HGRN_VDIM = D_MODEL // HGRN_HEADS
HGRN_FWD = HGRN_HEADS * HGRN_KDIM
HGRN_CHUNK = 64
FFN_HIDDEN = ((8 * D_MODEL // 3 + 255) // 256) * 256
RMS_EPS = 1e-6

kernel_name = "hybrid_dilated_attn_hgrn2_swiglu_adaln"


def rms_norm(x, gain, eps=RMS_EPS):
    xf = x.astype(jnp.float32)
    y = xf * lax.rsqrt(jnp.mean(xf * xf, axis=-1, keepdims=True) + eps)
    return (y * gain.astype(jnp.float32)).astype(x.dtype)


def t5_bucket(dist):
    n = np.asarray(dist, dtype=np.int64)
    max_exact = NUM_BUCKETS // 2
    large = max_exact + (np.log(np.maximum(n, 1) / max_exact)
                         / np.log(MAX_DISTANCE / max_exact)
                         * (NUM_BUCKETS - max_exact)).astype(np.int64)
    large = np.minimum(large, NUM_BUCKETS - 1)
    return np.where(n < max_exact, n, large).astype(np.int32)


def dilated_window_group(q, k, v, bias_j, window, dilation):
    B, S, H, Dh = q.shape
    d = dilation
    blk = window // dilation
    span = d * blk
    s_pad = -(-S // span) * span
    pad = s_pad - S
    if pad:
        padw = ((0, 0), (0, pad), (0, 0), (0, 0))
        q, k, v = jnp.pad(q, padw), jnp.pad(k, padw), jnp.pad(v, padw)
    L = s_pad // d
    nb = L // blk

    def to_sub(t):
        X = t.shape[-1]
        return t.reshape(B, L, d, H, X).transpose(0, 2, 1, 3, 4).reshape(B, d, nb, blk, H, X)

    def from_sub(t):
        X = t.shape[-1]
        return t.reshape(B, d, L, H, X).transpose(0, 2, 1, 3, 4).reshape(B, s_pad, H, X)[:, :S]

    qb, kb, vb = to_sub(q), to_sub(k), to_sub(v)
    def with_prev(t):
        prev = jnp.concatenate([jnp.zeros_like(t[:, :, :1]), t[:, :, :-1]], axis=2)
        return jnp.concatenate([prev, t], axis=3)
    kk, vv = with_prev(kb), with_prev(vb)

    qi = np.arange(blk)[:, None]
    ki = np.arange(2 * blk)[None, :]
    j = blk + qi - ki
    band = (j >= 0) & (j <= blk)
    mask = np.broadcast_to(band, (nb, blk, 2 * blk)).copy()
    mask[0, :, :blk] = False
    mask = jnp.asarray(mask.reshape(1, 1, nb, blk, 1, 2 * blk))
    bias = bias_j[np.clip(j, 0, blk)].transpose(0, 2, 1)

    s = jnp.einsum('brnqhd,brnkhd->brnqhk', qb, kk) * (Dh ** -0.5) + bias
    s = jnp.where(mask, s, -jnp.inf)
    m = jnp.max(s, axis=-1, keepdims=True)
    p = jnp.exp(s - m)
    l = jnp.sum(p, axis=-1, keepdims=True)
    o = jnp.einsum('brnqhk,brnkhd->brnqhd', p, vv) / l
    return from_sub(o), from_sub(m), from_sub(l)


def dilated_attention(h, w_qkv, w_out, q_gain, k_gain, rel_bias):
    B, S, _ = h.shape
    qkv = (h @ w_qkv).astype(jnp.float32).reshape(B, S, N_GROUPS, 3, ATTN_HEADS, ATTN_HEAD_DIM)
    outs, maxes, dens = [], [], []
    for g, (window, dil) in enumerate(DILATED_GROUPS):
        q = rms_norm(qkv[:, :, g, 0], q_gain[g])
        k = rms_norm(qkv[:, :, g, 1], k_gain[g])
        v = qkv[:, :, g, 2]
        steps = window // dil
        bucket = t5_bucket(np.arange(steps + 1) * dil)
        bias_j = rel_bias[bucket, g * ATTN_HEADS:(g + 1) * ATTN_HEADS].astype(jnp.float32)
        o, m, l = dilated_window_group(q, k, v, bias_j, window, dil)
        outs.append(o); maxes.append(m); dens.append(l)
    o = jnp.stack(outs)
    m = jnp.stack(maxes)
    l = jnp.stack(dens)
    wts = l * jnp.exp(m - jnp.max(m, axis=0, keepdims=True))
    out = jnp.sum(wts * o, axis=0) / jnp.sum(wts, axis=0)
    return out.reshape(B, S, ATTN_WIDTH).astype(h.dtype) @ w_out


def hgrn2_mixer(h, w_in, w_out, gnorm_gain, lower_bound):
    B, S, _ = h.shape
    proj = h @ w_in
    q, f, i, g = jnp.split(proj, [HGRN_FWD, 2 * HGRN_FWD, 2 * HGRN_FWD + D_MODEL], axis=-1)
    q = jax.nn.silu(q.astype(jnp.float32))
    lb = lower_bound.astype(jnp.float32)
    fg = lb + (1.0 - lb) * jax.nn.sigmoid(f.astype(jnp.float32))
    logf = jnp.log(fg)
    kin = 1.0 - fg
    nc = S // HGRN_CHUNK

    def chunk(t, dim):
        t = t.reshape(B, S, HGRN_HEADS, dim).transpose(0, 2, 1, 3)
        return jnp.moveaxis(t.reshape(B, HGRN_HEADS, nc, HGRN_CHUNK, dim), 2, 0)

    qc, kc, gc = chunk(q, HGRN_KDIM), chunk(kin, HGRN_KDIM), chunk(logf, HGRN_KDIM)
    vc = chunk(i.astype(jnp.float32), HGRN_VDIM)
    tri = jnp.asarray(np.tril(np.ones((HGRN_CHUNK, HGRN_CHUNK), dtype=bool))[None, None, :, :, None])

    def step(state, xs):
        qq, kq, vq, gq = xs
        G = jnp.cumsum(gq, axis=2)
        diff = G[:, :, :, None, :] - G[:, :, None, :, :]
        decay = jnp.exp(jnp.where(tri, diff, -jnp.inf))
        A = jnp.einsum('bhtk,bhsk,bhtsk->bhts', qq, kq, decay)
        o = jnp.einsum('bhts,bhsv->bhtv', A, vq) + jnp.einsum('bhtk,bhkv->bhtv', qq * jnp.exp(G), state)
        G_last = G[:, :, -1:, :]
        new_state = (jnp.exp(G_last[:, :, 0, :])[..., None] * state
                     + jnp.einsum('bhsk,bhsv->bhkv', kq * jnp.exp(G_last - G), vq))
        return new_state, o

    state0 = jnp.zeros((B, HGRN_HEADS, HGRN_KDIM, HGRN_VDIM), jnp.float32)
    _, o = lax.scan(step, state0, (qc, kc, vc, gc))
    o = jnp.moveaxis(o, 0, 2).reshape(B, HGRN_HEADS, S, HGRN_VDIM).transpose(0, 2, 1, 3)
    o = rms_norm(o, gnorm_gain).reshape(B, S, D_MODEL)
    o = o * jax.nn.silu(g.astype(jnp.float32))
    return o.astype(h.dtype) @ w_out


def swiglu(h, w1, w3, w2):
    return (jax.nn.silu(h @ w1) * (h @ w3)) @ w2


def _fwd_setup_inputs(seed: int = 0) -> dict:
    key = jax.random.key(seed)
    ks = jax.random.split(key, 20)
    f32 = jnp.float32
    def nrm(k, shape, fan_in, mult=1.0):
        return jax.random.normal(k, shape, f32) * (mult * fan_in ** -0.5)
    def gain(k, shape):
        return 1.0 + 0.05 * jax.random.normal(k, shape, f32)
    return {
        "x": jax.random.normal(ks[0], (BATCH, SEQ, D_MODEL), f32),
        "c": jax.random.normal(ks[1], (BATCH, D_MODEL), f32),
        "rel_bias": 0.5 * jax.random.normal(ks[2], (NUM_BUCKETS, N_GROUPS * ATTN_HEADS), f32),
        "ada_w": nrm(ks[3], (DEPTH, D_MODEL, 6 * D_MODEL), D_MODEL, 0.5),
        "ada_b": 0.02 * jax.random.normal(ks[4], (DEPTH, 6 * D_MODEL), f32),
        "norm_mix": gain(ks[5], (DEPTH, D_MODEL)),
        "norm_ffn": gain(ks[6], (DEPTH, D_MODEL)),
        "attn_w_qkv": nrm(ks[7], (N_ATTN_LAYERS, D_MODEL, N_GROUPS * 3 * ATTN_WIDTH), D_MODEL),
        "attn_w_out": nrm(ks[8], (N_ATTN_LAYERS, ATTN_WIDTH, D_MODEL), ATTN_WIDTH),
        "attn_q_gain": gain(ks[9], (N_ATTN_LAYERS, N_GROUPS, ATTN_HEAD_DIM)),
        "attn_k_gain": gain(ks[10], (N_ATTN_LAYERS, N_GROUPS, ATTN_HEAD_DIM)),
        "hgrn_w_in": nrm(ks[11], (N_HGRN_LAYERS, D_MODEL, 2 * HGRN_FWD + 2 * D_MODEL), D_MODEL),
        "hgrn_w_out": nrm(ks[12], (N_HGRN_LAYERS, D_MODEL, D_MODEL), D_MODEL),
        "hgrn_gnorm": gain(ks[13], (N_HGRN_LAYERS, HGRN_VDIM)),
        "hgrn_lower_bounds": jax.random.normal(ks[14], (DEPTH, HGRN_FWD), f32),
        "ffn_w1": nrm(ks[15], (DEPTH, D_MODEL, FFN_HIDDEN), D_MODEL),
        "ffn_w3": nrm(ks[16], (DEPTH, D_MODEL, FFN_HIDDEN), D_MODEL),
        "ffn_w2": nrm(ks[17], (DEPTH, FFN_HIDDEN, D_MODEL), FFN_HIDDEN),
    }


def _fwd_reference(x, c, rel_bias, ada_w, ada_b, norm_mix, norm_ffn, attn_w_qkv, attn_w_out,
              attn_q_gain, attn_k_gain, hgrn_w_in, hgrn_w_out, hgrn_gnorm, hgrn_lower_bounds,
              ffn_w1, ffn_w3, ffn_w2):
    p = jax.nn.softmax(hgrn_lower_bounds.astype(jnp.float32), axis=0)
    lbs = jnp.cumsum(p, axis=0) - p[0]
    c_act = jax.nn.silu(c)
    for i in range(DEPTH):
        mod = c_act @ ada_w[i] + ada_b[i]
        sh1, sc1, g1, sh2, sc2, g2 = jnp.split(mod[:, None, :], 6, axis=-1)
        h = rms_norm(x, norm_mix[i]) * (1.0 + sc1) + sh1
        a = i // N_MIXERS
        if i % N_MIXERS == 0:
            y = dilated_attention(h, attn_w_qkv[a], attn_w_out[a], attn_q_gain[a],
                                  attn_k_gain[a], rel_bias)
        else:
            y = hgrn2_mixer(h, hgrn_w_in[a], hgrn_w_out[a], hgrn_gnorm[a], lbs[i])
        x = x + g1 * y
        h = rms_norm(x, norm_ffn[i]) * (1.0 + sc2) + sh2
        x = x + g2 * swiglu(h, ffn_w1[i], ffn_w3[i], ffn_w2[i])
    return x


import jax as _jax
import jax.numpy as _jnp

TWIN_FORMAT = 'train_step'
FWD_PARAMS = ['x', 'c', 'rel_bias', 'ada_w', 'ada_b', 'norm_mix', 'norm_ffn', 'attn_w_qkv', 'attn_w_out', 'attn_q_gain', 'attn_k_gain', 'hgrn_w_in', 'hgrn_w_out', 'hgrn_gnorm', 'hgrn_lower_bounds', 'ffn_w1', 'ffn_w3', 'ffn_w2']
TWIN_WEIGHTS = ['rel_bias', 'ada_w', 'ada_b', 'norm_mix', 'norm_ffn', 'attn_w_qkv', 'attn_w_out', 'attn_q_gain', 'attn_k_gain', 'hgrn_w_in', 'hgrn_w_out', 'hgrn_gnorm', 'hgrn_lower_bounds', 'ffn_w1', 'ffn_w3', 'ffn_w2']
TWIN_DIFF_INPUT = 'x'
TWIN_INPUTS = ['x', 'c', 'rel_bias', 'ada_w', 'ada_b', 'norm_mix', 'norm_ffn', 'attn_w_qkv', 'attn_w_out', 'attn_q_gain', 'attn_k_gain', 'hgrn_w_in', 'hgrn_w_out', 'hgrn_gnorm', 'hgrn_lower_bounds', 'ffn_w1', 'ffn_w3', 'ffn_w2', 'loss_target', 'm_rel_bias', 'm_ada_w', 'm_ada_b', 'm_norm_mix', 'm_norm_ffn', 'm_attn_w_qkv', 'm_attn_w_out', 'm_attn_q_gain', 'm_attn_k_gain', 'm_hgrn_w_in', 'm_hgrn_w_out', 'm_hgrn_gnorm', 'm_hgrn_lower_bounds', 'm_ffn_w1', 'm_ffn_w3', 'm_ffn_w2', 'v_rel_bias', 'v_ada_w', 'v_ada_b', 'v_norm_mix', 'v_norm_ffn', 'v_attn_w_qkv', 'v_attn_w_out', 'v_attn_q_gain', 'v_attn_k_gain', 'v_hgrn_w_in', 'v_hgrn_w_out', 'v_hgrn_gnorm', 'v_hgrn_lower_bounds', 'v_ffn_w1', 'v_ffn_w3', 'v_ffn_w2']
TWIN_OUTPUTS = ['loss', 'grad_x', 'grad_rel_bias', 'grad_ada_w', 'grad_ada_b', 'grad_norm_mix', 'grad_norm_ffn', 'grad_attn_w_qkv', 'grad_attn_w_out', 'grad_attn_q_gain', 'grad_attn_k_gain', 'grad_hgrn_w_in', 'grad_hgrn_w_out', 'grad_hgrn_gnorm', 'grad_hgrn_lower_bounds', 'grad_ffn_w1', 'grad_ffn_w3', 'grad_ffn_w2', 'delta_rel_bias', 'delta_ada_w', 'delta_ada_b', 'delta_norm_mix', 'delta_norm_ffn', 'delta_attn_w_qkv', 'delta_attn_w_out', 'delta_attn_q_gain', 'delta_attn_k_gain', 'delta_hgrn_w_in', 'delta_hgrn_w_out', 'delta_hgrn_gnorm', 'delta_hgrn_lower_bounds', 'delta_ffn_w1', 'delta_ffn_w3', 'delta_ffn_w2', 'new_m_rel_bias', 'new_m_ada_w', 'new_m_ada_b', 'new_m_norm_mix', 'new_m_norm_ffn', 'new_m_attn_w_qkv', 'new_m_attn_w_out', 'new_m_attn_q_gain', 'new_m_attn_k_gain', 'new_m_hgrn_w_in', 'new_m_hgrn_w_out', 'new_m_hgrn_gnorm', 'new_m_hgrn_lower_bounds', 'new_m_ffn_w1', 'new_m_ffn_w3', 'new_m_ffn_w2', 'new_v_rel_bias', 'new_v_ada_w', 'new_v_ada_b', 'new_v_norm_mix', 'new_v_norm_ffn', 'new_v_attn_w_qkv', 'new_v_attn_w_out', 'new_v_attn_q_gain', 'new_v_attn_k_gain', 'new_v_hgrn_w_in', 'new_v_hgrn_w_out', 'new_v_hgrn_gnorm', 'new_v_hgrn_lower_bounds', 'new_v_ffn_w1', 'new_v_ffn_w3', 'new_v_ffn_w2']
TWIN_LEAF_KINDS = {'loss': 'loss', 'grad_x': 'grad_x', 'grad_rel_bias': 'grad_w', 'grad_ada_w': 'grad_w', 'grad_ada_b': 'grad_w', 'grad_norm_mix': 'grad_w', 'grad_norm_ffn': 'grad_w', 'grad_attn_w_qkv': 'grad_w', 'grad_attn_w_out': 'grad_w', 'grad_attn_q_gain': 'grad_w', 'grad_attn_k_gain': 'grad_w', 'grad_hgrn_w_in': 'grad_w', 'grad_hgrn_w_out': 'grad_w', 'grad_hgrn_gnorm': 'grad_w', 'grad_hgrn_lower_bounds': 'grad_w', 'grad_ffn_w1': 'grad_w', 'grad_ffn_w3': 'grad_w', 'grad_ffn_w2': 'grad_w', 'delta_rel_bias': 'delta_w', 'delta_ada_w': 'delta_w', 'delta_ada_b': 'delta_w', 'delta_norm_mix': 'delta_w', 'delta_norm_ffn': 'delta_w', 'delta_attn_w_qkv': 'delta_w', 'delta_attn_w_out': 'delta_w', 'delta_attn_q_gain': 'delta_w', 'delta_attn_k_gain': 'delta_w', 'delta_hgrn_w_in': 'delta_w', 'delta_hgrn_w_out': 'delta_w', 'delta_hgrn_gnorm': 'delta_w', 'delta_hgrn_lower_bounds': 'delta_w', 'delta_ffn_w1': 'delta_w', 'delta_ffn_w3': 'delta_w', 'delta_ffn_w2': 'delta_w', 'new_m_rel_bias': 'new_m', 'new_m_ada_w': 'new_m', 'new_m_ada_b': 'new_m', 'new_m_norm_mix': 'new_m', 'new_m_norm_ffn': 'new_m', 'new_m_attn_w_qkv': 'new_m', 'new_m_attn_w_out': 'new_m', 'new_m_attn_q_gain': 'new_m', 'new_m_attn_k_gain': 'new_m', 'new_m_hgrn_w_in': 'new_m', 'new_m_hgrn_w_out': 'new_m', 'new_m_hgrn_gnorm': 'new_m', 'new_m_hgrn_lower_bounds': 'new_m', 'new_m_ffn_w1': 'new_m', 'new_m_ffn_w3': 'new_m', 'new_m_ffn_w2': 'new_m', 'new_v_rel_bias': 'new_v', 'new_v_ada_w': 'new_v', 'new_v_ada_b': 'new_v', 'new_v_norm_mix': 'new_v', 'new_v_norm_ffn': 'new_v', 'new_v_attn_w_qkv': 'new_v', 'new_v_attn_w_out': 'new_v', 'new_v_attn_q_gain': 'new_v', 'new_v_attn_k_gain': 'new_v', 'new_v_hgrn_w_in': 'new_v', 'new_v_hgrn_w_out': 'new_v', 'new_v_hgrn_gnorm': 'new_v', 'new_v_hgrn_lower_bounds': 'new_v', 'new_v_ffn_w1': 'new_v', 'new_v_ffn_w3': 'new_v', 'new_v_ffn_w2': 'new_v'}


def _forward(args):
    return _fwd_reference(*[args[k] for k in FWD_PARAMS])


def _output_shape():
    def fwd():
        inp = _fwd_setup_inputs(0)
        return _fwd_reference(*[inp[k] for k in FWD_PARAMS])
    out = _jax.eval_shape(fwd)
    return out.shape, out.dtype

N_MICROBATCH = 1
ADAM_LR = 0.001
ADAM_B1 = 0.9
ADAM_B2 = 0.999
ADAM_EPS = 1e-08
ADAM_WD = 0.01
ADAM_STEP = 10
PER_EXAMPLE_BATCH_AXIS = {'x': 0, 'c': 0, 'loss_target': 0}
SHARED_INPUTS = []
_WEIGHT_DTYPES = {'rel_bias': _jnp.float32, 'ada_w': _jnp.float32, 'ada_b': _jnp.float32, 'norm_mix': _jnp.float32, 'norm_ffn': _jnp.float32, 'attn_w_qkv': _jnp.float32, 'attn_w_out': _jnp.float32, 'attn_q_gain': _jnp.float32, 'attn_k_gain': _jnp.float32, 'hgrn_w_in': _jnp.float32, 'hgrn_w_out': _jnp.float32, 'hgrn_gnorm': _jnp.float32, 'hgrn_lower_bounds': _jnp.float32, 'ffn_w1': _jnp.float32, 'ffn_w3': _jnp.float32, 'ffn_w2': _jnp.float32}
MOMENT_SCALE = {'rel_bias': 6.756768e-02, 'ada_w': 6.577930e-01, 'ada_b': 1.734977e+00, 'norm_mix': 9.947793e-01, 'norm_ffn': 3.197668e+00, 'attn_w_qkv': 2.468509e-02, 'attn_w_out': 7.036773e-02, 'attn_q_gain': 9.243554e-02, 'attn_k_gain': 9.277829e-02, 'hgrn_w_in': 9.824753e-02, 'hgrn_w_out': 9.471356e-02, 'hgrn_gnorm': 1.024726e+01, 'hgrn_lower_bounds': 4.433097e-03, 'ffn_w1': 6.507068e-02, 'ffn_w3': 4.879851e-02, 'ffn_w2': 7.678004e-02}


def _to_microbatches(a, axis):
    t = _jnp.moveaxis(a, axis, 0)
    t = t.reshape((N_MICROBATCH, t.shape[0] // N_MICROBATCH) + t.shape[1:])
    return _jnp.moveaxis(t, 1, axis + 1)


def setup_inputs(seed: int = 0) -> dict:
    inp = _fwd_setup_inputs(seed)
    key = _jax.random.fold_in(_jax.random.key(seed), 7919)
    shape, _ = _output_shape()
    out = dict(inp)
    out["loss_target"] = _jax.random.normal(_jax.random.fold_in(key, 0), shape, _jnp.float32)
    for i, name in enumerate(TWIN_WEIGHTS):
        w = inp[name].astype(_jnp.float32)
        if MOMENT_SCALE is None:
            s = _jnp.sqrt(_jnp.mean(_jnp.square(w)) + 1e-30)
        else:
            s = MOMENT_SCALE[name]
        km, kv = _jax.random.split(_jax.random.fold_in(key, i + 1))
        out[name] = w
        out["m_" + name] = s * _jax.random.normal(km, w.shape, _jnp.float32)
        out["v_" + name] = (s * s) * _jax.random.uniform(kv, w.shape, _jnp.float32, 0.5, 1.5)
    if N_MICROBATCH > 1:
        for name, axis in PER_EXAMPLE_BATCH_AXIS.items():
            out[name] = _to_microbatches(out[name], axis)
    return {'x': out['x'], 'c': out['c'], 'rel_bias': out['rel_bias'], 'ada_w': out['ada_w'], 'ada_b': out['ada_b'], 'norm_mix': out['norm_mix'], 'norm_ffn': out['norm_ffn'], 'attn_w_qkv': out['attn_w_qkv'], 'attn_w_out': out['attn_w_out'], 'attn_q_gain': out['attn_q_gain'], 'attn_k_gain': out['attn_k_gain'], 'hgrn_w_in': out['hgrn_w_in'], 'hgrn_w_out': out['hgrn_w_out'], 'hgrn_gnorm': out['hgrn_gnorm'], 'hgrn_lower_bounds': out['hgrn_lower_bounds'], 'ffn_w1': out['ffn_w1'], 'ffn_w3': out['ffn_w3'], 'ffn_w2': out['ffn_w2'], 'loss_target': out['loss_target'], 'm_rel_bias': out['m_rel_bias'], 'm_ada_w': out['m_ada_w'], 'm_ada_b': out['m_ada_b'], 'm_norm_mix': out['m_norm_mix'], 'm_norm_ffn': out['m_norm_ffn'], 'm_attn_w_qkv': out['m_attn_w_qkv'], 'm_attn_w_out': out['m_attn_w_out'], 'm_attn_q_gain': out['m_attn_q_gain'], 'm_attn_k_gain': out['m_attn_k_gain'], 'm_hgrn_w_in': out['m_hgrn_w_in'], 'm_hgrn_w_out': out['m_hgrn_w_out'], 'm_hgrn_gnorm': out['m_hgrn_gnorm'], 'm_hgrn_lower_bounds': out['m_hgrn_lower_bounds'], 'm_ffn_w1': out['m_ffn_w1'], 'm_ffn_w3': out['m_ffn_w3'], 'm_ffn_w2': out['m_ffn_w2'], 'v_rel_bias': out['v_rel_bias'], 'v_ada_w': out['v_ada_w'], 'v_ada_b': out['v_ada_b'], 'v_norm_mix': out['v_norm_mix'], 'v_norm_ffn': out['v_norm_ffn'], 'v_attn_w_qkv': out['v_attn_w_qkv'], 'v_attn_w_out': out['v_attn_w_out'], 'v_attn_q_gain': out['v_attn_q_gain'], 'v_attn_k_gain': out['v_attn_k_gain'], 'v_hgrn_w_in': out['v_hgrn_w_in'], 'v_hgrn_w_out': out['v_hgrn_w_out'], 'v_hgrn_gnorm': out['v_hgrn_gnorm'], 'v_hgrn_lower_bounds': out['v_hgrn_lower_bounds'], 'v_ffn_w1': out['v_ffn_w1'], 'v_ffn_w3': out['v_ffn_w3'], 'v_ffn_w2': out['v_ffn_w2']}


def _loss(weights, diff, rest, loss_target):
    with _jax.named_scope("forward"):
        args = {**rest, TWIN_DIFF_INPUT: diff, **{k: w.astype(_WEIGHT_DTYPES[k]) for k, w in weights.items()}}
        y = _forward(args)
    with _jax.named_scope("loss_head"):
        err = _jnp.square(y.astype(_jnp.float32) - loss_target)
        return 0.5 * _jnp.sum(_jnp.mean(err, axis=-1)) if err.ndim else 0.5 * err


def _adamw(w, g, m, v):
    m = ADAM_B1 * m + (1.0 - ADAM_B1) * g
    v = ADAM_B2 * v + (1.0 - ADAM_B2) * _jnp.square(g)
    m_hat = m / (1.0 - ADAM_B1 ** ADAM_STEP)
    v_hat = v / (1.0 - ADAM_B2 ** ADAM_STEP)
    delta = -ADAM_LR * (m_hat / (_jnp.sqrt(v_hat) + ADAM_EPS) + ADAM_WD * w)
    return delta, m, v


def reference(x, c, rel_bias, ada_w, ada_b, norm_mix, norm_ffn, attn_w_qkv, attn_w_out, attn_q_gain, attn_k_gain, hgrn_w_in, hgrn_w_out, hgrn_gnorm, hgrn_lower_bounds, ffn_w1, ffn_w3, ffn_w2, loss_target, m_rel_bias, m_ada_w, m_ada_b, m_norm_mix, m_norm_ffn, m_attn_w_qkv, m_attn_w_out, m_attn_q_gain, m_attn_k_gain, m_hgrn_w_in, m_hgrn_w_out, m_hgrn_gnorm, m_hgrn_lower_bounds, m_ffn_w1, m_ffn_w3, m_ffn_w2, v_rel_bias, v_ada_w, v_ada_b, v_norm_mix, v_norm_ffn, v_attn_w_qkv, v_attn_w_out, v_attn_q_gain, v_attn_k_gain, v_hgrn_w_in, v_hgrn_w_out, v_hgrn_gnorm, v_hgrn_lower_bounds, v_ffn_w1, v_ffn_w3, v_ffn_w2):
    given = dict(x=x, c=c, rel_bias=rel_bias, ada_w=ada_w, ada_b=ada_b, norm_mix=norm_mix, norm_ffn=norm_ffn, attn_w_qkv=attn_w_qkv, attn_w_out=attn_w_out, attn_q_gain=attn_q_gain, attn_k_gain=attn_k_gain, hgrn_w_in=hgrn_w_in, hgrn_w_out=hgrn_w_out, hgrn_gnorm=hgrn_gnorm, hgrn_lower_bounds=hgrn_lower_bounds, ffn_w1=ffn_w1, ffn_w3=ffn_w3, ffn_w2=ffn_w2, loss_target=loss_target, m_rel_bias=m_rel_bias, m_ada_w=m_ada_w, m_ada_b=m_ada_b, m_norm_mix=m_norm_mix, m_norm_ffn=m_norm_ffn, m_attn_w_qkv=m_attn_w_qkv, m_attn_w_out=m_attn_w_out, m_attn_q_gain=m_attn_q_gain, m_attn_k_gain=m_attn_k_gain, m_hgrn_w_in=m_hgrn_w_in, m_hgrn_w_out=m_hgrn_w_out, m_hgrn_gnorm=m_hgrn_gnorm, m_hgrn_lower_bounds=m_hgrn_lower_bounds, m_ffn_w1=m_ffn_w1, m_ffn_w3=m_ffn_w3, m_ffn_w2=m_ffn_w2, v_rel_bias=v_rel_bias, v_ada_w=v_ada_w, v_ada_b=v_ada_b, v_norm_mix=v_norm_mix, v_norm_ffn=v_norm_ffn, v_attn_w_qkv=v_attn_w_qkv, v_attn_w_out=v_attn_w_out, v_attn_q_gain=v_attn_q_gain, v_attn_k_gain=v_attn_k_gain, v_hgrn_w_in=v_hgrn_w_in, v_hgrn_w_out=v_hgrn_w_out, v_hgrn_gnorm=v_hgrn_gnorm, v_hgrn_lower_bounds=v_hgrn_lower_bounds, v_ffn_w1=v_ffn_w1, v_ffn_w3=v_ffn_w3, v_ffn_w2=v_ffn_w2)
    weights = {n: given[n] for n in TWIN_WEIGHTS}
    shared = {n: given[n] for n in SHARED_INPUTS}
    per_example = {n: given[n] for n in ['x', 'c']}
    grad_fn = _jax.value_and_grad(_loss, argnums=(0, 1))

    def one_microbatch(ex, loss_target):
        ex = dict(ex)
        diff = ex.pop(TWIN_DIFF_INPUT)
        return grad_fn(weights, diff, {**shared, **ex}, loss_target)

    if N_MICROBATCH == 1:
        loss, (grad_w, grad_x) = one_microbatch(per_example, given["loss_target"])
    else:
        def body(carry, xs):
            loss_sum, grad_sum = carry
            l_k, (gw_k, gx_k) = one_microbatch(xs[0], xs[1])
            with _jax.named_scope("update"):
                return (loss_sum + l_k, _jax.tree.map(_jnp.add, grad_sum, gw_k)), gx_k

        init = (_jnp.zeros((), _jnp.float32), _jax.tree.map(_jnp.zeros_like, weights))
        (loss, grad_w), grad_x = _jax.lax.scan(body, init, (per_example, given["loss_target"]))
    with _jax.named_scope("update"):
        delta_w, new_m, new_v = {}, {}, {}
        for n in TWIN_WEIGHTS:
            delta_w[n], new_m[n], new_v[n] = _adamw(weights[n], grad_w[n], given["m_" + n], given["v_" + n])
    return (loss, grad_x, *[grad_w[n] for n in TWIN_WEIGHTS], *[delta_w[n] for n in TWIN_WEIGHTS],
            *[new_m[n] for n in TWIN_WEIGHTS], *[new_v[n] for n in TWIN_WEIGHTS])
```

```python
import functools

import numpy as np
import jax
import jax.numpy as jnp
from jax import lax
from jax.experimental import pallas as pl
from jax.experimental.pallas import tpu as pltpu

F32 = jnp.float32
BF16 = jnp.bfloat16

D_MODEL = 1024
SEQ = 4096
N_DEV = 8
N_CHIP = 4
DEPTH = 2
HEADS = 8
HEAD_DIM = 128
GROUPS = ((128, 1), (512, 4), (2048, 16))
ATT_BLK = 128
NUM_BUCKETS = 32
MAX_DISTANCE = 2048
FFN_HIDDEN = 2816
FFN_SHARD = FFN_HIDDEN // N_CHIP
HG_SUB = 16
HG_TC = 512
RMS_EPS = 1e-6
NEG = -1e30
ATT_SCALE = HEAD_DIM ** -0.5
ADAM_LR, ADAM_B1, ADAM_B2, ADAM_EPS, ADAM_WD, ADAM_STEP = 0.001, 0.9, 0.999, 1e-08, 0.01, 10
VMEM_LIMIT = 56 * 1024 * 1024
MESH = pl.DeviceIdType.MESH


def _pcall(body, **kw):
    return pl.pallas_call(body, **kw)


def _cparams(sem=None):
    if sem is None:
        return pltpu.CompilerParams(vmem_limit_bytes=VMEM_LIMIT)
    return pltpu.CompilerParams(dimension_semantics=sem, vmem_limit_bytes=VMEM_LIMIT)


def _sds(shape, dtype):
    return jax.ShapeDtypeStruct(shape, dtype)


def _dot(a, b):
    return jnp.dot(a, b, preferred_element_type=F32)


def _dot_nt(a, b):
    return lax.dot_general(a, b, (((1,), (1,)), ((), ())), preferred_element_type=F32)


def _dot_tn(a, b):
    return lax.dot_general(a, b, (((0,), (0,)), ((), ())), preferred_element_type=F32)


def _sigmoid(x):
    return 1.0 / (1.0 + jnp.exp(-x))


def _silu(x):
    return x * _sigmoid(x)


def _dsilu(x):
    s = _sigmoid(x)
    return s * (1.0 + x * (1.0 - s))


def _norm_mod(x, gain, sc, sh, name):
    tm = 512

    def body(x_ref, g_ref, sc_ref, sh_ref, h_ref):
        xv = x_ref[...]
        rs = lax.rsqrt(jnp.mean(xv * xv, axis=-1, keepdims=True) + RMS_EPS)
        h_ref[...] = ((xv * rs * g_ref[...]) * (1.0 + sc_ref[...]) + sh_ref[...]).astype(BF16)

    vec = pl.BlockSpec((1, D_MODEL), lambda i: (0, 0))
    return _pcall(
        body, name=name, grid=(SEQ // tm,),
        in_specs=[pl.BlockSpec((tm, D_MODEL), lambda i: (i, 0)), vec, vec, vec],
        out_specs=pl.BlockSpec((tm, D_MODEL), lambda i: (i, 0)),
        out_shape=_sds((SEQ, D_MODEL), BF16),
        compiler_params=_cparams(("parallel",)),
    )(x, gain, sc, sh)


def _norm_mod_bwd(x, gain, sc, sh, dh, dres, name):
    tm = 512

    def body(x_ref, g_ref, sc_ref, sh_ref, dh_ref, dres_ref, dx_ref, dsc_ref, dsh_ref, dg_ref):
        @pl.when(pl.program_id(0) == 0)
        def _():
            dsc_ref[...] = jnp.zeros_like(dsc_ref)
            dsh_ref[...] = jnp.zeros_like(dsh_ref)
            dg_ref[...] = jnp.zeros_like(dg_ref)

        xv = x_ref[...]
        dhv = dh_ref[...]
        rs = lax.rsqrt(jnp.mean(xv * xv, axis=-1, keepdims=True) + RMS_EPS)
        xh = xv * rs
        dsc_ref[...] += jnp.sum(dhv * (xh * g_ref[...]), axis=0, keepdims=True)
        dsh_ref[...] += jnp.sum(dhv, axis=0, keepdims=True)
        dhn = dhv * (1.0 + sc_ref[...])
        dg_ref[...] += jnp.sum(dhn * xh, axis=0, keepdims=True)
        dxh = dhn * g_ref[...]
        dx_ref[...] = dres_ref[...] + rs * (dxh - xh * jnp.mean(dxh * xh, axis=-1, keepdims=True))

    vec = pl.BlockSpec((1, D_MODEL), lambda i: (0, 0))
    big = pl.BlockSpec((tm, D_MODEL), lambda i: (i, 0))
    return _pcall(
        body, name=name, grid=(SEQ // tm,),
        in_specs=[big, vec, vec, vec, big, big],
        out_specs=[big, vec, vec, vec],
        out_shape=[_sds((SEQ, D_MODEL), F32)] + [_sds((1, D_MODEL), F32)] * 3,
        compiler_params=_cparams(("arbitrary",)),
    )(x, gain, sc, sh, dh, dres)


def _mm_cols(a, wg, layer, *, n_blocks, width, tn, act_map, w_map, out_dtype, name, tm=1024):
    k = a.shape[1]
    n_tiles = n_blocks * width // tn

    def body(a_ref, w_ref, o_ref):
        o_ref[...] = _dot(a_ref[...], w_ref[...]).astype(o_ref.dtype)

    return _pcall(
        body, name=name, grid=(SEQ // tm, n_tiles),
        in_specs=[pl.BlockSpec((tm, k), lambda i, t: (i, 0)),
                  pl.BlockSpec((None, None, k, tn), lambda i, t: (w_map(t)[0], layer, 0, w_map(t)[1]))],
        out_specs=pl.BlockSpec((None, tm, tn), lambda i, t: (act_map(t)[0], i, act_map(t)[1])),
        out_shape=_sds((n_blocks, SEQ, width), out_dtype),
        compiler_params=_cparams(("parallel", "arbitrary")),
    )(a, wg)


def _mm_cols_bwd_a(pairs, *, tn, act_map, w_map, n_tiles, name, tm=512):
    k = pairs[0][1].shape[2]
    n_p = len(pairs)

    def body(*refs):
        o_ref = refs[-1]

        @pl.when(pl.program_id(1) == 0)
        def _():
            o_ref[...] = jnp.zeros_like(o_ref)

        acc = _dot_nt(refs[0][...], refs[1][...])
        for p in range(1, n_p):
            acc += _dot_nt(refs[2 * p][...], refs[2 * p + 1][...])
        o_ref[...] += acc

    in_specs, args = [], []
    for dout, wg, layer in pairs:
        in_specs.append(pl.BlockSpec((None, tm, tn), lambda i, t: (act_map(t)[0], i, act_map(t)[1])))
        in_specs.append(pl.BlockSpec((None, None, k, tn),
                                     lambda i, t, layer=layer: (w_map(t)[0], layer, 0, w_map(t)[1])))
        args += [dout, wg]
    return _pcall(
        body, name=name, grid=(SEQ // tm, n_tiles),
        in_specs=in_specs,
        out_specs=pl.BlockSpec((tm, k), lambda i, t: (i, 0)),
        out_shape=_sds((SEQ, k), F32),
        compiler_params=_cparams(("parallel", "arbitrary")),
    )(*args)


def _mm_cols_bwd_w(a, dout, *, ns, tn, act_map, w_map, n_tiles, name, tm=1024):
    k = a.shape[1]

    def body(a_ref, d_ref, o_ref):
        @pl.when(pl.program_id(1) == 0)
        def _():
            o_ref[...] = jnp.zeros_like(o_ref)

        o_ref[...] += _dot_tn(a_ref[...], d_ref[...])

    return _pcall(
        body, name=name, grid=(n_tiles, SEQ // tm),
        in_specs=[pl.BlockSpec((tm, k), lambda t, i: (i, 0)),
                  pl.BlockSpec((None, tm, tn), lambda t, i: (act_map(t)[0], i, act_map(t)[1]))],
        out_specs=pl.BlockSpec((None, k, tn), lambda t, i: (w_map(t)[0], 0, w_map(t)[1])),
        out_shape=_sds((N_CHIP, k, ns), F32),
        compiler_params=_cparams(("parallel", "arbitrary")),
    )(a, dout)


def _mm_rows(a4, wg, layer, x, gate, name, tm=512):
    ks = a4.shape[2]
    n = wg.shape[3]

    def body(a_ref, w_ref, x_ref, g_ref, z_ref, xn_ref):
        s = pl.program_id(1)

        @pl.when(s == 0)
        def _():
            z_ref[...] = jnp.zeros_like(z_ref)

        z_ref[...] += _dot(a_ref[...], w_ref[...])

        @pl.when(s == N_CHIP - 1)
        def _():
            xn_ref[...] = x_ref[...] + g_ref[...] * z_ref[...]

    big = pl.BlockSpec((tm, n), lambda i, s: (i, 0))
    return _pcall(
        body, name=name, grid=(SEQ // tm, N_CHIP),
        in_specs=[pl.BlockSpec((None, tm, ks), lambda i, s: (s, i, 0)),
                  pl.BlockSpec((None, None, ks, n), lambda i, s: (s, layer, 0, 0)),
                  big, pl.BlockSpec((1, n), lambda i, s: (0, 0))],
        out_specs=[big, big],
        out_shape=[_sds((SEQ, n), F32), _sds((SEQ, n), F32)],
        compiler_params=_cparams(("parallel", "arbitrary")),
    )(a4, wg, x, gate)


def _gate_bwd(dx, z, gate, name):
    tm = 512

    def body(dx_ref, z_ref, g_ref, dz_ref, dg_ref):
        @pl.when(pl.program_id(0) == 0)
        def _():
            dg_ref[...] = jnp.zeros_like(dg_ref)

        dxv = dx_ref[...]
        dz_ref[...] = (dxv * g_ref[...]).astype(BF16)
        dg_ref[...] += jnp.sum(dxv * z_ref[...], axis=0, keepdims=True)

    big = pl.BlockSpec((tm, D_MODEL), lambda i: (i, 0))
    vec = pl.BlockSpec((1, D_MODEL), lambda i: (0, 0))
    return _pcall(
        body, name=name, grid=(SEQ // tm,),
        in_specs=[big, big, vec], out_specs=[big, vec],
        out_shape=[_sds((SEQ, D_MODEL), BF16), _sds((1, D_MODEL), F32)],
        compiler_params=_cparams(("arbitrary",)),
    )(dx, z, gate)


def _mm_rows_bwd_a(dz, wg, layer, name, tm=512):
    ks, n = wg.shape[2], wg.shape[3]

    def body(dz_ref, w_ref, o_ref):
        o_ref[...] = _dot_nt(dz_ref[...], w_ref[...])

    return _pcall(
        body, name=name, grid=(SEQ // tm, N_CHIP),
        in_specs=[pl.BlockSpec((tm, n), lambda i, s: (i, 0)),
                  pl.BlockSpec((None, None, ks, n), lambda i, s: (s, layer, 0, 0))],
        out_specs=pl.BlockSpec((None, tm, ks), lambda i, s: (s, i, 0)),
        out_shape=_sds((N_CHIP, SEQ, ks), F32),
        compiler_params=_cparams(("parallel", "arbitrary")),
    )(dz, wg)


def _mm_rows_bwd_w(a4, dz, name, tm=1024):
    ks = a4.shape[2]
    n = dz.shape[1]

    def body(a_ref, dz_ref, o_ref):
        @pl.when(pl.program_id(1) == 0)
        def _():
            o_ref[...] = jnp.zeros_like(o_ref)

        o_ref[...] += _dot_tn(a_ref[...], dz_ref[...])

    return _pcall(
        body, name=name, grid=(N_CHIP, SEQ // tm),
        in_specs=[pl.BlockSpec((None, tm, ks), lambda s, i: (s, i, 0)),
                  pl.BlockSpec((tm, n), lambda s, i: (i, 0))],
        out_specs=pl.BlockSpec((None, ks, n), lambda s, i: (s, 0, 0)),
        out_shape=_sds((N_CHIP, ks, n), F32),
        compiler_params=_cparams(("parallel", "arbitrary")),
    )(a4, dz)


def _ffn_up(h, w1g, w3g, layer, name, tm=512):
    def body(h_ref, w1_ref, w3_ref, a1_ref, a3_ref, u_ref):
        hv = h_ref[...]
        a1 = _dot(hv, w1_ref[...])
        a3 = _dot(hv, w3_ref[...])
        a1_ref[...] = a1
        a3_ref[...] = a3
        u_ref[...] = (_silu(a1) * a3).astype(BF16)

    wspec = pl.BlockSpec((None, None, D_MODEL, FFN_SHARD), lambda i, s: (s, layer, 0, 0))
    ospec = pl.BlockSpec((None, tm, FFN_SHARD), lambda i, s: (s, i, 0))
    shp = (N_CHIP, SEQ, FFN_SHARD)
    return _pcall(
        body, name=name, grid=(SEQ // tm, N_CHIP),
        in_specs=[pl.BlockSpec((tm, D_MODEL), lambda i, s: (i, 0)), wspec, wspec],
        out_specs=[ospec, ospec, ospec],
        out_shape=[_sds(shp, F32), _sds(shp, F32), _sds(shp, BF16)],
        compiler_params=_cparams(("parallel", "arbitrary")),
    )(h, w1g, w3g)


def _ffn_down_bwd(dz, w2g, layer, a1, a3, name, tm=512):
    def body(dz_ref, w_ref, a1_ref, a3_ref, da1_ref, da3_ref):
        du = _dot_nt(dz_ref[...], w_ref[...])
        a1 = a1_ref[...]
        da1_ref[...] = (du * a3_ref[...] * _dsilu(a1)).astype(BF16)
        da3_ref[...] = (du * _silu(a1)).astype(BF16)

    blk = pl.BlockSpec((None, tm, FFN_SHARD), lambda i, s: (s, i, 0))
    shp = (N_CHIP, SEQ, FFN_SHARD)
    return _pcall(
        body, name=name, grid=(SEQ // tm, N_CHIP),
        in_specs=[pl.BlockSpec((tm, D_MODEL), lambda i, s: (i, 0)),
                  pl.BlockSpec((None, None, FFN_SHARD, D_MODEL), lambda i, s: (s, layer, 0, 0)),
                  blk, blk],
        out_specs=[blk, blk],
        out_shape=[_sds(shp, BF16), _sds(shp, BF16)],
        compiler_params=_cparams(("parallel", "arbitrary")),
    )(dz, w2g, a1, a3)


def _loss_head(y, target, name):
    tm = 512

    def body(y_ref, t_ref, dy_ref, l_ref, acc_ref):
        @pl.when(pl.program_id(0) == 0)
        def _():
            acc_ref[...] = jnp.zeros_like(acc_ref)

        err = y_ref[...] - t_ref[...]
        dy_ref[...] = err * (1.0 / D_MODEL)
        acc_ref[...] += jnp.sum(jnp.mean(err * err, axis=-1, keepdims=True), axis=0, keepdims=True)

        @pl.when(pl.program_id(0) == pl.num_programs(0) - 1)
        def _():
            l_ref[...] = 0.5 * acc_ref[...]

    big = pl.BlockSpec((tm, D_MODEL), lambda i: (i, 0))
    return _pcall(
        body, name=name, grid=(SEQ // tm,),
        in_specs=[big, big],
        out_specs=[big, pl.BlockSpec((1, 1), lambda i: (0, 0))],
        out_shape=[_sds((SEQ, D_MODEL), F32), _sds((1, 1), F32)],
        scratch_shapes=[pltpu.VMEM((1, 1), F32)],
        compiler_params=_cparams(("arbitrary",)),
    )(y, target)


def _attn_rows(base, d):
    if d == 1:
        return pl.ds(pl.multiple_of(base, ATT_BLK), ATT_BLK)
    return pl.ds(base, ATT_BLK, stride=d)


def _attn_block_index(i, d):
    nb = SEQ // (ATT_BLK * d)
    r = i // nb
    n = i % nb
    base = r + n * (ATT_BLK * d)
    pbase = jnp.maximum(base - ATT_BLK * d, r)
    return n, _attn_rows(base, d), _attn_rows(pbase, d)


def _qk_normed(x, gain):
    rs = lax.rsqrt(jnp.mean(x * x, axis=-1, keepdims=True) + RMS_EPS)
    return x * rs, rs


def _attn_fwd(qkv9, qgain, kgain, bias, name):
    def body(q_ref, k_ref, v_ref, qg_ref, kg_ref, b_ref, o_ref, lse_ref, qn_s, kn_s, acc_s, m_s, l_s):
        g = pl.program_id(1)

        @pl.when(g == 0)
        def _():
            m_s[...] = jnp.full_like(m_s, NEG)
            l_s[...] = jnp.zeros_like(l_s)
            acc_s[...] = jnp.zeros_like(acc_s)

        qn_s[...] = _qk_normed(q_ref[...], None)[0] * qg_ref[...]
        kn_s[...] = _qk_normed(k_ref[...], None)[0] * kg_ref[...]

        for gi, (_, d) in enumerate(GROUPS):
            @pl.when(g == gi)
            def _(d=d):
                def it(i, carry):
                    n, rows, prow = _attn_block_index(i, d)
                    qb = qn_s[rows, :].astype(BF16)
                    kc = kn_s[rows, :].astype(BF16)
                    kp = kn_s[prow, :].astype(BF16)
                    vc = v_ref[rows, :].astype(BF16)
                    vp = v_ref[prow, :].astype(BF16)
                    sc = _dot_nt(qb, kc) * ATT_SCALE + b_ref[1]
                    sp = _dot_nt(qb, kp) * ATT_SCALE + jnp.where(n > 0, b_ref[0], NEG)
                    m_old = m_s[rows, :]
                    m_new = jnp.maximum(m_old, jnp.maximum(jnp.max(sc, axis=-1, keepdims=True),
                                                           jnp.max(sp, axis=-1, keepdims=True)))
                    alpha = jnp.exp(m_old - m_new)
                    pc = jnp.exp(sc - m_new)
                    pp = jnp.exp(sp - m_new)
                    l_s[rows, :] = alpha * l_s[rows, :] + jnp.sum(pc, axis=-1, keepdims=True) \
                        + jnp.sum(pp, axis=-1, keepdims=True)
                    acc_s[rows, :] = alpha * acc_s[rows, :] + _dot(pc.astype(BF16), vc) + _dot(pp.astype(BF16), vp)
                    m_s[rows, :] = m_new
                    return carry

                lax.fori_loop(0, SEQ // ATT_BLK, it, 0)

        @pl.when(g == len(GROUPS) - 1)
        def _():
            o_ref[...] = (acc_s[...] / l_s[...]).astype(BF16)
            lse_ref[...] = m_s[...] + jnp.log(l_s[...])

    def col(j):
        return pl.BlockSpec((None, SEQ, HEAD_DIM), lambda h, g: (g * 3 + j, 0, h))

    gspec = pl.BlockSpec((None, 1, HEAD_DIM), lambda h, g: (g, 0, 0))
    return _pcall(
        body, name=name, grid=(HEADS, len(GROUPS)),
        in_specs=[col(0), col(1), col(2), gspec, gspec,
                  pl.BlockSpec((None, None, 2, ATT_BLK, ATT_BLK), lambda h, g: (g, h, 0, 0, 0))],
        out_specs=[pl.BlockSpec((None, SEQ, HEAD_DIM), lambda h, g: (h // 2, 0, h % 2)),
                   pl.BlockSpec((None, SEQ, 1), lambda h, g: (h, 0, 0))],
        out_shape=[_sds((N_CHIP, SEQ, 2 * HEAD_DIM), BF16), _sds((HEADS, SEQ, 1), F32)],
        scratch_shapes=[pltpu.VMEM((SEQ, HEAD_DIM), F32)] * 3 + [pltpu.VMEM((SEQ, 1), F32)] * 2,
        compiler_params=_cparams(("parallel", "arbitrary")),
    )(qkv9, qkv9, qkv9, qgain, kgain, bias)


def _attn_bwd(qkv9, qgain, kgain, bias, do4, o4, lse, name):
    def body(q_ref, k_ref, v_ref, qg_ref, kg_ref, b_ref, do_ref, o_ref, lse_ref,
             dqkv_ref, dqg_ref, dkg_ref, db_ref, qn_s, kn_s, dq_s, dk_s, dv_s, dl_s):
        g = pl.program_id(1)
        qh, rq = _qk_normed(q_ref[...], None)
        kh, rk = _qk_normed(k_ref[...], None)
        qn_s[...] = qh * qg_ref[...]
        kn_s[...] = kh * kg_ref[...]
        dl_s[...] = jnp.sum(do_ref[...] * o_ref[...].astype(F32), axis=-1, keepdims=True)
        dk_s[...] = jnp.zeros_like(dk_s)
        dv_s[...] = jnp.zeros_like(dv_s)
        db_ref[...] = jnp.zeros_like(db_ref)

        for gi, (_, d) in enumerate(GROUPS):
            @pl.when(g == gi)
            def _(d=d):
                def it(i, carry):
                    n, rows, prow = _attn_block_index(i, d)
                    qb = qn_s[rows, :].astype(BF16)
                    kc = kn_s[rows, :].astype(BF16)
                    kp = kn_s[prow, :].astype(BF16)
                    vc = v_ref[rows, :].astype(BF16)
                    vp = v_ref[prow, :].astype(BF16)
                    dob = do_ref[rows, :].astype(BF16)
                    lse_b = lse_ref[rows, :]
                    dl = dl_s[rows, :]
                    sc = _dot_nt(qb, kc) * ATT_SCALE + b_ref[1]
                    sp = _dot_nt(qb, kp) * ATT_SCALE + jnp.where(n > 0, b_ref[0], NEG)
                    pc = jnp.exp(sc - lse_b)
                    pp = jnp.exp(sp - lse_b)
                    dsc = pc * (_dot_nt(dob, vc) - dl)
                    dsp = pp * (_dot_nt(dob, vp) - dl)
                    db_ref[1] += dsc
                    db_ref[0] += dsp
                    dsc16 = dsc.astype(BF16)
                    dsp16 = dsp.astype(BF16)
                    dq_s[rows, :] = (_dot(dsc16, kc) + _dot(dsp16, kp)) * ATT_SCALE
                    dk_s[rows, :] += _dot_tn(dsc16, qb) * ATT_SCALE
                    dk_s[prow, :] += _dot_tn(dsp16, qb) * ATT_SCALE
                    dv_s[rows, :] += _dot_tn(pc.astype(BF16), dob)
                    dv_s[prow, :] += _dot_tn(pp.astype(BF16), dob)
                    return carry

                lax.fori_loop(0, SEQ // ATT_BLK, it, 0)

        def norm_bwd(dn, xh, rs, gain):
            dgain = jnp.sum(dn * xh, axis=0, keepdims=True)
            dxh = dn * gain
            return rs * (dxh - xh * jnp.mean(dxh * xh, axis=-1, keepdims=True)), dgain

        dq, dqg = norm_bwd(dq_s[...], qh, rq, qg_ref[...])
        dk, dkg = norm_bwd(dk_s[...], kh, rk, kg_ref[...])
        dqkv_ref[0] = dq.astype(BF16)
        dqkv_ref[1] = dk.astype(BF16)
        dqkv_ref[2] = dv_s[...].astype(BF16)
        dqg_ref[...] = dqg
        dkg_ref[...] = dkg

    def col(j):
        return pl.BlockSpec((None, SEQ, HEAD_DIM), lambda h, g: (g * 3 + j, 0, h))

    gspec = pl.BlockSpec((None, 1, HEAD_DIM), lambda h, g: (g, 0, 0))
    bspec = pl.BlockSpec((None, None, 2, ATT_BLK, ATT_BLK), lambda h, g: (g, h, 0, 0, 0))
    hcol = pl.BlockSpec((None, SEQ, HEAD_DIM), lambda h, g: (h // 2, 0, h % 2))
    dgspec = pl.BlockSpec((None, None, 1, HEAD_DIM), lambda h, g: (h, g, 0, 0))
    ng = len(GROUPS)
    return _pcall(
        body, name=name, grid=(HEADS, ng),
        in_specs=[col(0), col(1), col(2), gspec, gspec, bspec, hcol, hcol,
                  pl.BlockSpec((None, SEQ, 1), lambda h, g: (h, 0, 0))],
        out_specs=[pl.BlockSpec((None, 3, SEQ, HEAD_DIM), lambda h, g: (g, 0, 0, h)), dgspec, dgspec, bspec],
        out_shape=[_sds((ng, 3, SEQ, D_MODEL), BF16), _sds((HEADS, ng, 1, HEAD_DIM), F32),
                   _sds((HEADS, ng, 1, HEAD_DIM), F32), _sds((ng, HEADS, 2, ATT_BLK, ATT_BLK), F32)],
        scratch_shapes=[pltpu.VMEM((SEQ, HEAD_DIM), F32)] * 5 + [pltpu.VMEM((SEQ, 1), F32)],
        compiler_params=_cparams(("parallel", "arbitrary")),
    )(qkv9, qkv9, qkv9, qgain, kgain, bias, do4, o4, lse)


def _relbias_bwd(dbias, bucket_idx, name):
    ng = len(GROUPS)

    def body(db_ref, idx_ref, o_ref):
        lane = lax.broadcasted_iota(jnp.int32, (HEADS, 128), 1)
        acc = jnp.zeros((HEADS, 128), F32)
        for g in range(ng):
            dbg = db_ref[g]
            idx = idx_ref[g]
            for b in range(NUM_BUCKETS):
                sel = jnp.where((idx == b)[None], dbg, 0.0)
                part = jnp.sum(jnp.sum(sel, axis=1), axis=1)
                val = jnp.sum(part, axis=-1, keepdims=True)
                acc = jnp.where(lane == g * NUM_BUCKETS + b, val, acc)
        o_ref[...] = acc

    return _pcall(body, name=name, out_shape=_sds((HEADS, 128), F32), compiler_params=_cparams())(dbias, bucket_idx)


def _scan16(x, reverse=False):
    row = lax.broadcasted_iota(jnp.int32, x.shape, 0)
    for sh in (1, 2, 4, 8):
        if reverse:
            x = x + jnp.where(row < HG_SUB - sh, pltpu.roll(x, HG_SUB - sh, 0), 0.0)
        else:
            x = x + jnp.where(row >= sh, pltpu.roll(x, sh, 0), 0.0)
    return x


def _hgrn_gates(qr, fr, lbv):
    q = _silu(qr)
    sig = _sigmoid(fr)
    fg = lbv + (1.0 - lbv) * sig
    lf = jnp.log(fg)
    gcum = _scan16(lf)
    glast = jnp.sum(lf, axis=0, keepdims=True)
    return q, sig, fg, 1.0 - fg, gcum, glast


def _hgrn_intra(q, k, gcum, tri):
    e = jnp.exp(jnp.where(tri, gcum[:, None, :] - gcum[None, :, :], NEG))
    a = jnp.sum(q[:, None, :] * k[None, :, :] * e, axis=-1, keepdims=True)
    return e, a


def _hgrn_fwd(proj4, lb, gain, name):
    nsub = HG_TC // HG_SUB

    def body(p_ref, lb_ref, gn_ref, o_ref, y_ref, st_ref, state_s):
        @pl.when(pl.program_id(1) == 0)
        def _():
            state_s[...] = jnp.zeros_like(state_s)

        lbv = lb_ref[...]
        gnv = gn_ref[...]
        shp = (HG_SUB, HG_SUB, HEAD_DIM)
        tri = lax.broadcasted_iota(jnp.int32, shp, 0) >= lax.broadcasted_iota(jnp.int32, shp, 1)

        def it(i, carry):
            rows = pl.ds(pl.multiple_of(i * HG_SUB, HG_SUB), HG_SUB)
            q, _, _, k, gcum, glast = _hgrn_gates(p_ref[0, rows, :], p_ref[1, rows, :], lbv)
            vv = p_ref[2, rows, :]
            st = state_s[...]
            st_ref[i] = st.astype(BF16)
            _, a = _hgrn_intra(q, k, gcum, tri)
            o = jnp.sum(a * vv[None, :, :], axis=1) + _dot_nt((q * jnp.exp(gcum)).astype(BF16), st.astype(BF16))
            kg = k * jnp.exp(glast - gcum)
            state_s[...] = st * jnp.exp(glast) + _dot_tn(vv.astype(BF16), kg.astype(BF16))
            o_ref[rows, :] = o
            rs = lax.rsqrt(jnp.mean(o * o, axis=-1, keepdims=True) + RMS_EPS)
            y_ref[rows, :] = (o * rs * gnv * _silu(p_ref[3, rows, :])).astype(BF16)
            return carry

        lax.fori_loop(0, nsub, it, 0)

    return _pcall(
        body, name=name, grid=(HEADS, SEQ // HG_TC),
        in_specs=[pl.BlockSpec((4, HG_TC, HEAD_DIM), lambda h, j: (0, j, h)),
                  pl.BlockSpec((1, HEAD_DIM), lambda h, j: (0, h)),
                  pl.BlockSpec((1, HEAD_DIM), lambda h, j: (0, 0))],
        out_specs=[pl.BlockSpec((HG_TC, HEAD_DIM), lambda h, j: (j, h)),
                   pl.BlockSpec((None, HG_TC, HEAD_DIM), lambda h, j: (h // 2, j, h % 2)),
                   pl.BlockSpec((None, nsub, HEAD_DIM, HEAD_DIM), lambda h, j: (h, j, 0, 0))],
        out_shape=[_sds((SEQ, D_MODEL), F32), _sds((N_CHIP, SEQ, 2 * HEAD_DIM), BF16),
                   _sds((HEADS, SEQ // HG_SUB, HEAD_DIM, HEAD_DIM), BF16)],
        scratch_shapes=[pltpu.VMEM((HEAD_DIM, HEAD_DIM), F32)],
        compiler_params=_cparams(("parallel", "arbitrary")),
    )(proj4, lb, gain)


def _hgrn_bwd(proj4, lb, gain, o_raw, dy4, states, name):
    nsub = HG_TC // HG_SUB
    nt = SEQ // HG_TC

    def body(p_ref, lb_ref, gn_ref, o_ref, dy_ref, st_ref, dp_ref, dlb_ref, dgn_ref, dst_s):
        @pl.when(pl.program_id(1) == 0)
        def _():
            dst_s[...] = jnp.zeros_like(dst_s)
            dlb_ref[...] = jnp.zeros_like(dlb_ref)
            dgn_ref[...] = jnp.zeros_like(dgn_ref)

        lbv = lb_ref[...]
        gnv = gn_ref[...]
        shp = (HG_SUB, HG_SUB, HEAD_DIM)
        tri = lax.broadcasted_iota(jnp.int32, shp, 0) >= lax.broadcasted_iota(jnp.int32, shp, 1)

        def it(ii, carry):
            i = nsub - 1 - ii
            rows = pl.ds(pl.multiple_of(i * HG_SUB, HG_SUB), HG_SUB)
            qr = p_ref[0, rows, :]
            gr = p_ref[3, rows, :]
            q, sig, fg, k, gcum, glast = _hgrn_gates(qr, p_ref[1, rows, :], lbv)
            vv = p_ref[2, rows, :]
            o = o_ref[rows, :]
            dy = dy_ref[rows, :]
            rs = lax.rsqrt(jnp.mean(o * o, axis=-1, keepdims=True) + RMS_EPS)
            oh = o * rs
            don = dy * _silu(gr)
            dgn_ref[...] += jnp.sum(don * oh, axis=0, keepdims=True)
            dgr = dy * oh * gnv * _dsilu(gr)
            doh = don * gnv
            do = rs * (doh - oh * jnp.mean(doh * oh, axis=-1, keepdims=True))
            st0 = st_ref[i]
            dst = dst_s[...]
            dst16 = dst.astype(BF16)
            do16 = do.astype(BF16)
            eg = jnp.exp(gcum)
            eb = jnp.exp(glast - gcum)
            e, a = _hgrn_intra(q, k, gcum, tri)
            da = jnp.sum(do[:, None, :] * vv[None, :, :], axis=-1, keepdims=True)
            dae = da * e
            dq = jnp.sum(dae * k[None, :, :], axis=1) + eg * _dot(do16, st0)
            dk_state = eb * _dot(vv.astype(BF16), dst16)
            dk = jnp.sum(dae * q[:, None, :], axis=0) + dk_state
            dv = jnp.sum(a * do[:, None, :], axis=0) + _dot_nt((k * eb).astype(BF16), dst16)
            eglast = jnp.exp(glast)
            dst_s[...] = dst * eglast + _dot_tn(do16, (q * eg).astype(BF16))
            dglast = jnp.sum(k * dk_state, axis=0, keepdims=True) \
                + eglast * jnp.sum(dst * st0.astype(F32), axis=0, keepdims=True)
            dlf = _scan16(q * dq - k * dk, reverse=True) + dglast
            dfg = dlf / fg - dk
            dlb_ref[...] += jnp.sum(dfg * (1.0 - sig), axis=0, keepdims=True)
            dp_ref[0, rows, :] = (dq * _dsilu(qr)).astype(BF16)
            dp_ref[1, rows, :] = (dfg * (1.0 - lbv) * sig * (1.0 - sig)).astype(BF16)
            dp_ref[2, rows, :] = dv.astype(BF16)
            dp_ref[3, rows, :] = dgr.astype(BF16)
            return carry

        lax.fori_loop(0, nsub, it, 0)

    vspec = pl.BlockSpec((None, 1, HEAD_DIM), lambda h, j: (h, 0, 0))
    return _pcall(
        body, name=name, grid=(HEADS, nt),
        in_specs=[pl.BlockSpec((4, HG_TC, HEAD_DIM), lambda h, j: (0, nt - 1 - j, h)),
                  pl.BlockSpec((1, HEAD_DIM), lambda h, j: (0, h)),
                  pl.BlockSpec((1, HEAD_DIM), lambda h, j: (0, 0)),
                  pl.BlockSpec((HG_TC, HEAD_DIM), lambda h, j: (nt - 1 - j, h)),
                  pl.BlockSpec((None, HG_TC, HEAD_DIM), lambda h, j: (h // 2, nt - 1 - j, h % 2)),
                  pl.BlockSpec((None, nsub, HEAD_DIM, HEAD_DIM), lambda h, j: (h, nt - 1 - j, 0, 0))],
        out_specs=[pl.BlockSpec((4, HG_TC, HEAD_DIM), lambda h, j: (0, nt - 1 - j, h)), vspec, vspec],
        out_shape=[_sds((4, SEQ, D_MODEL), BF16), _sds((HEADS, 1, HEAD_DIM), F32), _sds((HEADS, 1, HEAD_DIM), F32)],
        scratch_shapes=[pltpu.VMEM((HEAD_DIM, HEAD_DIM), F32)],
        compiler_params=_cparams(("parallel", "arbitrary")),
    )(proj4, lb, gain, o_raw, dy4, states)


def _t5_bucket(dist):
    n = np.asarray(dist, dtype=np.int64)
    max_exact = NUM_BUCKETS // 2
    large = max_exact + (np.log(np.maximum(n, 1) / max_exact) / np.log(MAX_DISTANCE / max_exact)
                         * (NUM_BUCKETS - max_exact)).astype(np.int64)
    large = np.minimum(large, NUM_BUCKETS - 1)
    return np.where(n < max_exact, n, large).astype(np.int32)


def _bias_tables():
    qi = np.arange(ATT_BLK)[:, None]
    ki = np.arange(ATT_BLK)[None, :]
    steps = (ATT_BLK + qi - ki, qi - ki)
    idx = np.zeros((len(GROUPS), 2, ATT_BLK, ATT_BLK), np.int32)
    for g, (_, d) in enumerate(GROUPS):
        for p, j in enumerate(steps):
            valid = (j >= 0) & (j <= ATT_BLK)
            idx[g, p] = np.where(valid, _t5_bucket(np.clip(j, 0, ATT_BLK) * d), -1)
    return idx


def _attn_bias(rel_bias):
    idx = _bias_tables()
    out = []
    for g in range(len(GROUPS)):
        tab = rel_bias[:, g * HEADS:(g + 1) * HEADS]
        b = jnp.transpose(tab[np.maximum(idx[g], 0)], (3, 0, 1, 2))
        out.append(jnp.where(jnp.asarray(idx[g] >= 0)[None], b, NEG))
    return jnp.stack(out)


ADA_SHARD = 6 * D_MODEL // N_CHIP
ADA_TN = 512


def _ada_fwd(c_all, ada_w, ada_b_cols, name):
    def body(c_ref, w_ref, b_ref, o_ref):
        ca = _silu(c_ref[...]).astype(BF16)
        o_ref[...] = _dot(ca, w_ref[...].astype(BF16)) + b_ref[...]

    return _pcall(
        body, name=name, grid=(DEPTH, ADA_SHARD // ADA_TN),
        in_specs=[pl.BlockSpec((N_DEV, D_MODEL), lambda l, j: (0, 0)),
                  pl.BlockSpec((None, D_MODEL, ADA_TN), lambda l, j: (l, 0, j)),
                  pl.BlockSpec((None, 1, ADA_TN), lambda l, j: (l, 0, j))],
        out_specs=pl.BlockSpec((None, N_DEV, ADA_TN), lambda l, j: (l, 0, j)),
        out_shape=_sds((DEPTH, N_DEV, ADA_SHARD), F32),
        compiler_params=_cparams(("parallel", "parallel")),
    )(c_all, ada_w, ada_b_cols)


def _ada_bwd(c_all, dmod_cols, name):
    def body(c_ref, d_ref, o_ref):
        ca = _silu(c_ref[...]).astype(BF16)
        o_ref[...] = _dot_tn(ca, d_ref[...].astype(BF16))

    return _pcall(
        body, name=name, grid=(DEPTH, ADA_SHARD // ADA_TN),
        in_specs=[pl.BlockSpec((N_DEV, D_MODEL), lambda l, j: (0, 0)),
                  pl.BlockSpec((None, N_DEV, ADA_TN), lambda l, j: (l, 0, j))],
        out_specs=pl.BlockSpec((None, D_MODEL, ADA_TN), lambda l, j: (l, 0, j)),
        out_shape=_sds((DEPTH, D_MODEL, ADA_SHARD), F32),
        compiler_params=_cparams(("parallel", "parallel")),
    )(c_all, dmod_cols)


def _lower_bounds(logits, name):
    def body(l_ref, o_ref):
        l0 = l_ref[0:1, :]
        l1 = l_ref[1:2, :]
        mx = jnp.maximum(l0, l1)
        e0 = jnp.exp(l0 - mx)
        e1 = jnp.exp(l1 - mx)
        p0 = e0 / (e0 + e1)
        p1 = e1 / (e0 + e1)
        o_ref[0:1, :] = p0 - p0
        o_ref[1:2, :] = (p0 + p1) - p0

    return _pcall(body, name=name, out_shape=_sds((DEPTH, D_MODEL), F32), compiler_params=_cparams())(logits)


_R_DMOD = 0
_R_NMIX = 96
_R_NFFN = 112
_R_QG = 128
_R_KG = 152
_R_GN = 176
_R_LB = 184
_R_RB = 192
SMALL_ROWS = 200


def _small_totals(gathered, logits8, name):
    ng = len(GROUPS)

    def body(g_ref, l_ref, main_ref, gains_ref, dlb_ref, rb_ref):
        tot = g_ref[0]
        for dev in range(1, N_DEV):
            tot = tot + g_ref[dev]
        main_ref[...] = tot[0:_R_QG]
        gains_ref[...] = jnp.zeros_like(gains_ref)
        for g in range(ng):
            gains_ref[g:g + 1, :] = jnp.sum(tot[_R_QG + 8 * g:_R_QG + 8 * g + 8], axis=0, keepdims=True)
            gains_ref[ng + g:ng + g + 1, :] = jnp.sum(tot[_R_KG + 8 * g:_R_KG + 8 * g + 8], axis=0, keepdims=True)
        gains_ref[2 * ng:2 * ng + 1, :] = jnp.sum(tot[_R_GN:_R_GN + 8], axis=0, keepdims=True)
        rb_ref[...] = tot[_R_RB:_R_RB + 8]
        dlb1 = tot[_R_LB:_R_LB + 8]
        l0 = l_ref[0]
        l1 = l_ref[1]
        mx = jnp.maximum(l0, l1)
        e0 = jnp.exp(l0 - mx)
        e1 = jnp.exp(l1 - mx)
        p0 = e0 / (e0 + e1)
        p1 = e1 / (e0 + e1)
        dlb_ref[0] = -p0 * p1 * dlb1
        dlb_ref[1] = p1 * (1.0 - p1) * dlb1

    return _pcall(
        body, name=name,
        out_shape=[_sds((_R_QG, 128), F32), _sds((8, 128), F32), _sds((DEPTH, 8, 128), F32), _sds((8, 128), F32)],
        compiler_params=_cparams(),
    )(gathered, logits8)


def _row_tile(rows):
    return 128 if rows % 128 == 0 else rows


def _adamw(w, grads, m, v, name):
    nl, r, cdim = w.shape
    tr = _row_tile(r)

    def body(*refs):
        g_refs = refs[:nl]
        w_ref, m_ref, v_ref, go_ref, d_ref, mo_ref, vo_ref = refs[nl:]

        def step(g):
            m2 = ADAM_B1 * m_ref[...] + (1.0 - ADAM_B1) * g
            v2 = ADAM_B2 * v_ref[...] + (1.0 - ADAM_B2) * (g * g)
            m_hat = m2 / (1.0 - ADAM_B1 ** ADAM_STEP)
            v_hat = v2 / (1.0 - ADAM_B2 ** ADAM_STEP)
            go_ref[...] = g
            d_ref[...] = -ADAM_LR * (m_hat / (jnp.sqrt(v_hat) + ADAM_EPS) + ADAM_WD * w_ref[...])
            mo_ref[...] = m2
            vo_ref[...] = v2

        if nl == 1:
            step(g_refs[0][...])
        else:
            for layer in range(nl):
                @pl.when(pl.program_id(0) == layer)
                def _(layer=layer):
                    step(g_refs[layer][...])

    big = pl.BlockSpec((None, tr, cdim), lambda l, i: (l, i, 0))
    g_specs = [pl.BlockSpec((tr, cdim), lambda l, i, layer=layer: (jnp.where(l == layer, i, 0), 0))
               for layer in range(nl)]
    shp = _sds((nl, r, cdim), F32)
    return _pcall(
        body, name=name, grid=(nl, r // tr),
        in_specs=g_specs + [big, big, big],
        out_specs=[big, big, big, big],
        out_shape=[shp, shp, shp, shp],
        compiler_params=_cparams(("parallel", "parallel")),
    )(*grads, w, m, v)


def _cast_bf16(w, name):
    nl, r, cdim = w.shape
    tr = _row_tile(r)

    def body(w_ref, o_ref):
        o_ref[...] = w_ref[...].astype(BF16)

    spec = pl.BlockSpec((None, tr, cdim), lambda l, i: (l, i, 0))
    return _pcall(body, name=name, grid=(nl, r // tr), in_specs=[spec], out_specs=spec,
                  out_shape=_sds((nl, r, cdim), BF16), compiler_params=_cparams(("parallel", "parallel")))(w)


def _rs_add_cast(core, grad, recv, name):
    _, k, n = grad.shape
    kh = k // 2
    tr = _row_tile(kh)
    nb = kh // tr

    def body(core_ref, g_ref, r_ref, o_ref):
        o_ref[...] = (g_ref[...] + r_ref[...]).astype(BF16)

    half = pl.BlockSpec((None, tr, n), lambda s, i, core_ref: (s, i, 0))
    return _pcall(
        body, name=name,
        grid_spec=pltpu.PrefetchScalarGridSpec(
            num_scalar_prefetch=1, grid=(N_CHIP, nb),
            in_specs=[pl.BlockSpec((None, tr, n), lambda s, i, core_ref: (s, core_ref[0] * nb + i, 0)), half],
            out_specs=half),
        out_shape=_sds((N_CHIP, kh, n), BF16),
        compiler_params=_cparams(("parallel", "parallel")),
    )(core, grad, recv)


def _rs_sum4(parts, name):
    _, kh, n = parts.shape
    tr = _row_tile(kh)

    def body(p_ref, o_ref):
        acc = p_ref[0].astype(F32)
        for s in range(1, N_CHIP):
            acc = acc + p_ref[s].astype(F32)
        o_ref[...] = acc

    return _pcall(body, name=name, grid=(kh // tr,),
                  in_specs=[pl.BlockSpec((N_CHIP, tr, n), lambda i: (0, i, 0))],
                  out_specs=pl.BlockSpec((tr, n), lambda i: (i, 0)),
                  out_shape=_sds((kh, n), F32), compiler_params=_cparams(("parallel",)))(parts)


_ANY = pl.BlockSpec(memory_space=pl.ANY)


def _position():
    return lax.axis_index("x"), lax.axis_index("y"), lax.axis_index("c")


def _other_chips(x, y):
    return [(1 - x, y), (x, 1 - y), (1 - x, 1 - y)]


def _remote(src, dst, send_sem, recv_sem, to):
    return pltpu.make_async_remote_copy(src_ref=src, dst_ref=dst, send_sem=send_sem, recv_sem=recv_sem,
                                        device_id=to, device_id_type=MESH)


def _small_allgather(v, name):
    r = v.shape[0]

    def body(x_ref, out_ref, send_sems, recv_sems, local_sem):
        x, y, c = _position()
        me, sibling = (x, y, c), (x, y, 1 - c)
        chips = _other_chips(x, y)

        def slab(px, py, pc):
            return out_ref.at[4 * px + 2 * py + pc]

        def copy(k, block, to, src=None):
            return _remote(slab(*block) if src is None else src, slab(*block), send_sems.at[k], recv_sems.at[k], to)

        mine = pltpu.make_async_copy(x_ref, slab(*me), local_sem)
        mine.start()
        first = [copy(0, me, sibling, src=x_ref)]
        first += [copy(1 + j, me, (*chip, c), src=x_ref) for j, chip in enumerate(chips)]
        for cp in first:
            cp.start()
        passed = [copy(4 + j, (*chip, c), sibling) for j, chip in enumerate(chips)]
        for j, chip in enumerate(chips):
            copy(1 + j, (*chip, c), me).wait_recv()
            passed[j].start()
        copy(0, sibling, me).wait_recv()
        for j, chip in enumerate(chips):
            copy(4 + j, (*chip, 1 - c), me).wait_recv()
        for cp in first + passed:
            cp.wait_send()
        mine.wait()

    return _pcall(
        body, name=name,
        out_shape=_sds((N_DEV, r, 128), F32),
        in_specs=[pl.BlockSpec(memory_space=pltpu.VMEM)],
        out_specs=pl.BlockSpec(memory_space=pltpu.VMEM),
        scratch_shapes=[pltpu.SemaphoreType.DMA((7,)), pltpu.SemaphoreType.DMA((7,)), pltpu.SemaphoreType.DMA],
        compiler_params=_cparams(),
    )(v)


def _half_rows(core, kh):
    return pl.ds(pl.multiple_of(core * kh, 8), kh)


def _gather_weights(shards, name):
    n = len(shards)

    def body(*refs):
        w = refs[:n]
        out = refs[n:2 * n]
        ici_send, ici_recv, d2d_send, d2d_recv, local_sems = refs[2 * n:]
        x, y, c = _position()
        me_chip = 2 * x + y
        sibling = (x, y, 1 - c)
        chips = _other_chips(x, y)

        def region(a, chip, core):
            kh = w[a].shape[1] // 2
            return out[a].at[chip, :, _half_rows(core, kh), :]

        local = [pltpu.make_async_copy(w[a], out[a].at[me_chip], local_sems.at[a]) for a in range(n)]
        for cp in local:
            cp.start()
        sends = []
        for a in range(n):
            kh = w[a].shape[1] // 2
            for j, (px, py) in enumerate(chips):
                cp = _remote(w[a].at[:, _half_rows(c, kh), :], region(a, me_chip, c),
                             ici_send.at[a, j], ici_recv.at[a, j], (px, py, c))
                cp.start()
                sends.append(cp)
        for j, (px, py) in enumerate(chips):
            for a in range(n):
                landed = region(a, 2 * px + py, c)
                _remote(landed, landed, ici_send.at[a, j], ici_recv.at[a, j], (px, py, c)).wait_recv()
                cp = _remote(landed, landed, d2d_send.at[a, j], d2d_recv.at[a, j], sibling)
                cp.start()
                sends.append(cp)
        for j, (px, py) in enumerate(chips):
            for a in range(n):
                other = region(a, 2 * px + py, 1 - c)
                _remote(other, other, d2d_send.at[a, j], d2d_recv.at[a, j], sibling).wait_recv()
        for cp in sends:
            cp.wait_send()
        for cp in local:
            cp.wait()

    sem = pltpu.SemaphoreType.DMA((n, 3))
    return _pcall(
        body, name=name,
        out_shape=[_sds((N_CHIP,) + s.shape, BF16) for s in shards],
        in_specs=[_ANY] * n, out_specs=[_ANY] * n,
        scratch_shapes=[sem, sem, sem, sem, pltpu.SemaphoreType.DMA((n,))],
        compiler_params=_cparams(),
    )(*shards)


def _rs_exchange_halves(grads, name):
    n = len(grads)

    def body(*refs):
        g = refs[:n]
        out = refs[n:2 * n]
        send_sems, recv_sems = refs[2 * n:]
        x, y, c = _position()
        copies = []
        for a in range(n):
            kh = g[a].shape[1] // 2
            cp = _remote(g[a].at[:, _half_rows(1 - c, kh), :], out[a], send_sems.at[a], recv_sems.at[a], (x, y, 1 - c))
            cp.start()
            copies.append(cp)
        for cp in copies:
            cp.wait()

    return _pcall(
        body, name=name,
        out_shape=[_sds((N_CHIP, g.shape[1] // 2, g.shape[2]), F32) for g in grads],
        in_specs=[_ANY] * n, out_specs=[_ANY] * n,
        scratch_shapes=[pltpu.SemaphoreType.DMA((n,)), pltpu.SemaphoreType.DMA((n,))],
        compiler_params=_cparams(),
    )(*grads)


def _rs_exchange_chips(parts, name):
    n = len(parts)

    def body(*refs):
        p = refs[:n]
        out = refs[n:2 * n]
        send_sems, recv_sems, local_sems = refs[2 * n:]
        x, y, c = _position()
        me_chip = 2 * x + y
        chips = _other_chips(x, y)
        local = [pltpu.make_async_copy(p[a].at[me_chip], out[a].at[me_chip], local_sems.at[a]) for a in range(n)]
        for cp in local:
            cp.start()
        sends = []
        for a in range(n):
            for j, (px, py) in enumerate(chips):
                cp = _remote(p[a].at[2 * px + py], out[a].at[me_chip], send_sems.at[a, j], recv_sems.at[a, j], (px, py, c))
                cp.start()
                sends.append(cp)
        for a in range(n):
            for j, (px, py) in enumerate(chips):
                got = out[a].at[2 * px + py]
                _remote(got, got, send_sems.at[a, j], recv_sems.at[a, j], (px, py, c)).wait_recv()
        for cp in sends:
            cp.wait_send()
        for cp in local:
            cp.wait()

    sem = pltpu.SemaphoreType.DMA((n, 3))
    return _pcall(
        body, name=name,
        out_shape=[_sds(p.shape, BF16) for p in parts],
        in_specs=[_ANY] * n, out_specs=[_ANY] * n,
        scratch_shapes=[sem, sem, pltpu.SemaphoreType.DMA((n,))],
        compiler_params=_cparams(),
    )(*parts)


def _rs_join_halves(halves, name):
    n = len(halves)

    def body(*refs):
        h = refs[:n]
        out = refs[n:2 * n]
        send_sems, recv_sems, local_sems = refs[2 * n:]
        x, y, c = _position()
        copies, local = [], []
        for a in range(n):
            kh = h[a].shape[0]
            mine = out[a].at[_half_rows(c, kh), :]
            lc = pltpu.make_async_copy(h[a], mine, local_sems.at[a])
            lc.start()
            local.append(lc)
            cp = _remote(h[a], mine, send_sems.at[a], recv_sems.at[a], (x, y, 1 - c))
            cp.start()
            copies.append(cp)
        for a in range(n):
            kh = h[a].shape[0]
            theirs = out[a].at[_half_rows(1 - c, kh), :]
            _remote(theirs, theirs, send_sems.at[a], recv_sems.at[a], (x, y, 1 - c)).wait_recv()
        for cp in copies:
            cp.wait_send()
        for lc in local:
            lc.wait()

    return _pcall(
        body, name=name,
        out_shape=[_sds((2 * h.shape[0], h.shape[1]), F32) for h in halves],
        in_specs=[_ANY] * n, out_specs=[_ANY] * n,
        scratch_shapes=[pltpu.SemaphoreType.DMA((n,)), pltpu.SemaphoreType.DMA((n,)), pltpu.SemaphoreType.DMA((n,))],
        compiler_params=_cparams(),
    )(*halves)


_SMALL_ORDER = ("rel_bias", "ada_b", "norm_mix", "norm_ffn", "attn_q_gain", "attn_k_gain", "hgrn_gnorm",
                "hgrn_lower_bounds")
_WEIGHT_ORDER = ("rel_bias", "ada_w", "ada_b", "norm_mix", "norm_ffn", "attn_w_qkv", "attn_w_out", "attn_q_gain",
                 "attn_k_gain", "hgrn_w_in", "hgrn_w_out", "hgrn_gnorm", "hgrn_lower_bounds", "ffn_w1", "ffn_w3",
                 "ffn_w2")


def _qkv_act_map(t):
    return t // 4, t % 4


def _qkv_w_map(t):
    return t // 9, t % 9


def _hin_map(t):
    return t // 2, t % 2


def _chip_map(t):
    return t, 0


def _pack_rows(parts):
    return jnp.concatenate([p.reshape(-1, 128) for p in parts], axis=0)


def kernel(x, c, rel_bias, ada_w, ada_b, norm_mix, norm_ffn, attn_w_qkv, attn_w_out, attn_q_gain, attn_k_gain, hgrn_w_in, hgrn_w_out, hgrn_gnorm, hgrn_lower_bounds, ffn_w1, ffn_w3, ffn_w2, loss_target, m_rel_bias, m_ada_w, m_ada_b, m_norm_mix, m_norm_ffn, m_attn_w_qkv, m_attn_w_out, m_attn_q_gain, m_attn_k_gain, m_hgrn_w_in, m_hgrn_w_out, m_hgrn_gnorm, m_hgrn_lower_bounds, m_ffn_w1, m_ffn_w3, m_ffn_w2, v_rel_bias, v_ada_w, v_ada_b, v_norm_mix, v_norm_ffn, v_attn_w_qkv, v_attn_w_out, v_attn_q_gain, v_attn_k_gain, v_hgrn_w_in, v_hgrn_w_out, v_hgrn_gnorm, v_hgrn_lower_bounds, v_ffn_w1, v_ffn_w3, v_ffn_w2):
    weights = dict(rel_bias=rel_bias, ada_w=ada_w, ada_b=ada_b, norm_mix=norm_mix, norm_ffn=norm_ffn,
                   attn_w_qkv=attn_w_qkv, attn_w_out=attn_w_out, attn_q_gain=attn_q_gain, attn_k_gain=attn_k_gain,
                   hgrn_w_in=hgrn_w_in, hgrn_w_out=hgrn_w_out, hgrn_gnorm=hgrn_gnorm,
                   hgrn_lower_bounds=hgrn_lower_bounds, ffn_w1=ffn_w1, ffn_w3=ffn_w3, ffn_w2=ffn_w2)
    mom1 = dict(rel_bias=m_rel_bias, ada_w=m_ada_w, ada_b=m_ada_b, norm_mix=m_norm_mix, norm_ffn=m_norm_ffn,
                attn_w_qkv=m_attn_w_qkv, attn_w_out=m_attn_w_out, attn_q_gain=m_attn_q_gain,
                attn_k_gain=m_attn_k_gain, hgrn_w_in=m_hgrn_w_in, hgrn_w_out=m_hgrn_w_out, hgrn_gnorm=m_hgrn_gnorm,
                hgrn_lower_bounds=m_hgrn_lower_bounds, ffn_w1=m_ffn_w1, ffn_w3=m_ffn_w3, ffn_w2=m_ffn_w2)
    mom2 = dict(rel_bias=v_rel_bias, ada_w=v_ada_w, ada_b=v_ada_b, norm_mix=v_norm_mix, norm_ffn=v_norm_ffn,
                attn_w_qkv=v_attn_w_qkv, attn_w_out=v_attn_w_out, attn_q_gain=v_attn_q_gain,
                attn_k_gain=v_attn_k_gain, hgrn_w_in=v_hgrn_w_in, hgrn_w_out=v_hgrn_w_out, hgrn_gnorm=v_hgrn_gnorm,
                hgrn_lower_bounds=v_hgrn_lower_bounds, ffn_w1=v_ffn_w1, ffn_w3=v_ffn_w3, ffn_w2=v_ffn_w2)

    xi, yi, ci = _position()
    chip = 2 * xi + yi
    dev = 4 * xi + 2 * yi + ci
    core = jnp.reshape(ci, (1,)).astype(jnp.int32)
    d = D_MODEL

    big_names = ("attn_w_qkv", "attn_w_out", "hgrn_w_in", "hgrn_w_out", "ffn_w1", "ffn_w3", "ffn_w2")
    shards16 = [_cast_bf16(weights[k], "cast_" + k) for k in big_names]
    wg = dict(zip(big_names, _gather_weights(shards16, "gather_weights")))

    c_all = _small_allgather(c.reshape(8, 128), "gather_c").reshape(N_DEV, d)
    ada_b_cols = lax.dynamic_slice(ada_b, (0, chip * ADA_SHARD), (DEPTH, ADA_SHARD)).reshape(DEPTH, 1, ADA_SHARD)
    mod_shard = _ada_fwd(c_all, ada_w, ada_b_cols, "ada_fwd")
    mod_all = _small_allgather(mod_shard.reshape(-1, 128), "gather_mod").reshape(N_DEV, DEPTH, N_DEV, ADA_SHARD)
    mod_mine = lax.dynamic_index_in_dim(mod_all[0::2], dev, axis=2, keepdims=False)
    mod = jnp.transpose(mod_mine, (1, 0, 2)).reshape(DEPTH, 6 * d)

    def mods(layer):
        return [mod[layer:layer + 1, j * d:(j + 1) * d] for j in range(6)]

    x0 = x.reshape(SEQ, d)
    target = loss_target.reshape(SEQ, d)
    qg = attn_q_gain.reshape(len(GROUPS), 1, HEAD_DIM)
    kg = attn_k_gain.reshape(len(GROUPS), 1, HEAD_DIM)
    bias = _attn_bias(rel_bias)
    lb1 = _lower_bounds(hgrn_lower_bounds, "lower_bounds")[1:2]

    def ffn_fwd(layer, x_in, sc2, sh2, g2):
        hf = _norm_mod(x_in, norm_ffn[layer:layer + 1], sc2, sh2, f"l{layer}_norm_ffn")
        a1, a3, u = _ffn_up(hf, wg["ffn_w1"], wg["ffn_w3"], layer, f"l{layer}_ffn_up")
        z, x_out = _mm_rows(u, wg["ffn_w2"], layer, x_in, g2, f"l{layer}_ffn_down")
        return x_out, (hf, a1, a3, u, z)

    def ffn_bwd(layer, dx_out, x_in, sc2, sh2, g2, saved):
        hf, a1, a3, u, z = saved
        dz, dg2 = _gate_bwd(dx_out, z, g2, f"l{layer}_ffn_gate_bwd")
        da1, da3 = _ffn_down_bwd(dz, wg["ffn_w2"], layer, a1, a3, f"l{layer}_ffn_down_bwd")
        dw2 = _mm_rows_bwd_w(u, dz, f"l{layer}_dw2")
        dh = _mm_cols_bwd_a([(da1, wg["ffn_w1"], layer), (da3, wg["ffn_w3"], layer)], tn=FFN_SHARD,
                            act_map=_chip_map, w_map=_chip_map, n_tiles=N_CHIP, name=f"l{layer}_ffn_up_bwd")
        dw1 = _mm_cols_bwd_w(hf, da1, ns=FFN_SHARD, tn=FFN_SHARD, act_map=_chip_map, w_map=_chip_map,
                             n_tiles=N_CHIP, name=f"l{layer}_dw1")
        dw3 = _mm_cols_bwd_w(hf, da3, ns=FFN_SHARD, tn=FFN_SHARD, act_map=_chip_map, w_map=_chip_map,
                             n_tiles=N_CHIP, name=f"l{layer}_dw3")
        dx_in, dsc2, dsh2, dnf = _norm_mod_bwd(x_in, norm_ffn[layer:layer + 1], sc2, sh2, dh, dx_out,
                                               f"l{layer}_norm_ffn_bwd")
        return dx_in, (dw1, dw3, dw2), (dsh2, dsc2, dg2), dnf

    sh1_0, sc1_0, g1_0, sh2_0, sc2_0, g2_0 = mods(0)
    h0 = _norm_mod(x0, norm_mix[0:1], sc1_0, sh1_0, "l0_norm_mix")
    qkv9 = _mm_cols(h0, wg["attn_w_qkv"], 0, n_blocks=9, width=d, tn=256, act_map=_qkv_act_map, w_map=_qkv_w_map,
                    out_dtype=F32, name="l0_qkv", tm=2048)
    o4, lse = _attn_fwd(qkv9, qg, kg, bias, "l0_attn")
    y0, x1 = _mm_rows(o4, wg["attn_w_out"], 0, x0, g1_0, "l0_attn_out")
    x2, ffn0 = ffn_fwd(0, x1, sc2_0, sh2_0, g2_0)

    sh1_1, sc1_1, g1_1, sh2_1, sc2_1, g2_1 = mods(1)
    h1 = _norm_mod(x2, norm_mix[1:2], sc1_1, sh1_1, "l1_norm_mix")
    proj4 = _mm_cols(h1, wg["hgrn_w_in"], 0, n_blocks=4, width=d, tn=512, act_map=_hin_map, w_map=_hin_map,
                     out_dtype=F32, name="l1_hgrn_in")
    o_raw, yg4, states = _hgrn_fwd(proj4, lb1, hgrn_gnorm, "l1_hgrn")
    y1, x3 = _mm_rows(yg4, wg["hgrn_w_out"], 0, x2, g1_1, "l1_hgrn_out")
    x4, ffn1 = ffn_fwd(1, x3, sc2_1, sh2_1, g2_1)

    dx4, loss_part = _loss_head(x4, target, "loss_head")
    loss = lax.psum(loss_part[0, 0], ("x", "y", "c"))

    dx3, (dw1_1, dw3_1, dw2_1), dmod2_1, dnf_1 = ffn_bwd(1, dx4, x3, sc2_1, sh2_1, g2_1, ffn1)
    dzm1, dg1_1 = _gate_bwd(dx3, y1, g1_1, "l1_mix_gate_bwd")
    dyg4 = _mm_rows_bwd_a(dzm1, wg["hgrn_w_out"], 0, "l1_hgrn_out_bwd")
    dw_hout = _mm_rows_bwd_w(yg4, dzm1, "l1_dw_hgrn_out")
    dproj4, dlb_h, dgn_h = _hgrn_bwd(proj4, lb1, hgrn_gnorm, o_raw, dyg4, states, "l1_hgrn_bwd")
    dh1 = _mm_cols_bwd_a([(dproj4, wg["hgrn_w_in"], 0)], tn=512, act_map=_hin_map, w_map=_hin_map, n_tiles=8,
                         name="l1_hgrn_in_bwd")
    dw_hin = _mm_cols_bwd_w(h1, dproj4, ns=d, tn=512, act_map=_hin_map, w_map=_hin_map, n_tiles=8,
                            name="l1_dw_hgrn_in")
    dx2, dsc1_1, dsh1_1, dnm_1 = _norm_mod_bwd(x2, norm_mix[1:2], sc1_1, sh1_1, dh1, dx3, "l1_norm_mix_bwd")

    dx1, (dw1_0, dw3_0, dw2_0), dmod2_0, dnf_0 = ffn_bwd(0, dx2, x1, sc2_0, sh2_0, g2_0, ffn0)
    dzm0, dg1_0 = _gate_bwd(dx1, y0, g1_0, "l0_mix_gate_bwd")
    do4 = _mm_rows_bwd_a(dzm0, wg["attn_w_out"], 0, "l0_attn_out_bwd")
    dw_aout = _mm_rows_bwd_w(o4, dzm0, "l0_dw_attn_out")
    dqkv, dqg_h, dkg_h, dbias = _attn_bwd(qkv9, qg, kg, bias, do4, o4, lse, "l0_attn_bwd")
    dqkv9 = dqkv.reshape(9, SEQ, d)
    dh0 = _mm_cols_bwd_a([(dqkv9, wg["attn_w_qkv"], 0)], tn=256, act_map=_qkv_act_map, w_map=_qkv_w_map,
                         n_tiles=36, name="l0_qkv_bwd")
    dw_qkv = _mm_cols_bwd_w(h0, dqkv9, ns=2304, tn=256, act_map=_qkv_act_map, w_map=_qkv_w_map, n_tiles=36,
                            name="l0_dw_qkv")
    dx0, dsc1_0, dsh1_0, dnm_0 = _norm_mod_bwd(x0, norm_mix[0:1], sc1_0, sh1_0, dh0, dx1, "l0_norm_mix_bwd")
    drb8 = _relbias_bwd(dbias, jnp.asarray(_bias_tables()), "rel_bias_bwd")

    small = _pack_rows([
        dsh1_0, dsc1_0, dg1_0, *dmod2_0, dsh1_1, dsc1_1, dg1_1, *dmod2_1,
        dnm_0, dnm_1, dnf_0, dnf_1,
        jnp.transpose(dqg_h, (1, 0, 2, 3)), jnp.transpose(dkg_h, (1, 0, 2, 3)), dgn_h, dlb_h, drb8])
    small_all = _small_allgather(small, "gather_small")
    main, gains, dlbnd, rbt = _small_totals(small_all, hgrn_lower_bounds.reshape(DEPTH, 8, 128), "small_totals")
    ng = len(GROUPS)
    grads = {
        "ada_b": main[_R_DMOD:_R_NMIX].reshape(DEPTH, 6 * d),
        "norm_mix": main[_R_NMIX:_R_NFFN].reshape(DEPTH, d),
        "norm_ffn": main[_R_NFFN:_R_QG].reshape(DEPTH, d),
        "attn_q_gain": gains[0:ng].reshape(1, ng, HEAD_DIM),
        "attn_k_gain": gains[ng:2 * ng].reshape(1, ng, HEAD_DIM),
        "hgrn_gnorm": gains[2 * ng:2 * ng + 1],
        "hgrn_lower_bounds": dlbnd.reshape(DEPTH, d),
        "rel_bias": jnp.transpose(rbt[:, :ng * NUM_BUCKETS].reshape(HEADS, ng, NUM_BUCKETS), (2, 1, 0))
                       .reshape(NUM_BUCKETS, ng * HEADS),
    }
    dmod_all = small_all[:, _R_DMOD:_R_NMIX].reshape(N_DEV, DEPTH, 6 * d)
    dmod_cols = jnp.transpose(lax.dynamic_slice(dmod_all, (0, 0, chip * ADA_SHARD), (N_DEV, DEPTH, ADA_SHARD)),
                              (1, 0, 2))
    grad_ada_w = _ada_bwd(c_all, dmod_cols, "ada_bwd")

    layer_grads = {"attn_w_qkv": [dw_qkv], "attn_w_out": [dw_aout], "hgrn_w_in": [dw_hin], "hgrn_w_out": [dw_hout],
                   "ffn_w1": [dw1_0, dw1_1], "ffn_w3": [dw3_0, dw3_1], "ffn_w2": [dw2_0, dw2_1]}
    tags = [(k, layer) for k in big_names for layer in range(len(layer_grads[k]))]
    flat = [layer_grads[k][layer] for k, layer in tags]
    recv = _rs_exchange_halves(flat, "rs_exchange_halves")
    parts = [_rs_add_cast(core, g, r, f"rs_add_{k}_{layer}") for (k, layer), g, r in zip(tags, flat, recv)]
    got = _rs_exchange_chips(parts, "rs_exchange_chips")
    halves = [_rs_sum4(p, f"rs_sum_{k}_{layer}") for (k, layer), p in zip(tags, got)]
    full = dict(zip(tags, _rs_join_halves(halves, "rs_join_halves")))

    out_g, out_d, out_m, out_v = {}, {}, {}, {}
    for k in big_names:
        gs = [full[(k, layer)] for layer in range(len(layer_grads[k]))]
        out_g[k], out_d[k], out_m[k], out_v[k] = _adamw(weights[k], gs, mom1[k], mom2[k], "adamw_" + k)
    shp = (1, DEPTH * d, ADA_SHARD)
    res = _adamw(ada_w.reshape(shp), [grad_ada_w.reshape(shp[1:])], m_ada_w.reshape(shp), v_ada_w.reshape(shp),
                 "adamw_ada_w")
    out_g["ada_w"], out_d["ada_w"], out_m["ada_w"], out_v["ada_w"] = [r.reshape(ada_w.shape) for r in res]
    packed = [_pack_rows([src[k] for k in _SMALL_ORDER])[None] for src in (weights, grads, mom1, mom2)]
    res = _adamw(packed[0], [packed[1][0]], packed[2], packed[3], "adamw_small")
    offset = 0
    for k in _SMALL_ORDER:
        size = weights[k].size
        for dst, r in zip((out_g, out_d, out_m, out_v), res):
            dst[k] = r.reshape(-1)[offset:offset + size].reshape(weights[k].shape)
        offset += size

    return (loss, dx0.reshape(x.shape), *[out_g[k] for k in _WEIGHT_ORDER], *[out_d[k] for k in _WEIGHT_ORDER],
            *[out_m[k] for k in _WEIGHT_ORDER], *[out_v[k] for k in _WEIGHT_ORDER])
```

```python
import functools

import numpy as np
import jax
import jax.numpy as jnp
from jax import lax
from jax.experimental import pallas as pl
from jax.experimental.pallas import tpu as pltpu

F32 = jnp.float32
BF16 = jnp.bfloat16

D_MODEL = 1024
SEQ = 4096
N_DEV = 8
N_CHIP = 4
DEPTH = 2
HEADS = 8
HEAD_DIM = 128
GROUPS = ((128, 1), (512, 4), (2048, 16))
ATT_BLK = 128
NUM_BUCKETS = 32
MAX_DISTANCE = 2048
FFN_HIDDEN = 2816
FFN_SHARD = FFN_HIDDEN // N_CHIP
HG_SUB = 16
HG_TC = 512
HG_HP = 2
RMS_EPS = 1e-6
NEG = -1e30
ATT_SCALE = HEAD_DIM ** -0.5
ADAM_LR, ADAM_B1, ADAM_B2, ADAM_EPS, ADAM_WD, ADAM_STEP = 0.001, 0.9, 0.999, 1e-08, 0.01, 10
VMEM_LIMIT = 56 * 1024 * 1024
MESH = pl.DeviceIdType.MESH


def _pcall(body, **kw):
    return pl.pallas_call(body, **kw)


def _cparams(sem=None):
    if sem is None:
        return pltpu.CompilerParams(vmem_limit_bytes=VMEM_LIMIT)
    return pltpu.CompilerParams(dimension_semantics=sem, vmem_limit_bytes=VMEM_LIMIT)


def _sds(shape, dtype):
    return jax.ShapeDtypeStruct(shape, dtype)


def _dot(a, b):
    return jnp.dot(a, b, preferred_element_type=F32)


def _dot_nt(a, b):
    return lax.dot_general(a, b, (((1,), (1,)), ((), ())), preferred_element_type=F32)


def _dot_tn(a, b):
    return lax.dot_general(a, b, (((0,), (0,)), ((), ())), preferred_element_type=F32)


def _sigmoid(x):
    return 1.0 / (1.0 + jnp.exp(-x))


def _silu(x):
    return x * _sigmoid(x)


def _dsilu(x):
    s = _sigmoid(x)
    return s * (1.0 + x * (1.0 - s))


def _norm_mod(x, gain, sc, sh, name):
    tm = 512

    def body(x_ref, g_ref, sc_ref, sh_ref, h_ref):
        xv = x_ref[...]
        rs = lax.rsqrt(jnp.mean(xv * xv, axis=-1, keepdims=True) + RMS_EPS)
        h_ref[...] = ((xv * rs * g_ref[...]) * (1.0 + sc_ref[...]) + sh_ref[...]).astype(BF16)

    vec = pl.BlockSpec((1, D_MODEL), lambda i: (0, 0))
    return _pcall(
        body, name=name, grid=(SEQ // tm,),
        in_specs=[pl.BlockSpec((tm, D_MODEL), lambda i: (i, 0)), vec, vec, vec],
        out_specs=pl.BlockSpec((tm, D_MODEL), lambda i: (i, 0)),
        out_shape=_sds((SEQ, D_MODEL), BF16),
        compiler_params=_cparams(("parallel",)),
    )(x, gain, sc, sh)


def _norm_mod_bwd(x, gain, sc, sh, dh, dres, name):
    tm = 512

    def body(x_ref, g_ref, sc_ref, sh_ref, dh_ref, dres_ref, dx_ref, dsc_ref, dsh_ref, dg_ref):
        @pl.when(pl.program_id(0) == 0)
        def _():
            dsc_ref[...] = jnp.zeros_like(dsc_ref)
            dsh_ref[...] = jnp.zeros_like(dsh_ref)
            dg_ref[...] = jnp.zeros_like(dg_ref)

        xv = x_ref[...]
        dhv = dh_ref[...]
        rs = lax.rsqrt(jnp.mean(xv * xv, axis=-1, keepdims=True) + RMS_EPS)
        xh = xv * rs
        dsc_ref[...] += jnp.sum(dhv * (xh * g_ref[...]), axis=0, keepdims=True)
        dsh_ref[...] += jnp.sum(dhv, axis=0, keepdims=True)
        dhn = dhv * (1.0 + sc_ref[...])
        dg_ref[...] += jnp.sum(dhn * xh, axis=0, keepdims=True)
        dxh = dhn * g_ref[...]
        dx_ref[...] = dres_ref[...] + rs * (dxh - xh * jnp.mean(dxh * xh, axis=-1, keepdims=True))

    vec = pl.BlockSpec((1, D_MODEL), lambda i: (0, 0))
    big = pl.BlockSpec((tm, D_MODEL), lambda i: (i, 0))
    return _pcall(
        body, name=name, grid=(SEQ // tm,),
        in_specs=[big, vec, vec, vec, big, big],
        out_specs=[big, vec, vec, vec],
        out_shape=[_sds((SEQ, D_MODEL), F32)] + [_sds((1, D_MODEL), F32)] * 3,
        compiler_params=_cparams(("arbitrary",)),
    )(x, gain, sc, sh, dh, dres)


def _mm_cols(a, wg, layer, *, n_blocks, width, tn, act_map, w_map, out_dtype, name, tm=1024):
    k = a.shape[1]
    n_tiles = n_blocks * width // tn

    def body(a_ref, w_ref, o_ref):
        o_ref[...] = _dot(a_ref[...], w_ref[...]).astype(o_ref.dtype)

    return _pcall(
        body, name=name, grid=(SEQ // tm, n_tiles),
        in_specs=[pl.BlockSpec((tm, k), lambda i, t: (i, 0)),
                  pl.BlockSpec((None, None, k, tn), lambda i, t: (w_map(t)[0], layer, 0, w_map(t)[1]))],
        out_specs=pl.BlockSpec((None, tm, tn), lambda i, t: (act_map(t)[0], i, act_map(t)[1])),
        out_shape=_sds((n_blocks, SEQ, width), out_dtype),
        compiler_params=_cparams(("parallel", "arbitrary")),
    )(a, wg)


def _mm_cols_bwd_a(pairs, *, tn, act_map, w_map, n_tiles, name, tm=512):
    k = pairs[0][1].shape[2]
    n_p = len(pairs)

    def body(*refs):
        o_ref = refs[-1]

        @pl.when(pl.program_id(1) == 0)
        def _():
            o_ref[...] = jnp.zeros_like(o_ref)

        acc = _dot_nt(refs[0][...], refs[1][...])
        for p in range(1, n_p):
            acc += _dot_nt(refs[2 * p][...], refs[2 * p + 1][...])
        o_ref[...] += acc

    in_specs, args = [], []
    for dout, wg, layer in pairs:
        in_specs.append(pl.BlockSpec((None, tm, tn), lambda i, t: (act_map(t)[0], i, act_map(t)[1])))
        in_specs.append(pl.BlockSpec((None, None, k, tn),
                                     lambda i, t, layer=layer: (w_map(t)[0], layer, 0, w_map(t)[1])))
        args += [dout, wg]
    return _pcall(
        body, name=name, grid=(SEQ // tm, n_tiles),
        in_specs=in_specs,
        out_specs=pl.BlockSpec((tm, k), lambda i, t: (i, 0)),
        out_shape=_sds((SEQ, k), F32),
        compiler_params=_cparams(("parallel", "arbitrary")),
    )(*args)


def _mm_cols_bwd_w(a, dout, *, ns, tn, act_map, w_map, n_tiles, name, tm=1024):
    k = a.shape[1]

    def body(a_ref, d_ref, o_ref):
        @pl.when(pl.program_id(1) == 0)
        def _():
            o_ref[...] = jnp.zeros_like(o_ref)

        o_ref[...] += _dot_tn(a_ref[...], d_ref[...])

    return _pcall(
        body, name=name, grid=(n_tiles, SEQ // tm),
        in_specs=[pl.BlockSpec((tm, k), lambda t, i: (i, 0)),
                  pl.BlockSpec((None, tm, tn), lambda t, i: (act_map(t)[0], i, act_map(t)[1]))],
        out_specs=pl.BlockSpec((None, k, tn), lambda t, i: (w_map(t)[0], 0, w_map(t)[1])),
        out_shape=_sds((N_CHIP, k, ns), F32),
        compiler_params=_cparams(("parallel", "arbitrary")),
    )(a, dout)


def _mm_rows(a4, wg, layer, x, gate, name, tm=512):
    ks = a4.shape[2]
    n = wg.shape[3]

    def body(a_ref, w_ref, x_ref, g_ref, z_ref, xn_ref):
        s = pl.program_id(1)

        @pl.when(s == 0)
        def _():
            z_ref[...] = jnp.zeros_like(z_ref)

        z_ref[...] += _dot(a_ref[...], w_ref[...])

        @pl.when(s == N_CHIP - 1)
        def _():
            xn_ref[...] = x_ref[...] + g_ref[...] * z_ref[...]

    big = pl.BlockSpec((tm, n), lambda i, s: (i, 0))
    return _pcall(
        body, name=name, grid=(SEQ // tm, N_CHIP),
        in_specs=[pl.BlockSpec((None, tm, ks), lambda i, s: (s, i, 0)),
                  pl.BlockSpec((None, None, ks, n), lambda i, s: (s, layer, 0, 0)),
                  big, pl.BlockSpec((1, n), lambda i, s: (0, 0))],
        out_specs=[big, big],
        out_shape=[_sds((SEQ, n), F32), _sds((SEQ, n), F32)],
        compiler_params=_cparams(("parallel", "arbitrary")),
    )(a4, wg, x, gate)


def _gate_bwd(dx, z, gate, name):
    tm = 512

    def body(dx_ref, z_ref, g_ref, dz_ref, dg_ref):
        @pl.when(pl.program_id(0) == 0)
        def _():
            dg_ref[...] = jnp.zeros_like(dg_ref)

        dxv = dx_ref[...]
        dz_ref[...] = (dxv * g_ref[...]).astype(BF16)
        dg_ref[...] += jnp.sum(dxv * z_ref[...], axis=0, keepdims=True)

    big = pl.BlockSpec((tm, D_MODEL), lambda i: (i, 0))
    vec = pl.BlockSpec((1, D_MODEL), lambda i: (0, 0))
    return _pcall(
        body, name=name, grid=(SEQ // tm,),
        in_specs=[big, big, vec], out_specs=[big, vec],
        out_shape=[_sds((SEQ, D_MODEL), BF16), _sds((1, D_MODEL), F32)],
        compiler_params=_cparams(("arbitrary",)),
    )(dx, z, gate)


def _mm_rows_bwd_a(dz, wg, layer, name, tm=512):
    ks, n = wg.shape[2], wg.shape[3]

    def body(dz_ref, w_ref, o_ref):
        o_ref[...] = _dot_nt(dz_ref[...], w_ref[...])

    return _pcall(
        body, name=name, grid=(SEQ // tm, N_CHIP),
        in_specs=[pl.BlockSpec((tm, n), lambda i, s: (i, 0)),
                  pl.BlockSpec((None, None, ks, n), lambda i, s: (s, layer, 0, 0))],
        out_specs=pl.BlockSpec((None, tm, ks), lambda i, s: (s, i, 0)),
        out_shape=_sds((N_CHIP, SEQ, ks), F32),
        compiler_params=_cparams(("parallel", "arbitrary")),
    )(dz, wg)


def _mm_rows_bwd_w(a4, dz, name, tm=1024):
    ks = a4.shape[2]
    n = dz.shape[1]

    def body(a_ref, dz_ref, o_ref):
        @pl.when(pl.program_id(1) == 0)
        def _():
            o_ref[...] = jnp.zeros_like(o_ref)

        o_ref[...] += _dot_tn(a_ref[...], dz_ref[...])

    return _pcall(
        body, name=name, grid=(N_CHIP, SEQ // tm),
        in_specs=[pl.BlockSpec((None, tm, ks), lambda s, i: (s, i, 0)),
                  pl.BlockSpec((tm, n), lambda s, i: (i, 0))],
        out_specs=pl.BlockSpec((None, ks, n), lambda s, i: (s, 0, 0)),
        out_shape=_sds((N_CHIP, ks, n), F32),
        compiler_params=_cparams(("parallel", "arbitrary")),
    )(a4, dz)


def _ffn_up(h, w1g, w3g, layer, name, tm=512):
    def body(h_ref, w1_ref, w3_ref, a1_ref, a3_ref, u_ref):
        hv = h_ref[...]
        a1 = _dot(hv, w1_ref[...])
        a3 = _dot(hv, w3_ref[...])
        a1_ref[...] = a1
        a3_ref[...] = a3
        u_ref[...] = (_silu(a1) * a3).astype(BF16)

    wspec = pl.BlockSpec((None, None, D_MODEL, FFN_SHARD), lambda i, s: (s, layer, 0, 0))
    ospec = pl.BlockSpec((None, tm, FFN_SHARD), lambda i, s: (s, i, 0))
    shp = (N_CHIP, SEQ, FFN_SHARD)
    return _pcall(
        body, name=name, grid=(SEQ // tm, N_CHIP),
        in_specs=[pl.BlockSpec((tm, D_MODEL), lambda i, s: (i, 0)), wspec, wspec],
        out_specs=[ospec, ospec, ospec],
        out_shape=[_sds(shp, F32), _sds(shp, F32), _sds(shp, BF16)],
        compiler_params=_cparams(("parallel", "arbitrary")),
    )(h, w1g, w3g)


def _ffn_down_bwd(dz, w2g, layer, a1, a3, name, tm=512):
    def body(dz_ref, w_ref, a1_ref, a3_ref, da1_ref, da3_ref):
        du = _dot_nt(dz_ref[...], w_ref[...])
        a1 = a1_ref[...]
        da1_ref[...] = (du * a3_ref[...] * _dsilu(a1)).astype(BF16)
        da3_ref[...] = (du * _silu(a1)).astype(BF16)

    blk = pl.BlockSpec((None, tm, FFN_SHARD), lambda i, s: (s, i, 0))
    shp = (N_CHIP, SEQ, FFN_SHARD)
    return _pcall(
        body, name=name, grid=(SEQ // tm, N_CHIP),
        in_specs=[pl.BlockSpec((tm, D_MODEL), lambda i, s: (i, 0)),
                  pl.BlockSpec((None, None, FFN_SHARD, D_MODEL), lambda i, s: (s, layer, 0, 0)),
                  blk, blk],
        out_specs=[blk, blk],
        out_shape=[_sds(shp, BF16), _sds(shp, BF16)],
        compiler_params=_cparams(("parallel", "arbitrary")),
    )(dz, w2g, a1, a3)


def _loss_head(y, target, name):
    tm = 512

    def body(y_ref, t_ref, dy_ref, l_ref, acc_ref):
        @pl.when(pl.program_id(0) == 0)
        def _():
            acc_ref[...] = jnp.zeros_like(acc_ref)

        err = y_ref[...] - t_ref[...]
        dy_ref[...] = err * (1.0 / D_MODEL)
        acc_ref[...] += jnp.sum(jnp.mean(err * err, axis=-1, keepdims=True), axis=0, keepdims=True)

        @pl.when(pl.program_id(0) == pl.num_programs(0) - 1)
        def _():
            l_ref[...] = 0.5 * acc_ref[...]

    big = pl.BlockSpec((tm, D_MODEL), lambda i: (i, 0))
    return _pcall(
        body, name=name, grid=(SEQ // tm,),
        in_specs=[big, big],
        out_specs=[big, pl.BlockSpec((1, 1), lambda i: (0, 0))],
        out_shape=[_sds((SEQ, D_MODEL), F32), _sds((1, 1), F32)],
        scratch_shapes=[pltpu.VMEM((1, 1), F32)],
        compiler_params=_cparams(("arbitrary",)),
    )(y, target)


def _attn_rows(base, d):
    if d == 1:
        return pl.ds(pl.multiple_of(base, ATT_BLK), ATT_BLK)
    return pl.ds(base, ATT_BLK, stride=d)


def _attn_block_index(i, d):
    nb = SEQ // (ATT_BLK * d)
    r = i // nb
    n = i % nb
    base = r + n * (ATT_BLK * d)
    pbase = jnp.maximum(base - ATT_BLK * d, r)
    return n, _attn_rows(base, d), _attn_rows(pbase, d)


def _qk_normed(x, gain):
    rs = lax.rsqrt(jnp.mean(x * x, axis=-1, keepdims=True) + RMS_EPS)
    return x * rs, rs


def _attn_fwd(qkv9, qgain, kgain, bias, name):
    def body(q_ref, k_ref, v_ref, qg_ref, kg_ref, b_ref, o_ref, lse_ref, qn_s, kn_s, acc_s, m_s, l_s):
        g = pl.program_id(1)

        @pl.when(g == 0)
        def _():
            m_s[...] = jnp.full_like(m_s, NEG)
            l_s[...] = jnp.zeros_like(l_s)
            acc_s[...] = jnp.zeros_like(acc_s)

        qn_s[...] = _qk_normed(q_ref[...], None)[0] * qg_ref[...]
        kn_s[...] = _qk_normed(k_ref[...], None)[0] * kg_ref[...]

        for gi, (_, d) in enumerate(GROUPS):
            @pl.when(g == gi)
            def _(d=d):
                def it(i, carry):
                    n, rows, prow = _attn_block_index(i, d)
                    qb = qn_s[rows, :].astype(BF16)
                    kc = kn_s[rows, :].astype(BF16)
                    kp = kn_s[prow, :].astype(BF16)
                    vc = v_ref[rows, :].astype(BF16)
                    vp = v_ref[prow, :].astype(BF16)
                    sc = _dot_nt(qb, kc) * ATT_SCALE + b_ref[1]
                    sp = _dot_nt(qb, kp) * ATT_SCALE + jnp.where(n > 0, b_ref[0], NEG)
                    m_old = m_s[rows, :]
                    m_new = jnp.maximum(m_old, jnp.maximum(jnp.max(sc, axis=-1, keepdims=True),
                                                           jnp.max(sp, axis=-1, keepdims=True)))
                    alpha = jnp.exp(m_old - m_new)
                    pc = jnp.exp(sc - m_new)
                    pp = jnp.exp(sp - m_new)
                    l_s[rows, :] = alpha * l_s[rows, :] + jnp.sum(pc, axis=-1, keepdims=True) \
                        + jnp.sum(pp, axis=-1, keepdims=True)
                    acc_s[rows, :] = alpha * acc_s[rows, :] + _dot(pc.astype(BF16), vc) + _dot(pp.astype(BF16), vp)
                    m_s[rows, :] = m_new
                    return carry

                lax.fori_loop(0, SEQ // ATT_BLK, it, 0)

        @pl.when(g == len(GROUPS) - 1)
        def _():
            o_ref[...] = (acc_s[...] / l_s[...]).astype(BF16)
            lse_ref[...] = m_s[...] + jnp.log(l_s[...])

    def col(j):
        return pl.BlockSpec((None, SEQ, HEAD_DIM), lambda h, g: (g * 3 + j, 0, h))

    gspec = pl.BlockSpec((None, 1, HEAD_DIM), lambda h, g: (g, 0, 0))
    return _pcall(
        body, name=name, grid=(HEADS, len(GROUPS)),
        in_specs=[col(0), col(1), col(2), gspec, gspec,
                  pl.BlockSpec((None, None, 2, ATT_BLK, ATT_BLK), lambda h, g: (g, h, 0, 0, 0))],
        out_specs=[pl.BlockSpec((None, SEQ, HEAD_DIM), lambda h, g: (h // 2, 0, h % 2)),
                   pl.BlockSpec((None, SEQ, 1), lambda h, g: (h, 0, 0))],
        out_shape=[_sds((N_CHIP, SEQ, 2 * HEAD_DIM), BF16), _sds((HEADS, SEQ, 1), F32)],
        scratch_shapes=[pltpu.VMEM((SEQ, HEAD_DIM), F32)] * 3 + [pltpu.VMEM((SEQ, 1), F32)] * 2,
        compiler_params=_cparams(("parallel", "arbitrary")),
    )(qkv9, qkv9, qkv9, qgain, kgain, bias)


def _attn_bwd(qkv9, qgain, kgain, bias, do4, o4, lse, name):
    def body(q_ref, k_ref, v_ref, qg_ref, kg_ref, b_ref, do_ref, o_ref, lse_ref,
             dqkv_ref, dqg_ref, dkg_ref, db_ref, qn_s, kn_s, dq_s, dk_s, dv_s, dl_s):
        g = pl.program_id(1)
        qh, rq = _qk_normed(q_ref[...], None)
        kh, rk = _qk_normed(k_ref[...], None)
        qn_s[...] = qh * qg_ref[...]
        kn_s[...] = kh * kg_ref[...]
        dl_s[...] = jnp.sum(do_ref[...] * o_ref[...].astype(F32), axis=-1, keepdims=True)
        dk_s[...] = jnp.zeros_like(dk_s)
        dv_s[...] = jnp.zeros_like(dv_s)
        db_ref[...] = jnp.zeros_like(db_ref)

        for gi, (_, d) in enumerate(GROUPS):
            @pl.when(g == gi)
            def _(d=d):
                def it(i, carry):
                    n, rows, prow = _attn_block_index(i, d)
                    qb = qn_s[rows, :].astype(BF16)
                    kc = kn_s[rows, :].astype(BF16)
                    kp = kn_s[prow, :].astype(BF16)
                    vc = v_ref[rows, :].astype(BF16)
                    vp = v_ref[prow, :].astype(BF16)
                    dob = do_ref[rows, :].astype(BF16)
                    lse_b = lse_ref[rows, :]
                    dl = dl_s[rows, :]
                    sc = _dot_nt(qb, kc) * ATT_SCALE + b_ref[1]
                    sp = _dot_nt(qb, kp) * ATT_SCALE + jnp.where(n > 0, b_ref[0], NEG)
                    pc = jnp.exp(sc - lse_b)
                    pp = jnp.exp(sp - lse_b)
                    dsc = pc * (_dot_nt(dob, vc) - dl)
                    dsp = pp * (_dot_nt(dob, vp) - dl)
                    db_ref[1] += dsc
                    db_ref[0] += dsp
                    dsc16 = dsc.astype(BF16)
                    dsp16 = dsp.astype(BF16)
                    dq_s[rows, :] = (_dot(dsc16, kc) + _dot(dsp16, kp)) * ATT_SCALE
                    dk_s[rows, :] += _dot_tn(dsc16, qb) * ATT_SCALE
                    dk_s[prow, :] += _dot_tn(dsp16, qb) * ATT_SCALE
                    dv_s[rows, :] += _dot_tn(pc.astype(BF16), dob)
                    dv_s[prow, :] += _dot_tn(pp.astype(BF16), dob)
                    return carry

                lax.fori_loop(0, SEQ // ATT_BLK, it, 0)

        def norm_bwd(dn, xh, rs, gain):
            dgain = jnp.sum(dn * xh, axis=0, keepdims=True)
            dxh = dn * gain
            return rs * (dxh - xh * jnp.mean(dxh * xh, axis=-1, keepdims=True)), dgain

        dq, dqg = norm_bwd(dq_s[...], qh, rq, qg_ref[...])
        dk, dkg = norm_bwd(dk_s[...], kh, rk, kg_ref[...])
        dqkv_ref[0] = dq.astype(BF16)
        dqkv_ref[1] = dk.astype(BF16)
        dqkv_ref[2] = dv_s[...].astype(BF16)
        dqg_ref[...] = dqg
        dkg_ref[...] = dkg

    def col(j):
        return pl.BlockSpec((None, SEQ, HEAD_DIM), lambda h, g: (g * 3 + j, 0, h))

    gspec = pl.BlockSpec((None, 1, HEAD_DIM), lambda h, g: (g, 0, 0))
    bspec = pl.BlockSpec((None, None, 2, ATT_BLK, ATT_BLK), lambda h, g: (g, h, 0, 0, 0))
    hcol = pl.BlockSpec((None, SEQ, HEAD_DIM), lambda h, g: (h // 2, 0, h % 2))
    dgspec = pl.BlockSpec((None, None, 1, HEAD_DIM), lambda h, g: (h, g, 0, 0))
    ng = len(GROUPS)
    return _pcall(
        body, name=name, grid=(HEADS, ng),
        in_specs=[col(0), col(1), col(2), gspec, gspec, bspec, hcol, hcol,
                  pl.BlockSpec((None, SEQ, 1), lambda h, g: (h, 0, 0))],
        out_specs=[pl.BlockSpec((None, 3, SEQ, HEAD_DIM), lambda h, g: (g, 0, 0, h)), dgspec, dgspec, bspec],
        out_shape=[_sds((ng, 3, SEQ, D_MODEL), BF16), _sds((HEADS, ng, 1, HEAD_DIM), F32),
                   _sds((HEADS, ng, 1, HEAD_DIM), F32), _sds((ng, HEADS, 2, ATT_BLK, ATT_BLK), F32)],
        scratch_shapes=[pltpu.VMEM((SEQ, HEAD_DIM), F32)] * 5 + [pltpu.VMEM((SEQ, 1), F32)],
        compiler_params=_cparams(("parallel", "arbitrary")),
    )(qkv9, qkv9, qkv9, qgain, kgain, bias, do4, o4, lse)


def _relbias_bwd(dbias, bucket_idx, name):
    ng = len(GROUPS)

    def body(db_ref, idx_ref, o_ref):
        lane = lax.broadcasted_iota(jnp.int32, (HEADS, 128), 1)
        acc = jnp.zeros((HEADS, 128), F32)
        for g in range(ng):
            dbg = db_ref[g]
            idx = idx_ref[g]
            for b in range(NUM_BUCKETS):
                sel = jnp.where((idx == b)[None], dbg, 0.0)
                part = jnp.sum(jnp.sum(sel, axis=1), axis=1)
                val = jnp.sum(part, axis=-1, keepdims=True)
                acc = jnp.where(lane == g * NUM_BUCKETS + b, val, acc)
        o_ref[...] = acc

    return _pcall(body, name=name, out_shape=_sds((HEADS, 128), F32), compiler_params=_cparams())(dbias, bucket_idx)


def _scan16(x, reverse=False):
    row = lax.broadcasted_iota(jnp.int32, x.shape, 0)
    for sh in (1, 2, 4, 8):
        if reverse:
            x = x + jnp.where(row < HG_SUB - sh, pltpu.roll(x, HG_SUB - sh, 0), 0.0)
        else:
            x = x + jnp.where(row >= sh, pltpu.roll(x, sh, 0), 0.0)
    return x


def _hgrn_gates(qr, fr, lbv):
    q = _silu(qr)
    sig = _sigmoid(fr)
    fg = lbv + (1.0 - lbv) * sig
    lf = jnp.log(fg)
    gcum = _scan16(lf)
    glast = jnp.sum(lf, axis=0, keepdims=True)
    return q, sig, fg, 1.0 - fg, gcum, glast


def _hgrn_intra(q, k, gcum, tri):
    e = jnp.exp(jnp.where(tri, gcum[:, None, :] - gcum[None, :, :], NEG))
    a = jnp.sum(q[:, None, :] * k[None, :, :] * e, axis=-1, keepdims=True)
    return e, a


def _hgrn_fwd(proj4, lb, gain, name):
    nsub = HG_TC // HG_SUB
    wide = HG_HP * HEAD_DIM

    def body(p_ref, lb_ref, gn_ref, o_ref, y_ref, st_ref, state_s):
        @pl.when(pl.program_id(1) == 0)
        def _():
            state_s[...] = jnp.zeros_like(state_s)

        gnv = gn_ref[...]
        shp = (HG_SUB, HG_SUB, HEAD_DIM)
        tri = lax.broadcasted_iota(jnp.int32, shp, 0) >= lax.broadcasted_iota(jnp.int32, shp, 1)

        def head(qr, fr, vv, gr, lbv, st):
            q, _, _, k, gcum, glast = _hgrn_gates(qr, fr, lbv)
            _, a = _hgrn_intra(q, k, gcum, tri)
            o = jnp.sum(a * vv[None, :, :], axis=1) + _dot_nt((q * jnp.exp(gcum)).astype(BF16), st.astype(BF16))
            kg = k * jnp.exp(glast - gcum)
            st_new = st * jnp.exp(glast) + _dot_tn(vv.astype(BF16), kg.astype(BF16))
            rs = lax.rsqrt(jnp.mean(o * o, axis=-1, keepdims=True) + RMS_EPS)
            return o, (o * rs * gnv * _silu(gr)).astype(BF16), st_new

        def it(i, carry):
            rows = pl.ds(pl.multiple_of(i * HG_SUB, HG_SUB), HG_SUB)
            loaded = []
            for hh in range(HG_HP):
                lanes = pl.ds(hh * HEAD_DIM, HEAD_DIM)
                loaded.append(([p_ref[j, rows, lanes] for j in range(4)], lb_ref[:, lanes], state_s[hh]))
            results = [head(blk[0], blk[1], blk[2], blk[3], lbv, st) for blk, lbv, st in loaded]
            for hh, ((_, _, st), (o, y, st_new)) in enumerate(zip(loaded, results)):
                lanes = pl.ds(hh * HEAD_DIM, HEAD_DIM)
                st_ref[hh, i] = st.astype(BF16)
                state_s[hh] = st_new
                o_ref[rows, lanes] = o
                y_ref[rows, lanes] = y
            return carry

        lax.fori_loop(0, nsub, it, 0)

    return _pcall(
        body, name=name, grid=(HEADS // HG_HP, SEQ // HG_TC),
        in_specs=[pl.BlockSpec((4, HG_TC, wide), lambda h, j: (0, j, h)),
                  pl.BlockSpec((1, wide), lambda h, j: (0, h)),
                  pl.BlockSpec((1, HEAD_DIM), lambda h, j: (0, 0))],
        out_specs=[pl.BlockSpec((HG_TC, wide), lambda h, j: (j, h)),
                   pl.BlockSpec((None, HG_TC, wide), lambda h, j: (h, j, 0)),
                   pl.BlockSpec((HG_HP, nsub, HEAD_DIM, HEAD_DIM), lambda h, j: (h, j, 0, 0))],
        out_shape=[_sds((SEQ, D_MODEL), F32), _sds((N_CHIP, SEQ, 2 * HEAD_DIM), BF16),
                   _sds((HEADS, SEQ // HG_SUB, HEAD_DIM, HEAD_DIM), BF16)],
        scratch_shapes=[pltpu.VMEM((HG_HP, HEAD_DIM, HEAD_DIM), F32)],
        compiler_params=_cparams(("parallel", "arbitrary")),
    )(proj4, lb, gain)


def _hgrn_bwd(proj4, lb, gain, o_raw, dy4, states, name):
    nsub = HG_TC // HG_SUB
    nt = SEQ // HG_TC
    wide = HG_HP * HEAD_DIM

    def body(p_ref, lb_ref, gn_ref, o_ref, dy_ref, st_ref, dp_ref, dlb_ref, dgn_ref, dst_s):
        @pl.when(pl.program_id(1) == 0)
        def _():
            dst_s[...] = jnp.zeros_like(dst_s)
            dlb_ref[...] = jnp.zeros_like(dlb_ref)
            dgn_ref[...] = jnp.zeros_like(dgn_ref)

        gnv = gn_ref[...]
        shp = (HG_SUB, HG_SUB, HEAD_DIM)
        tri = lax.broadcasted_iota(jnp.int32, shp, 0) >= lax.broadcasted_iota(jnp.int32, shp, 1)

        def head(qr, fr, vv, gr, o, dy, lbv, st0, dst):
            q, sig, fg, k, gcum, glast = _hgrn_gates(qr, fr, lbv)
            rs = lax.rsqrt(jnp.mean(o * o, axis=-1, keepdims=True) + RMS_EPS)
            oh = o * rs
            don = dy * _silu(gr)
            dgn = jnp.sum(don * oh, axis=0, keepdims=True)
            dgr = dy * oh * gnv * _dsilu(gr)
            doh = don * gnv
            do = rs * (doh - oh * jnp.mean(doh * oh, axis=-1, keepdims=True))
            dst16 = dst.astype(BF16)
            do16 = do.astype(BF16)
            eg = jnp.exp(gcum)
            eb = jnp.exp(glast - gcum)
            e, a = _hgrn_intra(q, k, gcum, tri)
            da = jnp.sum(do[:, None, :] * vv[None, :, :], axis=-1, keepdims=True)
            dae = da * e
            dq = jnp.sum(dae * k[None, :, :], axis=1) + eg * _dot(do16, st0)
            dk_state = eb * _dot(vv.astype(BF16), dst16)
            dk = jnp.sum(dae * q[:, None, :], axis=0) + dk_state
            dv = jnp.sum(a * do[:, None, :], axis=0) + _dot_nt((k * eb).astype(BF16), dst16)
            eglast = jnp.exp(glast)
            dst_new = dst * eglast + _dot_tn(do16, (q * eg).astype(BF16))
            dglast = jnp.sum(k * dk_state, axis=0, keepdims=True) \
                + eglast * jnp.sum(dst * st0.astype(F32), axis=0, keepdims=True)
            dlf = _scan16(q * dq - k * dk, reverse=True) + dglast
            dfg = dlf / fg - dk
            dlb = jnp.sum(dfg * (1.0 - sig), axis=0, keepdims=True)
            dproj = ((dq * _dsilu(qr)).astype(BF16), (dfg * (1.0 - lbv) * sig * (1.0 - sig)).astype(BF16),
                     dv.astype(BF16), dgr.astype(BF16))
            return dproj, dst_new, dlb, dgn

        def it(ii, carry):
            i = nsub - 1 - ii
            rows = pl.ds(pl.multiple_of(i * HG_SUB, HG_SUB), HG_SUB)
            results = []
            for hh in range(HG_HP):
                lanes = pl.ds(hh * HEAD_DIM, HEAD_DIM)
                blk = [p_ref[j, rows, lanes] for j in range(4)]
                results.append(head(blk[0], blk[1], blk[2], blk[3], o_ref[rows, lanes], dy_ref[rows, lanes],
                                    lb_ref[:, lanes], st_ref[hh, i], dst_s[hh]))
            new_carry = []
            for hh, (dproj, dst_new, dlb, dgn) in enumerate(results):
                lanes = pl.ds(hh * HEAD_DIM, HEAD_DIM)
                dst_s[hh] = dst_new
                for j in range(4):
                    dp_ref[j, rows, lanes] = dproj[j]
                new_carry.append((carry[hh][0] + dlb, carry[hh][1] + dgn))
            return tuple(new_carry)

        zero = jnp.zeros((1, HEAD_DIM), F32)
        sums = lax.fori_loop(0, nsub, it, tuple((zero, zero) for _ in range(HG_HP)))
        for hh in range(HG_HP):
            dlb_ref[hh] += sums[hh][0]
            dgn_ref[hh] += sums[hh][1]

    vspec = pl.BlockSpec((HG_HP, 1, HEAD_DIM), lambda h, j: (h, 0, 0))
    return _pcall(
        body, name=name, grid=(HEADS // HG_HP, nt),
        in_specs=[pl.BlockSpec((4, HG_TC, wide), lambda h, j: (0, nt - 1 - j, h)),
                  pl.BlockSpec((1, wide), lambda h, j: (0, h)),
                  pl.BlockSpec((1, HEAD_DIM), lambda h, j: (0, 0)),
                  pl.BlockSpec((HG_TC, wide), lambda h, j: (nt - 1 - j, h)),
                  pl.BlockSpec((None, HG_TC, wide), lambda h, j: (h, nt - 1 - j, 0)),
                  pl.BlockSpec((HG_HP, nsub, HEAD_DIM, HEAD_DIM), lambda h, j: (h, nt - 1 - j, 0, 0))],
        out_specs=[pl.BlockSpec((4, HG_TC, wide), lambda h, j: (0, nt - 1 - j, h)), vspec, vspec],
        out_shape=[_sds((4, SEQ, D_MODEL), BF16), _sds((HEADS, 1, HEAD_DIM), F32), _sds((HEADS, 1, HEAD_DIM), F32)],
        scratch_shapes=[pltpu.VMEM((HG_HP, HEAD_DIM, HEAD_DIM), F32)],
        compiler_params=_cparams(("parallel", "arbitrary")),
    )(proj4, lb, gain, o_raw, dy4, states)


def _t5_bucket(dist):
    n = np.asarray(dist, dtype=np.int64)
    max_exact = NUM_BUCKETS // 2
    large = max_exact + (np.log(np.maximum(n, 1) / max_exact) / np.log(MAX_DISTANCE / max_exact)
                         * (NUM_BUCKETS - max_exact)).astype(np.int64)
    large = np.minimum(large, NUM_BUCKETS - 1)
    return np.where(n < max_exact, n, large).astype(np.int32)


def _bias_tables():
    qi = np.arange(ATT_BLK)[:, None]
    ki = np.arange(ATT_BLK)[None, :]
    steps = (ATT_BLK + qi - ki, qi - ki)
    idx = np.zeros((len(GROUPS), 2, ATT_BLK, ATT_BLK), np.int32)
    for g, (_, d) in enumerate(GROUPS):
        for p, j in enumerate(steps):
            valid = (j >= 0) & (j <= ATT_BLK)
            idx[g, p] = np.where(valid, _t5_bucket(np.clip(j, 0, ATT_BLK) * d), -1)
    return idx


def _attn_bias(rel_bias, name):
    idx = _bias_tables()
    ng = len(GROUPS)
    buckets = [sorted(set(idx[g][idx[g] >= 0].tolist())) for g in range(ng)]

    def body(rb_ref, idx_ref, o_ref):
        h = pl.program_id(0)
        for g in range(ng):
            ig = idx_ref[g]
            acc = jnp.full(ig.shape, NEG, F32)
            for b in buckets[g]:
                acc = jnp.where(ig == b, rb_ref[b, g * HEADS + h], acc)
            o_ref[g] = acc

    return _pcall(
        body, name=name, grid=(HEADS,),
        in_specs=[pl.BlockSpec(memory_space=pltpu.SMEM),
                  pl.BlockSpec((ng, 2, ATT_BLK, ATT_BLK), lambda h: (0, 0, 0, 0))],
        out_specs=pl.BlockSpec((ng, None, 2, ATT_BLK, ATT_BLK), lambda h: (0, h, 0, 0, 0)),
        out_shape=_sds((ng, HEADS, 2, ATT_BLK, ATT_BLK), F32),
        compiler_params=_cparams(("parallel",)),
    )(rel_bias, jnp.asarray(idx))


ADA_SHARD = 6 * D_MODEL // N_CHIP
ADA_TN = 512


def _ada_fwd(c_all, ada_w, ada_b_cols, name):
    def body(c_ref, w_ref, b_ref, o_ref):
        ca = _silu(c_ref[...]).astype(BF16)
        o_ref[...] = _dot(ca, w_ref[...].astype(BF16)) + b_ref[...]

    return _pcall(
        body, name=name, grid=(DEPTH, ADA_SHARD // ADA_TN),
        in_specs=[pl.BlockSpec((N_DEV, D_MODEL), lambda l, j: (0, 0)),
                  pl.BlockSpec((None, D_MODEL, ADA_TN), lambda l, j: (l, 0, j)),
                  pl.BlockSpec((None, 1, ADA_TN), lambda l, j: (l, 0, j))],
        out_specs=pl.BlockSpec((None, N_DEV, ADA_TN), lambda l, j: (l, 0, j)),
        out_shape=_sds((DEPTH, N_DEV, ADA_SHARD), F32),
        compiler_params=_cparams(("parallel", "parallel")),
    )(c_all, ada_w, ada_b_cols)


def _ada_bwd(c_all, dmod_cols, name):
    def body(c_ref, d_ref, o_ref):
        ca = _silu(c_ref[...]).astype(BF16)
        o_ref[...] = _dot_tn(ca, d_ref[...].astype(BF16))

    return _pcall(
        body, name=name, grid=(DEPTH, ADA_SHARD // ADA_TN),
        in_specs=[pl.BlockSpec((N_DEV, D_MODEL), lambda l, j: (0, 0)),
                  pl.BlockSpec((None, N_DEV, ADA_TN), lambda l, j: (l, 0, j))],
        out_specs=pl.BlockSpec((None, D_MODEL, ADA_TN), lambda l, j: (l, 0, j)),
        out_shape=_sds((DEPTH, D_MODEL, ADA_SHARD), F32),
        compiler_params=_cparams(("parallel", "parallel")),
    )(c_all, dmod_cols)


def _lower_bounds(logits, name):
    def body(l_ref, o_ref):
        l0 = l_ref[0:1, :]
        l1 = l_ref[1:2, :]
        mx = jnp.maximum(l0, l1)
        e0 = jnp.exp(l0 - mx)
        e1 = jnp.exp(l1 - mx)
        p0 = e0 / (e0 + e1)
        p1 = e1 / (e0 + e1)
        o_ref[0:1, :] = p0 - p0
        o_ref[1:2, :] = (p0 + p1) - p0

    return _pcall(body, name=name, out_shape=_sds((DEPTH, D_MODEL), F32), compiler_params=_cparams())(logits)


_R_DMOD = 0
_R_NMIX = 96
_R_NFFN = 112
_R_QG = 128
_R_KG = 152
_R_GN = 176
_R_LB = 184
_R_RB = 192
SMALL_ROWS = 200


def _small_totals(gathered, logits8, name):
    ng = len(GROUPS)

    def body(g_ref, l_ref, main_ref, gains_ref, dlb_ref, rb_ref):
        tot = g_ref[0]
        for dev in range(1, N_DEV):
            tot = tot + g_ref[dev]
        main_ref[...] = tot[0:_R_QG]
        gains_ref[...] = jnp.zeros_like(gains_ref)
        for g in range(ng):
            gains_ref[g:g + 1, :] = jnp.sum(tot[_R_QG + 8 * g:_R_QG + 8 * g + 8], axis=0, keepdims=True)
            gains_ref[ng + g:ng + g + 1, :] = jnp.sum(tot[_R_KG + 8 * g:_R_KG + 8 * g + 8], axis=0, keepdims=True)
        gains_ref[2 * ng:2 * ng + 1, :] = jnp.sum(tot[_R_GN:_R_GN + 8], axis=0, keepdims=True)
        rb_ref[...] = tot[_R_RB:_R_RB + 8]
        dlb1 = tot[_R_LB:_R_LB + 8]
        l0 = l_ref[0]
        l1 = l_ref[1]
        mx = jnp.maximum(l0, l1)
        e0 = jnp.exp(l0 - mx)
        e1 = jnp.exp(l1 - mx)
        p0 = e0 / (e0 + e1)
        p1 = e1 / (e0 + e1)
        dlb_ref[0] = -p0 * p1 * dlb1
        dlb_ref[1] = p1 * (1.0 - p1) * dlb1

    return _pcall(
        body, name=name,
        out_shape=[_sds((_R_QG, 128), F32), _sds((8, 128), F32), _sds((DEPTH, 8, 128), F32), _sds((8, 128), F32)],
        compiler_params=_cparams(),
    )(gathered, logits8)


def _row_tile(rows):
    return 128 if rows % 128 == 0 else rows


def _adamw(w, grads, m, v, name):
    nl, r, cdim = w.shape
    tr = _row_tile(r)

    def body(*refs):
        g_refs = refs[:nl]
        w_ref, m_ref, v_ref, go_ref, d_ref, mo_ref, vo_ref = refs[nl:]

        def step(g):
            m2 = ADAM_B1 * m_ref[...] + (1.0 - ADAM_B1) * g
            v2 = ADAM_B2 * v_ref[...] + (1.0 - ADAM_B2) * (g * g)
            m_hat = m2 / (1.0 - ADAM_B1 ** ADAM_STEP)
            v_hat = v2 / (1.0 - ADAM_B2 ** ADAM_STEP)
            go_ref[...] = g
            d_ref[...] = -ADAM_LR * (m_hat / (jnp.sqrt(v_hat) + ADAM_EPS) + ADAM_WD * w_ref[...])
            mo_ref[...] = m2
            vo_ref[...] = v2

        if nl == 1:
            step(g_refs[0][...])
        else:
            for layer in range(nl):
                @pl.when(pl.program_id(0) == layer)
                def _(layer=layer):
                    step(g_refs[layer][...])

    big = pl.BlockSpec((None, tr, cdim), lambda l, i: (l, i, 0))
    g_specs = [pl.BlockSpec((tr, cdim), lambda l, i, layer=layer: (jnp.where(l == layer, i, 0), 0))
               for layer in range(nl)]
    shp = _sds((nl, r, cdim), F32)
    return _pcall(
        body, name=name, grid=(nl, r // tr),
        in_specs=g_specs + [big, big, big],
        out_specs=[big, big, big, big],
        out_shape=[shp, shp, shp, shp],
        compiler_params=_cparams(("parallel", "parallel")),
    )(*grads, w, m, v)


def _cast_bf16(place, w, name):
    nl, r, cdim = w.shape
    tr = _row_tile(r)

    def body(place_ref, w_ref, o_ref):
        o_ref[...] = w_ref[...].astype(BF16)

    return _pcall(
        body, name=name,
        grid_spec=pltpu.PrefetchScalarGridSpec(
            num_scalar_prefetch=1, grid=(nl, r // tr),
            in_specs=[pl.BlockSpec((None, tr, cdim), lambda l, i, place_ref: (l, i, 0))],
            out_specs=pl.BlockSpec((None, None, tr, cdim), lambda l, i, place_ref: (place_ref[1], l, i, 0))),
        out_shape=_sds((N_CHIP, nl, r, cdim), BF16),
        compiler_params=_cparams(("parallel", "parallel")),
    )(place, w)


def _rs_add_cast(place, grad, recv, name):
    _, k, n = grad.shape
    kh = k // 2
    tr = _row_tile(kh)
    nb = kh // tr

    def body(place_ref, g_ref, r_ref, o_ref):
        o_ref[...] = (g_ref[...] + r_ref[...]).astype(BF16)

    half = pl.BlockSpec((None, tr, n), lambda s, i, place_ref: (s, i, 0))
    return _pcall(
        body, name=name,
        grid_spec=pltpu.PrefetchScalarGridSpec(
            num_scalar_prefetch=1, grid=(N_CHIP, nb),
            in_specs=[pl.BlockSpec((None, tr, n), lambda s, i, place_ref: (s, place_ref[0] * nb + i, 0)), half],
            out_specs=half),
        out_shape=_sds((N_CHIP, kh, n), BF16),
        compiler_params=_cparams(("parallel", "parallel")),
    )(place, grad, recv)


def _rs_sum4(place, parts, got, name):
    _, kh, n = parts.shape
    tr = _row_tile(kh)
    nb = kh // tr

    def body(place_ref, p_ref, g_ref, o_ref):
        acc = p_ref[...].astype(F32)
        for j in range(N_CHIP - 1):
            acc = acc + g_ref[j].astype(F32)
        o_ref[...] = acc

    return _pcall(
        body, name=name,
        grid_spec=pltpu.PrefetchScalarGridSpec(
            num_scalar_prefetch=1, grid=(nb,),
            in_specs=[pl.BlockSpec((None, tr, n), lambda i, place_ref: (place_ref[1], i, 0)),
                      pl.BlockSpec((N_CHIP - 1, tr, n), lambda i, place_ref: (0, i, 0))],
            out_specs=pl.BlockSpec((tr, n), lambda i, place_ref: (place_ref[0] * nb + i, 0))),
        out_shape=_sds((2 * kh, n), F32),
        compiler_params=_cparams(("parallel",)),
    )(place, parts, got)


_ANY = pl.BlockSpec(memory_space=pl.ANY)


def _position():
    return lax.axis_index("x"), lax.axis_index("y"), lax.axis_index("c")


def _other_chips(x, y):
    return [(1 - x, y), (x, 1 - y), (1 - x, 1 - y)]


def _remote(src, dst, send_sem, recv_sem, to):
    return pltpu.make_async_remote_copy(src_ref=src, dst_ref=dst, send_sem=send_sem, recv_sem=recv_sem,
                                        device_id=to, device_id_type=MESH)


def _small_allgather(v, name):
    r = v.shape[0]

    def body(x_ref, out_ref, send_sems, recv_sems, local_sem):
        x, y, c = _position()
        me, sibling = (x, y, c), (x, y, 1 - c)
        chips = _other_chips(x, y)

        def slab(px, py, pc):
            return out_ref.at[4 * px + 2 * py + pc]

        def copy(k, block, to, src=None):
            return _remote(slab(*block) if src is None else src, slab(*block), send_sems.at[k], recv_sems.at[k], to)

        mine = pltpu.make_async_copy(x_ref, slab(*me), local_sem)
        mine.start()
        first = [copy(0, me, sibling, src=x_ref)]
        first += [copy(1 + j, me, (*chip, c), src=x_ref) for j, chip in enumerate(chips)]
        for cp in first:
            cp.start()
        passed = [copy(4 + j, (*chip, c), sibling) for j, chip in enumerate(chips)]
        for j, chip in enumerate(chips):
            copy(1 + j, (*chip, c), me).wait_recv()
            passed[j].start()
        copy(0, sibling, me).wait_recv()
        for j, chip in enumerate(chips):
            copy(4 + j, (*chip, 1 - c), me).wait_recv()
        for cp in first + passed:
            cp.wait_send()
        mine.wait()

    return _pcall(
        body, name=name,
        out_shape=_sds((N_DEV, r, 128), F32),
        in_specs=[pl.BlockSpec(memory_space=pltpu.VMEM)],
        out_specs=pl.BlockSpec(memory_space=pltpu.VMEM),
        scratch_shapes=[pltpu.SemaphoreType.DMA((7,)), pltpu.SemaphoreType.DMA((7,)), pltpu.SemaphoreType.DMA],
        compiler_params=_cparams(),
    )(v)


def _half_rows(core, kh):
    return pl.ds(pl.multiple_of(core * kh, 8), kh)


def _gather_weights(slabs, name):
    n = len(slabs)

    def body(*refs):
        out = refs[n:2 * n]
        ici_send, ici_recv, d2d_send, d2d_recv = refs[2 * n:]
        x, y, c = _position()
        me_chip = 2 * x + y
        sibling = (x, y, 1 - c)
        chips = _other_chips(x, y)

        def region(a, chip, core):
            kh = out[a].shape[2] // 2
            return out[a].at[chip, :, _half_rows(core, kh), :]

        sends = []
        for a in range(n):
            for j, (px, py) in enumerate(chips):
                mine = region(a, me_chip, c)
                cp = _remote(mine, mine, ici_send.at[a, j], ici_recv.at[a, j], (px, py, c))
                cp.start()
                sends.append(cp)
        for j, (px, py) in enumerate(chips):
            for a in range(n):
                landed = region(a, 2 * px + py, c)
                _remote(landed, landed, ici_send.at[a, j], ici_recv.at[a, j], (px, py, c)).wait_recv()
                cp = _remote(landed, landed, d2d_send.at[a, j], d2d_recv.at[a, j], sibling)
                cp.start()
                sends.append(cp)
        for j, (px, py) in enumerate(chips):
            for a in range(n):
                other = region(a, 2 * px + py, 1 - c)
                _remote(other, other, d2d_send.at[a, j], d2d_recv.at[a, j], sibling).wait_recv()
        for cp in sends:
            cp.wait_send()

    sem = pltpu.SemaphoreType.DMA((n, 3))
    return _pcall(
        body, name=name,
        out_shape=[_sds(s.shape, BF16) for s in slabs],
        in_specs=[_ANY] * n, out_specs=[_ANY] * n,
        input_output_aliases={a: a for a in range(n)},
        scratch_shapes=[sem, sem, sem, sem],
        compiler_params=_cparams(),
    )(*slabs)


def _rs_exchange_halves(grads, name):
    n = len(grads)

    def body(*refs):
        g = refs[:n]
        out = refs[n:2 * n]
        send_sems, recv_sems = refs[2 * n:]
        x, y, c = _position()
        copies = []
        for a in range(n):
            kh = g[a].shape[1] // 2
            cp = _remote(g[a].at[:, _half_rows(1 - c, kh), :], out[a], send_sems.at[a], recv_sems.at[a], (x, y, 1 - c))
            cp.start()
            copies.append(cp)
        for cp in copies:
            cp.wait()

    return _pcall(
        body, name=name,
        out_shape=[_sds((N_CHIP, g.shape[1] // 2, g.shape[2]), F32) for g in grads],
        in_specs=[_ANY] * n, out_specs=[_ANY] * n,
        scratch_shapes=[pltpu.SemaphoreType.DMA((n,)), pltpu.SemaphoreType.DMA((n,))],
        compiler_params=_cparams(),
    )(*grads)


def _rs_exchange_chips(parts, name):
    n = len(parts)

    def body(*refs):
        p = refs[:n]
        out = refs[n:2 * n]
        send_sems, recv_sems = refs[2 * n:]
        x, y, c = _position()
        chips = _other_chips(x, y)
        sends = []
        for a in range(n):
            for j, (px, py) in enumerate(chips):
                cp = _remote(p[a].at[2 * px + py], out[a].at[j], send_sems.at[a, j], recv_sems.at[a, j], (px, py, c))
                cp.start()
                sends.append(cp)
        for a in range(n):
            for j, (px, py) in enumerate(chips):
                got = out[a].at[j]
                _remote(got, got, send_sems.at[a, j], recv_sems.at[a, j], (px, py, c)).wait_recv()
        for cp in sends:
            cp.wait_send()

    sem = pltpu.SemaphoreType.DMA((n, 3))
    return _pcall(
        body, name=name,
        out_shape=[_sds((N_CHIP - 1,) + p.shape[1:], BF16) for p in parts],
        in_specs=[_ANY] * n, out_specs=[_ANY] * n,
        scratch_shapes=[sem, sem],
        compiler_params=_cparams(),
    )(*parts)


def _rs_join_halves(fulls, name):
    n = len(fulls)

    def body(*refs):
        out = refs[n:2 * n]
        send_sems, recv_sems = refs[2 * n:]
        x, y, c = _position()
        copies = []
        for a in range(n):
            kh = out[a].shape[0] // 2
            mine = out[a].at[_half_rows(c, kh), :]
            cp = _remote(mine, mine, send_sems.at[a], recv_sems.at[a], (x, y, 1 - c))
            cp.start()
            copies.append(cp)
        for a in range(n):
            kh = out[a].shape[0] // 2
            theirs = out[a].at[_half_rows(1 - c, kh), :]
            _remote(theirs, theirs, send_sems.at[a], recv_sems.at[a], (x, y, 1 - c)).wait_recv()
        for cp in copies:
            cp.wait_send()

    return _pcall(
        body, name=name,
        out_shape=[_sds(f.shape, F32) for f in fulls],
        in_specs=[_ANY] * n, out_specs=[_ANY] * n,
        input_output_aliases={a: a for a in range(n)},
        scratch_shapes=[pltpu.SemaphoreType.DMA((n,)), pltpu.SemaphoreType.DMA((n,))],
        compiler_params=_cparams(),
    )(*fulls)


_SMALL_ORDER = ("rel_bias", "ada_b", "norm_mix", "norm_ffn", "attn_q_gain", "attn_k_gain", "hgrn_gnorm",
                "hgrn_lower_bounds")
_WEIGHT_ORDER = ("rel_bias", "ada_w", "ada_b", "norm_mix", "norm_ffn", "attn_w_qkv", "attn_w_out", "attn_q_gain",
                 "attn_k_gain", "hgrn_w_in", "hgrn_w_out", "hgrn_gnorm", "hgrn_lower_bounds", "ffn_w1", "ffn_w3",
                 "ffn_w2")


def _qkv_act_map(t):
    return t // 4, t % 4


def _qkv_w_map(t):
    return t // 9, t % 9


def _hin_map(t):
    return t // 2, t % 2


def _chip_map(t):
    return t, 0


def _pack_rows(parts):
    return jnp.concatenate([p.reshape(-1, 128) for p in parts], axis=0)


def kernel(x, c, rel_bias, ada_w, ada_b, norm_mix, norm_ffn, attn_w_qkv, attn_w_out, attn_q_gain, attn_k_gain, hgrn_w_in, hgrn_w_out, hgrn_gnorm, hgrn_lower_bounds, ffn_w1, ffn_w3, ffn_w2, loss_target, m_rel_bias, m_ada_w, m_ada_b, m_norm_mix, m_norm_ffn, m_attn_w_qkv, m_attn_w_out, m_attn_q_gain, m_attn_k_gain, m_hgrn_w_in, m_hgrn_w_out, m_hgrn_gnorm, m_hgrn_lower_bounds, m_ffn_w1, m_ffn_w3, m_ffn_w2, v_rel_bias, v_ada_w, v_ada_b, v_norm_mix, v_norm_ffn, v_attn_w_qkv, v_attn_w_out, v_attn_q_gain, v_attn_k_gain, v_hgrn_w_in, v_hgrn_w_out, v_hgrn_gnorm, v_hgrn_lower_bounds, v_ffn_w1, v_ffn_w3, v_ffn_w2):
    weights = dict(rel_bias=rel_bias, ada_w=ada_w, ada_b=ada_b, norm_mix=norm_mix, norm_ffn=norm_ffn,
                   attn_w_qkv=attn_w_qkv, attn_w_out=attn_w_out, attn_q_gain=attn_q_gain, attn_k_gain=attn_k_gain,
                   hgrn_w_in=hgrn_w_in, hgrn_w_out=hgrn_w_out, hgrn_gnorm=hgrn_gnorm,
                   hgrn_lower_bounds=hgrn_lower_bounds, ffn_w1=ffn_w1, ffn_w3=ffn_w3, ffn_w2=ffn_w2)
    mom1 = dict(rel_bias=m_rel_bias, ada_w=m_ada_w, ada_b=m_ada_b, norm_mix=m_norm_mix, norm_ffn=m_norm_ffn,
                attn_w_qkv=m_attn_w_qkv, attn_w_out=m_attn_w_out, attn_q_gain=m_attn_q_gain,
                attn_k_gain=m_attn_k_gain, hgrn_w_in=m_hgrn_w_in, hgrn_w_out=m_hgrn_w_out, hgrn_gnorm=m_hgrn_gnorm,
                hgrn_lower_bounds=m_hgrn_lower_bounds, ffn_w1=m_ffn_w1, ffn_w3=m_ffn_w3, ffn_w2=m_ffn_w2)
    mom2 = dict(rel_bias=v_rel_bias, ada_w=v_ada_w, ada_b=v_ada_b, norm_mix=v_norm_mix, norm_ffn=v_norm_ffn,
                attn_w_qkv=v_attn_w_qkv, attn_w_out=v_attn_w_out, attn_q_gain=v_attn_q_gain,
                attn_k_gain=v_attn_k_gain, hgrn_w_in=v_hgrn_w_in, hgrn_w_out=v_hgrn_w_out, hgrn_gnorm=v_hgrn_gnorm,
                hgrn_lower_bounds=v_hgrn_lower_bounds, ffn_w1=v_ffn_w1, ffn_w3=v_ffn_w3, ffn_w2=v_ffn_w2)

    xi, yi, ci = _position()
    chip = 2 * xi + yi
    dev = 4 * xi + 2 * yi + ci
    place = jnp.stack([ci, chip]).astype(jnp.int32)
    d = D_MODEL

    big_names = ("attn_w_qkv", "attn_w_out", "hgrn_w_in", "hgrn_w_out", "ffn_w1", "ffn_w3", "ffn_w2")
    slabs16 = [_cast_bf16(place, weights[k], "cast_" + k) for k in big_names]
    wg = dict(zip(big_names, _gather_weights(slabs16, "gather_weights")))

    c_all = _small_allgather(c.reshape(8, 128), "gather_c").reshape(N_DEV, d)
    ada_b_cols = lax.dynamic_slice(ada_b, (0, chip * ADA_SHARD), (DEPTH, ADA_SHARD)).reshape(DEPTH, 1, ADA_SHARD)
    mod_shard = _ada_fwd(c_all, ada_w, ada_b_cols, "ada_fwd")
    mod_all = _small_allgather(mod_shard.reshape(-1, 128), "gather_mod").reshape(N_DEV, DEPTH, N_DEV, ADA_SHARD)
    mod_mine = lax.dynamic_index_in_dim(mod_all[0::2], dev, axis=2, keepdims=False)
    mod = jnp.transpose(mod_mine, (1, 0, 2)).reshape(DEPTH, 6 * d)

    def mods(layer):
        return [mod[layer:layer + 1, j * d:(j + 1) * d] for j in range(6)]

    x0 = x.reshape(SEQ, d)
    target = loss_target.reshape(SEQ, d)
    qg = attn_q_gain.reshape(len(GROUPS), 1, HEAD_DIM)
    kg = attn_k_gain.reshape(len(GROUPS), 1, HEAD_DIM)
    bias = _attn_bias(rel_bias, "attn_bias")
    lb1 = _lower_bounds(hgrn_lower_bounds, "lower_bounds")[1:2]

    def ffn_fwd(layer, x_in, sc2, sh2, g2):
        hf = _norm_mod(x_in, norm_ffn[layer:layer + 1], sc2, sh2, f"l{layer}_norm_ffn")
        a1, a3, u = _ffn_up(hf, wg["ffn_w1"], wg["ffn_w3"], layer, f"l{layer}_ffn_up")
        z, x_out = _mm_rows(u, wg["ffn_w2"], layer, x_in, g2, f"l{layer}_ffn_down")
        return x_out, (hf, a1, a3, u, z)

    def ffn_bwd(layer, dx_out, x_in, sc2, sh2, g2, saved):
        hf, a1, a3, u, z = saved
        dz, dg2 = _gate_bwd(dx_out, z, g2, f"l{layer}_ffn_gate_bwd")
        da1, da3 = _ffn_down_bwd(dz, wg["ffn_w2"], layer, a1, a3, f"l{layer}_ffn_down_bwd")
        dw2 = _mm_rows_bwd_w(u, dz, f"l{layer}_dw2")
        dh = _mm_cols_bwd_a([(da1, wg["ffn_w1"], layer), (da3, wg["ffn_w3"], layer)], tn=FFN_SHARD,
                            act_map=_chip_map, w_map=_chip_map, n_tiles=N_CHIP, name=f"l{layer}_ffn_up_bwd")
        dw1 = _mm_cols_bwd_w(hf, da1, ns=FFN_SHARD, tn=FFN_SHARD, act_map=_chip_map, w_map=_chip_map,
                             n_tiles=N_CHIP, name=f"l{layer}_dw1")
        dw3 = _mm_cols_bwd_w(hf, da3, ns=FFN_SHARD, tn=FFN_SHARD, act_map=_chip_map, w_map=_chip_map,
                             n_tiles=N_CHIP, name=f"l{layer}_dw3")
        dx_in, dsc2, dsh2, dnf = _norm_mod_bwd(x_in, norm_ffn[layer:layer + 1], sc2, sh2, dh, dx_out,
                                               f"l{layer}_norm_ffn_bwd")
        return dx_in, (dw1, dw3, dw2), (dsh2, dsc2, dg2), dnf

    sh1_0, sc1_0, g1_0, sh2_0, sc2_0, g2_0 = mods(0)
    h0 = _norm_mod(x0, norm_mix[0:1], sc1_0, sh1_0, "l0_norm_mix")
    qkv9 = _mm_cols(h0, wg["attn_w_qkv"], 0, n_blocks=9, width=d, tn=256, act_map=_qkv_act_map, w_map=_qkv_w_map,
                    out_dtype=F32, name="l0_qkv", tm=2048)
    o4, lse = _attn_fwd(qkv9, qg, kg, bias, "l0_attn")
    y0, x1 = _mm_rows(o4, wg["attn_w_out"], 0, x0, g1_0, "l0_attn_out")
    x2, ffn0 = ffn_fwd(0, x1, sc2_0, sh2_0, g2_0)

    sh1_1, sc1_1, g1_1, sh2_1, sc2_1, g2_1 = mods(1)
    h1 = _norm_mod(x2, norm_mix[1:2], sc1_1, sh1_1, "l1_norm_mix")
    proj4 = _mm_cols(h1, wg["hgrn_w_in"], 0, n_blocks=4, width=d, tn=512, act_map=_hin_map, w_map=_hin_map,
                     out_dtype=F32, name="l1_hgrn_in")
    o_raw, yg4, states = _hgrn_fwd(proj4, lb1, hgrn_gnorm, "l1_hgrn")
    y1, x3 = _mm_rows(yg4, wg["hgrn_w_out"], 0, x2, g1_1, "l1_hgrn_out")
    x4, ffn1 = ffn_fwd(1, x3, sc2_1, sh2_1, g2_1)

    dx4, loss_part = _loss_head(x4, target, "loss_head")
    loss = lax.psum(loss_part[0, 0], ("x", "y", "c"))

    dx3, (dw1_1, dw3_1, dw2_1), dmod2_1, dnf_1 = ffn_bwd(1, dx4, x3, sc2_1, sh2_1, g2_1, ffn1)
    dzm1, dg1_1 = _gate_bwd(dx3, y1, g1_1, "l1_mix_gate_bwd")
    dyg4 = _mm_rows_bwd_a(dzm1, wg["hgrn_w_out"], 0, "l1_hgrn_out_bwd")
    dw_hout = _mm_rows_bwd_w(yg4, dzm1, "l1_dw_hgrn_out")
    dproj4, dlb_h, dgn_h = _hgrn_bwd(proj4, lb1, hgrn_gnorm, o_raw, dyg4, states, "l1_hgrn_bwd")
    dh1 = _mm_cols_bwd_a([(dproj4, wg["hgrn_w_in"], 0)], tn=512, act_map=_hin_map, w_map=_hin_map, n_tiles=8,
                         name="l1_hgrn_in_bwd")
    dw_hin = _mm_cols_bwd_w(h1, dproj4, ns=d, tn=512, act_map=_hin_map, w_map=_hin_map, n_tiles=8,
                            name="l1_dw_hgrn_in")
    dx2, dsc1_1, dsh1_1, dnm_1 = _norm_mod_bwd(x2, norm_mix[1:2], sc1_1, sh1_1, dh1, dx3, "l1_norm_mix_bwd")

    dx1, (dw1_0, dw3_0, dw2_0), dmod2_0, dnf_0 = ffn_bwd(0, dx2, x1, sc2_0, sh2_0, g2_0, ffn0)
    dzm0, dg1_0 = _gate_bwd(dx1, y0, g1_0, "l0_mix_gate_bwd")
    do4 = _mm_rows_bwd_a(dzm0, wg["attn_w_out"], 0, "l0_attn_out_bwd")
    dw_aout = _mm_rows_bwd_w(o4, dzm0, "l0_dw_attn_out")
    dqkv, dqg_h, dkg_h, dbias = _attn_bwd(qkv9, qg, kg, bias, do4, o4, lse, "l0_attn_bwd")
    dqkv9 = dqkv.reshape(9, SEQ, d)
    dh0 = _mm_cols_bwd_a([(dqkv9, wg["attn_w_qkv"], 0)], tn=256, act_map=_qkv_act_map, w_map=_qkv_w_map,
                         n_tiles=36, name="l0_qkv_bwd")
    dw_qkv = _mm_cols_bwd_w(h0, dqkv9, ns=2304, tn=256, act_map=_qkv_act_map, w_map=_qkv_w_map, n_tiles=36,
                            name="l0_dw_qkv")
    dx0, dsc1_0, dsh1_0, dnm_0 = _norm_mod_bwd(x0, norm_mix[0:1], sc1_0, sh1_0, dh0, dx1, "l0_norm_mix_bwd")
    drb8 = _relbias_bwd(dbias, jnp.asarray(_bias_tables()), "rel_bias_bwd")

    small = _pack_rows([
        dsh1_0, dsc1_0, dg1_0, *dmod2_0, dsh1_1, dsc1_1, dg1_1, *dmod2_1,
        dnm_0, dnm_1, dnf_0, dnf_1,
        jnp.transpose(dqg_h, (1, 0, 2, 3)), jnp.transpose(dkg_h, (1, 0, 2, 3)), dgn_h, dlb_h, drb8])
    small_all = _small_allgather(small, "gather_small")
    main, gains, dlbnd, rbt = _small_totals(small_all, hgrn_lower_bounds.reshape(DEPTH, 8, 128), "small_totals")
    ng = len(GROUPS)
    grads = {
        "ada_b": main[_R_DMOD:_R_NMIX].reshape(DEPTH, 6 * d),
        "norm_mix": main[_R_NMIX:_R_NFFN].reshape(DEPTH, d),
        "norm_ffn": main[_R_NFFN:_R_QG].reshape(DEPTH, d),
        "attn_q_gain": gains[0:ng].reshape(1, ng, HEAD_DIM),
        "attn_k_gain": gains[ng:2 * ng].reshape(1, ng, HEAD_DIM),
        "hgrn_gnorm": gains[2 * ng:2 * ng + 1],
        "hgrn_lower_bounds": dlbnd.reshape(DEPTH, d),
        "rel_bias": jnp.transpose(rbt[:, :ng * NUM_BUCKETS].reshape(HEADS, ng, NUM_BUCKETS), (2, 1, 0))
                       .reshape(NUM_BUCKETS, ng * HEADS),
    }
    dmod_all = small_all[:, _R_DMOD:_R_NMIX].reshape(N_DEV, DEPTH, 6 * d)
    dmod_cols = jnp.transpose(lax.dynamic_slice(dmod_all, (0, 0, chip * ADA_SHARD), (N_DEV, DEPTH, ADA_SHARD)),
                              (1, 0, 2))
    grad_ada_w = _ada_bwd(c_all, dmod_cols, "ada_bwd")

    layer_grads = {"attn_w_qkv": [dw_qkv], "attn_w_out": [dw_aout], "hgrn_w_in": [dw_hin], "hgrn_w_out": [dw_hout],
                   "ffn_w1": [dw1_0, dw1_1], "ffn_w3": [dw3_0, dw3_1], "ffn_w2": [dw2_0, dw2_1]}
    tags = [(k, layer) for k in big_names for layer in range(len(layer_grads[k]))]
    flat = [layer_grads[k][layer] for k, layer in tags]
    recv = _rs_exchange_halves(flat, "rs_exchange_halves")
    parts = [_rs_add_cast(place, g, r, f"rs_add_{k}_{layer}") for (k, layer), g, r in zip(tags, flat, recv)]
    got = _rs_exchange_chips(parts, "rs_exchange_chips")
    halves = [_rs_sum4(place, p, r, f"rs_sum_{k}_{layer}") for (k, layer), p, r in zip(tags, parts, got)]
    full = dict(zip(tags, _rs_join_halves(halves, "rs_join_halves")))

    out_g, out_d, out_m, out_v = {}, {}, {}, {}
    for k in big_names:
        gs = [full[(k, layer)] for layer in range(len(layer_grads[k]))]
        out_g[k], out_d[k], out_m[k], out_v[k] = _adamw(weights[k], gs, mom1[k], mom2[k], "adamw_" + k)
    shp = (1, DEPTH * d, ADA_SHARD)
    res = _adamw(ada_w.reshape(shp), [grad_ada_w.reshape(shp[1:])], m_ada_w.reshape(shp), v_ada_w.reshape(shp),
                 "adamw_ada_w")
    out_g["ada_w"], out_d["ada_w"], out_m["ada_w"], out_v["ada_w"] = [r.reshape(ada_w.shape) for r in res]
    packed = [_pack_rows([src[k] for k in _SMALL_ORDER])[None] for src in (weights, grads, mom1, mom2)]
    res = _adamw(packed[0], [packed[1][0]], packed[2], packed[3], "adamw_small")
    offset = 0
    for k in _SMALL_ORDER:
        size = weights[k].size
        for dst, r in zip((out_g, out_d, out_m, out_v), res):
            dst[k] = r.reshape(-1)[offset:offset + size].reshape(weights[k].shape)
        offset += size

    return (loss, dx0.reshape(x.shape), *[out_g[k] for k in _WEIGHT_ORDER], *[out_d[k] for k in _WEIGHT_ORDER],
            *[out_m[k] for k in _WEIGHT_ORDER], *[out_v[k] for k in _WEIGHT_ORDER])
```

```python
import functools

import numpy as np
import jax
import jax.numpy as jnp
from jax import lax
from jax.experimental import pallas as pl
from jax.experimental.pallas import tpu as pltpu

F32 = jnp.float32
BF16 = jnp.bfloat16

D_MODEL = 1024
SEQ = 4096
N_DEV = 8
N_CHIP = 4
DEPTH = 2
HEADS = 8
HEAD_DIM = 128
GROUPS = ((128, 1), (512, 4), (2048, 16))
ATT_BLK = 128
ATT_PAIR = SEQ // ATT_BLK // 2
NUM_BUCKETS = 32
MAX_DISTANCE = 2048
FFN_HIDDEN = 2816
FFN_SHARD = FFN_HIDDEN // N_CHIP
HG_SUB = 16
HG_TC = 512
HG_HP = 2
RMS_EPS = 1e-6
NEG = -1e30
ATT_SCALE = HEAD_DIM ** -0.5
ADAM_LR, ADAM_B1, ADAM_B2, ADAM_EPS, ADAM_WD, ADAM_STEP = 0.001, 0.9, 0.999, 1e-08, 0.01, 10
VMEM_LIMIT = 56 * 1024 * 1024
MESH = pl.DeviceIdType.MESH


def _pcall(body, **kw):
    return pl.pallas_call(body, **kw)


def _cparams(sem=None):
    if sem is None:
        return pltpu.CompilerParams(vmem_limit_bytes=VMEM_LIMIT)
    return pltpu.CompilerParams(dimension_semantics=sem, vmem_limit_bytes=VMEM_LIMIT)


def _sds(shape, dtype):
    return jax.ShapeDtypeStruct(shape, dtype)


def _dot(a, b):
    return jnp.dot(a, b, preferred_element_type=F32)


def _dot_nt(a, b):
    return lax.dot_general(a, b, (((1,), (1,)), ((), ())), preferred_element_type=F32)


def _dot_tn(a, b):
    return lax.dot_general(a, b, (((0,), (0,)), ((), ())), preferred_element_type=F32)


def _sigmoid(x):
    return 1.0 / (1.0 + jnp.exp(-x))


def _silu(x):
    return x * _sigmoid(x)


def _dsilu(x):
    s = _sigmoid(x)
    return s * (1.0 + x * (1.0 - s))


def _norm_mod(x, gain, sc, sh, name):
    tm = 512

    def body(x_ref, g_ref, sc_ref, sh_ref, h_ref):
        xv = x_ref[...]
        rs = lax.rsqrt(jnp.mean(xv * xv, axis=-1, keepdims=True) + RMS_EPS)
        h_ref[...] = ((xv * rs * g_ref[...]) * (1.0 + sc_ref[...]) + sh_ref[...]).astype(BF16)

    vec = pl.BlockSpec((1, D_MODEL), lambda i: (0, 0))
    return _pcall(
        body, name=name, grid=(SEQ // tm,),
        in_specs=[pl.BlockSpec((tm, D_MODEL), lambda i: (i, 0)), vec, vec, vec],
        out_specs=pl.BlockSpec((tm, D_MODEL), lambda i: (i, 0)),
        out_shape=_sds((SEQ, D_MODEL), BF16),
        compiler_params=_cparams(("parallel",)),
    )(x, gain, sc, sh)


def _norm_mod_bwd(x, gain, sc, sh, dh, dres, name):
    tm = 512

    def body(x_ref, g_ref, sc_ref, sh_ref, dh_ref, dres_ref, dx_ref, dsc_ref, dsh_ref, dg_ref):
        @pl.when(pl.program_id(0) == 0)
        def _():
            dsc_ref[...] = jnp.zeros_like(dsc_ref)
            dsh_ref[...] = jnp.zeros_like(dsh_ref)
            dg_ref[...] = jnp.zeros_like(dg_ref)

        xv = x_ref[...]
        dhv = dh_ref[...]
        rs = lax.rsqrt(jnp.mean(xv * xv, axis=-1, keepdims=True) + RMS_EPS)
        xh = xv * rs
        dsc_ref[...] += jnp.sum(dhv * (xh * g_ref[...]), axis=0, keepdims=True)
        dsh_ref[...] += jnp.sum(dhv, axis=0, keepdims=True)
        dhn = dhv * (1.0 + sc_ref[...])
        dg_ref[...] += jnp.sum(dhn * xh, axis=0, keepdims=True)
        dxh = dhn * g_ref[...]
        dx_ref[...] = dres_ref[...] + rs * (dxh - xh * jnp.mean(dxh * xh, axis=-1, keepdims=True))

    vec = pl.BlockSpec((1, D_MODEL), lambda i: (0, 0))
    big = pl.BlockSpec((tm, D_MODEL), lambda i: (i, 0))
    return _pcall(
        body, name=name, grid=(SEQ // tm,),
        in_specs=[big, vec, vec, vec, big, big],
        out_specs=[big, vec, vec, vec],
        out_shape=[_sds((SEQ, D_MODEL), F32)] + [_sds((1, D_MODEL), F32)] * 3,
        compiler_params=_cparams(("arbitrary",)),
    )(x, gain, sc, sh, dh, dres)


def _mm_cols(a, wg, layer, *, n_blocks, width, tn, act_map, w_map, out_dtype, name, tm=1024):
    k = a.shape[1]
    n_tiles = n_blocks * width // tn

    def body(a_ref, w_ref, o_ref):
        o_ref[...] = _dot(a_ref[...], w_ref[...]).astype(o_ref.dtype)

    return _pcall(
        body, name=name, grid=(SEQ // tm, n_tiles),
        in_specs=[pl.BlockSpec((tm, k), lambda i, t: (i, 0)),
                  pl.BlockSpec((None, None, k, tn), lambda i, t: (w_map(t)[0], layer, 0, w_map(t)[1]))],
        out_specs=pl.BlockSpec((None, tm, tn), lambda i, t: (act_map(t)[0], i, act_map(t)[1])),
        out_shape=_sds((n_blocks, SEQ, width), out_dtype),
        compiler_params=_cparams(("parallel", "arbitrary")),
    )(a, wg)


def _mm_cols_bwd_a(pairs, *, tn, act_map, w_map, n_tiles, name, tm=512):
    k = pairs[0][1].shape[2]
    n_p = len(pairs)

    def body(*refs):
        o_ref = refs[-1]

        @pl.when(pl.program_id(1) == 0)
        def _():
            o_ref[...] = jnp.zeros_like(o_ref)

        acc = _dot_nt(refs[0][...], refs[1][...])
        for p in range(1, n_p):
            acc += _dot_nt(refs[2 * p][...], refs[2 * p + 1][...])
        o_ref[...] += acc

    in_specs, args = [], []
    for dout, wg, layer in pairs:
        in_specs.append(pl.BlockSpec((None, tm, tn), lambda i, t: (act_map(t)[0], i, act_map(t)[1])))
        in_specs.append(pl.BlockSpec((None, None, k, tn),
                                     lambda i, t, layer=layer: (w_map(t)[0], layer, 0, w_map(t)[1])))
        args += [dout, wg]
    return _pcall(
        body, name=name, grid=(SEQ // tm, n_tiles),
        in_specs=in_specs,
        out_specs=pl.BlockSpec((tm, k), lambda i, t: (i, 0)),
        out_shape=_sds((SEQ, k), F32),
        compiler_params=_cparams(("parallel", "arbitrary")),
    )(*args)


def _mm_cols_bwd_w(a, dout, *, ns, tn, act_map, w_map, n_tiles, name, tm=1024, n_out=N_CHIP):
    k = a.shape[1]

    def body(a_ref, d_ref, o_ref):
        @pl.when(pl.program_id(1) == 0)
        def _():
            o_ref[...] = jnp.zeros_like(o_ref)

        o_ref[...] += _dot_tn(a_ref[...], d_ref[...])

    return _pcall(
        body, name=name, grid=(n_tiles, SEQ // tm),
        in_specs=[pl.BlockSpec((tm, k), lambda t, i: (i, 0)),
                  pl.BlockSpec((None, tm, tn), lambda t, i: (act_map(t)[0], i, act_map(t)[1]))],
        out_specs=pl.BlockSpec((None, k, tn), lambda t, i: (w_map(t)[0], 0, w_map(t)[1])),
        out_shape=_sds((n_out, k, ns), F32),
        compiler_params=_cparams(("parallel", "arbitrary")),
    )(a, dout)


def _retile_cols(src, *, n_out, width_out, tn, src_map, dst_map, n_tiles, name):
    k = src.shape[1]

    def body(s_ref, o_ref):
        o_ref[...] = s_ref[...]

    return _pcall(
        body, name=name, grid=(n_tiles,),
        in_specs=[pl.BlockSpec((None, k, tn), lambda t: (src_map(t)[0], 0, src_map(t)[1]))],
        out_specs=pl.BlockSpec((None, k, tn), lambda t: (dst_map(t)[0], 0, dst_map(t)[1])),
        out_shape=_sds((n_out, k, width_out), src.dtype),
        compiler_params=_cparams(("parallel",)),
    )(src)


def _mm_rows(a4, wg, layer, x, gate, name, tm=512):
    ks = a4.shape[2]
    n = wg.shape[3]

    def body(a_ref, w_ref, x_ref, g_ref, z_ref, xn_ref):
        s = pl.program_id(1)

        @pl.when(s == 0)
        def _():
            z_ref[...] = jnp.zeros_like(z_ref)

        z_ref[...] += _dot(a_ref[...], w_ref[...])

        @pl.when(s == N_CHIP - 1)
        def _():
            xn_ref[...] = x_ref[...] + g_ref[...] * z_ref[...]

    big = pl.BlockSpec((tm, n), lambda i, s: (i, 0))
    return _pcall(
        body, name=name, grid=(SEQ // tm, N_CHIP),
        in_specs=[pl.BlockSpec((None, tm, ks), lambda i, s: (s, i, 0)),
                  pl.BlockSpec((None, None, ks, n), lambda i, s: (s, layer, 0, 0)),
                  big, pl.BlockSpec((1, n), lambda i, s: (0, 0))],
        out_specs=[big, big],
        out_shape=[_sds((SEQ, n), F32), _sds((SEQ, n), F32)],
        compiler_params=_cparams(("parallel", "arbitrary")),
    )(a4, wg, x, gate)


def _gate_bwd(dx, z, gate, name):
    tm = 512

    def body(dx_ref, z_ref, g_ref, dz_ref, dg_ref):
        @pl.when(pl.program_id(0) == 0)
        def _():
            dg_ref[...] = jnp.zeros_like(dg_ref)

        dxv = dx_ref[...]
        dz_ref[...] = (dxv * g_ref[...]).astype(BF16)
        dg_ref[...] += jnp.sum(dxv * z_ref[...], axis=0, keepdims=True)

    big = pl.BlockSpec((tm, D_MODEL), lambda i: (i, 0))
    vec = pl.BlockSpec((1, D_MODEL), lambda i: (0, 0))
    return _pcall(
        body, name=name, grid=(SEQ // tm,),
        in_specs=[big, big, vec], out_specs=[big, vec],
        out_shape=[_sds((SEQ, D_MODEL), BF16), _sds((1, D_MODEL), F32)],
        compiler_params=_cparams(("arbitrary",)),
    )(dx, z, gate)


def _mm_rows_bwd_a(dz, wg, layer, name, tm=512):
    ks, n = wg.shape[2], wg.shape[3]

    def body(dz_ref, w_ref, o_ref):
        o_ref[...] = _dot_nt(dz_ref[...], w_ref[...])

    return _pcall(
        body, name=name, grid=(SEQ // tm, N_CHIP),
        in_specs=[pl.BlockSpec((tm, n), lambda i, s: (i, 0)),
                  pl.BlockSpec((None, None, ks, n), lambda i, s: (s, layer, 0, 0))],
        out_specs=pl.BlockSpec((None, tm, ks), lambda i, s: (s, i, 0)),
        out_shape=_sds((N_CHIP, SEQ, ks), F32),
        compiler_params=_cparams(("parallel", "arbitrary")),
    )(dz, wg)


def _mm_rows_bwd_w(a4, dz, name, tm=1024):
    ks = a4.shape[2]
    n = dz.shape[1]

    def body(a_ref, dz_ref, o_ref):
        @pl.when(pl.program_id(1) == 0)
        def _():
            o_ref[...] = jnp.zeros_like(o_ref)

        o_ref[...] += _dot_tn(a_ref[...], dz_ref[...])

    return _pcall(
        body, name=name, grid=(N_CHIP, SEQ // tm),
        in_specs=[pl.BlockSpec((None, tm, ks), lambda s, i: (s, i, 0)),
                  pl.BlockSpec((tm, n), lambda s, i: (i, 0))],
        out_specs=pl.BlockSpec((None, ks, n), lambda s, i: (s, 0, 0)),
        out_shape=_sds((N_CHIP, ks, n), F32),
        compiler_params=_cparams(("parallel", "arbitrary")),
    )(a4, dz)


def _ffn_up(h, w1g, w3g, layer, name, tm=512):
    def body(h_ref, w1_ref, w3_ref, a1_ref, a3_ref, u_ref):
        hv = h_ref[...]
        a1 = _dot(hv, w1_ref[...])
        a3 = _dot(hv, w3_ref[...])
        a1_ref[...] = a1
        a3_ref[...] = a3
        u_ref[...] = (_silu(a1) * a3).astype(BF16)

    wspec = pl.BlockSpec((None, None, D_MODEL, FFN_SHARD), lambda i, s: (s, layer, 0, 0))
    ospec = pl.BlockSpec((None, tm, FFN_SHARD), lambda i, s: (s, i, 0))
    shp = (N_CHIP, SEQ, FFN_SHARD)
    return _pcall(
        body, name=name, grid=(SEQ // tm, N_CHIP),
        in_specs=[pl.BlockSpec((tm, D_MODEL), lambda i, s: (i, 0)), wspec, wspec],
        out_specs=[ospec, ospec, ospec],
        out_shape=[_sds(shp, F32), _sds(shp, F32), _sds(shp, BF16)],
        compiler_params=_cparams(("parallel", "arbitrary")),
    )(h, w1g, w3g)


def _ffn_down_bwd(dz, w2g, layer, a1, a3, name, tm=512):
    def body(dz_ref, w_ref, a1_ref, a3_ref, da1_ref, da3_ref):
        du = _dot_nt(dz_ref[...], w_ref[...])
        a1 = a1_ref[...]
        da1_ref[...] = (du * a3_ref[...] * _dsilu(a1)).astype(BF16)
        da3_ref[...] = (du * _silu(a1)).astype(BF16)

    blk = pl.BlockSpec((None, tm, FFN_SHARD), lambda i, s: (s, i, 0))
    shp = (N_CHIP, SEQ, FFN_SHARD)
    return _pcall(
        body, name=name, grid=(SEQ // tm, N_CHIP),
        in_specs=[pl.BlockSpec((tm, D_MODEL), lambda i, s: (i, 0)),
                  pl.BlockSpec((None, None, FFN_SHARD, D_MODEL), lambda i, s: (s, layer, 0, 0)),
                  blk, blk],
        out_specs=[blk, blk],
        out_shape=[_sds(shp, BF16), _sds(shp, BF16)],
        compiler_params=_cparams(("parallel", "arbitrary")),
    )(dz, w2g, a1, a3)


def _loss_head(y, target, name):
    tm = 512

    def body(y_ref, t_ref, dy_ref, l_ref, acc_ref):
        @pl.when(pl.program_id(0) == 0)
        def _():
            acc_ref[...] = jnp.zeros_like(acc_ref)

        err = y_ref[...] - t_ref[...]
        dy_ref[...] = err * (1.0 / D_MODEL)
        acc_ref[...] += jnp.sum(jnp.mean(err * err, axis=-1, keepdims=True), axis=0, keepdims=True)

        @pl.when(pl.program_id(0) == pl.num_programs(0) - 1)
        def _():
            l_ref[...] = 0.5 * acc_ref[...]

    big = pl.BlockSpec((tm, D_MODEL), lambda i: (i, 0))
    return _pcall(
        body, name=name, grid=(SEQ // tm,),
        in_specs=[big, big],
        out_specs=[big, pl.BlockSpec((1, 1), lambda i: (0, 0))],
        out_shape=[_sds((SEQ, D_MODEL), F32), _sds((1, 1), F32)],
        scratch_shapes=[pltpu.VMEM((1, 1), F32)],
        compiler_params=_cparams(("arbitrary",)),
    )(y, target)


def _attn_rows(base, d):
    if d == 1:
        return pl.ds(pl.multiple_of(base, ATT_BLK), ATT_BLK)
    return pl.ds(base, ATT_BLK, stride=d)


def _attn_block_index(i, d):
    nb = SEQ // (ATT_BLK * d)
    r = i // nb
    n = i % nb
    base = r + n * (ATT_BLK * d)
    pbase = jnp.maximum(base - ATT_BLK * d, r)
    return n, _attn_rows(base, d), _attn_rows(pbase, d)


def _qk_normed(x, gain):
    rs = lax.rsqrt(jnp.mean(x * x, axis=-1, keepdims=True) + RMS_EPS)
    return x * rs, rs


def _attn_fwd(qkv9, qgain, kgain, bias, name):
    def body(q_ref, k_ref, v_ref, qg_ref, kg_ref, b_ref, o_ref, lse_ref, qn_s, kn_s, acc_s, m_s, l_s):
        g = pl.program_id(1)

        @pl.when(g == 0)
        def _():
            m_s[...] = jnp.full_like(m_s, NEG)
            l_s[...] = jnp.zeros_like(l_s)
            acc_s[...] = jnp.zeros_like(acc_s)

        qn_s[...] = _qk_normed(q_ref[...], None)[0] * qg_ref[...]
        kn_s[...] = _qk_normed(k_ref[...], None)[0] * kg_ref[...]

        for gi, (_, d) in enumerate(GROUPS):
            @pl.when(g == gi)
            def _(d=d):
                def block(n, qb, kc, kp, vc, vp, m_old, l_old, acc_old):
                    sc = _dot_nt(qb, kc) * ATT_SCALE + b_ref[1]
                    sp = _dot_nt(qb, kp) * ATT_SCALE + jnp.where(n > 0, b_ref[0], NEG)
                    m_new = jnp.maximum(m_old, jnp.maximum(jnp.max(sc, axis=-1, keepdims=True),
                                                           jnp.max(sp, axis=-1, keepdims=True)))
                    alpha = jnp.exp(m_old - m_new)
                    pc = jnp.exp(sc - m_new)
                    pp = jnp.exp(sp - m_new)
                    l_new = alpha * l_old + jnp.sum(pc, axis=-1, keepdims=True) + jnp.sum(pp, axis=-1, keepdims=True)
                    acc_new = alpha * acc_old + _dot(pc.astype(BF16), vc) + _dot(pp.astype(BF16), vp)
                    return m_new, l_new, acc_new

                def it(i, carry):
                    where, loaded = [], []
                    for off in (0, ATT_PAIR):
                        n, rows, prow = _attn_block_index(i + off, d)
                        where.append(rows)
                        loaded.append((n, qn_s[rows, :].astype(BF16), kn_s[rows, :].astype(BF16),
                                       kn_s[prow, :].astype(BF16), v_ref[rows, :].astype(BF16),
                                       v_ref[prow, :].astype(BF16), m_s[rows, :], l_s[rows, :], acc_s[rows, :]))
                    results = [block(*vals) for vals in loaded]
                    for rows, (m_new, l_new, acc_new) in zip(where, results):
                        m_s[rows, :] = m_new
                        l_s[rows, :] = l_new
                        acc_s[rows, :] = acc_new
                    return carry

                lax.fori_loop(0, ATT_PAIR, it, 0)

        @pl.when(g == len(GROUPS) - 1)
        def _():
            o_ref[...] = (acc_s[...] / l_s[...]).astype(BF16)
            lse_ref[...] = m_s[...] + jnp.log(l_s[...])

    def col(j):
        return pl.BlockSpec((None, SEQ, HEAD_DIM), lambda h, g: (g * 3 + j, 0, h))

    gspec = pl.BlockSpec((None, 1, HEAD_DIM), lambda h, g: (g, 0, 0))
    return _pcall(
        body, name=name, grid=(HEADS, len(GROUPS)),
        in_specs=[col(0), col(1), col(2), gspec, gspec,
                  pl.BlockSpec((None, None, 2, ATT_BLK, ATT_BLK), lambda h, g: (g, h, 0, 0, 0))],
        out_specs=[pl.BlockSpec((None, SEQ, HEAD_DIM), lambda h, g: (h // 2, 0, h % 2)),
                   pl.BlockSpec((None, SEQ, 1), lambda h, g: (h, 0, 0))],
        out_shape=[_sds((N_CHIP, SEQ, 2 * HEAD_DIM), BF16), _sds((HEADS, SEQ, 1), F32)],
        scratch_shapes=[pltpu.VMEM((SEQ, HEAD_DIM), F32)] * 3 + [pltpu.VMEM((SEQ, 1), F32)] * 2,
        compiler_params=_cparams(("parallel", "arbitrary")),
    )(qkv9, qkv9, qkv9, qgain, kgain, bias)


def _attn_bwd(qkv9, qgain, kgain, bias, do4, o4, lse, name):
    def body(q_ref, k_ref, v_ref, qg_ref, kg_ref, b_ref, do_ref, o_ref, lse_ref,
             dqkv_ref, dqg_ref, dkg_ref, db_ref, qn_s, kn_s, dq_s, dk_s, dv_s, dl_s):
        g = pl.program_id(1)
        qh, rq = _qk_normed(q_ref[...], None)
        kh, rk = _qk_normed(k_ref[...], None)
        qn_s[...] = qh * qg_ref[...]
        kn_s[...] = kh * kg_ref[...]
        dl_s[...] = jnp.sum(do_ref[...] * o_ref[...].astype(F32), axis=-1, keepdims=True)
        dk_s[...] = jnp.zeros_like(dk_s)
        dv_s[...] = jnp.zeros_like(dv_s)
        db_ref[...] = jnp.zeros_like(db_ref)

        for gi, (_, d) in enumerate(GROUPS):
            @pl.when(g == gi)
            def _(d=d):
                def block(n, qb, kc, kp, vc, vp, dob, lse_b, dl):
                    sc = _dot_nt(qb, kc) * ATT_SCALE + b_ref[1]
                    sp = _dot_nt(qb, kp) * ATT_SCALE + jnp.where(n > 0, b_ref[0], NEG)
                    pc = jnp.exp(sc - lse_b)
                    pp = jnp.exp(sp - lse_b)
                    dsc = pc * (_dot_nt(dob, vc) - dl)
                    dsp = pp * (_dot_nt(dob, vp) - dl)
                    dsc16 = dsc.astype(BF16)
                    dsp16 = dsp.astype(BF16)
                    dq = (_dot(dsc16, kc) + _dot(dsp16, kp)) * ATT_SCALE
                    return (dsc, dsp, dq, _dot_tn(dsc16, qb) * ATT_SCALE, _dot_tn(dsp16, qb) * ATT_SCALE,
                            _dot_tn(pc.astype(BF16), dob), _dot_tn(pp.astype(BF16), dob))

                def it(i, carry):
                    where, loaded, old = [], [], []
                    for off in (0, ATT_PAIR):
                        n, rows, prow = _attn_block_index(i + off, d)
                        where.append((rows, prow))
                        loaded.append((n, qn_s[rows, :].astype(BF16), kn_s[rows, :].astype(BF16),
                                       kn_s[prow, :].astype(BF16), v_ref[rows, :].astype(BF16),
                                       v_ref[prow, :].astype(BF16), do_ref[rows, :].astype(BF16),
                                       lse_ref[rows, :], dl_s[rows, :]))
                        old.append((dk_s[rows, :], dk_s[prow, :], dv_s[rows, :], dv_s[prow, :]))
                    results = [block(*vals) for vals in loaded]
                    db_ref[1] += results[0][0] + results[1][0]
                    db_ref[0] += results[0][1] + results[1][1]
                    for (rows, prow), (dk_c, dk_p, dv_c, dv_p), (_, _, dq, dkc, dkp, dvc, dvp) in zip(where, old, results):
                        dq_s[rows, :] = dq
                        dk_s[prow, :] = dk_p + dkp
                        dv_s[prow, :] = dv_p + dvp
                        dk_s[rows, :] = dk_c + dkc
                        dv_s[rows, :] = dv_c + dvc
                    return carry

                lax.fori_loop(0, ATT_PAIR, it, 0)

        def norm_bwd(dn, xh, rs, gain):
            dgain = jnp.sum(dn * xh, axis=0, keepdims=True)
            dxh = dn * gain
            return rs * (dxh - xh * jnp.mean(dxh * xh, axis=-1, keepdims=True)), dgain

        dq, dqg = norm_bwd(dq_s[...], qh, rq, qg_ref[...])
        dk, dkg = norm_bwd(dk_s[...], kh, rk, kg_ref[...])
        dqkv_ref[0] = dq.astype(BF16)
        dqkv_ref[1] = dk.astype(BF16)
        dqkv_ref[2] = dv_s[...].astype(BF16)
        dqg_ref[...] = dqg
        dkg_ref[...] = dkg

    def col(j):
        return pl.BlockSpec((None, SEQ, HEAD_DIM), lambda h, g: (g * 3 + j, 0, h))

    gspec = pl.BlockSpec((None, 1, HEAD_DIM), lambda h, g: (g, 0, 0))
    bspec = pl.BlockSpec((None, None, 2, ATT_BLK, ATT_BLK), lambda h, g: (g, h, 0, 0, 0))
    hcol = pl.BlockSpec((None, SEQ, HEAD_DIM), lambda h, g: (h // 2, 0, h % 2))
    dgspec = pl.BlockSpec((None, None, 1, HEAD_DIM), lambda h, g: (h, g, 0, 0))
    ng = len(GROUPS)
    return _pcall(
        body, name=name, grid=(HEADS, ng),
        in_specs=[col(0), col(1), col(2), gspec, gspec, bspec, hcol, hcol,
                  pl.BlockSpec((None, SEQ, 1), lambda h, g: (h, 0, 0))],
        out_specs=[pl.BlockSpec((None, 3, SEQ, HEAD_DIM), lambda h, g: (g, 0, 0, h)), dgspec, dgspec, bspec],
        out_shape=[_sds((ng, 3, SEQ, D_MODEL), BF16), _sds((HEADS, ng, 1, HEAD_DIM), F32),
                   _sds((HEADS, ng, 1, HEAD_DIM), F32), _sds((ng, HEADS, 2, ATT_BLK, ATT_BLK), F32)],
        scratch_shapes=[pltpu.VMEM((SEQ, HEAD_DIM), F32)] * 5 + [pltpu.VMEM((SEQ, 1), F32)],
        compiler_params=_cparams(("parallel", "arbitrary")),
    )(qkv9, qkv9, qkv9, qgain, kgain, bias, do4, o4, lse)


def _relbias_bwd(dbias, bucket_idx, name):
    ng = len(GROUPS)

    def body(db_ref, idx_ref, o_ref):
        lane = lax.broadcasted_iota(jnp.int32, (HEADS, 128), 1)
        acc = jnp.zeros((HEADS, 128), F32)
        for g in range(ng):
            dbg = db_ref[g]
            idx = idx_ref[g]
            for b in range(NUM_BUCKETS):
                sel = jnp.where((idx == b)[None], dbg, 0.0)
                part = jnp.sum(jnp.sum(sel, axis=1), axis=1)
                val = jnp.sum(part, axis=-1, keepdims=True)
                acc = jnp.where(lane == g * NUM_BUCKETS + b, val, acc)
        o_ref[...] = acc

    return _pcall(body, name=name, out_shape=_sds((HEADS, 128), F32), compiler_params=_cparams())(dbias, bucket_idx)


def _scan16(x, reverse=False):
    row = lax.broadcasted_iota(jnp.int32, x.shape, 0)
    for sh in (1, 2, 4, 8):
        if reverse:
            x = x + jnp.where(row < HG_SUB - sh, pltpu.roll(x, HG_SUB - sh, 0), 0.0)
        else:
            x = x + jnp.where(row >= sh, pltpu.roll(x, sh, 0), 0.0)
    return x


def _hgrn_gates(qr, fr, lbv):
    q = _silu(qr)
    sig = _sigmoid(fr)
    fg = lbv + (1.0 - lbv) * sig
    lf = jnp.log(fg)
    gcum = _scan16(lf)
    glast = jnp.sum(lf, axis=0, keepdims=True)
    return q, sig, fg, 1.0 - fg, gcum, glast


def _hgrn_intra(q, k, gcum, tri):
    e = jnp.exp(jnp.where(tri, gcum[:, None, :] - gcum[None, :, :], NEG))
    a = jnp.sum(q[:, None, :] * k[None, :, :] * e, axis=-1, keepdims=True)
    return e, a


def _hgrn_fwd(proj4, lb, gain, name):
    nsub = HG_TC // HG_SUB
    wide = HG_HP * HEAD_DIM

    def body(p_ref, lb_ref, gn_ref, o_ref, y_ref, st_ref, state_s):
        @pl.when(pl.program_id(1) == 0)
        def _():
            state_s[...] = jnp.zeros_like(state_s)

        gnv = gn_ref[...]
        shp = (HG_SUB, HG_SUB, HEAD_DIM)
        tri = lax.broadcasted_iota(jnp.int32, shp, 0) >= lax.broadcasted_iota(jnp.int32, shp, 1)

        def head(qr, fr, vv, gr, lbv, st):
            q, _, _, k, gcum, glast = _hgrn_gates(qr, fr, lbv)
            _, a = _hgrn_intra(q, k, gcum, tri)
            o = jnp.sum(a * vv[None, :, :], axis=1) + _dot_nt((q * jnp.exp(gcum)).astype(BF16), st.astype(BF16))
            kg = k * jnp.exp(glast - gcum)
            st_new = st * jnp.exp(glast) + _dot_tn(vv.astype(BF16), kg.astype(BF16))
            rs = lax.rsqrt(jnp.mean(o * o, axis=-1, keepdims=True) + RMS_EPS)
            return o, (o * rs * gnv * _silu(gr)).astype(BF16), st_new

        def it(i, carry):
            rows = pl.ds(pl.multiple_of(i * HG_SUB, HG_SUB), HG_SUB)
            loaded = []
            for hh in range(HG_HP):
                lanes = pl.ds(hh * HEAD_DIM, HEAD_DIM)
                loaded.append(([p_ref[j, rows, lanes] for j in range(4)], lb_ref[:, lanes], state_s[hh]))
            results = [head(blk[0], blk[1], blk[2], blk[3], lbv, st) for blk, lbv, st in loaded]
            for hh, ((_, _, st), (o, y, st_new)) in enumerate(zip(loaded, results)):
                lanes = pl.ds(hh * HEAD_DIM, HEAD_DIM)
                st_ref[hh, i] = st.astype(BF16)
                state_s[hh] = st_new
                o_ref[rows, lanes] = o
                y_ref[rows, lanes] = y
            return carry

        lax.fori_loop(0, nsub, it, 0)

    return _pcall(
        body, name=name, grid=(HEADS // HG_HP, SEQ // HG_TC),
        in_specs=[pl.BlockSpec((4, HG_TC, wide), lambda h, j: (0, j, h)),
                  pl.BlockSpec((1, wide), lambda h, j: (0, h)),
                  pl.BlockSpec((1, HEAD_DIM), lambda h, j: (0, 0))],
        out_specs=[pl.BlockSpec((HG_TC, wide), lambda h, j: (j, h)),
                   pl.BlockSpec((None, HG_TC, wide), lambda h, j: (h, j, 0)),
                   pl.BlockSpec((HG_HP, nsub, HEAD_DIM, HEAD_DIM), lambda h, j: (h, j, 0, 0))],
        out_shape=[_sds((SEQ, D_MODEL), F32), _sds((N_CHIP, SEQ, 2 * HEAD_DIM), BF16),
                   _sds((HEADS, SEQ // HG_SUB, HEAD_DIM, HEAD_DIM), BF16)],
        scratch_shapes=[pltpu.VMEM((HG_HP, HEAD_DIM, HEAD_DIM), F32)],
        compiler_params=_cparams(("parallel", "arbitrary")),
    )(proj4, lb, gain)


def _hgrn_bwd(proj4, lb, gain, o_raw, dy4, states, name):
    nsub = HG_TC // HG_SUB
    nt = SEQ // HG_TC
    wide = HG_HP * HEAD_DIM

    def body(p_ref, lb_ref, gn_ref, o_ref, dy_ref, st_ref, dp_ref, dlb_ref, dgn_ref, dst_s):
        @pl.when(pl.program_id(1) == 0)
        def _():
            dst_s[...] = jnp.zeros_like(dst_s)
            dlb_ref[...] = jnp.zeros_like(dlb_ref)
            dgn_ref[...] = jnp.zeros_like(dgn_ref)

        gnv = gn_ref[...]
        shp = (HG_SUB, HG_SUB, HEAD_DIM)
        tri = lax.broadcasted_iota(jnp.int32, shp, 0) >= lax.broadcasted_iota(jnp.int32, shp, 1)

        def head(qr, fr, vv, gr, o, dy, lbv, st0, dst):
            q, sig, fg, k, gcum, glast = _hgrn_gates(qr, fr, lbv)
            rs = lax.rsqrt(jnp.mean(o * o, axis=-1, keepdims=True) + RMS_EPS)
            oh = o * rs
            don = dy * _silu(gr)
            dgn = jnp.sum(don * oh, axis=0, keepdims=True)
            dgr = dy * oh * gnv * _dsilu(gr)
            doh = don * gnv
            do = rs * (doh - oh * jnp.mean(doh * oh, axis=-1, keepdims=True))
            dst16 = dst.astype(BF16)
            do16 = do.astype(BF16)
            eg = jnp.exp(gcum)
            eb = jnp.exp(glast - gcum)
            e, a = _hgrn_intra(q, k, gcum, tri)
            da = jnp.sum(do[:, None, :] * vv[None, :, :], axis=-1, keepdims=True)
            dae = da * e
            dq = jnp.sum(dae * k[None, :, :], axis=1) + eg * _dot(do16, st0)
            dk_state = eb * _dot(vv.astype(BF16), dst16)
            dk = jnp.sum(dae * q[:, None, :], axis=0) + dk_state
            dv = jnp.sum(a * do[:, None, :], axis=0) + _dot_nt((k * eb).astype(BF16), dst16)
            eglast = jnp.exp(glast)
            dst_new = dst * eglast + _dot_tn(do16, (q * eg).astype(BF16))
            dglast = jnp.sum(k * dk_state, axis=0, keepdims=True) \
                + eglast * jnp.sum(dst * st0.astype(F32), axis=0, keepdims=True)
            dlf = _scan16(q * dq - k * dk, reverse=True) + dglast
            dfg = dlf / fg - dk
            dlb = jnp.sum(dfg * (1.0 - sig), axis=0, keepdims=True)
            dproj = ((dq * _dsilu(qr)).astype(BF16), (dfg * (1.0 - lbv) * sig * (1.0 - sig)).astype(BF16),
                     dv.astype(BF16), dgr.astype(BF16))
            return dproj, dst_new, dlb, dgn

        def it(ii, carry):
            i = nsub - 1 - ii
            rows = pl.ds(pl.multiple_of(i * HG_SUB, HG_SUB), HG_SUB)
            results = []
            for hh in range(HG_HP):
                lanes = pl.ds(hh * HEAD_DIM, HEAD_DIM)
                blk = [p_ref[j, rows, lanes] for j in range(4)]
                results.append(head(blk[0], blk[1], blk[2], blk[3], o_ref[rows, lanes], dy_ref[rows, lanes],
                                    lb_ref[:, lanes], st_ref[hh, i], dst_s[hh]))
            new_carry = []
            for hh, (dproj, dst_new, dlb, dgn) in enumerate(results):
                lanes = pl.ds(hh * HEAD_DIM, HEAD_DIM)
                dst_s[hh] = dst_new
                for j in range(4):
                    dp_ref[j, rows, lanes] = dproj[j]
                new_carry.append((carry[hh][0] + dlb, carry[hh][1] + dgn))
            return tuple(new_carry)

        zero = jnp.zeros((1, HEAD_DIM), F32)
        sums = lax.fori_loop(0, nsub, it, tuple((zero, zero) for _ in range(HG_HP)))
        for hh in range(HG_HP):
            dlb_ref[hh] += sums[hh][0]
            dgn_ref[hh] += sums[hh][1]

    vspec = pl.BlockSpec((HG_HP, 1, HEAD_DIM), lambda h, j: (h, 0, 0))
    return _pcall(
        body, name=name, grid=(HEADS // HG_HP, nt),
        in_specs=[pl.BlockSpec((4, HG_TC, wide), lambda h, j: (0, nt - 1 - j, h)),
                  pl.BlockSpec((1, wide), lambda h, j: (0, h)),
                  pl.BlockSpec((1, HEAD_DIM), lambda h, j: (0, 0)),
                  pl.BlockSpec((HG_TC, wide), lambda h, j: (nt - 1 - j, h)),
                  pl.BlockSpec((None, HG_TC, wide), lambda h, j: (h, nt - 1 - j, 0)),
                  pl.BlockSpec((HG_HP, nsub, HEAD_DIM, HEAD_DIM), lambda h, j: (h, nt - 1 - j, 0, 0))],
        out_specs=[pl.BlockSpec((4, HG_TC, wide), lambda h, j: (0, nt - 1 - j, h)), vspec, vspec],
        out_shape=[_sds((4, SEQ, D_MODEL), BF16), _sds((HEADS, 1, HEAD_DIM), F32), _sds((HEADS, 1, HEAD_DIM), F32)],
        scratch_shapes=[pltpu.VMEM((HG_HP, HEAD_DIM, HEAD_DIM), F32)],
        compiler_params=_cparams(("parallel", "arbitrary")),
    )(proj4, lb, gain, o_raw, dy4, states)


def _t5_bucket(dist):
    n = np.asarray(dist, dtype=np.int64)
    max_exact = NUM_BUCKETS // 2
    large = max_exact + (np.log(np.maximum(n, 1) / max_exact) / np.log(MAX_DISTANCE / max_exact)
                         * (NUM_BUCKETS - max_exact)).astype(np.int64)
    large = np.minimum(large, NUM_BUCKETS - 1)
    return np.where(n < max_exact, n, large).astype(np.int32)


def _bias_tables():
    qi = np.arange(ATT_BLK)[:, None]
    ki = np.arange(ATT_BLK)[None, :]
    steps = (ATT_BLK + qi - ki, qi - ki)
    idx = np.zeros((len(GROUPS), 2, ATT_BLK, ATT_BLK), np.int32)
    for g, (_, d) in enumerate(GROUPS):
        for p, j in enumerate(steps):
            valid = (j >= 0) & (j <= ATT_BLK)
            idx[g, p] = np.where(valid, _t5_bucket(np.clip(j, 0, ATT_BLK) * d), -1)
    return idx


def _attn_bias(rel_bias, name):
    idx = _bias_tables()
    ng = len(GROUPS)
    buckets = [sorted(set(idx[g][idx[g] >= 0].tolist())) for g in range(ng)]

    def body(rb_ref, idx_ref, o_ref):
        h = pl.program_id(0)
        for g in range(ng):
            ig = idx_ref[g]
            acc = jnp.full(ig.shape, NEG, F32)
            for b in buckets[g]:
                acc = jnp.where(ig == b, rb_ref[b, g * HEADS + h], acc)
            o_ref[g] = acc

    return _pcall(
        body, name=name, grid=(HEADS,),
        in_specs=[pl.BlockSpec(memory_space=pltpu.SMEM),
                  pl.BlockSpec((ng, 2, ATT_BLK, ATT_BLK), lambda h: (0, 0, 0, 0))],
        out_specs=pl.BlockSpec((ng, None, 2, ATT_BLK, ATT_BLK), lambda h: (0, h, 0, 0, 0)),
        out_shape=_sds((ng, HEADS, 2, ATT_BLK, ATT_BLK), F32),
        compiler_params=_cparams(("parallel",)),
    )(rel_bias, jnp.asarray(idx))


ADA_SHARD = 6 * D_MODEL // N_CHIP
ADA_TN = 512


def _ada_fwd(c_all, ada_w, ada_b_cols, name):
    def body(c_ref, w_ref, b_ref, o_ref):
        ca = _silu(c_ref[...]).astype(BF16)
        o_ref[...] = _dot(ca, w_ref[...].astype(BF16)) + b_ref[...]

    return _pcall(
        body, name=name, grid=(DEPTH, ADA_SHARD // ADA_TN),
        in_specs=[pl.BlockSpec((N_DEV, D_MODEL), lambda l, j: (0, 0)),
                  pl.BlockSpec((None, D_MODEL, ADA_TN), lambda l, j: (l, 0, j)),
                  pl.BlockSpec((None, 1, ADA_TN), lambda l, j: (l, 0, j))],
        out_specs=pl.BlockSpec((None, N_DEV, ADA_TN), lambda l, j: (l, 0, j)),
        out_shape=_sds((DEPTH, N_DEV, ADA_SHARD), F32),
        compiler_params=_cparams(("parallel", "parallel")),
    )(c_all, ada_w, ada_b_cols)


def _ada_bwd(c_all, dmod_cols, name):
    def body(c_ref, d_ref, o_ref):
        ca = _silu(c_ref[...]).astype(BF16)
        o_ref[...] = _dot_tn(ca, d_ref[...].astype(BF16))

    return _pcall(
        body, name=name, grid=(DEPTH, ADA_SHARD // ADA_TN),
        in_specs=[pl.BlockSpec((N_DEV, D_MODEL), lambda l, j: (0, 0)),
                  pl.BlockSpec((None, N_DEV, ADA_TN), lambda l, j: (l, 0, j))],
        out_specs=pl.BlockSpec((None, D_MODEL, ADA_TN), lambda l, j: (l, 0, j)),
        out_shape=_sds((DEPTH, D_MODEL, ADA_SHARD), F32),
        compiler_params=_cparams(("parallel", "parallel")),
    )(c_all, dmod_cols)


def _lower_bounds(logits, name):
    def body(l_ref, o_ref):
        l0 = l_ref[0:1, :]
        l1 = l_ref[1:2, :]
        mx = jnp.maximum(l0, l1)
        e0 = jnp.exp(l0 - mx)
        e1 = jnp.exp(l1 - mx)
        p0 = e0 / (e0 + e1)
        p1 = e1 / (e0 + e1)
        o_ref[0:1, :] = p0 - p0
        o_ref[1:2, :] = (p0 + p1) - p0

    return _pcall(body, name=name, out_shape=_sds((DEPTH, D_MODEL), F32), compiler_params=_cparams())(logits)


_R_DMOD = 0
_R_NMIX = 96
_R_NFFN = 112
_R_QG = 128
_R_KG = 152
_R_GN = 176
_R_LB = 184
_R_RB = 192
SMALL_ROWS = 200


def _small_totals(gathered, logits8, name):
    ng = len(GROUPS)

    def body(g_ref, l_ref, main_ref, gains_ref, dlb_ref, rb_ref):
        tot = g_ref[0]
        for dev in range(1, N_DEV):
            tot = tot + g_ref[dev]
        main_ref[...] = tot[0:_R_QG]
        gains_ref[...] = jnp.zeros_like(gains_ref)
        for g in range(ng):
            gains_ref[g:g + 1, :] = jnp.sum(tot[_R_QG + 8 * g:_R_QG + 8 * g + 8], axis=0, keepdims=True)
            gains_ref[ng + g:ng + g + 1, :] = jnp.sum(tot[_R_KG + 8 * g:_R_KG + 8 * g + 8], axis=0, keepdims=True)
        gains_ref[2 * ng:2 * ng + 1, :] = jnp.sum(tot[_R_GN:_R_GN + 8], axis=0, keepdims=True)
        rb_ref[...] = tot[_R_RB:_R_RB + 8]
        dlb1 = tot[_R_LB:_R_LB + 8]
        l0 = l_ref[0]
        l1 = l_ref[1]
        mx = jnp.maximum(l0, l1)
        e0 = jnp.exp(l0 - mx)
        e1 = jnp.exp(l1 - mx)
        p0 = e0 / (e0 + e1)
        p1 = e1 / (e0 + e1)
        dlb_ref[0] = -p0 * p1 * dlb1
        dlb_ref[1] = p1 * (1.0 - p1) * dlb1

    return _pcall(
        body, name=name,
        out_shape=[_sds((_R_QG, 128), F32), _sds((8, 128), F32), _sds((DEPTH, 8, 128), F32), _sds((8, 128), F32)],
        compiler_params=_cparams(),
    )(gathered, logits8)


def _row_tile(rows):
    return 128 if rows % 128 == 0 else rows


def _adamw(w, grads, m, v, name):
    nl, r, cdim = w.shape
    tr = _row_tile(r)

    def body(*refs):
        g_refs = refs[:nl]
        w_ref, m_ref, v_ref, go_ref, d_ref, mo_ref, vo_ref = refs[nl:]

        def step(g):
            m2 = ADAM_B1 * m_ref[...] + (1.0 - ADAM_B1) * g
            v2 = ADAM_B2 * v_ref[...] + (1.0 - ADAM_B2) * (g * g)
            m_hat = m2 / (1.0 - ADAM_B1 ** ADAM_STEP)
            v_hat = v2 / (1.0 - ADAM_B2 ** ADAM_STEP)
            go_ref[...] = g
            d_ref[...] = -ADAM_LR * (m_hat / (jnp.sqrt(v_hat) + ADAM_EPS) + ADAM_WD * w_ref[...])
            mo_ref[...] = m2
            vo_ref[...] = v2

        if nl == 1:
            step(g_refs[0][...])
        else:
            for layer in range(nl):
                @pl.when(pl.program_id(0) == layer)
                def _(layer=layer):
                    step(g_refs[layer][...])

    big = pl.BlockSpec((None, tr, cdim), lambda l, i: (l, i, 0))
    g_specs = [pl.BlockSpec((tr, cdim), lambda l, i, layer=layer: (jnp.where(l == layer, i, 0), 0))
               for layer in range(nl)]
    shp = _sds((nl, r, cdim), F32)
    return _pcall(
        body, name=name, grid=(nl, r // tr),
        in_specs=g_specs + [big, big, big],
        out_specs=[big, big, big, big],
        out_shape=[shp, shp, shp, shp],
        compiler_params=_cparams(("parallel", "parallel")),
    )(*grads, w, m, v)


def _cast_bf16(place, w, name):
    nl, r, cdim = w.shape
    tr = _row_tile(r)

    def body(place_ref, w_ref, o_ref):
        o_ref[...] = w_ref[...].astype(BF16)

    return _pcall(
        body, name=name,
        grid_spec=pltpu.PrefetchScalarGridSpec(
            num_scalar_prefetch=1, grid=(nl, r // tr),
            in_specs=[pl.BlockSpec((None, tr, cdim), lambda l, i, place_ref: (l, i, 0))],
            out_specs=pl.BlockSpec((None, None, tr, cdim), lambda l, i, place_ref: (place_ref[1], l, i, 0))),
        out_shape=_sds((N_CHIP, nl, r, cdim), BF16),
        compiler_params=_cparams(("parallel", "parallel")),
    )(place, w)


def _rs_add_cast(place, grad, recv, name):
    _, k, n = grad.shape
    kh = k // 2
    tr = _row_tile(kh)
    nb = kh // tr

    def body(place_ref, g_ref, r_ref, o_ref):
        o_ref[...] = (g_ref[...] + r_ref[...]).astype(BF16)

    half = pl.BlockSpec((None, tr, n), lambda s, i, place_ref: (s, i, 0))
    return _pcall(
        body, name=name,
        grid_spec=pltpu.PrefetchScalarGridSpec(
            num_scalar_prefetch=1, grid=(N_CHIP, nb),
            in_specs=[pl.BlockSpec((None, tr, n), lambda s, i, place_ref: (s, place_ref[0] * nb + i, 0)), half],
            out_specs=half),
        out_shape=_sds((N_CHIP, kh, n), BF16),
        compiler_params=_cparams(("parallel", "parallel")),
    )(place, grad, recv)


def _rs_sum4(place, parts, got, name):
    _, kh, n = parts.shape
    tr = _row_tile(kh)
    nb = kh // tr

    def body(place_ref, p_ref, g_ref, o_ref):
        acc = p_ref[...].astype(F32)
        for j in range(N_CHIP - 1):
            acc = acc + g_ref[j].astype(F32)
        o_ref[...] = acc

    return _pcall(
        body, name=name,
        grid_spec=pltpu.PrefetchScalarGridSpec(
            num_scalar_prefetch=1, grid=(nb,),
            in_specs=[pl.BlockSpec((None, tr, n), lambda i, place_ref: (place_ref[1], i, 0)),
                      pl.BlockSpec((N_CHIP - 1, tr, n), lambda i, place_ref: (0, i, 0))],
            out_specs=pl.BlockSpec((tr, n), lambda i, place_ref: (place_ref[0] * nb + i, 0))),
        out_shape=_sds((2 * kh, n), F32),
        compiler_params=_cparams(("parallel",)),
    )(place, parts, got)


_ANY = pl.BlockSpec(memory_space=pl.ANY)


def _position():
    return lax.axis_index("x"), lax.axis_index("y"), lax.axis_index("c")


def _other_chips(x, y):
    return [(1 - x, y), (x, 1 - y), (1 - x, 1 - y)]


def _remote(src, dst, send_sem, recv_sem, to):
    return pltpu.make_async_remote_copy(src_ref=src, dst_ref=dst, send_sem=send_sem, recv_sem=recv_sem,
                                        device_id=to, device_id_type=MESH)


def _small_allgather(v, name):
    r = v.shape[0]

    def body(x_ref, out_ref, send_sems, recv_sems, local_sem):
        x, y, c = _position()
        me, sibling = (x, y, c), (x, y, 1 - c)
        chips = _other_chips(x, y)

        def slab(px, py, pc):
            return out_ref.at[4 * px + 2 * py + pc]

        def copy(k, block, to, src=None):
            return _remote(slab(*block) if src is None else src, slab(*block), send_sems.at[k], recv_sems.at[k], to)

        mine = pltpu.make_async_copy(x_ref, slab(*me), local_sem)
        mine.start()
        first = [copy(0, me, sibling, src=x_ref)]
        first += [copy(1 + j, me, (*chip, c), src=x_ref) for j, chip in enumerate(chips)]
        for cp in first:
            cp.start()
        passed = [copy(4 + j, (*chip, c), sibling) for j, chip in enumerate(chips)]
        for j, chip in enumerate(chips):
            copy(1 + j, (*chip, c), me).wait_recv()
            passed[j].start()
        copy(0, sibling, me).wait_recv()
        for j, chip in enumerate(chips):
            copy(4 + j, (*chip, 1 - c), me).wait_recv()
        for cp in first + passed:
            cp.wait_send()
        mine.wait()

    return _pcall(
        body, name=name,
        out_shape=_sds((N_DEV, r, 128), F32),
        in_specs=[pl.BlockSpec(memory_space=pltpu.VMEM)],
        out_specs=pl.BlockSpec(memory_space=pltpu.VMEM),
        scratch_shapes=[pltpu.SemaphoreType.DMA((7,)), pltpu.SemaphoreType.DMA((7,)), pltpu.SemaphoreType.DMA],
        compiler_params=_cparams(),
    )(v)


def _half_rows(core, kh):
    return pl.ds(pl.multiple_of(core * kh, 8), kh)


def _gather_weights(slabs, name):
    n = len(slabs)

    def body(*refs):
        out = refs[n:2 * n]
        ici_send, ici_recv, d2d_send, d2d_recv = refs[2 * n:]
        x, y, c = _position()
        me_chip = 2 * x + y
        sibling = (x, y, 1 - c)
        chips = _other_chips(x, y)

        def region(a, chip, core):
            kh = out[a].shape[2] // 2
            return out[a].at[chip, :, _half_rows(core, kh), :]

        sends = []
        for a in range(n):
            for j, (px, py) in enumerate(chips):
                mine = region(a, me_chip, c)
                cp = _remote(mine, mine, ici_send.at[a, j], ici_recv.at[a, j], (px, py, c))
                cp.start()
                sends.append(cp)
        for j, (px, py) in enumerate(chips):
            for a in range(n):
                landed = region(a, 2 * px + py, c)
                _remote(landed, landed, ici_send.at[a, j], ici_recv.at[a, j], (px, py, c)).wait_recv()
                cp = _remote(landed, landed, d2d_send.at[a, j], d2d_recv.at[a, j], sibling)
                cp.start()
                sends.append(cp)
        for j, (px, py) in enumerate(chips):
            for a in range(n):
                other = region(a, 2 * px + py, 1 - c)
                _remote(other, other, d2d_send.at[a, j], d2d_recv.at[a, j], sibling).wait_recv()
        for cp in sends:
            cp.wait_send()

    sem = pltpu.SemaphoreType.DMA((n, 3))
    return _pcall(
        body, name=name,
        out_shape=[_sds(s.shape, BF16) for s in slabs],
        in_specs=[_ANY] * n, out_specs=[_ANY] * n,
        input_output_aliases={a: a for a in range(n)},
        scratch_shapes=[sem, sem, sem, sem],
        compiler_params=_cparams(),
    )(*slabs)


def _rs_exchange_halves(grads, name):
    n = len(grads)

    def body(*refs):
        g = refs[:n]
        out = refs[n:2 * n]
        send_sems, recv_sems = refs[2 * n:]
        x, y, c = _position()
        copies = []
        for a in range(n):
            kh = g[a].shape[1] // 2
            cp = _remote(g[a].at[:, _half_rows(1 - c, kh), :], out[a], send_sems.at[a], recv_sems.at[a], (x, y, 1 - c))
            cp.start()
            copies.append(cp)
        for cp in copies:
            cp.wait()

    return _pcall(
        body, name=name,
        out_shape=[_sds((N_CHIP, g.shape[1] // 2, g.shape[2]), F32) for g in grads],
        in_specs=[_ANY] * n, out_specs=[_ANY] * n,
        scratch_shapes=[pltpu.SemaphoreType.DMA((n,)), pltpu.SemaphoreType.DMA((n,))],
        compiler_params=_cparams(),
    )(*grads)


def _rs_exchange_chips(parts, name):
    n = len(parts)

    def body(*refs):
        p = refs[:n]
        out = refs[n:2 * n]
        send_sems, recv_sems = refs[2 * n:]
        x, y, c = _position()
        chips = _other_chips(x, y)
        sends = []
        for a in range(n):
            for j, (px, py) in enumerate(chips):
                cp = _remote(p[a].at[2 * px + py], out[a].at[j], send_sems.at[a, j], recv_sems.at[a, j], (px, py, c))
                cp.start()
                sends.append(cp)
        for a in range(n):
            for j, (px, py) in enumerate(chips):
                got = out[a].at[j]
                _remote(got, got, send_sems.at[a, j], recv_sems.at[a, j], (px, py, c)).wait_recv()
        for cp in sends:
            cp.wait_send()

    sem = pltpu.SemaphoreType.DMA((n, 3))
    return _pcall(
        body, name=name,
        out_shape=[_sds((N_CHIP - 1,) + p.shape[1:], BF16) for p in parts],
        in_specs=[_ANY] * n, out_specs=[_ANY] * n,
        scratch_shapes=[sem, sem],
        compiler_params=_cparams(),
    )(*parts)


def _rs_join_halves(fulls, name):
    n = len(fulls)

    def body(*refs):
        out = refs[n:2 * n]
        send_sems, recv_sems = refs[2 * n:]
        x, y, c = _position()
        copies = []
        for a in range(n):
            kh = out[a].shape[0] // 2
            mine = out[a].at[_half_rows(c, kh), :]
            cp = _remote(mine, mine, send_sems.at[a], recv_sems.at[a], (x, y, 1 - c))
            cp.start()
            copies.append(cp)
        for a in range(n):
            kh = out[a].shape[0] // 2
            theirs = out[a].at[_half_rows(1 - c, kh), :]
            _remote(theirs, theirs, send_sems.at[a], recv_sems.at[a], (x, y, 1 - c)).wait_recv()
        for cp in copies:
            cp.wait_send()

    return _pcall(
        body, name=name,
        out_shape=[_sds(f.shape, F32) for f in fulls],
        in_specs=[_ANY] * n, out_specs=[_ANY] * n,
        input_output_aliases={a: a for a in range(n)},
        scratch_shapes=[pltpu.SemaphoreType.DMA((n,)), pltpu.SemaphoreType.DMA((n,))],
        compiler_params=_cparams(),
    )(*fulls)


_SMALL_ORDER = ("rel_bias", "ada_b", "norm_mix", "norm_ffn", "attn_q_gain", "attn_k_gain", "hgrn_gnorm",
                "hgrn_lower_bounds")
_WEIGHT_ORDER = ("rel_bias", "ada_w", "ada_b", "norm_mix", "norm_ffn", "attn_w_qkv", "attn_w_out", "attn_q_gain",
                 "attn_k_gain", "hgrn_w_in", "hgrn_w_out", "hgrn_gnorm", "hgrn_lower_bounds", "ffn_w1", "ffn_w3",
                 "ffn_w2")


def _qkv_group_map(t):
    return t // 4, t % 4


def _qkv_chip_map(t):
    return t // 9, t % 9


def _hin_map(t):
    return t // 2, t % 2


def _block_map(t):
    return t, 0


def _pack_rows(parts):
    return jnp.concatenate([p.reshape(-1, 128) for p in parts], axis=0)


def kernel(x, c, rel_bias, ada_w, ada_b, norm_mix, norm_ffn, attn_w_qkv, attn_w_out, attn_q_gain, attn_k_gain, hgrn_w_in, hgrn_w_out, hgrn_gnorm, hgrn_lower_bounds, ffn_w1, ffn_w3, ffn_w2, loss_target, m_rel_bias, m_ada_w, m_ada_b, m_norm_mix, m_norm_ffn, m_attn_w_qkv, m_attn_w_out, m_attn_q_gain, m_attn_k_gain, m_hgrn_w_in, m_hgrn_w_out, m_hgrn_gnorm, m_hgrn_lower_bounds, m_ffn_w1, m_ffn_w3, m_ffn_w2, v_rel_bias, v_ada_w, v_ada_b, v_norm_mix, v_norm_ffn, v_attn_w_qkv, v_attn_w_out, v_attn_q_gain, v_attn_k_gain, v_hgrn_w_in, v_hgrn_w_out, v_hgrn_gnorm, v_hgrn_lower_bounds, v_ffn_w1, v_ffn_w3, v_ffn_w2):
    weights = dict(rel_bias=rel_bias, ada_w=ada_w, ada_b=ada_b, norm_mix=norm_mix, norm_ffn=norm_ffn,
                   attn_w_qkv=attn_w_qkv, attn_w_out=attn_w_out, attn_q_gain=attn_q_gain, attn_k_gain=attn_k_gain,
                   hgrn_w_in=hgrn_w_in, hgrn_w_out=hgrn_w_out, hgrn_gnorm=hgrn_gnorm,
                   hgrn_lower_bounds=hgrn_lower_bounds, ffn_w1=ffn_w1, ffn_w3=ffn_w3, ffn_w2=ffn_w2)
    mom1 = dict(rel_bias=m_rel_bias, ada_w=m_ada_w, ada_b=m_ada_b, norm_mix=m_norm_mix, norm_ffn=m_norm_ffn,
                attn_w_qkv=m_attn_w_qkv, attn_w_out=m_attn_w_out, attn_q_gain=m_attn_q_gain,
                attn_k_gain=m_attn_k_gain, hgrn_w_in=m_hgrn_w_in, hgrn_w_out=m_hgrn_w_out, hgrn_gnorm=m_hgrn_gnorm,
                hgrn_lower_bounds=m_hgrn_lower_bounds, ffn_w1=m_ffn_w1, ffn_w3=m_ffn_w3, ffn_w2=m_ffn_w2)
    mom2 = dict(rel_bias=v_rel_bias, ada_w=v_ada_w, ada_b=v_ada_b, norm_mix=v_norm_mix, norm_ffn=v_norm_ffn,
                attn_w_qkv=v_attn_w_qkv, attn_w_out=v_attn_w_out, attn_q_gain=v_attn_q_gain,
                attn_k_gain=v_attn_k_gain, hgrn_w_in=v_hgrn_w_in, hgrn_w_out=v_hgrn_w_out, hgrn_gnorm=v_hgrn_gnorm,
                hgrn_lower_bounds=v_hgrn_lower_bounds, ffn_w1=v_ffn_w1, ffn_w3=v_ffn_w3, ffn_w2=v_ffn_w2)

    xi, yi, ci = _position()
    chip = 2 * xi + yi
    dev = 4 * xi + 2 * yi + ci
    place = jnp.stack([ci, chip]).astype(jnp.int32)
    d = D_MODEL

    big_names = ("attn_w_qkv", "attn_w_out", "hgrn_w_in", "hgrn_w_out", "ffn_w1", "ffn_w3", "ffn_w2")
    slabs16 = [_cast_bf16(place, weights[k], "cast_" + k) for k in big_names]
    wg = dict(zip(big_names, _gather_weights(slabs16, "gather_weights")))

    c_all = _small_allgather(c.reshape(8, 128), "gather_c").reshape(N_DEV, d)
    ada_b_cols = lax.dynamic_slice(ada_b, (0, chip * ADA_SHARD), (DEPTH, ADA_SHARD)).reshape(DEPTH, 1, ADA_SHARD)
    mod_shard = _ada_fwd(c_all, ada_w, ada_b_cols, "ada_fwd")
    mod_all = _small_allgather(mod_shard.reshape(-1, 128), "gather_mod").reshape(N_DEV, DEPTH, N_DEV, ADA_SHARD)
    mod_mine = lax.dynamic_index_in_dim(mod_all[0::2], dev, axis=2, keepdims=False)
    mod = jnp.transpose(mod_mine, (1, 0, 2)).reshape(DEPTH, 6 * d)

    def mods(layer):
        return [mod[layer:layer + 1, j * d:(j + 1) * d] for j in range(6)]

    x0 = x.reshape(SEQ, d)
    target = loss_target.reshape(SEQ, d)
    qg = attn_q_gain.reshape(len(GROUPS), 1, HEAD_DIM)
    kg = attn_k_gain.reshape(len(GROUPS), 1, HEAD_DIM)
    bias = _attn_bias(rel_bias, "attn_bias")
    lb1 = _lower_bounds(hgrn_lower_bounds, "lower_bounds")[1:2]

    def ffn_fwd(layer, x_in, sc2, sh2, g2):
        hf = _norm_mod(x_in, norm_ffn[layer:layer + 1], sc2, sh2, f"l{layer}_norm_ffn")
        a1, a3, u = _ffn_up(hf, wg["ffn_w1"], wg["ffn_w3"], layer, f"l{layer}_ffn_up")
        z, x_out = _mm_rows(u, wg["ffn_w2"], layer, x_in, g2, f"l{layer}_ffn_down")
        return x_out, (hf, a1, a3, u, z)

    def ffn_bwd(layer, dx_out, x_in, sc2, sh2, g2, saved):
        hf, a1, a3, u, z = saved
        dz, dg2 = _gate_bwd(dx_out, z, g2, f"l{layer}_ffn_gate_bwd")
        da1, da3 = _ffn_down_bwd(dz, wg["ffn_w2"], layer, a1, a3, f"l{layer}_ffn_down_bwd")
        dw2 = _mm_rows_bwd_w(u, dz, f"l{layer}_dw2")
        dh = _mm_cols_bwd_a([(da1, wg["ffn_w1"], layer), (da3, wg["ffn_w3"], layer)], tn=FFN_SHARD,
                            act_map=_block_map, w_map=_block_map, n_tiles=N_CHIP, name=f"l{layer}_ffn_up_bwd")
        dw1 = _mm_cols_bwd_w(hf, da1, ns=FFN_SHARD, tn=FFN_SHARD, act_map=_block_map, w_map=_block_map,
                             n_tiles=N_CHIP, name=f"l{layer}_dw1")
        dw3 = _mm_cols_bwd_w(hf, da3, ns=FFN_SHARD, tn=FFN_SHARD, act_map=_block_map, w_map=_block_map,
                             n_tiles=N_CHIP, name=f"l{layer}_dw3")
        dx_in, dsc2, dsh2, dnf = _norm_mod_bwd(x_in, norm_ffn[layer:layer + 1], sc2, sh2, dh, dx_out,
                                               f"l{layer}_norm_ffn_bwd")
        return dx_in, (dw1, dw3, dw2), (dsh2, dsc2, dg2), dnf

    sh1_0, sc1_0, g1_0, sh2_0, sc2_0, g2_0 = mods(0)
    h0 = _norm_mod(x0, norm_mix[0:1], sc1_0, sh1_0, "l0_norm_mix")
    w_qkv9 = _retile_cols(wg["attn_w_qkv"].reshape(N_CHIP, d, 2304), n_out=9, width_out=d, tn=256,
                          src_map=_qkv_chip_map, dst_map=_qkv_group_map, n_tiles=36,
                          name="regroup_w_qkv").reshape(9, 1, d, d)
    qkv9 = _mm_cols(h0, w_qkv9, 0, n_blocks=9, width=d, tn=d, act_map=_block_map, w_map=_block_map,
                    out_dtype=F32, name="l0_qkv")
    o4, lse = _attn_fwd(qkv9, qg, kg, bias, "l0_attn")
    y0, x1 = _mm_rows(o4, wg["attn_w_out"], 0, x0, g1_0, "l0_attn_out")
    x2, ffn0 = ffn_fwd(0, x1, sc2_0, sh2_0, g2_0)

    sh1_1, sc1_1, g1_1, sh2_1, sc2_1, g2_1 = mods(1)
    h1 = _norm_mod(x2, norm_mix[1:2], sc1_1, sh1_1, "l1_norm_mix")
    proj4 = _mm_cols(h1, wg["hgrn_w_in"], 0, n_blocks=4, width=d, tn=512, act_map=_hin_map, w_map=_hin_map,
                     out_dtype=F32, name="l1_hgrn_in")
    o_raw, yg4, states = _hgrn_fwd(proj4, lb1, hgrn_gnorm, "l1_hgrn")
    y1, x3 = _mm_rows(yg4, wg["hgrn_w_out"], 0, x2, g1_1, "l1_hgrn_out")
    x4, ffn1 = ffn_fwd(1, x3, sc2_1, sh2_1, g2_1)

    dx4, loss_part = _loss_head(x4, target, "loss_head")
    loss = lax.psum(loss_part[0, 0], ("x", "y", "c"))

    dx3, (dw1_1, dw3_1, dw2_1), dmod2_1, dnf_1 = ffn_bwd(1, dx4, x3, sc2_1, sh2_1, g2_1, ffn1)
    dzm1, dg1_1 = _gate_bwd(dx3, y1, g1_1, "l1_mix_gate_bwd")
    dyg4 = _mm_rows_bwd_a(dzm1, wg["hgrn_w_out"], 0, "l1_hgrn_out_bwd")
    dw_hout = _mm_rows_bwd_w(yg4, dzm1, "l1_dw_hgrn_out")
    dproj4, dlb_h, dgn_h = _hgrn_bwd(proj4, lb1, hgrn_gnorm, o_raw, dyg4, states, "l1_hgrn_bwd")
    dh1 = _mm_cols_bwd_a([(dproj4, wg["hgrn_w_in"], 0)], tn=512, act_map=_hin_map, w_map=_hin_map, n_tiles=8,
                         name="l1_hgrn_in_bwd")
    dw_hin = _mm_cols_bwd_w(h1, dproj4, ns=d, tn=512, act_map=_hin_map, w_map=_hin_map, n_tiles=8,
                            name="l1_dw_hgrn_in")
    dx2, dsc1_1, dsh1_1, dnm_1 = _norm_mod_bwd(x2, norm_mix[1:2], sc1_1, sh1_1, dh1, dx3, "l1_norm_mix_bwd")

    dx1, (dw1_0, dw3_0, dw2_0), dmod2_0, dnf_0 = ffn_bwd(0, dx2, x1, sc2_0, sh2_0, g2_0, ffn0)
    dzm0, dg1_0 = _gate_bwd(dx1, y0, g1_0, "l0_mix_gate_bwd")
    do4 = _mm_rows_bwd_a(dzm0, wg["attn_w_out"], 0, "l0_attn_out_bwd")
    dw_aout = _mm_rows_bwd_w(o4, dzm0, "l0_dw_attn_out")
    dqkv, dqg_h, dkg_h, dbias = _attn_bwd(qkv9, qg, kg, bias, do4, o4, lse, "l0_attn_bwd")
    dqkv9 = dqkv.reshape(9, SEQ, d)
    dh0 = _mm_cols_bwd_a([(dqkv9, w_qkv9, 0)], tn=d, act_map=_block_map, w_map=_block_map, n_tiles=9,
                         name="l0_qkv_bwd")
    dw_qkv9 = _mm_cols_bwd_w(h0, dqkv9, ns=d, tn=d, act_map=_block_map, w_map=_block_map, n_tiles=9,
                             name="l0_dw_qkv", tm=512, n_out=9)
    dw_qkv = _retile_cols(dw_qkv9, n_out=N_CHIP, width_out=2304, tn=256, src_map=_qkv_group_map,
                          dst_map=_qkv_chip_map, n_tiles=36, name="regroup_dw_qkv")
    dx0, dsc1_0, dsh1_0, dnm_0 = _norm_mod_bwd(x0, norm_mix[0:1], sc1_0, sh1_0, dh0, dx1, "l0_norm_mix_bwd")
    drb8 = _relbias_bwd(dbias, jnp.asarray(_bias_tables()), "rel_bias_bwd")

    small = _pack_rows([
        dsh1_0, dsc1_0, dg1_0, *dmod2_0, dsh1_1, dsc1_1, dg1_1, *dmod2_1,
        dnm_0, dnm_1, dnf_0, dnf_1,
        jnp.transpose(dqg_h, (1, 0, 2, 3)), jnp.transpose(dkg_h, (1, 0, 2, 3)), dgn_h, dlb_h, drb8])
    small_all = _small_allgather(small, "gather_small")
    main, gains, dlbnd, rbt = _small_totals(small_all, hgrn_lower_bounds.reshape(DEPTH, 8, 128), "small_totals")
    ng = len(GROUPS)
    grads = {
        "ada_b": main[_R_DMOD:_R_NMIX].reshape(DEPTH, 6 * d),
        "norm_mix": main[_R_NMIX:_R_NFFN].reshape(DEPTH, d),
        "norm_ffn": main[_R_NFFN:_R_QG].reshape(DEPTH, d),
        "attn_q_gain": gains[0:ng].reshape(1, ng, HEAD_DIM),
        "attn_k_gain": gains[ng:2 * ng].reshape(1, ng, HEAD_DIM),
        "hgrn_gnorm": gains[2 * ng:2 * ng + 1],
        "hgrn_lower_bounds": dlbnd.reshape(DEPTH, d),
        "rel_bias": jnp.transpose(rbt[:, :ng * NUM_BUCKETS].reshape(HEADS, ng, NUM_BUCKETS), (2, 1, 0))
                       .reshape(NUM_BUCKETS, ng * HEADS),
    }
    dmod_all = small_all[:, _R_DMOD:_R_NMIX].reshape(N_DEV, DEPTH, 6 * d)
    dmod_cols = jnp.transpose(lax.dynamic_slice(dmod_all, (0, 0, chip * ADA_SHARD), (N_DEV, DEPTH, ADA_SHARD)),
                              (1, 0, 2))
    grad_ada_w = _ada_bwd(c_all, dmod_cols, "ada_bwd")

    layer_grads = {"attn_w_qkv": [dw_qkv], "attn_w_out": [dw_aout], "hgrn_w_in": [dw_hin], "hgrn_w_out": [dw_hout],
                   "ffn_w1": [dw1_0, dw1_1], "ffn_w3": [dw3_0, dw3_1], "ffn_w2": [dw2_0, dw2_1]}
    tags = [(k, layer) for k in big_names for layer in range(len(layer_grads[k]))]
    flat = [layer_grads[k][layer] for k, layer in tags]
    recv = _rs_exchange_halves(flat, "rs_exchange_halves")
    parts = [_rs_add_cast(place, g, r, f"rs_add_{k}_{layer}") for (k, layer), g, r in zip(tags, flat, recv)]
    got = _rs_exchange_chips(parts, "rs_exchange_chips")
    halves = [_rs_sum4(place, p, r, f"rs_sum_{k}_{layer}") for (k, layer), p, r in zip(tags, parts, got)]
    full = dict(zip(tags, _rs_join_halves(halves, "rs_join_halves")))

    out_g, out_d, out_m, out_v = {}, {}, {}, {}
    for k in big_names:
        gs = [full[(k, layer)] for layer in range(len(layer_grads[k]))]
        out_g[k], out_d[k], out_m[k], out_v[k] = _adamw(weights[k], gs, mom1[k], mom2[k], "adamw_" + k)
    shp = (1, DEPTH * d, ADA_SHARD)
    res = _adamw(ada_w.reshape(shp), [grad_ada_w.reshape(shp[1:])], m_ada_w.reshape(shp), v_ada_w.reshape(shp),
                 "adamw_ada_w")
    out_g["ada_w"], out_d["ada_w"], out_m["ada_w"], out_v["ada_w"] = [r.reshape(ada_w.shape) for r in res]
    packed = [_pack_rows([src[k] for k in _SMALL_ORDER])[None] for src in (weights, grads, mom1, mom2)]
    res = _adamw(packed[0], [packed[1][0]], packed[2], packed[3], "adamw_small")
    offset = 0
    for k in _SMALL_ORDER:
        size = weights[k].size
        for dst, r in zip((out_g, out_d, out_m, out_v), res):
            dst[k] = r.reshape(-1)[offset:offset + size].reshape(weights[k].shape)
        offset += size

    return (loss, dx0.reshape(x.shape), *[out_g[k] for k in _WEIGHT_ORDER], *[out_d[k] for k in _WEIGHT_ORDER],
            *[out_m[k] for k in _WEIGHT_ORDER], *[out_v[k] for k in _WEIGHT_ORDER])
```

```python
import functools

import numpy as np
import jax
import jax.numpy as jnp
from jax import lax
from jax.experimental import pallas as pl
from jax.experimental.pallas import tpu as pltpu

F32 = jnp.float32
BF16 = jnp.bfloat16

D_MODEL = 1024
SEQ = 4096
N_DEV = 8
N_CHIP = 4
DEPTH = 2
HEADS = 8
HEAD_DIM = 128
GROUPS = ((128, 1), (512, 4), (2048, 16))
ATT_BLK = 128
ATT_PAIR = SEQ // ATT_BLK // 2
NUM_BUCKETS = 32
MAX_DISTANCE = 2048
FFN_HIDDEN = 2816
FFN_SHARD = FFN_HIDDEN // N_CHIP
HG_SUB = 16
HG_TC = 512
HG_HP = 2
RMS_EPS = 1e-6
NEG = -1e30
ATT_SCALE = HEAD_DIM ** -0.5
ADAM_LR, ADAM_B1, ADAM_B2, ADAM_EPS, ADAM_WD, ADAM_STEP = 0.001, 0.9, 0.999, 1e-08, 0.01, 10
VMEM_LIMIT = 56 * 1024 * 1024
MESH = pl.DeviceIdType.MESH


def _pcall(body, **kw):
    return pl.pallas_call(body, **kw)


def _cparams(sem=None):
    if sem is None:
        return pltpu.CompilerParams(vmem_limit_bytes=VMEM_LIMIT)
    return pltpu.CompilerParams(dimension_semantics=sem, vmem_limit_bytes=VMEM_LIMIT)


def _sds(shape, dtype):
    return jax.ShapeDtypeStruct(shape, dtype)


def _dot(a, b):
    return jnp.dot(a, b, preferred_element_type=F32)


def _dot_nt(a, b):
    return lax.dot_general(a, b, (((1,), (1,)), ((), ())), preferred_element_type=F32)


def _dot_tn(a, b):
    return lax.dot_general(a, b, (((0,), (0,)), ((), ())), preferred_element_type=F32)


def _sigmoid(x):
    return 1.0 / (1.0 + jnp.exp(-x))


def _silu(x):
    return x * _sigmoid(x)


def _dsilu(x):
    s = _sigmoid(x)
    return s * (1.0 + x * (1.0 - s))


def _norm_mod(x, gain, sc, sh, name):
    tm = 512

    def body(x_ref, g_ref, sc_ref, sh_ref, h_ref):
        xv = x_ref[...]
        rs = lax.rsqrt(jnp.mean(xv * xv, axis=-1, keepdims=True) + RMS_EPS)
        h_ref[...] = ((xv * rs * g_ref[...]) * (1.0 + sc_ref[...]) + sh_ref[...]).astype(BF16)

    vec = pl.BlockSpec((1, D_MODEL), lambda i: (0, 0))
    return _pcall(
        body, name=name, grid=(SEQ // tm,),
        in_specs=[pl.BlockSpec((tm, D_MODEL), lambda i: (i, 0)), vec, vec, vec],
        out_specs=pl.BlockSpec((tm, D_MODEL), lambda i: (i, 0)),
        out_shape=_sds((SEQ, D_MODEL), BF16),
        compiler_params=_cparams(("parallel",)),
    )(x, gain, sc, sh)


def _norm_mod_bwd(x, gain, sc, sh, dh, dres, name):
    tm = 512

    def body(x_ref, g_ref, sc_ref, sh_ref, dh_ref, dres_ref, dx_ref, dsc_ref, dsh_ref, dg_ref):
        @pl.when(pl.program_id(0) == 0)
        def _():
            dsc_ref[...] = jnp.zeros_like(dsc_ref)
            dsh_ref[...] = jnp.zeros_like(dsh_ref)
            dg_ref[...] = jnp.zeros_like(dg_ref)

        xv = x_ref[...]
        dhv = dh_ref[...]
        rs = lax.rsqrt(jnp.mean(xv * xv, axis=-1, keepdims=True) + RMS_EPS)
        xh = xv * rs
        dsc_ref[...] += jnp.sum(dhv * (xh * g_ref[...]), axis=0, keepdims=True)
        dsh_ref[...] += jnp.sum(dhv, axis=0, keepdims=True)
        dhn = dhv * (1.0 + sc_ref[...])
        dg_ref[...] += jnp.sum(dhn * xh, axis=0, keepdims=True)
        dxh = dhn * g_ref[...]
        dx_ref[...] = dres_ref[...] + rs * (dxh - xh * jnp.mean(dxh * xh, axis=-1, keepdims=True))

    vec = pl.BlockSpec((1, D_MODEL), lambda i: (0, 0))
    big = pl.BlockSpec((tm, D_MODEL), lambda i: (i, 0))
    return _pcall(
        body, name=name, grid=(SEQ // tm,),
        in_specs=[big, vec, vec, vec, big, big],
        out_specs=[big, vec, vec, vec],
        out_shape=[_sds((SEQ, D_MODEL), F32)] + [_sds((1, D_MODEL), F32)] * 3,
        compiler_params=_cparams(("arbitrary",)),
    )(x, gain, sc, sh, dh, dres)


def _mm_cols(a, wg, layer, *, n_blocks, width, tn, act_map, w_map, out_dtype, name, tm=1024):
    k = a.shape[1]
    n_tiles = n_blocks * width // tn

    def body(a_ref, w_ref, o_ref):
        o_ref[...] = _dot(a_ref[...], w_ref[...]).astype(o_ref.dtype)

    return _pcall(
        body, name=name, grid=(SEQ // tm, n_tiles),
        in_specs=[pl.BlockSpec((tm, k), lambda i, t: (i, 0)),
                  pl.BlockSpec((None, None, k, tn), lambda i, t: (w_map(t)[0], layer, 0, w_map(t)[1]))],
        out_specs=pl.BlockSpec((None, tm, tn), lambda i, t: (act_map(t)[0], i, act_map(t)[1])),
        out_shape=_sds((n_blocks, SEQ, width), out_dtype),
        compiler_params=_cparams(("parallel", "arbitrary")),
    )(a, wg)


def _mm_cols_bwd_a(pairs, *, tn, act_map, w_map, n_tiles, name, tm=512):
    k = pairs[0][1].shape[2]
    n_p = len(pairs)

    def body(*refs):
        o_ref = refs[-1]

        @pl.when(pl.program_id(1) == 0)
        def _():
            o_ref[...] = jnp.zeros_like(o_ref)

        acc = _dot_nt(refs[0][...], refs[1][...])
        for p in range(1, n_p):
            acc += _dot_nt(refs[2 * p][...], refs[2 * p + 1][...])
        o_ref[...] += acc

    in_specs, args = [], []
    for dout, wg, layer in pairs:
        in_specs.append(pl.BlockSpec((None, tm, tn), lambda i, t: (act_map(t)[0], i, act_map(t)[1])))
        in_specs.append(pl.BlockSpec((None, None, k, tn),
                                     lambda i, t, layer=layer: (w_map(t)[0], layer, 0, w_map(t)[1])))
        args += [dout, wg]
    return _pcall(
        body, name=name, grid=(SEQ // tm, n_tiles),
        in_specs=in_specs,
        out_specs=pl.BlockSpec((tm, k), lambda i, t: (i, 0)),
        out_shape=_sds((SEQ, k), F32),
        compiler_params=_cparams(("parallel", "arbitrary")),
    )(*args)


def _mm_cols_bwd_w(a, dout, *, ns, tn, act_map, w_map, n_tiles, name, tm=1024, n_out=N_CHIP):
    k = a.shape[1]

    def body(a_ref, d_ref, o_ref):
        @pl.when(pl.program_id(1) == 0)
        def _():
            o_ref[...] = jnp.zeros_like(o_ref)

        o_ref[...] += _dot_tn(a_ref[...], d_ref[...])

    return _pcall(
        body, name=name, grid=(n_tiles, SEQ // tm),
        in_specs=[pl.BlockSpec((tm, k), lambda t, i: (i, 0)),
                  pl.BlockSpec((None, tm, tn), lambda t, i: (act_map(t)[0], i, act_map(t)[1]))],
        out_specs=pl.BlockSpec((None, k, tn), lambda t, i: (w_map(t)[0], 0, w_map(t)[1])),
        out_shape=_sds((n_out, k, ns), F32),
        compiler_params=_cparams(("parallel", "arbitrary")),
    )(a, dout)


def _retile_cols(src, *, n_out, width_out, tn, src_map, dst_map, n_tiles, name):
    k = src.shape[1]

    def body(s_ref, o_ref):
        o_ref[...] = s_ref[...]

    return _pcall(
        body, name=name, grid=(n_tiles,),
        in_specs=[pl.BlockSpec((None, k, tn), lambda t: (src_map(t)[0], 0, src_map(t)[1]))],
        out_specs=pl.BlockSpec((None, k, tn), lambda t: (dst_map(t)[0], 0, dst_map(t)[1])),
        out_shape=_sds((n_out, k, width_out), src.dtype),
        compiler_params=_cparams(("parallel",)),
    )(src)


def _mm_rows(a4, wg, layer, x, gate, name, tm=512):
    ks = a4.shape[2]
    n = wg.shape[3]

    def body(a_ref, w_ref, x_ref, g_ref, z_ref, xn_ref):
        s = pl.program_id(1)

        @pl.when(s == 0)
        def _():
            z_ref[...] = jnp.zeros_like(z_ref)

        z_ref[...] += _dot(a_ref[...], w_ref[...])

        @pl.when(s == N_CHIP - 1)
        def _():
            xn_ref[...] = x_ref[...] + g_ref[...] * z_ref[...]

    big = pl.BlockSpec((tm, n), lambda i, s: (i, 0))
    return _pcall(
        body, name=name, grid=(SEQ // tm, N_CHIP),
        in_specs=[pl.BlockSpec((None, tm, ks), lambda i, s: (s, i, 0)),
                  pl.BlockSpec((None, None, ks, n), lambda i, s: (s, layer, 0, 0)),
                  big, pl.BlockSpec((1, n), lambda i, s: (0, 0))],
        out_specs=[big, big],
        out_shape=[_sds((SEQ, n), F32), _sds((SEQ, n), F32)],
        compiler_params=_cparams(("parallel", "arbitrary")),
    )(a4, wg, x, gate)


def _gate_bwd(dx, z, gate, name):
    tm = 512

    def body(dx_ref, z_ref, g_ref, dz_ref, dg_ref):
        @pl.when(pl.program_id(0) == 0)
        def _():
            dg_ref[...] = jnp.zeros_like(dg_ref)

        dxv = dx_ref[...]
        dz_ref[...] = (dxv * g_ref[...]).astype(BF16)
        dg_ref[...] += jnp.sum(dxv * z_ref[...], axis=0, keepdims=True)

    big = pl.BlockSpec((tm, D_MODEL), lambda i: (i, 0))
    vec = pl.BlockSpec((1, D_MODEL), lambda i: (0, 0))
    return _pcall(
        body, name=name, grid=(SEQ // tm,),
        in_specs=[big, big, vec], out_specs=[big, vec],
        out_shape=[_sds((SEQ, D_MODEL), BF16), _sds((1, D_MODEL), F32)],
        compiler_params=_cparams(("arbitrary",)),
    )(dx, z, gate)


def _mm_rows_bwd_a(dz, wg, layer, name, tm=512):
    ks, n = wg.shape[2], wg.shape[3]

    def body(dz_ref, w_ref, o_ref):
        o_ref[...] = _dot_nt(dz_ref[...], w_ref[...])

    return _pcall(
        body, name=name, grid=(SEQ // tm, N_CHIP),
        in_specs=[pl.BlockSpec((tm, n), lambda i, s: (i, 0)),
                  pl.BlockSpec((None, None, ks, n), lambda i, s: (s, layer, 0, 0))],
        out_specs=pl.BlockSpec((None, tm, ks), lambda i, s: (s, i, 0)),
        out_shape=_sds((N_CHIP, SEQ, ks), F32),
        compiler_params=_cparams(("parallel", "arbitrary")),
    )(dz, wg)


def _mm_rows_bwd_w(a4, dz, name, tm=1024):
    ks = a4.shape[2]
    n = dz.shape[1]

    def body(a_ref, dz_ref, o_ref):
        @pl.when(pl.program_id(1) == 0)
        def _():
            o_ref[...] = jnp.zeros_like(o_ref)

        o_ref[...] += _dot_tn(a_ref[...], dz_ref[...])

    return _pcall(
        body, name=name, grid=(N_CHIP, SEQ // tm),
        in_specs=[pl.BlockSpec((None, tm, ks), lambda s, i: (s, i, 0)),
                  pl.BlockSpec((tm, n), lambda s, i: (i, 0))],
        out_specs=pl.BlockSpec((None, ks, n), lambda s, i: (s, 0, 0)),
        out_shape=_sds((N_CHIP, ks, n), F32),
        compiler_params=_cparams(("parallel", "arbitrary")),
    )(a4, dz)


def _ffn_up(h, w1g, w3g, layer, name, tm=512):
    def body(h_ref, w1_ref, w3_ref, a1_ref, a3_ref, u_ref):
        hv = h_ref[...]
        a1 = _dot(hv, w1_ref[...])
        a3 = _dot(hv, w3_ref[...])
        a1_ref[...] = a1
        a3_ref[...] = a3
        u_ref[...] = (_silu(a1) * a3).astype(BF16)

    wspec = pl.BlockSpec((None, None, D_MODEL, FFN_SHARD), lambda i, s: (s, layer, 0, 0))
    ospec = pl.BlockSpec((None, tm, FFN_SHARD), lambda i, s: (s, i, 0))
    shp = (N_CHIP, SEQ, FFN_SHARD)
    return _pcall(
        body, name=name, grid=(SEQ // tm, N_CHIP),
        in_specs=[pl.BlockSpec((tm, D_MODEL), lambda i, s: (i, 0)), wspec, wspec],
        out_specs=[ospec, ospec, ospec],
        out_shape=[_sds(shp, F32), _sds(shp, F32), _sds(shp, BF16)],
        compiler_params=_cparams(("parallel", "arbitrary")),
    )(h, w1g, w3g)


def _ffn_down_bwd(dz, w2g, layer, a1, a3, name, tm=512):
    def body(dz_ref, w_ref, a1_ref, a3_ref, da1_ref, da3_ref):
        du = _dot_nt(dz_ref[...], w_ref[...])
        a1 = a1_ref[...]
        da1_ref[...] = (du * a3_ref[...] * _dsilu(a1)).astype(BF16)
        da3_ref[...] = (du * _silu(a1)).astype(BF16)

    blk = pl.BlockSpec((None, tm, FFN_SHARD), lambda i, s: (s, i, 0))
    shp = (N_CHIP, SEQ, FFN_SHARD)
    return _pcall(
        body, name=name, grid=(SEQ // tm, N_CHIP),
        in_specs=[pl.BlockSpec((tm, D_MODEL), lambda i, s: (i, 0)),
                  pl.BlockSpec((None, None, FFN_SHARD, D_MODEL), lambda i, s: (s, layer, 0, 0)),
                  blk, blk],
        out_specs=[blk, blk],
        out_shape=[_sds(shp, BF16), _sds(shp, BF16)],
        compiler_params=_cparams(("parallel", "arbitrary")),
    )(dz, w2g, a1, a3)


def _loss_head(y, target, name):
    tm = 512

    def body(y_ref, t_ref, dy_ref, l_ref, acc_ref):
        @pl.when(pl.program_id(0) == 0)
        def _():
            acc_ref[...] = jnp.zeros_like(acc_ref)

        err = y_ref[...] - t_ref[...]
        dy_ref[...] = err * (1.0 / D_MODEL)
        acc_ref[...] += jnp.sum(jnp.mean(err * err, axis=-1, keepdims=True), axis=0, keepdims=True)

        @pl.when(pl.program_id(0) == pl.num_programs(0) - 1)
        def _():
            l_ref[...] = 0.5 * acc_ref[...]

    big = pl.BlockSpec((tm, D_MODEL), lambda i: (i, 0))
    return _pcall(
        body, name=name, grid=(SEQ // tm,),
        in_specs=[big, big],
        out_specs=[big, pl.BlockSpec((1, 1), lambda i: (0, 0))],
        out_shape=[_sds((SEQ, D_MODEL), F32), _sds((1, 1), F32)],
        scratch_shapes=[pltpu.VMEM((1, 1), F32)],
        compiler_params=_cparams(("arbitrary",)),
    )(y, target)


def _attn_rows(base, d):
    if d == 1:
        return pl.ds(pl.multiple_of(base, ATT_BLK), ATT_BLK)
    return pl.ds(base, ATT_BLK, stride=d)


def _attn_block_index(i, d):
    nb = SEQ // (ATT_BLK * d)
    r = i // nb
    n = i % nb
    base = r + n * (ATT_BLK * d)
    pbase = jnp.maximum(base - ATT_BLK * d, r)
    return n, _attn_rows(base, d), _attn_rows(pbase, d)


def _qk_normed(x):
    rs = lax.rsqrt(jnp.mean(x * x, axis=-1, keepdims=True) + RMS_EPS)
    return x * rs, rs


def _attn_fwd(qkv9, qgain, kgain, bias, name, gather=()):
    n_g = len(gather)

    def body(*refs):
        q_ref, k_ref, v_ref, qg_ref, kg_ref, b_ref = refs[:6]
        o_ref, lse_ref = refs[6 + n_g:8 + n_g]
        qn_s, kn_s, acc_s, m_s, l_s = refs[8 + 2 * n_g:13 + 2 * n_g]
        g = pl.program_id(1)
        if n_g:
            comm_start, comm_wait = _gather_ici(refs[8 + n_g:8 + 2 * n_g], *refs[13 + 2 * n_g:])
            pl.when((pl.program_id(0) == 0) & (g == 0))(comm_start)

        @pl.when(g == 0)
        def _():
            m_s[...] = jnp.full_like(m_s, NEG)
            l_s[...] = jnp.zeros_like(l_s)
            acc_s[...] = jnp.zeros_like(acc_s)

        qn_s[...] = _qk_normed(q_ref[...])[0] * qg_ref[...]
        kn_s[...] = _qk_normed(k_ref[...])[0] * kg_ref[...]

        for gi, (_, d) in enumerate(GROUPS):
            @pl.when(g == gi)
            def _(d=d):
                def block(n, qb, kc, kp, vc, vp, m_old, l_old, acc_old):
                    sc = _dot_nt(qb, kc) * ATT_SCALE + b_ref[1]
                    sp = _dot_nt(qb, kp) * ATT_SCALE + jnp.where(n > 0, b_ref[0], NEG)
                    m_new = jnp.maximum(m_old, jnp.maximum(jnp.max(sc, axis=-1, keepdims=True),
                                                           jnp.max(sp, axis=-1, keepdims=True)))
                    alpha = jnp.exp(m_old - m_new)
                    pc = jnp.exp(sc - m_new)
                    pp = jnp.exp(sp - m_new)
                    l_new = alpha * l_old + jnp.sum(pc, axis=-1, keepdims=True) + jnp.sum(pp, axis=-1, keepdims=True)
                    acc_new = alpha * acc_old + _dot(pc.astype(BF16), vc) + _dot(pp.astype(BF16), vp)
                    return m_new, l_new, acc_new

                def it(i, carry):
                    where, loaded = [], []
                    for off in (0, ATT_PAIR):
                        n, rows, prow = _attn_block_index(i + off, d)
                        where.append(rows)
                        loaded.append((n, qn_s[rows, :].astype(BF16), kn_s[rows, :].astype(BF16),
                                       kn_s[prow, :].astype(BF16), v_ref[rows, :].astype(BF16),
                                       v_ref[prow, :].astype(BF16), m_s[rows, :], l_s[rows, :], acc_s[rows, :]))
                    results = [block(*vals) for vals in loaded]
                    for rows, (m_new, l_new, acc_new) in zip(where, results):
                        m_s[rows, :] = m_new
                        l_s[rows, :] = l_new
                        acc_s[rows, :] = acc_new
                    return carry

                lax.fori_loop(0, ATT_PAIR, it, 0)

        @pl.when(g == len(GROUPS) - 1)
        def _():
            o_ref[...] = (acc_s[...] / l_s[...]).astype(BF16)
            lse_ref[...] = m_s[...] + jnp.log(l_s[...])

        if n_g:
            pl.when((pl.program_id(0) == HEADS - 1) & (g == len(GROUPS) - 1))(comm_wait)

    def col(j):
        return pl.BlockSpec((None, SEQ, HEAD_DIM), lambda h, g: (g * 3 + j, 0, h))

    gspec = pl.BlockSpec((None, 1, HEAD_DIM), lambda h, g: (g, 0, 0))
    sem = pltpu.SemaphoreType.DMA((max(n_g, 1), 3))
    return _pcall(
        body, name=name, grid=(HEADS, len(GROUPS)),
        in_specs=[col(0), col(1), col(2), gspec, gspec,
                  pl.BlockSpec((None, None, 2, ATT_BLK, ATT_BLK), lambda h, g: (g, h, 0, 0, 0))] + [_ANY] * n_g,
        out_specs=[pl.BlockSpec((None, SEQ, HEAD_DIM), lambda h, g: (h // 2, 0, h % 2)),
                   pl.BlockSpec((None, SEQ, 1), lambda h, g: (h, 0, 0))] + [_ANY] * n_g,
        out_shape=[_sds((N_CHIP, SEQ, 2 * HEAD_DIM), BF16), _sds((HEADS, SEQ, 1), F32)]
        + [_sds(s.shape, s.dtype) for s in gather],
        input_output_aliases={6 + a: 2 + a for a in range(n_g)},
        scratch_shapes=[pltpu.VMEM((SEQ, HEAD_DIM), F32)] * 3 + [pltpu.VMEM((SEQ, 1), F32)] * 2
        + ([sem, sem] if n_g else []),
        compiler_params=_cparams(("arbitrary", "arbitrary")),
    )(qkv9, qkv9, qkv9, qgain, kgain, bias, *gather)


def _attn_bwd(qkv9, qgain, kgain, bias, do4, o4, lse, name, scatter=()):
    n_s = len(scatter)

    def body(*refs):
        q_ref, k_ref, v_ref, qg_ref, kg_ref, b_ref, do_ref, o_ref, lse_ref = refs[:9]
        dqkv_ref, dqg_ref, dkg_ref, db_ref = refs[9 + n_s:13 + n_s]
        qn_s, kn_s, dq_s, dk_s, dv_s, dl_s = refs[13 + 2 * n_s:19 + 2 * n_s]
        g = pl.program_id(1)
        if n_s:
            comm_start, comm_wait = _rs_chips(refs[9:9 + n_s], refs[13 + n_s:13 + 2 * n_s], *refs[19 + 2 * n_s:])
            pl.when((pl.program_id(0) == 0) & (g == 0))(comm_start)
        qh, rq = _qk_normed(q_ref[...])
        kh, rk = _qk_normed(k_ref[...])
        qn_s[...] = qh * qg_ref[...]
        kn_s[...] = kh * kg_ref[...]
        dl_s[...] = jnp.sum(do_ref[...] * o_ref[...].astype(F32), axis=-1, keepdims=True)
        dk_s[...] = jnp.zeros_like(dk_s)
        dv_s[...] = jnp.zeros_like(dv_s)
        db_ref[...] = jnp.zeros_like(db_ref)

        for gi, (_, d) in enumerate(GROUPS):
            @pl.when(g == gi)
            def _(d=d):
                def block(n, qb, kc, kp, vc, vp, dob, lse_b, dl):
                    sc = _dot_nt(qb, kc) * ATT_SCALE + b_ref[1]
                    sp = _dot_nt(qb, kp) * ATT_SCALE + jnp.where(n > 0, b_ref[0], NEG)
                    pc = jnp.exp(sc - lse_b)
                    pp = jnp.exp(sp - lse_b)
                    dsc = pc * (_dot_nt(dob, vc) - dl)
                    dsp = pp * (_dot_nt(dob, vp) - dl)
                    dsc16 = dsc.astype(BF16)
                    dsp16 = dsp.astype(BF16)
                    dq = (_dot(dsc16, kc) + _dot(dsp16, kp)) * ATT_SCALE
                    return (dsc, dsp, dq, _dot_tn(dsc16, qb) * ATT_SCALE, _dot_tn(dsp16, qb) * ATT_SCALE,
                            _dot_tn(pc.astype(BF16), dob), _dot_tn(pp.astype(BF16), dob))

                def it(i, carry):
                    where, loaded, old = [], [], []
                    for off in (0, ATT_PAIR):
                        n, rows, prow = _attn_block_index(i + off, d)
                        where.append((rows, prow))
                        loaded.append((n, qn_s[rows, :].astype(BF16), kn_s[rows, :].astype(BF16),
                                       kn_s[prow, :].astype(BF16), v_ref[rows, :].astype(BF16),
                                       v_ref[prow, :].astype(BF16), do_ref[rows, :].astype(BF16),
                                       lse_ref[rows, :], dl_s[rows, :]))
                        old.append((dk_s[rows, :], dk_s[prow, :], dv_s[rows, :], dv_s[prow, :]))
                    results = [block(*vals) for vals in loaded]
                    db_ref[1] += results[0][0] + results[1][0]
                    db_ref[0] += results[0][1] + results[1][1]
                    for (rows, prow), (dk_c, dk_p, dv_c, dv_p), (_, _, dq, dkc, dkp, dvc, dvp) in zip(where, old, results):
                        dq_s[rows, :] = dq
                        dk_s[prow, :] = dk_p + dkp
                        dv_s[prow, :] = dv_p + dvp
                        dk_s[rows, :] = dk_c + dkc
                        dv_s[rows, :] = dv_c + dvc
                    return carry

                lax.fori_loop(0, ATT_PAIR, it, 0)

        def norm_bwd(dn, xh, rs, gain):
            dgain = jnp.sum(dn * xh, axis=0, keepdims=True)
            dxh = dn * gain
            return rs * (dxh - xh * jnp.mean(dxh * xh, axis=-1, keepdims=True)), dgain

        dq, dqg = norm_bwd(dq_s[...], qh, rq, qg_ref[...])
        dk, dkg = norm_bwd(dk_s[...], kh, rk, kg_ref[...])
        dqkv_ref[0] = dq.astype(BF16)
        dqkv_ref[1] = dk.astype(BF16)
        dqkv_ref[2] = dv_s[...].astype(BF16)
        dqg_ref[...] = dqg
        dkg_ref[...] = dkg
        if n_s:
            pl.when((pl.program_id(0) == HEADS - 1) & (g == len(GROUPS) - 1))(comm_wait)

    def col(j):
        return pl.BlockSpec((None, SEQ, HEAD_DIM), lambda h, g: (g * 3 + j, 0, h))

    gspec = pl.BlockSpec((None, 1, HEAD_DIM), lambda h, g: (g, 0, 0))
    bspec = pl.BlockSpec((None, None, 2, ATT_BLK, ATT_BLK), lambda h, g: (g, h, 0, 0, 0))
    hcol = pl.BlockSpec((None, SEQ, HEAD_DIM), lambda h, g: (h // 2, 0, h % 2))
    dgspec = pl.BlockSpec((None, None, 1, HEAD_DIM), lambda h, g: (h, g, 0, 0))
    ng = len(GROUPS)
    sem = pltpu.SemaphoreType.DMA((max(n_s, 1), 3))
    return _pcall(
        body, name=name, grid=(HEADS, ng),
        in_specs=[col(0), col(1), col(2), gspec, gspec, bspec, hcol, hcol,
                  pl.BlockSpec((None, SEQ, 1), lambda h, g: (h, 0, 0))] + [_ANY] * n_s,
        out_specs=[pl.BlockSpec((None, 3, SEQ, HEAD_DIM), lambda h, g: (g, 0, 0, h)), dgspec, dgspec, bspec]
        + [_ANY] * n_s,
        out_shape=[_sds((ng, 3, SEQ, D_MODEL), BF16), _sds((HEADS, ng, 1, HEAD_DIM), F32),
                   _sds((HEADS, ng, 1, HEAD_DIM), F32), _sds((ng, HEADS, 2, ATT_BLK, ATT_BLK), F32)]
        + _rs_chips_shapes(scatter),
        scratch_shapes=[pltpu.VMEM((SEQ, HEAD_DIM), F32)] * 5 + [pltpu.VMEM((SEQ, 1), F32)]
        + ([sem, sem] if n_s else []),
        compiler_params=_cparams(("arbitrary", "arbitrary")),
    )(qkv9, qkv9, qkv9, qgain, kgain, bias, do4, o4, lse, *scatter)


def _relbias_bwd(dbias, bucket_idx, name):
    ng = len(GROUPS)

    def body(db_ref, idx_ref, o_ref):
        lane = lax.broadcasted_iota(jnp.int32, (HEADS, 128), 1)
        acc = jnp.zeros((HEADS, 128), F32)
        for g in range(ng):
            dbg = db_ref[g]
            idx = idx_ref[g]
            for b in range(NUM_BUCKETS):
                sel = jnp.where((idx == b)[None], dbg, 0.0)
                part = jnp.sum(jnp.sum(sel, axis=1), axis=1)
                val = jnp.sum(part, axis=-1, keepdims=True)
                acc = jnp.where(lane == g * NUM_BUCKETS + b, val, acc)
        o_ref[...] = acc

    return _pcall(body, name=name, out_shape=_sds((HEADS, 128), F32), compiler_params=_cparams())(dbias, bucket_idx)


def _scan16(x, reverse=False):
    row = lax.broadcasted_iota(jnp.int32, x.shape, 0)
    for sh in (1, 2, 4, 8):
        if reverse:
            x = x + jnp.where(row < HG_SUB - sh, pltpu.roll(x, HG_SUB - sh, 0), 0.0)
        else:
            x = x + jnp.where(row >= sh, pltpu.roll(x, sh, 0), 0.0)
    return x


def _hgrn_gates(qr, fr, lbv):
    q = _silu(qr)
    sig = _sigmoid(fr)
    fg = lbv + (1.0 - lbv) * sig
    lf = jnp.log(fg)
    gcum = _scan16(lf)
    glast = jnp.sum(lf, axis=0, keepdims=True)
    return q, sig, fg, 1.0 - fg, gcum, glast


def _hgrn_intra(q, k, gcum, tri):
    e = jnp.exp(jnp.where(tri, gcum[:, None, :] - gcum[None, :, :], NEG))
    a = jnp.sum(q[:, None, :] * k[None, :, :] * e, axis=-1, keepdims=True)
    return e, a


def _hgrn_fwd(proj4, lb, gain, name):
    nsub = HG_TC // HG_SUB
    wide = HG_HP * HEAD_DIM

    def body(p_ref, lb_ref, gn_ref, o_ref, y_ref, st_ref, state_s):
        @pl.when(pl.program_id(1) == 0)
        def _():
            state_s[...] = jnp.zeros_like(state_s)

        gnv = gn_ref[...]
        shp = (HG_SUB, HG_SUB, HEAD_DIM)
        tri = lax.broadcasted_iota(jnp.int32, shp, 0) >= lax.broadcasted_iota(jnp.int32, shp, 1)

        def head(qr, fr, vv, gr, lbv, st):
            q, _, _, k, gcum, glast = _hgrn_gates(qr, fr, lbv)
            _, a = _hgrn_intra(q, k, gcum, tri)
            o = jnp.sum(a * vv[None, :, :], axis=1) + _dot_nt((q * jnp.exp(gcum)).astype(BF16), st.astype(BF16))
            kg = k * jnp.exp(glast - gcum)
            st_new = st * jnp.exp(glast) + _dot_tn(vv.astype(BF16), kg.astype(BF16))
            rs = lax.rsqrt(jnp.mean(o * o, axis=-1, keepdims=True) + RMS_EPS)
            return o, (o * rs * gnv * _silu(gr)).astype(BF16), st_new

        def it(i, carry):
            rows = pl.ds(pl.multiple_of(i * HG_SUB, HG_SUB), HG_SUB)
            loaded = []
            for hh in range(HG_HP):
                lanes = pl.ds(hh * HEAD_DIM, HEAD_DIM)
                loaded.append(([p_ref[j, rows, lanes] for j in range(4)], lb_ref[:, lanes], state_s[hh]))
            results = [head(blk[0], blk[1], blk[2], blk[3], lbv, st) for blk, lbv, st in loaded]
            for hh, ((_, _, st), (o, y, st_new)) in enumerate(zip(loaded, results)):
                lanes = pl.ds(hh * HEAD_DIM, HEAD_DIM)
                st_ref[hh, i] = st.astype(BF16)
                state_s[hh] = st_new
                o_ref[rows, lanes] = o
                y_ref[rows, lanes] = y
            return carry

        lax.fori_loop(0, nsub, it, 0)

    return _pcall(
        body, name=name, grid=(HEADS // HG_HP, SEQ // HG_TC),
        in_specs=[pl.BlockSpec((4, HG_TC, wide), lambda h, j: (0, j, h)),
                  pl.BlockSpec((1, wide), lambda h, j: (0, h)),
                  pl.BlockSpec((1, HEAD_DIM), lambda h, j: (0, 0))],
        out_specs=[pl.BlockSpec((HG_TC, wide), lambda h, j: (j, h)),
                   pl.BlockSpec((None, HG_TC, wide), lambda h, j: (h, j, 0)),
                   pl.BlockSpec((HG_HP, nsub, HEAD_DIM, HEAD_DIM), lambda h, j: (h, j, 0, 0))],
        out_shape=[_sds((SEQ, D_MODEL), F32), _sds((N_CHIP, SEQ, 2 * HEAD_DIM), BF16),
                   _sds((HEADS, SEQ // HG_SUB, HEAD_DIM, HEAD_DIM), BF16)],
        scratch_shapes=[pltpu.VMEM((HG_HP, HEAD_DIM, HEAD_DIM), F32)],
        compiler_params=_cparams(("parallel", "arbitrary")),
    )(proj4, lb, gain)


def _hgrn_bwd(proj4, lb, gain, o_raw, dy4, states, name):
    nsub = HG_TC // HG_SUB
    nt = SEQ // HG_TC
    wide = HG_HP * HEAD_DIM

    def body(p_ref, lb_ref, gn_ref, o_ref, dy_ref, st_ref, dp_ref, dlb_ref, dgn_ref, dst_s):
        @pl.when(pl.program_id(1) == 0)
        def _():
            dst_s[...] = jnp.zeros_like(dst_s)
            dlb_ref[...] = jnp.zeros_like(dlb_ref)
            dgn_ref[...] = jnp.zeros_like(dgn_ref)

        gnv = gn_ref[...]
        shp = (HG_SUB, HG_SUB, HEAD_DIM)
        tri = lax.broadcasted_iota(jnp.int32, shp, 0) >= lax.broadcasted_iota(jnp.int32, shp, 1)

        def head(qr, fr, vv, gr, o, dy, lbv, st0, dst):
            q, sig, fg, k, gcum, glast = _hgrn_gates(qr, fr, lbv)
            rs = lax.rsqrt(jnp.mean(o * o, axis=-1, keepdims=True) + RMS_EPS)
            oh = o * rs
            don = dy * _silu(gr)
            dgn = jnp.sum(don * oh, axis=0, keepdims=True)
            dgr = dy * oh * gnv * _dsilu(gr)
            doh = don * gnv
            do = rs * (doh - oh * jnp.mean(doh * oh, axis=-1, keepdims=True))
            dst16 = dst.astype(BF16)
            do16 = do.astype(BF16)
            eg = jnp.exp(gcum)
            eb = jnp.exp(glast - gcum)
            e, a = _hgrn_intra(q, k, gcum, tri)
            da = jnp.sum(do[:, None, :] * vv[None, :, :], axis=-1, keepdims=True)
            dae = da * e
            dq = jnp.sum(dae * k[None, :, :], axis=1) + eg * _dot(do16, st0)
            dk_state = eb * _dot(vv.astype(BF16), dst16)
            dk = jnp.sum(dae * q[:, None, :], axis=0) + dk_state
            dv = jnp.sum(a * do[:, None, :], axis=0) + _dot_nt((k * eb).astype(BF16), dst16)
            eglast = jnp.exp(glast)
            dst_new = dst * eglast + _dot_tn(do16, (q * eg).astype(BF16))
            dglast = jnp.sum(k * dk_state, axis=0, keepdims=True) \
                + eglast * jnp.sum(dst * st0.astype(F32), axis=0, keepdims=True)
            dlf = _scan16(q * dq - k * dk, reverse=True) + dglast
            dfg = dlf / fg - dk
            dlb = jnp.sum(dfg * (1.0 - sig), axis=0, keepdims=True)
            dproj = ((dq * _dsilu(qr)).astype(BF16), (dfg * (1.0 - lbv) * sig * (1.0 - sig)).astype(BF16),
                     dv.astype(BF16), dgr.astype(BF16))
            return dproj, dst_new, dlb, dgn

        def it(ii, carry):
            i = nsub - 1 - ii
            rows = pl.ds(pl.multiple_of(i * HG_SUB, HG_SUB), HG_SUB)
            results = []
            for hh in range(HG_HP):
                lanes = pl.ds(hh * HEAD_DIM, HEAD_DIM)
                blk = [p_ref[j, rows, lanes] for j in range(4)]
                results.append(head(blk[0], blk[1], blk[2], blk[3], o_ref[rows, lanes], dy_ref[rows, lanes],
                                    lb_ref[:, lanes], st_ref[hh, i], dst_s[hh]))
            new_carry = []
            for hh, (dproj, dst_new, dlb, dgn) in enumerate(results):
                lanes = pl.ds(hh * HEAD_DIM, HEAD_DIM)
                dst_s[hh] = dst_new
                for j in range(4):
                    dp_ref[j, rows, lanes] = dproj[j]
                new_carry.append((carry[hh][0] + dlb, carry[hh][1] + dgn))
            return tuple(new_carry)

        zero = jnp.zeros((1, HEAD_DIM), F32)
        sums = lax.fori_loop(0, nsub, it, tuple((zero, zero) for _ in range(HG_HP)))
        for hh in range(HG_HP):
            dlb_ref[hh] += sums[hh][0]
            dgn_ref[hh] += sums[hh][1]

    vspec = pl.BlockSpec((HG_HP, 1, HEAD_DIM), lambda h, j: (h, 0, 0))
    return _pcall(
        body, name=name, grid=(HEADS // HG_HP, nt),
        in_specs=[pl.BlockSpec((4, HG_TC, wide), lambda h, j: (0, nt - 1 - j, h)),
                  pl.BlockSpec((1, wide), lambda h, j: (0, h)),
                  pl.BlockSpec((1, HEAD_DIM), lambda h, j: (0, 0)),
                  pl.BlockSpec((HG_TC, wide), lambda h, j: (nt - 1 - j, h)),
                  pl.BlockSpec((None, HG_TC, wide), lambda h, j: (h, nt - 1 - j, 0)),
                  pl.BlockSpec((HG_HP, nsub, HEAD_DIM, HEAD_DIM), lambda h, j: (h, nt - 1 - j, 0, 0))],
        out_specs=[pl.BlockSpec((4, HG_TC, wide), lambda h, j: (0, nt - 1 - j, h)), vspec, vspec],
        out_shape=[_sds((4, SEQ, D_MODEL), BF16), _sds((HEADS, 1, HEAD_DIM), F32), _sds((HEADS, 1, HEAD_DIM), F32)],
        scratch_shapes=[pltpu.VMEM((HG_HP, HEAD_DIM, HEAD_DIM), F32)],
        compiler_params=_cparams(("parallel", "arbitrary")),
    )(proj4, lb, gain, o_raw, dy4, states)


def _t5_bucket(dist):
    n = np.asarray(dist, dtype=np.int64)
    max_exact = NUM_BUCKETS // 2
    large = max_exact + (np.log(np.maximum(n, 1) / max_exact) / np.log(MAX_DISTANCE / max_exact)
                         * (NUM_BUCKETS - max_exact)).astype(np.int64)
    large = np.minimum(large, NUM_BUCKETS - 1)
    return np.where(n < max_exact, n, large).astype(np.int32)


def _bias_tables():
    qi = np.arange(ATT_BLK)[:, None]
    ki = np.arange(ATT_BLK)[None, :]
    steps = (ATT_BLK + qi - ki, qi - ki)
    idx = np.zeros((len(GROUPS), 2, ATT_BLK, ATT_BLK), np.int32)
    for g, (_, d) in enumerate(GROUPS):
        for p, j in enumerate(steps):
            valid = (j >= 0) & (j <= ATT_BLK)
            idx[g, p] = np.where(valid, _t5_bucket(np.clip(j, 0, ATT_BLK) * d), -1)
    return idx


def _attn_bias(rel_bias, name):
    idx = _bias_tables()
    ng = len(GROUPS)
    buckets = [sorted(set(idx[g][idx[g] >= 0].tolist())) for g in range(ng)]

    def body(rb_ref, idx_ref, o_ref):
        h = pl.program_id(0)
        for g in range(ng):
            ig = idx_ref[g]
            acc = jnp.full(ig.shape, NEG, F32)
            for b in buckets[g]:
                acc = jnp.where(ig == b, rb_ref[b, g * HEADS + h], acc)
            o_ref[g] = acc

    return _pcall(
        body, name=name, grid=(HEADS,),
        in_specs=[pl.BlockSpec(memory_space=pltpu.SMEM),
                  pl.BlockSpec((ng, 2, ATT_BLK, ATT_BLK), lambda h: (0, 0, 0, 0))],
        out_specs=pl.BlockSpec((ng, None, 2, ATT_BLK, ATT_BLK), lambda h: (0, h, 0, 0, 0)),
        out_shape=_sds((ng, HEADS, 2, ATT_BLK, ATT_BLK), F32),
        compiler_params=_cparams(("parallel",)),
    )(rel_bias, jnp.asarray(idx))


ADA_SHARD = 6 * D_MODEL // N_CHIP
ADA_TN = 512


def _ada_fwd(c_all, ada_w, ada_b_cols, name):
    def body(c_ref, w_ref, b_ref, o_ref):
        ca = _silu(c_ref[...]).astype(BF16)
        o_ref[...] = _dot(ca, w_ref[...].astype(BF16)) + b_ref[...]

    return _pcall(
        body, name=name, grid=(DEPTH, ADA_SHARD // ADA_TN),
        in_specs=[pl.BlockSpec((N_DEV, D_MODEL), lambda l, j: (0, 0)),
                  pl.BlockSpec((None, D_MODEL, ADA_TN), lambda l, j: (l, 0, j)),
                  pl.BlockSpec((None, 1, ADA_TN), lambda l, j: (l, 0, j))],
        out_specs=pl.BlockSpec((None, N_DEV, ADA_TN), lambda l, j: (l, 0, j)),
        out_shape=_sds((DEPTH, N_DEV, ADA_SHARD), F32),
        compiler_params=_cparams(("parallel", "parallel")),
    )(c_all, ada_w, ada_b_cols)


def _ada_bwd(c_all, dmod_cols, name):
    def body(c_ref, d_ref, o_ref):
        ca = _silu(c_ref[...]).astype(BF16)
        o_ref[...] = _dot_tn(ca, d_ref[...].astype(BF16))

    return _pcall(
        body, name=name, grid=(DEPTH, ADA_SHARD // ADA_TN),
        in_specs=[pl.BlockSpec((N_DEV, D_MODEL), lambda l, j: (0, 0)),
                  pl.BlockSpec((None, N_DEV, ADA_TN), lambda l, j: (l, 0, j))],
        out_specs=pl.BlockSpec((None, D_MODEL, ADA_TN), lambda l, j: (l, 0, j)),
        out_shape=_sds((DEPTH, D_MODEL, ADA_SHARD), F32),
        compiler_params=_cparams(("parallel", "parallel")),
    )(c_all, dmod_cols)


def _lower_bounds(logits, name):
    def body(l_ref, o_ref):
        l0 = l_ref[0:1, :]
        l1 = l_ref[1:2, :]
        mx = jnp.maximum(l0, l1)
        e0 = jnp.exp(l0 - mx)
        e1 = jnp.exp(l1 - mx)
        p0 = e0 / (e0 + e1)
        p1 = e1 / (e0 + e1)
        o_ref[0:1, :] = p0 - p0
        o_ref[1:2, :] = (p0 + p1) - p0

    return _pcall(body, name=name, out_shape=_sds((DEPTH, D_MODEL), F32), compiler_params=_cparams())(logits)


_R_DMOD = 0
_R_NMIX = 96
_R_NFFN = 112
_R_QG = 128
_R_KG = 152
_R_GN = 176
_R_LB = 184
_R_RB = 192
SMALL_ROWS = 200


def _small_totals(gathered, logits8, name):
    ng = len(GROUPS)

    def body(g_ref, l_ref, main_ref, gains_ref, dlb_ref, rb_ref):
        tot = g_ref[0]
        for dev in range(1, N_DEV):
            tot = tot + g_ref[dev]
        main_ref[...] = tot[0:_R_QG]
        gains_ref[...] = jnp.zeros_like(gains_ref)
        for g in range(ng):
            gains_ref[g:g + 1, :] = jnp.sum(tot[_R_QG + 8 * g:_R_QG + 8 * g + 8], axis=0, keepdims=True)
            gains_ref[ng + g:ng + g + 1, :] = jnp.sum(tot[_R_KG + 8 * g:_R_KG + 8 * g + 8], axis=0, keepdims=True)
        gains_ref[2 * ng:2 * ng + 1, :] = jnp.sum(tot[_R_GN:_R_GN + 8], axis=0, keepdims=True)
        rb_ref[...] = tot[_R_RB:_R_RB + 8]
        dlb1 = tot[_R_LB:_R_LB + 8]
        l0 = l_ref[0]
        l1 = l_ref[1]
        mx = jnp.maximum(l0, l1)
        e0 = jnp.exp(l0 - mx)
        e1 = jnp.exp(l1 - mx)
        p0 = e0 / (e0 + e1)
        p1 = e1 / (e0 + e1)
        dlb_ref[0] = -p0 * p1 * dlb1
        dlb_ref[1] = p1 * (1.0 - p1) * dlb1

    return _pcall(
        body, name=name,
        out_shape=[_sds((_R_QG, 128), F32), _sds((8, 128), F32), _sds((DEPTH, 8, 128), F32), _sds((8, 128), F32)],
        compiler_params=_cparams(),
    )(gathered, logits8)


def _row_tile(rows):
    return 128 if rows % 128 == 0 else rows


def _adamw(w, grads, m, v, name):
    nl, r, cdim = w.shape
    tr = _row_tile(r)

    def body(*refs):
        g_refs = refs[:nl]
        w_ref, m_ref, v_ref, go_ref, d_ref, mo_ref, vo_ref = refs[nl:]

        def step(g):
            m2 = ADAM_B1 * m_ref[...] + (1.0 - ADAM_B1) * g
            v2 = ADAM_B2 * v_ref[...] + (1.0 - ADAM_B2) * (g * g)
            m_hat = m2 / (1.0 - ADAM_B1 ** ADAM_STEP)
            v_hat = v2 / (1.0 - ADAM_B2 ** ADAM_STEP)
            go_ref[...] = g
            d_ref[...] = -ADAM_LR * (m_hat / (jnp.sqrt(v_hat) + ADAM_EPS) + ADAM_WD * w_ref[...])
            mo_ref[...] = m2
            vo_ref[...] = v2

        if nl == 1:
            step(g_refs[0][...])
        else:
            for layer in range(nl):
                @pl.when(pl.program_id(0) == layer)
                def _(layer=layer):
                    step(g_refs[layer][...])

    big = pl.BlockSpec((None, tr, cdim), lambda l, i: (l, i, 0))
    g_specs = [pl.BlockSpec((tr, cdim), lambda l, i, layer=layer: (jnp.where(l == layer, i, 0), 0))
               for layer in range(nl)]
    shp = _sds((nl, r, cdim), F32)
    return _pcall(
        body, name=name, grid=(nl, r // tr),
        in_specs=g_specs + [big, big, big],
        out_specs=[big, big, big, big],
        out_shape=[shp, shp, shp, shp],
        compiler_params=_cparams(("parallel", "parallel")),
    )(*grads, w, m, v)


def _cast_bf16(place, w, name):
    nl, r, cdim = w.shape
    tr = _row_tile(r)

    def body(place_ref, w_ref, o_ref):
        o_ref[...] = w_ref[...].astype(BF16)

    return _pcall(
        body, name=name,
        grid_spec=pltpu.PrefetchScalarGridSpec(
            num_scalar_prefetch=1, grid=(nl, r // tr),
            in_specs=[pl.BlockSpec((None, tr, cdim), lambda l, i, place_ref: (l, i, 0))],
            out_specs=pl.BlockSpec((None, None, tr, cdim), lambda l, i, place_ref: (place_ref[1], l, i, 0))),
        out_shape=_sds((N_CHIP, nl, r, cdim), BF16),
        compiler_params=_cparams(("parallel", "parallel")),
    )(place, w)


def _rs_add_cast(place, grad, recv, name):
    _, k, n = grad.shape
    kh = k // 2
    tr = _row_tile(kh)
    nb = kh // tr

    def body(place_ref, g_ref, r_ref, o_ref):
        o_ref[...] = (g_ref[...] + r_ref[...]).astype(BF16)

    half = pl.BlockSpec((None, tr, n), lambda s, i, place_ref: (s, i, 0))
    return _pcall(
        body, name=name,
        grid_spec=pltpu.PrefetchScalarGridSpec(
            num_scalar_prefetch=1, grid=(N_CHIP, nb),
            in_specs=[pl.BlockSpec((None, tr, n), lambda s, i, place_ref: (s, place_ref[0] * nb + i, 0)), half],
            out_specs=half),
        out_shape=_sds((N_CHIP, kh, n), BF16),
        compiler_params=_cparams(("parallel", "parallel")),
    )(place, grad, recv)


def _rs_sum4(place, parts, got, name):
    _, kh, n = parts.shape
    tr = _row_tile(kh)
    nb = kh // tr

    def body(place_ref, p_ref, g_ref, o_ref):
        acc = p_ref[...].astype(F32)
        for j in range(N_CHIP - 1):
            acc = acc + g_ref[j].astype(F32)
        o_ref[...] = acc

    return _pcall(
        body, name=name,
        grid_spec=pltpu.PrefetchScalarGridSpec(
            num_scalar_prefetch=1, grid=(nb,),
            in_specs=[pl.BlockSpec((None, tr, n), lambda i, place_ref: (place_ref[1], i, 0)),
                      pl.BlockSpec((N_CHIP - 1, tr, n), lambda i, place_ref: (0, i, 0))],
            out_specs=pl.BlockSpec((tr, n), lambda i, place_ref: (place_ref[0] * nb + i, 0))),
        out_shape=_sds((2 * kh, n), F32),
        compiler_params=_cparams(("parallel",)),
    )(place, parts, got)


_ANY = pl.BlockSpec(memory_space=pl.ANY)


def _position():
    return lax.axis_index("x"), lax.axis_index("y"), lax.axis_index("c")


def _other_chips(x, y):
    return [(1 - x, y), (x, 1 - y), (1 - x, 1 - y)]


def _remote(src, dst, send_sem, recv_sem, to):
    return pltpu.make_async_remote_copy(src_ref=src, dst_ref=dst, send_sem=send_sem, recv_sem=recv_sem,
                                        device_id=to, device_id_type=MESH)


def _small_allgather(v, name):
    r = v.shape[0]

    def body(x_ref, out_ref, send_sems, recv_sems, local_sem):
        x, y, c = _position()
        me, sibling = (x, y, c), (x, y, 1 - c)
        chips = _other_chips(x, y)

        def slab(px, py, pc):
            return out_ref.at[4 * px + 2 * py + pc]

        def copy(k, block, to, src=None):
            return _remote(slab(*block) if src is None else src, slab(*block), send_sems.at[k], recv_sems.at[k], to)

        mine = pltpu.make_async_copy(x_ref, slab(*me), local_sem)
        mine.start()
        first = [copy(0, me, sibling, src=x_ref)]
        first += [copy(1 + j, me, (*chip, c), src=x_ref) for j, chip in enumerate(chips)]
        for cp in first:
            cp.start()
        passed = [copy(4 + j, (*chip, c), sibling) for j, chip in enumerate(chips)]
        for j, chip in enumerate(chips):
            copy(1 + j, (*chip, c), me).wait_recv()
            passed[j].start()
        copy(0, sibling, me).wait_recv()
        for j, chip in enumerate(chips):
            copy(4 + j, (*chip, 1 - c), me).wait_recv()
        for cp in first + passed:
            cp.wait_send()
        mine.wait()

    return _pcall(
        body, name=name,
        out_shape=_sds((N_DEV, r, 128), F32),
        in_specs=[pl.BlockSpec(memory_space=pltpu.VMEM)],
        out_specs=pl.BlockSpec(memory_space=pltpu.VMEM),
        scratch_shapes=[pltpu.SemaphoreType.DMA((7,)), pltpu.SemaphoreType.DMA((7,)), pltpu.SemaphoreType.DMA],
        compiler_params=_cparams(),
    )(v)


def _half_rows(core, kh):
    return pl.ds(pl.multiple_of(core * kh, 8), kh)


def _slab_half(ref, chip, core):
    return ref.at[chip, :, _half_rows(core, ref.shape[2] // 2), :]


def _gather_ici(out, send_sems, recv_sems):
    def copies():
        x, y, c = _position()
        for a in range(len(out)):
            for j, (px, py) in enumerate(_other_chips(x, y)):
                mine = _slab_half(out[a], 2 * x + y, c)
                landed = _slab_half(out[a], 2 * px + py, c)
                yield (_remote(mine, mine, send_sems.at[a, j], recv_sems.at[a, j], (px, py, c)),
                       _remote(landed, landed, send_sems.at[a, j], recv_sems.at[a, j], (px, py, c)))

    def start():
        for send, _ in copies():
            send.start()

    def wait():
        for send, recv in copies():
            recv.wait_recv()
            send.wait_send()

    return start, wait


def _gather_d2d(out, send_sems, recv_sems):
    def copies():
        x, y, c = _position()
        for a in range(len(out)):
            for j, (px, py) in enumerate(_other_chips(x, y)):
                landed = _slab_half(out[a], 2 * px + py, c)
                other = _slab_half(out[a], 2 * px + py, 1 - c)
                yield (_remote(landed, landed, send_sems.at[a, j], recv_sems.at[a, j], (x, y, 1 - c)),
                       _remote(other, other, send_sems.at[a, j], recv_sems.at[a, j], (x, y, 1 - c)))

    def start():
        for send, _ in copies():
            send.start()

    def wait():
        for send, recv in copies():
            recv.wait_recv()
            send.wait_send()

    return start, wait


def _gather_weights(slabs, name, ici=True):
    n = len(slabs)

    def body(*refs):
        out = refs[n:2 * n]
        sems = refs[2 * n:]
        if ici:
            start, wait = _gather_ici(out, sems[2], sems[3])
            start()
            wait()
        start, wait = _gather_d2d(out, sems[0], sems[1])
        start()
        wait()

    sem = pltpu.SemaphoreType.DMA((n, 3))
    return _pcall(
        body, name=name,
        out_shape=[_sds(s.shape, BF16) for s in slabs],
        in_specs=[_ANY] * n, out_specs=[_ANY] * n,
        input_output_aliases={a: a for a in range(n)},
        scratch_shapes=[sem, sem] + ([sem, sem] if ici else []),
        compiler_params=_cparams(),
    )(*slabs)


def _rs_exchange_halves(grads, name):
    n = len(grads)

    def body(*refs):
        g = refs[:n]
        out = refs[n:2 * n]
        send_sems, recv_sems = refs[2 * n:]
        x, y, c = _position()
        copies = []
        for a in range(n):
            kh = g[a].shape[1] // 2
            cp = _remote(g[a].at[:, _half_rows(1 - c, kh), :], out[a], send_sems.at[a], recv_sems.at[a], (x, y, 1 - c))
            cp.start()
            copies.append(cp)
        for cp in copies:
            cp.wait()

    return _pcall(
        body, name=name,
        out_shape=[_sds((N_CHIP, g.shape[1] // 2, g.shape[2]), F32) for g in grads],
        in_specs=[_ANY] * n, out_specs=[_ANY] * n,
        scratch_shapes=[pltpu.SemaphoreType.DMA((n,)), pltpu.SemaphoreType.DMA((n,))],
        compiler_params=_cparams(),
    )(*grads)


def _rs_chips(parts, out, send_sems, recv_sems):
    def copies():
        x, y, c = _position()
        for a in range(len(parts)):
            for j, (px, py) in enumerate(_other_chips(x, y)):
                got = out[a].at[j]
                yield (_remote(parts[a].at[2 * px + py], got, send_sems.at[a, j], recv_sems.at[a, j], (px, py, c)),
                       _remote(got, got, send_sems.at[a, j], recv_sems.at[a, j], (px, py, c)))

    def start():
        for send, _ in copies():
            send.start()

    def wait():
        for send, recv in copies():
            recv.wait_recv()
            send.wait_send()

    return start, wait


def _rs_chips_shapes(parts):
    return [_sds((N_CHIP - 1,) + p.shape[1:], BF16) for p in parts]


def _rs_exchange_chips(parts, name):
    n = len(parts)

    def body(*refs):
        start, wait = _rs_chips(refs[:n], refs[n:2 * n], *refs[2 * n:])
        start()
        wait()

    sem = pltpu.SemaphoreType.DMA((n, 3))
    return _pcall(
        body, name=name,
        out_shape=_rs_chips_shapes(parts),
        in_specs=[_ANY] * n, out_specs=[_ANY] * n,
        scratch_shapes=[sem, sem],
        compiler_params=_cparams(),
    )(*parts)


def _rs_join_halves(fulls, name):
    n = len(fulls)

    def body(*refs):
        out = refs[n:2 * n]
        send_sems, recv_sems = refs[2 * n:]
        x, y, c = _position()
        copies = []
        for a in range(n):
            kh = out[a].shape[0] // 2
            mine = out[a].at[_half_rows(c, kh), :]
            cp = _remote(mine, mine, send_sems.at[a], recv_sems.at[a], (x, y, 1 - c))
            cp.start()
            copies.append(cp)
        for a in range(n):
            kh = out[a].shape[0] // 2
            theirs = out[a].at[_half_rows(1 - c, kh), :]
            _remote(theirs, theirs, send_sems.at[a], recv_sems.at[a], (x, y, 1 - c)).wait_recv()
        for cp in copies:
            cp.wait_send()

    return _pcall(
        body, name=name,
        out_shape=[_sds(f.shape, F32) for f in fulls],
        in_specs=[_ANY] * n, out_specs=[_ANY] * n,
        input_output_aliases={a: a for a in range(n)},
        scratch_shapes=[pltpu.SemaphoreType.DMA((n,)), pltpu.SemaphoreType.DMA((n,))],
        compiler_params=_cparams(),
    )(*fulls)


_SMALL_ORDER = ("rel_bias", "ada_b", "norm_mix", "norm_ffn", "attn_q_gain", "attn_k_gain", "hgrn_gnorm",
                "hgrn_lower_bounds")
_WEIGHT_ORDER = ("rel_bias", "ada_w", "ada_b", "norm_mix", "norm_ffn", "attn_w_qkv", "attn_w_out", "attn_q_gain",
                 "attn_k_gain", "hgrn_w_in", "hgrn_w_out", "hgrn_gnorm", "hgrn_lower_bounds", "ffn_w1", "ffn_w3",
                 "ffn_w2")


def _qkv_group_map(t):
    return t // 4, t % 4


def _qkv_chip_map(t):
    return t // 9, t % 9


def _hin_map(t):
    return t // 2, t % 2


def _block_map(t):
    return t, 0


def _pack_rows(parts):
    return jnp.concatenate([p.reshape(-1, 128) for p in parts], axis=0)


def kernel(x, c, rel_bias, ada_w, ada_b, norm_mix, norm_ffn, attn_w_qkv, attn_w_out, attn_q_gain, attn_k_gain, hgrn_w_in, hgrn_w_out, hgrn_gnorm, hgrn_lower_bounds, ffn_w1, ffn_w3, ffn_w2, loss_target, m_rel_bias, m_ada_w, m_ada_b, m_norm_mix, m_norm_ffn, m_attn_w_qkv, m_attn_w_out, m_attn_q_gain, m_attn_k_gain, m_hgrn_w_in, m_hgrn_w_out, m_hgrn_gnorm, m_hgrn_lower_bounds, m_ffn_w1, m_ffn_w3, m_ffn_w2, v_rel_bias, v_ada_w, v_ada_b, v_norm_mix, v_norm_ffn, v_attn_w_qkv, v_attn_w_out, v_attn_q_gain, v_attn_k_gain, v_hgrn_w_in, v_hgrn_w_out, v_hgrn_gnorm, v_hgrn_lower_bounds, v_ffn_w1, v_ffn_w3, v_ffn_w2):
    weights = dict(rel_bias=rel_bias, ada_w=ada_w, ada_b=ada_b, norm_mix=norm_mix, norm_ffn=norm_ffn,
                   attn_w_qkv=attn_w_qkv, attn_w_out=attn_w_out, attn_q_gain=attn_q_gain, attn_k_gain=attn_k_gain,
                   hgrn_w_in=hgrn_w_in, hgrn_w_out=hgrn_w_out, hgrn_gnorm=hgrn_gnorm,
                   hgrn_lower_bounds=hgrn_lower_bounds, ffn_w1=ffn_w1, ffn_w3=ffn_w3, ffn_w2=ffn_w2)
    mom1 = dict(rel_bias=m_rel_bias, ada_w=m_ada_w, ada_b=m_ada_b, norm_mix=m_norm_mix, norm_ffn=m_norm_ffn,
                attn_w_qkv=m_attn_w_qkv, attn_w_out=m_attn_w_out, attn_q_gain=m_attn_q_gain,
                attn_k_gain=m_attn_k_gain, hgrn_w_in=m_hgrn_w_in, hgrn_w_out=m_hgrn_w_out, hgrn_gnorm=m_hgrn_gnorm,
                hgrn_lower_bounds=m_hgrn_lower_bounds, ffn_w1=m_ffn_w1, ffn_w3=m_ffn_w3, ffn_w2=m_ffn_w2)
    mom2 = dict(rel_bias=v_rel_bias, ada_w=v_ada_w, ada_b=v_ada_b, norm_mix=v_norm_mix, norm_ffn=v_norm_ffn,
                attn_w_qkv=v_attn_w_qkv, attn_w_out=v_attn_w_out, attn_q_gain=v_attn_q_gain,
                attn_k_gain=v_attn_k_gain, hgrn_w_in=v_hgrn_w_in, hgrn_w_out=v_hgrn_w_out, hgrn_gnorm=v_hgrn_gnorm,
                hgrn_lower_bounds=v_hgrn_lower_bounds, ffn_w1=v_ffn_w1, ffn_w3=v_ffn_w3, ffn_w2=v_ffn_w2)

    xi, yi, ci = _position()
    chip = 2 * xi + yi
    dev = 4 * xi + 2 * yi + ci
    place = jnp.stack([ci, chip]).astype(jnp.int32)
    d = D_MODEL

    big_names = ("attn_w_qkv", "attn_w_out", "hgrn_w_in", "hgrn_w_out", "ffn_w1", "ffn_w3", "ffn_w2")
    early_names, late_names = big_names[:2], big_names[2:]
    slabs16 = {k: _cast_bf16(place, weights[k], "cast_" + k) for k in big_names}
    wg = dict(zip(early_names, _gather_weights([slabs16[k] for k in early_names], "gather_early")))

    c_all = _small_allgather(c.reshape(8, 128), "gather_c").reshape(N_DEV, d)
    ada_b_cols = lax.dynamic_slice(ada_b, (0, chip * ADA_SHARD), (DEPTH, ADA_SHARD)).reshape(DEPTH, 1, ADA_SHARD)
    mod_shard = _ada_fwd(c_all, ada_w, ada_b_cols, "ada_fwd")
    mod_all = _small_allgather(mod_shard.reshape(-1, 128), "gather_mod").reshape(N_DEV, DEPTH, N_DEV, ADA_SHARD)
    mod_mine = lax.dynamic_index_in_dim(mod_all[0::2], dev, axis=2, keepdims=False)
    mod = jnp.transpose(mod_mine, (1, 0, 2)).reshape(DEPTH, 6 * d)

    def mods(layer):
        return [mod[layer:layer + 1, j * d:(j + 1) * d] for j in range(6)]

    x0 = x.reshape(SEQ, d)
    target = loss_target.reshape(SEQ, d)
    qg = attn_q_gain.reshape(len(GROUPS), 1, HEAD_DIM)
    kg = attn_k_gain.reshape(len(GROUPS), 1, HEAD_DIM)
    bias = _attn_bias(rel_bias, "attn_bias")
    lb1 = _lower_bounds(hgrn_lower_bounds, "lower_bounds")[1:2]

    def ffn_fwd(layer, x_in, sc2, sh2, g2):
        hf = _norm_mod(x_in, norm_ffn[layer:layer + 1], sc2, sh2, f"l{layer}_norm_ffn")
        a1, a3, u = _ffn_up(hf, wg["ffn_w1"], wg["ffn_w3"], layer, f"l{layer}_ffn_up")
        z, x_out = _mm_rows(u, wg["ffn_w2"], layer, x_in, g2, f"l{layer}_ffn_down")
        return x_out, (hf, a1, a3, u, z)

    def ffn_bwd(layer, dx_out, x_in, sc2, sh2, g2, saved):
        hf, a1, a3, u, z = saved
        dz, dg2 = _gate_bwd(dx_out, z, g2, f"l{layer}_ffn_gate_bwd")
        da1, da3 = _ffn_down_bwd(dz, wg["ffn_w2"], layer, a1, a3, f"l{layer}_ffn_down_bwd")
        dw2 = _mm_rows_bwd_w(u, dz, f"l{layer}_dw2")
        dh = _mm_cols_bwd_a([(da1, wg["ffn_w1"], layer), (da3, wg["ffn_w3"], layer)], tn=FFN_SHARD,
                            act_map=_block_map, w_map=_block_map, n_tiles=N_CHIP, name=f"l{layer}_ffn_up_bwd")
        dw1 = _mm_cols_bwd_w(hf, da1, ns=FFN_SHARD, tn=FFN_SHARD, act_map=_block_map, w_map=_block_map,
                             n_tiles=N_CHIP, name=f"l{layer}_dw1")
        dw3 = _mm_cols_bwd_w(hf, da3, ns=FFN_SHARD, tn=FFN_SHARD, act_map=_block_map, w_map=_block_map,
                             n_tiles=N_CHIP, name=f"l{layer}_dw3")
        dx_in, dsc2, dsh2, dnf = _norm_mod_bwd(x_in, norm_ffn[layer:layer + 1], sc2, sh2, dh, dx_out,
                                               f"l{layer}_norm_ffn_bwd")
        return dx_in, (dw1, dw3, dw2), (dsh2, dsc2, dg2), dnf

    sh1_0, sc1_0, g1_0, sh2_0, sc2_0, g2_0 = mods(0)
    h0 = _norm_mod(x0, norm_mix[0:1], sc1_0, sh1_0, "l0_norm_mix")
    w_qkv9 = _retile_cols(wg["attn_w_qkv"].reshape(N_CHIP, d, 2304), n_out=9, width_out=d, tn=256,
                          src_map=_qkv_chip_map, dst_map=_qkv_group_map, n_tiles=36,
                          name="regroup_w_qkv").reshape(9, 1, d, d)
    qkv9 = _mm_cols(h0, w_qkv9, 0, n_blocks=9, width=d, tn=d, act_map=_block_map, w_map=_block_map,
                    out_dtype=F32, name="l0_qkv")
    o4, lse, *late = _attn_fwd(qkv9, qg, kg, bias, "l0_attn", gather=[slabs16[k] for k in late_names])
    wg.update(zip(late_names, _gather_weights(late, "gather_late_siblings", ici=False)))
    y0, x1 = _mm_rows(o4, wg["attn_w_out"], 0, x0, g1_0, "l0_attn_out")
    x2, ffn0 = ffn_fwd(0, x1, sc2_0, sh2_0, g2_0)

    sh1_1, sc1_1, g1_1, sh2_1, sc2_1, g2_1 = mods(1)
    h1 = _norm_mod(x2, norm_mix[1:2], sc1_1, sh1_1, "l1_norm_mix")
    proj4 = _mm_cols(h1, wg["hgrn_w_in"], 0, n_blocks=4, width=d, tn=512, act_map=_hin_map, w_map=_hin_map,
                     out_dtype=F32, name="l1_hgrn_in")
    o_raw, yg4, states = _hgrn_fwd(proj4, lb1, hgrn_gnorm, "l1_hgrn")
    y1, x3 = _mm_rows(yg4, wg["hgrn_w_out"], 0, x2, g1_1, "l1_hgrn_out")
    x4, ffn1 = ffn_fwd(1, x3, sc2_1, sh2_1, g2_1)

    dx4, loss_part = _loss_head(x4, target, "loss_head")
    loss = lax.psum(loss_part[0, 0], ("x", "y", "c"))

    dx3, (dw1_1, dw3_1, dw2_1), dmod2_1, dnf_1 = ffn_bwd(1, dx4, x3, sc2_1, sh2_1, g2_1, ffn1)
    dzm1, dg1_1 = _gate_bwd(dx3, y1, g1_1, "l1_mix_gate_bwd")
    dyg4 = _mm_rows_bwd_a(dzm1, wg["hgrn_w_out"], 0, "l1_hgrn_out_bwd")
    dw_hout = _mm_rows_bwd_w(yg4, dzm1, "l1_dw_hgrn_out")
    dproj4, dlb_h, dgn_h = _hgrn_bwd(proj4, lb1, hgrn_gnorm, o_raw, dyg4, states, "l1_hgrn_bwd")
    dh1 = _mm_cols_bwd_a([(dproj4, wg["hgrn_w_in"], 0)], tn=512, act_map=_hin_map, w_map=_hin_map, n_tiles=8,
                         name="l1_hgrn_in_bwd")
    dw_hin = _mm_cols_bwd_w(h1, dproj4, ns=d, tn=512, act_map=_hin_map, w_map=_hin_map, n_tiles=8,
                            name="l1_dw_hgrn_in")
    dx2, dsc1_1, dsh1_1, dnm_1 = _norm_mod_bwd(x2, norm_mix[1:2], sc1_1, sh1_1, dh1, dx3, "l1_norm_mix_bwd")

    dx1, (dw1_0, dw3_0, dw2_0), dmod2_0, dnf_0 = ffn_bwd(0, dx2, x1, sc2_0, sh2_0, g2_0, ffn0)
    dzm0, dg1_0 = _gate_bwd(dx1, y0, g1_0, "l0_mix_gate_bwd")
    do4 = _mm_rows_bwd_a(dzm0, wg["attn_w_out"], 0, "l0_attn_out_bwd")
    dw_aout = _mm_rows_bwd_w(o4, dzm0, "l0_dw_attn_out")

    def rs_prepare(tags, grads_in, suffix):
        recv = _rs_exchange_halves(grads_in, "rs_exchange_halves_" + suffix)
        return [_rs_add_cast(place, g, r, f"rs_add_{k}_{layer}") for (k, layer), g, r in zip(tags, grads_in, recv)]

    tags_a = [("attn_w_out", 0), ("hgrn_w_in", 0), ("hgrn_w_out", 0), ("ffn_w1", 0), ("ffn_w1", 1), ("ffn_w3", 0),
              ("ffn_w3", 1), ("ffn_w2", 0), ("ffn_w2", 1)]
    parts_a = rs_prepare(tags_a, [dw_aout, dw_hin, dw_hout, dw1_0, dw1_1, dw3_0, dw3_1, dw2_0, dw2_1], "a")
    dqkv, dqg_h, dkg_h, dbias, *got_a = _attn_bwd(qkv9, qg, kg, bias, do4, o4, lse, "l0_attn_bwd", scatter=parts_a)
    dqkv9 = dqkv.reshape(9, SEQ, d)
    dh0 = _mm_cols_bwd_a([(dqkv9, w_qkv9, 0)], tn=d, act_map=_block_map, w_map=_block_map, n_tiles=9,
                         name="l0_qkv_bwd")
    dw_qkv9 = _mm_cols_bwd_w(h0, dqkv9, ns=d, tn=d, act_map=_block_map, w_map=_block_map, n_tiles=9,
                             name="l0_dw_qkv", tm=512, n_out=9)
    dw_qkv = _retile_cols(dw_qkv9, n_out=N_CHIP, width_out=2304, tn=256, src_map=_qkv_group_map,
                          dst_map=_qkv_chip_map, n_tiles=36, name="regroup_dw_qkv")
    dx0, dsc1_0, dsh1_0, dnm_0 = _norm_mod_bwd(x0, norm_mix[0:1], sc1_0, sh1_0, dh0, dx1, "l0_norm_mix_bwd")
    drb8 = _relbias_bwd(dbias, jnp.asarray(_bias_tables()), "rel_bias_bwd")

    small = _pack_rows([
        dsh1_0, dsc1_0, dg1_0, *dmod2_0, dsh1_1, dsc1_1, dg1_1, *dmod2_1,
        dnm_0, dnm_1, dnf_0, dnf_1,
        jnp.transpose(dqg_h, (1, 0, 2, 3)), jnp.transpose(dkg_h, (1, 0, 2, 3)), dgn_h, dlb_h, drb8])
    small_all = _small_allgather(small, "gather_small")
    main, gains, dlbnd, rbt = _small_totals(small_all, hgrn_lower_bounds.reshape(DEPTH, 8, 128), "small_totals")
    ng = len(GROUPS)
    grads = {
        "ada_b": main[_R_DMOD:_R_NMIX].reshape(DEPTH, 6 * d),
        "norm_mix": main[_R_NMIX:_R_NFFN].reshape(DEPTH, d),
        "norm_ffn": main[_R_NFFN:_R_QG].reshape(DEPTH, d),
        "attn_q_gain": gains[0:ng].reshape(1, ng, HEAD_DIM),
        "attn_k_gain": gains[ng:2 * ng].reshape(1, ng, HEAD_DIM),
        "hgrn_gnorm": gains[2 * ng:2 * ng + 1],
        "hgrn_lower_bounds": dlbnd.reshape(DEPTH, d),
        "rel_bias": jnp.transpose(rbt[:, :ng * NUM_BUCKETS].reshape(HEADS, ng, NUM_BUCKETS), (2, 1, 0))
                       .reshape(NUM_BUCKETS, ng * HEADS),
    }
    dmod_all = small_all[:, _R_DMOD:_R_NMIX].reshape(N_DEV, DEPTH, 6 * d)
    dmod_cols = jnp.transpose(lax.dynamic_slice(dmod_all, (0, 0, chip * ADA_SHARD), (N_DEV, DEPTH, ADA_SHARD)),
                              (1, 0, 2))
    grad_ada_w = _ada_bwd(c_all, dmod_cols, "ada_bwd")

    tags_b = [("attn_w_qkv", 0)]
    parts_b = rs_prepare(tags_b, [dw_qkv], "b")
    got_b = _rs_exchange_chips(parts_b, "rs_exchange_chips_b")
    tags = tags_a + tags_b
    halves = [_rs_sum4(place, p, r, f"rs_sum_{k}_{layer}")
              for (k, layer), p, r in zip(tags, parts_a + parts_b, list(got_a) + list(got_b))]
    full = dict(zip(tags, _rs_join_halves(halves, "rs_join_halves")))

    out_g, out_d, out_m, out_v = {}, {}, {}, {}
    for k in big_names:
        gs = [full[(k, layer)] for layer in range(weights[k].shape[0])]
        out_g[k], out_d[k], out_m[k], out_v[k] = _adamw(weights[k], gs, mom1[k], mom2[k], "adamw_" + k)
    shp = (1, DEPTH * d, ADA_SHARD)
    res = _adamw(ada_w.reshape(shp), [grad_ada_w.reshape(shp[1:])], m_ada_w.reshape(shp), v_ada_w.reshape(shp),
                 "adamw_ada_w")
    out_g["ada_w"], out_d["ada_w"], out_m["ada_w"], out_v["ada_w"] = [r.reshape(ada_w.shape) for r in res]
    packed = [_pack_rows([src[k] for k in _SMALL_ORDER])[None] for src in (weights, grads, mom1, mom2)]
    res = _adamw(packed[0], [packed[1][0]], packed[2], packed[3], "adamw_small")
    offset = 0
    for k in _SMALL_ORDER:
        size = weights[k].size
        for dst, r in zip((out_g, out_d, out_m, out_v), res):
            dst[k] = r.reshape(-1)[offset:offset + size].reshape(weights[k].shape)
        offset += size

    return (loss, dx0.reshape(x.shape), *[out_g[k] for k in _WEIGHT_ORDER], *[out_d[k] for k in _WEIGHT_ORDER],
            *[out_m[k] for k in _WEIGHT_ORDER], *[out_v[k] for k in _WEIGHT_ORDER])
```

```python
import functools

import numpy as np
import jax
import jax.numpy as jnp
from jax import lax
from jax.experimental import pallas as pl
from jax.experimental.pallas import tpu as pltpu

F32 = jnp.float32
BF16 = jnp.bfloat16

D_MODEL = 1024
SEQ = 4096
N_DEV = 8
N_CHIP = 4
DEPTH = 2
HEADS = 8
HEAD_DIM = 128
GROUPS = ((128, 1), (512, 4), (2048, 16))
ATT_BLK = 128
ATT_PAIR = SEQ // ATT_BLK // 2
NUM_BUCKETS = 32
MAX_DISTANCE = 2048
FFN_HIDDEN = 2816
FFN_SHARD = FFN_HIDDEN // N_CHIP
HG_SUB = 16
HG_TC = 512
HG_HP = 2
RMS_EPS = 1e-6
NEG = -1e30
ATT_SCALE = HEAD_DIM ** -0.5
ADAM_LR, ADAM_B1, ADAM_B2, ADAM_EPS, ADAM_WD, ADAM_STEP = 0.001, 0.9, 0.999, 1e-08, 0.01, 10
VMEM_LIMIT = 56 * 1024 * 1024
MESH = pl.DeviceIdType.MESH


def _pcall(body, **kw):
    return pl.pallas_call(body, **kw)


def _cparams(sem=None):
    if sem is None:
        return pltpu.CompilerParams(vmem_limit_bytes=VMEM_LIMIT)
    return pltpu.CompilerParams(dimension_semantics=sem, vmem_limit_bytes=VMEM_LIMIT)


def _sds(shape, dtype):
    return jax.ShapeDtypeStruct(shape, dtype)


def _dot(a, b):
    return jnp.dot(a, b, preferred_element_type=F32)


def _dot_nt(a, b):
    return lax.dot_general(a, b, (((1,), (1,)), ((), ())), preferred_element_type=F32)


def _dot_tn(a, b):
    return lax.dot_general(a, b, (((0,), (0,)), ((), ())), preferred_element_type=F32)


def _sigmoid(x):
    return 1.0 / (1.0 + jnp.exp(-x))


def _silu(x):
    return x * _sigmoid(x)


def _dsilu(x):
    s = _sigmoid(x)
    return s * (1.0 + x * (1.0 - s))


def _norm_mod(x, gain, sc, sh, name):
    tm = 512

    def body(x_ref, g_ref, sc_ref, sh_ref, h_ref):
        xv = x_ref[...]
        rs = lax.rsqrt(jnp.mean(xv * xv, axis=-1, keepdims=True) + RMS_EPS)
        h_ref[...] = ((xv * rs * g_ref[...]) * (1.0 + sc_ref[...]) + sh_ref[...]).astype(BF16)

    vec = pl.BlockSpec((1, D_MODEL), lambda i: (0, 0))
    return _pcall(
        body, name=name, grid=(SEQ // tm,),
        in_specs=[pl.BlockSpec((tm, D_MODEL), lambda i: (i, 0)), vec, vec, vec],
        out_specs=pl.BlockSpec((tm, D_MODEL), lambda i: (i, 0)),
        out_shape=_sds((SEQ, D_MODEL), BF16),
        compiler_params=_cparams(("parallel",)),
    )(x, gain, sc, sh)


def _norm_mod_bwd(x, gain, sc, sh, dh, dres, name):
    tm = 512

    def body(x_ref, g_ref, sc_ref, sh_ref, dh_ref, dres_ref, dx_ref, dsc_ref, dsh_ref, dg_ref):
        @pl.when(pl.program_id(0) == 0)
        def _():
            dsc_ref[...] = jnp.zeros_like(dsc_ref)
            dsh_ref[...] = jnp.zeros_like(dsh_ref)
            dg_ref[...] = jnp.zeros_like(dg_ref)

        xv = x_ref[...]
        dhv = dh_ref[...]
        rs = lax.rsqrt(jnp.mean(xv * xv, axis=-1, keepdims=True) + RMS_EPS)
        xh = xv * rs
        dsc_ref[...] += jnp.sum(dhv * (xh * g_ref[...]), axis=0, keepdims=True)
        dsh_ref[...] += jnp.sum(dhv, axis=0, keepdims=True)
        dhn = dhv * (1.0 + sc_ref[...])
        dg_ref[...] += jnp.sum(dhn * xh, axis=0, keepdims=True)
        dxh = dhn * g_ref[...]
        dx_ref[...] = dres_ref[...] + rs * (dxh - xh * jnp.mean(dxh * xh, axis=-1, keepdims=True))

    vec = pl.BlockSpec((1, D_MODEL), lambda i: (0, 0))
    big = pl.BlockSpec((tm, D_MODEL), lambda i: (i, 0))
    return _pcall(
        body, name=name, grid=(SEQ // tm,),
        in_specs=[big, vec, vec, vec, big, big],
        out_specs=[big, vec, vec, vec],
        out_shape=[_sds((SEQ, D_MODEL), F32)] + [_sds((1, D_MODEL), F32)] * 3,
        compiler_params=_cparams(("arbitrary",)),
    )(x, gain, sc, sh, dh, dres)


def _mm_cols(a, wg, layer, *, n_blocks, width, tn, act_map, w_map, out_dtype, name, tm=1024):
    k = a.shape[1]
    n_tiles = n_blocks * width // tn

    def body(a_ref, w_ref, o_ref):
        o_ref[...] = _dot(a_ref[...], w_ref[...]).astype(o_ref.dtype)

    return _pcall(
        body, name=name, grid=(SEQ // tm, n_tiles),
        in_specs=[pl.BlockSpec((tm, k), lambda i, t: (i, 0)),
                  pl.BlockSpec((None, None, k, tn), lambda i, t: (w_map(t)[0], layer, 0, w_map(t)[1]))],
        out_specs=pl.BlockSpec((None, tm, tn), lambda i, t: (act_map(t)[0], i, act_map(t)[1])),
        out_shape=_sds((n_blocks, SEQ, width), out_dtype),
        compiler_params=_cparams(("parallel", "arbitrary")),
    )(a, wg)


def _mm_cols_bwd_a(pairs, *, tn, act_map, w_map, n_tiles, name, tm=1024, scatter=()):
    k = pairs[0][1].shape[2]
    n_p = len(pairs)
    n_s = len(scatter)
    n_rows = SEQ // tm

    def body(*refs):
        o_ref = refs[2 * n_p + n_s]
        if n_s:
            comm_start, comm_wait = _rs_chips(refs[2 * n_p:2 * n_p + n_s], refs[2 * n_p + n_s + 1:2 * n_p + 2 * n_s + 1],
                                              *refs[2 * n_p + 2 * n_s + 1:])
            pl.when((pl.program_id(0) == 0) & (pl.program_id(1) == 0))(comm_start)

        @pl.when(pl.program_id(1) == 0)
        def _():
            o_ref[...] = jnp.zeros_like(o_ref)

        acc = _dot_nt(refs[0][...], refs[1][...])
        for p in range(1, n_p):
            acc += _dot_nt(refs[2 * p][...], refs[2 * p + 1][...])
        o_ref[...] += acc
        if n_s:
            pl.when((pl.program_id(0) == n_rows - 1) & (pl.program_id(1) == n_tiles - 1))(comm_wait)

    in_specs, args = [], []
    for dout, wg, layer in pairs:
        in_specs.append(pl.BlockSpec((None, tm, tn), lambda i, t: (act_map(t)[0], i, act_map(t)[1])))
        in_specs.append(pl.BlockSpec((None, None, k, tn),
                                     lambda i, t, layer=layer: (w_map(t)[0], layer, 0, w_map(t)[1])))
        args += [dout, wg]
    sem = pltpu.SemaphoreType.DMA((max(n_s, 1), 3))
    res = _pcall(
        body, name=name, grid=(n_rows, n_tiles),
        in_specs=in_specs + [_ANY] * n_s,
        out_specs=[pl.BlockSpec((tm, k), lambda i, t: (i, 0))] + [_ANY] * n_s,
        out_shape=[_sds((SEQ, k), F32)] + _rs_chips_shapes(scatter),
        scratch_shapes=[sem, sem] if n_s else [],
        compiler_params=_cparams(("arbitrary", "arbitrary") if n_s else ("parallel", "arbitrary")),
    )(*args, *scatter)
    return res if n_s else res[0]


def _mm_cols_bwd_w(a, dout, *, ns, tn, act_map, w_map, n_tiles, name, tm=1024, n_out=N_CHIP):
    k = a.shape[1]

    def body(a_ref, d_ref, o_ref):
        @pl.when(pl.program_id(1) == 0)
        def _():
            o_ref[...] = jnp.zeros_like(o_ref)

        o_ref[...] += _dot_tn(a_ref[...], d_ref[...])

    return _pcall(
        body, name=name, grid=(n_tiles, SEQ // tm),
        in_specs=[pl.BlockSpec((tm, k), lambda t, i: (i, 0)),
                  pl.BlockSpec((None, tm, tn), lambda t, i: (act_map(t)[0], i, act_map(t)[1]))],
        out_specs=pl.BlockSpec((None, k, tn), lambda t, i: (w_map(t)[0], 0, w_map(t)[1])),
        out_shape=_sds((n_out, k, ns), F32),
        compiler_params=_cparams(("parallel", "arbitrary")),
    )(a, dout)


def _retile_cols(src, *, n_out, width_out, tn, src_map, dst_map, n_tiles, name):
    k = src.shape[1]

    def body(s_ref, o_ref):
        o_ref[...] = s_ref[...]

    return _pcall(
        body, name=name, grid=(n_tiles,),
        in_specs=[pl.BlockSpec((None, k, tn), lambda t: (src_map(t)[0], 0, src_map(t)[1]))],
        out_specs=pl.BlockSpec((None, k, tn), lambda t: (dst_map(t)[0], 0, dst_map(t)[1])),
        out_shape=_sds((n_out, k, width_out), src.dtype),
        compiler_params=_cparams(("parallel",)),
    )(src)


def _mm_rows(a4, wg, layer, x, gate, name, tm=1024):
    ks = a4.shape[2]
    n = wg.shape[3]

    def body(a_ref, w_ref, x_ref, g_ref, z_ref, xn_ref):
        s = pl.program_id(1)

        @pl.when(s == 0)
        def _():
            z_ref[...] = jnp.zeros_like(z_ref)

        z_ref[...] += _dot(a_ref[...], w_ref[...])

        @pl.when(s == N_CHIP - 1)
        def _():
            xn_ref[...] = x_ref[...] + g_ref[...] * z_ref[...]

    big = pl.BlockSpec((tm, n), lambda i, s: (i, 0))
    return _pcall(
        body, name=name, grid=(SEQ // tm, N_CHIP),
        in_specs=[pl.BlockSpec((None, tm, ks), lambda i, s: (s, i, 0)),
                  pl.BlockSpec((None, None, ks, n), lambda i, s: (s, layer, 0, 0)),
                  big, pl.BlockSpec((1, n), lambda i, s: (0, 0))],
        out_specs=[big, big],
        out_shape=[_sds((SEQ, n), F32), _sds((SEQ, n), F32)],
        compiler_params=_cparams(("parallel", "arbitrary")),
    )(a4, wg, x, gate)


def _gate_bwd(dx, z, gate, name):
    tm = 512

    def body(dx_ref, z_ref, g_ref, dz_ref, dg_ref):
        @pl.when(pl.program_id(0) == 0)
        def _():
            dg_ref[...] = jnp.zeros_like(dg_ref)

        dxv = dx_ref[...]
        dz_ref[...] = (dxv * g_ref[...]).astype(BF16)
        dg_ref[...] += jnp.sum(dxv * z_ref[...], axis=0, keepdims=True)

    big = pl.BlockSpec((tm, D_MODEL), lambda i: (i, 0))
    vec = pl.BlockSpec((1, D_MODEL), lambda i: (0, 0))
    return _pcall(
        body, name=name, grid=(SEQ // tm,),
        in_specs=[big, big, vec], out_specs=[big, vec],
        out_shape=[_sds((SEQ, D_MODEL), BF16), _sds((1, D_MODEL), F32)],
        compiler_params=_cparams(("arbitrary",)),
    )(dx, z, gate)


def _mm_rows_bwd_a(dz, wg, layer, name, tm=1024):
    ks, n = wg.shape[2], wg.shape[3]

    def body(dz_ref, w_ref, o_ref):
        o_ref[...] = _dot_nt(dz_ref[...], w_ref[...])

    return _pcall(
        body, name=name, grid=(SEQ // tm, N_CHIP),
        in_specs=[pl.BlockSpec((tm, n), lambda i, s: (i, 0)),
                  pl.BlockSpec((None, None, ks, n), lambda i, s: (s, layer, 0, 0))],
        out_specs=pl.BlockSpec((None, tm, ks), lambda i, s: (s, i, 0)),
        out_shape=_sds((N_CHIP, SEQ, ks), F32),
        compiler_params=_cparams(("parallel", "arbitrary")),
    )(dz, wg)


def _mm_rows_bwd_w(a4, dz, name, tm=1024):
    ks = a4.shape[2]
    n = dz.shape[1]

    def body(a_ref, dz_ref, o_ref):
        @pl.when(pl.program_id(1) == 0)
        def _():
            o_ref[...] = jnp.zeros_like(o_ref)

        o_ref[...] += _dot_tn(a_ref[...], dz_ref[...])

    return _pcall(
        body, name=name, grid=(N_CHIP, SEQ // tm),
        in_specs=[pl.BlockSpec((None, tm, ks), lambda s, i: (s, i, 0)),
                  pl.BlockSpec((tm, n), lambda s, i: (i, 0))],
        out_specs=pl.BlockSpec((None, ks, n), lambda s, i: (s, 0, 0)),
        out_shape=_sds((N_CHIP, ks, n), F32),
        compiler_params=_cparams(("parallel", "arbitrary")),
    )(a4, dz)


def _ffn_up(h, w1g, w3g, layer, name, tm=1024):
    def body(h_ref, w1_ref, w3_ref, a1_ref, a3_ref, u_ref):
        hv = h_ref[...]
        a1 = _dot(hv, w1_ref[...])
        a3 = _dot(hv, w3_ref[...])
        a1_ref[...] = a1
        a3_ref[...] = a3
        u_ref[...] = (_silu(a1) * a3).astype(BF16)

    wspec = pl.BlockSpec((None, None, D_MODEL, FFN_SHARD), lambda i, s: (s, layer, 0, 0))
    ospec = pl.BlockSpec((None, tm, FFN_SHARD), lambda i, s: (s, i, 0))
    shp = (N_CHIP, SEQ, FFN_SHARD)
    return _pcall(
        body, name=name, grid=(SEQ // tm, N_CHIP),
        in_specs=[pl.BlockSpec((tm, D_MODEL), lambda i, s: (i, 0)), wspec, wspec],
        out_specs=[ospec, ospec, ospec],
        out_shape=[_sds(shp, F32), _sds(shp, F32), _sds(shp, BF16)],
        compiler_params=_cparams(("parallel", "arbitrary")),
    )(h, w1g, w3g)


def _ffn_down_bwd(dz, w2g, layer, a1, a3, name, tm=1024):
    def body(dz_ref, w_ref, a1_ref, a3_ref, da1_ref, da3_ref):
        du = _dot_nt(dz_ref[...], w_ref[...])
        a1 = a1_ref[...]
        da1_ref[...] = (du * a3_ref[...] * _dsilu(a1)).astype(BF16)
        da3_ref[...] = (du * _silu(a1)).astype(BF16)

    blk = pl.BlockSpec((None, tm, FFN_SHARD), lambda i, s: (s, i, 0))
    shp = (N_CHIP, SEQ, FFN_SHARD)
    return _pcall(
        body, name=name, grid=(SEQ // tm, N_CHIP),
        in_specs=[pl.BlockSpec((tm, D_MODEL), lambda i, s: (i, 0)),
                  pl.BlockSpec((None, None, FFN_SHARD, D_MODEL), lambda i, s: (s, layer, 0, 0)),
                  blk, blk],
        out_specs=[blk, blk],
        out_shape=[_sds(shp, BF16), _sds(shp, BF16)],
        compiler_params=_cparams(("parallel", "arbitrary")),
    )(dz, w2g, a1, a3)


def _loss_head(y, target, name):
    tm = 512

    def body(y_ref, t_ref, dy_ref, l_ref, acc_ref):
        @pl.when(pl.program_id(0) == 0)
        def _():
            acc_ref[...] = jnp.zeros_like(acc_ref)

        err = y_ref[...] - t_ref[...]
        dy_ref[...] = err * (1.0 / D_MODEL)
        acc_ref[...] += jnp.sum(jnp.mean(err * err, axis=-1, keepdims=True), axis=0, keepdims=True)

        @pl.when(pl.program_id(0) == pl.num_programs(0) - 1)
        def _():
            l_ref[...] = 0.5 * acc_ref[...]

    big = pl.BlockSpec((tm, D_MODEL), lambda i: (i, 0))
    return _pcall(
        body, name=name, grid=(SEQ // tm,),
        in_specs=[big, big],
        out_specs=[big, pl.BlockSpec((1, 1), lambda i: (0, 0))],
        out_shape=[_sds((SEQ, D_MODEL), F32), _sds((1, 1), F32)],
        scratch_shapes=[pltpu.VMEM((1, 1), F32)],
        compiler_params=_cparams(("arbitrary",)),
    )(y, target)


def _attn_rows(base, d):
    if d == 1:
        return pl.ds(pl.multiple_of(base, ATT_BLK), ATT_BLK)
    return pl.ds(base, ATT_BLK, stride=d)


def _attn_block_index(i, d):
    nb = SEQ // (ATT_BLK * d)
    r = i // nb
    n = i % nb
    base = r + n * (ATT_BLK * d)
    pbase = jnp.maximum(base - ATT_BLK * d, r)
    return n, _attn_rows(base, d), _attn_rows(pbase, d)


def _qk_normed(x):
    rs = lax.rsqrt(jnp.mean(x * x, axis=-1, keepdims=True) + RMS_EPS)
    return x * rs, rs


def _attn_fwd(qkv9, qgain, kgain, bias, name, gather=()):
    n_g = len(gather)

    def body(*refs):
        q_ref, k_ref, v_ref, qg_ref, kg_ref, b_ref = refs[:6]
        o_ref, lse_ref = refs[6 + n_g:8 + n_g]
        qn_s, kn_s, acc_s, m_s, l_s = refs[8 + 2 * n_g:13 + 2 * n_g]
        g = pl.program_id(1)
        if n_g:
            comm_start, comm_wait = _gather_ici(refs[8 + n_g:8 + 2 * n_g], *refs[13 + 2 * n_g:])
            pl.when((pl.program_id(0) == 0) & (g == 0))(comm_start)

        @pl.when(g == 0)
        def _():
            m_s[...] = jnp.full_like(m_s, NEG)
            l_s[...] = jnp.zeros_like(l_s)
            acc_s[...] = jnp.zeros_like(acc_s)

        qn_s[...] = _qk_normed(q_ref[...])[0] * qg_ref[...]
        kn_s[...] = _qk_normed(k_ref[...])[0] * kg_ref[...]

        for gi, (_, d) in enumerate(GROUPS):
            @pl.when(g == gi)
            def _(d=d):
                def block(n, qb, kc, kp, vc, vp, m_old, l_old, acc_old):
                    sc = _dot_nt(qb, kc) * ATT_SCALE + b_ref[1]
                    sp = _dot_nt(qb, kp) * ATT_SCALE + jnp.where(n > 0, b_ref[0], NEG)
                    m_new = jnp.maximum(m_old, jnp.maximum(jnp.max(sc, axis=-1, keepdims=True),
                                                           jnp.max(sp, axis=-1, keepdims=True)))
                    alpha = jnp.exp(m_old - m_new)
                    pc = jnp.exp(sc - m_new)
                    pp = jnp.exp(sp - m_new)
                    l_new = alpha * l_old + jnp.sum(pc, axis=-1, keepdims=True) + jnp.sum(pp, axis=-1, keepdims=True)
                    acc_new = alpha * acc_old + _dot(pc.astype(BF16), vc) + _dot(pp.astype(BF16), vp)
                    return m_new, l_new, acc_new

                def it(i, carry):
                    where, loaded = [], []
                    for off in (0, ATT_PAIR):
                        n, rows, prow = _attn_block_index(i + off, d)
                        where.append(rows)
                        loaded.append((n, qn_s[rows, :].astype(BF16), kn_s[rows, :].astype(BF16),
                                       kn_s[prow, :].astype(BF16), v_ref[rows, :].astype(BF16),
                                       v_ref[prow, :].astype(BF16), m_s[rows, :], l_s[rows, :], acc_s[rows, :]))
                    results = [block(*vals) for vals in loaded]
                    for rows, (m_new, l_new, acc_new) in zip(where, results):
                        m_s[rows, :] = m_new
                        l_s[rows, :] = l_new
                        acc_s[rows, :] = acc_new
                    return carry

                lax.fori_loop(0, ATT_PAIR, it, 0)

        @pl.when(g == len(GROUPS) - 1)
        def _():
            o_ref[...] = (acc_s[...] / l_s[...]).astype(BF16)
            lse_ref[...] = m_s[...] + jnp.log(l_s[...])

        if n_g:
            pl.when((pl.program_id(0) == HEADS - 1) & (g == len(GROUPS) - 1))(comm_wait)

    def col(j):
        return pl.BlockSpec((None, SEQ, HEAD_DIM), lambda h, g: (g * 3 + j, 0, h))

    gspec = pl.BlockSpec((None, 1, HEAD_DIM), lambda h, g: (g, 0, 0))
    sem = pltpu.SemaphoreType.DMA((max(n_g, 1), 3))
    return _pcall(
        body, name=name, grid=(HEADS, len(GROUPS)),
        in_specs=[col(0), col(1), col(2), gspec, gspec,
                  pl.BlockSpec((None, None, 2, ATT_BLK, ATT_BLK), lambda h, g: (g, h, 0, 0, 0))] + [_ANY] * n_g,
        out_specs=[pl.BlockSpec((None, SEQ, HEAD_DIM), lambda h, g: (h // 2, 0, h % 2)),
                   pl.BlockSpec((None, SEQ, 1), lambda h, g: (h, 0, 0))] + [_ANY] * n_g,
        out_shape=[_sds((N_CHIP, SEQ, 2 * HEAD_DIM), BF16), _sds((HEADS, SEQ, 1), F32)]
        + [_sds(s.shape, s.dtype) for s in gather],
        input_output_aliases={6 + a: 2 + a for a in range(n_g)},
        scratch_shapes=[pltpu.VMEM((SEQ, HEAD_DIM), F32)] * 3 + [pltpu.VMEM((SEQ, 1), F32)] * 2
        + ([sem, sem] if n_g else []),
        compiler_params=_cparams(("arbitrary", "arbitrary")),
    )(qkv9, qkv9, qkv9, qgain, kgain, bias, *gather)


def _attn_bwd(qkv9, qgain, kgain, bias, do4, o4, lse, name, scatter=()):
    n_s = len(scatter)

    def body(*refs):
        q_ref, k_ref, v_ref, qg_ref, kg_ref, b_ref, do_ref, o_ref, lse_ref = refs[:9]
        dqkv_ref, dqg_ref, dkg_ref, db_ref = refs[9 + n_s:13 + n_s]
        qn_s, kn_s, dq_s, dk_s, dv_s, dl_s = refs[13 + 2 * n_s:19 + 2 * n_s]
        g = pl.program_id(1)
        if n_s:
            comm_start, comm_wait = _rs_chips(refs[9:9 + n_s], refs[13 + n_s:13 + 2 * n_s], *refs[19 + 2 * n_s:])
            pl.when((pl.program_id(0) == 0) & (g == 0))(comm_start)
        qh, rq = _qk_normed(q_ref[...])
        kh, rk = _qk_normed(k_ref[...])
        qn_s[...] = qh * qg_ref[...]
        kn_s[...] = kh * kg_ref[...]
        dl_s[...] = jnp.sum(do_ref[...] * o_ref[...].astype(F32), axis=-1, keepdims=True)
        dk_s[...] = jnp.zeros_like(dk_s)
        dv_s[...] = jnp.zeros_like(dv_s)
        db_ref[...] = jnp.zeros_like(db_ref)

        for gi, (_, d) in enumerate(GROUPS):
            @pl.when(g == gi)
            def _(d=d):
                def block(n, qb, kc, kp, vc, vp, dob, lse_b, dl):
                    sc = _dot_nt(qb, kc) * ATT_SCALE + b_ref[1]
                    sp = _dot_nt(qb, kp) * ATT_SCALE + jnp.where(n > 0, b_ref[0], NEG)
                    pc = jnp.exp(sc - lse_b)
                    pp = jnp.exp(sp - lse_b)
                    dsc = pc * (_dot_nt(dob, vc) - dl)
                    dsp = pp * (_dot_nt(dob, vp) - dl)
                    dsc16 = dsc.astype(BF16)
                    dsp16 = dsp.astype(BF16)
                    dq = (_dot(dsc16, kc) + _dot(dsp16, kp)) * ATT_SCALE
                    return (dsc, dsp, dq, _dot_tn(dsc16, qb) * ATT_SCALE, _dot_tn(dsp16, qb) * ATT_SCALE,
                            _dot_tn(pc.astype(BF16), dob), _dot_tn(pp.astype(BF16), dob))

                def it(i, carry):
                    where, loaded, old = [], [], []
                    for off in (0, ATT_PAIR):
                        n, rows, prow = _attn_block_index(i + off, d)
                        where.append((rows, prow))
                        loaded.append((n, qn_s[rows, :].astype(BF16), kn_s[rows, :].astype(BF16),
                                       kn_s[prow, :].astype(BF16), v_ref[rows, :].astype(BF16),
                                       v_ref[prow, :].astype(BF16), do_ref[rows, :].astype(BF16),
                                       lse_ref[rows, :], dl_s[rows, :]))
                        old.append((dk_s[rows, :], dk_s[prow, :], dv_s[rows, :], dv_s[prow, :]))
                    results = [block(*vals) for vals in loaded]
                    db_ref[1] += results[0][0] + results[1][0]
                    db_ref[0] += results[0][1] + results[1][1]
                    for (rows, prow), (dk_c, dk_p, dv_c, dv_p), (_, _, dq, dkc, dkp, dvc, dvp) in zip(where, old, results):
                        dq_s[rows, :] = dq
                        dk_s[prow, :] = dk_p + dkp
                        dv_s[prow, :] = dv_p + dvp
                        dk_s[rows, :] = dk_c + dkc
                        dv_s[rows, :] = dv_c + dvc
                    return carry

                lax.fori_loop(0, ATT_PAIR, it, 0)

        def norm_bwd(dn, xh, rs, gain):
            dgain = jnp.sum(dn * xh, axis=0, keepdims=True)
            dxh = dn * gain
            return rs * (dxh - xh * jnp.mean(dxh * xh, axis=-1, keepdims=True)), dgain

        dq, dqg = norm_bwd(dq_s[...], qh, rq, qg_ref[...])
        dk, dkg = norm_bwd(dk_s[...], kh, rk, kg_ref[...])
        dqkv_ref[0] = dq.astype(BF16)
        dqkv_ref[1] = dk.astype(BF16)
        dqkv_ref[2] = dv_s[...].astype(BF16)
        dqg_ref[...] = dqg
        dkg_ref[...] = dkg
        if n_s:
            pl.when((pl.program_id(0) == HEADS - 1) & (g == len(GROUPS) - 1))(comm_wait)

    def col(j):
        return pl.BlockSpec((None, SEQ, HEAD_DIM), lambda h, g: (g * 3 + j, 0, h))

    gspec = pl.BlockSpec((None, 1, HEAD_DIM), lambda h, g: (g, 0, 0))
    bspec = pl.BlockSpec((None, None, 2, ATT_BLK, ATT_BLK), lambda h, g: (g, h, 0, 0, 0))
    hcol = pl.BlockSpec((None, SEQ, HEAD_DIM), lambda h, g: (h // 2, 0, h % 2))
    dgspec = pl.BlockSpec((None, None, 1, HEAD_DIM), lambda h, g: (h, g, 0, 0))
    ng = len(GROUPS)
    sem = pltpu.SemaphoreType.DMA((max(n_s, 1), 3))
    return _pcall(
        body, name=name, grid=(HEADS, ng),
        in_specs=[col(0), col(1), col(2), gspec, gspec, bspec, hcol, hcol,
                  pl.BlockSpec((None, SEQ, 1), lambda h, g: (h, 0, 0))] + [_ANY] * n_s,
        out_specs=[pl.BlockSpec((None, 3, SEQ, HEAD_DIM), lambda h, g: (g, 0, 0, h)), dgspec, dgspec, bspec]
        + [_ANY] * n_s,
        out_shape=[_sds((ng, 3, SEQ, D_MODEL), BF16), _sds((HEADS, ng, 1, HEAD_DIM), F32),
                   _sds((HEADS, ng, 1, HEAD_DIM), F32), _sds((ng, HEADS, 2, ATT_BLK, ATT_BLK), F32)]
        + _rs_chips_shapes(scatter),
        scratch_shapes=[pltpu.VMEM((SEQ, HEAD_DIM), F32)] * 5 + [pltpu.VMEM((SEQ, 1), F32)]
        + ([sem, sem] if n_s else []),
        compiler_params=_cparams(("arbitrary", "arbitrary")),
    )(qkv9, qkv9, qkv9, qgain, kgain, bias, do4, o4, lse, *scatter)


def _relbias_bwd(dbias, bucket_idx, name):
    ng = len(GROUPS)

    def body(db_ref, idx_ref, o_ref):
        lane = lax.broadcasted_iota(jnp.int32, (HEADS, 128), 1)
        acc = jnp.zeros((HEADS, 128), F32)
        for g in range(ng):
            dbg = db_ref[g]
            idx = idx_ref[g]
            for b in range(NUM_BUCKETS):
                sel = jnp.where((idx == b)[None], dbg, 0.0)
                part = jnp.sum(jnp.sum(sel, axis=1), axis=1)
                val = jnp.sum(part, axis=-1, keepdims=True)
                acc = jnp.where(lane == g * NUM_BUCKETS + b, val, acc)
        o_ref[...] = acc

    return _pcall(body, name=name, out_shape=_sds((HEADS, 128), F32), compiler_params=_cparams())(dbias, bucket_idx)


def _scan16(x, reverse=False):
    row = lax.broadcasted_iota(jnp.int32, x.shape, 0)
    for sh in (1, 2, 4, 8):
        if reverse:
            x = x + jnp.where(row < HG_SUB - sh, pltpu.roll(x, HG_SUB - sh, 0), 0.0)
        else:
            x = x + jnp.where(row >= sh, pltpu.roll(x, sh, 0), 0.0)
    return x


def _hgrn_gates(qr, fr, lbv):
    q = _silu(qr)
    sig = _sigmoid(fr)
    fg = lbv + (1.0 - lbv) * sig
    lf = jnp.log(fg)
    gcum = _scan16(lf)
    glast = jnp.sum(lf, axis=0, keepdims=True)
    return q, sig, fg, 1.0 - fg, gcum, glast


def _hgrn_intra(q, k, gcum, tri):
    e = jnp.exp(jnp.where(tri, gcum[:, None, :] - gcum[None, :, :], NEG))
    a = jnp.sum(q[:, None, :] * k[None, :, :] * e, axis=-1, keepdims=True)
    return e, a


def _hgrn_fwd(proj4, lb, gain, name):
    nsub = HG_TC // HG_SUB
    wide = HG_HP * HEAD_DIM

    def body(p_ref, lb_ref, gn_ref, o_ref, y_ref, st_ref, state_s):
        @pl.when(pl.program_id(1) == 0)
        def _():
            state_s[...] = jnp.zeros_like(state_s)

        gnv = gn_ref[...]
        shp = (HG_SUB, HG_SUB, HEAD_DIM)
        tri = lax.broadcasted_iota(jnp.int32, shp, 0) >= lax.broadcasted_iota(jnp.int32, shp, 1)

        def head(qr, fr, vv, gr, lbv, st):
            q, _, _, k, gcum, glast = _hgrn_gates(qr, fr, lbv)
            _, a = _hgrn_intra(q, k, gcum, tri)
            o = jnp.sum(a * vv[None, :, :], axis=1) + _dot_nt((q * jnp.exp(gcum)).astype(BF16), st.astype(BF16))
            kg = k * jnp.exp(glast - gcum)
            st_new = st * jnp.exp(glast) + _dot_tn(vv.astype(BF16), kg.astype(BF16))
            rs = lax.rsqrt(jnp.mean(o * o, axis=-1, keepdims=True) + RMS_EPS)
            return o, (o * rs * gnv * _silu(gr)).astype(BF16), st_new

        def it(i, carry):
            rows = pl.ds(pl.multiple_of(i * HG_SUB, HG_SUB), HG_SUB)
            loaded = []
            for hh in range(HG_HP):
                lanes = pl.ds(hh * HEAD_DIM, HEAD_DIM)
                loaded.append(([p_ref[j, rows, lanes] for j in range(4)], lb_ref[:, lanes], state_s[hh]))
            results = [head(blk[0], blk[1], blk[2], blk[3], lbv, st) for blk, lbv, st in loaded]
            for hh, ((_, _, st), (o, y, st_new)) in enumerate(zip(loaded, results)):
                lanes = pl.ds(hh * HEAD_DIM, HEAD_DIM)
                st_ref[hh, i] = st.astype(BF16)
                state_s[hh] = st_new
                o_ref[rows, lanes] = o
                y_ref[rows, lanes] = y
            return carry

        lax.fori_loop(0, nsub, it, 0)

    return _pcall(
        body, name=name, grid=(HEADS // HG_HP, SEQ // HG_TC),
        in_specs=[pl.BlockSpec((4, HG_TC, wide), lambda h, j: (0, j, h)),
                  pl.BlockSpec((1, wide), lambda h, j: (0, h)),
                  pl.BlockSpec((1, HEAD_DIM), lambda h, j: (0, 0))],
        out_specs=[pl.BlockSpec((HG_TC, wide), lambda h, j: (j, h)),
                   pl.BlockSpec((None, HG_TC, wide), lambda h, j: (h, j, 0)),
                   pl.BlockSpec((HG_HP, nsub, HEAD_DIM, HEAD_DIM), lambda h, j: (h, j, 0, 0))],
        out_shape=[_sds((SEQ, D_MODEL), F32), _sds((N_CHIP, SEQ, 2 * HEAD_DIM), BF16),
                   _sds((HEADS, SEQ // HG_SUB, HEAD_DIM, HEAD_DIM), BF16)],
        scratch_shapes=[pltpu.VMEM((HG_HP, HEAD_DIM, HEAD_DIM), F32)],
        compiler_params=_cparams(("parallel", "arbitrary")),
    )(proj4, lb, gain)


def _hgrn_bwd(proj4, lb, gain, o_raw, dy4, states, name):
    nsub = HG_TC // HG_SUB
    nt = SEQ // HG_TC
    wide = HG_HP * HEAD_DIM

    def body(p_ref, lb_ref, gn_ref, o_ref, dy_ref, st_ref, dp_ref, dlb_ref, dgn_ref, dst_s):
        @pl.when(pl.program_id(1) == 0)
        def _():
            dst_s[...] = jnp.zeros_like(dst_s)
            dlb_ref[...] = jnp.zeros_like(dlb_ref)
            dgn_ref[...] = jnp.zeros_like(dgn_ref)

        gnv = gn_ref[...]
        shp = (HG_SUB, HG_SUB, HEAD_DIM)
        tri = lax.broadcasted_iota(jnp.int32, shp, 0) >= lax.broadcasted_iota(jnp.int32, shp, 1)

        def head(qr, fr, vv, gr, o, dy, lbv, st0, dst):
            q, sig, fg, k, gcum, glast = _hgrn_gates(qr, fr, lbv)
            rs = lax.rsqrt(jnp.mean(o * o, axis=-1, keepdims=True) + RMS_EPS)
            oh = o * rs
            don = dy * _silu(gr)
            dgn = jnp.sum(don * oh, axis=0, keepdims=True)
            dgr = dy * oh * gnv * _dsilu(gr)
            doh = don * gnv
            do = rs * (doh - oh * jnp.mean(doh * oh, axis=-1, keepdims=True))
            dst16 = dst.astype(BF16)
            do16 = do.astype(BF16)
            eg = jnp.exp(gcum)
            eb = jnp.exp(glast - gcum)
            e, a = _hgrn_intra(q, k, gcum, tri)
            da = jnp.sum(do[:, None, :] * vv[None, :, :], axis=-1, keepdims=True)
            dae = da * e
            dq = jnp.sum(dae * k[None, :, :], axis=1) + eg * _dot(do16, st0)
            dk_state = eb * _dot(vv.astype(BF16), dst16)
            dk = jnp.sum(dae * q[:, None, :], axis=0) + dk_state
            dv = jnp.sum(a * do[:, None, :], axis=0) + _dot_nt((k * eb).astype(BF16), dst16)
            eglast = jnp.exp(glast)
            dst_new = dst * eglast + _dot_tn(do16, (q * eg).astype(BF16))
            dglast = jnp.sum(k * dk_state, axis=0, keepdims=True) \
                + eglast * jnp.sum(dst * st0.astype(F32), axis=0, keepdims=True)
            dlf = _scan16(q * dq - k * dk, reverse=True) + dglast
            dfg = dlf / fg - dk
            dlb = jnp.sum(dfg * (1.0 - sig), axis=0, keepdims=True)
            dproj = ((dq * _dsilu(qr)).astype(BF16), (dfg * (1.0 - lbv) * sig * (1.0 - sig)).astype(BF16),
                     dv.astype(BF16), dgr.astype(BF16))
            return dproj, dst_new, dlb, dgn

        def it(ii, carry):
            i = nsub - 1 - ii
            rows = pl.ds(pl.multiple_of(i * HG_SUB, HG_SUB), HG_SUB)
            results = []
            for hh in range(HG_HP):
                lanes = pl.ds(hh * HEAD_DIM, HEAD_DIM)
                blk = [p_ref[j, rows, lanes] for j in range(4)]
                results.append(head(blk[0], blk[1], blk[2], blk[3], o_ref[rows, lanes], dy_ref[rows, lanes],
                                    lb_ref[:, lanes], st_ref[hh, i], dst_s[hh]))
            new_carry = []
            for hh, (dproj, dst_new, dlb, dgn) in enumerate(results):
                lanes = pl.ds(hh * HEAD_DIM, HEAD_DIM)
                dst_s[hh] = dst_new
                for j in range(4):
                    dp_ref[j, rows, lanes] = dproj[j]
                new_carry.append((carry[hh][0] + dlb, carry[hh][1] + dgn))
            return tuple(new_carry)

        zero = jnp.zeros((1, HEAD_DIM), F32)
        sums = lax.fori_loop(0, nsub, it, tuple((zero, zero) for _ in range(HG_HP)))
        for hh in range(HG_HP):
            dlb_ref[hh] += sums[hh][0]
            dgn_ref[hh] += sums[hh][1]

    vspec = pl.BlockSpec((HG_HP, 1, HEAD_DIM), lambda h, j: (h, 0, 0))
    return _pcall(
        body, name=name, grid=(HEADS // HG_HP, nt),
        in_specs=[pl.BlockSpec((4, HG_TC, wide), lambda h, j: (0, nt - 1 - j, h)),
                  pl.BlockSpec((1, wide), lambda h, j: (0, h)),
                  pl.BlockSpec((1, HEAD_DIM), lambda h, j: (0, 0)),
                  pl.BlockSpec((HG_TC, wide), lambda h, j: (nt - 1 - j, h)),
                  pl.BlockSpec((None, HG_TC, wide), lambda h, j: (h, nt - 1 - j, 0)),
                  pl.BlockSpec((HG_HP, nsub, HEAD_DIM, HEAD_DIM), lambda h, j: (h, nt - 1 - j, 0, 0))],
        out_specs=[pl.BlockSpec((4, HG_TC, wide), lambda h, j: (0, nt - 1 - j, h)), vspec, vspec],
        out_shape=[_sds((4, SEQ, D_MODEL), BF16), _sds((HEADS, 1, HEAD_DIM), F32), _sds((HEADS, 1, HEAD_DIM), F32)],
        scratch_shapes=[pltpu.VMEM((HG_HP, HEAD_DIM, HEAD_DIM), F32)],
        compiler_params=_cparams(("parallel", "arbitrary")),
    )(proj4, lb, gain, o_raw, dy4, states)


def _t5_bucket(dist):
    n = np.asarray(dist, dtype=np.int64)
    max_exact = NUM_BUCKETS // 2
    large = max_exact + (np.log(np.maximum(n, 1) / max_exact) / np.log(MAX_DISTANCE / max_exact)
                         * (NUM_BUCKETS - max_exact)).astype(np.int64)
    large = np.minimum(large, NUM_BUCKETS - 1)
    return np.where(n < max_exact, n, large).astype(np.int32)


def _bias_tables():
    qi = np.arange(ATT_BLK)[:, None]
    ki = np.arange(ATT_BLK)[None, :]
    steps = (ATT_BLK + qi - ki, qi - ki)
    idx = np.zeros((len(GROUPS), 2, ATT_BLK, ATT_BLK), np.int32)
    for g, (_, d) in enumerate(GROUPS):
        for p, j in enumerate(steps):
            valid = (j >= 0) & (j <= ATT_BLK)
            idx[g, p] = np.where(valid, _t5_bucket(np.clip(j, 0, ATT_BLK) * d), -1)
    return idx


def _attn_bias(rel_bias, name):
    idx = _bias_tables()
    ng = len(GROUPS)
    buckets = [sorted(set(idx[g][idx[g] >= 0].tolist())) for g in range(ng)]

    def body(rb_ref, idx_ref, o_ref):
        h = pl.program_id(0)
        for g in range(ng):
            ig = idx_ref[g]
            acc = jnp.full(ig.shape, NEG, F32)
            for b in buckets[g]:
                acc = jnp.where(ig == b, rb_ref[b, g * HEADS + h], acc)
            o_ref[g] = acc

    return _pcall(
        body, name=name, grid=(HEADS,),
        in_specs=[pl.BlockSpec(memory_space=pltpu.SMEM),
                  pl.BlockSpec((ng, 2, ATT_BLK, ATT_BLK), lambda h: (0, 0, 0, 0))],
        out_specs=pl.BlockSpec((ng, None, 2, ATT_BLK, ATT_BLK), lambda h: (0, h, 0, 0, 0)),
        out_shape=_sds((ng, HEADS, 2, ATT_BLK, ATT_BLK), F32),
        compiler_params=_cparams(("parallel",)),
    )(rel_bias, jnp.asarray(idx))


ADA_SHARD = 6 * D_MODEL // N_CHIP
ADA_TN = 512


def _ada_fwd(c_all, ada_w, ada_b_cols, name):
    def body(c_ref, w_ref, b_ref, o_ref):
        ca = _silu(c_ref[...]).astype(BF16)
        o_ref[...] = _dot(ca, w_ref[...].astype(BF16)) + b_ref[...]

    return _pcall(
        body, name=name, grid=(DEPTH, ADA_SHARD // ADA_TN),
        in_specs=[pl.BlockSpec((N_DEV, D_MODEL), lambda l, j: (0, 0)),
                  pl.BlockSpec((None, D_MODEL, ADA_TN), lambda l, j: (l, 0, j)),
                  pl.BlockSpec((None, 1, ADA_TN), lambda l, j: (l, 0, j))],
        out_specs=pl.BlockSpec((None, N_DEV, ADA_TN), lambda l, j: (l, 0, j)),
        out_shape=_sds((DEPTH, N_DEV, ADA_SHARD), F32),
        compiler_params=_cparams(("parallel", "parallel")),
    )(c_all, ada_w, ada_b_cols)


def _ada_bwd(c_all, dmod_cols, name):
    def body(c_ref, d_ref, o_ref):
        ca = _silu(c_ref[...]).astype(BF16)
        o_ref[...] = _dot_tn(ca, d_ref[...].astype(BF16))

    return _pcall(
        body, name=name, grid=(DEPTH, ADA_SHARD // ADA_TN),
        in_specs=[pl.BlockSpec((N_DEV, D_MODEL), lambda l, j: (0, 0)),
                  pl.BlockSpec((None, N_DEV, ADA_TN), lambda l, j: (l, 0, j))],
        out_specs=pl.BlockSpec((None, D_MODEL, ADA_TN), lambda l, j: (l, 0, j)),
        out_shape=_sds((DEPTH, D_MODEL, ADA_SHARD), F32),
        compiler_params=_cparams(("parallel", "parallel")),
    )(c_all, dmod_cols)


def _lower_bounds(logits, name):
    def body(l_ref, o_ref):
        l0 = l_ref[0:1, :]
        l1 = l_ref[1:2, :]
        mx = jnp.maximum(l0, l1)
        e0 = jnp.exp(l0 - mx)
        e1 = jnp.exp(l1 - mx)
        p0 = e0 / (e0 + e1)
        p1 = e1 / (e0 + e1)
        o_ref[0:1, :] = p0 - p0
        o_ref[1:2, :] = (p0 + p1) - p0

    return _pcall(body, name=name, out_shape=_sds((DEPTH, D_MODEL), F32), compiler_params=_cparams())(logits)


_R_DMOD = 0
_R_NMIX = 96
_R_NFFN = 112
_R_QG = 128
_R_KG = 152
_R_GN = 176
_R_LB = 184
_R_RB = 192
SMALL_ROWS = 200


def _small_totals(gathered, logits8, name):
    ng = len(GROUPS)

    def body(g_ref, l_ref, main_ref, gains_ref, dlb_ref, rb_ref):
        tot = g_ref[0]
        for dev in range(1, N_DEV):
            tot = tot + g_ref[dev]
        main_ref[...] = tot[0:_R_QG]
        gains_ref[...] = jnp.zeros_like(gains_ref)
        for g in range(ng):
            gains_ref[g:g + 1, :] = jnp.sum(tot[_R_QG + 8 * g:_R_QG + 8 * g + 8], axis=0, keepdims=True)
            gains_ref[ng + g:ng + g + 1, :] = jnp.sum(tot[_R_KG + 8 * g:_R_KG + 8 * g + 8], axis=0, keepdims=True)
        gains_ref[2 * ng:2 * ng + 1, :] = jnp.sum(tot[_R_GN:_R_GN + 8], axis=0, keepdims=True)
        rb_ref[...] = tot[_R_RB:_R_RB + 8]
        dlb1 = tot[_R_LB:_R_LB + 8]
        l0 = l_ref[0]
        l1 = l_ref[1]
        mx = jnp.maximum(l0, l1)
        e0 = jnp.exp(l0 - mx)
        e1 = jnp.exp(l1 - mx)
        p0 = e0 / (e0 + e1)
        p1 = e1 / (e0 + e1)
        dlb_ref[0] = -p0 * p1 * dlb1
        dlb_ref[1] = p1 * (1.0 - p1) * dlb1

    return _pcall(
        body, name=name,
        out_shape=[_sds((_R_QG, 128), F32), _sds((8, 128), F32), _sds((DEPTH, 8, 128), F32), _sds((8, 128), F32)],
        compiler_params=_cparams(),
    )(gathered, logits8)


def _row_tile(rows):
    return 128 if rows % 128 == 0 else rows


def _adamw(w, grads, m, v, name):
    nl, r, cdim = w.shape
    tr = _row_tile(r)

    def body(*refs):
        g_refs = refs[:nl]
        w_ref, m_ref, v_ref, go_ref, d_ref, mo_ref, vo_ref = refs[nl:]

        def step(g):
            m2 = ADAM_B1 * m_ref[...] + (1.0 - ADAM_B1) * g
            v2 = ADAM_B2 * v_ref[...] + (1.0 - ADAM_B2) * (g * g)
            m_hat = m2 / (1.0 - ADAM_B1 ** ADAM_STEP)
            v_hat = v2 / (1.0 - ADAM_B2 ** ADAM_STEP)
            go_ref[...] = g
            d_ref[...] = -ADAM_LR * (m_hat / (jnp.sqrt(v_hat) + ADAM_EPS) + ADAM_WD * w_ref[...])
            mo_ref[...] = m2
            vo_ref[...] = v2

        if nl == 1:
            step(g_refs[0][...])
        else:
            for layer in range(nl):
                @pl.when(pl.program_id(0) == layer)
                def _(layer=layer):
                    step(g_refs[layer][...])

    big = pl.BlockSpec((None, tr, cdim), lambda l, i: (l, i, 0))
    g_specs = [pl.BlockSpec((tr, cdim), lambda l, i, layer=layer: (jnp.where(l == layer, i, 0), 0))
               for layer in range(nl)]
    shp = _sds((nl, r, cdim), F32)
    return _pcall(
        body, name=name, grid=(nl, r // tr),
        in_specs=g_specs + [big, big, big],
        out_specs=[big, big, big, big],
        out_shape=[shp, shp, shp, shp],
        compiler_params=_cparams(("parallel", "parallel")),
    )(*grads, w, m, v)


def _cast_bf16(place, w, name):
    nl, r, cdim = w.shape
    tr = _row_tile(r)

    def body(place_ref, w_ref, o_ref):
        o_ref[...] = w_ref[...].astype(BF16)

    return _pcall(
        body, name=name,
        grid_spec=pltpu.PrefetchScalarGridSpec(
            num_scalar_prefetch=1, grid=(nl, r // tr),
            in_specs=[pl.BlockSpec((None, tr, cdim), lambda l, i, place_ref: (l, i, 0))],
            out_specs=pl.BlockSpec((None, None, tr, cdim), lambda l, i, place_ref: (place_ref[1], l, i, 0))),
        out_shape=_sds((N_CHIP, nl, r, cdim), BF16),
        compiler_params=_cparams(("parallel", "parallel")),
    )(place, w)


def _rs_add_cast(place, grad, recv, name):
    _, k, n = grad.shape
    kh = k // 2
    tr = _row_tile(kh)
    nb = kh // tr

    def body(place_ref, g_ref, r_ref, o_ref):
        o_ref[...] = (g_ref[...] + r_ref[...]).astype(BF16)

    half = pl.BlockSpec((None, tr, n), lambda s, i, place_ref: (s, i, 0))
    return _pcall(
        body, name=name,
        grid_spec=pltpu.PrefetchScalarGridSpec(
            num_scalar_prefetch=1, grid=(N_CHIP, nb),
            in_specs=[pl.BlockSpec((None, tr, n), lambda s, i, place_ref: (s, place_ref[0] * nb + i, 0)), half],
            out_specs=half),
        out_shape=_sds((N_CHIP, kh, n), BF16),
        compiler_params=_cparams(("parallel", "parallel")),
    )(place, grad, recv)


def _rs_sum4(place, parts, got, name):
    _, kh, n = parts.shape
    tr = _row_tile(kh)
    nb = kh // tr

    def body(place_ref, p_ref, g_ref, o_ref):
        acc = p_ref[...].astype(F32)
        for j in range(N_CHIP - 1):
            acc = acc + g_ref[j].astype(F32)
        o_ref[...] = acc

    return _pcall(
        body, name=name,
        grid_spec=pltpu.PrefetchScalarGridSpec(
            num_scalar_prefetch=1, grid=(nb,),
            in_specs=[pl.BlockSpec((None, tr, n), lambda i, place_ref: (place_ref[1], i, 0)),
                      pl.BlockSpec((N_CHIP - 1, tr, n), lambda i, place_ref: (0, i, 0))],
            out_specs=pl.BlockSpec((tr, n), lambda i, place_ref: (place_ref[0] * nb + i, 0))),
        out_shape=_sds((2 * kh, n), F32),
        compiler_params=_cparams(("parallel",)),
    )(place, parts, got)


_ANY = pl.BlockSpec(memory_space=pl.ANY)


def _position():
    return lax.axis_index("x"), lax.axis_index("y"), lax.axis_index("c")


def _other_chips(x, y):
    return [(1 - x, y), (x, 1 - y), (1 - x, 1 - y)]


def _remote(src, dst, send_sem, recv_sem, to):
    return pltpu.make_async_remote_copy(src_ref=src, dst_ref=dst, send_sem=send_sem, recv_sem=recv_sem,
                                        device_id=to, device_id_type=MESH)


def _small_allgather(v, name):
    r = v.shape[0]

    def body(x_ref, out_ref, send_sems, recv_sems, local_sem):
        x, y, c = _position()
        me, sibling = (x, y, c), (x, y, 1 - c)
        chips = _other_chips(x, y)

        def slab(px, py, pc):
            return out_ref.at[4 * px + 2 * py + pc]

        def copy(k, block, to, src=None):
            return _remote(slab(*block) if src is None else src, slab(*block), send_sems.at[k], recv_sems.at[k], to)

        mine = pltpu.make_async_copy(x_ref, slab(*me), local_sem)
        mine.start()
        first = [copy(0, me, sibling, src=x_ref)]
        first += [copy(1 + j, me, (*chip, c), src=x_ref) for j, chip in enumerate(chips)]
        for cp in first:
            cp.start()
        passed = [copy(4 + j, (*chip, c), sibling) for j, chip in enumerate(chips)]
        for j, chip in enumerate(chips):
            copy(1 + j, (*chip, c), me).wait_recv()
            passed[j].start()
        copy(0, sibling, me).wait_recv()
        for j, chip in enumerate(chips):
            copy(4 + j, (*chip, 1 - c), me).wait_recv()
        for cp in first + passed:
            cp.wait_send()
        mine.wait()

    return _pcall(
        body, name=name,
        out_shape=_sds((N_DEV, r, 128), F32),
        in_specs=[pl.BlockSpec(memory_space=pltpu.VMEM)],
        out_specs=pl.BlockSpec(memory_space=pltpu.VMEM),
        scratch_shapes=[pltpu.SemaphoreType.DMA((7,)), pltpu.SemaphoreType.DMA((7,)), pltpu.SemaphoreType.DMA],
        compiler_params=_cparams(),
    )(v)


def _half_rows(core, kh):
    return pl.ds(pl.multiple_of(core * kh, 8), kh)


def _slab_half(ref, chip, core):
    return ref.at[chip, :, _half_rows(core, ref.shape[2] // 2), :]


def _gather_ici(out, send_sems, recv_sems):
    def copies():
        x, y, c = _position()
        for a in range(len(out)):
            for j, (px, py) in enumerate(_other_chips(x, y)):
                mine = _slab_half(out[a], 2 * x + y, c)
                landed = _slab_half(out[a], 2 * px + py, c)
                yield (_remote(mine, mine, send_sems.at[a, j], recv_sems.at[a, j], (px, py, c)),
                       _remote(landed, landed, send_sems.at[a, j], recv_sems.at[a, j], (px, py, c)))

    def start():
        for send, _ in copies():
            send.start()

    def wait():
        for send, recv in copies():
            recv.wait_recv()
            send.wait_send()

    return start, wait


def _gather_d2d(out, send_sems, recv_sems):
    def copies():
        x, y, c = _position()
        for a in range(len(out)):
            for j, (px, py) in enumerate(_other_chips(x, y)):
                landed = _slab_half(out[a], 2 * px + py, c)
                other = _slab_half(out[a], 2 * px + py, 1 - c)
                yield (_remote(landed, landed, send_sems.at[a, j], recv_sems.at[a, j], (x, y, 1 - c)),
                       _remote(other, other, send_sems.at[a, j], recv_sems.at[a, j], (x, y, 1 - c)))

    def start():
        for send, _ in copies():
            send.start()

    def wait():
        for send, recv in copies():
            recv.wait_recv()
            send.wait_send()

    return start, wait


def _gather_weights(slabs, name, ici=True):
    n = len(slabs)

    def body(*refs):
        out = refs[n:2 * n]
        sems = refs[2 * n:]
        if ici:
            start, wait = _gather_ici(out, sems[2], sems[3])
            start()
            wait()
        start, wait = _gather_d2d(out, sems[0], sems[1])
        start()
        wait()

    sem = pltpu.SemaphoreType.DMA((n, 3))
    return _pcall(
        body, name=name,
        out_shape=[_sds(s.shape, BF16) for s in slabs],
        in_specs=[_ANY] * n, out_specs=[_ANY] * n,
        input_output_aliases={a: a for a in range(n)},
        scratch_shapes=[sem, sem] + ([sem, sem] if ici else []),
        compiler_params=_cparams(),
    )(*slabs)


def _rs_exchange_halves(grads, name):
    n = len(grads)

    def body(*refs):
        g = refs[:n]
        out = refs[n:2 * n]
        send_sems, recv_sems = refs[2 * n:]
        x, y, c = _position()
        copies = []
        for a in range(n):
            kh = g[a].shape[1] // 2
            cp = _remote(g[a].at[:, _half_rows(1 - c, kh), :], out[a], send_sems.at[a], recv_sems.at[a], (x, y, 1 - c))
            cp.start()
            copies.append(cp)
        for cp in copies:
            cp.wait()

    return _pcall(
        body, name=name,
        out_shape=[_sds((N_CHIP, g.shape[1] // 2, g.shape[2]), F32) for g in grads],
        in_specs=[_ANY] * n, out_specs=[_ANY] * n,
        scratch_shapes=[pltpu.SemaphoreType.DMA((n,)), pltpu.SemaphoreType.DMA((n,))],
        compiler_params=_cparams(),
    )(*grads)


def _rs_chips(parts, out, send_sems, recv_sems):
    def copies():
        x, y, c = _position()
        for a in range(len(parts)):
            for j, (px, py) in enumerate(_other_chips(x, y)):
                got = out[a].at[j]
                yield (_remote(parts[a].at[2 * px + py], got, send_sems.at[a, j], recv_sems.at[a, j], (px, py, c)),
                       _remote(got, got, send_sems.at[a, j], recv_sems.at[a, j], (px, py, c)))

    def start():
        for send, _ in copies():
            send.start()

    def wait():
        for send, recv in copies():
            recv.wait_recv()
            send.wait_send()

    return start, wait


def _rs_chips_shapes(parts):
    return [_sds((N_CHIP - 1,) + p.shape[1:], BF16) for p in parts]


def _rs_join_halves(fulls, name):
    n = len(fulls)

    def body(*refs):
        out = refs[n:2 * n]
        send_sems, recv_sems = refs[2 * n:]
        x, y, c = _position()
        copies = []
        for a in range(n):
            kh = out[a].shape[0] // 2
            mine = out[a].at[_half_rows(c, kh), :]
            cp = _remote(mine, mine, send_sems.at[a], recv_sems.at[a], (x, y, 1 - c))
            cp.start()
            copies.append(cp)
        for a in range(n):
            kh = out[a].shape[0] // 2
            theirs = out[a].at[_half_rows(1 - c, kh), :]
            _remote(theirs, theirs, send_sems.at[a], recv_sems.at[a], (x, y, 1 - c)).wait_recv()
        for cp in copies:
            cp.wait_send()

    return _pcall(
        body, name=name,
        out_shape=[_sds(f.shape, F32) for f in fulls],
        in_specs=[_ANY] * n, out_specs=[_ANY] * n,
        input_output_aliases={a: a for a in range(n)},
        scratch_shapes=[pltpu.SemaphoreType.DMA((n,)), pltpu.SemaphoreType.DMA((n,))],
        compiler_params=_cparams(),
    )(*fulls)


_SMALL_ORDER = ("rel_bias", "ada_b", "norm_mix", "norm_ffn", "attn_q_gain", "attn_k_gain", "hgrn_gnorm",
                "hgrn_lower_bounds")
_WEIGHT_ORDER = ("rel_bias", "ada_w", "ada_b", "norm_mix", "norm_ffn", "attn_w_qkv", "attn_w_out", "attn_q_gain",
                 "attn_k_gain", "hgrn_w_in", "hgrn_w_out", "hgrn_gnorm", "hgrn_lower_bounds", "ffn_w1", "ffn_w3",
                 "ffn_w2")


def _qkv_group_map(t):
    return t // 4, t % 4


def _qkv_chip_map(t):
    return t // 9, t % 9


def _hin_map(t):
    return t // 2, t % 2


def _block_map(t):
    return t, 0


def _pack_rows(parts):
    return jnp.concatenate([p.reshape(-1, 128) for p in parts], axis=0)


def kernel(x, c, rel_bias, ada_w, ada_b, norm_mix, norm_ffn, attn_w_qkv, attn_w_out, attn_q_gain, attn_k_gain, hgrn_w_in, hgrn_w_out, hgrn_gnorm, hgrn_lower_bounds, ffn_w1, ffn_w3, ffn_w2, loss_target, m_rel_bias, m_ada_w, m_ada_b, m_norm_mix, m_norm_ffn, m_attn_w_qkv, m_attn_w_out, m_attn_q_gain, m_attn_k_gain, m_hgrn_w_in, m_hgrn_w_out, m_hgrn_gnorm, m_hgrn_lower_bounds, m_ffn_w1, m_ffn_w3, m_ffn_w2, v_rel_bias, v_ada_w, v_ada_b, v_norm_mix, v_norm_ffn, v_attn_w_qkv, v_attn_w_out, v_attn_q_gain, v_attn_k_gain, v_hgrn_w_in, v_hgrn_w_out, v_hgrn_gnorm, v_hgrn_lower_bounds, v_ffn_w1, v_ffn_w3, v_ffn_w2):
    weights = dict(rel_bias=rel_bias, ada_w=ada_w, ada_b=ada_b, norm_mix=norm_mix, norm_ffn=norm_ffn,
                   attn_w_qkv=attn_w_qkv, attn_w_out=attn_w_out, attn_q_gain=attn_q_gain, attn_k_gain=attn_k_gain,
                   hgrn_w_in=hgrn_w_in, hgrn_w_out=hgrn_w_out, hgrn_gnorm=hgrn_gnorm,
                   hgrn_lower_bounds=hgrn_lower_bounds, ffn_w1=ffn_w1, ffn_w3=ffn_w3, ffn_w2=ffn_w2)
    mom1 = dict(rel_bias=m_rel_bias, ada_w=m_ada_w, ada_b=m_ada_b, norm_mix=m_norm_mix, norm_ffn=m_norm_ffn,
                attn_w_qkv=m_attn_w_qkv, attn_w_out=m_attn_w_out, attn_q_gain=m_attn_q_gain,
                attn_k_gain=m_attn_k_gain, hgrn_w_in=m_hgrn_w_in, hgrn_w_out=m_hgrn_w_out, hgrn_gnorm=m_hgrn_gnorm,
                hgrn_lower_bounds=m_hgrn_lower_bounds, ffn_w1=m_ffn_w1, ffn_w3=m_ffn_w3, ffn_w2=m_ffn_w2)
    mom2 = dict(rel_bias=v_rel_bias, ada_w=v_ada_w, ada_b=v_ada_b, norm_mix=v_norm_mix, norm_ffn=v_norm_ffn,
                attn_w_qkv=v_attn_w_qkv, attn_w_out=v_attn_w_out, attn_q_gain=v_attn_q_gain,
                attn_k_gain=v_attn_k_gain, hgrn_w_in=v_hgrn_w_in, hgrn_w_out=v_hgrn_w_out, hgrn_gnorm=v_hgrn_gnorm,
                hgrn_lower_bounds=v_hgrn_lower_bounds, ffn_w1=v_ffn_w1, ffn_w3=v_ffn_w3, ffn_w2=v_ffn_w2)

    xi, yi, ci = _position()
    chip = 2 * xi + yi
    dev = 4 * xi + 2 * yi + ci
    place = jnp.stack([ci, chip]).astype(jnp.int32)
    d = D_MODEL

    big_names = ("attn_w_qkv", "attn_w_out", "hgrn_w_in", "hgrn_w_out", "ffn_w1", "ffn_w3", "ffn_w2")
    early_names, late_names = big_names[:2], big_names[2:]
    slabs16 = {k: _cast_bf16(place, weights[k], "cast_" + k) for k in big_names}
    wg = dict(zip(early_names, _gather_weights([slabs16[k] for k in early_names], "gather_early")))

    c_all = _small_allgather(c.reshape(8, 128), "gather_c").reshape(N_DEV, d)
    ada_b_cols = lax.dynamic_slice(ada_b, (0, chip * ADA_SHARD), (DEPTH, ADA_SHARD)).reshape(DEPTH, 1, ADA_SHARD)
    mod_shard = _ada_fwd(c_all, ada_w, ada_b_cols, "ada_fwd")
    mod_all = _small_allgather(mod_shard.reshape(-1, 128), "gather_mod").reshape(N_DEV, DEPTH, N_DEV, ADA_SHARD)
    mod_mine = lax.dynamic_index_in_dim(mod_all[0::2], dev, axis=2, keepdims=False)
    mod = jnp.transpose(mod_mine, (1, 0, 2)).reshape(DEPTH, 6 * d)

    def mods(layer):
        return [mod[layer:layer + 1, j * d:(j + 1) * d] for j in range(6)]

    x0 = x.reshape(SEQ, d)
    target = loss_target.reshape(SEQ, d)
    qg = attn_q_gain.reshape(len(GROUPS), 1, HEAD_DIM)
    kg = attn_k_gain.reshape(len(GROUPS), 1, HEAD_DIM)
    bias = _attn_bias(rel_bias, "attn_bias")
    lb1 = _lower_bounds(hgrn_lower_bounds, "lower_bounds")[1:2]

    def ffn_fwd(layer, x_in, sc2, sh2, g2):
        hf = _norm_mod(x_in, norm_ffn[layer:layer + 1], sc2, sh2, f"l{layer}_norm_ffn")
        a1, a3, u = _ffn_up(hf, wg["ffn_w1"], wg["ffn_w3"], layer, f"l{layer}_ffn_up")
        z, x_out = _mm_rows(u, wg["ffn_w2"], layer, x_in, g2, f"l{layer}_ffn_down")
        return x_out, (hf, a1, a3, u, z)

    def ffn_bwd(layer, dx_out, x_in, sc2, sh2, g2, saved):
        hf, a1, a3, u, z = saved
        dz, dg2 = _gate_bwd(dx_out, z, g2, f"l{layer}_ffn_gate_bwd")
        da1, da3 = _ffn_down_bwd(dz, wg["ffn_w2"], layer, a1, a3, f"l{layer}_ffn_down_bwd")
        dw2 = _mm_rows_bwd_w(u, dz, f"l{layer}_dw2")
        dh = _mm_cols_bwd_a([(da1, wg["ffn_w1"], layer), (da3, wg["ffn_w3"], layer)], tn=FFN_SHARD,
                            act_map=_block_map, w_map=_block_map, n_tiles=N_CHIP, name=f"l{layer}_ffn_up_bwd")
        dw1 = _mm_cols_bwd_w(hf, da1, ns=FFN_SHARD, tn=FFN_SHARD, act_map=_block_map, w_map=_block_map,
                             n_tiles=N_CHIP, name=f"l{layer}_dw1")
        dw3 = _mm_cols_bwd_w(hf, da3, ns=FFN_SHARD, tn=FFN_SHARD, act_map=_block_map, w_map=_block_map,
                             n_tiles=N_CHIP, name=f"l{layer}_dw3")
        dx_in, dsc2, dsh2, dnf = _norm_mod_bwd(x_in, norm_ffn[layer:layer + 1], sc2, sh2, dh, dx_out,
                                               f"l{layer}_norm_ffn_bwd")
        return dx_in, (dw1, dw3, dw2), (dsh2, dsc2, dg2), dnf

    sh1_0, sc1_0, g1_0, sh2_0, sc2_0, g2_0 = mods(0)
    h0 = _norm_mod(x0, norm_mix[0:1], sc1_0, sh1_0, "l0_norm_mix")
    w_qkv9 = _retile_cols(wg["attn_w_qkv"].reshape(N_CHIP, d, 2304), n_out=9, width_out=d, tn=256,
                          src_map=_qkv_chip_map, dst_map=_qkv_group_map, n_tiles=36,
                          name="regroup_w_qkv").reshape(9, 1, d, d)
    qkv9 = _mm_cols(h0, w_qkv9, 0, n_blocks=9, width=d, tn=d, act_map=_block_map, w_map=_block_map,
                    out_dtype=F32, name="l0_qkv")
    o4, lse, *late = _attn_fwd(qkv9, qg, kg, bias, "l0_attn", gather=[slabs16[k] for k in late_names])
    wg.update(zip(late_names, _gather_weights(late, "gather_late_siblings", ici=False)))
    y0, x1 = _mm_rows(o4, wg["attn_w_out"], 0, x0, g1_0, "l0_attn_out")
    x2, ffn0 = ffn_fwd(0, x1, sc2_0, sh2_0, g2_0)

    sh1_1, sc1_1, g1_1, sh2_1, sc2_1, g2_1 = mods(1)
    h1 = _norm_mod(x2, norm_mix[1:2], sc1_1, sh1_1, "l1_norm_mix")
    proj4 = _mm_cols(h1, wg["hgrn_w_in"], 0, n_blocks=4, width=d, tn=512, act_map=_hin_map, w_map=_hin_map,
                     out_dtype=F32, name="l1_hgrn_in")
    o_raw, yg4, states = _hgrn_fwd(proj4, lb1, hgrn_gnorm, "l1_hgrn")
    y1, x3 = _mm_rows(yg4, wg["hgrn_w_out"], 0, x2, g1_1, "l1_hgrn_out")
    x4, ffn1 = ffn_fwd(1, x3, sc2_1, sh2_1, g2_1)

    dx4, loss_part = _loss_head(x4, target, "loss_head")
    loss = lax.psum(loss_part[0, 0], ("x", "y", "c"))

    dx3, (dw1_1, dw3_1, dw2_1), dmod2_1, dnf_1 = ffn_bwd(1, dx4, x3, sc2_1, sh2_1, g2_1, ffn1)
    dzm1, dg1_1 = _gate_bwd(dx3, y1, g1_1, "l1_mix_gate_bwd")
    dyg4 = _mm_rows_bwd_a(dzm1, wg["hgrn_w_out"], 0, "l1_hgrn_out_bwd")
    dw_hout = _mm_rows_bwd_w(yg4, dzm1, "l1_dw_hgrn_out")
    dproj4, dlb_h, dgn_h = _hgrn_bwd(proj4, lb1, hgrn_gnorm, o_raw, dyg4, states, "l1_hgrn_bwd")
    dh1 = _mm_cols_bwd_a([(dproj4, wg["hgrn_w_in"], 0)], tn=512, act_map=_hin_map, w_map=_hin_map, n_tiles=8,
                         name="l1_hgrn_in_bwd")
    dw_hin = _mm_cols_bwd_w(h1, dproj4, ns=d, tn=512, act_map=_hin_map, w_map=_hin_map, n_tiles=8,
                            name="l1_dw_hgrn_in")
    dx2, dsc1_1, dsh1_1, dnm_1 = _norm_mod_bwd(x2, norm_mix[1:2], sc1_1, sh1_1, dh1, dx3, "l1_norm_mix_bwd")

    dx1, (dw1_0, dw3_0, dw2_0), dmod2_0, dnf_0 = ffn_bwd(0, dx2, x1, sc2_0, sh2_0, g2_0, ffn0)
    dzm0, dg1_0 = _gate_bwd(dx1, y0, g1_0, "l0_mix_gate_bwd")
    do4 = _mm_rows_bwd_a(dzm0, wg["attn_w_out"], 0, "l0_attn_out_bwd")
    dw_aout = _mm_rows_bwd_w(o4, dzm0, "l0_dw_attn_out")

    def rs_prepare(tags, grads_in, suffix):
        recv = _rs_exchange_halves(grads_in, "rs_exchange_halves_" + suffix)
        return [_rs_add_cast(place, g, r, f"rs_add_{k}_{layer}") for (k, layer), g, r in zip(tags, grads_in, recv)]

    tags_a = [("attn_w_out", 0), ("hgrn_w_in", 0), ("hgrn_w_out", 0), ("ffn_w1", 0), ("ffn_w1", 1), ("ffn_w3", 0),
              ("ffn_w3", 1), ("ffn_w2", 0), ("ffn_w2", 1)]
    parts_a = rs_prepare(tags_a, [dw_aout, dw_hin, dw_hout, dw1_0, dw1_1, dw3_0, dw3_1, dw2_0, dw2_1], "a")
    dqkv, dqg_h, dkg_h, dbias, *got_a = _attn_bwd(qkv9, qg, kg, bias, do4, o4, lse, "l0_attn_bwd", scatter=parts_a)
    dqkv9 = dqkv.reshape(9, SEQ, d)
    dw_qkv9 = _mm_cols_bwd_w(h0, dqkv9, ns=d, tn=d, act_map=_block_map, w_map=_block_map, n_tiles=9,
                             name="l0_dw_qkv", tm=512, n_out=9)
    dw_qkv = _retile_cols(dw_qkv9, n_out=N_CHIP, width_out=2304, tn=256, src_map=_qkv_group_map,
                          dst_map=_qkv_chip_map, n_tiles=36, name="regroup_dw_qkv")
    tags_b = [("attn_w_qkv", 0)]
    parts_b = rs_prepare(tags_b, [dw_qkv], "b")
    dh0, *got_b = _mm_cols_bwd_a([(dqkv9, w_qkv9, 0)], tn=d, act_map=_block_map, w_map=_block_map, n_tiles=9,
                                 name="l0_qkv_bwd", scatter=parts_b)
    dx0, dsc1_0, dsh1_0, dnm_0 = _norm_mod_bwd(x0, norm_mix[0:1], sc1_0, sh1_0, dh0, dx1, "l0_norm_mix_bwd")
    drb8 = _relbias_bwd(dbias, jnp.asarray(_bias_tables()), "rel_bias_bwd")

    small = _pack_rows([
        dsh1_0, dsc1_0, dg1_0, *dmod2_0, dsh1_1, dsc1_1, dg1_1, *dmod2_1,
        dnm_0, dnm_1, dnf_0, dnf_1,
        jnp.transpose(dqg_h, (1, 0, 2, 3)), jnp.transpose(dkg_h, (1, 0, 2, 3)), dgn_h, dlb_h, drb8])
    small_all = _small_allgather(small, "gather_small")
    main, gains, dlbnd, rbt = _small_totals(small_all, hgrn_lower_bounds.reshape(DEPTH, 8, 128), "small_totals")
    ng = len(GROUPS)
    grads = {
        "ada_b": main[_R_DMOD:_R_NMIX].reshape(DEPTH, 6 * d),
        "norm_mix": main[_R_NMIX:_R_NFFN].reshape(DEPTH, d),
        "norm_ffn": main[_R_NFFN:_R_QG].reshape(DEPTH, d),
        "attn_q_gain": gains[0:ng].reshape(1, ng, HEAD_DIM),
        "attn_k_gain": gains[ng:2 * ng].reshape(1, ng, HEAD_DIM),
        "hgrn_gnorm": gains[2 * ng:2 * ng + 1],
        "hgrn_lower_bounds": dlbnd.reshape(DEPTH, d),
        "rel_bias": jnp.transpose(rbt[:, :ng * NUM_BUCKETS].reshape(HEADS, ng, NUM_BUCKETS), (2, 1, 0))
                       .reshape(NUM_BUCKETS, ng * HEADS),
    }
    dmod_all = small_all[:, _R_DMOD:_R_NMIX].reshape(N_DEV, DEPTH, 6 * d)
    dmod_cols = jnp.transpose(lax.dynamic_slice(dmod_all, (0, 0, chip * ADA_SHARD), (N_DEV, DEPTH, ADA_SHARD)),
                              (1, 0, 2))
    grad_ada_w = _ada_bwd(c_all, dmod_cols, "ada_bwd")

    tags = tags_a + tags_b
    halves = [_rs_sum4(place, p, r, f"rs_sum_{k}_{layer}")
              for (k, layer), p, r in zip(tags, parts_a + parts_b, list(got_a) + list(got_b))]
    full = dict(zip(tags, _rs_join_halves(halves, "rs_join_halves")))

    out_g, out_d, out_m, out_v = {}, {}, {}, {}
    for k in big_names:
        gs = [full[(k, layer)] for layer in range(weights[k].shape[0])]
        out_g[k], out_d[k], out_m[k], out_v[k] = _adamw(weights[k], gs, mom1[k], mom2[k], "adamw_" + k)
    shp = (1, DEPTH * d, ADA_SHARD)
    res = _adamw(ada_w.reshape(shp), [grad_ada_w.reshape(shp[1:])], m_ada_w.reshape(shp), v_ada_w.reshape(shp),
                 "adamw_ada_w")
    out_g["ada_w"], out_d["ada_w"], out_m["ada_w"], out_v["ada_w"] = [r.reshape(ada_w.shape) for r in res]
    packed = [_pack_rows([src[k] for k in _SMALL_ORDER])[None] for src in (weights, grads, mom1, mom2)]
    res = _adamw(packed[0], [packed[1][0]], packed[2], packed[3], "adamw_small")
    offset = 0
    for k in _SMALL_ORDER:
        size = weights[k].size
        for dst, r in zip((out_g, out_d, out_m, out_v), res):
            dst[k] = r.reshape(-1)[offset:offset + size].reshape(weights[k].shape)
        offset += size

    return (loss, dx0.reshape(x.shape), *[out_g[k] for k in _WEIGHT_ORDER], *[out_d[k] for k in _WEIGHT_ORDER],
            *[out_m[k] for k in _WEIGHT_ORDER], *[out_v[k] for k in _WEIGHT_ORDER])
```

```python
import functools

import numpy as np
import jax
import jax.numpy as jnp
from jax import lax
from jax.experimental import pallas as pl
from jax.experimental.pallas import tpu as pltpu

F32 = jnp.float32
BF16 = jnp.bfloat16

D_MODEL = 1024
SEQ = 4096
N_DEV = 8
N_CHIP = 4
DEPTH = 2
HEADS = 8
HEAD_DIM = 128
GROUPS = ((128, 1), (512, 4), (2048, 16))
ATT_BLK = 128
ATT_WAYS = 4
ATT_STEPS = SEQ // ATT_BLK // ATT_WAYS
NUM_BUCKETS = 32
MAX_DISTANCE = 2048
FFN_HIDDEN = 2816
FFN_SHARD = FFN_HIDDEN // N_CHIP
HG_SUB = 16
HG_TC = 512
HG_HP = 4
RMS_EPS = 1e-6
NEG = -1e30
ATT_SCALE = HEAD_DIM ** -0.5
ADAM_LR, ADAM_B1, ADAM_B2, ADAM_EPS, ADAM_WD, ADAM_STEP = 0.001, 0.9, 0.999, 1e-08, 0.01, 10
VMEM_LIMIT = 56 * 1024 * 1024
MESH = pl.DeviceIdType.MESH


def _pcall(body, **kw):
    return pl.pallas_call(body, **kw)


def _cparams(sem=None):
    if sem is None:
        return pltpu.CompilerParams(vmem_limit_bytes=VMEM_LIMIT)
    return pltpu.CompilerParams(dimension_semantics=sem, vmem_limit_bytes=VMEM_LIMIT)


def _sds(shape, dtype):
    return jax.ShapeDtypeStruct(shape, dtype)


def _dot(a, b):
    return jnp.dot(a, b, preferred_element_type=F32)


def _dot_nt(a, b):
    return lax.dot_general(a, b, (((1,), (1,)), ((), ())), preferred_element_type=F32)


def _dot_tn(a, b):
    return lax.dot_general(a, b, (((0,), (0,)), ((), ())), preferred_element_type=F32)


def _sigmoid(x):
    return 1.0 / (1.0 + jnp.exp(-x))


def _silu(x):
    return x * _sigmoid(x)


def _dsilu(x):
    s = _sigmoid(x)
    return s * (1.0 + x * (1.0 - s))


def _norm_mod(x, gain, sc, sh, name):
    tm = 512

    def body(x_ref, g_ref, sc_ref, sh_ref, h_ref):
        xv = x_ref[...]
        rs = lax.rsqrt(jnp.mean(xv * xv, axis=-1, keepdims=True) + RMS_EPS)
        h_ref[...] = ((xv * rs * g_ref[...]) * (1.0 + sc_ref[...]) + sh_ref[...]).astype(BF16)

    vec = pl.BlockSpec((1, D_MODEL), lambda i: (0, 0))
    return _pcall(
        body, name=name, grid=(SEQ // tm,),
        in_specs=[pl.BlockSpec((tm, D_MODEL), lambda i: (i, 0)), vec, vec, vec],
        out_specs=pl.BlockSpec((tm, D_MODEL), lambda i: (i, 0)),
        out_shape=_sds((SEQ, D_MODEL), BF16),
        compiler_params=_cparams(("parallel",)),
    )(x, gain, sc, sh)


def _norm_mod_bwd(x, gain, sc, sh, dh, dres, name):
    tm = 512

    def body(x_ref, g_ref, sc_ref, sh_ref, dh_ref, dres_ref, dx_ref, dsc_ref, dsh_ref, dg_ref):
        @pl.when(pl.program_id(0) == 0)
        def _():
            dsc_ref[...] = jnp.zeros_like(dsc_ref)
            dsh_ref[...] = jnp.zeros_like(dsh_ref)
            dg_ref[...] = jnp.zeros_like(dg_ref)

        xv = x_ref[...]
        dhv = dh_ref[...]
        rs = lax.rsqrt(jnp.mean(xv * xv, axis=-1, keepdims=True) + RMS_EPS)
        xh = xv * rs
        dsc_ref[...] += jnp.sum(dhv * (xh * g_ref[...]), axis=0, keepdims=True)
        dsh_ref[...] += jnp.sum(dhv, axis=0, keepdims=True)
        dhn = dhv * (1.0 + sc_ref[...])
        dg_ref[...] += jnp.sum(dhn * xh, axis=0, keepdims=True)
        dxh = dhn * g_ref[...]
        dx_ref[...] = dres_ref[...] + rs * (dxh - xh * jnp.mean(dxh * xh, axis=-1, keepdims=True))

    vec = pl.BlockSpec((1, D_MODEL), lambda i: (0, 0))
    big = pl.BlockSpec((tm, D_MODEL), lambda i: (i, 0))
    return _pcall(
        body, name=name, grid=(SEQ // tm,),
        in_specs=[big, vec, vec, vec, big, big],
        out_specs=[big, vec, vec, vec],
        out_shape=[_sds((SEQ, D_MODEL), F32)] + [_sds((1, D_MODEL), F32)] * 3,
        compiler_params=_cparams(("arbitrary",)),
    )(x, gain, sc, sh, dh, dres)


def _mm_cols(a, wg, layer, *, n_blocks, width, tn, act_map, w_map, out_dtype, name, tm=1024):
    k = a.shape[1]
    n_tiles = n_blocks * width // tn

    def body(a_ref, w_ref, o_ref):
        o_ref[...] = _dot(a_ref[...], w_ref[...]).astype(o_ref.dtype)

    return _pcall(
        body, name=name, grid=(SEQ // tm, n_tiles),
        in_specs=[pl.BlockSpec((tm, k), lambda i, t: (i, 0)),
                  pl.BlockSpec((None, None, k, tn), lambda i, t: (w_map(t)[0], layer, 0, w_map(t)[1]))],
        out_specs=pl.BlockSpec((None, tm, tn), lambda i, t: (act_map(t)[0], i, act_map(t)[1])),
        out_shape=_sds((n_blocks, SEQ, width), out_dtype),
        compiler_params=_cparams(("parallel", "arbitrary")),
    )(a, wg)


def _mm_cols_bwd_a(pairs, *, tn, act_map, w_map, n_tiles, name, tm=1024, scatter=()):
    k = pairs[0][1].shape[2]
    n_p = len(pairs)
    n_s = len(scatter)
    n_rows = SEQ // tm

    def body(*refs):
        o_ref = refs[2 * n_p + n_s]
        if n_s:
            comm_start, comm_wait = _rs_chips(refs[2 * n_p:2 * n_p + n_s], refs[2 * n_p + n_s + 1:2 * n_p + 2 * n_s + 1],
                                              *refs[2 * n_p + 2 * n_s + 1:])
            pl.when((pl.program_id(0) == 0) & (pl.program_id(1) == 0))(comm_start)

        @pl.when(pl.program_id(1) == 0)
        def _():
            o_ref[...] = jnp.zeros_like(o_ref)

        acc = _dot_nt(refs[0][...], refs[1][...])
        for p in range(1, n_p):
            acc += _dot_nt(refs[2 * p][...], refs[2 * p + 1][...])
        o_ref[...] += acc
        if n_s:
            pl.when((pl.program_id(0) == n_rows - 1) & (pl.program_id(1) == n_tiles - 1))(comm_wait)

    in_specs, args = [], []
    for dout, wg, layer in pairs:
        in_specs.append(pl.BlockSpec((None, tm, tn), lambda i, t: (act_map(t)[0], i, act_map(t)[1])))
        in_specs.append(pl.BlockSpec((None, None, k, tn),
                                     lambda i, t, layer=layer: (w_map(t)[0], layer, 0, w_map(t)[1])))
        args += [dout, wg]
    sem = pltpu.SemaphoreType.DMA((max(n_s, 1), 3))
    res = _pcall(
        body, name=name, grid=(n_rows, n_tiles),
        in_specs=in_specs + [_ANY] * n_s,
        out_specs=[pl.BlockSpec((tm, k), lambda i, t: (i, 0))] + [_ANY] * n_s,
        out_shape=[_sds((SEQ, k), F32)] + _rs_chips_shapes(scatter),
        scratch_shapes=[sem, sem] if n_s else [],
        compiler_params=_cparams(("arbitrary", "arbitrary") if n_s else ("parallel", "arbitrary")),
    )(*args, *scatter)
    return res if n_s else res[0]


def _mm_cols_bwd_w(a, dout, *, ns, tn, act_map, w_map, n_tiles, name, tm=1024, n_out=N_CHIP):
    k = a.shape[1]

    def body(a_ref, d_ref, o_ref):
        @pl.when(pl.program_id(1) == 0)
        def _():
            o_ref[...] = jnp.zeros_like(o_ref)

        o_ref[...] += _dot_tn(a_ref[...], d_ref[...])

    return _pcall(
        body, name=name, grid=(n_tiles, SEQ // tm),
        in_specs=[pl.BlockSpec((tm, k), lambda t, i: (i, 0)),
                  pl.BlockSpec((None, tm, tn), lambda t, i: (act_map(t)[0], i, act_map(t)[1]))],
        out_specs=pl.BlockSpec((None, k, tn), lambda t, i: (w_map(t)[0], 0, w_map(t)[1])),
        out_shape=_sds((n_out, k, ns), F32),
        compiler_params=_cparams(("parallel", "arbitrary")),
    )(a, dout)


def _retile_cols(src, *, n_out, width_out, tn, src_map, dst_map, n_tiles, name):
    k = src.shape[1]

    def body(s_ref, o_ref):
        o_ref[...] = s_ref[...]

    return _pcall(
        body, name=name, grid=(n_tiles,),
        in_specs=[pl.BlockSpec((None, k, tn), lambda t: (src_map(t)[0], 0, src_map(t)[1]))],
        out_specs=pl.BlockSpec((None, k, tn), lambda t: (dst_map(t)[0], 0, dst_map(t)[1])),
        out_shape=_sds((n_out, k, width_out), src.dtype),
        compiler_params=_cparams(("parallel",)),
    )(src)


def _mm_rows(a4, wg, layer, x, gate, name, tm=1024):
    ks = a4.shape[2]
    n = wg.shape[3]

    def body(a_ref, w_ref, x_ref, g_ref, z_ref, xn_ref):
        s = pl.program_id(1)

        @pl.when(s == 0)
        def _():
            z_ref[...] = jnp.zeros_like(z_ref)

        z_ref[...] += _dot(a_ref[...], w_ref[...])

        @pl.when(s == N_CHIP - 1)
        def _():
            xn_ref[...] = x_ref[...] + g_ref[...] * z_ref[...]

    big = pl.BlockSpec((tm, n), lambda i, s: (i, 0))
    return _pcall(
        body, name=name, grid=(SEQ // tm, N_CHIP),
        in_specs=[pl.BlockSpec((None, tm, ks), lambda i, s: (s, i, 0)),
                  pl.BlockSpec((None, None, ks, n), lambda i, s: (s, layer, 0, 0)),
                  big, pl.BlockSpec((1, n), lambda i, s: (0, 0))],
        out_specs=[big, big],
        out_shape=[_sds((SEQ, n), F32), _sds((SEQ, n), F32)],
        compiler_params=_cparams(("parallel", "arbitrary")),
    )(a4, wg, x, gate)


def _gate_bwd(dx, z, gate, name):
    tm = 512

    def body(dx_ref, z_ref, g_ref, dz_ref, dg_ref):
        @pl.when(pl.program_id(0) == 0)
        def _():
            dg_ref[...] = jnp.zeros_like(dg_ref)

        dxv = dx_ref[...]
        dz_ref[...] = (dxv * g_ref[...]).astype(BF16)
        dg_ref[...] += jnp.sum(dxv * z_ref[...], axis=0, keepdims=True)

    big = pl.BlockSpec((tm, D_MODEL), lambda i: (i, 0))
    vec = pl.BlockSpec((1, D_MODEL), lambda i: (0, 0))
    return _pcall(
        body, name=name, grid=(SEQ // tm,),
        in_specs=[big, big, vec], out_specs=[big, vec],
        out_shape=[_sds((SEQ, D_MODEL), BF16), _sds((1, D_MODEL), F32)],
        compiler_params=_cparams(("arbitrary",)),
    )(dx, z, gate)


def _mm_rows_bwd_a(dz, wg, layer, name, tm=1024):
    ks, n = wg.shape[2], wg.shape[3]

    def body(dz_ref, w_ref, o_ref):
        o_ref[...] = _dot_nt(dz_ref[...], w_ref[...])

    return _pcall(
        body, name=name, grid=(SEQ // tm, N_CHIP),
        in_specs=[pl.BlockSpec((tm, n), lambda i, s: (i, 0)),
                  pl.BlockSpec((None, None, ks, n), lambda i, s: (s, layer, 0, 0))],
        out_specs=pl.BlockSpec((None, tm, ks), lambda i, s: (s, i, 0)),
        out_shape=_sds((N_CHIP, SEQ, ks), F32),
        compiler_params=_cparams(("parallel", "arbitrary")),
    )(dz, wg)


def _mm_rows_bwd_w(a4, dz, name, tm=1024):
    ks = a4.shape[2]
    n = dz.shape[1]

    def body(a_ref, dz_ref, o_ref):
        @pl.when(pl.program_id(1) == 0)
        def _():
            o_ref[...] = jnp.zeros_like(o_ref)

        o_ref[...] += _dot_tn(a_ref[...], dz_ref[...])

    return _pcall(
        body, name=name, grid=(N_CHIP, SEQ // tm),
        in_specs=[pl.BlockSpec((None, tm, ks), lambda s, i: (s, i, 0)),
                  pl.BlockSpec((tm, n), lambda s, i: (i, 0))],
        out_specs=pl.BlockSpec((None, ks, n), lambda s, i: (s, 0, 0)),
        out_shape=_sds((N_CHIP, ks, n), F32),
        compiler_params=_cparams(("parallel", "arbitrary")),
    )(a4, dz)


def _ffn_up(h, w1g, w3g, layer, name, tm=1024):
    def body(h_ref, w1_ref, w3_ref, a1_ref, a3_ref, u_ref):
        hv = h_ref[...]
        a1 = _dot(hv, w1_ref[...])
        a3 = _dot(hv, w3_ref[...])
        a1_ref[...] = a1
        a3_ref[...] = a3
        u_ref[...] = (_silu(a1) * a3).astype(BF16)

    wspec = pl.BlockSpec((None, None, D_MODEL, FFN_SHARD), lambda i, s: (s, layer, 0, 0))
    ospec = pl.BlockSpec((None, tm, FFN_SHARD), lambda i, s: (s, i, 0))
    shp = (N_CHIP, SEQ, FFN_SHARD)
    return _pcall(
        body, name=name, grid=(SEQ // tm, N_CHIP),
        in_specs=[pl.BlockSpec((tm, D_MODEL), lambda i, s: (i, 0)), wspec, wspec],
        out_specs=[ospec, ospec, ospec],
        out_shape=[_sds(shp, F32), _sds(shp, F32), _sds(shp, BF16)],
        compiler_params=_cparams(("parallel", "arbitrary")),
    )(h, w1g, w3g)


def _ffn_down_bwd(dz, w2g, layer, a1, a3, name, tm=1024):
    def body(dz_ref, w_ref, a1_ref, a3_ref, da1_ref, da3_ref):
        du = _dot_nt(dz_ref[...], w_ref[...])
        a1 = a1_ref[...]
        da1_ref[...] = (du * a3_ref[...] * _dsilu(a1)).astype(BF16)
        da3_ref[...] = (du * _silu(a1)).astype(BF16)

    blk = pl.BlockSpec((None, tm, FFN_SHARD), lambda i, s: (s, i, 0))
    shp = (N_CHIP, SEQ, FFN_SHARD)
    return _pcall(
        body, name=name, grid=(SEQ // tm, N_CHIP),
        in_specs=[pl.BlockSpec((tm, D_MODEL), lambda i, s: (i, 0)),
                  pl.BlockSpec((None, None, FFN_SHARD, D_MODEL), lambda i, s: (s, layer, 0, 0)),
                  blk, blk],
        out_specs=[blk, blk],
        out_shape=[_sds(shp, BF16), _sds(shp, BF16)],
        compiler_params=_cparams(("parallel", "arbitrary")),
    )(dz, w2g, a1, a3)


def _loss_head(y, target, name):
    tm = 512

    def body(y_ref, t_ref, dy_ref, l_ref, acc_ref):
        @pl.when(pl.program_id(0) == 0)
        def _():
            acc_ref[...] = jnp.zeros_like(acc_ref)

        err = y_ref[...] - t_ref[...]
        dy_ref[...] = err * (1.0 / D_MODEL)
        acc_ref[...] += jnp.sum(jnp.mean(err * err, axis=-1, keepdims=True), axis=0, keepdims=True)

        @pl.when(pl.program_id(0) == pl.num_programs(0) - 1)
        def _():
            l_ref[...] = 0.5 * acc_ref[...]

    big = pl.BlockSpec((tm, D_MODEL), lambda i: (i, 0))
    return _pcall(
        body, name=name, grid=(SEQ // tm,),
        in_specs=[big, big],
        out_specs=[big, pl.BlockSpec((1, 1), lambda i: (0, 0))],
        out_shape=[_sds((SEQ, D_MODEL), F32), _sds((1, 1), F32)],
        scratch_shapes=[pltpu.VMEM((1, 1), F32)],
        compiler_params=_cparams(("arbitrary",)),
    )(y, target)


def _attn_rows(base, d):
    if d == 1:
        return pl.ds(pl.multiple_of(base, ATT_BLK), ATT_BLK)
    return pl.ds(base, ATT_BLK, stride=d)


def _attn_block_index(i, d):
    nb = SEQ // (ATT_BLK * d)
    r = i // nb
    n = i % nb
    base = r + n * (ATT_BLK * d)
    pbase = jnp.maximum(base - ATT_BLK * d, r)
    return n, _attn_rows(base, d), _attn_rows(pbase, d)


def _qk_normed(x):
    rs = lax.rsqrt(jnp.mean(x * x, axis=-1, keepdims=True) + RMS_EPS)
    return x * rs, rs


def _attn_fwd(qkv9, qgain, kgain, bias, name, gather=()):
    n_g = len(gather)

    def body(*refs):
        q_ref, k_ref, v_ref, qg_ref, kg_ref, b_ref = refs[:6]
        o_ref, lse_ref = refs[6 + n_g:8 + n_g]
        qn_s, kn_s, acc_s, m_s, l_s = refs[8 + 2 * n_g:13 + 2 * n_g]
        g = pl.program_id(1)
        if n_g:
            comm_start, comm_wait = _gather_ici(refs[8 + n_g:8 + 2 * n_g], *refs[13 + 2 * n_g:])
            pl.when((pl.program_id(0) == 0) & (g == 0))(comm_start)

        @pl.when(g == 0)
        def _():
            m_s[...] = jnp.full_like(m_s, NEG)
            l_s[...] = jnp.zeros_like(l_s)
            acc_s[...] = jnp.zeros_like(acc_s)

        qn_s[...] = _qk_normed(q_ref[...])[0] * qg_ref[...]
        kn_s[...] = _qk_normed(k_ref[...])[0] * kg_ref[...]

        for gi, (_, d) in enumerate(GROUPS):
            @pl.when(g == gi)
            def _(d=d):
                def block(n, qb, kc, kp, vc, vp, m_old, l_old, acc_old):
                    sc = _dot_nt(qb, kc) * ATT_SCALE + b_ref[1]
                    sp = _dot_nt(qb, kp) * ATT_SCALE + jnp.where(n > 0, b_ref[0], NEG)
                    m_new = jnp.maximum(m_old, jnp.maximum(jnp.max(sc, axis=-1, keepdims=True),
                                                           jnp.max(sp, axis=-1, keepdims=True)))
                    alpha = jnp.exp(m_old - m_new)
                    pc = jnp.exp(sc - m_new)
                    pp = jnp.exp(sp - m_new)
                    l_new = alpha * l_old + jnp.sum(pc, axis=-1, keepdims=True) + jnp.sum(pp, axis=-1, keepdims=True)
                    acc_new = alpha * acc_old + _dot(pc.astype(BF16), vc) + _dot(pp.astype(BF16), vp)
                    return m_new, l_new, acc_new

                def it(i, carry):
                    where, loaded = [], []
                    for way in range(ATT_WAYS):
                        n, rows, prow = _attn_block_index(i + way * ATT_STEPS, d)
                        where.append(rows)
                        loaded.append((n, qn_s[rows, :].astype(BF16), kn_s[rows, :].astype(BF16),
                                       kn_s[prow, :].astype(BF16), v_ref[rows, :].astype(BF16),
                                       v_ref[prow, :].astype(BF16), m_s[rows, :], l_s[rows, :], acc_s[rows, :]))
                    results = [block(*vals) for vals in loaded]
                    for rows, (m_new, l_new, acc_new) in zip(where, results):
                        m_s[rows, :] = m_new
                        l_s[rows, :] = l_new
                        acc_s[rows, :] = acc_new
                    return carry

                lax.fori_loop(0, ATT_STEPS, it, 0)

        @pl.when(g == len(GROUPS) - 1)
        def _():
            o_ref[...] = (acc_s[...] / l_s[...]).astype(BF16)
            lse_ref[...] = m_s[...] + jnp.log(l_s[...])

        if n_g:
            pl.when((pl.program_id(0) == HEADS - 1) & (g == len(GROUPS) - 1))(comm_wait)

    def col(j):
        return pl.BlockSpec((None, SEQ, HEAD_DIM), lambda h, g: (g * 3 + j, 0, h))

    gspec = pl.BlockSpec((None, 1, HEAD_DIM), lambda h, g: (g, 0, 0))
    sem = pltpu.SemaphoreType.DMA((max(n_g, 1), 3))
    return _pcall(
        body, name=name, grid=(HEADS, len(GROUPS)),
        in_specs=[col(0), col(1), col(2), gspec, gspec,
                  pl.BlockSpec((None, None, 2, ATT_BLK, ATT_BLK), lambda h, g: (g, h, 0, 0, 0))] + [_ANY] * n_g,
        out_specs=[pl.BlockSpec((None, SEQ, HEAD_DIM), lambda h, g: (h // 2, 0, h % 2)),
                   pl.BlockSpec((None, SEQ, 1), lambda h, g: (h, 0, 0))] + [_ANY] * n_g,
        out_shape=[_sds((N_CHIP, SEQ, 2 * HEAD_DIM), BF16), _sds((HEADS, SEQ, 1), F32)]
        + [_sds(s.shape, s.dtype) for s in gather],
        input_output_aliases={6 + a: 2 + a for a in range(n_g)},
        scratch_shapes=[pltpu.VMEM((SEQ, HEAD_DIM), F32)] * 3 + [pltpu.VMEM((SEQ, 1), F32)] * 2
        + ([sem, sem] if n_g else []),
        compiler_params=_cparams(("arbitrary", "arbitrary")),
    )(qkv9, qkv9, qkv9, qgain, kgain, bias, *gather)


def _attn_bwd(qkv9, qgain, kgain, bias, do4, o4, lse, name, scatter=()):
    n_s = len(scatter)

    def body(*refs):
        q_ref, k_ref, v_ref, qg_ref, kg_ref, b_ref, do_ref, o_ref, lse_ref = refs[:9]
        dqkv_ref, dqg_ref, dkg_ref, db_ref = refs[9 + n_s:13 + n_s]
        qn_s, kn_s, dq_s, dk_s, dv_s, dl_s = refs[13 + 2 * n_s:19 + 2 * n_s]
        g = pl.program_id(1)
        if n_s:
            comm_start, comm_wait = _rs_chips(refs[9:9 + n_s], refs[13 + n_s:13 + 2 * n_s], *refs[19 + 2 * n_s:])
            pl.when((pl.program_id(0) == 0) & (g == 0))(comm_start)
        qh, rq = _qk_normed(q_ref[...])
        kh, rk = _qk_normed(k_ref[...])
        qn_s[...] = qh * qg_ref[...]
        kn_s[...] = kh * kg_ref[...]
        dl_s[...] = jnp.sum(do_ref[...] * o_ref[...].astype(F32), axis=-1, keepdims=True)
        dk_s[...] = jnp.zeros_like(dk_s)
        dv_s[...] = jnp.zeros_like(dv_s)
        db_ref[...] = jnp.zeros_like(db_ref)

        for gi, (_, d) in enumerate(GROUPS):
            @pl.when(g == gi)
            def _(d=d):
                def block(n, qb, kc, kp, vc, vp, dob, lse_b, dl):
                    sc = _dot_nt(qb, kc) * ATT_SCALE + b_ref[1]
                    sp = _dot_nt(qb, kp) * ATT_SCALE + jnp.where(n > 0, b_ref[0], NEG)
                    pc = jnp.exp(sc - lse_b)
                    pp = jnp.exp(sp - lse_b)
                    dsc = pc * (_dot_nt(dob, vc) - dl)
                    dsp = pp * (_dot_nt(dob, vp) - dl)
                    dsc16 = dsc.astype(BF16)
                    dsp16 = dsp.astype(BF16)
                    dq = (_dot(dsc16, kc) + _dot(dsp16, kp)) * ATT_SCALE
                    return (dsc, dsp, dq, _dot_tn(dsc16, qb) * ATT_SCALE, _dot_tn(dsp16, qb) * ATT_SCALE,
                            _dot_tn(pc.astype(BF16), dob), _dot_tn(pp.astype(BF16), dob))

                def it(i, carry):
                    where, loaded, old = [], [], []
                    for way in range(ATT_WAYS):
                        n, rows, prow = _attn_block_index(i + way * ATT_STEPS, d)
                        where.append((rows, prow))
                        loaded.append((n, qn_s[rows, :].astype(BF16), kn_s[rows, :].astype(BF16),
                                       kn_s[prow, :].astype(BF16), v_ref[rows, :].astype(BF16),
                                       v_ref[prow, :].astype(BF16), do_ref[rows, :].astype(BF16),
                                       lse_ref[rows, :], dl_s[rows, :]))
                        old.append((dk_s[rows, :], dk_s[prow, :], dv_s[rows, :], dv_s[prow, :]))
                    results = [block(*vals) for vals in loaded]
                    db_ref[1] += functools.reduce(lambda a, b: a + b, [r[0] for r in results])
                    db_ref[0] += functools.reduce(lambda a, b: a + b, [r[1] for r in results])
                    for (rows, prow), (dk_c, dk_p, dv_c, dv_p), (_, _, dq, dkc, dkp, dvc, dvp) in zip(where, old, results):
                        dq_s[rows, :] = dq
                        dk_s[prow, :] = dk_p + dkp
                        dv_s[prow, :] = dv_p + dvp
                        dk_s[rows, :] = dk_c + dkc
                        dv_s[rows, :] = dv_c + dvc
                    return carry

                lax.fori_loop(0, ATT_STEPS, it, 0)

        def norm_bwd(dn, xh, rs, gain):
            dgain = jnp.sum(dn * xh, axis=0, keepdims=True)
            dxh = dn * gain
            return rs * (dxh - xh * jnp.mean(dxh * xh, axis=-1, keepdims=True)), dgain

        dq, dqg = norm_bwd(dq_s[...], qh, rq, qg_ref[...])
        dk, dkg = norm_bwd(dk_s[...], kh, rk, kg_ref[...])
        dqkv_ref[0] = dq.astype(BF16)
        dqkv_ref[1] = dk.astype(BF16)
        dqkv_ref[2] = dv_s[...].astype(BF16)
        dqg_ref[...] = dqg
        dkg_ref[...] = dkg
        if n_s:
            pl.when((pl.program_id(0) == HEADS - 1) & (g == len(GROUPS) - 1))(comm_wait)

    def col(j):
        return pl.BlockSpec((None, SEQ, HEAD_DIM), lambda h, g: (g * 3 + j, 0, h))

    gspec = pl.BlockSpec((None, 1, HEAD_DIM), lambda h, g: (g, 0, 0))
    bspec = pl.BlockSpec((None, None, 2, ATT_BLK, ATT_BLK), lambda h, g: (g, h, 0, 0, 0))
    hcol = pl.BlockSpec((None, SEQ, HEAD_DIM), lambda h, g: (h // 2, 0, h % 2))
    dgspec = pl.BlockSpec((None, None, 1, HEAD_DIM), lambda h, g: (h, g, 0, 0))
    ng = len(GROUPS)
    sem = pltpu.SemaphoreType.DMA((max(n_s, 1), 3))
    return _pcall(
        body, name=name, grid=(HEADS, ng),
        in_specs=[col(0), col(1), col(2), gspec, gspec, bspec, hcol, hcol,
                  pl.BlockSpec((None, SEQ, 1), lambda h, g: (h, 0, 0))] + [_ANY] * n_s,
        out_specs=[pl.BlockSpec((None, 3, SEQ, HEAD_DIM), lambda h, g: (g, 0, 0, h)), dgspec, dgspec, bspec]
        + [_ANY] * n_s,
        out_shape=[_sds((ng, 3, SEQ, D_MODEL), BF16), _sds((HEADS, ng, 1, HEAD_DIM), F32),
                   _sds((HEADS, ng, 1, HEAD_DIM), F32), _sds((ng, HEADS, 2, ATT_BLK, ATT_BLK), F32)]
        + _rs_chips_shapes(scatter),
        scratch_shapes=[pltpu.VMEM((SEQ, HEAD_DIM), F32)] * 5 + [pltpu.VMEM((SEQ, 1), F32)]
        + ([sem, sem] if n_s else []),
        compiler_params=_cparams(("arbitrary", "arbitrary")),
    )(qkv9, qkv9, qkv9, qgain, kgain, bias, do4, o4, lse, *scatter)


def _relbias_bwd(dbias, bucket_idx, name):
    ng = len(GROUPS)

    def body(db_ref, idx_ref, o_ref):
        lane = lax.broadcasted_iota(jnp.int32, (HEADS, 128), 1)
        acc = jnp.zeros((HEADS, 128), F32)
        for g in range(ng):
            dbg = db_ref[g]
            idx = idx_ref[g]
            for b in range(NUM_BUCKETS):
                sel = jnp.where((idx == b)[None], dbg, 0.0)
                part = jnp.sum(jnp.sum(sel, axis=1), axis=1)
                val = jnp.sum(part, axis=-1, keepdims=True)
                acc = jnp.where(lane == g * NUM_BUCKETS + b, val, acc)
        o_ref[...] = acc

    return _pcall(body, name=name, out_shape=_sds((HEADS, 128), F32), compiler_params=_cparams())(dbias, bucket_idx)


def _scan16(x, reverse=False):
    row = lax.broadcasted_iota(jnp.int32, x.shape, 0)
    for sh in (1, 2, 4, 8):
        if reverse:
            x = x + jnp.where(row < HG_SUB - sh, pltpu.roll(x, HG_SUB - sh, 0), 0.0)
        else:
            x = x + jnp.where(row >= sh, pltpu.roll(x, sh, 0), 0.0)
    return x


def _hgrn_gates(qr, fr, lbv):
    q = _silu(qr)
    sig = _sigmoid(fr)
    fg = lbv + (1.0 - lbv) * sig
    lf = jnp.log(fg)
    gcum = _scan16(lf)
    glast = jnp.sum(lf, axis=0, keepdims=True)
    return q, sig, fg, 1.0 - fg, gcum, glast


def _hgrn_intra(q, k, gcum, tri):
    e = jnp.exp(jnp.where(tri, gcum[:, None, :] - gcum[None, :, :], NEG))
    a = jnp.sum(q[:, None, :] * k[None, :, :] * e, axis=-1, keepdims=True)
    return e, a


def _hgrn_fwd(proj4, lb, gain, name):
    nsub = HG_TC // HG_SUB
    wide = HG_HP * HEAD_DIM

    def body(p_ref, lb_ref, gn_ref, o_ref, y_ref, st_ref, state_s):
        @pl.when(pl.program_id(1) == 0)
        def _():
            state_s[...] = jnp.zeros_like(state_s)

        gnv = gn_ref[...]
        shp = (HG_SUB, HG_SUB, HEAD_DIM)
        tri = lax.broadcasted_iota(jnp.int32, shp, 0) >= lax.broadcasted_iota(jnp.int32, shp, 1)

        def head(qr, fr, vv, gr, lbv, st):
            q, _, _, k, gcum, glast = _hgrn_gates(qr, fr, lbv)
            _, a = _hgrn_intra(q, k, gcum, tri)
            o = jnp.sum(a * vv[None, :, :], axis=1) + _dot_nt((q * jnp.exp(gcum)).astype(BF16), st.astype(BF16))
            kg = k * jnp.exp(glast - gcum)
            st_new = st * jnp.exp(glast) + _dot_tn(vv.astype(BF16), kg.astype(BF16))
            rs = lax.rsqrt(jnp.mean(o * o, axis=-1, keepdims=True) + RMS_EPS)
            return o, (o * rs * gnv * _silu(gr)).astype(BF16), st_new

        def it(i, carry):
            rows = pl.ds(pl.multiple_of(i * HG_SUB, HG_SUB), HG_SUB)
            loaded = []
            for hh in range(HG_HP):
                lanes = pl.ds(hh * HEAD_DIM, HEAD_DIM)
                loaded.append(([p_ref[j, rows, lanes] for j in range(4)], lb_ref[:, lanes], state_s[hh]))
            results = [head(blk[0], blk[1], blk[2], blk[3], lbv, st) for blk, lbv, st in loaded]
            for hh, ((_, _, st), (o, y, st_new)) in enumerate(zip(loaded, results)):
                lanes = pl.ds(hh * HEAD_DIM, HEAD_DIM)
                st_ref[hh, i] = st.astype(BF16)
                state_s[hh] = st_new
                o_ref[rows, lanes] = o
                y_ref[hh // 2, rows, pl.ds((hh % 2) * HEAD_DIM, HEAD_DIM)] = y
            return carry

        lax.fori_loop(0, nsub, it, 0)

    return _pcall(
        body, name=name, grid=(HEADS // HG_HP, SEQ // HG_TC),
        in_specs=[pl.BlockSpec((4, HG_TC, wide), lambda h, j: (0, j, h)),
                  pl.BlockSpec((1, wide), lambda h, j: (0, h)),
                  pl.BlockSpec((1, HEAD_DIM), lambda h, j: (0, 0))],
        out_specs=[pl.BlockSpec((HG_TC, wide), lambda h, j: (j, h)),
                   pl.BlockSpec((HG_HP // 2, HG_TC, 2 * HEAD_DIM), lambda h, j: (h, j, 0)),
                   pl.BlockSpec((HG_HP, nsub, HEAD_DIM, HEAD_DIM), lambda h, j: (h, j, 0, 0))],
        out_shape=[_sds((SEQ, D_MODEL), F32), _sds((N_CHIP, SEQ, 2 * HEAD_DIM), BF16),
                   _sds((HEADS, SEQ // HG_SUB, HEAD_DIM, HEAD_DIM), BF16)],
        scratch_shapes=[pltpu.VMEM((HG_HP, HEAD_DIM, HEAD_DIM), F32)],
        compiler_params=_cparams(("parallel", "arbitrary")),
    )(proj4, lb, gain)


def _hgrn_bwd(proj4, lb, gain, o_raw, dy4, states, name):
    nsub = HG_TC // HG_SUB
    nt = SEQ // HG_TC
    wide = HG_HP * HEAD_DIM

    def body(p_ref, lb_ref, gn_ref, o_ref, dy_ref, st_ref, dp_ref, dlb_ref, dgn_ref, dst_s):
        @pl.when(pl.program_id(1) == 0)
        def _():
            dst_s[...] = jnp.zeros_like(dst_s)
            dlb_ref[...] = jnp.zeros_like(dlb_ref)
            dgn_ref[...] = jnp.zeros_like(dgn_ref)

        gnv = gn_ref[...]
        shp = (HG_SUB, HG_SUB, HEAD_DIM)
        tri = lax.broadcasted_iota(jnp.int32, shp, 0) >= lax.broadcasted_iota(jnp.int32, shp, 1)

        def head(qr, fr, vv, gr, o, dy, lbv, st0, dst):
            q, sig, fg, k, gcum, glast = _hgrn_gates(qr, fr, lbv)
            rs = lax.rsqrt(jnp.mean(o * o, axis=-1, keepdims=True) + RMS_EPS)
            oh = o * rs
            don = dy * _silu(gr)
            dgn = jnp.sum(don * oh, axis=0, keepdims=True)
            dgr = dy * oh * gnv * _dsilu(gr)
            doh = don * gnv
            do = rs * (doh - oh * jnp.mean(doh * oh, axis=-1, keepdims=True))
            dst16 = dst.astype(BF16)
            do16 = do.astype(BF16)
            eg = jnp.exp(gcum)
            eb = jnp.exp(glast - gcum)
            e, a = _hgrn_intra(q, k, gcum, tri)
            da = jnp.sum(do[:, None, :] * vv[None, :, :], axis=-1, keepdims=True)
            dae = da * e
            dq = jnp.sum(dae * k[None, :, :], axis=1) + eg * _dot(do16, st0)
            dk_state = eb * _dot(vv.astype(BF16), dst16)
            dk = jnp.sum(dae * q[:, None, :], axis=0) + dk_state
            dv = jnp.sum(a * do[:, None, :], axis=0) + _dot_nt((k * eb).astype(BF16), dst16)
            eglast = jnp.exp(glast)
            dst_new = dst * eglast + _dot_tn(do16, (q * eg).astype(BF16))
            dglast = jnp.sum(k * dk_state, axis=0, keepdims=True) \
                + eglast * jnp.sum(dst * st0.astype(F32), axis=0, keepdims=True)
            dlf = _scan16(q * dq - k * dk, reverse=True) + dglast
            dfg = dlf / fg - dk
            dlb = jnp.sum(dfg * (1.0 - sig), axis=0, keepdims=True)
            dproj = ((dq * _dsilu(qr)).astype(BF16), (dfg * (1.0 - lbv) * sig * (1.0 - sig)).astype(BF16),
                     dv.astype(BF16), dgr.astype(BF16))
            return dproj, dst_new, dlb, dgn

        def it(ii, carry):
            i = nsub - 1 - ii
            rows = pl.ds(pl.multiple_of(i * HG_SUB, HG_SUB), HG_SUB)
            results = []
            for hh in range(HG_HP):
                lanes = pl.ds(hh * HEAD_DIM, HEAD_DIM)
                blk = [p_ref[j, rows, lanes] for j in range(4)]
                dy = dy_ref[hh // 2, rows, pl.ds((hh % 2) * HEAD_DIM, HEAD_DIM)]
                results.append(head(blk[0], blk[1], blk[2], blk[3], o_ref[rows, lanes], dy,
                                    lb_ref[:, lanes], st_ref[hh, i], dst_s[hh]))
            new_carry = []
            for hh, (dproj, dst_new, dlb, dgn) in enumerate(results):
                lanes = pl.ds(hh * HEAD_DIM, HEAD_DIM)
                dst_s[hh] = dst_new
                for j in range(4):
                    dp_ref[j, rows, lanes] = dproj[j]
                new_carry.append((carry[hh][0] + dlb, carry[hh][1] + dgn))
            return tuple(new_carry)

        zero = jnp.zeros((1, HEAD_DIM), F32)
        sums = lax.fori_loop(0, nsub, it, tuple((zero, zero) for _ in range(HG_HP)))
        for hh in range(HG_HP):
            dlb_ref[hh] += sums[hh][0]
            dgn_ref[hh] += sums[hh][1]

    vspec = pl.BlockSpec((HG_HP, 1, HEAD_DIM), lambda h, j: (h, 0, 0))
    return _pcall(
        body, name=name, grid=(HEADS // HG_HP, nt),
        in_specs=[pl.BlockSpec((4, HG_TC, wide), lambda h, j: (0, nt - 1 - j, h)),
                  pl.BlockSpec((1, wide), lambda h, j: (0, h)),
                  pl.BlockSpec((1, HEAD_DIM), lambda h, j: (0, 0)),
                  pl.BlockSpec((HG_TC, wide), lambda h, j: (nt - 1 - j, h)),
                  pl.BlockSpec((HG_HP // 2, HG_TC, 2 * HEAD_DIM), lambda h, j: (h, nt - 1 - j, 0)),
                  pl.BlockSpec((HG_HP, nsub, HEAD_DIM, HEAD_DIM), lambda h, j: (h, nt - 1 - j, 0, 0))],
        out_specs=[pl.BlockSpec((4, HG_TC, wide), lambda h, j: (0, nt - 1 - j, h)), vspec, vspec],
        out_shape=[_sds((4, SEQ, D_MODEL), BF16), _sds((HEADS, 1, HEAD_DIM), F32), _sds((HEADS, 1, HEAD_DIM), F32)],
        scratch_shapes=[pltpu.VMEM((HG_HP, HEAD_DIM, HEAD_DIM), F32)],
        compiler_params=_cparams(("parallel", "arbitrary")),
    )(proj4, lb, gain, o_raw, dy4, states)


def _t5_bucket(dist):
    n = np.asarray(dist, dtype=np.int64)
    max_exact = NUM_BUCKETS // 2
    large = max_exact + (np.log(np.maximum(n, 1) / max_exact) / np.log(MAX_DISTANCE / max_exact)
                         * (NUM_BUCKETS - max_exact)).astype(np.int64)
    large = np.minimum(large, NUM_BUCKETS - 1)
    return np.where(n < max_exact, n, large).astype(np.int32)


def _bias_tables():
    qi = np.arange(ATT_BLK)[:, None]
    ki = np.arange(ATT_BLK)[None, :]
    steps = (ATT_BLK + qi - ki, qi - ki)
    idx = np.zeros((len(GROUPS), 2, ATT_BLK, ATT_BLK), np.int32)
    for g, (_, d) in enumerate(GROUPS):
        for p, j in enumerate(steps):
            valid = (j >= 0) & (j <= ATT_BLK)
            idx[g, p] = np.where(valid, _t5_bucket(np.clip(j, 0, ATT_BLK) * d), -1)
    return idx


def _attn_bias(rel_bias, name):
    idx = _bias_tables()
    ng = len(GROUPS)
    buckets = [sorted(set(idx[g][idx[g] >= 0].tolist())) for g in range(ng)]

    def body(rb_ref, idx_ref, o_ref):
        h = pl.program_id(0)
        for g in range(ng):
            ig = idx_ref[g]
            acc = jnp.full(ig.shape, NEG, F32)
            for b in buckets[g]:
                acc = jnp.where(ig == b, rb_ref[b, g * HEADS + h], acc)
            o_ref[g] = acc

    return _pcall(
        body, name=name, grid=(HEADS,),
        in_specs=[pl.BlockSpec(memory_space=pltpu.SMEM),
                  pl.BlockSpec((ng, 2, ATT_BLK, ATT_BLK), lambda h: (0, 0, 0, 0))],
        out_specs=pl.BlockSpec((ng, None, 2, ATT_BLK, ATT_BLK), lambda h: (0, h, 0, 0, 0)),
        out_shape=_sds((ng, HEADS, 2, ATT_BLK, ATT_BLK), F32),
        compiler_params=_cparams(("parallel",)),
    )(rel_bias, jnp.asarray(idx))


ADA_SHARD = 6 * D_MODEL // N_CHIP
ADA_TN = 512


def _ada_fwd(c_all, ada_w, ada_b_cols, name):
    def body(c_ref, w_ref, b_ref, o_ref):
        ca = _silu(c_ref[...]).astype(BF16)
        o_ref[...] = _dot(ca, w_ref[...].astype(BF16)) + b_ref[...]

    return _pcall(
        body, name=name, grid=(DEPTH, ADA_SHARD // ADA_TN),
        in_specs=[pl.BlockSpec((N_DEV, D_MODEL), lambda l, j: (0, 0)),
                  pl.BlockSpec((None, D_MODEL, ADA_TN), lambda l, j: (l, 0, j)),
                  pl.BlockSpec((None, 1, ADA_TN), lambda l, j: (l, 0, j))],
        out_specs=pl.BlockSpec((None, N_DEV, ADA_TN), lambda l, j: (l, 0, j)),
        out_shape=_sds((DEPTH, N_DEV, ADA_SHARD), F32),
        compiler_params=_cparams(("parallel", "parallel")),
    )(c_all, ada_w, ada_b_cols)


def _ada_bwd(c_all, dmod_cols, name):
    def body(c_ref, d_ref, o_ref):
        ca = _silu(c_ref[...]).astype(BF16)
        o_ref[...] = _dot_tn(ca, d_ref[...].astype(BF16))

    return _pcall(
        body, name=name, grid=(DEPTH, ADA_SHARD // ADA_TN),
        in_specs=[pl.BlockSpec((N_DEV, D_MODEL), lambda l, j: (0, 0)),
                  pl.BlockSpec((None, N_DEV, ADA_TN), lambda l, j: (l, 0, j))],
        out_specs=pl.BlockSpec((None, D_MODEL, ADA_TN), lambda l, j: (l, 0, j)),
        out_shape=_sds((DEPTH, D_MODEL, ADA_SHARD), F32),
        compiler_params=_cparams(("parallel", "parallel")),
    )(c_all, dmod_cols)


def _lower_bounds(logits, name):
    def body(l_ref, o_ref):
        l0 = l_ref[0:1, :]
        l1 = l_ref[1:2, :]
        mx = jnp.maximum(l0, l1)
        e0 = jnp.exp(l0 - mx)
        e1 = jnp.exp(l1 - mx)
        p0 = e0 / (e0 + e1)
        p1 = e1 / (e0 + e1)
        o_ref[0:1, :] = p0 - p0
        o_ref[1:2, :] = (p0 + p1) - p0

    return _pcall(body, name=name, out_shape=_sds((DEPTH, D_MODEL), F32), compiler_params=_cparams())(logits)


_R_DMOD = 0
_R_NMIX = 96
_R_NFFN = 112
_R_QG = 128
_R_KG = 152
_R_GN = 176
_R_LB = 184
_R_RB = 192
SMALL_ROWS = 200


def _small_totals(gathered, logits8, name):
    ng = len(GROUPS)

    def body(g_ref, l_ref, main_ref, gains_ref, dlb_ref, rb_ref):
        tot = g_ref[0]
        for dev in range(1, N_DEV):
            tot = tot + g_ref[dev]
        main_ref[...] = tot[0:_R_QG]
        gains_ref[...] = jnp.zeros_like(gains_ref)
        for g in range(ng):
            gains_ref[g:g + 1, :] = jnp.sum(tot[_R_QG + 8 * g:_R_QG + 8 * g + 8], axis=0, keepdims=True)
            gains_ref[ng + g:ng + g + 1, :] = jnp.sum(tot[_R_KG + 8 * g:_R_KG + 8 * g + 8], axis=0, keepdims=True)
        gains_ref[2 * ng:2 * ng + 1, :] = jnp.sum(tot[_R_GN:_R_GN + 8], axis=0, keepdims=True)
        rb_ref[...] = tot[_R_RB:_R_RB + 8]
        dlb1 = tot[_R_LB:_R_LB + 8]
        l0 = l_ref[0]
        l1 = l_ref[1]
        mx = jnp.maximum(l0, l1)
        e0 = jnp.exp(l0 - mx)
        e1 = jnp.exp(l1 - mx)
        p0 = e0 / (e0 + e1)
        p1 = e1 / (e0 + e1)
        dlb_ref[0] = -p0 * p1 * dlb1
        dlb_ref[1] = p1 * (1.0 - p1) * dlb1

    return _pcall(
        body, name=name,
        out_shape=[_sds((_R_QG, 128), F32), _sds((8, 128), F32), _sds((DEPTH, 8, 128), F32), _sds((8, 128), F32)],
        compiler_params=_cparams(),
    )(gathered, logits8)


def _row_tile(rows):
    return 128 if rows % 128 == 0 else rows


def _adamw(w, grads, m, v, name):
    nl, r, cdim = w.shape
    tr = _row_tile(r)

    def body(*refs):
        g_refs = refs[:nl]
        w_ref, m_ref, v_ref, go_ref, d_ref, mo_ref, vo_ref = refs[nl:]

        def step(g):
            m2 = ADAM_B1 * m_ref[...] + (1.0 - ADAM_B1) * g
            v2 = ADAM_B2 * v_ref[...] + (1.0 - ADAM_B2) * (g * g)
            m_hat = m2 / (1.0 - ADAM_B1 ** ADAM_STEP)
            v_hat = v2 / (1.0 - ADAM_B2 ** ADAM_STEP)
            go_ref[...] = g
            d_ref[...] = -ADAM_LR * (m_hat / (jnp.sqrt(v_hat) + ADAM_EPS) + ADAM_WD * w_ref[...])
            mo_ref[...] = m2
            vo_ref[...] = v2

        if nl == 1:
            step(g_refs[0][...])
        else:
            for layer in range(nl):
                @pl.when(pl.program_id(0) == layer)
                def _(layer=layer):
                    step(g_refs[layer][...])

    big = pl.BlockSpec((None, tr, cdim), lambda l, i: (l, i, 0))
    g_specs = [pl.BlockSpec((tr, cdim), lambda l, i, layer=layer: (jnp.where(l == layer, i, 0), 0))
               for layer in range(nl)]
    shp = _sds((nl, r, cdim), F32)
    return _pcall(
        body, name=name, grid=(nl, r // tr),
        in_specs=g_specs + [big, big, big],
        out_specs=[big, big, big, big],
        out_shape=[shp, shp, shp, shp],
        compiler_params=_cparams(("parallel", "parallel")),
    )(*grads, w, m, v)


def _cast_bf16(place, w, name):
    nl, r, cdim = w.shape
    tr = _row_tile(r)

    def body(place_ref, w_ref, o_ref):
        o_ref[...] = w_ref[...].astype(BF16)

    return _pcall(
        body, name=name,
        grid_spec=pltpu.PrefetchScalarGridSpec(
            num_scalar_prefetch=1, grid=(nl, r // tr),
            in_specs=[pl.BlockSpec((None, tr, cdim), lambda l, i, place_ref: (l, i, 0))],
            out_specs=pl.BlockSpec((None, None, tr, cdim), lambda l, i, place_ref: (place_ref[1], l, i, 0))),
        out_shape=_sds((N_CHIP, nl, r, cdim), BF16),
        compiler_params=_cparams(("parallel", "parallel")),
    )(place, w)


def _rs_add_cast(place, grad, recv, name):
    _, k, n = grad.shape
    kh = k // 2
    tr = _row_tile(kh)
    nb = kh // tr

    def body(place_ref, g_ref, r_ref, o_ref):
        o_ref[...] = (g_ref[...] + r_ref[...]).astype(BF16)

    half = pl.BlockSpec((None, tr, n), lambda s, i, place_ref: (s, i, 0))
    return _pcall(
        body, name=name,
        grid_spec=pltpu.PrefetchScalarGridSpec(
            num_scalar_prefetch=1, grid=(N_CHIP, nb),
            in_specs=[pl.BlockSpec((None, tr, n), lambda s, i, place_ref: (s, place_ref[0] * nb + i, 0)), half],
            out_specs=half),
        out_shape=_sds((N_CHIP, kh, n), BF16),
        compiler_params=_cparams(("parallel", "parallel")),
    )(place, grad, recv)


def _rs_sum4(place, parts, got, name):
    _, kh, n = parts.shape
    tr = _row_tile(kh)
    nb = kh // tr

    def body(place_ref, p_ref, g_ref, o_ref):
        acc = p_ref[...].astype(F32)
        for j in range(N_CHIP - 1):
            acc = acc + g_ref[j].astype(F32)
        o_ref[...] = acc

    return _pcall(
        body, name=name,
        grid_spec=pltpu.PrefetchScalarGridSpec(
            num_scalar_prefetch=1, grid=(nb,),
            in_specs=[pl.BlockSpec((None, tr, n), lambda i, place_ref: (place_ref[1], i, 0)),
                      pl.BlockSpec((N_CHIP - 1, tr, n), lambda i, place_ref: (0, i, 0))],
            out_specs=pl.BlockSpec((tr, n), lambda i, place_ref: (place_ref[0] * nb + i, 0))),
        out_shape=_sds((2 * kh, n), F32),
        compiler_params=_cparams(("parallel",)),
    )(place, parts, got)


_ANY = pl.BlockSpec(memory_space=pl.ANY)


def _position():
    return lax.axis_index("x"), lax.axis_index("y"), lax.axis_index("c")


def _other_chips(x, y):
    return [(1 - x, y), (x, 1 - y), (1 - x, 1 - y)]


def _remote(src, dst, send_sem, recv_sem, to):
    return pltpu.make_async_remote_copy(src_ref=src, dst_ref=dst, send_sem=send_sem, recv_sem=recv_sem,
                                        device_id=to, device_id_type=MESH)


def _small_allgather(v, name):
    r = v.shape[0]

    def body(x_ref, out_ref, send_sems, recv_sems, local_sem):
        x, y, c = _position()
        me, sibling = (x, y, c), (x, y, 1 - c)
        chips = _other_chips(x, y)

        def slab(px, py, pc):
            return out_ref.at[4 * px + 2 * py + pc]

        def copy(k, block, to, src=None):
            return _remote(slab(*block) if src is None else src, slab(*block), send_sems.at[k], recv_sems.at[k], to)

        mine = pltpu.make_async_copy(x_ref, slab(*me), local_sem)
        mine.start()
        first = [copy(0, me, sibling, src=x_ref)]
        first += [copy(1 + j, me, (*chip, c), src=x_ref) for j, chip in enumerate(chips)]
        for cp in first:
            cp.start()
        passed = [copy(4 + j, (*chip, c), sibling) for j, chip in enumerate(chips)]
        for j, chip in enumerate(chips):
            copy(1 + j, (*chip, c), me).wait_recv()
            passed[j].start()
        copy(0, sibling, me).wait_recv()
        for j, chip in enumerate(chips):
            copy(4 + j, (*chip, 1 - c), me).wait_recv()
        for cp in first + passed:
            cp.wait_send()
        mine.wait()

    return _pcall(
        body, name=name,
        out_shape=_sds((N_DEV, r, 128), F32),
        in_specs=[pl.BlockSpec(memory_space=pltpu.VMEM)],
        out_specs=pl.BlockSpec(memory_space=pltpu.VMEM),
        scratch_shapes=[pltpu.SemaphoreType.DMA((7,)), pltpu.SemaphoreType.DMA((7,)), pltpu.SemaphoreType.DMA],
        compiler_params=_cparams(),
    )(v)


def _half_rows(core, kh):
    return pl.ds(pl.multiple_of(core * kh, 8), kh)


def _slab_half(ref, chip, core):
    return ref.at[chip, :, _half_rows(core, ref.shape[2] // 2), :]


def _gather_ici(out, send_sems, recv_sems):
    def copies():
        x, y, c = _position()
        for a in range(len(out)):
            for j, (px, py) in enumerate(_other_chips(x, y)):
                mine = _slab_half(out[a], 2 * x + y, c)
                landed = _slab_half(out[a], 2 * px + py, c)
                yield (_remote(mine, mine, send_sems.at[a, j], recv_sems.at[a, j], (px, py, c)),
                       _remote(landed, landed, send_sems.at[a, j], recv_sems.at[a, j], (px, py, c)))

    def start():
        for send, _ in copies():
            send.start()

    def wait():
        for send, recv in copies():
            recv.wait_recv()
            send.wait_send()

    return start, wait


def _gather_d2d(out, send_sems, recv_sems):
    def copies():
        x, y, c = _position()
        for a in range(len(out)):
            for j, (px, py) in enumerate(_other_chips(x, y)):
                landed = _slab_half(out[a], 2 * px + py, c)
                other = _slab_half(out[a], 2 * px + py, 1 - c)
                yield (_remote(landed, landed, send_sems.at[a, j], recv_sems.at[a, j], (x, y, 1 - c)),
                       _remote(other, other, send_sems.at[a, j], recv_sems.at[a, j], (x, y, 1 - c)))

    def start():
        for send, _ in copies():
            send.start()

    def wait():
        for send, recv in copies():
            recv.wait_recv()
            send.wait_send()

    return start, wait


def _gather_weights(slabs, name, ici=True):
    n = len(slabs)

    def body(*refs):
        out = refs[n:2 * n]
        sems = refs[2 * n:]
        if ici:
            start, wait = _gather_ici(out, sems[2], sems[3])
            start()
            wait()
        start, wait = _gather_d2d(out, sems[0], sems[1])
        start()
        wait()

    sem = pltpu.SemaphoreType.DMA((n, 3))
    return _pcall(
        body, name=name,
        out_shape=[_sds(s.shape, BF16) for s in slabs],
        in_specs=[_ANY] * n, out_specs=[_ANY] * n,
        input_output_aliases={a: a for a in range(n)},
        scratch_shapes=[sem, sem] + ([sem, sem] if ici else []),
        compiler_params=_cparams(),
    )(*slabs)


def _rs_exchange_halves(grads, name):
    n = len(grads)

    def body(*refs):
        g = refs[:n]
        out = refs[n:2 * n]
        send_sems, recv_sems = refs[2 * n:]
        x, y, c = _position()
        copies = []
        for a in range(n):
            kh = g[a].shape[1] // 2
            cp = _remote(g[a].at[:, _half_rows(1 - c, kh), :], out[a], send_sems.at[a], recv_sems.at[a], (x, y, 1 - c))
            cp.start()
            copies.append(cp)
        for cp in copies:
            cp.wait()

    return _pcall(
        body, name=name,
        out_shape=[_sds((N_CHIP, g.shape[1] // 2, g.shape[2]), F32) for g in grads],
        in_specs=[_ANY] * n, out_specs=[_ANY] * n,
        scratch_shapes=[pltpu.SemaphoreType.DMA((n,)), pltpu.SemaphoreType.DMA((n,))],
        compiler_params=_cparams(),
    )(*grads)


def _rs_chips(parts, out, send_sems, recv_sems):
    def copies():
        x, y, c = _position()
        for a in range(len(parts)):
            for j, (px, py) in enumerate(_other_chips(x, y)):
                got = out[a].at[j]
                yield (_remote(parts[a].at[2 * px + py], got, send_sems.at[a, j], recv_sems.at[a, j], (px, py, c)),
                       _remote(got, got, send_sems.at[a, j], recv_sems.at[a, j], (px, py, c)))

    def start():
        for send, _ in copies():
            send.start()

    def wait():
        for send, recv in copies():
            recv.wait_recv()
            send.wait_send()

    return start, wait


def _rs_chips_shapes(parts):
    return [_sds((N_CHIP - 1,) + p.shape[1:], BF16) for p in parts]


def _rs_join_halves(fulls, name):
    n = len(fulls)

    def body(*refs):
        out = refs[n:2 * n]
        send_sems, recv_sems = refs[2 * n:]
        x, y, c = _position()
        copies = []
        for a in range(n):
            kh = out[a].shape[0] // 2
            mine = out[a].at[_half_rows(c, kh), :]
            cp = _remote(mine, mine, send_sems.at[a], recv_sems.at[a], (x, y, 1 - c))
            cp.start()
            copies.append(cp)
        for a in range(n):
            kh = out[a].shape[0] // 2
            theirs = out[a].at[_half_rows(1 - c, kh), :]
            _remote(theirs, theirs, send_sems.at[a], recv_sems.at[a], (x, y, 1 - c)).wait_recv()
        for cp in copies:
            cp.wait_send()

    return _pcall(
        body, name=name,
        out_shape=[_sds(f.shape, F32) for f in fulls],
        in_specs=[_ANY] * n, out_specs=[_ANY] * n,
        input_output_aliases={a: a for a in range(n)},
        scratch_shapes=[pltpu.SemaphoreType.DMA((n,)), pltpu.SemaphoreType.DMA((n,))],
        compiler_params=_cparams(),
    )(*fulls)


_SMALL_ORDER = ("rel_bias", "ada_b", "norm_mix", "norm_ffn", "attn_q_gain", "attn_k_gain", "hgrn_gnorm",
                "hgrn_lower_bounds")
_WEIGHT_ORDER = ("rel_bias", "ada_w", "ada_b", "norm_mix", "norm_ffn", "attn_w_qkv", "attn_w_out", "attn_q_gain",
                 "attn_k_gain", "hgrn_w_in", "hgrn_w_out", "hgrn_gnorm", "hgrn_lower_bounds", "ffn_w1", "ffn_w3",
                 "ffn_w2")


def _qkv_group_map(t):
    return t // 4, t % 4


def _qkv_chip_map(t):
    return t // 9, t % 9


def _hin_map(t):
    return t // 2, t % 2


def _block_map(t):
    return t, 0


def _pack_rows(parts):
    return jnp.concatenate([p.reshape(-1, 128) for p in parts], axis=0)


def kernel(x, c, rel_bias, ada_w, ada_b, norm_mix, norm_ffn, attn_w_qkv, attn_w_out, attn_q_gain, attn_k_gain, hgrn_w_in, hgrn_w_out, hgrn_gnorm, hgrn_lower_bounds, ffn_w1, ffn_w3, ffn_w2, loss_target, m_rel_bias, m_ada_w, m_ada_b, m_norm_mix, m_norm_ffn, m_attn_w_qkv, m_attn_w_out, m_attn_q_gain, m_attn_k_gain, m_hgrn_w_in, m_hgrn_w_out, m_hgrn_gnorm, m_hgrn_lower_bounds, m_ffn_w1, m_ffn_w3, m_ffn_w2, v_rel_bias, v_ada_w, v_ada_b, v_norm_mix, v_norm_ffn, v_attn_w_qkv, v_attn_w_out, v_attn_q_gain, v_attn_k_gain, v_hgrn_w_in, v_hgrn_w_out, v_hgrn_gnorm, v_hgrn_lower_bounds, v_ffn_w1, v_ffn_w3, v_ffn_w2):
    weights = dict(rel_bias=rel_bias, ada_w=ada_w, ada_b=ada_b, norm_mix=norm_mix, norm_ffn=norm_ffn,
                   attn_w_qkv=attn_w_qkv, attn_w_out=attn_w_out, attn_q_gain=attn_q_gain, attn_k_gain=attn_k_gain,
                   hgrn_w_in=hgrn_w_in, hgrn_w_out=hgrn_w_out, hgrn_gnorm=hgrn_gnorm,
                   hgrn_lower_bounds=hgrn_lower_bounds, ffn_w1=ffn_w1, ffn_w3=ffn_w3, ffn_w2=ffn_w2)
    mom1 = dict(rel_bias=m_rel_bias, ada_w=m_ada_w, ada_b=m_ada_b, norm_mix=m_norm_mix, norm_ffn=m_norm_ffn,
                attn_w_qkv=m_attn_w_qkv, attn_w_out=m_attn_w_out, attn_q_gain=m_attn_q_gain,
                attn_k_gain=m_attn_k_gain, hgrn_w_in=m_hgrn_w_in, hgrn_w_out=m_hgrn_w_out, hgrn_gnorm=m_hgrn_gnorm,
                hgrn_lower_bounds=m_hgrn_lower_bounds, ffn_w1=m_ffn_w1, ffn_w3=m_ffn_w3, ffn_w2=m_ffn_w2)
    mom2 = dict(rel_bias=v_rel_bias, ada_w=v_ada_w, ada_b=v_ada_b, norm_mix=v_norm_mix, norm_ffn=v_norm_ffn,
                attn_w_qkv=v_attn_w_qkv, attn_w_out=v_attn_w_out, attn_q_gain=v_attn_q_gain,
                attn_k_gain=v_attn_k_gain, hgrn_w_in=v_hgrn_w_in, hgrn_w_out=v_hgrn_w_out, hgrn_gnorm=v_hgrn_gnorm,
                hgrn_lower_bounds=v_hgrn_lower_bounds, ffn_w1=v_ffn_w1, ffn_w3=v_ffn_w3, ffn_w2=v_ffn_w2)

    xi, yi, ci = _position()
    chip = 2 * xi + yi
    dev = 4 * xi + 2 * yi + ci
    place = jnp.stack([ci, chip]).astype(jnp.int32)
    d = D_MODEL

    big_names = ("attn_w_qkv", "attn_w_out", "hgrn_w_in", "hgrn_w_out", "ffn_w1", "ffn_w3", "ffn_w2")
    early_names, late_names = big_names[:2], big_names[2:]
    slabs16 = {k: _cast_bf16(place, weights[k], "cast_" + k) for k in big_names}
    wg = dict(zip(early_names, _gather_weights([slabs16[k] for k in early_names], "gather_early")))

    c_all = _small_allgather(c.reshape(8, 128), "gather_c").reshape(N_DEV, d)
    ada_b_cols = lax.dynamic_slice(ada_b, (0, chip * ADA_SHARD), (DEPTH, ADA_SHARD)).reshape(DEPTH, 1, ADA_SHARD)
    mod_shard = _ada_fwd(c_all, ada_w, ada_b_cols, "ada_fwd")
    mod_all = _small_allgather(mod_shard.reshape(-1, 128), "gather_mod").reshape(N_DEV, DEPTH, N_DEV, ADA_SHARD)
    mod_mine = lax.dynamic_index_in_dim(mod_all[0::2], dev, axis=2, keepdims=False)
    mod = jnp.transpose(mod_mine, (1, 0, 2)).reshape(DEPTH, 6 * d)

    def mods(layer):
        return [mod[layer:layer + 1, j * d:(j + 1) * d] for j in range(6)]

    x0 = x.reshape(SEQ, d)
    target = loss_target.reshape(SEQ, d)
    qg = attn_q_gain.reshape(len(GROUPS), 1, HEAD_DIM)
    kg = attn_k_gain.reshape(len(GROUPS), 1, HEAD_DIM)
    bias = _attn_bias(rel_bias, "attn_bias")
    lb1 = _lower_bounds(hgrn_lower_bounds, "lower_bounds")[1:2]

    def ffn_fwd(layer, x_in, sc2, sh2, g2):
        hf = _norm_mod(x_in, norm_ffn[layer:layer + 1], sc2, sh2, f"l{layer}_norm_ffn")
        a1, a3, u = _ffn_up(hf, wg["ffn_w1"], wg["ffn_w3"], layer, f"l{layer}_ffn_up")
        z, x_out = _mm_rows(u, wg["ffn_w2"], layer, x_in, g2, f"l{layer}_ffn_down")
        return x_out, (hf, a1, a3, u, z)

    def ffn_bwd(layer, dx_out, x_in, sc2, sh2, g2, saved):
        hf, a1, a3, u, z = saved
        dz, dg2 = _gate_bwd(dx_out, z, g2, f"l{layer}_ffn_gate_bwd")
        da1, da3 = _ffn_down_bwd(dz, wg["ffn_w2"], layer, a1, a3, f"l{layer}_ffn_down_bwd")
        dw2 = _mm_rows_bwd_w(u, dz, f"l{layer}_dw2")
        dh = _mm_cols_bwd_a([(da1, wg["ffn_w1"], layer), (da3, wg["ffn_w3"], layer)], tn=FFN_SHARD,
                            act_map=_block_map, w_map=_block_map, n_tiles=N_CHIP, name=f"l{layer}_ffn_up_bwd")
        dw1 = _mm_cols_bwd_w(hf, da1, ns=FFN_SHARD, tn=FFN_SHARD, act_map=_block_map, w_map=_block_map,
                             n_tiles=N_CHIP, name=f"l{layer}_dw1")
        dw3 = _mm_cols_bwd_w(hf, da3, ns=FFN_SHARD, tn=FFN_SHARD, act_map=_block_map, w_map=_block_map,
                             n_tiles=N_CHIP, name=f"l{layer}_dw3")
        dx_in, dsc2, dsh2, dnf = _norm_mod_bwd(x_in, norm_ffn[layer:layer + 1], sc2, sh2, dh, dx_out,
                                               f"l{layer}_norm_ffn_bwd")
        return dx_in, (dw1, dw3, dw2), (dsh2, dsc2, dg2), dnf

    sh1_0, sc1_0, g1_0, sh2_0, sc2_0, g2_0 = mods(0)
    h0 = _norm_mod(x0, norm_mix[0:1], sc1_0, sh1_0, "l0_norm_mix")
    w_qkv9 = _retile_cols(wg["attn_w_qkv"].reshape(N_CHIP, d, 2304), n_out=9, width_out=d, tn=256,
                          src_map=_qkv_chip_map, dst_map=_qkv_group_map, n_tiles=36,
                          name="regroup_w_qkv").reshape(9, 1, d, d)
    qkv9 = _mm_cols(h0, w_qkv9, 0, n_blocks=9, width=d, tn=d, act_map=_block_map, w_map=_block_map,
                    out_dtype=F32, name="l0_qkv")
    o4, lse, *late = _attn_fwd(qkv9, qg, kg, bias, "l0_attn", gather=[slabs16[k] for k in late_names])
    wg.update(zip(late_names, _gather_weights(late, "gather_late_siblings", ici=False)))
    y0, x1 = _mm_rows(o4, wg["attn_w_out"], 0, x0, g1_0, "l0_attn_out")
    x2, ffn0 = ffn_fwd(0, x1, sc2_0, sh2_0, g2_0)

    sh1_1, sc1_1, g1_1, sh2_1, sc2_1, g2_1 = mods(1)
    h1 = _norm_mod(x2, norm_mix[1:2], sc1_1, sh1_1, "l1_norm_mix")
    proj4 = _mm_cols(h1, wg["hgrn_w_in"], 0, n_blocks=4, width=d, tn=512, act_map=_hin_map, w_map=_hin_map,
                     out_dtype=F32, name="l1_hgrn_in")
    o_raw, yg4, states = _hgrn_fwd(proj4, lb1, hgrn_gnorm, "l1_hgrn")
    y1, x3 = _mm_rows(yg4, wg["hgrn_w_out"], 0, x2, g1_1, "l1_hgrn_out")
    x4, ffn1 = ffn_fwd(1, x3, sc2_1, sh2_1, g2_1)

    dx4, loss_part = _loss_head(x4, target, "loss_head")
    loss = lax.psum(loss_part[0, 0], ("x", "y", "c"))

    dx3, (dw1_1, dw3_1, dw2_1), dmod2_1, dnf_1 = ffn_bwd(1, dx4, x3, sc2_1, sh2_1, g2_1, ffn1)
    dzm1, dg1_1 = _gate_bwd(dx3, y1, g1_1, "l1_mix_gate_bwd")
    dyg4 = _mm_rows_bwd_a(dzm1, wg["hgrn_w_out"], 0, "l1_hgrn_out_bwd")
    dw_hout = _mm_rows_bwd_w(yg4, dzm1, "l1_dw_hgrn_out")
    dproj4, dlb_h, dgn_h = _hgrn_bwd(proj4, lb1, hgrn_gnorm, o_raw, dyg4, states, "l1_hgrn_bwd")
    dh1 = _mm_cols_bwd_a([(dproj4, wg["hgrn_w_in"], 0)], tn=512, act_map=_hin_map, w_map=_hin_map, n_tiles=8,
                         name="l1_hgrn_in_bwd")
    dw_hin = _mm_cols_bwd_w(h1, dproj4, ns=d, tn=512, act_map=_hin_map, w_map=_hin_map, n_tiles=8,
                            name="l1_dw_hgrn_in")
    dx2, dsc1_1, dsh1_1, dnm_1 = _norm_mod_bwd(x2, norm_mix[1:2], sc1_1, sh1_1, dh1, dx3, "l1_norm_mix_bwd")

    dx1, (dw1_0, dw3_0, dw2_0), dmod2_0, dnf_0 = ffn_bwd(0, dx2, x1, sc2_0, sh2_0, g2_0, ffn0)
    dzm0, dg1_0 = _gate_bwd(dx1, y0, g1_0, "l0_mix_gate_bwd")
    do4 = _mm_rows_bwd_a(dzm0, wg["attn_w_out"], 0, "l0_attn_out_bwd")
    dw_aout = _mm_rows_bwd_w(o4, dzm0, "l0_dw_attn_out")

    def rs_prepare(tags, grads_in, suffix):
        recv = _rs_exchange_halves(grads_in, "rs_exchange_halves_" + suffix)
        return [_rs_add_cast(place, g, r, f"rs_add_{k}_{layer}") for (k, layer), g, r in zip(tags, grads_in, recv)]

    tags_a = [("attn_w_out", 0), ("hgrn_w_in", 0), ("hgrn_w_out", 0), ("ffn_w1", 0), ("ffn_w1", 1), ("ffn_w3", 0),
              ("ffn_w3", 1), ("ffn_w2", 0), ("ffn_w2", 1)]
    parts_a = rs_prepare(tags_a, [dw_aout, dw_hin, dw_hout, dw1_0, dw1_1, dw3_0, dw3_1, dw2_0, dw2_1], "a")
    dqkv, dqg_h, dkg_h, dbias, *got_a = _attn_bwd(qkv9, qg, kg, bias, do4, o4, lse, "l0_attn_bwd", scatter=parts_a)
    dqkv9 = dqkv.reshape(9, SEQ, d)
    dw_qkv9 = _mm_cols_bwd_w(h0, dqkv9, ns=d, tn=d, act_map=_block_map, w_map=_block_map, n_tiles=9,
                             name="l0_dw_qkv", tm=512, n_out=9)
    dw_qkv = _retile_cols(dw_qkv9, n_out=N_CHIP, width_out=2304, tn=256, src_map=_qkv_group_map,
                          dst_map=_qkv_chip_map, n_tiles=36, name="regroup_dw_qkv")
    tags_b = [("attn_w_qkv", 0)]
    parts_b = rs_prepare(tags_b, [dw_qkv], "b")
    dh0, *got_b = _mm_cols_bwd_a([(dqkv9, w_qkv9, 0)], tn=d, act_map=_block_map, w_map=_block_map, n_tiles=9,
                                 name="l0_qkv_bwd", scatter=parts_b)
    dx0, dsc1_0, dsh1_0, dnm_0 = _norm_mod_bwd(x0, norm_mix[0:1], sc1_0, sh1_0, dh0, dx1, "l0_norm_mix_bwd")
    drb8 = _relbias_bwd(dbias, jnp.asarray(_bias_tables()), "rel_bias_bwd")

    small = _pack_rows([
        dsh1_0, dsc1_0, dg1_0, *dmod2_0, dsh1_1, dsc1_1, dg1_1, *dmod2_1,
        dnm_0, dnm_1, dnf_0, dnf_1,
        jnp.transpose(dqg_h, (1, 0, 2, 3)), jnp.transpose(dkg_h, (1, 0, 2, 3)), dgn_h, dlb_h, drb8])
    small_all = _small_allgather(small, "gather_small")
    main, gains, dlbnd, rbt = _small_totals(small_all, hgrn_lower_bounds.reshape(DEPTH, 8, 128), "small_totals")
    ng = len(GROUPS)
    grads = {
        "ada_b": main[_R_DMOD:_R_NMIX].reshape(DEPTH, 6 * d),
        "norm_mix": main[_R_NMIX:_R_NFFN].reshape(DEPTH, d),
        "norm_ffn": main[_R_NFFN:_R_QG].reshape(DEPTH, d),
        "attn_q_gain": gains[0:ng].reshape(1, ng, HEAD_DIM),
        "attn_k_gain": gains[ng:2 * ng].reshape(1, ng, HEAD_DIM),
        "hgrn_gnorm": gains[2 * ng:2 * ng + 1],
        "hgrn_lower_bounds": dlbnd.reshape(DEPTH, d),
        "rel_bias": jnp.transpose(rbt[:, :ng * NUM_BUCKETS].reshape(HEADS, ng, NUM_BUCKETS), (2, 1, 0))
                       .reshape(NUM_BUCKETS, ng * HEADS),
    }
    dmod_all = small_all[:, _R_DMOD:_R_NMIX].reshape(N_DEV, DEPTH, 6 * d)
    dmod_cols = jnp.transpose(lax.dynamic_slice(dmod_all, (0, 0, chip * ADA_SHARD), (N_DEV, DEPTH, ADA_SHARD)),
                              (1, 0, 2))
    grad_ada_w = _ada_bwd(c_all, dmod_cols, "ada_bwd")

    tags = tags_a + tags_b
    halves = [_rs_sum4(place, p, r, f"rs_sum_{k}_{layer}")
              for (k, layer), p, r in zip(tags, parts_a + parts_b, list(got_a) + list(got_b))]
    full = dict(zip(tags, _rs_join_halves(halves, "rs_join_halves")))

    out_g, out_d, out_m, out_v = {}, {}, {}, {}
    for k in big_names:
        gs = [full[(k, layer)] for layer in range(weights[k].shape[0])]
        out_g[k], out_d[k], out_m[k], out_v[k] = _adamw(weights[k], gs, mom1[k], mom2[k], "adamw_" + k)
    shp = (1, DEPTH * d, ADA_SHARD)
    res = _adamw(ada_w.reshape(shp), [grad_ada_w.reshape(shp[1:])], m_ada_w.reshape(shp), v_ada_w.reshape(shp),
                 "adamw_ada_w")
    out_g["ada_w"], out_d["ada_w"], out_m["ada_w"], out_v["ada_w"] = [r.reshape(ada_w.shape) for r in res]
    packed = [_pack_rows([src[k] for k in _SMALL_ORDER])[None] for src in (weights, grads, mom1, mom2)]
    res = _adamw(packed[0], [packed[1][0]], packed[2], packed[3], "adamw_small")
    offset = 0
    for k in _SMALL_ORDER:
        size = weights[k].size
        for dst, r in zip((out_g, out_d, out_m, out_v), res):
            dst[k] = r.reshape(-1)[offset:offset + size].reshape(weights[k].shape)
        offset += size

    return (loss, dx0.reshape(x.shape), *[out_g[k] for k in _WEIGHT_ORDER], *[out_d[k] for k in _WEIGHT_ORDER],
            *[out_m[k] for k in _WEIGHT_ORDER], *[out_v[k] for k in _WEIGHT_ORDER])
```

```python
import functools

import numpy as np
import jax
import jax.numpy as jnp
from jax import lax
from jax.experimental import pallas as pl
from jax.experimental.pallas import tpu as pltpu

F32 = jnp.float32
BF16 = jnp.bfloat16

D_MODEL = 1024
SEQ = 4096
N_DEV = 8
N_CHIP = 4
DEPTH = 2
HEADS = 8
HEAD_DIM = 128
GROUPS = ((128, 1), (512, 4), (2048, 16))
ATT_BLK = 128
ATT_WAYS = 4
ATT_STEPS = SEQ // ATT_BLK // ATT_WAYS
NUM_BUCKETS = 32
MAX_DISTANCE = 2048
FFN_HIDDEN = 2816
FFN_SHARD = FFN_HIDDEN // N_CHIP
HG_SUB = 16
HG_TC = 512
HG_HP = 4
RMS_EPS = 1e-6
NEG = -1e30
ATT_SCALE = HEAD_DIM ** -0.5
ADAM_LR, ADAM_B1, ADAM_B2, ADAM_EPS, ADAM_WD, ADAM_STEP = 0.001, 0.9, 0.999, 1e-08, 0.01, 10
VMEM_LIMIT = 56 * 1024 * 1024
MESH = pl.DeviceIdType.MESH


def _pcall(body, **kw):
    return pl.pallas_call(body, **kw)


def _cparams(sem=None):
    if sem is None:
        return pltpu.CompilerParams(vmem_limit_bytes=VMEM_LIMIT)
    return pltpu.CompilerParams(dimension_semantics=sem, vmem_limit_bytes=VMEM_LIMIT)


def _sds(shape, dtype):
    return jax.ShapeDtypeStruct(shape, dtype)


def _dot(a, b):
    return jnp.dot(a, b, preferred_element_type=F32)


def _dot_nt(a, b):
    return lax.dot_general(a, b, (((1,), (1,)), ((), ())), preferred_element_type=F32)


def _dot_tn(a, b):
    return lax.dot_general(a, b, (((0,), (0,)), ((), ())), preferred_element_type=F32)


def _sigmoid(x):
    return 1.0 / (1.0 + jnp.exp(-x))


def _silu(x):
    return x * _sigmoid(x)


def _dsilu(x):
    s = _sigmoid(x)
    return s * (1.0 + x * (1.0 - s))


def _norm_mod(x, gain, sc, sh, name):
    tm = 512

    def body(x_ref, g_ref, sc_ref, sh_ref, h_ref):
        xv = x_ref[...]
        rs = lax.rsqrt(jnp.mean(xv * xv, axis=-1, keepdims=True) + RMS_EPS)
        h_ref[...] = ((xv * rs * g_ref[...]) * (1.0 + sc_ref[...]) + sh_ref[...]).astype(BF16)

    vec = pl.BlockSpec((1, D_MODEL), lambda i: (0, 0))
    return _pcall(
        body, name=name, grid=(SEQ // tm,),
        in_specs=[pl.BlockSpec((tm, D_MODEL), lambda i: (i, 0)), vec, vec, vec],
        out_specs=pl.BlockSpec((tm, D_MODEL), lambda i: (i, 0)),
        out_shape=_sds((SEQ, D_MODEL), BF16),
        compiler_params=_cparams(("parallel",)),
    )(x, gain, sc, sh)


def _norm_mod_bwd(x, gain, sc, sh, dh, dres, name):
    tm = 512

    def body(x_ref, g_ref, sc_ref, sh_ref, dh_ref, dres_ref, dx_ref, dsc_ref, dsh_ref, dg_ref):
        @pl.when(pl.program_id(0) == 0)
        def _():
            dsc_ref[...] = jnp.zeros_like(dsc_ref)
            dsh_ref[...] = jnp.zeros_like(dsh_ref)
            dg_ref[...] = jnp.zeros_like(dg_ref)

        xv = x_ref[...]
        dhv = dh_ref[...]
        rs = lax.rsqrt(jnp.mean(xv * xv, axis=-1, keepdims=True) + RMS_EPS)
        xh = xv * rs
        dsc_ref[...] += jnp.sum(dhv * (xh * g_ref[...]), axis=0, keepdims=True)
        dsh_ref[...] += jnp.sum(dhv, axis=0, keepdims=True)
        dhn = dhv * (1.0 + sc_ref[...])
        dg_ref[...] += jnp.sum(dhn * xh, axis=0, keepdims=True)
        dxh = dhn * g_ref[...]
        dx_ref[...] = dres_ref[...] + rs * (dxh - xh * jnp.mean(dxh * xh, axis=-1, keepdims=True))

    vec = pl.BlockSpec((1, D_MODEL), lambda i: (0, 0))
    big = pl.BlockSpec((tm, D_MODEL), lambda i: (i, 0))
    return _pcall(
        body, name=name, grid=(SEQ // tm,),
        in_specs=[big, vec, vec, vec, big, big],
        out_specs=[big, vec, vec, vec],
        out_shape=[_sds((SEQ, D_MODEL), F32)] + [_sds((1, D_MODEL), F32)] * 3,
        compiler_params=_cparams(("arbitrary",)),
    )(x, gain, sc, sh, dh, dres)


def _mm_cols(a, wg, layer, *, n_blocks, width, tn, act_map, w_map, out_dtype, name, tm=1024):
    k = a.shape[1]
    n_tiles = n_blocks * width // tn

    def body(a_ref, w_ref, o_ref):
        o_ref[...] = _dot(a_ref[...], w_ref[...]).astype(o_ref.dtype)

    return _pcall(
        body, name=name, grid=(SEQ // tm, n_tiles),
        in_specs=[pl.BlockSpec((tm, k), lambda i, t: (i, 0)),
                  pl.BlockSpec((None, None, k, tn), lambda i, t: (w_map(t)[0], layer, 0, w_map(t)[1]))],
        out_specs=pl.BlockSpec((None, tm, tn), lambda i, t: (act_map(t)[0], i, act_map(t)[1])),
        out_shape=_sds((n_blocks, SEQ, width), out_dtype),
        compiler_params=_cparams(("parallel", "arbitrary")),
    )(a, wg)


def _mm_cols_bwd_a(pairs, *, tn, act_map, w_map, n_tiles, name, tm=1024, scatter=()):
    k = pairs[0][1].shape[2]
    n_p = len(pairs)
    n_s = len(scatter)
    n_rows = SEQ // tm

    def body(*refs):
        o_ref = refs[2 * n_p + n_s]
        if n_s:
            comm_start, comm_wait = _rs_chips(refs[2 * n_p:2 * n_p + n_s], refs[2 * n_p + n_s + 1:2 * n_p + 2 * n_s + 1],
                                              *refs[2 * n_p + 2 * n_s + 1:])
            pl.when((pl.program_id(0) == 0) & (pl.program_id(1) == 0))(comm_start)

        @pl.when(pl.program_id(1) == 0)
        def _():
            o_ref[...] = jnp.zeros_like(o_ref)

        acc = _dot_nt(refs[0][...], refs[1][...])
        for p in range(1, n_p):
            acc += _dot_nt(refs[2 * p][...], refs[2 * p + 1][...])
        o_ref[...] += acc
        if n_s:
            pl.when((pl.program_id(0) == n_rows - 1) & (pl.program_id(1) == n_tiles - 1))(comm_wait)

    in_specs, args = [], []
    for dout, wg, layer in pairs:
        in_specs.append(pl.BlockSpec((None, tm, tn), lambda i, t: (act_map(t)[0], i, act_map(t)[1])))
        in_specs.append(pl.BlockSpec((None, None, k, tn),
                                     lambda i, t, layer=layer: (w_map(t)[0], layer, 0, w_map(t)[1])))
        args += [dout, wg]
    sem = pltpu.SemaphoreType.DMA((max(n_s, 1), 3))
    res = _pcall(
        body, name=name, grid=(n_rows, n_tiles),
        in_specs=in_specs + [_ANY] * n_s,
        out_specs=[pl.BlockSpec((tm, k), lambda i, t: (i, 0))] + [_ANY] * n_s,
        out_shape=[_sds((SEQ, k), F32)] + _rs_chips_shapes(scatter),
        scratch_shapes=[sem, sem] if n_s else [],
        compiler_params=_cparams(("arbitrary", "arbitrary") if n_s else ("parallel", "arbitrary")),
    )(*args, *scatter)
    return res if n_s else res[0]


def _mm_cols_bwd_w(a, dout, *, ns, tn, act_map, w_map, n_tiles, name, tm=1024, n_out=N_CHIP):
    k = a.shape[1]

    def body(a_ref, d_ref, o_ref):
        @pl.when(pl.program_id(1) == 0)
        def _():
            o_ref[...] = jnp.zeros_like(o_ref)

        o_ref[...] += _dot_tn(a_ref[...], d_ref[...])

    return _pcall(
        body, name=name, grid=(n_tiles, SEQ // tm),
        in_specs=[pl.BlockSpec((tm, k), lambda t, i: (i, 0)),
                  pl.BlockSpec((None, tm, tn), lambda t, i: (act_map(t)[0], i, act_map(t)[1]))],
        out_specs=pl.BlockSpec((None, k, tn), lambda t, i: (w_map(t)[0], 0, w_map(t)[1])),
        out_shape=_sds((n_out, k, ns), F32),
        compiler_params=_cparams(("parallel", "arbitrary")),
    )(a, dout)


def _retile_cols(src, *, n_out, width_out, tn, src_map, dst_map, n_tiles, name):
    k = src.shape[1]

    def body(s_ref, o_ref):
        o_ref[...] = s_ref[...]

    return _pcall(
        body, name=name, grid=(n_tiles,),
        in_specs=[pl.BlockSpec((None, k, tn), lambda t: (src_map(t)[0], 0, src_map(t)[1]))],
        out_specs=pl.BlockSpec((None, k, tn), lambda t: (dst_map(t)[0], 0, dst_map(t)[1])),
        out_shape=_sds((n_out, k, width_out), src.dtype),
        compiler_params=_cparams(("parallel",)),
    )(src)


def _mm_rows(a4, wg, layer, x, gate, name, tm=1024):
    ks = a4.shape[2]
    n = wg.shape[3]

    def body(a_ref, w_ref, x_ref, g_ref, z_ref, xn_ref):
        s = pl.program_id(1)

        @pl.when(s == 0)
        def _():
            z_ref[...] = jnp.zeros_like(z_ref)

        z_ref[...] += _dot(a_ref[...], w_ref[...])

        @pl.when(s == N_CHIP - 1)
        def _():
            xn_ref[...] = x_ref[...] + g_ref[...] * z_ref[...]

    big = pl.BlockSpec((tm, n), lambda i, s: (i, 0))
    return _pcall(
        body, name=name, grid=(SEQ // tm, N_CHIP),
        in_specs=[pl.BlockSpec((None, tm, ks), lambda i, s: (s, i, 0)),
                  pl.BlockSpec((None, None, ks, n), lambda i, s: (s, layer, 0, 0)),
                  big, pl.BlockSpec((1, n), lambda i, s: (0, 0))],
        out_specs=[big, big],
        out_shape=[_sds((SEQ, n), F32), _sds((SEQ, n), F32)],
        compiler_params=_cparams(("parallel", "arbitrary")),
    )(a4, wg, x, gate)


def _gate_bwd(dx, z, gate, name):
    tm = 512

    def body(dx_ref, z_ref, g_ref, dz_ref, dg_ref):
        @pl.when(pl.program_id(0) == 0)
        def _():
            dg_ref[...] = jnp.zeros_like(dg_ref)

        dxv = dx_ref[...]
        dz_ref[...] = (dxv * g_ref[...]).astype(BF16)
        dg_ref[...] += jnp.sum(dxv * z_ref[...], axis=0, keepdims=True)

    big = pl.BlockSpec((tm, D_MODEL), lambda i: (i, 0))
    vec = pl.BlockSpec((1, D_MODEL), lambda i: (0, 0))
    return _pcall(
        body, name=name, grid=(SEQ // tm,),
        in_specs=[big, big, vec], out_specs=[big, vec],
        out_shape=[_sds((SEQ, D_MODEL), BF16), _sds((1, D_MODEL), F32)],
        compiler_params=_cparams(("arbitrary",)),
    )(dx, z, gate)


def _mm_rows_bwd_a(dz, wg, layer, name, tm=1024):
    ks, n = wg.shape[2], wg.shape[3]

    def body(dz_ref, w_ref, o_ref):
        o_ref[...] = _dot_nt(dz_ref[...], w_ref[...])

    return _pcall(
        body, name=name, grid=(SEQ // tm, N_CHIP),
        in_specs=[pl.BlockSpec((tm, n), lambda i, s: (i, 0)),
                  pl.BlockSpec((None, None, ks, n), lambda i, s: (s, layer, 0, 0))],
        out_specs=pl.BlockSpec((None, tm, ks), lambda i, s: (s, i, 0)),
        out_shape=_sds((N_CHIP, SEQ, ks), F32),
        compiler_params=_cparams(("parallel", "arbitrary")),
    )(dz, wg)


def _mm_rows_bwd_w(a4, dz, name, tm=1024):
    ks = a4.shape[2]
    n = dz.shape[1]

    def body(a_ref, dz_ref, o_ref):
        @pl.when(pl.program_id(1) == 0)
        def _():
            o_ref[...] = jnp.zeros_like(o_ref)

        o_ref[...] += _dot_tn(a_ref[...], dz_ref[...])

    return _pcall(
        body, name=name, grid=(N_CHIP, SEQ // tm),
        in_specs=[pl.BlockSpec((None, tm, ks), lambda s, i: (s, i, 0)),
                  pl.BlockSpec((tm, n), lambda s, i: (i, 0))],
        out_specs=pl.BlockSpec((None, ks, n), lambda s, i: (s, 0, 0)),
        out_shape=_sds((N_CHIP, ks, n), F32),
        compiler_params=_cparams(("parallel", "arbitrary")),
    )(a4, dz)


def _ffn_up(h, w1g, w3g, layer, name, tm=1024):
    def body(h_ref, w1_ref, w3_ref, a1_ref, a3_ref, u_ref):
        hv = h_ref[...]
        a1 = _dot(hv, w1_ref[...])
        a3 = _dot(hv, w3_ref[...])
        a1_ref[...] = a1
        a3_ref[...] = a3
        u_ref[...] = (_silu(a1) * a3).astype(BF16)

    wspec = pl.BlockSpec((None, None, D_MODEL, FFN_SHARD), lambda i, s: (s, layer, 0, 0))
    ospec = pl.BlockSpec((None, tm, FFN_SHARD), lambda i, s: (s, i, 0))
    shp = (N_CHIP, SEQ, FFN_SHARD)
    return _pcall(
        body, name=name, grid=(SEQ // tm, N_CHIP),
        in_specs=[pl.BlockSpec((tm, D_MODEL), lambda i, s: (i, 0)), wspec, wspec],
        out_specs=[ospec, ospec, ospec],
        out_shape=[_sds(shp, F32), _sds(shp, F32), _sds(shp, BF16)],
        compiler_params=_cparams(("parallel", "arbitrary")),
    )(h, w1g, w3g)


def _ffn_down_bwd(dz, w2g, layer, a1, a3, name, tm=1024):
    def body(dz_ref, w_ref, a1_ref, a3_ref, da1_ref, da3_ref):
        du = _dot_nt(dz_ref[...], w_ref[...])
        a1 = a1_ref[...]
        da1_ref[...] = (du * a3_ref[...] * _dsilu(a1)).astype(BF16)
        da3_ref[...] = (du * _silu(a1)).astype(BF16)

    blk = pl.BlockSpec((None, tm, FFN_SHARD), lambda i, s: (s, i, 0))
    shp = (N_CHIP, SEQ, FFN_SHARD)
    return _pcall(
        body, name=name, grid=(SEQ // tm, N_CHIP),
        in_specs=[pl.BlockSpec((tm, D_MODEL), lambda i, s: (i, 0)),
                  pl.BlockSpec((None, None, FFN_SHARD, D_MODEL), lambda i, s: (s, layer, 0, 0)),
                  blk, blk],
        out_specs=[blk, blk],
        out_shape=[_sds(shp, BF16), _sds(shp, BF16)],
        compiler_params=_cparams(("parallel", "arbitrary")),
    )(dz, w2g, a1, a3)


def _loss_head(y, target, name):
    tm = 512

    def body(y_ref, t_ref, dy_ref, l_ref, acc_ref):
        @pl.when(pl.program_id(0) == 0)
        def _():
            acc_ref[...] = jnp.zeros_like(acc_ref)

        err = y_ref[...] - t_ref[...]
        dy_ref[...] = err * (1.0 / D_MODEL)
        acc_ref[...] += jnp.sum(jnp.mean(err * err, axis=-1, keepdims=True), axis=0, keepdims=True)

        @pl.when(pl.program_id(0) == pl.num_programs(0) - 1)
        def _():
            l_ref[...] = 0.5 * acc_ref[...]

    big = pl.BlockSpec((tm, D_MODEL), lambda i: (i, 0))
    return _pcall(
        body, name=name, grid=(SEQ // tm,),
        in_specs=[big, big],
        out_specs=[big, pl.BlockSpec((1, 1), lambda i: (0, 0))],
        out_shape=[_sds((SEQ, D_MODEL), F32), _sds((1, 1), F32)],
        scratch_shapes=[pltpu.VMEM((1, 1), F32)],
        compiler_params=_cparams(("arbitrary",)),
    )(y, target)


def _attn_rows(base, d):
    if d == 1:
        return pl.ds(pl.multiple_of(base, ATT_BLK), ATT_BLK)
    return pl.ds(base, ATT_BLK, stride=d)


def _attn_block_index(i, d):
    nb = SEQ // (ATT_BLK * d)
    r = i // nb
    n = i % nb
    base = r + n * (ATT_BLK * d)
    pbase = jnp.maximum(base - ATT_BLK * d, r)
    return n, _attn_rows(base, d), _attn_rows(pbase, d)


def _attn_two_blocks(ref, prow, rows):
    return jnp.concatenate([ref[prow, :].astype(BF16), ref[rows, :].astype(BF16)], axis=0)


def _attn_block_bias(b_ref, n):
    b = b_ref[...]
    prev_half = lax.broadcasted_iota(jnp.int32, b.shape, 1) < ATT_BLK
    return jnp.where(prev_half & (n == 0), NEG, b)


def _qk_normed(x):
    rs = lax.rsqrt(jnp.mean(x * x, axis=-1, keepdims=True) + RMS_EPS)
    return x * rs, rs


def _attn_fwd(qkv9, qgain, kgain, bias, name, gather=()):
    n_g = len(gather)

    def body(*refs):
        q_ref, k_ref, v_ref, qg_ref, kg_ref, b_ref = refs[:6]
        o_ref, lse_ref = refs[6 + n_g:8 + n_g]
        qn_s, kn_s, acc_s, m_s, l_s = refs[8 + 2 * n_g:13 + 2 * n_g]
        g = pl.program_id(1)
        if n_g:
            comm_start, comm_wait = _gather_ici(refs[8 + n_g:8 + 2 * n_g], *refs[13 + 2 * n_g:])
            pl.when((pl.program_id(0) == 0) & (g == 0))(comm_start)

        @pl.when(g == 0)
        def _():
            m_s[...] = jnp.full_like(m_s, NEG)
            l_s[...] = jnp.zeros_like(l_s)
            acc_s[...] = jnp.zeros_like(acc_s)

        qn_s[...] = _qk_normed(q_ref[...])[0] * qg_ref[...]
        kn_s[...] = _qk_normed(k_ref[...])[0] * kg_ref[...]

        for gi, (_, d) in enumerate(GROUPS):
            @pl.when(g == gi)
            def _(d=d):
                def block(n, qb, kk, vv, m_old, l_old, acc_old):
                    s = _dot_nt(qb, kk) * ATT_SCALE + _attn_block_bias(b_ref, n)
                    m_new = jnp.maximum(m_old, jnp.max(s, axis=-1, keepdims=True))
                    alpha = jnp.exp(m_old - m_new)
                    p = jnp.exp(s - m_new)
                    l_new = alpha * l_old + jnp.sum(p, axis=-1, keepdims=True)
                    acc_new = alpha * acc_old + _dot(p.astype(BF16), vv)
                    return m_new, l_new, acc_new

                def it(i, carry):
                    where, loaded = [], []
                    for way in range(ATT_WAYS):
                        n, rows, prow = _attn_block_index(i + way * ATT_STEPS, d)
                        where.append(rows)
                        loaded.append((n, qn_s[rows, :].astype(BF16), _attn_two_blocks(kn_s, prow, rows),
                                       _attn_two_blocks(v_ref, prow, rows), m_s[rows, :], l_s[rows, :],
                                       acc_s[rows, :]))
                    results = [block(*vals) for vals in loaded]
                    for rows, (m_new, l_new, acc_new) in zip(where, results):
                        m_s[rows, :] = m_new
                        l_s[rows, :] = l_new
                        acc_s[rows, :] = acc_new
                    return carry

                lax.fori_loop(0, ATT_STEPS, it, 0)

        @pl.when(g == len(GROUPS) - 1)
        def _():
            o_ref[...] = (acc_s[...] / l_s[...]).astype(BF16)
            lse_ref[...] = m_s[...] + jnp.log(l_s[...])

        if n_g:
            pl.when((pl.program_id(0) == HEADS - 1) & (g == len(GROUPS) - 1))(comm_wait)

    def col(j):
        return pl.BlockSpec((None, SEQ, HEAD_DIM), lambda h, g: (g * 3 + j, 0, h))

    gspec = pl.BlockSpec((None, 1, HEAD_DIM), lambda h, g: (g, 0, 0))
    sem = pltpu.SemaphoreType.DMA((max(n_g, 1), 3))
    return _pcall(
        body, name=name, grid=(HEADS, len(GROUPS)),
        in_specs=[col(0), col(1), col(2), gspec, gspec,
                  pl.BlockSpec((None, None, ATT_BLK, 2 * ATT_BLK), lambda h, g: (g, h, 0, 0))] + [_ANY] * n_g,
        out_specs=[pl.BlockSpec((None, SEQ, HEAD_DIM), lambda h, g: (h // 2, 0, h % 2)),
                   pl.BlockSpec((None, SEQ, 1), lambda h, g: (h, 0, 0))] + [_ANY] * n_g,
        out_shape=[_sds((N_CHIP, SEQ, 2 * HEAD_DIM), BF16), _sds((HEADS, SEQ, 1), F32)]
        + [_sds(s.shape, s.dtype) for s in gather],
        input_output_aliases={6 + a: 2 + a for a in range(n_g)},
        scratch_shapes=[pltpu.VMEM((SEQ, HEAD_DIM), F32)] * 3 + [pltpu.VMEM((SEQ, 1), F32)] * 2
        + ([sem, sem] if n_g else []),
        compiler_params=_cparams(("arbitrary", "arbitrary")),
    )(qkv9, qkv9, qkv9, qgain, kgain, bias, *gather)


def _attn_bwd(qkv9, qgain, kgain, bias, do4, o4, lse, name, scatter=()):
    n_s = len(scatter)

    def body(*refs):
        q_ref, k_ref, v_ref, qg_ref, kg_ref, b_ref, do_ref, o_ref, lse_ref = refs[:9]
        dqkv_ref, dqg_ref, dkg_ref, db_ref = refs[9 + n_s:13 + n_s]
        qn_s, kn_s, dq_s, dk_s, dv_s, dl_s = refs[13 + 2 * n_s:19 + 2 * n_s]
        g = pl.program_id(1)
        if n_s:
            comm_start, comm_wait = _rs_chips(refs[9:9 + n_s], refs[13 + n_s:13 + 2 * n_s], *refs[19 + 2 * n_s:])
            pl.when((pl.program_id(0) == 0) & (g == 0))(comm_start)
        qh, rq = _qk_normed(q_ref[...])
        kh, rk = _qk_normed(k_ref[...])
        qn_s[...] = qh * qg_ref[...]
        kn_s[...] = kh * kg_ref[...]
        dl_s[...] = jnp.sum(do_ref[...] * o_ref[...].astype(F32), axis=-1, keepdims=True)
        dk_s[...] = jnp.zeros_like(dk_s)
        dv_s[...] = jnp.zeros_like(dv_s)
        db_ref[...] = jnp.zeros_like(db_ref)

        for gi, (_, d) in enumerate(GROUPS):
            @pl.when(g == gi)
            def _(d=d):
                def block(n, qb, kk, vv, dob, lse_b, dl):
                    s = _dot_nt(qb, kk) * ATT_SCALE + _attn_block_bias(b_ref, n)
                    p = jnp.exp(s - lse_b)
                    ds = p * (_dot_nt(dob, vv) - dl)
                    ds16 = ds.astype(BF16)
                    return (ds, _dot(ds16, kk) * ATT_SCALE, _dot_tn(ds16, qb) * ATT_SCALE,
                            _dot_tn(p.astype(BF16), dob))

                def it(i, carry):
                    where, loaded, old = [], [], []
                    for way in range(ATT_WAYS):
                        n, rows, prow = _attn_block_index(i + way * ATT_STEPS, d)
                        where.append((rows, prow))
                        loaded.append((n, qn_s[rows, :].astype(BF16), _attn_two_blocks(kn_s, prow, rows),
                                       _attn_two_blocks(v_ref, prow, rows), do_ref[rows, :].astype(BF16),
                                       lse_ref[rows, :], dl_s[rows, :]))
                        old.append((dk_s[rows, :], dk_s[prow, :], dv_s[rows, :], dv_s[prow, :]))
                    results = [block(*vals) for vals in loaded]
                    db_ref[...] += functools.reduce(lambda a, b: a + b, [r[0] for r in results])
                    for (rows, prow), (dk_c, dk_p, dv_c, dv_p), (_, dq, dkk, dvv) in zip(where, old, results):
                        dq_s[rows, :] = dq
                        dk_s[prow, :] = dk_p + dkk[:ATT_BLK]
                        dv_s[prow, :] = dv_p + dvv[:ATT_BLK]
                        dk_s[rows, :] = dk_c + dkk[ATT_BLK:]
                        dv_s[rows, :] = dv_c + dvv[ATT_BLK:]
                    return carry

                lax.fori_loop(0, ATT_STEPS, it, 0)

        def norm_bwd(dn, xh, rs, gain):
            dgain = jnp.sum(dn * xh, axis=0, keepdims=True)
            dxh = dn * gain
            return rs * (dxh - xh * jnp.mean(dxh * xh, axis=-1, keepdims=True)), dgain

        dq, dqg = norm_bwd(dq_s[...], qh, rq, qg_ref[...])
        dk, dkg = norm_bwd(dk_s[...], kh, rk, kg_ref[...])
        dqkv_ref[0] = dq.astype(BF16)
        dqkv_ref[1] = dk.astype(BF16)
        dqkv_ref[2] = dv_s[...].astype(BF16)
        dqg_ref[...] = dqg
        dkg_ref[...] = dkg
        if n_s:
            pl.when((pl.program_id(0) == HEADS - 1) & (g == len(GROUPS) - 1))(comm_wait)

    def col(j):
        return pl.BlockSpec((None, SEQ, HEAD_DIM), lambda h, g: (g * 3 + j, 0, h))

    gspec = pl.BlockSpec((None, 1, HEAD_DIM), lambda h, g: (g, 0, 0))
    bspec = pl.BlockSpec((None, None, ATT_BLK, 2 * ATT_BLK), lambda h, g: (g, h, 0, 0))
    hcol = pl.BlockSpec((None, SEQ, HEAD_DIM), lambda h, g: (h // 2, 0, h % 2))
    dgspec = pl.BlockSpec((None, None, 1, HEAD_DIM), lambda h, g: (h, g, 0, 0))
    ng = len(GROUPS)
    sem = pltpu.SemaphoreType.DMA((max(n_s, 1), 3))
    return _pcall(
        body, name=name, grid=(HEADS, ng),
        in_specs=[col(0), col(1), col(2), gspec, gspec, bspec, hcol, hcol,
                  pl.BlockSpec((None, SEQ, 1), lambda h, g: (h, 0, 0))] + [_ANY] * n_s,
        out_specs=[pl.BlockSpec((None, 3, SEQ, HEAD_DIM), lambda h, g: (g, 0, 0, h)), dgspec, dgspec, bspec]
        + [_ANY] * n_s,
        out_shape=[_sds((ng, 3, SEQ, D_MODEL), BF16), _sds((HEADS, ng, 1, HEAD_DIM), F32),
                   _sds((HEADS, ng, 1, HEAD_DIM), F32), _sds((ng, HEADS, ATT_BLK, 2 * ATT_BLK), F32)]
        + _rs_chips_shapes(scatter),
        scratch_shapes=[pltpu.VMEM((SEQ, HEAD_DIM), F32)] * 5 + [pltpu.VMEM((SEQ, 1), F32)]
        + ([sem, sem] if n_s else []),
        compiler_params=_cparams(("arbitrary", "arbitrary")),
    )(qkv9, qkv9, qkv9, qgain, kgain, bias, do4, o4, lse, *scatter)


def _relbias_bwd(dbias, bucket_idx, name):
    ng = len(GROUPS)

    def body(db_ref, idx_ref, o_ref):
        lane = lax.broadcasted_iota(jnp.int32, (HEADS, 128), 1)
        acc = jnp.zeros((HEADS, 128), F32)
        for g in range(ng):
            dbg = db_ref[g]
            idx = idx_ref[g]
            for b in range(NUM_BUCKETS):
                sel = jnp.where((idx == b)[None], dbg, 0.0)
                part = jnp.sum(sel, axis=1)
                val = jnp.sum(part, axis=-1, keepdims=True)
                acc = jnp.where(lane == g * NUM_BUCKETS + b, val, acc)
        o_ref[...] = acc

    return _pcall(body, name=name, out_shape=_sds((HEADS, 128), F32), compiler_params=_cparams())(dbias, bucket_idx)


def _scan16(x, reverse=False):
    row = lax.broadcasted_iota(jnp.int32, x.shape, 0)
    for sh in (1, 2, 4, 8):
        if reverse:
            x = x + jnp.where(row < HG_SUB - sh, pltpu.roll(x, HG_SUB - sh, 0), 0.0)
        else:
            x = x + jnp.where(row >= sh, pltpu.roll(x, sh, 0), 0.0)
    return x


def _hgrn_gates(qr, fr, lbv):
    q = _silu(qr)
    sig = _sigmoid(fr)
    fg = lbv + (1.0 - lbv) * sig
    lf = jnp.log(fg)
    gcum = _scan16(lf)
    glast = jnp.sum(lf, axis=0, keepdims=True)
    return q, sig, fg, 1.0 - fg, gcum, glast


def _hgrn_intra(q, k, gcum, tri):
    e = jnp.exp(jnp.where(tri, gcum[:, None, :] - gcum[None, :, :], NEG))
    a = jnp.sum(q[:, None, :] * k[None, :, :] * e, axis=-1, keepdims=True)
    return e, a


def _hgrn_fwd(proj4, lb, gain, name):
    nsub = HG_TC // HG_SUB
    wide = HG_HP * HEAD_DIM

    def body(p_ref, lb_ref, gn_ref, o_ref, y_ref, st_ref, state_s):
        @pl.when(pl.program_id(1) == 0)
        def _():
            state_s[...] = jnp.zeros_like(state_s)

        gnv = gn_ref[...]
        shp = (HG_SUB, HG_SUB, HEAD_DIM)
        tri = lax.broadcasted_iota(jnp.int32, shp, 0) >= lax.broadcasted_iota(jnp.int32, shp, 1)

        def head(qr, fr, vv, gr, lbv, st):
            q, _, _, k, gcum, glast = _hgrn_gates(qr, fr, lbv)
            _, a = _hgrn_intra(q, k, gcum, tri)
            o = jnp.sum(a * vv[None, :, :], axis=1) + _dot_nt((q * jnp.exp(gcum)).astype(BF16), st.astype(BF16))
            kg = k * jnp.exp(glast - gcum)
            st_new = st * jnp.exp(glast) + _dot_tn(vv.astype(BF16), kg.astype(BF16))
            rs = lax.rsqrt(jnp.mean(o * o, axis=-1, keepdims=True) + RMS_EPS)
            return o, (o * rs * gnv * _silu(gr)).astype(BF16), st_new

        def it(i, carry):
            rows = pl.ds(pl.multiple_of(i * HG_SUB, HG_SUB), HG_SUB)
            loaded = []
            for hh in range(HG_HP):
                lanes = pl.ds(hh * HEAD_DIM, HEAD_DIM)
                loaded.append(([p_ref[j, rows, lanes] for j in range(4)], lb_ref[:, lanes], state_s[hh]))
            results = [head(blk[0], blk[1], blk[2], blk[3], lbv, st) for blk, lbv, st in loaded]
            for hh, ((_, _, st), (o, y, st_new)) in enumerate(zip(loaded, results)):
                lanes = pl.ds(hh * HEAD_DIM, HEAD_DIM)
                st_ref[hh, i] = st.astype(BF16)
                state_s[hh] = st_new
                o_ref[rows, lanes] = o
                y_ref[hh // 2, rows, pl.ds((hh % 2) * HEAD_DIM, HEAD_DIM)] = y
            return carry

        lax.fori_loop(0, nsub, it, 0)

    return _pcall(
        body, name=name, grid=(HEADS // HG_HP, SEQ // HG_TC),
        in_specs=[pl.BlockSpec((4, HG_TC, wide), lambda h, j: (0, j, h)),
                  pl.BlockSpec((1, wide), lambda h, j: (0, h)),
                  pl.BlockSpec((1, HEAD_DIM), lambda h, j: (0, 0))],
        out_specs=[pl.BlockSpec((HG_TC, wide), lambda h, j: (j, h)),
                   pl.BlockSpec((HG_HP // 2, HG_TC, 2 * HEAD_DIM), lambda h, j: (h, j, 0)),
                   pl.BlockSpec((HG_HP, nsub, HEAD_DIM, HEAD_DIM), lambda h, j: (h, j, 0, 0))],
        out_shape=[_sds((SEQ, D_MODEL), F32), _sds((N_CHIP, SEQ, 2 * HEAD_DIM), BF16),
                   _sds((HEADS, SEQ // HG_SUB, HEAD_DIM, HEAD_DIM), BF16)],
        scratch_shapes=[pltpu.VMEM((HG_HP, HEAD_DIM, HEAD_DIM), F32)],
        compiler_params=_cparams(("parallel", "arbitrary")),
    )(proj4, lb, gain)


def _hgrn_bwd(proj4, lb, gain, o_raw, dy4, states, name):
    nsub = HG_TC // HG_SUB
    nt = SEQ // HG_TC
    wide = HG_HP * HEAD_DIM

    def body(p_ref, lb_ref, gn_ref, o_ref, dy_ref, st_ref, dp_ref, dlb_ref, dgn_ref, dst_s):
        @pl.when(pl.program_id(1) == 0)
        def _():
            dst_s[...] = jnp.zeros_like(dst_s)
            dlb_ref[...] = jnp.zeros_like(dlb_ref)
            dgn_ref[...] = jnp.zeros_like(dgn_ref)

        gnv = gn_ref[...]
        shp = (HG_SUB, HG_SUB, HEAD_DIM)
        tri = lax.broadcasted_iota(jnp.int32, shp, 0) >= lax.broadcasted_iota(jnp.int32, shp, 1)

        def head(qr, fr, vv, gr, o, dy, lbv, st0, dst):
            q, sig, fg, k, gcum, glast = _hgrn_gates(qr, fr, lbv)
            rs = lax.rsqrt(jnp.mean(o * o, axis=-1, keepdims=True) + RMS_EPS)
            oh = o * rs
            don = dy * _silu(gr)
            dgn = jnp.sum(don * oh, axis=0, keepdims=True)
            dgr = dy * oh * gnv * _dsilu(gr)
            doh = don * gnv
            do = rs * (doh - oh * jnp.mean(doh * oh, axis=-1, keepdims=True))
            dst16 = dst.astype(BF16)
            do16 = do.astype(BF16)
            eg = jnp.exp(gcum)
            eb = jnp.exp(glast - gcum)
            e, a = _hgrn_intra(q, k, gcum, tri)
            da = jnp.sum(do[:, None, :] * vv[None, :, :], axis=-1, keepdims=True)
            dae = da * e
            dq = jnp.sum(dae * k[None, :, :], axis=1) + eg * _dot(do16, st0)
            dk_state = eb * _dot(vv.astype(BF16), dst16)
            dk = jnp.sum(dae * q[:, None, :], axis=0) + dk_state
            dv = jnp.sum(a * do[:, None, :], axis=0) + _dot_nt((k * eb).astype(BF16), dst16)
            eglast = jnp.exp(glast)
            dst_new = dst * eglast + _dot_tn(do16, (q * eg).astype(BF16))
            dglast = jnp.sum(k * dk_state, axis=0, keepdims=True) \
                + eglast * jnp.sum(dst * st0.astype(F32), axis=0, keepdims=True)
            dlf = _scan16(q * dq - k * dk, reverse=True) + dglast
            dfg = dlf / fg - dk
            dlb = jnp.sum(dfg * (1.0 - sig), axis=0, keepdims=True)
            dproj = ((dq * _dsilu(qr)).astype(BF16), (dfg * (1.0 - lbv) * sig * (1.0 - sig)).astype(BF16),
                     dv.astype(BF16), dgr.astype(BF16))
            return dproj, dst_new, dlb, dgn

        def it(ii, carry):
            i = nsub - 1 - ii
            rows = pl.ds(pl.multiple_of(i * HG_SUB, HG_SUB), HG_SUB)
            results = []
            for hh in range(HG_HP):
                lanes = pl.ds(hh * HEAD_DIM, HEAD_DIM)
                blk = [p_ref[j, rows, lanes] for j in range(4)]
                dy = dy_ref[hh // 2, rows, pl.ds((hh % 2) * HEAD_DIM, HEAD_DIM)]
                results.append(head(blk[0], blk[1], blk[2], blk[3], o_ref[rows, lanes], dy,
                                    lb_ref[:, lanes], st_ref[hh, i], dst_s[hh]))
            new_carry = []
            for hh, (dproj, dst_new, dlb, dgn) in enumerate(results):
                lanes = pl.ds(hh * HEAD_DIM, HEAD_DIM)
                dst_s[hh] = dst_new
                for j in range(4):
                    dp_ref[j, rows, lanes] = dproj[j]
                new_carry.append((carry[hh][0] + dlb, carry[hh][1] + dgn))
            return tuple(new_carry)

        zero = jnp.zeros((1, HEAD_DIM), F32)
        sums = lax.fori_loop(0, nsub, it, tuple((zero, zero) for _ in range(HG_HP)))
        for hh in range(HG_HP):
            dlb_ref[hh] += sums[hh][0]
            dgn_ref[hh] += sums[hh][1]

    vspec = pl.BlockSpec((HG_HP, 1, HEAD_DIM), lambda h, j: (h, 0, 0))
    return _pcall(
        body, name=name, grid=(HEADS // HG_HP, nt),
        in_specs=[pl.BlockSpec((4, HG_TC, wide), lambda h, j: (0, nt - 1 - j, h)),
                  pl.BlockSpec((1, wide), lambda h, j: (0, h)),
                  pl.BlockSpec((1, HEAD_DIM), lambda h, j: (0, 0)),
                  pl.BlockSpec((HG_TC, wide), lambda h, j: (nt - 1 - j, h)),
                  pl.BlockSpec((HG_HP // 2, HG_TC, 2 * HEAD_DIM), lambda h, j: (h, nt - 1 - j, 0)),
                  pl.BlockSpec((HG_HP, nsub, HEAD_DIM, HEAD_DIM), lambda h, j: (h, nt - 1 - j, 0, 0))],
        out_specs=[pl.BlockSpec((4, HG_TC, wide), lambda h, j: (0, nt - 1 - j, h)), vspec, vspec],
        out_shape=[_sds((4, SEQ, D_MODEL), BF16), _sds((HEADS, 1, HEAD_DIM), F32), _sds((HEADS, 1, HEAD_DIM), F32)],
        scratch_shapes=[pltpu.VMEM((HG_HP, HEAD_DIM, HEAD_DIM), F32)],
        compiler_params=_cparams(("parallel", "arbitrary")),
    )(proj4, lb, gain, o_raw, dy4, states)


def _t5_bucket(dist):
    n = np.asarray(dist, dtype=np.int64)
    max_exact = NUM_BUCKETS // 2
    large = max_exact + (np.log(np.maximum(n, 1) / max_exact) / np.log(MAX_DISTANCE / max_exact)
                         * (NUM_BUCKETS - max_exact)).astype(np.int64)
    large = np.minimum(large, NUM_BUCKETS - 1)
    return np.where(n < max_exact, n, large).astype(np.int32)


def _bias_tables():
    qi = np.arange(ATT_BLK)[:, None]
    ki = np.arange(2 * ATT_BLK)[None, :]
    j = ATT_BLK + qi - ki
    valid = (j >= 0) & (j <= ATT_BLK)
    return np.stack([np.where(valid, _t5_bucket(np.clip(j, 0, ATT_BLK) * d), -1) for _, d in GROUPS]).astype(np.int32)


def _attn_bias(rel_bias, name):
    idx = _bias_tables()
    ng = len(GROUPS)
    buckets = [sorted(set(idx[g][idx[g] >= 0].tolist())) for g in range(ng)]

    def body(rb_ref, idx_ref, o_ref):
        h = pl.program_id(0)
        for g in range(ng):
            ig = idx_ref[g]
            acc = jnp.full(ig.shape, NEG, F32)
            for b in buckets[g]:
                acc = jnp.where(ig == b, rb_ref[b, g * HEADS + h], acc)
            o_ref[g] = acc

    return _pcall(
        body, name=name, grid=(HEADS,),
        in_specs=[pl.BlockSpec(memory_space=pltpu.SMEM),
                  pl.BlockSpec((ng, ATT_BLK, 2 * ATT_BLK), lambda h: (0, 0, 0))],
        out_specs=pl.BlockSpec((ng, None, ATT_BLK, 2 * ATT_BLK), lambda h: (0, h, 0, 0)),
        out_shape=_sds((ng, HEADS, ATT_BLK, 2 * ATT_BLK), F32),
        compiler_params=_cparams(("parallel",)),
    )(rel_bias, jnp.asarray(idx))


ADA_SHARD = 6 * D_MODEL // N_CHIP
ADA_TN = 512


def _ada_fwd(c_all, ada_w, ada_b_cols, name):
    def body(c_ref, w_ref, b_ref, o_ref):
        ca = _silu(c_ref[...]).astype(BF16)
        o_ref[...] = _dot(ca, w_ref[...].astype(BF16)) + b_ref[...]

    return _pcall(
        body, name=name, grid=(DEPTH, ADA_SHARD // ADA_TN),
        in_specs=[pl.BlockSpec((N_DEV, D_MODEL), lambda l, j: (0, 0)),
                  pl.BlockSpec((None, D_MODEL, ADA_TN), lambda l, j: (l, 0, j)),
                  pl.BlockSpec((None, 1, ADA_TN), lambda l, j: (l, 0, j))],
        out_specs=pl.BlockSpec((None, N_DEV, ADA_TN), lambda l, j: (l, 0, j)),
        out_shape=_sds((DEPTH, N_DEV, ADA_SHARD), F32),
        compiler_params=_cparams(("parallel", "parallel")),
    )(c_all, ada_w, ada_b_cols)


def _ada_bwd(c_all, dmod_cols, name):
    def body(c_ref, d_ref, o_ref):
        ca = _silu(c_ref[...]).astype(BF16)
        o_ref[...] = _dot_tn(ca, d_ref[...].astype(BF16))

    return _pcall(
        body, name=name, grid=(DEPTH, ADA_SHARD // ADA_TN),
        in_specs=[pl.BlockSpec((N_DEV, D_MODEL), lambda l, j: (0, 0)),
                  pl.BlockSpec((None, N_DEV, ADA_TN), lambda l, j: (l, 0, j))],
        out_specs=pl.BlockSpec((None, D_MODEL, ADA_TN), lambda l, j: (l, 0, j)),
        out_shape=_sds((DEPTH, D_MODEL, ADA_SHARD), F32),
        compiler_params=_cparams(("parallel", "parallel")),
    )(c_all, dmod_cols)


def _lower_bounds(logits, name):
    def body(l_ref, o_ref):
        l0 = l_ref[0:1, :]
        l1 = l_ref[1:2, :]
        mx = jnp.maximum(l0, l1)
        e0 = jnp.exp(l0 - mx)
        e1 = jnp.exp(l1 - mx)
        p0 = e0 / (e0 + e1)
        p1 = e1 / (e0 + e1)
        o_ref[0:1, :] = p0 - p0
        o_ref[1:2, :] = (p0 + p1) - p0

    return _pcall(body, name=name, out_shape=_sds((DEPTH, D_MODEL), F32), compiler_params=_cparams())(logits)


_R_DMOD = 0
_R_NMIX = 96
_R_NFFN = 112
_R_QG = 128
_R_KG = 152
_R_GN = 176
_R_LB = 184
_R_RB = 192
SMALL_ROWS = 200


def _small_totals(gathered, logits8, name):
    ng = len(GROUPS)

    def body(g_ref, l_ref, main_ref, gains_ref, dlb_ref, rb_ref):
        tot = g_ref[0]
        for dev in range(1, N_DEV):
            tot = tot + g_ref[dev]
        main_ref[...] = tot[0:_R_QG]
        gains_ref[...] = jnp.zeros_like(gains_ref)
        for g in range(ng):
            gains_ref[g:g + 1, :] = jnp.sum(tot[_R_QG + 8 * g:_R_QG + 8 * g + 8], axis=0, keepdims=True)
            gains_ref[ng + g:ng + g + 1, :] = jnp.sum(tot[_R_KG + 8 * g:_R_KG + 8 * g + 8], axis=0, keepdims=True)
        gains_ref[2 * ng:2 * ng + 1, :] = jnp.sum(tot[_R_GN:_R_GN + 8], axis=0, keepdims=True)
        rb_ref[...] = tot[_R_RB:_R_RB + 8]
        dlb1 = tot[_R_LB:_R_LB + 8]
        l0 = l_ref[0]
        l1 = l_ref[1]
        mx = jnp.maximum(l0, l1)
        e0 = jnp.exp(l0 - mx)
        e1 = jnp.exp(l1 - mx)
        p0 = e0 / (e0 + e1)
        p1 = e1 / (e0 + e1)
        dlb_ref[0] = -p0 * p1 * dlb1
        dlb_ref[1] = p1 * (1.0 - p1) * dlb1

    return _pcall(
        body, name=name,
        out_shape=[_sds((_R_QG, 128), F32), _sds((8, 128), F32), _sds((DEPTH, 8, 128), F32), _sds((8, 128), F32)],
        compiler_params=_cparams(),
    )(gathered, logits8)


def _row_tile(rows):
    return 128 if rows % 128 == 0 else rows


def _adamw(w, grads, m, v, name):
    nl, r, cdim = w.shape
    tr = _row_tile(r)

    def body(*refs):
        g_refs = refs[:nl]
        w_ref, m_ref, v_ref, go_ref, d_ref, mo_ref, vo_ref = refs[nl:]

        def step(g):
            m2 = ADAM_B1 * m_ref[...] + (1.0 - ADAM_B1) * g
            v2 = ADAM_B2 * v_ref[...] + (1.0 - ADAM_B2) * (g * g)
            m_hat = m2 / (1.0 - ADAM_B1 ** ADAM_STEP)
            v_hat = v2 / (1.0 - ADAM_B2 ** ADAM_STEP)
            go_ref[...] = g
            d_ref[...] = -ADAM_LR * (m_hat / (jnp.sqrt(v_hat) + ADAM_EPS) + ADAM_WD * w_ref[...])
            mo_ref[...] = m2
            vo_ref[...] = v2

        if nl == 1:
            step(g_refs[0][...])
        else:
            for layer in range(nl):
                @pl.when(pl.program_id(0) == layer)
                def _(layer=layer):
                    step(g_refs[layer][...])

    big = pl.BlockSpec((None, tr, cdim), lambda l, i: (l, i, 0))
    g_specs = [pl.BlockSpec((tr, cdim), lambda l, i, layer=layer: (jnp.where(l == layer, i, 0), 0))
               for layer in range(nl)]
    shp = _sds((nl, r, cdim), F32)
    return _pcall(
        body, name=name, grid=(nl, r // tr),
        in_specs=g_specs + [big, big, big],
        out_specs=[big, big, big, big],
        out_shape=[shp, shp, shp, shp],
        compiler_params=_cparams(("parallel", "parallel")),
    )(*grads, w, m, v)


def _cast_bf16(place, w, name):
    nl, r, cdim = w.shape
    tr = _row_tile(r)

    def body(place_ref, w_ref, o_ref):
        o_ref[...] = w_ref[...].astype(BF16)

    return _pcall(
        body, name=name,
        grid_spec=pltpu.PrefetchScalarGridSpec(
            num_scalar_prefetch=1, grid=(nl, r // tr),
            in_specs=[pl.BlockSpec((None, tr, cdim), lambda l, i, place_ref: (l, i, 0))],
            out_specs=pl.BlockSpec((None, None, tr, cdim), lambda l, i, place_ref: (place_ref[1], l, i, 0))),
        out_shape=_sds((N_CHIP, nl, r, cdim), BF16),
        compiler_params=_cparams(("parallel", "parallel")),
    )(place, w)


def _rs_add_cast(place, grad, recv, name):
    _, k, n = grad.shape
    kh = k // 2
    tr = _row_tile(kh)
    nb = kh // tr

    def body(place_ref, g_ref, r_ref, o_ref):
        o_ref[...] = (g_ref[...] + r_ref[...]).astype(BF16)

    half = pl.BlockSpec((None, tr, n), lambda s, i, place_ref: (s, i, 0))
    return _pcall(
        body, name=name,
        grid_spec=pltpu.PrefetchScalarGridSpec(
            num_scalar_prefetch=1, grid=(N_CHIP, nb),
            in_specs=[pl.BlockSpec((None, tr, n), lambda s, i, place_ref: (s, place_ref[0] * nb + i, 0)), half],
            out_specs=half),
        out_shape=_sds((N_CHIP, kh, n), BF16),
        compiler_params=_cparams(("parallel", "parallel")),
    )(place, grad, recv)


def _rs_sum4(place, parts, got, name):
    _, kh, n = parts.shape
    tr = _row_tile(kh)
    nb = kh // tr

    def body(place_ref, p_ref, g_ref, o_ref):
        acc = p_ref[...].astype(F32)
        for j in range(N_CHIP - 1):
            acc = acc + g_ref[j].astype(F32)
        o_ref[...] = acc

    return _pcall(
        body, name=name,
        grid_spec=pltpu.PrefetchScalarGridSpec(
            num_scalar_prefetch=1, grid=(nb,),
            in_specs=[pl.BlockSpec((None, tr, n), lambda i, place_ref: (place_ref[1], i, 0)),
                      pl.BlockSpec((N_CHIP - 1, tr, n), lambda i, place_ref: (0, i, 0))],
            out_specs=pl.BlockSpec((tr, n), lambda i, place_ref: (place_ref[0] * nb + i, 0))),
        out_shape=_sds((2 * kh, n), F32),
        compiler_params=_cparams(("parallel",)),
    )(place, parts, got)


_ANY = pl.BlockSpec(memory_space=pl.ANY)


def _position():
    return lax.axis_index("x"), lax.axis_index("y"), lax.axis_index("c")


def _other_chips(x, y):
    return [(1 - x, y), (x, 1 - y), (1 - x, 1 - y)]


def _remote(src, dst, send_sem, recv_sem, to):
    return pltpu.make_async_remote_copy(src_ref=src, dst_ref=dst, send_sem=send_sem, recv_sem=recv_sem,
                                        device_id=to, device_id_type=MESH)


def _small_allgather(v, name):
    r = v.shape[0]

    def body(x_ref, out_ref, send_sems, recv_sems, local_sem):
        x, y, c = _position()
        me, sibling = (x, y, c), (x, y, 1 - c)
        chips = _other_chips(x, y)

        def slab(px, py, pc):
            return out_ref.at[4 * px + 2 * py + pc]

        def copy(k, block, to, src=None):
            return _remote(slab(*block) if src is None else src, slab(*block), send_sems.at[k], recv_sems.at[k], to)

        mine = pltpu.make_async_copy(x_ref, slab(*me), local_sem)
        mine.start()
        first = [copy(0, me, sibling, src=x_ref)]
        first += [copy(1 + j, me, (*chip, c), src=x_ref) for j, chip in enumerate(chips)]
        for cp in first:
            cp.start()
        passed = [copy(4 + j, (*chip, c), sibling) for j, chip in enumerate(chips)]
        for j, chip in enumerate(chips):
            copy(1 + j, (*chip, c), me).wait_recv()
            passed[j].start()
        copy(0, sibling, me).wait_recv()
        for j, chip in enumerate(chips):
            copy(4 + j, (*chip, 1 - c), me).wait_recv()
        for cp in first + passed:
            cp.wait_send()
        mine.wait()

    return _pcall(
        body, name=name,
        out_shape=_sds((N_DEV, r, 128), F32),
        in_specs=[pl.BlockSpec(memory_space=pltpu.VMEM)],
        out_specs=pl.BlockSpec(memory_space=pltpu.VMEM),
        scratch_shapes=[pltpu.SemaphoreType.DMA((7,)), pltpu.SemaphoreType.DMA((7,)), pltpu.SemaphoreType.DMA],
        compiler_params=_cparams(),
    )(v)


def _half_rows(core, kh):
    return pl.ds(pl.multiple_of(core * kh, 8), kh)


def _slab_half(ref, chip, core):
    return ref.at[chip, :, _half_rows(core, ref.shape[2] // 2), :]


def _gather_ici(out, send_sems, recv_sems):
    def copies():
        x, y, c = _position()
        for a in range(len(out)):
            for j, (px, py) in enumerate(_other_chips(x, y)):
                mine = _slab_half(out[a], 2 * x + y, c)
                landed = _slab_half(out[a], 2 * px + py, c)
                yield (_remote(mine, mine, send_sems.at[a, j], recv_sems.at[a, j], (px, py, c)),
                       _remote(landed, landed, send_sems.at[a, j], recv_sems.at[a, j], (px, py, c)))

    def start():
        for send, _ in copies():
            send.start()

    def wait():
        for send, recv in copies():
            recv.wait_recv()
            send.wait_send()

    return start, wait


def _gather_d2d(out, send_sems, recv_sems):
    def copies():
        x, y, c = _position()
        for a in range(len(out)):
            for j, (px, py) in enumerate(_other_chips(x, y)):
                landed = _slab_half(out[a], 2 * px + py, c)
                other = _slab_half(out[a], 2 * px + py, 1 - c)
                yield (_remote(landed, landed, send_sems.at[a, j], recv_sems.at[a, j], (x, y, 1 - c)),
                       _remote(other, other, send_sems.at[a, j], recv_sems.at[a, j], (x, y, 1 - c)))

    def start():
        for send, _ in copies():
            send.start()

    def wait():
        for send, recv in copies():
            recv.wait_recv()
            send.wait_send()

    return start, wait


def _gather_weights(slabs, name, ici=True):
    n = len(slabs)

    def body(*refs):
        out = refs[n:2 * n]
        sems = refs[2 * n:]
        if ici:
            start, wait = _gather_ici(out, sems[2], sems[3])
            start()
            wait()
        start, wait = _gather_d2d(out, sems[0], sems[1])
        start()
        wait()

    sem = pltpu.SemaphoreType.DMA((n, 3))
    return _pcall(
        body, name=name,
        out_shape=[_sds(s.shape, BF16) for s in slabs],
        in_specs=[_ANY] * n, out_specs=[_ANY] * n,
        input_output_aliases={a: a for a in range(n)},
        scratch_shapes=[sem, sem] + ([sem, sem] if ici else []),
        compiler_params=_cparams(),
    )(*slabs)


def _rs_exchange_halves(grads, name):
    n = len(grads)

    def body(*refs):
        g = refs[:n]
        out = refs[n:2 * n]
        send_sems, recv_sems = refs[2 * n:]
        x, y, c = _position()
        copies = []
        for a in range(n):
            kh = g[a].shape[1] // 2
            cp = _remote(g[a].at[:, _half_rows(1 - c, kh), :], out[a], send_sems.at[a], recv_sems.at[a], (x, y, 1 - c))
            cp.start()
            copies.append(cp)
        for cp in copies:
            cp.wait()

    return _pcall(
        body, name=name,
        out_shape=[_sds((N_CHIP, g.shape[1] // 2, g.shape[2]), F32) for g in grads],
        in_specs=[_ANY] * n, out_specs=[_ANY] * n,
        scratch_shapes=[pltpu.SemaphoreType.DMA((n,)), pltpu.SemaphoreType.DMA((n,))],
        compiler_params=_cparams(),
    )(*grads)


def _rs_chips(parts, out, send_sems, recv_sems):
    def copies():
        x, y, c = _position()
        for a in range(len(parts)):
            for j, (px, py) in enumerate(_other_chips(x, y)):
                got = out[a].at[j]
                yield (_remote(parts[a].at[2 * px + py], got, send_sems.at[a, j], recv_sems.at[a, j], (px, py, c)),
                       _remote(got, got, send_sems.at[a, j], recv_sems.at[a, j], (px, py, c)))

    def start():
        for send, _ in copies():
            send.start()

    def wait():
        for send, recv in copies():
            recv.wait_recv()
            send.wait_send()

    return start, wait


def _rs_chips_shapes(parts):
    return [_sds((N_CHIP - 1,) + p.shape[1:], BF16) for p in parts]


def _rs_join_halves(fulls, name):
    n = len(fulls)

    def body(*refs):
        out = refs[n:2 * n]
        send_sems, recv_sems = refs[2 * n:]
        x, y, c = _position()
        copies = []
        for a in range(n):
            kh = out[a].shape[0] // 2
            mine = out[a].at[_half_rows(c, kh), :]
            cp = _remote(mine, mine, send_sems.at[a], recv_sems.at[a], (x, y, 1 - c))
            cp.start()
            copies.append(cp)
        for a in range(n):
            kh = out[a].shape[0] // 2
            theirs = out[a].at[_half_rows(1 - c, kh), :]
            _remote(theirs, theirs, send_sems.at[a], recv_sems.at[a], (x, y, 1 - c)).wait_recv()
        for cp in copies:
            cp.wait_send()

    return _pcall(
        body, name=name,
        out_shape=[_sds(f.shape, F32) for f in fulls],
        in_specs=[_ANY] * n, out_specs=[_ANY] * n,
        input_output_aliases={a: a for a in range(n)},
        scratch_shapes=[pltpu.SemaphoreType.DMA((n,)), pltpu.SemaphoreType.DMA((n,))],
        compiler_params=_cparams(),
    )(*fulls)


_SMALL_ORDER = ("rel_bias", "ada_b", "norm_mix", "norm_ffn", "attn_q_gain", "attn_k_gain", "hgrn_gnorm",
                "hgrn_lower_bounds")
_WEIGHT_ORDER = ("rel_bias", "ada_w", "ada_b", "norm_mix", "norm_ffn", "attn_w_qkv", "attn_w_out", "attn_q_gain",
                 "attn_k_gain", "hgrn_w_in", "hgrn_w_out", "hgrn_gnorm", "hgrn_lower_bounds", "ffn_w1", "ffn_w3",
                 "ffn_w2")


def _qkv_group_map(t):
    return t // 4, t % 4


def _qkv_chip_map(t):
    return t // 9, t % 9


def _hin_map(t):
    return t // 2, t % 2


def _block_map(t):
    return t, 0


def _pack_rows(parts):
    return jnp.concatenate([p.reshape(-1, 128) for p in parts], axis=0)


def kernel(x, c, rel_bias, ada_w, ada_b, norm_mix, norm_ffn, attn_w_qkv, attn_w_out, attn_q_gain, attn_k_gain, hgrn_w_in, hgrn_w_out, hgrn_gnorm, hgrn_lower_bounds, ffn_w1, ffn_w3, ffn_w2, loss_target, m_rel_bias, m_ada_w, m_ada_b, m_norm_mix, m_norm_ffn, m_attn_w_qkv, m_attn_w_out, m_attn_q_gain, m_attn_k_gain, m_hgrn_w_in, m_hgrn_w_out, m_hgrn_gnorm, m_hgrn_lower_bounds, m_ffn_w1, m_ffn_w3, m_ffn_w2, v_rel_bias, v_ada_w, v_ada_b, v_norm_mix, v_norm_ffn, v_attn_w_qkv, v_attn_w_out, v_attn_q_gain, v_attn_k_gain, v_hgrn_w_in, v_hgrn_w_out, v_hgrn_gnorm, v_hgrn_lower_bounds, v_ffn_w1, v_ffn_w3, v_ffn_w2):
    weights = dict(rel_bias=rel_bias, ada_w=ada_w, ada_b=ada_b, norm_mix=norm_mix, norm_ffn=norm_ffn,
                   attn_w_qkv=attn_w_qkv, attn_w_out=attn_w_out, attn_q_gain=attn_q_gain, attn_k_gain=attn_k_gain,
                   hgrn_w_in=hgrn_w_in, hgrn_w_out=hgrn_w_out, hgrn_gnorm=hgrn_gnorm,
                   hgrn_lower_bounds=hgrn_lower_bounds, ffn_w1=ffn_w1, ffn_w3=ffn_w3, ffn_w2=ffn_w2)
    mom1 = dict(rel_bias=m_rel_bias, ada_w=m_ada_w, ada_b=m_ada_b, norm_mix=m_norm_mix, norm_ffn=m_norm_ffn,
                attn_w_qkv=m_attn_w_qkv, attn_w_out=m_attn_w_out, attn_q_gain=m_attn_q_gain,
                attn_k_gain=m_attn_k_gain, hgrn_w_in=m_hgrn_w_in, hgrn_w_out=m_hgrn_w_out, hgrn_gnorm=m_hgrn_gnorm,
                hgrn_lower_bounds=m_hgrn_lower_bounds, ffn_w1=m_ffn_w1, ffn_w3=m_ffn_w3, ffn_w2=m_ffn_w2)
    mom2 = dict(rel_bias=v_rel_bias, ada_w=v_ada_w, ada_b=v_ada_b, norm_mix=v_norm_mix, norm_ffn=v_norm_ffn,
                attn_w_qkv=v_attn_w_qkv, attn_w_out=v_attn_w_out, attn_q_gain=v_attn_q_gain,
                attn_k_gain=v_attn_k_gain, hgrn_w_in=v_hgrn_w_in, hgrn_w_out=v_hgrn_w_out, hgrn_gnorm=v_hgrn_gnorm,
                hgrn_lower_bounds=v_hgrn_lower_bounds, ffn_w1=v_ffn_w1, ffn_w3=v_ffn_w3, ffn_w2=v_ffn_w2)

    xi, yi, ci = _position()
    chip = 2 * xi + yi
    dev = 4 * xi + 2 * yi + ci
    place = jnp.stack([ci, chip]).astype(jnp.int32)
    d = D_MODEL

    big_names = ("attn_w_qkv", "attn_w_out", "hgrn_w_in", "hgrn_w_out", "ffn_w1", "ffn_w3", "ffn_w2")
    early_names, late_names = big_names[:2], big_names[2:]
    slabs16 = {k: _cast_bf16(place, weights[k], "cast_" + k) for k in big_names}
    wg = dict(zip(early_names, _gather_weights([slabs16[k] for k in early_names], "gather_early")))

    c_all = _small_allgather(c.reshape(8, 128), "gather_c").reshape(N_DEV, d)
    ada_b_cols = lax.dynamic_slice(ada_b, (0, chip * ADA_SHARD), (DEPTH, ADA_SHARD)).reshape(DEPTH, 1, ADA_SHARD)
    mod_shard = _ada_fwd(c_all, ada_w, ada_b_cols, "ada_fwd")
    mod_all = _small_allgather(mod_shard.reshape(-1, 128), "gather_mod").reshape(N_DEV, DEPTH, N_DEV, ADA_SHARD)
    mod_mine = lax.dynamic_index_in_dim(mod_all[0::2], dev, axis=2, keepdims=False)
    mod = jnp.transpose(mod_mine, (1, 0, 2)).reshape(DEPTH, 6 * d)

    def mods(layer):
        return [mod[layer:layer + 1, j * d:(j + 1) * d] for j in range(6)]

    x0 = x.reshape(SEQ, d)
    target = loss_target.reshape(SEQ, d)
    qg = attn_q_gain.reshape(len(GROUPS), 1, HEAD_DIM)
    kg = attn_k_gain.reshape(len(GROUPS), 1, HEAD_DIM)
    bias = _attn_bias(rel_bias, "attn_bias")
    lb1 = _lower_bounds(hgrn_lower_bounds, "lower_bounds")[1:2]

    def ffn_fwd(layer, x_in, sc2, sh2, g2):
        hf = _norm_mod(x_in, norm_ffn[layer:layer + 1], sc2, sh2, f"l{layer}_norm_ffn")
        a1, a3, u = _ffn_up(hf, wg["ffn_w1"], wg["ffn_w3"], layer, f"l{layer}_ffn_up")
        z, x_out = _mm_rows(u, wg["ffn_w2"], layer, x_in, g2, f"l{layer}_ffn_down")
        return x_out, (hf, a1, a3, u, z)

    def ffn_bwd(layer, dx_out, x_in, sc2, sh2, g2, saved):
        hf, a1, a3, u, z = saved
        dz, dg2 = _gate_bwd(dx_out, z, g2, f"l{layer}_ffn_gate_bwd")
        da1, da3 = _ffn_down_bwd(dz, wg["ffn_w2"], layer, a1, a3, f"l{layer}_ffn_down_bwd")
        dw2 = _mm_rows_bwd_w(u, dz, f"l{layer}_dw2")
        dh = _mm_cols_bwd_a([(da1, wg["ffn_w1"], layer), (da3, wg["ffn_w3"], layer)], tn=FFN_SHARD,
                            act_map=_block_map, w_map=_block_map, n_tiles=N_CHIP, name=f"l{layer}_ffn_up_bwd")
        dw1 = _mm_cols_bwd_w(hf, da1, ns=FFN_SHARD, tn=FFN_SHARD, act_map=_block_map, w_map=_block_map,
                             n_tiles=N_CHIP, name=f"l{layer}_dw1")
        dw3 = _mm_cols_bwd_w(hf, da3, ns=FFN_SHARD, tn=FFN_SHARD, act_map=_block_map, w_map=_block_map,
                             n_tiles=N_CHIP, name=f"l{layer}_dw3")
        dx_in, dsc2, dsh2, dnf = _norm_mod_bwd(x_in, norm_ffn[layer:layer + 1], sc2, sh2, dh, dx_out,
                                               f"l{layer}_norm_ffn_bwd")
        return dx_in, (dw1, dw3, dw2), (dsh2, dsc2, dg2), dnf

    sh1_0, sc1_0, g1_0, sh2_0, sc2_0, g2_0 = mods(0)
    h0 = _norm_mod(x0, norm_mix[0:1], sc1_0, sh1_0, "l0_norm_mix")
    w_qkv9 = _retile_cols(wg["attn_w_qkv"].reshape(N_CHIP, d, 2304), n_out=9, width_out=d, tn=256,
                          src_map=_qkv_chip_map, dst_map=_qkv_group_map, n_tiles=36,
                          name="regroup_w_qkv").reshape(9, 1, d, d)
    qkv9 = _mm_cols(h0, w_qkv9, 0, n_blocks=9, width=d, tn=d, act_map=_block_map, w_map=_block_map,
                    out_dtype=F32, name="l0_qkv")
    o4, lse, *late = _attn_fwd(qkv9, qg, kg, bias, "l0_attn", gather=[slabs16[k] for k in late_names])
    wg.update(zip(late_names, _gather_weights(late, "gather_late_siblings", ici=False)))
    y0, x1 = _mm_rows(o4, wg["attn_w_out"], 0, x0, g1_0, "l0_attn_out")
    x2, ffn0 = ffn_fwd(0, x1, sc2_0, sh2_0, g2_0)

    sh1_1, sc1_1, g1_1, sh2_1, sc2_1, g2_1 = mods(1)
    h1 = _norm_mod(x2, norm_mix[1:2], sc1_1, sh1_1, "l1_norm_mix")
    proj4 = _mm_cols(h1, wg["hgrn_w_in"], 0, n_blocks=4, width=d, tn=512, act_map=_hin_map, w_map=_hin_map,
                     out_dtype=F32, name="l1_hgrn_in")
    o_raw, yg4, states = _hgrn_fwd(proj4, lb1, hgrn_gnorm, "l1_hgrn")
    y1, x3 = _mm_rows(yg4, wg["hgrn_w_out"], 0, x2, g1_1, "l1_hgrn_out")
    x4, ffn1 = ffn_fwd(1, x3, sc2_1, sh2_1, g2_1)

    dx4, loss_part = _loss_head(x4, target, "loss_head")
    loss = lax.psum(loss_part[0, 0], ("x", "y", "c"))

    dx3, (dw1_1, dw3_1, dw2_1), dmod2_1, dnf_1 = ffn_bwd(1, dx4, x3, sc2_1, sh2_1, g2_1, ffn1)
    dzm1, dg1_1 = _gate_bwd(dx3, y1, g1_1, "l1_mix_gate_bwd")
    dyg4 = _mm_rows_bwd_a(dzm1, wg["hgrn_w_out"], 0, "l1_hgrn_out_bwd")
    dw_hout = _mm_rows_bwd_w(yg4, dzm1, "l1_dw_hgrn_out")
    dproj4, dlb_h, dgn_h = _hgrn_bwd(proj4, lb1, hgrn_gnorm, o_raw, dyg4, states, "l1_hgrn_bwd")
    dh1 = _mm_cols_bwd_a([(dproj4, wg["hgrn_w_in"], 0)], tn=512, act_map=_hin_map, w_map=_hin_map, n_tiles=8,
                         name="l1_hgrn_in_bwd")
    dw_hin = _mm_cols_bwd_w(h1, dproj4, ns=d, tn=512, act_map=_hin_map, w_map=_hin_map, n_tiles=8,
                            name="l1_dw_hgrn_in")
    dx2, dsc1_1, dsh1_1, dnm_1 = _norm_mod_bwd(x2, norm_mix[1:2], sc1_1, sh1_1, dh1, dx3, "l1_norm_mix_bwd")

    dx1, (dw1_0, dw3_0, dw2_0), dmod2_0, dnf_0 = ffn_bwd(0, dx2, x1, sc2_0, sh2_0, g2_0, ffn0)
    dzm0, dg1_0 = _gate_bwd(dx1, y0, g1_0, "l0_mix_gate_bwd")
    do4 = _mm_rows_bwd_a(dzm0, wg["attn_w_out"], 0, "l0_attn_out_bwd")
    dw_aout = _mm_rows_bwd_w(o4, dzm0, "l0_dw_attn_out")

    def rs_prepare(tags, grads_in, suffix):
        recv = _rs_exchange_halves(grads_in, "rs_exchange_halves_" + suffix)
        return [_rs_add_cast(place, g, r, f"rs_add_{k}_{layer}") for (k, layer), g, r in zip(tags, grads_in, recv)]

    tags_a = [("attn_w_out", 0), ("hgrn_w_in", 0), ("hgrn_w_out", 0), ("ffn_w1", 0), ("ffn_w1", 1), ("ffn_w3", 0),
              ("ffn_w3", 1), ("ffn_w2", 0), ("ffn_w2", 1)]
    parts_a = rs_prepare(tags_a, [dw_aout, dw_hin, dw_hout, dw1_0, dw1_1, dw3_0, dw3_1, dw2_0, dw2_1], "a")
    dqkv, dqg_h, dkg_h, dbias, *got_a = _attn_bwd(qkv9, qg, kg, bias, do4, o4, lse, "l0_attn_bwd", scatter=parts_a)
    dqkv9 = dqkv.reshape(9, SEQ, d)
    dw_qkv9 = _mm_cols_bwd_w(h0, dqkv9, ns=d, tn=d, act_map=_block_map, w_map=_block_map, n_tiles=9,
                             name="l0_dw_qkv", tm=512, n_out=9)
    dw_qkv = _retile_cols(dw_qkv9, n_out=N_CHIP, width_out=2304, tn=256, src_map=_qkv_group_map,
                          dst_map=_qkv_chip_map, n_tiles=36, name="regroup_dw_qkv")
    tags_b = [("attn_w_qkv", 0)]
    parts_b = rs_prepare(tags_b, [dw_qkv], "b")
    dh0, *got_b = _mm_cols_bwd_a([(dqkv9, w_qkv9, 0)], tn=d, act_map=_block_map, w_map=_block_map, n_tiles=9,
                                 name="l0_qkv_bwd", scatter=parts_b)
    dx0, dsc1_0, dsh1_0, dnm_0 = _norm_mod_bwd(x0, norm_mix[0:1], sc1_0, sh1_0, dh0, dx1, "l0_norm_mix_bwd")
    drb8 = _relbias_bwd(dbias, jnp.asarray(_bias_tables()), "rel_bias_bwd")

    small = _pack_rows([
        dsh1_0, dsc1_0, dg1_0, *dmod2_0, dsh1_1, dsc1_1, dg1_1, *dmod2_1,
        dnm_0, dnm_1, dnf_0, dnf_1,
        jnp.transpose(dqg_h, (1, 0, 2, 3)), jnp.transpose(dkg_h, (1, 0, 2, 3)), dgn_h, dlb_h, drb8])
    small_all = _small_allgather(small, "gather_small")
    main, gains, dlbnd, rbt = _small_totals(small_all, hgrn_lower_bounds.reshape(DEPTH, 8, 128), "small_totals")
    ng = len(GROUPS)
    grads = {
        "ada_b": main[_R_DMOD:_R_NMIX].reshape(DEPTH, 6 * d),
        "norm_mix": main[_R_NMIX:_R_NFFN].reshape(DEPTH, d),
        "norm_ffn": main[_R_NFFN:_R_QG].reshape(DEPTH, d),
        "attn_q_gain": gains[0:ng].reshape(1, ng, HEAD_DIM),
        "attn_k_gain": gains[ng:2 * ng].reshape(1, ng, HEAD_DIM),
        "hgrn_gnorm": gains[2 * ng:2 * ng + 1],
        "hgrn_lower_bounds": dlbnd.reshape(DEPTH, d),
        "rel_bias": jnp.transpose(rbt[:, :ng * NUM_BUCKETS].reshape(HEADS, ng, NUM_BUCKETS), (2, 1, 0))
                       .reshape(NUM_BUCKETS, ng * HEADS),
    }
    dmod_all = small_all[:, _R_DMOD:_R_NMIX].reshape(N_DEV, DEPTH, 6 * d)
    dmod_cols = jnp.transpose(lax.dynamic_slice(dmod_all, (0, 0, chip * ADA_SHARD), (N_DEV, DEPTH, ADA_SHARD)),
                              (1, 0, 2))
    grad_ada_w = _ada_bwd(c_all, dmod_cols, "ada_bwd")

    tags = tags_a + tags_b
    halves = [_rs_sum4(place, p, r, f"rs_sum_{k}_{layer}")
              for (k, layer), p, r in zip(tags, parts_a + parts_b, list(got_a) + list(got_b))]
    full = dict(zip(tags, _rs_join_halves(halves, "rs_join_halves")))

    out_g, out_d, out_m, out_v = {}, {}, {}, {}
    for k in big_names:
        gs = [full[(k, layer)] for layer in range(weights[k].shape[0])]
        out_g[k], out_d[k], out_m[k], out_v[k] = _adamw(weights[k], gs, mom1[k], mom2[k], "adamw_" + k)
    shp = (1, DEPTH * d, ADA_SHARD)
    res = _adamw(ada_w.reshape(shp), [grad_ada_w.reshape(shp[1:])], m_ada_w.reshape(shp), v_ada_w.reshape(shp),
                 "adamw_ada_w")
    out_g["ada_w"], out_d["ada_w"], out_m["ada_w"], out_v["ada_w"] = [r.reshape(ada_w.shape) for r in res]
    packed = [_pack_rows([src[k] for k in _SMALL_ORDER])[None] for src in (weights, grads, mom1, mom2)]
    res = _adamw(packed[0], [packed[1][0]], packed[2], packed[3], "adamw_small")
    offset = 0
    for k in _SMALL_ORDER:
        size = weights[k].size
        for dst, r in zip((out_g, out_d, out_m, out_v), res):
            dst[k] = r.reshape(-1)[offset:offset + size].reshape(weights[k].shape)
        offset += size

    return (loss, dx0.reshape(x.shape), *[out_g[k] for k in _WEIGHT_ORDER], *[out_d[k] for k in _WEIGHT_ORDER],
            *[out_m[k] for k in _WEIGHT_ORDER], *[out_v[k] for k in _WEIGHT_ORDER])
```

```python
import functools

import numpy as np
import jax
import jax.numpy as jnp
from jax import lax
from jax.experimental import pallas as pl
from jax.experimental.pallas import tpu as pltpu

F32 = jnp.float32
BF16 = jnp.bfloat16

D_MODEL = 1024
SEQ = 4096
N_DEV = 8
N_CHIP = 4
DEPTH = 2
HEADS = 8
HEAD_DIM = 128
GROUPS = ((128, 1), (512, 4), (2048, 16))
ATT_BLK = 128
ATT_WAYS = 4
ATT_STEPS = SEQ // ATT_BLK // ATT_WAYS
NUM_BUCKETS = 32
MAX_DISTANCE = 2048
FFN_HIDDEN = 2816
FFN_SHARD = FFN_HIDDEN // N_CHIP
HG_SUB = 16
HG_TC = 512
HG_HP = 4
RMS_EPS = 1e-6
NEG = -1e30
ATT_SCALE = HEAD_DIM ** -0.5
ADAM_LR, ADAM_B1, ADAM_B2, ADAM_EPS, ADAM_WD, ADAM_STEP = 0.001, 0.9, 0.999, 1e-08, 0.01, 10
VMEM_LIMIT = 56 * 1024 * 1024
MESH = pl.DeviceIdType.MESH


def _pcall(body, **kw):
    return pl.pallas_call(body, **kw)


def _cparams(sem=None):
    if sem is None:
        return pltpu.CompilerParams(vmem_limit_bytes=VMEM_LIMIT)
    return pltpu.CompilerParams(dimension_semantics=sem, vmem_limit_bytes=VMEM_LIMIT)


def _sds(shape, dtype):
    return jax.ShapeDtypeStruct(shape, dtype)


def _dot(a, b):
    return jnp.dot(a, b, preferred_element_type=F32)


def _dot_nt(a, b):
    return lax.dot_general(a, b, (((1,), (1,)), ((), ())), preferred_element_type=F32)


def _dot_tn(a, b):
    return lax.dot_general(a, b, (((0,), (0,)), ((), ())), preferred_element_type=F32)


def _sigmoid(x):
    return 1.0 / (1.0 + jnp.exp(-x))


def _silu(x):
    return x * _sigmoid(x)


def _dsilu(x):
    s = _sigmoid(x)
    return s * (1.0 + x * (1.0 - s))


def _norm_mod(x, gain, sc, sh, name):
    tm = 512

    def body(x_ref, g_ref, sc_ref, sh_ref, h_ref):
        xv = x_ref[...]
        rs = lax.rsqrt(jnp.mean(xv * xv, axis=-1, keepdims=True) + RMS_EPS)
        h_ref[...] = ((xv * rs * g_ref[...]) * (1.0 + sc_ref[...]) + sh_ref[...]).astype(BF16)

    vec = pl.BlockSpec((1, D_MODEL), lambda i: (0, 0))
    return _pcall(
        body, name=name, grid=(SEQ // tm,),
        in_specs=[pl.BlockSpec((tm, D_MODEL), lambda i: (i, 0)), vec, vec, vec],
        out_specs=pl.BlockSpec((tm, D_MODEL), lambda i: (i, 0)),
        out_shape=_sds((SEQ, D_MODEL), BF16),
        compiler_params=_cparams(("parallel",)),
    )(x, gain, sc, sh)


def _norm_mod_bwd(x, gain, sc, sh, dh, dres, name):
    tm = 512

    def body(x_ref, g_ref, sc_ref, sh_ref, dh_ref, dres_ref, dx_ref, dsc_ref, dsh_ref, dg_ref):
        @pl.when(pl.program_id(0) == 0)
        def _():
            dsc_ref[...] = jnp.zeros_like(dsc_ref)
            dsh_ref[...] = jnp.zeros_like(dsh_ref)
            dg_ref[...] = jnp.zeros_like(dg_ref)

        xv = x_ref[...]
        dhv = dh_ref[...]
        rs = lax.rsqrt(jnp.mean(xv * xv, axis=-1, keepdims=True) + RMS_EPS)
        xh = xv * rs
        dsc_ref[...] += jnp.sum(dhv * (xh * g_ref[...]), axis=0, keepdims=True)
        dsh_ref[...] += jnp.sum(dhv, axis=0, keepdims=True)
        dhn = dhv * (1.0 + sc_ref[...])
        dg_ref[...] += jnp.sum(dhn * xh, axis=0, keepdims=True)
        dxh = dhn * g_ref[...]
        dx_ref[...] = dres_ref[...] + rs * (dxh - xh * jnp.mean(dxh * xh, axis=-1, keepdims=True))

    vec = pl.BlockSpec((1, D_MODEL), lambda i: (0, 0))
    big = pl.BlockSpec((tm, D_MODEL), lambda i: (i, 0))
    return _pcall(
        body, name=name, grid=(SEQ // tm,),
        in_specs=[big, vec, vec, vec, big, big],
        out_specs=[big, vec, vec, vec],
        out_shape=[_sds((SEQ, D_MODEL), F32)] + [_sds((1, D_MODEL), F32)] * 3,
        compiler_params=_cparams(("arbitrary",)),
    )(x, gain, sc, sh, dh, dres)


def _mm_cols(a, wg, layer, *, n_blocks, width, tn, act_map, w_map, out_dtype, name, tm=1024):
    k = a.shape[1]
    n_tiles = n_blocks * width // tn

    def body(a_ref, w_ref, o_ref):
        o_ref[...] = _dot(a_ref[...], w_ref[...]).astype(o_ref.dtype)

    return _pcall(
        body, name=name, grid=(SEQ // tm, n_tiles),
        in_specs=[pl.BlockSpec((tm, k), lambda i, t: (i, 0)),
                  pl.BlockSpec((None, None, k, tn), lambda i, t: (w_map(t)[0], layer, 0, w_map(t)[1]))],
        out_specs=pl.BlockSpec((None, tm, tn), lambda i, t: (act_map(t)[0], i, act_map(t)[1])),
        out_shape=_sds((n_blocks, SEQ, width), out_dtype),
        compiler_params=_cparams(("parallel", "arbitrary")),
    )(a, wg)


def _mm_cols_bwd_a(pairs, *, tn, act_map, w_map, n_tiles, name, tm=1024, scatter=()):
    k = pairs[0][1].shape[2]
    n_p = len(pairs)
    n_s = len(scatter)
    n_rows = SEQ // tm

    def body(*refs):
        o_ref = refs[2 * n_p + n_s]
        if n_s:
            comm_start, comm_wait = _rs_chips(refs[2 * n_p:2 * n_p + n_s], refs[2 * n_p + n_s + 1:2 * n_p + 2 * n_s + 1],
                                              *refs[2 * n_p + 2 * n_s + 1:])
            pl.when((pl.program_id(0) == 0) & (pl.program_id(1) == 0))(comm_start)

        @pl.when(pl.program_id(1) == 0)
        def _():
            o_ref[...] = jnp.zeros_like(o_ref)

        acc = _dot_nt(refs[0][...], refs[1][...])
        for p in range(1, n_p):
            acc += _dot_nt(refs[2 * p][...], refs[2 * p + 1][...])
        o_ref[...] += acc
        if n_s:
            pl.when((pl.program_id(0) == n_rows - 1) & (pl.program_id(1) == n_tiles - 1))(comm_wait)

    in_specs, args = [], []
    for dout, wg, layer in pairs:
        in_specs.append(pl.BlockSpec((None, tm, tn), lambda i, t: (act_map(t)[0], i, act_map(t)[1])))
        in_specs.append(pl.BlockSpec((None, None, k, tn),
                                     lambda i, t, layer=layer: (w_map(t)[0], layer, 0, w_map(t)[1])))
        args += [dout, wg]
    sem = pltpu.SemaphoreType.DMA((max(n_s, 1), 3))
    res = _pcall(
        body, name=name, grid=(n_rows, n_tiles),
        in_specs=in_specs + [_ANY] * n_s,
        out_specs=[pl.BlockSpec((tm, k), lambda i, t: (i, 0))] + [_ANY] * n_s,
        out_shape=[_sds((SEQ, k), F32)] + _rs_chips_shapes(scatter),
        scratch_shapes=[sem, sem] if n_s else [],
        compiler_params=_cparams(("arbitrary", "arbitrary") if n_s else ("parallel", "arbitrary")),
    )(*args, *scatter)
    return res if n_s else res[0]


def _mm_cols_bwd_w(a, dout, *, ns, tn, act_map, w_map, n_tiles, name, tm=1024, n_out=N_CHIP):
    k = a.shape[1]

    def body(a_ref, d_ref, o_ref):
        @pl.when(pl.program_id(1) == 0)
        def _():
            o_ref[...] = jnp.zeros_like(o_ref)

        o_ref[...] += _dot_tn(a_ref[...], d_ref[...])

    return _pcall(
        body, name=name, grid=(n_tiles, SEQ // tm),
        in_specs=[pl.BlockSpec((tm, k), lambda t, i: (i, 0)),
                  pl.BlockSpec((None, tm, tn), lambda t, i: (act_map(t)[0], i, act_map(t)[1]))],
        out_specs=pl.BlockSpec((None, k, tn), lambda t, i: (w_map(t)[0], 0, w_map(t)[1])),
        out_shape=_sds((n_out, k, ns), F32),
        compiler_params=_cparams(("parallel", "arbitrary")),
    )(a, dout)


def _retile_cols(src, *, n_out, width_out, tn, src_map, dst_map, n_tiles, name):
    k = src.shape[1]

    def body(s_ref, o_ref):
        o_ref[...] = s_ref[...]

    return _pcall(
        body, name=name, grid=(n_tiles,),
        in_specs=[pl.BlockSpec((None, k, tn), lambda t: (src_map(t)[0], 0, src_map(t)[1]))],
        out_specs=pl.BlockSpec((None, k, tn), lambda t: (dst_map(t)[0], 0, dst_map(t)[1])),
        out_shape=_sds((n_out, k, width_out), src.dtype),
        compiler_params=_cparams(("parallel",)),
    )(src)


def _mm_rows(a4, wg, layer, x, gate, name, tm=1024):
    ks = a4.shape[2]
    n = wg.shape[3]

    def body(a_ref, w_ref, x_ref, g_ref, z_ref, xn_ref):
        s = pl.program_id(1)

        @pl.when(s == 0)
        def _():
            z_ref[...] = jnp.zeros_like(z_ref)

        z_ref[...] += _dot(a_ref[...], w_ref[...])

        @pl.when(s == N_CHIP - 1)
        def _():
            xn_ref[...] = x_ref[...] + g_ref[...] * z_ref[...]

    big = pl.BlockSpec((tm, n), lambda i, s: (i, 0))
    return _pcall(
        body, name=name, grid=(SEQ // tm, N_CHIP),
        in_specs=[pl.BlockSpec((None, tm, ks), lambda i, s: (s, i, 0)),
                  pl.BlockSpec((None, None, ks, n), lambda i, s: (s, layer, 0, 0)),
                  big, pl.BlockSpec((1, n), lambda i, s: (0, 0))],
        out_specs=[big, big],
        out_shape=[_sds((SEQ, n), F32), _sds((SEQ, n), F32)],
        compiler_params=_cparams(("parallel", "arbitrary")),
    )(a4, wg, x, gate)


def _gate_bwd(dx, z, gate, name):
    tm = 512

    def body(dx_ref, z_ref, g_ref, dz_ref, dg_ref):
        @pl.when(pl.program_id(0) == 0)
        def _():
            dg_ref[...] = jnp.zeros_like(dg_ref)

        dxv = dx_ref[...]
        dz_ref[...] = (dxv * g_ref[...]).astype(BF16)
        dg_ref[...] += jnp.sum(dxv * z_ref[...], axis=0, keepdims=True)

    big = pl.BlockSpec((tm, D_MODEL), lambda i: (i, 0))
    vec = pl.BlockSpec((1, D_MODEL), lambda i: (0, 0))
    return _pcall(
        body, name=name, grid=(SEQ // tm,),
        in_specs=[big, big, vec], out_specs=[big, vec],
        out_shape=[_sds((SEQ, D_MODEL), BF16), _sds((1, D_MODEL), F32)],
        compiler_params=_cparams(("arbitrary",)),
    )(dx, z, gate)


def _mm_rows_bwd_a(dz, wg, layer, name, tm=1024, halves=()):
    ks, n = wg.shape[2], wg.shape[3]
    n_h = len(halves)
    n_rows = SEQ // tm

    def body(*refs):
        dz_ref, w_ref = refs[:2]
        o_ref = refs[2 + n_h]
        if n_h:
            comm_start, comm_wait = _rs_halves(refs[2:2 + n_h], refs[3 + n_h:3 + 2 * n_h], *refs[3 + 2 * n_h:])
            pl.when((pl.program_id(0) == 0) & (pl.program_id(1) == 0))(comm_start)
        o_ref[...] = _dot_nt(dz_ref[...], w_ref[...])
        if n_h:
            pl.when((pl.program_id(0) == n_rows - 1) & (pl.program_id(1) == N_CHIP - 1))(comm_wait)

    sem = pltpu.SemaphoreType.DMA((max(n_h, 1),))
    res = _pcall(
        body, name=name, grid=(n_rows, N_CHIP),
        in_specs=[pl.BlockSpec((tm, n), lambda i, s: (i, 0)),
                  pl.BlockSpec((None, None, ks, n), lambda i, s: (s, layer, 0, 0))] + [_ANY] * n_h,
        out_specs=[pl.BlockSpec((None, tm, ks), lambda i, s: (s, i, 0))] + [_ANY] * n_h,
        out_shape=[_sds((N_CHIP, SEQ, ks), F32)] + _rs_halves_shapes(halves),
        scratch_shapes=[sem, sem] if n_h else [],
        compiler_params=_cparams(("arbitrary", "arbitrary") if n_h else ("parallel", "arbitrary")),
    )(dz, wg, *halves)
    return res if n_h else res[0]


def _mm_rows_bwd_w(a4, dz, name, tm=1024):
    ks = a4.shape[2]
    n = dz.shape[1]

    def body(a_ref, dz_ref, o_ref):
        @pl.when(pl.program_id(1) == 0)
        def _():
            o_ref[...] = jnp.zeros_like(o_ref)

        o_ref[...] += _dot_tn(a_ref[...], dz_ref[...])

    return _pcall(
        body, name=name, grid=(N_CHIP, SEQ // tm),
        in_specs=[pl.BlockSpec((None, tm, ks), lambda s, i: (s, i, 0)),
                  pl.BlockSpec((tm, n), lambda s, i: (i, 0))],
        out_specs=pl.BlockSpec((None, ks, n), lambda s, i: (s, 0, 0)),
        out_shape=_sds((N_CHIP, ks, n), F32),
        compiler_params=_cparams(("parallel", "arbitrary")),
    )(a4, dz)


def _ffn_up(h, w1g, w3g, layer, name, tm=1024):
    def body(h_ref, w1_ref, w3_ref, a1_ref, a3_ref, u_ref):
        hv = h_ref[...]
        a1 = _dot_nt(hv, w1_ref[...])
        a3 = _dot_nt(hv, w3_ref[...])
        a1_ref[...] = a1
        a3_ref[...] = a3
        u_ref[...] = (_silu(a1) * a3).astype(BF16)

    wspec = pl.BlockSpec((None, None, FFN_SHARD, D_MODEL), lambda i, s: (s, layer, 0, 0))
    ospec = pl.BlockSpec((None, tm, FFN_SHARD), lambda i, s: (s, i, 0))
    shp = (N_CHIP, SEQ, FFN_SHARD)
    return _pcall(
        body, name=name, grid=(SEQ // tm, N_CHIP),
        in_specs=[pl.BlockSpec((tm, D_MODEL), lambda i, s: (i, 0)), wspec, wspec],
        out_specs=[ospec, ospec, ospec],
        out_shape=[_sds(shp, F32), _sds(shp, F32), _sds(shp, BF16)],
        compiler_params=_cparams(("parallel", "arbitrary")),
    )(h, w1g, w3g)


def _ffn_up_bwd(da1, da3, w1g, w3g, layer, name, tm=1024):
    def body(d1_ref, d3_ref, w1_ref, w3_ref, o_ref):
        @pl.when(pl.program_id(1) == 0)
        def _():
            o_ref[...] = jnp.zeros_like(o_ref)

        o_ref[...] += _dot(d1_ref[...], w1_ref[...]) + _dot(d3_ref[...], w3_ref[...])

    wspec = pl.BlockSpec((None, None, FFN_SHARD, D_MODEL), lambda i, s: (s, layer, 0, 0))
    dspec = pl.BlockSpec((None, tm, FFN_SHARD), lambda i, s: (s, i, 0))
    return _pcall(
        body, name=name, grid=(SEQ // tm, N_CHIP),
        in_specs=[dspec, dspec, wspec, wspec],
        out_specs=pl.BlockSpec((tm, D_MODEL), lambda i, s: (i, 0)),
        out_shape=_sds((SEQ, D_MODEL), F32),
        compiler_params=_cparams(("parallel", "arbitrary")),
    )(da1, da3, w1g, w3g)


def _ffn_down_bwd(dz, w2g, layer, a1, a3, name, tm=1024, halves=()):
    n_h = len(halves)
    n_rows = SEQ // tm

    def body(*refs):
        dz_ref, w_ref, a1_ref, a3_ref = refs[:4]
        da1_ref, da3_ref = refs[4 + n_h:6 + n_h]
        if n_h:
            comm_start, comm_wait = _rs_halves(refs[4:4 + n_h], refs[6 + n_h:6 + 2 * n_h], *refs[6 + 2 * n_h:])
            pl.when((pl.program_id(0) == 0) & (pl.program_id(1) == 0))(comm_start)
        du = _dot_nt(dz_ref[...], w_ref[...])
        a1 = a1_ref[...]
        da1_ref[...] = (du * a3_ref[...] * _dsilu(a1)).astype(BF16)
        da3_ref[...] = (du * _silu(a1)).astype(BF16)
        if n_h:
            pl.when((pl.program_id(0) == n_rows - 1) & (pl.program_id(1) == N_CHIP - 1))(comm_wait)

    blk = pl.BlockSpec((None, tm, FFN_SHARD), lambda i, s: (s, i, 0))
    shp = (N_CHIP, SEQ, FFN_SHARD)
    sem = pltpu.SemaphoreType.DMA((max(n_h, 1),))
    return _pcall(
        body, name=name, grid=(n_rows, N_CHIP),
        in_specs=[pl.BlockSpec((tm, D_MODEL), lambda i, s: (i, 0)),
                  pl.BlockSpec((None, None, FFN_SHARD, D_MODEL), lambda i, s: (s, layer, 0, 0)),
                  blk, blk] + [_ANY] * n_h,
        out_specs=[blk, blk] + [_ANY] * n_h,
        out_shape=[_sds(shp, BF16), _sds(shp, BF16)] + _rs_halves_shapes(halves),
        scratch_shapes=[sem, sem] if n_h else [],
        compiler_params=_cparams(("arbitrary", "arbitrary") if n_h else ("parallel", "arbitrary")),
    )(dz, w2g, a1, a3, *halves)


def _loss_head(y, target, name):
    tm = 512

    def body(y_ref, t_ref, dy_ref, l_ref, acc_ref):
        @pl.when(pl.program_id(0) == 0)
        def _():
            acc_ref[...] = jnp.zeros_like(acc_ref)

        err = y_ref[...] - t_ref[...]
        dy_ref[...] = err * (1.0 / D_MODEL)
        acc_ref[...] += jnp.sum(jnp.mean(err * err, axis=-1, keepdims=True), axis=0, keepdims=True)

        @pl.when(pl.program_id(0) == pl.num_programs(0) - 1)
        def _():
            l_ref[...] = 0.5 * acc_ref[...]

    big = pl.BlockSpec((tm, D_MODEL), lambda i: (i, 0))
    return _pcall(
        body, name=name, grid=(SEQ // tm,),
        in_specs=[big, big],
        out_specs=[big, pl.BlockSpec((1, 1), lambda i: (0, 0))],
        out_shape=[_sds((SEQ, D_MODEL), F32), _sds((1, 1), F32)],
        scratch_shapes=[pltpu.VMEM((1, 1), F32)],
        compiler_params=_cparams(("arbitrary",)),
    )(y, target)


def _attn_rows(base, d):
    if d == 1:
        return pl.ds(pl.multiple_of(base, ATT_BLK), ATT_BLK)
    return pl.ds(base, ATT_BLK, stride=d)


def _attn_block_index(i, d):
    nb = SEQ // (ATT_BLK * d)
    r = i // nb
    n = i % nb
    base = r + n * (ATT_BLK * d)
    pbase = jnp.maximum(base - ATT_BLK * d, r)
    return n, _attn_rows(base, d), _attn_rows(pbase, d)


def _attn_two_blocks(ref, prow, rows):
    return jnp.concatenate([ref[prow, :].astype(BF16), ref[rows, :].astype(BF16)], axis=0)


def _attn_block_bias(b_ref, n):
    b = b_ref[...]
    prev_half = lax.broadcasted_iota(jnp.int32, b.shape, 1) < ATT_BLK
    return jnp.where(prev_half & (n == 0), NEG, b)


def _qk_normed(x):
    rs = lax.rsqrt(jnp.mean(x * x, axis=-1, keepdims=True) + RMS_EPS)
    return x * rs, rs


def _attn_fwd(qkv9, qgain, kgain, bias, name, gather=()):
    n_g = len(gather)

    def body(*refs):
        q_ref, k_ref, v_ref, qg_ref, kg_ref, b_ref = refs[:6]
        o_ref, lse_ref = refs[6 + n_g:8 + n_g]
        qn_s, kn_s, acc_s, m_s, l_s = refs[8 + 2 * n_g:13 + 2 * n_g]
        g = pl.program_id(1)
        if n_g:
            comm_start, comm_wait = _gather_ici(refs[8 + n_g:8 + 2 * n_g], *refs[13 + 2 * n_g:])
            pl.when((pl.program_id(0) == 0) & (g == 0))(comm_start)

        @pl.when(g == 0)
        def _():
            m_s[...] = jnp.full_like(m_s, NEG)
            l_s[...] = jnp.zeros_like(l_s)
            acc_s[...] = jnp.zeros_like(acc_s)

        qn_s[...] = _qk_normed(q_ref[...])[0] * qg_ref[...]
        kn_s[...] = _qk_normed(k_ref[...])[0] * kg_ref[...]

        for gi, (_, d) in enumerate(GROUPS):
            @pl.when(g == gi)
            def _(d=d):
                def block(n, qb, kk, vv, m_old, l_old, acc_old):
                    s = _dot_nt(qb, kk) * ATT_SCALE + _attn_block_bias(b_ref, n)
                    m_new = jnp.maximum(m_old, jnp.max(s, axis=-1, keepdims=True))
                    alpha = jnp.exp(m_old - m_new)
                    p = jnp.exp(s - m_new)
                    l_new = alpha * l_old + jnp.sum(p, axis=-1, keepdims=True)
                    acc_new = alpha * acc_old + _dot(p.astype(BF16), vv)
                    return m_new, l_new, acc_new

                def it(i, carry):
                    where, loaded = [], []
                    for way in range(ATT_WAYS):
                        n, rows, prow = _attn_block_index(i + way * ATT_STEPS, d)
                        where.append(rows)
                        loaded.append((n, qn_s[rows, :].astype(BF16), _attn_two_blocks(kn_s, prow, rows),
                                       _attn_two_blocks(v_ref, prow, rows), m_s[rows, :], l_s[rows, :],
                                       acc_s[rows, :]))
                    results = [block(*vals) for vals in loaded]
                    for rows, (m_new, l_new, acc_new) in zip(where, results):
                        m_s[rows, :] = m_new
                        l_s[rows, :] = l_new
                        acc_s[rows, :] = acc_new
                    return carry

                lax.fori_loop(0, ATT_STEPS, it, 0)

        @pl.when(g == len(GROUPS) - 1)
        def _():
            o_ref[...] = (acc_s[...] / l_s[...]).astype(BF16)
            lse_ref[...] = m_s[...] + jnp.log(l_s[...])

        if n_g:
            pl.when((pl.program_id(0) == HEADS - 1) & (g == len(GROUPS) - 1))(comm_wait)

    def col(j):
        return pl.BlockSpec((None, SEQ, HEAD_DIM), lambda h, g: (g * 3 + j, 0, h))

    gspec = pl.BlockSpec((None, 1, HEAD_DIM), lambda h, g: (g, 0, 0))
    sem = pltpu.SemaphoreType.DMA((max(n_g, 1), 3))
    return _pcall(
        body, name=name, grid=(HEADS, len(GROUPS)),
        in_specs=[col(0), col(1), col(2), gspec, gspec,
                  pl.BlockSpec((None, None, ATT_BLK, 2 * ATT_BLK), lambda h, g: (g, h, 0, 0))] + [_ANY] * n_g,
        out_specs=[pl.BlockSpec((None, SEQ, HEAD_DIM), lambda h, g: (h // 2, 0, h % 2)),
                   pl.BlockSpec((None, SEQ, 1), lambda h, g: (h, 0, 0))] + [_ANY] * n_g,
        out_shape=[_sds((N_CHIP, SEQ, 2 * HEAD_DIM), BF16), _sds((HEADS, SEQ, 1), F32)]
        + [_sds(s.shape, s.dtype) for s in gather],
        input_output_aliases={6 + a: 2 + a for a in range(n_g)},
        scratch_shapes=[pltpu.VMEM((SEQ, HEAD_DIM), F32)] * 3 + [pltpu.VMEM((SEQ, 1), F32)] * 2
        + ([sem, sem] if n_g else []),
        compiler_params=_cparams(("arbitrary", "arbitrary")),
    )(qkv9, qkv9, qkv9, qgain, kgain, bias, *gather)


def _attn_bwd(qkv9, qgain, kgain, bias, do4, o4, lse, name, scatter=()):
    n_s = len(scatter)

    def body(*refs):
        q_ref, k_ref, v_ref, qg_ref, kg_ref, b_ref, do_ref, o_ref, lse_ref = refs[:9]
        dqkv_ref, dqg_ref, dkg_ref, db_ref = refs[9 + n_s:13 + n_s]
        qn_s, kn_s, dq_s, dk_s, dv_s, dl_s = refs[13 + 2 * n_s:19 + 2 * n_s]
        g = pl.program_id(1)
        if n_s:
            comm_start, comm_wait = _rs_chips(refs[9:9 + n_s], refs[13 + n_s:13 + 2 * n_s], *refs[19 + 2 * n_s:])
            pl.when((pl.program_id(0) == 0) & (g == 0))(comm_start)
        qh, rq = _qk_normed(q_ref[...])
        kh, rk = _qk_normed(k_ref[...])
        qn_s[...] = qh * qg_ref[...]
        kn_s[...] = kh * kg_ref[...]
        dl_s[...] = jnp.sum(do_ref[...] * o_ref[...].astype(F32), axis=-1, keepdims=True)
        dk_s[...] = jnp.zeros_like(dk_s)
        dv_s[...] = jnp.zeros_like(dv_s)
        db_ref[...] = jnp.zeros_like(db_ref)

        for gi, (_, d) in enumerate(GROUPS):
            @pl.when(g == gi)
            def _(d=d):
                def block(n, qb, kk, vv, dob, lse_b, dl):
                    s = _dot_nt(qb, kk) * ATT_SCALE + _attn_block_bias(b_ref, n)
                    p = jnp.exp(s - lse_b)
                    ds = p * (_dot_nt(dob, vv) - dl)
                    ds16 = ds.astype(BF16)
                    return (ds, _dot(ds16, kk) * ATT_SCALE, _dot_tn(ds16, qb) * ATT_SCALE,
                            _dot_tn(p.astype(BF16), dob))

                def it(i, carry):
                    where, loaded, old = [], [], []
                    for way in range(ATT_WAYS):
                        n, rows, prow = _attn_block_index(i + way * ATT_STEPS, d)
                        where.append((rows, prow))
                        loaded.append((n, qn_s[rows, :].astype(BF16), _attn_two_blocks(kn_s, prow, rows),
                                       _attn_two_blocks(v_ref, prow, rows), do_ref[rows, :].astype(BF16),
                                       lse_ref[rows, :], dl_s[rows, :]))
                        old.append((dk_s[rows, :], dk_s[prow, :], dv_s[rows, :], dv_s[prow, :]))
                    results = [block(*vals) for vals in loaded]
                    db_ref[...] += functools.reduce(lambda a, b: a + b, [r[0] for r in results])
                    for (rows, prow), (dk_c, dk_p, dv_c, dv_p), (_, dq, dkk, dvv) in zip(where, old, results):
                        dq_s[rows, :] = dq
                        dk_s[prow, :] = dk_p + dkk[:ATT_BLK]
                        dv_s[prow, :] = dv_p + dvv[:ATT_BLK]
                        dk_s[rows, :] = dk_c + dkk[ATT_BLK:]
                        dv_s[rows, :] = dv_c + dvv[ATT_BLK:]
                    return carry

                lax.fori_loop(0, ATT_STEPS, it, 0)

        def norm_bwd(dn, xh, rs, gain):
            dgain = jnp.sum(dn * xh, axis=0, keepdims=True)
            dxh = dn * gain
            return rs * (dxh - xh * jnp.mean(dxh * xh, axis=-1, keepdims=True)), dgain

        dq, dqg = norm_bwd(dq_s[...], qh, rq, qg_ref[...])
        dk, dkg = norm_bwd(dk_s[...], kh, rk, kg_ref[...])
        dqkv_ref[0] = dq.astype(BF16)
        dqkv_ref[1] = dk.astype(BF16)
        dqkv_ref[2] = dv_s[...].astype(BF16)
        dqg_ref[...] = dqg
        dkg_ref[...] = dkg
        if n_s:
            pl.when((pl.program_id(0) == HEADS - 1) & (g == len(GROUPS) - 1))(comm_wait)

    def col(j):
        return pl.BlockSpec((None, SEQ, HEAD_DIM), lambda h, g: (g * 3 + j, 0, h))

    gspec = pl.BlockSpec((None, 1, HEAD_DIM), lambda h, g: (g, 0, 0))
    bspec = pl.BlockSpec((None, None, ATT_BLK, 2 * ATT_BLK), lambda h, g: (g, h, 0, 0))
    hcol = pl.BlockSpec((None, SEQ, HEAD_DIM), lambda h, g: (h // 2, 0, h % 2))
    dgspec = pl.BlockSpec((None, None, 1, HEAD_DIM), lambda h, g: (h, g, 0, 0))
    ng = len(GROUPS)
    sem = pltpu.SemaphoreType.DMA((max(n_s, 1), 3))
    return _pcall(
        body, name=name, grid=(HEADS, ng),
        in_specs=[col(0), col(1), col(2), gspec, gspec, bspec, hcol, hcol,
                  pl.BlockSpec((None, SEQ, 1), lambda h, g: (h, 0, 0))] + [_ANY] * n_s,
        out_specs=[pl.BlockSpec((None, 3, SEQ, HEAD_DIM), lambda h, g: (g, 0, 0, h)), dgspec, dgspec, bspec]
        + [_ANY] * n_s,
        out_shape=[_sds((ng, 3, SEQ, D_MODEL), BF16), _sds((HEADS, ng, 1, HEAD_DIM), F32),
                   _sds((HEADS, ng, 1, HEAD_DIM), F32), _sds((ng, HEADS, ATT_BLK, 2 * ATT_BLK), F32)]
        + _rs_chips_shapes(scatter),
        scratch_shapes=[pltpu.VMEM((SEQ, HEAD_DIM), F32)] * 5 + [pltpu.VMEM((SEQ, 1), F32)]
        + ([sem, sem] if n_s else []),
        compiler_params=_cparams(("arbitrary", "arbitrary")),
    )(qkv9, qkv9, qkv9, qgain, kgain, bias, do4, o4, lse, *scatter)


def _relbias_bwd(dbias, bucket_idx, name):
    ng = len(GROUPS)

    def body(db_ref, idx_ref, o_ref):
        lane = lax.broadcasted_iota(jnp.int32, (HEADS, 128), 1)
        acc = jnp.zeros((HEADS, 128), F32)
        for g in range(ng):
            dbg = db_ref[g]
            idx = idx_ref[g]
            for b in range(NUM_BUCKETS):
                sel = jnp.where((idx == b)[None], dbg, 0.0)
                part = jnp.sum(sel, axis=1)
                val = jnp.sum(part, axis=-1, keepdims=True)
                acc = jnp.where(lane == g * NUM_BUCKETS + b, val, acc)
        o_ref[...] = acc

    return _pcall(body, name=name, out_shape=_sds((HEADS, 128), F32), compiler_params=_cparams())(dbias, bucket_idx)


def _scan16(x, reverse=False):
    row = lax.broadcasted_iota(jnp.int32, x.shape, 0)
    for sh in (1, 2, 4, 8):
        if reverse:
            x = x + jnp.where(row < HG_SUB - sh, pltpu.roll(x, HG_SUB - sh, 0), 0.0)
        else:
            x = x + jnp.where(row >= sh, pltpu.roll(x, sh, 0), 0.0)
    return x


def _hgrn_gates(qr, fr, lbv):
    q = _silu(qr)
    sig = _sigmoid(fr)
    fg = lbv + (1.0 - lbv) * sig
    lf = jnp.log(fg)
    gcum = _scan16(lf)
    glast = jnp.sum(lf, axis=0, keepdims=True)
    return q, sig, fg, 1.0 - fg, gcum, glast


def _hgrn_intra(q, k, gcum, tri):
    e = jnp.exp(jnp.where(tri, gcum[:, None, :] - gcum[None, :, :], NEG))
    a = jnp.sum(q[:, None, :] * k[None, :, :] * e, axis=-1, keepdims=True)
    return e, a


def _hgrn_fwd(proj4, lb, gain, name):
    nsub = HG_TC // HG_SUB
    wide = HG_HP * HEAD_DIM

    def body(p_ref, lb_ref, gn_ref, o_ref, y_ref, st_ref, state_s):
        @pl.when(pl.program_id(1) == 0)
        def _():
            state_s[...] = jnp.zeros_like(state_s)

        gnv = gn_ref[...]
        shp = (HG_SUB, HG_SUB, HEAD_DIM)
        tri = lax.broadcasted_iota(jnp.int32, shp, 0) >= lax.broadcasted_iota(jnp.int32, shp, 1)

        def head(qr, fr, vv, gr, lbv, st):
            q, _, _, k, gcum, glast = _hgrn_gates(qr, fr, lbv)
            _, a = _hgrn_intra(q, k, gcum, tri)
            o = jnp.sum(a * vv[None, :, :], axis=1) + _dot_nt((q * jnp.exp(gcum)).astype(BF16), st.astype(BF16))
            kg = k * jnp.exp(glast - gcum)
            st_new = st * jnp.exp(glast) + _dot_tn(vv.astype(BF16), kg.astype(BF16))
            rs = lax.rsqrt(jnp.mean(o * o, axis=-1, keepdims=True) + RMS_EPS)
            return o, (o * rs * gnv * _silu(gr)).astype(BF16), st_new

        def it(i, carry):
            rows = pl.ds(pl.multiple_of(i * HG_SUB, HG_SUB), HG_SUB)
            loaded = []
            for hh in range(HG_HP):
                lanes = pl.ds(hh * HEAD_DIM, HEAD_DIM)
                loaded.append(([p_ref[j, rows, lanes] for j in range(4)], lb_ref[:, lanes], state_s[hh]))
            results = [head(blk[0], blk[1], blk[2], blk[3], lbv, st) for blk, lbv, st in loaded]
            for hh, ((_, _, st), (o, y, st_new)) in enumerate(zip(loaded, results)):
                lanes = pl.ds(hh * HEAD_DIM, HEAD_DIM)
                st_ref[hh, i] = st.astype(BF16)
                state_s[hh] = st_new
                o_ref[rows, lanes] = o
                y_ref[hh // 2, rows, pl.ds((hh % 2) * HEAD_DIM, HEAD_DIM)] = y
            return carry

        lax.fori_loop(0, nsub, it, 0)

    return _pcall(
        body, name=name, grid=(HEADS // HG_HP, SEQ // HG_TC),
        in_specs=[pl.BlockSpec((4, HG_TC, wide), lambda h, j: (0, j, h)),
                  pl.BlockSpec((1, wide), lambda h, j: (0, h)),
                  pl.BlockSpec((1, HEAD_DIM), lambda h, j: (0, 0))],
        out_specs=[pl.BlockSpec((HG_TC, wide), lambda h, j: (j, h)),
                   pl.BlockSpec((HG_HP // 2, HG_TC, 2 * HEAD_DIM), lambda h, j: (h, j, 0)),
                   pl.BlockSpec((HG_HP, nsub, HEAD_DIM, HEAD_DIM), lambda h, j: (h, j, 0, 0))],
        out_shape=[_sds((SEQ, D_MODEL), F32), _sds((N_CHIP, SEQ, 2 * HEAD_DIM), BF16),
                   _sds((HEADS, SEQ // HG_SUB, HEAD_DIM, HEAD_DIM), BF16)],
        scratch_shapes=[pltpu.VMEM((HG_HP, HEAD_DIM, HEAD_DIM), F32)],
        compiler_params=_cparams(("parallel", "arbitrary")),
    )(proj4, lb, gain)


def _hgrn_bwd(proj4, lb, gain, o_raw, dy4, states, name):
    nsub = HG_TC // HG_SUB
    nt = SEQ // HG_TC
    wide = HG_HP * HEAD_DIM

    def body(p_ref, lb_ref, gn_ref, o_ref, dy_ref, st_ref, dp_ref, dlb_ref, dgn_ref, dst_s):
        @pl.when(pl.program_id(1) == 0)
        def _():
            dst_s[...] = jnp.zeros_like(dst_s)
            dlb_ref[...] = jnp.zeros_like(dlb_ref)
            dgn_ref[...] = jnp.zeros_like(dgn_ref)

        gnv = gn_ref[...]
        shp = (HG_SUB, HG_SUB, HEAD_DIM)
        tri = lax.broadcasted_iota(jnp.int32, shp, 0) >= lax.broadcasted_iota(jnp.int32, shp, 1)

        def head(qr, fr, vv, gr, o, dy, lbv, st0, dst):
            q, sig, fg, k, gcum, glast = _hgrn_gates(qr, fr, lbv)
            rs = lax.rsqrt(jnp.mean(o * o, axis=-1, keepdims=True) + RMS_EPS)
            oh = o * rs
            don = dy * _silu(gr)
            dgn = jnp.sum(don * oh, axis=0, keepdims=True)
            dgr = dy * oh * gnv * _dsilu(gr)
            doh = don * gnv
            do = rs * (doh - oh * jnp.mean(doh * oh, axis=-1, keepdims=True))
            dst16 = dst.astype(BF16)
            do16 = do.astype(BF16)
            eg = jnp.exp(gcum)
            eb = jnp.exp(glast - gcum)
            e, a = _hgrn_intra(q, k, gcum, tri)
            da = jnp.sum(do[:, None, :] * vv[None, :, :], axis=-1, keepdims=True)
            dae = da * e
            dq = jnp.sum(dae * k[None, :, :], axis=1) + eg * _dot(do16, st0)
            dk_state = eb * _dot(vv.astype(BF16), dst16)
            dk = jnp.sum(dae * q[:, None, :], axis=0) + dk_state
            dv = jnp.sum(a * do[:, None, :], axis=0) + _dot_nt((k * eb).astype(BF16), dst16)
            eglast = jnp.exp(glast)
            dst_new = dst * eglast + _dot_tn(do16, (q * eg).astype(BF16))
            dglast = jnp.sum(k * dk_state, axis=0, keepdims=True) \
                + eglast * jnp.sum(dst * st0.astype(F32), axis=0, keepdims=True)
            dlf = _scan16(q * dq - k * dk, reverse=True) + dglast
            dfg = dlf / fg - dk
            dlb = jnp.sum(dfg * (1.0 - sig), axis=0, keepdims=True)
            dproj = ((dq * _dsilu(qr)).astype(BF16), (dfg * (1.0 - lbv) * sig * (1.0 - sig)).astype(BF16),
                     dv.astype(BF16), dgr.astype(BF16))
            return dproj, dst_new, dlb, dgn

        def it(ii, carry):
            i = nsub - 1 - ii
            rows = pl.ds(pl.multiple_of(i * HG_SUB, HG_SUB), HG_SUB)
            results = []
            for hh in range(HG_HP):
                lanes = pl.ds(hh * HEAD_DIM, HEAD_DIM)
                blk = [p_ref[j, rows, lanes] for j in range(4)]
                dy = dy_ref[hh // 2, rows, pl.ds((hh % 2) * HEAD_DIM, HEAD_DIM)]
                results.append(head(blk[0], blk[1], blk[2], blk[3], o_ref[rows, lanes], dy,
                                    lb_ref[:, lanes], st_ref[hh, i], dst_s[hh]))
            new_carry = []
            for hh, (dproj, dst_new, dlb, dgn) in enumerate(results):
                lanes = pl.ds(hh * HEAD_DIM, HEAD_DIM)
                dst_s[hh] = dst_new
                for j in range(4):
                    dp_ref[j, rows, lanes] = dproj[j]
                new_carry.append((carry[hh][0] + dlb, carry[hh][1] + dgn))
            return tuple(new_carry)

        zero = jnp.zeros((1, HEAD_DIM), F32)
        sums = lax.fori_loop(0, nsub, it, tuple((zero, zero) for _ in range(HG_HP)))
        for hh in range(HG_HP):
            dlb_ref[hh] += sums[hh][0]
            dgn_ref[hh] += sums[hh][1]

    vspec = pl.BlockSpec((HG_HP, 1, HEAD_DIM), lambda h, j: (h, 0, 0))
    return _pcall(
        body, name=name, grid=(HEADS // HG_HP, nt),
        in_specs=[pl.BlockSpec((4, HG_TC, wide), lambda h, j: (0, nt - 1 - j, h)),
                  pl.BlockSpec((1, wide), lambda h, j: (0, h)),
                  pl.BlockSpec((1, HEAD_DIM), lambda h, j: (0, 0)),
                  pl.BlockSpec((HG_TC, wide), lambda h, j: (nt - 1 - j, h)),
                  pl.BlockSpec((HG_HP // 2, HG_TC, 2 * HEAD_DIM), lambda h, j: (h, nt - 1 - j, 0)),
                  pl.BlockSpec((HG_HP, nsub, HEAD_DIM, HEAD_DIM), lambda h, j: (h, nt - 1 - j, 0, 0))],
        out_specs=[pl.BlockSpec((4, HG_TC, wide), lambda h, j: (0, nt - 1 - j, h)), vspec, vspec],
        out_shape=[_sds((4, SEQ, D_MODEL), BF16), _sds((HEADS, 1, HEAD_DIM), F32), _sds((HEADS, 1, HEAD_DIM), F32)],
        scratch_shapes=[pltpu.VMEM((HG_HP, HEAD_DIM, HEAD_DIM), F32)],
        compiler_params=_cparams(("parallel", "arbitrary")),
    )(proj4, lb, gain, o_raw, dy4, states)


def _t5_bucket(dist):
    n = np.asarray(dist, dtype=np.int64)
    max_exact = NUM_BUCKETS // 2
    large = max_exact + (np.log(np.maximum(n, 1) / max_exact) / np.log(MAX_DISTANCE / max_exact)
                         * (NUM_BUCKETS - max_exact)).astype(np.int64)
    large = np.minimum(large, NUM_BUCKETS - 1)
    return np.where(n < max_exact, n, large).astype(np.int32)


def _bias_tables():
    qi = np.arange(ATT_BLK)[:, None]
    ki = np.arange(2 * ATT_BLK)[None, :]
    j = ATT_BLK + qi - ki
    valid = (j >= 0) & (j <= ATT_BLK)
    return np.stack([np.where(valid, _t5_bucket(np.clip(j, 0, ATT_BLK) * d), -1) for _, d in GROUPS]).astype(np.int32)


def _attn_bias(rel_bias, name):
    idx = _bias_tables()
    ng = len(GROUPS)
    buckets = [sorted(set(idx[g][idx[g] >= 0].tolist())) for g in range(ng)]

    def body(rb_ref, idx_ref, o_ref):
        h = pl.program_id(0)
        for g in range(ng):
            ig = idx_ref[g]
            acc = jnp.full(ig.shape, NEG, F32)
            for b in buckets[g]:
                acc = jnp.where(ig == b, rb_ref[b, g * HEADS + h], acc)
            o_ref[g] = acc

    return _pcall(
        body, name=name, grid=(HEADS,),
        in_specs=[pl.BlockSpec(memory_space=pltpu.SMEM),
                  pl.BlockSpec((ng, ATT_BLK, 2 * ATT_BLK), lambda h: (0, 0, 0))],
        out_specs=pl.BlockSpec((ng, None, ATT_BLK, 2 * ATT_BLK), lambda h: (0, h, 0, 0)),
        out_shape=_sds((ng, HEADS, ATT_BLK, 2 * ATT_BLK), F32),
        compiler_params=_cparams(("parallel",)),
    )(rel_bias, jnp.asarray(idx))


ADA_SHARD = 6 * D_MODEL // N_CHIP
ADA_TN = 512


def _ada_fwd(c_all, ada_w, ada_b_cols, name):
    def body(c_ref, w_ref, b_ref, o_ref):
        ca = _silu(c_ref[...]).astype(BF16)
        o_ref[...] = _dot(ca, w_ref[...].astype(BF16)) + b_ref[...]

    return _pcall(
        body, name=name, grid=(DEPTH, ADA_SHARD // ADA_TN),
        in_specs=[pl.BlockSpec((N_DEV, D_MODEL), lambda l, j: (0, 0)),
                  pl.BlockSpec((None, D_MODEL, ADA_TN), lambda l, j: (l, 0, j)),
                  pl.BlockSpec((None, 1, ADA_TN), lambda l, j: (l, 0, j))],
        out_specs=pl.BlockSpec((None, N_DEV, ADA_TN), lambda l, j: (l, 0, j)),
        out_shape=_sds((DEPTH, N_DEV, ADA_SHARD), F32),
        compiler_params=_cparams(("parallel", "parallel")),
    )(c_all, ada_w, ada_b_cols)


def _ada_bwd(c_all, dmod_cols, name):
    def body(c_ref, d_ref, o_ref):
        ca = _silu(c_ref[...]).astype(BF16)
        o_ref[...] = _dot_tn(ca, d_ref[...].astype(BF16))

    return _pcall(
        body, name=name, grid=(DEPTH, ADA_SHARD // ADA_TN),
        in_specs=[pl.BlockSpec((N_DEV, D_MODEL), lambda l, j: (0, 0)),
                  pl.BlockSpec((None, N_DEV, ADA_TN), lambda l, j: (l, 0, j))],
        out_specs=pl.BlockSpec((None, D_MODEL, ADA_TN), lambda l, j: (l, 0, j)),
        out_shape=_sds((DEPTH, D_MODEL, ADA_SHARD), F32),
        compiler_params=_cparams(("parallel", "parallel")),
    )(c_all, dmod_cols)


def _lower_bounds(logits, name):
    def body(l_ref, o_ref):
        l0 = l_ref[0:1, :]
        l1 = l_ref[1:2, :]
        mx = jnp.maximum(l0, l1)
        e0 = jnp.exp(l0 - mx)
        e1 = jnp.exp(l1 - mx)
        p0 = e0 / (e0 + e1)
        p1 = e1 / (e0 + e1)
        o_ref[0:1, :] = p0 - p0
        o_ref[1:2, :] = (p0 + p1) - p0

    return _pcall(body, name=name, out_shape=_sds((DEPTH, D_MODEL), F32), compiler_params=_cparams())(logits)


_R_DMOD = 0
_R_NMIX = 96
_R_NFFN = 112
_R_QG = 128
_R_KG = 152
_R_GN = 176
_R_LB = 184
_R_RB = 192
SMALL_ROWS = 200


def _small_totals(gathered, logits8, name):
    ng = len(GROUPS)

    def body(g_ref, l_ref, main_ref, gains_ref, dlb_ref, rb_ref):
        tot = g_ref[0]
        for dev in range(1, N_DEV):
            tot = tot + g_ref[dev]
        main_ref[...] = tot[0:_R_QG]
        gains_ref[...] = jnp.zeros_like(gains_ref)
        for g in range(ng):
            gains_ref[g:g + 1, :] = jnp.sum(tot[_R_QG + 8 * g:_R_QG + 8 * g + 8], axis=0, keepdims=True)
            gains_ref[ng + g:ng + g + 1, :] = jnp.sum(tot[_R_KG + 8 * g:_R_KG + 8 * g + 8], axis=0, keepdims=True)
        gains_ref[2 * ng:2 * ng + 1, :] = jnp.sum(tot[_R_GN:_R_GN + 8], axis=0, keepdims=True)
        rb_ref[...] = tot[_R_RB:_R_RB + 8]
        dlb1 = tot[_R_LB:_R_LB + 8]
        l0 = l_ref[0]
        l1 = l_ref[1]
        mx = jnp.maximum(l0, l1)
        e0 = jnp.exp(l0 - mx)
        e1 = jnp.exp(l1 - mx)
        p0 = e0 / (e0 + e1)
        p1 = e1 / (e0 + e1)
        dlb_ref[0] = -p0 * p1 * dlb1
        dlb_ref[1] = p1 * (1.0 - p1) * dlb1

    return _pcall(
        body, name=name,
        out_shape=[_sds((_R_QG, 128), F32), _sds((8, 128), F32), _sds((DEPTH, 8, 128), F32), _sds((8, 128), F32)],
        compiler_params=_cparams(),
    )(gathered, logits8)


def _row_tile(rows):
    return 128 if rows % 128 == 0 else rows


def _adamw(w, grads, m, v, name):
    nl, r, cdim = w.shape
    tr = _row_tile(r)

    def body(*refs):
        g_refs = refs[:nl]
        w_ref, m_ref, v_ref, go_ref, d_ref, mo_ref, vo_ref = refs[nl:]

        def step(g):
            m2 = ADAM_B1 * m_ref[...] + (1.0 - ADAM_B1) * g
            v2 = ADAM_B2 * v_ref[...] + (1.0 - ADAM_B2) * (g * g)
            m_hat = m2 / (1.0 - ADAM_B1 ** ADAM_STEP)
            v_hat = v2 / (1.0 - ADAM_B2 ** ADAM_STEP)
            go_ref[...] = g
            d_ref[...] = -ADAM_LR * (m_hat / (jnp.sqrt(v_hat) + ADAM_EPS) + ADAM_WD * w_ref[...])
            mo_ref[...] = m2
            vo_ref[...] = v2

        if nl == 1:
            step(g_refs[0][...])
        else:
            for layer in range(nl):
                @pl.when(pl.program_id(0) == layer)
                def _(layer=layer):
                    step(g_refs[layer][...])

    big = pl.BlockSpec((None, tr, cdim), lambda l, i: (l, i, 0))
    g_specs = [pl.BlockSpec((tr, cdim), lambda l, i, layer=layer: (jnp.where(l == layer, i, 0), 0))
               for layer in range(nl)]
    shp = _sds((nl, r, cdim), F32)
    return _pcall(
        body, name=name, grid=(nl, r // tr),
        in_specs=g_specs + [big, big, big],
        out_specs=[big, big, big, big],
        out_shape=[shp, shp, shp, shp],
        compiler_params=_cparams(("parallel", "parallel")),
    )(*grads, w, m, v)


def _cast_bf16(place, w, name):
    nl, r, cdim = w.shape
    tr = _row_tile(r)

    def body(place_ref, w_ref, o_ref):
        o_ref[...] = w_ref[...].astype(BF16)

    return _pcall(
        body, name=name,
        grid_spec=pltpu.PrefetchScalarGridSpec(
            num_scalar_prefetch=1, grid=(nl, r // tr),
            in_specs=[pl.BlockSpec((None, tr, cdim), lambda l, i, place_ref: (l, i, 0))],
            out_specs=pl.BlockSpec((None, None, tr, cdim), lambda l, i, place_ref: (place_ref[1], l, i, 0))),
        out_shape=_sds((N_CHIP, nl, r, cdim), BF16),
        compiler_params=_cparams(("parallel", "parallel")),
    )(place, w)


def _rs_add_cast(place, grad, recv, name):
    _, k, n = grad.shape
    kh = k // 2
    tr = _row_tile(kh)
    nb = kh // tr

    def body(place_ref, g_ref, r_ref, o_ref):
        o_ref[...] = (g_ref[...] + r_ref[...]).astype(BF16)

    half = pl.BlockSpec((None, tr, n), lambda s, i, place_ref: (s, i, 0))
    return _pcall(
        body, name=name,
        grid_spec=pltpu.PrefetchScalarGridSpec(
            num_scalar_prefetch=1, grid=(N_CHIP, nb),
            in_specs=[pl.BlockSpec((None, tr, n), lambda s, i, place_ref: (s, place_ref[0] * nb + i, 0)), half],
            out_specs=half),
        out_shape=_sds((N_CHIP, kh, n), BF16),
        compiler_params=_cparams(("parallel", "parallel")),
    )(place, grad, recv)


def _rs_sum4(place, parts, got, name):
    _, kh, n = parts.shape
    tr = _row_tile(kh)
    nb = kh // tr

    def body(place_ref, p_ref, g_ref, o_ref):
        acc = p_ref[...].astype(F32)
        for j in range(N_CHIP - 1):
            acc = acc + g_ref[j].astype(F32)
        o_ref[...] = acc

    return _pcall(
        body, name=name,
        grid_spec=pltpu.PrefetchScalarGridSpec(
            num_scalar_prefetch=1, grid=(nb,),
            in_specs=[pl.BlockSpec((None, tr, n), lambda i, place_ref: (place_ref[1], i, 0)),
                      pl.BlockSpec((N_CHIP - 1, tr, n), lambda i, place_ref: (0, i, 0))],
            out_specs=pl.BlockSpec((tr, n), lambda i, place_ref: (place_ref[0] * nb + i, 0))),
        out_shape=_sds((2 * kh, n), F32),
        compiler_params=_cparams(("parallel",)),
    )(place, parts, got)


_ANY = pl.BlockSpec(memory_space=pl.ANY)


def _position():
    return lax.axis_index("x"), lax.axis_index("y"), lax.axis_index("c")


def _other_chips(x, y):
    return [(1 - x, y), (x, 1 - y), (1 - x, 1 - y)]


def _remote(src, dst, send_sem, recv_sem, to):
    return pltpu.make_async_remote_copy(src_ref=src, dst_ref=dst, send_sem=send_sem, recv_sem=recv_sem,
                                        device_id=to, device_id_type=MESH)


def _small_allgather(v, name):
    r = v.shape[0]

    def body(x_ref, out_ref, send_sems, recv_sems, local_sem):
        x, y, c = _position()
        me, sibling = (x, y, c), (x, y, 1 - c)
        chips = _other_chips(x, y)

        def slab(px, py, pc):
            return out_ref.at[4 * px + 2 * py + pc]

        def copy(k, block, to, src=None):
            return _remote(slab(*block) if src is None else src, slab(*block), send_sems.at[k], recv_sems.at[k], to)

        mine = pltpu.make_async_copy(x_ref, slab(*me), local_sem)
        mine.start()
        first = [copy(0, me, sibling, src=x_ref)]
        first += [copy(1 + j, me, (*chip, c), src=x_ref) for j, chip in enumerate(chips)]
        for cp in first:
            cp.start()
        passed = [copy(4 + j, (*chip, c), sibling) for j, chip in enumerate(chips)]
        for j, chip in enumerate(chips):
            copy(1 + j, (*chip, c), me).wait_recv()
            passed[j].start()
        copy(0, sibling, me).wait_recv()
        for j, chip in enumerate(chips):
            copy(4 + j, (*chip, 1 - c), me).wait_recv()
        for cp in first + passed:
            cp.wait_send()
        mine.wait()

    return _pcall(
        body, name=name,
        out_shape=_sds((N_DEV, r, 128), F32),
        in_specs=[pl.BlockSpec(memory_space=pltpu.VMEM)],
        out_specs=pl.BlockSpec(memory_space=pltpu.VMEM),
        scratch_shapes=[pltpu.SemaphoreType.DMA((7,)), pltpu.SemaphoreType.DMA((7,)), pltpu.SemaphoreType.DMA],
        compiler_params=_cparams(),
    )(v)


def _half_rows(core, kh):
    return pl.ds(pl.multiple_of(core * kh, 8), kh)


def _slab_half(ref, chip, core):
    return ref.at[chip, :, _half_rows(core, ref.shape[2] // 2), :]


def _gather_ici(out, send_sems, recv_sems):
    def copies():
        x, y, c = _position()
        for a in range(len(out)):
            for j, (px, py) in enumerate(_other_chips(x, y)):
                mine = _slab_half(out[a], 2 * x + y, c)
                landed = _slab_half(out[a], 2 * px + py, c)
                yield (_remote(mine, mine, send_sems.at[a, j], recv_sems.at[a, j], (px, py, c)),
                       _remote(landed, landed, send_sems.at[a, j], recv_sems.at[a, j], (px, py, c)))

    def start():
        for send, _ in copies():
            send.start()

    def wait():
        for send, recv in copies():
            recv.wait_recv()
            send.wait_send()

    return start, wait


def _gather_d2d(out, send_sems, recv_sems):
    def copies():
        x, y, c = _position()
        for a in range(len(out)):
            for j, (px, py) in enumerate(_other_chips(x, y)):
                landed = _slab_half(out[a], 2 * px + py, c)
                other = _slab_half(out[a], 2 * px + py, 1 - c)
                yield (_remote(landed, landed, send_sems.at[a, j], recv_sems.at[a, j], (x, y, 1 - c)),
                       _remote(other, other, send_sems.at[a, j], recv_sems.at[a, j], (x, y, 1 - c)))

    def start():
        for send, _ in copies():
            send.start()

    def wait():
        for send, recv in copies():
            recv.wait_recv()
            send.wait_send()

    return start, wait


def _gather_weights(slabs, name, ici=True):
    n = len(slabs)

    def body(*refs):
        out = refs[n:2 * n]
        sems = refs[2 * n:]
        if ici:
            start, wait = _gather_ici(out, sems[2], sems[3])
            start()
            wait()
        start, wait = _gather_d2d(out, sems[0], sems[1])
        start()
        wait()

    sem = pltpu.SemaphoreType.DMA((n, 3))
    return _pcall(
        body, name=name,
        out_shape=[_sds(s.shape, BF16) for s in slabs],
        in_specs=[_ANY] * n, out_specs=[_ANY] * n,
        input_output_aliases={a: a for a in range(n)},
        scratch_shapes=[sem, sem] + ([sem, sem] if ici else []),
        compiler_params=_cparams(),
    )(*slabs)


def _rs_halves(grads, out, send_sems, recv_sems):
    def copies():
        x, y, c = _position()
        for a in range(len(grads)):
            kh = grads[a].shape[1] // 2
            yield _remote(grads[a].at[:, _half_rows(1 - c, kh), :], out[a], send_sems.at[a], recv_sems.at[a],
                          (x, y, 1 - c))

    def start():
        for cp in copies():
            cp.start()

    def wait():
        for cp in copies():
            cp.wait()

    return start, wait


def _rs_halves_shapes(grads):
    return [_sds((N_CHIP, g.shape[1] // 2, g.shape[2]), F32) for g in grads]


def _rs_exchange_halves(grads, name):
    n = len(grads)

    def body(*refs):
        start, wait = _rs_halves(refs[:n], refs[n:2 * n], *refs[2 * n:])
        start()
        wait()

    return _pcall(
        body, name=name,
        out_shape=_rs_halves_shapes(grads),
        in_specs=[_ANY] * n, out_specs=[_ANY] * n,
        scratch_shapes=[pltpu.SemaphoreType.DMA((n,)), pltpu.SemaphoreType.DMA((n,))],
        compiler_params=_cparams(),
    )(*grads)


def _rs_chips(parts, out, send_sems, recv_sems):
    def copies():
        x, y, c = _position()
        for a in range(len(parts)):
            for j, (px, py) in enumerate(_other_chips(x, y)):
                got = out[a].at[j]
                yield (_remote(parts[a].at[2 * px + py], got, send_sems.at[a, j], recv_sems.at[a, j], (px, py, c)),
                       _remote(got, got, send_sems.at[a, j], recv_sems.at[a, j], (px, py, c)))

    def start():
        for send, _ in copies():
            send.start()

    def wait():
        for send, recv in copies():
            recv.wait_recv()
            send.wait_send()

    return start, wait


def _rs_chips_shapes(parts):
    return [_sds((N_CHIP - 1,) + p.shape[1:], BF16) for p in parts]


def _rs_join_halves(fulls, name):
    n = len(fulls)

    def body(*refs):
        out = refs[n:2 * n]
        send_sems, recv_sems = refs[2 * n:]
        x, y, c = _position()
        copies = []
        for a in range(n):
            kh = out[a].shape[0] // 2
            mine = out[a].at[_half_rows(c, kh), :]
            cp = _remote(mine, mine, send_sems.at[a], recv_sems.at[a], (x, y, 1 - c))
            cp.start()
            copies.append(cp)
        for a in range(n):
            kh = out[a].shape[0] // 2
            theirs = out[a].at[_half_rows(1 - c, kh), :]
            _remote(theirs, theirs, send_sems.at[a], recv_sems.at[a], (x, y, 1 - c)).wait_recv()
        for cp in copies:
            cp.wait_send()

    return _pcall(
        body, name=name,
        out_shape=[_sds(f.shape, F32) for f in fulls],
        in_specs=[_ANY] * n, out_specs=[_ANY] * n,
        input_output_aliases={a: a for a in range(n)},
        scratch_shapes=[pltpu.SemaphoreType.DMA((n,)), pltpu.SemaphoreType.DMA((n,))],
        compiler_params=_cparams(),
    )(*fulls)


_SMALL_ORDER = ("rel_bias", "ada_b", "norm_mix", "norm_ffn", "attn_q_gain", "attn_k_gain", "hgrn_gnorm",
                "hgrn_lower_bounds")
_WEIGHT_ORDER = ("rel_bias", "ada_w", "ada_b", "norm_mix", "norm_ffn", "attn_w_qkv", "attn_w_out", "attn_q_gain",
                 "attn_k_gain", "hgrn_w_in", "hgrn_w_out", "hgrn_gnorm", "hgrn_lower_bounds", "ffn_w1", "ffn_w3",
                 "ffn_w2")


def _qkv_group_map(t):
    return t // 4, t % 4


def _qkv_chip_map(t):
    return t // 9, t % 9


def _hin_map(t):
    return t // 2, t % 2


def _block_map(t):
    return t, 0


def _pack_rows(parts):
    return jnp.concatenate([p.reshape(-1, 128) for p in parts], axis=0)


def kernel(x, c, rel_bias, ada_w, ada_b, norm_mix, norm_ffn, attn_w_qkv, attn_w_out, attn_q_gain, attn_k_gain, hgrn_w_in, hgrn_w_out, hgrn_gnorm, hgrn_lower_bounds, ffn_w1, ffn_w3, ffn_w2, loss_target, m_rel_bias, m_ada_w, m_ada_b, m_norm_mix, m_norm_ffn, m_attn_w_qkv, m_attn_w_out, m_attn_q_gain, m_attn_k_gain, m_hgrn_w_in, m_hgrn_w_out, m_hgrn_gnorm, m_hgrn_lower_bounds, m_ffn_w1, m_ffn_w3, m_ffn_w2, v_rel_bias, v_ada_w, v_ada_b, v_norm_mix, v_norm_ffn, v_attn_w_qkv, v_attn_w_out, v_attn_q_gain, v_attn_k_gain, v_hgrn_w_in, v_hgrn_w_out, v_hgrn_gnorm, v_hgrn_lower_bounds, v_ffn_w1, v_ffn_w3, v_ffn_w2):
    weights = dict(rel_bias=rel_bias, ada_w=ada_w, ada_b=ada_b, norm_mix=norm_mix, norm_ffn=norm_ffn,
                   attn_w_qkv=attn_w_qkv, attn_w_out=attn_w_out, attn_q_gain=attn_q_gain, attn_k_gain=attn_k_gain,
                   hgrn_w_in=hgrn_w_in, hgrn_w_out=hgrn_w_out, hgrn_gnorm=hgrn_gnorm,
                   hgrn_lower_bounds=hgrn_lower_bounds, ffn_w1=ffn_w1, ffn_w3=ffn_w3, ffn_w2=ffn_w2)
    mom1 = dict(rel_bias=m_rel_bias, ada_w=m_ada_w, ada_b=m_ada_b, norm_mix=m_norm_mix, norm_ffn=m_norm_ffn,
                attn_w_qkv=m_attn_w_qkv, attn_w_out=m_attn_w_out, attn_q_gain=m_attn_q_gain,
                attn_k_gain=m_attn_k_gain, hgrn_w_in=m_hgrn_w_in, hgrn_w_out=m_hgrn_w_out, hgrn_gnorm=m_hgrn_gnorm,
                hgrn_lower_bounds=m_hgrn_lower_bounds, ffn_w1=m_ffn_w1, ffn_w3=m_ffn_w3, ffn_w2=m_ffn_w2)
    mom2 = dict(rel_bias=v_rel_bias, ada_w=v_ada_w, ada_b=v_ada_b, norm_mix=v_norm_mix, norm_ffn=v_norm_ffn,
                attn_w_qkv=v_attn_w_qkv, attn_w_out=v_attn_w_out, attn_q_gain=v_attn_q_gain,
                attn_k_gain=v_attn_k_gain, hgrn_w_in=v_hgrn_w_in, hgrn_w_out=v_hgrn_w_out, hgrn_gnorm=v_hgrn_gnorm,
                hgrn_lower_bounds=v_hgrn_lower_bounds, ffn_w1=v_ffn_w1, ffn_w3=v_ffn_w3, ffn_w2=v_ffn_w2)

    transposed = ("ffn_w1", "ffn_w3")
    for group in (weights, mom1, mom2):
        for k in transposed:
            group[k] = jnp.transpose(group[k], (0, 2, 1))

    xi, yi, ci = _position()
    chip = 2 * xi + yi
    dev = 4 * xi + 2 * yi + ci
    place = jnp.stack([ci, chip]).astype(jnp.int32)
    d = D_MODEL

    big_names = ("attn_w_qkv", "attn_w_out", "hgrn_w_in", "hgrn_w_out", "ffn_w1", "ffn_w3", "ffn_w2")
    early_names, late_names = big_names[:2], big_names[2:]
    slabs16 = {k: _cast_bf16(place, weights[k], "cast_" + k) for k in big_names}
    wg = dict(zip(early_names, _gather_weights([slabs16[k] for k in early_names], "gather_early")))

    c_all = _small_allgather(c.reshape(8, 128), "gather_c").reshape(N_DEV, d)
    ada_b_cols = lax.dynamic_slice(ada_b, (0, chip * ADA_SHARD), (DEPTH, ADA_SHARD)).reshape(DEPTH, 1, ADA_SHARD)
    mod_shard = _ada_fwd(c_all, ada_w, ada_b_cols, "ada_fwd")
    mod_all = _small_allgather(mod_shard.reshape(-1, 128), "gather_mod").reshape(N_DEV, DEPTH, N_DEV, ADA_SHARD)
    mod_mine = lax.dynamic_index_in_dim(mod_all[0::2], dev, axis=2, keepdims=False)
    mod = jnp.transpose(mod_mine, (1, 0, 2)).reshape(DEPTH, 6 * d)

    def mods(layer):
        return [mod[layer:layer + 1, j * d:(j + 1) * d] for j in range(6)]

    x0 = x.reshape(SEQ, d)
    target = loss_target.reshape(SEQ, d)
    qg = attn_q_gain.reshape(len(GROUPS), 1, HEAD_DIM)
    kg = attn_k_gain.reshape(len(GROUPS), 1, HEAD_DIM)
    bias = _attn_bias(rel_bias, "attn_bias")
    lb1 = _lower_bounds(hgrn_lower_bounds, "lower_bounds")[1:2]

    def ffn_fwd(layer, x_in, sc2, sh2, g2):
        hf = _norm_mod(x_in, norm_ffn[layer:layer + 1], sc2, sh2, f"l{layer}_norm_ffn")
        a1, a3, u = _ffn_up(hf, wg["ffn_w1"], wg["ffn_w3"], layer, f"l{layer}_ffn_up")
        z, x_out = _mm_rows(u, wg["ffn_w2"], layer, x_in, g2, f"l{layer}_ffn_down")
        return x_out, (hf, a1, a3, u, z)

    def ffn_bwd(layer, dx_out, x_in, sc2, sh2, g2, saved, halves=()):
        hf, a1, a3, u, z = saved
        dz, dg2 = _gate_bwd(dx_out, z, g2, f"l{layer}_ffn_gate_bwd")
        da1, da3, *recv = _ffn_down_bwd(dz, wg["ffn_w2"], layer, a1, a3, f"l{layer}_ffn_down_bwd", halves=halves)
        dw2 = _mm_rows_bwd_w(u, dz, f"l{layer}_dw2")
        dh = _ffn_up_bwd(da1, da3, wg["ffn_w1"], wg["ffn_w3"], layer, f"l{layer}_ffn_up_bwd")
        dw1 = _mm_rows_bwd_w(da1, hf, f"l{layer}_dw1")
        dw3 = _mm_rows_bwd_w(da3, hf, f"l{layer}_dw3")
        dx_in, dsc2, dsh2, dnf = _norm_mod_bwd(x_in, norm_ffn[layer:layer + 1], sc2, sh2, dh, dx_out,
                                               f"l{layer}_norm_ffn_bwd")
        return dx_in, (dw1, dw3, dw2), (dsh2, dsc2, dg2), dnf, recv

    def rs_add(tags, grads_in, recv):
        return [_rs_add_cast(place, g, r, f"rs_add_{k}_{layer}") for (k, layer), g, r in zip(tags, grads_in, recv)]

    sh1_0, sc1_0, g1_0, sh2_0, sc2_0, g2_0 = mods(0)
    h0 = _norm_mod(x0, norm_mix[0:1], sc1_0, sh1_0, "l0_norm_mix")
    w_qkv9 = _retile_cols(wg["attn_w_qkv"].reshape(N_CHIP, d, 2304), n_out=9, width_out=d, tn=256,
                          src_map=_qkv_chip_map, dst_map=_qkv_group_map, n_tiles=36,
                          name="regroup_w_qkv").reshape(9, 1, d, d)
    qkv9 = _mm_cols(h0, w_qkv9, 0, n_blocks=9, width=d, tn=d, act_map=_block_map, w_map=_block_map,
                    out_dtype=F32, name="l0_qkv")
    o4, lse, *late = _attn_fwd(qkv9, qg, kg, bias, "l0_attn", gather=[slabs16[k] for k in late_names])
    wg.update(zip(late_names, _gather_weights(late, "gather_late_siblings", ici=False)))
    y0, x1 = _mm_rows(o4, wg["attn_w_out"], 0, x0, g1_0, "l0_attn_out")
    x2, ffn0 = ffn_fwd(0, x1, sc2_0, sh2_0, g2_0)

    sh1_1, sc1_1, g1_1, sh2_1, sc2_1, g2_1 = mods(1)
    h1 = _norm_mod(x2, norm_mix[1:2], sc1_1, sh1_1, "l1_norm_mix")
    proj4 = _mm_cols(h1, wg["hgrn_w_in"], 0, n_blocks=4, width=d, tn=512, act_map=_hin_map, w_map=_hin_map,
                     out_dtype=F32, name="l1_hgrn_in")
    o_raw, yg4, states = _hgrn_fwd(proj4, lb1, hgrn_gnorm, "l1_hgrn")
    y1, x3 = _mm_rows(yg4, wg["hgrn_w_out"], 0, x2, g1_1, "l1_hgrn_out")
    x4, ffn1 = ffn_fwd(1, x3, sc2_1, sh2_1, g2_1)

    dx4, loss_part = _loss_head(x4, target, "loss_head")
    loss = lax.psum(loss_part[0, 0], ("x", "y", "c"))

    dx3, (dw1_1, dw3_1, dw2_1), dmod2_1, dnf_1, _ = ffn_bwd(1, dx4, x3, sc2_1, sh2_1, g2_1, ffn1)
    dzm1, dg1_1 = _gate_bwd(dx3, y1, g1_1, "l1_mix_gate_bwd")
    dyg4 = _mm_rows_bwd_a(dzm1, wg["hgrn_w_out"], 0, "l1_hgrn_out_bwd")
    dw_hout = _mm_rows_bwd_w(yg4, dzm1, "l1_dw_hgrn_out")
    dproj4, dlb_h, dgn_h = _hgrn_bwd(proj4, lb1, hgrn_gnorm, o_raw, dyg4, states, "l1_hgrn_bwd")
    dh1 = _mm_cols_bwd_a([(dproj4, wg["hgrn_w_in"], 0)], tn=512, act_map=_hin_map, w_map=_hin_map, n_tiles=8,
                         name="l1_hgrn_in_bwd")
    dw_hin = _mm_cols_bwd_w(h1, dproj4, ns=d, tn=512, act_map=_hin_map, w_map=_hin_map, n_tiles=8,
                            name="l1_dw_hgrn_in")
    dx2, dsc1_1, dsh1_1, dnm_1 = _norm_mod_bwd(x2, norm_mix[1:2], sc1_1, sh1_1, dh1, dx3, "l1_norm_mix_bwd")

    tags_1 = [("hgrn_w_in", 0), ("hgrn_w_out", 0), ("ffn_w1", 1), ("ffn_w3", 1), ("ffn_w2", 1)]
    grads_1 = [dw_hin, dw_hout, dw1_1, dw3_1, dw2_1]
    dx1, (dw1_0, dw3_0, dw2_0), dmod2_0, dnf_0, recv_1 = ffn_bwd(0, dx2, x1, sc2_0, sh2_0, g2_0, ffn0,
                                                                halves=grads_1)
    tags_0 = [("ffn_w1", 0), ("ffn_w3", 0), ("ffn_w2", 0)]
    grads_0 = [dw1_0, dw3_0, dw2_0]
    dzm0, dg1_0 = _gate_bwd(dx1, y0, g1_0, "l0_mix_gate_bwd")
    do4, *recv_0 = _mm_rows_bwd_a(dzm0, wg["attn_w_out"], 0, "l0_attn_out_bwd", halves=grads_0)
    dw_aout = _mm_rows_bwd_w(o4, dzm0, "l0_dw_attn_out")
    tags_a = tags_1 + tags_0
    parts_a = rs_add(tags_1, grads_1, recv_1) + rs_add(tags_0, grads_0, recv_0)
    dqkv, dqg_h, dkg_h, dbias, *got_a = _attn_bwd(qkv9, qg, kg, bias, do4, o4, lse, "l0_attn_bwd", scatter=parts_a)
    dqkv9 = dqkv.reshape(9, SEQ, d)
    dw_qkv9 = _mm_cols_bwd_w(h0, dqkv9, ns=d, tn=d, act_map=_block_map, w_map=_block_map, n_tiles=9,
                             name="l0_dw_qkv", tm=512, n_out=9)
    dw_qkv = _retile_cols(dw_qkv9, n_out=N_CHIP, width_out=2304, tn=256, src_map=_qkv_group_map,
                          dst_map=_qkv_chip_map, n_tiles=36, name="regroup_dw_qkv")
    tags_b = [("attn_w_qkv", 0), ("attn_w_out", 0)]
    grads_b = [dw_qkv, dw_aout]
    parts_b = rs_add(tags_b, grads_b, _rs_exchange_halves(grads_b, "rs_exchange_halves_b"))
    dh0, *got_b = _mm_cols_bwd_a([(dqkv9, w_qkv9, 0)], tn=d, act_map=_block_map, w_map=_block_map, n_tiles=9,
                                 name="l0_qkv_bwd", scatter=parts_b)
    dx0, dsc1_0, dsh1_0, dnm_0 = _norm_mod_bwd(x0, norm_mix[0:1], sc1_0, sh1_0, dh0, dx1, "l0_norm_mix_bwd")
    drb8 = _relbias_bwd(dbias, jnp.asarray(_bias_tables()), "rel_bias_bwd")

    small = _pack_rows([
        dsh1_0, dsc1_0, dg1_0, *dmod2_0, dsh1_1, dsc1_1, dg1_1, *dmod2_1,
        dnm_0, dnm_1, dnf_0, dnf_1,
        jnp.transpose(dqg_h, (1, 0, 2, 3)), jnp.transpose(dkg_h, (1, 0, 2, 3)), dgn_h, dlb_h, drb8])
    small_all = _small_allgather(small, "gather_small")
    main, gains, dlbnd, rbt = _small_totals(small_all, hgrn_lower_bounds.reshape(DEPTH, 8, 128), "small_totals")
    ng = len(GROUPS)
    grads = {
        "ada_b": main[_R_DMOD:_R_NMIX].reshape(DEPTH, 6 * d),
        "norm_mix": main[_R_NMIX:_R_NFFN].reshape(DEPTH, d),
        "norm_ffn": main[_R_NFFN:_R_QG].reshape(DEPTH, d),
        "attn_q_gain": gains[0:ng].reshape(1, ng, HEAD_DIM),
        "attn_k_gain": gains[ng:2 * ng].reshape(1, ng, HEAD_DIM),
        "hgrn_gnorm": gains[2 * ng:2 * ng + 1],
        "hgrn_lower_bounds": dlbnd.reshape(DEPTH, d),
        "rel_bias": jnp.transpose(rbt[:, :ng * NUM_BUCKETS].reshape(HEADS, ng, NUM_BUCKETS), (2, 1, 0))
                       .reshape(NUM_BUCKETS, ng * HEADS),
    }
    dmod_all = small_all[:, _R_DMOD:_R_NMIX].reshape(N_DEV, DEPTH, 6 * d)
    dmod_cols = jnp.transpose(lax.dynamic_slice(dmod_all, (0, 0, chip * ADA_SHARD), (N_DEV, DEPTH, ADA_SHARD)),
                              (1, 0, 2))
    grad_ada_w = _ada_bwd(c_all, dmod_cols, "ada_bwd")

    tags = tags_a + tags_b
    halves = [_rs_sum4(place, p, r, f"rs_sum_{k}_{layer}")
              for (k, layer), p, r in zip(tags, parts_a + parts_b, list(got_a) + list(got_b))]
    full = dict(zip(tags, _rs_join_halves(halves, "rs_join_halves")))

    out_g, out_d, out_m, out_v = {}, {}, {}, {}
    for k in big_names:
        gs = [full[(k, layer)] for layer in range(weights[k].shape[0])]
        out_g[k], out_d[k], out_m[k], out_v[k] = _adamw(weights[k], gs, mom1[k], mom2[k], "adamw_" + k)
    shp = (1, DEPTH * d, ADA_SHARD)
    res = _adamw(ada_w.reshape(shp), [grad_ada_w.reshape(shp[1:])], m_ada_w.reshape(shp), v_ada_w.reshape(shp),
                 "adamw_ada_w")
    out_g["ada_w"], out_d["ada_w"], out_m["ada_w"], out_v["ada_w"] = [r.reshape(ada_w.shape) for r in res]
    packed = [_pack_rows([src[k] for k in _SMALL_ORDER])[None] for src in (weights, grads, mom1, mom2)]
    res = _adamw(packed[0], [packed[1][0]], packed[2], packed[3], "adamw_small")
    offset = 0
    for k in _SMALL_ORDER:
        size = weights[k].size
        for dst, r in zip((out_g, out_d, out_m, out_v), res):
            dst[k] = r.reshape(-1)[offset:offset + size].reshape(weights[k].shape)
        offset += size
    for dst in (out_g, out_d, out_m, out_v):
        for k in transposed:
            dst[k] = jnp.transpose(dst[k], (0, 2, 1))

    return (loss, dx0.reshape(x.shape), *[out_g[k] for k in _WEIGHT_ORDER], *[out_d[k] for k in _WEIGHT_ORDER],
            *[out_m[k] for k in _WEIGHT_ORDER], *[out_v[k] for k in _WEIGHT_ORDER])
```

```python
import functools

import numpy as np
import jax
import jax.numpy as jnp
from jax import lax
from jax.experimental import pallas as pl
from jax.experimental.pallas import tpu as pltpu

F32 = jnp.float32
BF16 = jnp.bfloat16

D_MODEL = 1024
SEQ = 4096
N_DEV = 8
N_CHIP = 4
DEPTH = 2
HEADS = 8
HEAD_DIM = 128
GROUPS = ((128, 1), (512, 4), (2048, 16))
ATT_BLK = 128
ATT_WAYS = 4
ATT_STEPS = SEQ // ATT_BLK // ATT_WAYS
NUM_BUCKETS = 32
MAX_DISTANCE = 2048
FFN_HIDDEN = 2816
FFN_SHARD = FFN_HIDDEN // N_CHIP
HG_SUB = 16
HG_TC = 512
HG_HP = 4
RMS_EPS = 1e-6
NEG = -1e30
ATT_SCALE = HEAD_DIM ** -0.5
ADAM_LR, ADAM_B1, ADAM_B2, ADAM_EPS, ADAM_WD, ADAM_STEP = 0.001, 0.9, 0.999, 1e-08, 0.01, 10
VMEM_LIMIT = 56 * 1024 * 1024
MESH = pl.DeviceIdType.MESH


def _pcall(body, **kw):
    return pl.pallas_call(body, **kw)


def _cparams(sem=None):
    if sem is None:
        return pltpu.CompilerParams(vmem_limit_bytes=VMEM_LIMIT)
    return pltpu.CompilerParams(dimension_semantics=sem, vmem_limit_bytes=VMEM_LIMIT)


def _sds(shape, dtype):
    return jax.ShapeDtypeStruct(shape, dtype)


def _dot(a, b):
    return jnp.dot(a, b, preferred_element_type=F32)


def _dot_nt(a, b):
    return lax.dot_general(a, b, (((1,), (1,)), ((), ())), preferred_element_type=F32)


def _dot_tn(a, b):
    return lax.dot_general(a, b, (((0,), (0,)), ((), ())), preferred_element_type=F32)


def _sigmoid(x):
    return 1.0 / (1.0 + jnp.exp(-x))


def _silu(x):
    return x * _sigmoid(x)


def _dsilu(x):
    s = _sigmoid(x)
    return s * (1.0 + x * (1.0 - s))


def _norm_mod(x, gain, sc, sh, name):
    tm = 512

    def body(x_ref, g_ref, sc_ref, sh_ref, h_ref):
        xv = x_ref[...]
        rs = lax.rsqrt(jnp.mean(xv * xv, axis=-1, keepdims=True) + RMS_EPS)
        h_ref[...] = ((xv * rs * g_ref[...]) * (1.0 + sc_ref[...]) + sh_ref[...]).astype(BF16)

    vec = pl.BlockSpec((1, D_MODEL), lambda i: (0, 0))
    return _pcall(
        body, name=name, grid=(SEQ // tm,),
        in_specs=[pl.BlockSpec((tm, D_MODEL), lambda i: (i, 0)), vec, vec, vec],
        out_specs=pl.BlockSpec((tm, D_MODEL), lambda i: (i, 0)),
        out_shape=_sds((SEQ, D_MODEL), BF16),
        compiler_params=_cparams(("parallel",)),
    )(x, gain, sc, sh)


def _norm_mod_bwd(x, gain, sc, sh, dh, dres, name):
    tm = 512

    def body(x_ref, g_ref, sc_ref, sh_ref, dh_ref, dres_ref, dx_ref, dsc_ref, dsh_ref, dg_ref):
        @pl.when(pl.program_id(0) == 0)
        def _():
            dsc_ref[...] = jnp.zeros_like(dsc_ref)
            dsh_ref[...] = jnp.zeros_like(dsh_ref)
            dg_ref[...] = jnp.zeros_like(dg_ref)

        xv = x_ref[...]
        dhv = dh_ref[...]
        rs = lax.rsqrt(jnp.mean(xv * xv, axis=-1, keepdims=True) + RMS_EPS)
        xh = xv * rs
        dsc_ref[...] += jnp.sum(dhv * (xh * g_ref[...]), axis=0, keepdims=True)
        dsh_ref[...] += jnp.sum(dhv, axis=0, keepdims=True)
        dhn = dhv * (1.0 + sc_ref[...])
        dg_ref[...] += jnp.sum(dhn * xh, axis=0, keepdims=True)
        dxh = dhn * g_ref[...]
        dx_ref[...] = dres_ref[...] + rs * (dxh - xh * jnp.mean(dxh * xh, axis=-1, keepdims=True))

    vec = pl.BlockSpec((1, D_MODEL), lambda i: (0, 0))
    big = pl.BlockSpec((tm, D_MODEL), lambda i: (i, 0))
    return _pcall(
        body, name=name, grid=(SEQ // tm,),
        in_specs=[big, vec, vec, vec, big, big],
        out_specs=[big, vec, vec, vec],
        out_shape=[_sds((SEQ, D_MODEL), F32)] + [_sds((1, D_MODEL), F32)] * 3,
        compiler_params=_cparams(("arbitrary",)),
    )(x, gain, sc, sh, dh, dres)


def _mm_cols(a, wg, layer, *, n_blocks, width, tn, act_map, w_map, out_dtype, name, tm=1024):
    k = a.shape[1]
    n_tiles = n_blocks * width // tn

    def body(a_ref, w_ref, o_ref):
        o_ref[...] = _dot(a_ref[...], w_ref[...]).astype(o_ref.dtype)

    return _pcall(
        body, name=name, grid=(SEQ // tm, n_tiles),
        in_specs=[pl.BlockSpec((tm, k), lambda i, t: (i, 0)),
                  pl.BlockSpec((None, None, k, tn), lambda i, t: (w_map(t)[0], layer, 0, w_map(t)[1]))],
        out_specs=pl.BlockSpec((None, tm, tn), lambda i, t: (act_map(t)[0], i, act_map(t)[1])),
        out_shape=_sds((n_blocks, SEQ, width), out_dtype),
        compiler_params=_cparams(("parallel", "arbitrary")),
    )(a, wg)


def _mm_cols_bwd_a(pairs, *, tn, act_map, w_map, n_tiles, name, tm=1024, scatter=()):
    k = pairs[0][1].shape[2]
    n_p = len(pairs)
    n_s = len(scatter)
    n_rows = SEQ // tm

    def body(*refs):
        o_ref = refs[2 * n_p + n_s]
        if n_s:
            comm_start, comm_wait = _rs_chips(refs[2 * n_p:2 * n_p + n_s], refs[2 * n_p + n_s + 1:2 * n_p + 2 * n_s + 1],
                                              *refs[2 * n_p + 2 * n_s + 1:])
            pl.when((pl.program_id(0) == 0) & (pl.program_id(1) == 0))(comm_start)

        @pl.when(pl.program_id(1) == 0)
        def _():
            o_ref[...] = jnp.zeros_like(o_ref)

        acc = _dot_nt(refs[0][...], refs[1][...])
        for p in range(1, n_p):
            acc += _dot_nt(refs[2 * p][...], refs[2 * p + 1][...])
        o_ref[...] += acc
        if n_s:
            pl.when((pl.program_id(0) == n_rows - 1) & (pl.program_id(1) == n_tiles - 1))(comm_wait)

    in_specs, args = [], []
    for dout, wg, layer in pairs:
        in_specs.append(pl.BlockSpec((None, tm, tn), lambda i, t: (act_map(t)[0], i, act_map(t)[1])))
        in_specs.append(pl.BlockSpec((None, None, k, tn),
                                     lambda i, t, layer=layer: (w_map(t)[0], layer, 0, w_map(t)[1])))
        args += [dout, wg]
    sem = pltpu.SemaphoreType.DMA((max(n_s, 1), 3))
    res = _pcall(
        body, name=name, grid=(n_rows, n_tiles),
        in_specs=in_specs + [_ANY] * n_s,
        out_specs=[pl.BlockSpec((tm, k), lambda i, t: (i, 0))] + [_ANY] * n_s,
        out_shape=[_sds((SEQ, k), F32)] + _rs_chips_shapes(scatter),
        scratch_shapes=[sem, sem] if n_s else [],
        compiler_params=_cparams(("arbitrary", "arbitrary") if n_s else ("parallel", "arbitrary")),
    )(*args, *scatter)
    return res if n_s else res[0]


def _mm_cols_bwd_w(a, dout, *, ns, tn, act_map, w_map, n_tiles, name, tm=1024, n_out=N_CHIP):
    k = a.shape[1]

    def body(a_ref, d_ref, o_ref):
        @pl.when(pl.program_id(1) == 0)
        def _():
            o_ref[...] = jnp.zeros_like(o_ref)

        o_ref[...] += _dot_tn(a_ref[...], d_ref[...])

    return _pcall(
        body, name=name, grid=(n_tiles, SEQ // tm),
        in_specs=[pl.BlockSpec((tm, k), lambda t, i: (i, 0)),
                  pl.BlockSpec((None, tm, tn), lambda t, i: (act_map(t)[0], i, act_map(t)[1]))],
        out_specs=pl.BlockSpec((None, k, tn), lambda t, i: (w_map(t)[0], 0, w_map(t)[1])),
        out_shape=_sds((n_out, k, ns), F32),
        compiler_params=_cparams(("parallel", "arbitrary")),
    )(a, dout)


def _retile_cols(src, *, n_out, width_out, tn, src_map, dst_map, n_tiles, name):
    k = src.shape[1]

    def body(s_ref, o_ref):
        o_ref[...] = s_ref[...]

    return _pcall(
        body, name=name, grid=(n_tiles,),
        in_specs=[pl.BlockSpec((None, k, tn), lambda t: (src_map(t)[0], 0, src_map(t)[1]))],
        out_specs=pl.BlockSpec((None, k, tn), lambda t: (dst_map(t)[0], 0, dst_map(t)[1])),
        out_shape=_sds((n_out, k, width_out), src.dtype),
        compiler_params=_cparams(("parallel",)),
    )(src)


def _mm_rows(a4, wg, layer, x, gate, name, tm=512):
    ks = a4.shape[2]
    n = wg.shape[3]

    def body(a_ref, w_ref, x_ref, g_ref, z_ref, xn_ref):
        z = _dot(a_ref[0], w_ref[0])
        for s in range(1, N_CHIP):
            z += _dot(a_ref[s], w_ref[s])
        z_ref[...] = z
        xn_ref[...] = x_ref[...] + g_ref[...] * z

    big = pl.BlockSpec((tm, n), lambda i: (i, 0))
    return _pcall(
        body, name=name, grid=(SEQ // tm,),
        in_specs=[pl.BlockSpec((N_CHIP, tm, ks), lambda i: (0, i, 0)),
                  pl.BlockSpec((N_CHIP, None, ks, n), lambda i: (0, layer, 0, 0)),
                  big, pl.BlockSpec((1, n), lambda i: (0, 0))],
        out_specs=[big, big],
        out_shape=[_sds((SEQ, n), F32), _sds((SEQ, n), F32)],
        compiler_params=_cparams(("parallel",)),
    )(a4, wg, x, gate)


def _gate_bwd(dx, z, gate, name):
    tm = 512

    def body(dx_ref, z_ref, g_ref, dz_ref, dg_ref):
        @pl.when(pl.program_id(0) == 0)
        def _():
            dg_ref[...] = jnp.zeros_like(dg_ref)

        dxv = dx_ref[...]
        dz_ref[...] = (dxv * g_ref[...]).astype(BF16)
        dg_ref[...] += jnp.sum(dxv * z_ref[...], axis=0, keepdims=True)

    big = pl.BlockSpec((tm, D_MODEL), lambda i: (i, 0))
    vec = pl.BlockSpec((1, D_MODEL), lambda i: (0, 0))
    return _pcall(
        body, name=name, grid=(SEQ // tm,),
        in_specs=[big, big, vec], out_specs=[big, vec],
        out_shape=[_sds((SEQ, D_MODEL), BF16), _sds((1, D_MODEL), F32)],
        compiler_params=_cparams(("arbitrary",)),
    )(dx, z, gate)


def _mm_rows_bwd_a(dz, wg, layer, name, tm=1024, halves=()):
    ks, n = wg.shape[2], wg.shape[3]
    n_h = len(halves)
    n_rows = SEQ // tm

    def body(*refs):
        dz_ref, w_ref = refs[:2]
        o_ref = refs[2 + n_h]
        if n_h:
            comm_start, comm_wait = _rs_halves(refs[2:2 + n_h], refs[3 + n_h:3 + 2 * n_h], *refs[3 + 2 * n_h:])
            pl.when((pl.program_id(0) == 0) & (pl.program_id(1) == 0))(comm_start)
        o_ref[...] = _dot_nt(dz_ref[...], w_ref[...])
        if n_h:
            pl.when((pl.program_id(0) == n_rows - 1) & (pl.program_id(1) == N_CHIP - 1))(comm_wait)

    sem = pltpu.SemaphoreType.DMA((max(n_h, 1),))
    res = _pcall(
        body, name=name, grid=(n_rows, N_CHIP),
        in_specs=[pl.BlockSpec((tm, n), lambda i, s: (i, 0)),
                  pl.BlockSpec((None, None, ks, n), lambda i, s: (s, layer, 0, 0))] + [_ANY] * n_h,
        out_specs=[pl.BlockSpec((None, tm, ks), lambda i, s: (s, i, 0))] + [_ANY] * n_h,
        out_shape=[_sds((N_CHIP, SEQ, ks), F32)] + _rs_halves_shapes(halves),
        scratch_shapes=[sem, sem] if n_h else [],
        compiler_params=_cparams(("arbitrary", "arbitrary") if n_h else ("parallel", "arbitrary")),
    )(dz, wg, *halves)
    return res if n_h else res[0]


def _mm_rows_bwd_w(a4, dz, name, tm=2048):
    ks = a4.shape[2]
    n = dz.shape[1]

    def body(a_ref, dz_ref, o_ref):
        @pl.when(pl.program_id(1) == 0)
        def _():
            o_ref[...] = jnp.zeros_like(o_ref)

        o_ref[...] += _dot_tn(a_ref[...], dz_ref[...])

    return _pcall(
        body, name=name, grid=(N_CHIP, SEQ // tm),
        in_specs=[pl.BlockSpec((None, tm, ks), lambda s, i: (s, i, 0)),
                  pl.BlockSpec((tm, n), lambda s, i: (i, 0))],
        out_specs=pl.BlockSpec((None, ks, n), lambda s, i: (s, 0, 0)),
        out_shape=_sds((N_CHIP, ks, n), F32),
        compiler_params=_cparams(("parallel", "arbitrary")),
    )(a4, dz)


def _ffn_up(h, w1g, w3g, layer, name, tm=1024):
    def body(h_ref, w1_ref, w3_ref, a1_ref, a3_ref, u_ref):
        hv = h_ref[...]
        a1 = _dot_nt(hv, w1_ref[...])
        a3 = _dot_nt(hv, w3_ref[...])
        a1_ref[...] = a1
        a3_ref[...] = a3
        u_ref[...] = (_silu(a1) * a3).astype(BF16)

    wspec = pl.BlockSpec((None, None, FFN_SHARD, D_MODEL), lambda i, s: (s, layer, 0, 0))
    ospec = pl.BlockSpec((None, tm, FFN_SHARD), lambda i, s: (s, i, 0))
    shp = (N_CHIP, SEQ, FFN_SHARD)
    return _pcall(
        body, name=name, grid=(SEQ // tm, N_CHIP),
        in_specs=[pl.BlockSpec((tm, D_MODEL), lambda i, s: (i, 0)), wspec, wspec],
        out_specs=[ospec, ospec, ospec],
        out_shape=[_sds(shp, F32), _sds(shp, F32), _sds(shp, BF16)],
        compiler_params=_cparams(("parallel", "arbitrary")),
    )(h, w1g, w3g)


def _ffn_up_bwd(da1, da3, w1g, w3g, layer, name, tm=512):
    def body(d1_ref, d3_ref, w1_ref, w3_ref, o_ref):
        acc = _dot(d1_ref[0], w1_ref[0]) + _dot(d3_ref[0], w3_ref[0])
        for s in range(1, N_CHIP):
            acc += _dot(d1_ref[s], w1_ref[s]) + _dot(d3_ref[s], w3_ref[s])
        o_ref[...] = acc

    wspec = pl.BlockSpec((N_CHIP, None, FFN_SHARD, D_MODEL), lambda i: (0, layer, 0, 0))
    dspec = pl.BlockSpec((N_CHIP, tm, FFN_SHARD), lambda i: (0, i, 0))
    return _pcall(
        body, name=name, grid=(SEQ // tm,),
        in_specs=[dspec, dspec, wspec, wspec],
        out_specs=pl.BlockSpec((tm, D_MODEL), lambda i: (i, 0)),
        out_shape=_sds((SEQ, D_MODEL), F32),
        compiler_params=_cparams(("parallel",)),
    )(da1, da3, w1g, w3g)


def _ffn_down_bwd(dz, w2g, layer, a1, a3, name, tm=1024, halves=()):
    n_h = len(halves)
    n_rows = SEQ // tm

    def body(*refs):
        dz_ref, w_ref, a1_ref, a3_ref = refs[:4]
        da1_ref, da3_ref = refs[4 + n_h:6 + n_h]
        if n_h:
            comm_start, comm_wait = _rs_halves(refs[4:4 + n_h], refs[6 + n_h:6 + 2 * n_h], *refs[6 + 2 * n_h:])
            pl.when((pl.program_id(0) == 0) & (pl.program_id(1) == 0))(comm_start)
        du = _dot_nt(dz_ref[...], w_ref[...])
        a1 = a1_ref[...]
        da1_ref[...] = (du * a3_ref[...] * _dsilu(a1)).astype(BF16)
        da3_ref[...] = (du * _silu(a1)).astype(BF16)
        if n_h:
            pl.when((pl.program_id(0) == n_rows - 1) & (pl.program_id(1) == N_CHIP - 1))(comm_wait)

    blk = pl.BlockSpec((None, tm, FFN_SHARD), lambda i, s: (s, i, 0))
    shp = (N_CHIP, SEQ, FFN_SHARD)
    sem = pltpu.SemaphoreType.DMA((max(n_h, 1),))
    return _pcall(
        body, name=name, grid=(n_rows, N_CHIP),
        in_specs=[pl.BlockSpec((tm, D_MODEL), lambda i, s: (i, 0)),
                  pl.BlockSpec((None, None, FFN_SHARD, D_MODEL), lambda i, s: (s, layer, 0, 0)),
                  blk, blk] + [_ANY] * n_h,
        out_specs=[blk, blk] + [_ANY] * n_h,
        out_shape=[_sds(shp, BF16), _sds(shp, BF16)] + _rs_halves_shapes(halves),
        scratch_shapes=[sem, sem] if n_h else [],
        compiler_params=_cparams(("arbitrary", "arbitrary") if n_h else ("parallel", "arbitrary")),
    )(dz, w2g, a1, a3, *halves)


def _loss_head(y, target, name):
    tm = 512

    def body(y_ref, t_ref, dy_ref, l_ref, acc_ref):
        @pl.when(pl.program_id(0) == 0)
        def _():
            acc_ref[...] = jnp.zeros_like(acc_ref)

        err = y_ref[...] - t_ref[...]
        dy_ref[...] = err * (1.0 / D_MODEL)
        acc_ref[...] += jnp.sum(jnp.mean(err * err, axis=-1, keepdims=True), axis=0, keepdims=True)

        @pl.when(pl.program_id(0) == pl.num_programs(0) - 1)
        def _():
            l_ref[...] = 0.5 * acc_ref[...]

    big = pl.BlockSpec((tm, D_MODEL), lambda i: (i, 0))
    return _pcall(
        body, name=name, grid=(SEQ // tm,),
        in_specs=[big, big],
        out_specs=[big, pl.BlockSpec((1, 1), lambda i: (0, 0))],
        out_shape=[_sds((SEQ, D_MODEL), F32), _sds((1, 1), F32)],
        scratch_shapes=[pltpu.VMEM((1, 1), F32)],
        compiler_params=_cparams(("arbitrary",)),
    )(y, target)


def _attn_rows(base, d):
    if d == 1:
        return pl.ds(pl.multiple_of(base, ATT_BLK), ATT_BLK)
    return pl.ds(base, ATT_BLK, stride=d)


def _attn_block_index(i, d):
    nb = SEQ // (ATT_BLK * d)
    r = i // nb
    n = i % nb
    base = r + n * (ATT_BLK * d)
    pbase = jnp.maximum(base - ATT_BLK * d, r)
    return n, _attn_rows(base, d), _attn_rows(pbase, d)


def _attn_two_blocks(ref, prow, rows):
    return jnp.concatenate([ref[prow, :].astype(BF16), ref[rows, :].astype(BF16)], axis=0)


def _attn_block_bias(b_ref, n):
    b = b_ref[...]
    prev_half = lax.broadcasted_iota(jnp.int32, b.shape, 1) < ATT_BLK
    return jnp.where(prev_half & (n == 0), NEG, b)


def _qk_normed(x):
    rs = lax.rsqrt(jnp.mean(x * x, axis=-1, keepdims=True) + RMS_EPS)
    return x * rs, rs


def _attn_fwd(qkv9, qgain, kgain, bias, name, gather=()):
    n_g = len(gather)

    def body(*refs):
        q_ref, k_ref, v_ref, qg_ref, kg_ref, b_ref = refs[:6]
        o_ref, lse_ref = refs[6 + n_g:8 + n_g]
        qn_s, kn_s, acc_s, m_s, l_s = refs[8 + 2 * n_g:13 + 2 * n_g]
        g = pl.program_id(1)
        if n_g:
            comm_start, comm_wait = _gather_ici(refs[8 + n_g:8 + 2 * n_g], *refs[13 + 2 * n_g:])
            pl.when((pl.program_id(0) == 0) & (g == 0))(comm_start)

        @pl.when(g == 0)
        def _():
            m_s[...] = jnp.full_like(m_s, NEG)
            l_s[...] = jnp.zeros_like(l_s)
            acc_s[...] = jnp.zeros_like(acc_s)

        qn_s[...] = _qk_normed(q_ref[...])[0] * qg_ref[...]
        kn_s[...] = _qk_normed(k_ref[...])[0] * kg_ref[...]

        for gi, (_, d) in enumerate(GROUPS):
            @pl.when(g == gi)
            def _(d=d):
                def block(n, qb, kk, vv, m_old, l_old, acc_old):
                    s = _dot_nt(qb, kk) * ATT_SCALE + _attn_block_bias(b_ref, n)
                    m_new = jnp.maximum(m_old, jnp.max(s, axis=-1, keepdims=True))
                    alpha = jnp.exp(m_old - m_new)
                    p = jnp.exp(s - m_new)
                    l_new = alpha * l_old + jnp.sum(p, axis=-1, keepdims=True)
                    acc_new = alpha * acc_old + _dot(p.astype(BF16), vv)
                    return m_new, l_new, acc_new

                def it(i, carry):
                    where, loaded = [], []
                    for way in range(ATT_WAYS):
                        n, rows, prow = _attn_block_index(i + way * ATT_STEPS, d)
                        where.append(rows)
                        loaded.append((n, qn_s[rows, :].astype(BF16), _attn_two_blocks(kn_s, prow, rows),
                                       _attn_two_blocks(v_ref, prow, rows), m_s[rows, :], l_s[rows, :],
                                       acc_s[rows, :]))
                    results = [block(*vals) for vals in loaded]
                    for rows, (m_new, l_new, acc_new) in zip(where, results):
                        m_s[rows, :] = m_new
                        l_s[rows, :] = l_new
                        acc_s[rows, :] = acc_new
                    return carry

                lax.fori_loop(0, ATT_STEPS, it, 0)

        @pl.when(g == len(GROUPS) - 1)
        def _():
            o_ref[...] = (acc_s[...] / l_s[...]).astype(BF16)
            lse_ref[...] = m_s[...] + jnp.log(l_s[...])

        if n_g:
            pl.when((pl.program_id(0) == HEADS - 1) & (g == len(GROUPS) - 1))(comm_wait)

    def col(j):
        return pl.BlockSpec((None, SEQ, HEAD_DIM), lambda h, g: (g * 3 + j, 0, h))

    gspec = pl.BlockSpec((None, 1, HEAD_DIM), lambda h, g: (g, 0, 0))
    sem = pltpu.SemaphoreType.DMA((max(n_g, 1), 3))
    return _pcall(
        body, name=name, grid=(HEADS, len(GROUPS)),
        in_specs=[col(0), col(1), col(2), gspec, gspec,
                  pl.BlockSpec((None, None, ATT_BLK, 2 * ATT_BLK), lambda h, g: (g, h, 0, 0))] + [_ANY] * n_g,
        out_specs=[pl.BlockSpec((None, SEQ, HEAD_DIM), lambda h, g: (h // 2, 0, h % 2)),
                   pl.BlockSpec((None, SEQ, 1), lambda h, g: (h, 0, 0))] + [_ANY] * n_g,
        out_shape=[_sds((N_CHIP, SEQ, 2 * HEAD_DIM), BF16), _sds((HEADS, SEQ, 1), F32)]
        + [_sds(s.shape, s.dtype) for s in gather],
        input_output_aliases={6 + a: 2 + a for a in range(n_g)},
        scratch_shapes=[pltpu.VMEM((SEQ, HEAD_DIM), F32)] * 3 + [pltpu.VMEM((SEQ, 1), F32)] * 2
        + ([sem, sem] if n_g else []),
        compiler_params=_cparams(("arbitrary", "arbitrary")),
    )(qkv9, qkv9, qkv9, qgain, kgain, bias, *gather)


def _attn_bwd(qkv9, qgain, kgain, bias, do4, o4, lse, name, scatter=()):
    n_s = len(scatter)

    def body(*refs):
        q_ref, k_ref, v_ref, qg_ref, kg_ref, b_ref, do_ref, o_ref, lse_ref = refs[:9]
        dqkv_ref, dqg_ref, dkg_ref, db_ref = refs[9 + n_s:13 + n_s]
        qn_s, kn_s, dq_s, dk_s, dv_s, dl_s = refs[13 + 2 * n_s:19 + 2 * n_s]
        g = pl.program_id(1)
        if n_s:
            comm_start, comm_wait = _rs_chips(refs[9:9 + n_s], refs[13 + n_s:13 + 2 * n_s], *refs[19 + 2 * n_s:])
            pl.when((pl.program_id(0) == 0) & (g == 0))(comm_start)
        qh, rq = _qk_normed(q_ref[...])
        kh, rk = _qk_normed(k_ref[...])
        qn_s[...] = qh * qg_ref[...]
        kn_s[...] = kh * kg_ref[...]
        @pl.when(g == 0)
        def _():
            dl_s[...] = jnp.sum(do_ref[...] * o_ref[...].astype(F32), axis=-1, keepdims=True)

        dk_s[...] = jnp.zeros_like(dk_s)
        dv_s[...] = jnp.zeros_like(dv_s)
        db_ref[...] = jnp.zeros_like(db_ref)

        for gi, (_, d) in enumerate(GROUPS):
            @pl.when(g == gi)
            def _(d=d):
                def block(n, qb, kk, vv, dob, lse_b, dl):
                    s = _dot_nt(qb, kk) * ATT_SCALE + _attn_block_bias(b_ref, n)
                    p = jnp.exp(s - lse_b)
                    ds = p * (_dot_nt(dob, vv) - dl)
                    ds16 = ds.astype(BF16)
                    return (ds, _dot(ds16, kk) * ATT_SCALE, _dot_tn(ds16, qb) * ATT_SCALE,
                            _dot_tn(p.astype(BF16), dob))

                def it(i, carry):
                    where, loaded, old = [], [], []
                    for way in range(ATT_WAYS):
                        n, rows, prow = _attn_block_index(i + way * ATT_STEPS, d)
                        where.append((rows, prow))
                        loaded.append((n, qn_s[rows, :].astype(BF16), _attn_two_blocks(kn_s, prow, rows),
                                       _attn_two_blocks(v_ref, prow, rows), do_ref[rows, :].astype(BF16),
                                       lse_ref[rows, :], dl_s[rows, :]))
                        old.append((dk_s[rows, :], dk_s[prow, :], dv_s[rows, :], dv_s[prow, :]))
                    results = [block(*vals) for vals in loaded]
                    db_ref[...] += functools.reduce(lambda a, b: a + b, [r[0] for r in results])
                    for (rows, prow), (dk_c, dk_p, dv_c, dv_p), (_, dq, dkk, dvv) in zip(where, old, results):
                        dq_s[rows, :] = dq
                        dk_s[prow, :] = dk_p + dkk[:ATT_BLK]
                        dv_s[prow, :] = dv_p + dvv[:ATT_BLK]
                        dk_s[rows, :] = dk_c + dkk[ATT_BLK:]
                        dv_s[rows, :] = dv_c + dvv[ATT_BLK:]
                    return carry

                lax.fori_loop(0, ATT_STEPS, it, 0)

        def norm_bwd(dn, xh, rs, gain):
            dgain = jnp.sum(dn * xh, axis=0, keepdims=True)
            dxh = dn * gain
            return rs * (dxh - xh * jnp.mean(dxh * xh, axis=-1, keepdims=True)), dgain

        dq, dqg = norm_bwd(dq_s[...], qh, rq, qg_ref[...])
        dk, dkg = norm_bwd(dk_s[...], kh, rk, kg_ref[...])
        dqkv_ref[0] = dq.astype(BF16)
        dqkv_ref[1] = dk.astype(BF16)
        dqkv_ref[2] = dv_s[...].astype(BF16)
        dqg_ref[...] = dqg
        dkg_ref[...] = dkg
        if n_s:
            pl.when((pl.program_id(0) == HEADS - 1) & (g == len(GROUPS) - 1))(comm_wait)

    def col(j):
        return pl.BlockSpec((None, SEQ, HEAD_DIM), lambda h, g: (g * 3 + j, 0, h))

    gspec = pl.BlockSpec((None, 1, HEAD_DIM), lambda h, g: (g, 0, 0))
    bspec = pl.BlockSpec((None, None, ATT_BLK, 2 * ATT_BLK), lambda h, g: (g, h, 0, 0))
    hcol = pl.BlockSpec((None, SEQ, HEAD_DIM), lambda h, g: (h // 2, 0, h % 2))
    dgspec = pl.BlockSpec((None, None, 1, HEAD_DIM), lambda h, g: (h, g, 0, 0))
    ng = len(GROUPS)
    sem = pltpu.SemaphoreType.DMA((max(n_s, 1), 3))
    return _pcall(
        body, name=name, grid=(HEADS, ng),
        in_specs=[col(0), col(1), col(2), gspec, gspec, bspec, hcol, hcol,
                  pl.BlockSpec((None, SEQ, 1), lambda h, g: (h, 0, 0))] + [_ANY] * n_s,
        out_specs=[pl.BlockSpec((None, 3, SEQ, HEAD_DIM), lambda h, g: (g, 0, 0, h)), dgspec, dgspec, bspec]
        + [_ANY] * n_s,
        out_shape=[_sds((ng, 3, SEQ, D_MODEL), BF16), _sds((HEADS, ng, 1, HEAD_DIM), F32),
                   _sds((HEADS, ng, 1, HEAD_DIM), F32), _sds((ng, HEADS, ATT_BLK, 2 * ATT_BLK), F32)]
        + _rs_chips_shapes(scatter),
        scratch_shapes=[pltpu.VMEM((SEQ, HEAD_DIM), F32)] * 5 + [pltpu.VMEM((SEQ, 1), F32)]
        + ([sem, sem] if n_s else []),
        compiler_params=_cparams(("arbitrary", "arbitrary")),
    )(qkv9, qkv9, qkv9, qgain, kgain, bias, do4, o4, lse, *scatter)


def _relbias_bwd(dbias, bucket_idx, name):
    ng = len(GROUPS)

    def body(db_ref, idx_ref, o_ref):
        lane = lax.broadcasted_iota(jnp.int32, (HEADS, 128), 1)
        acc = jnp.zeros((HEADS, 128), F32)
        for g in range(ng):
            dbg = db_ref[g]
            idx = idx_ref[g]
            for b in range(NUM_BUCKETS):
                sel = jnp.where((idx == b)[None], dbg, 0.0)
                part = jnp.sum(sel, axis=1)
                val = jnp.sum(part, axis=-1, keepdims=True)
                acc = jnp.where(lane == g * NUM_BUCKETS + b, val, acc)
        o_ref[...] = acc

    return _pcall(body, name=name, out_shape=_sds((HEADS, 128), F32), compiler_params=_cparams())(dbias, bucket_idx)


def _scan16(x, reverse=False):
    row = lax.broadcasted_iota(jnp.int32, x.shape, 0)
    for sh in (1, 2, 4, 8):
        if reverse:
            x = x + jnp.where(row < HG_SUB - sh, pltpu.roll(x, HG_SUB - sh, 0), 0.0)
        else:
            x = x + jnp.where(row >= sh, pltpu.roll(x, sh, 0), 0.0)
    return x


def _hgrn_gates(qr, fr, lbv):
    q = _silu(qr)
    sig = _sigmoid(fr)
    fg = lbv + (1.0 - lbv) * sig
    lf = jnp.log(fg)
    gcum = _scan16(lf)
    glast = jnp.sum(lf, axis=0, keepdims=True)
    return q, sig, fg, 1.0 - fg, gcum, glast


def _hgrn_intra(q, k, gcum, tri):
    e = jnp.exp(jnp.where(tri, gcum[:, None, :] - gcum[None, :, :], NEG))
    a = jnp.sum(q[:, None, :] * k[None, :, :] * e, axis=-1, keepdims=True)
    return e, a


def _hgrn_fwd(proj4, lb, gain, name):
    nsub = HG_TC // HG_SUB
    wide = HG_HP * HEAD_DIM

    def body(p_ref, lb_ref, gn_ref, o_ref, y_ref, st_ref, state_s):
        @pl.when(pl.program_id(1) == 0)
        def _():
            state_s[...] = jnp.zeros_like(state_s)

        gnv = gn_ref[...]
        shp = (HG_SUB, HG_SUB, HEAD_DIM)
        tri = lax.broadcasted_iota(jnp.int32, shp, 0) >= lax.broadcasted_iota(jnp.int32, shp, 1)

        def head(qr, fr, vv, gr, lbv, st):
            q, _, _, k, gcum, glast = _hgrn_gates(qr, fr, lbv)
            _, a = _hgrn_intra(q, k, gcum, tri)
            o = jnp.sum(a * vv[None, :, :], axis=1) + _dot_nt((q * jnp.exp(gcum)).astype(BF16), st.astype(BF16))
            kg = k * jnp.exp(glast - gcum)
            st_new = st * jnp.exp(glast) + _dot_tn(vv.astype(BF16), kg.astype(BF16))
            rs = lax.rsqrt(jnp.mean(o * o, axis=-1, keepdims=True) + RMS_EPS)
            return o, (o * rs * gnv * _silu(gr)).astype(BF16), st_new

        def it(i, carry):
            rows = pl.ds(pl.multiple_of(i * HG_SUB, HG_SUB), HG_SUB)
            loaded = []
            for hh in range(HG_HP):
                lanes = pl.ds(hh * HEAD_DIM, HEAD_DIM)
                loaded.append(([p_ref[j, rows, lanes] for j in range(4)], lb_ref[:, lanes], state_s[hh]))
            results = [head(blk[0], blk[1], blk[2], blk[3], lbv, st) for blk, lbv, st in loaded]
            for hh, ((_, _, st), (o, y, st_new)) in enumerate(zip(loaded, results)):
                lanes = pl.ds(hh * HEAD_DIM, HEAD_DIM)
                st_ref[hh, i] = st.astype(BF16)
                state_s[hh] = st_new
                o_ref[rows, lanes] = o
                y_ref[hh // 2, rows, pl.ds((hh % 2) * HEAD_DIM, HEAD_DIM)] = y
            return carry

        lax.fori_loop(0, nsub, it, 0)

    return _pcall(
        body, name=name, grid=(HEADS // HG_HP, SEQ // HG_TC),
        in_specs=[pl.BlockSpec((4, HG_TC, wide), lambda h, j: (0, j, h)),
                  pl.BlockSpec((1, wide), lambda h, j: (0, h)),
                  pl.BlockSpec((1, HEAD_DIM), lambda h, j: (0, 0))],
        out_specs=[pl.BlockSpec((HG_TC, wide), lambda h, j: (j, h)),
                   pl.BlockSpec((HG_HP // 2, HG_TC, 2 * HEAD_DIM), lambda h, j: (h, j, 0)),
                   pl.BlockSpec((HG_HP, nsub, HEAD_DIM, HEAD_DIM), lambda h, j: (h, j, 0, 0))],
        out_shape=[_sds((SEQ, D_MODEL), F32), _sds((N_CHIP, SEQ, 2 * HEAD_DIM), BF16),
                   _sds((HEADS, SEQ // HG_SUB, HEAD_DIM, HEAD_DIM), BF16)],
        scratch_shapes=[pltpu.VMEM((HG_HP, HEAD_DIM, HEAD_DIM), F32)],
        compiler_params=_cparams(("parallel", "arbitrary")),
    )(proj4, lb, gain)


def _hgrn_bwd(proj4, lb, gain, o_raw, dy4, states, name):
    nsub = HG_TC // HG_SUB
    nt = SEQ // HG_TC
    wide = HG_HP * HEAD_DIM

    def body(p_ref, lb_ref, gn_ref, o_ref, dy_ref, st_ref, dp_ref, dlb_ref, dgn_ref, dst_s):
        @pl.when(pl.program_id(1) == 0)
        def _():
            dst_s[...] = jnp.zeros_like(dst_s)
            dlb_ref[...] = jnp.zeros_like(dlb_ref)
            dgn_ref[...] = jnp.zeros_like(dgn_ref)

        gnv = gn_ref[...]
        shp = (HG_SUB, HG_SUB, HEAD_DIM)
        tri = lax.broadcasted_iota(jnp.int32, shp, 0) >= lax.broadcasted_iota(jnp.int32, shp, 1)

        def head(qr, fr, vv, gr, o, dy, lbv, st0, dst):
            q, sig, fg, k, gcum, glast = _hgrn_gates(qr, fr, lbv)
            rs = lax.rsqrt(jnp.mean(o * o, axis=-1, keepdims=True) + RMS_EPS)
            oh = o * rs
            don = dy * _silu(gr)
            dgn = jnp.sum(don * oh, axis=0, keepdims=True)
            dgr = dy * oh * gnv * _dsilu(gr)
            doh = don * gnv
            do = rs * (doh - oh * jnp.mean(doh * oh, axis=-1, keepdims=True))
            dst16 = dst.astype(BF16)
            do16 = do.astype(BF16)
            eg = jnp.exp(gcum)
            eb = jnp.exp(glast - gcum)
            e, a = _hgrn_intra(q, k, gcum, tri)
            da = jnp.sum(do[:, None, :] * vv[None, :, :], axis=-1, keepdims=True)
            dae = da * e
            dq = jnp.sum(dae * k[None, :, :], axis=1) + eg * _dot(do16, st0)
            dk_state = eb * _dot(vv.astype(BF16), dst16)
            dk = jnp.sum(dae * q[:, None, :], axis=0) + dk_state
            dv = jnp.sum(a * do[:, None, :], axis=0) + _dot_nt((k * eb).astype(BF16), dst16)
            eglast = jnp.exp(glast)
            dst_new = dst * eglast + _dot_tn(do16, (q * eg).astype(BF16))
            dglast = jnp.sum(k * dk_state, axis=0, keepdims=True) \
                + eglast * jnp.sum(dst * st0.astype(F32), axis=0, keepdims=True)
            dlf = _scan16(q * dq - k * dk, reverse=True) + dglast
            dfg = dlf / fg - dk
            dlb = jnp.sum(dfg * (1.0 - sig), axis=0, keepdims=True)
            dproj = ((dq * _dsilu(qr)).astype(BF16), (dfg * (1.0 - lbv) * sig * (1.0 - sig)).astype(BF16),
                     dv.astype(BF16), dgr.astype(BF16))
            return dproj, dst_new, dlb, dgn

        def it(ii, carry):
            i = nsub - 1 - ii
            rows = pl.ds(pl.multiple_of(i * HG_SUB, HG_SUB), HG_SUB)
            results = []
            for hh in range(HG_HP):
                lanes = pl.ds(hh * HEAD_DIM, HEAD_DIM)
                blk = [p_ref[j, rows, lanes] for j in range(4)]
                dy = dy_ref[hh // 2, rows, pl.ds((hh % 2) * HEAD_DIM, HEAD_DIM)]
                results.append(head(blk[0], blk[1], blk[2], blk[3], o_ref[rows, lanes], dy,
                                    lb_ref[:, lanes], st_ref[hh, i], dst_s[hh]))
            new_carry = []
            for hh, (dproj, dst_new, dlb, dgn) in enumerate(results):
                lanes = pl.ds(hh * HEAD_DIM, HEAD_DIM)
                dst_s[hh] = dst_new
                for j in range(4):
                    dp_ref[j, rows, lanes] = dproj[j]
                new_carry.append((carry[hh][0] + dlb, carry[hh][1] + dgn))
            return tuple(new_carry)

        zero = jnp.zeros((1, HEAD_DIM), F32)
        sums = lax.fori_loop(0, nsub, it, tuple((zero, zero) for _ in range(HG_HP)))
        for hh in range(HG_HP):
            dlb_ref[hh] += sums[hh][0]
            dgn_ref[hh] += sums[hh][1]

    vspec = pl.BlockSpec((HG_HP, 1, HEAD_DIM), lambda h, j: (h, 0, 0))
    return _pcall(
        body, name=name, grid=(HEADS // HG_HP, nt),
        in_specs=[pl.BlockSpec((4, HG_TC, wide), lambda h, j: (0, nt - 1 - j, h)),
                  pl.BlockSpec((1, wide), lambda h, j: (0, h)),
                  pl.BlockSpec((1, HEAD_DIM), lambda h, j: (0, 0)),
                  pl.BlockSpec((HG_TC, wide), lambda h, j: (nt - 1 - j, h)),
                  pl.BlockSpec((HG_HP // 2, HG_TC, 2 * HEAD_DIM), lambda h, j: (h, nt - 1 - j, 0)),
                  pl.BlockSpec((HG_HP, nsub, HEAD_DIM, HEAD_DIM), lambda h, j: (h, nt - 1 - j, 0, 0))],
        out_specs=[pl.BlockSpec((4, HG_TC, wide), lambda h, j: (0, nt - 1 - j, h)), vspec, vspec],
        out_shape=[_sds((4, SEQ, D_MODEL), BF16), _sds((HEADS, 1, HEAD_DIM), F32), _sds((HEADS, 1, HEAD_DIM), F32)],
        scratch_shapes=[pltpu.VMEM((HG_HP, HEAD_DIM, HEAD_DIM), F32)],
        compiler_params=_cparams(("parallel", "arbitrary")),
    )(proj4, lb, gain, o_raw, dy4, states)


def _t5_bucket(dist):
    n = np.asarray(dist, dtype=np.int64)
    max_exact = NUM_BUCKETS // 2
    large = max_exact + (np.log(np.maximum(n, 1) / max_exact) / np.log(MAX_DISTANCE / max_exact)
                         * (NUM_BUCKETS - max_exact)).astype(np.int64)
    large = np.minimum(large, NUM_BUCKETS - 1)
    return np.where(n < max_exact, n, large).astype(np.int32)


def _bias_tables():
    qi = np.arange(ATT_BLK)[:, None]
    ki = np.arange(2 * ATT_BLK)[None, :]
    j = ATT_BLK + qi - ki
    valid = (j >= 0) & (j <= ATT_BLK)
    return np.stack([np.where(valid, _t5_bucket(np.clip(j, 0, ATT_BLK) * d), -1) for _, d in GROUPS]).astype(np.int32)


def _attn_bias(rel_bias, name):
    idx = _bias_tables()
    ng = len(GROUPS)
    buckets = [sorted(set(idx[g][idx[g] >= 0].tolist())) for g in range(ng)]

    def body(rb_ref, idx_ref, o_ref):
        h = pl.program_id(0)
        for g in range(ng):
            ig = idx_ref[g]
            acc = jnp.full(ig.shape, NEG, F32)
            for b in buckets[g]:
                acc = jnp.where(ig == b, rb_ref[b, g * HEADS + h], acc)
            o_ref[g] = acc

    return _pcall(
        body, name=name, grid=(HEADS,),
        in_specs=[pl.BlockSpec(memory_space=pltpu.SMEM),
                  pl.BlockSpec((ng, ATT_BLK, 2 * ATT_BLK), lambda h: (0, 0, 0))],
        out_specs=pl.BlockSpec((ng, None, ATT_BLK, 2 * ATT_BLK), lambda h: (0, h, 0, 0)),
        out_shape=_sds((ng, HEADS, ATT_BLK, 2 * ATT_BLK), F32),
        compiler_params=_cparams(("parallel",)),
    )(rel_bias, jnp.asarray(idx))


ADA_SHARD = 6 * D_MODEL // N_CHIP
ADA_TN = 512


def _ada_fwd(c_all, ada_w, ada_b_cols, name):
    def body(c_ref, w_ref, b_ref, o_ref):
        ca = _silu(c_ref[...]).astype(BF16)
        o_ref[...] = _dot(ca, w_ref[...].astype(BF16)) + b_ref[...]

    return _pcall(
        body, name=name, grid=(DEPTH, ADA_SHARD // ADA_TN),
        in_specs=[pl.BlockSpec((N_DEV, D_MODEL), lambda l, j: (0, 0)),
                  pl.BlockSpec((None, D_MODEL, ADA_TN), lambda l, j: (l, 0, j)),
                  pl.BlockSpec((None, 1, ADA_TN), lambda l, j: (l, 0, j))],
        out_specs=pl.BlockSpec((None, N_DEV, ADA_TN), lambda l, j: (l, 0, j)),
        out_shape=_sds((DEPTH, N_DEV, ADA_SHARD), F32),
        compiler_params=_cparams(("parallel", "parallel")),
    )(c_all, ada_w, ada_b_cols)


def _ada_bwd(c_all, dmod_cols, name):
    def body(c_ref, d_ref, o_ref):
        ca = _silu(c_ref[...]).astype(BF16)
        o_ref[...] = _dot_tn(ca, d_ref[...].astype(BF16))

    return _pcall(
        body, name=name, grid=(DEPTH, ADA_SHARD // ADA_TN),
        in_specs=[pl.BlockSpec((N_DEV, D_MODEL), lambda l, j: (0, 0)),
                  pl.BlockSpec((None, N_DEV, ADA_TN), lambda l, j: (l, 0, j))],
        out_specs=pl.BlockSpec((None, D_MODEL, ADA_TN), lambda l, j: (l, 0, j)),
        out_shape=_sds((DEPTH, D_MODEL, ADA_SHARD), F32),
        compiler_params=_cparams(("parallel", "parallel")),
    )(c_all, dmod_cols)


def _lower_bounds(logits, name):
    def body(l_ref, o_ref):
        l0 = l_ref[0:1, :]
        l1 = l_ref[1:2, :]
        mx = jnp.maximum(l0, l1)
        e0 = jnp.exp(l0 - mx)
        e1 = jnp.exp(l1 - mx)
        p0 = e0 / (e0 + e1)
        p1 = e1 / (e0 + e1)
        o_ref[0:1, :] = p0 - p0
        o_ref[1:2, :] = (p0 + p1) - p0

    return _pcall(body, name=name, out_shape=_sds((DEPTH, D_MODEL), F32), compiler_params=_cparams())(logits)


_R_DMOD = 0
_R_NMIX = 96
_R_NFFN = 112
_R_QG = 128
_R_KG = 152
_R_GN = 176
_R_LB = 184
_R_RB = 192
SMALL_ROWS = 200


def _small_totals(gathered, logits8, name):
    ng = len(GROUPS)

    def body(g_ref, l_ref, main_ref, gains_ref, dlb_ref, rb_ref):
        tot = g_ref[0]
        for dev in range(1, N_DEV):
            tot = tot + g_ref[dev]
        main_ref[...] = tot[0:_R_QG]
        gains_ref[...] = jnp.zeros_like(gains_ref)
        for g in range(ng):
            gains_ref[g:g + 1, :] = jnp.sum(tot[_R_QG + 8 * g:_R_QG + 8 * g + 8], axis=0, keepdims=True)
            gains_ref[ng + g:ng + g + 1, :] = jnp.sum(tot[_R_KG + 8 * g:_R_KG + 8 * g + 8], axis=0, keepdims=True)
        gains_ref[2 * ng:2 * ng + 1, :] = jnp.sum(tot[_R_GN:_R_GN + 8], axis=0, keepdims=True)
        rb_ref[...] = tot[_R_RB:_R_RB + 8]
        dlb1 = tot[_R_LB:_R_LB + 8]
        l0 = l_ref[0]
        l1 = l_ref[1]
        mx = jnp.maximum(l0, l1)
        e0 = jnp.exp(l0 - mx)
        e1 = jnp.exp(l1 - mx)
        p0 = e0 / (e0 + e1)
        p1 = e1 / (e0 + e1)
        dlb_ref[0] = -p0 * p1 * dlb1
        dlb_ref[1] = p1 * (1.0 - p1) * dlb1

    return _pcall(
        body, name=name,
        out_shape=[_sds((_R_QG, 128), F32), _sds((8, 128), F32), _sds((DEPTH, 8, 128), F32), _sds((8, 128), F32)],
        compiler_params=_cparams(),
    )(gathered, logits8)


def _row_tile(rows):
    return 128 if rows % 128 == 0 else rows


def _adamw(w, grads, m, v, name):
    nl, r, cdim = w.shape
    tr = _row_tile(r)

    def body(*refs):
        g_refs = refs[:nl]
        w_ref, m_ref, v_ref, go_ref, d_ref, mo_ref, vo_ref = refs[nl:]

        def step(g):
            m2 = ADAM_B1 * m_ref[...] + (1.0 - ADAM_B1) * g
            v2 = ADAM_B2 * v_ref[...] + (1.0 - ADAM_B2) * (g * g)
            m_hat = m2 / (1.0 - ADAM_B1 ** ADAM_STEP)
            v_hat = v2 / (1.0 - ADAM_B2 ** ADAM_STEP)
            go_ref[...] = g
            d_ref[...] = -ADAM_LR * (m_hat / (jnp.sqrt(v_hat) + ADAM_EPS) + ADAM_WD * w_ref[...])
            mo_ref[...] = m2
            vo_ref[...] = v2

        if nl == 1:
            step(g_refs[0][...])
        else:
            for layer in range(nl):
                @pl.when(pl.program_id(0) == layer)
                def _(layer=layer):
                    step(g_refs[layer][...])

    big = pl.BlockSpec((None, tr, cdim), lambda l, i: (l, i, 0))
    g_specs = [pl.BlockSpec((tr, cdim), lambda l, i, layer=layer: (jnp.where(l == layer, i, 0), 0))
               for layer in range(nl)]
    shp = _sds((nl, r, cdim), F32)
    return _pcall(
        body, name=name, grid=(nl, r // tr),
        in_specs=g_specs + [big, big, big],
        out_specs=[big, big, big, big],
        out_shape=[shp, shp, shp, shp],
        compiler_params=_cparams(("parallel", "parallel")),
    )(*grads, w, m, v)


def _cast_bf16(place, w, name):
    nl, r, cdim = w.shape
    tr = _row_tile(r)

    def body(place_ref, w_ref, o_ref):
        o_ref[...] = w_ref[...].astype(BF16)

    return _pcall(
        body, name=name,
        grid_spec=pltpu.PrefetchScalarGridSpec(
            num_scalar_prefetch=1, grid=(nl, r // tr),
            in_specs=[pl.BlockSpec((None, tr, cdim), lambda l, i, place_ref: (l, i, 0))],
            out_specs=pl.BlockSpec((None, None, tr, cdim), lambda l, i, place_ref: (place_ref[1], l, i, 0))),
        out_shape=_sds((N_CHIP, nl, r, cdim), BF16),
        compiler_params=_cparams(("parallel", "parallel")),
    )(place, w)


def _rs_add_cast(place, grad, recv, name):
    _, k, n = grad.shape
    kh = k // 2
    tr = _row_tile(kh)
    nb = kh // tr

    def body(place_ref, g_ref, r_ref, o_ref):
        o_ref[...] = (g_ref[...] + r_ref[...]).astype(BF16)

    half = pl.BlockSpec((None, tr, n), lambda s, i, place_ref: (s, i, 0))
    return _pcall(
        body, name=name,
        grid_spec=pltpu.PrefetchScalarGridSpec(
            num_scalar_prefetch=1, grid=(N_CHIP, nb),
            in_specs=[pl.BlockSpec((None, tr, n), lambda s, i, place_ref: (s, place_ref[0] * nb + i, 0)), half],
            out_specs=half),
        out_shape=_sds((N_CHIP, kh, n), BF16),
        compiler_params=_cparams(("parallel", "parallel")),
    )(place, grad, recv)


def _rs_sum4(place, parts, got, name):
    _, kh, n = parts.shape
    tr = _row_tile(kh)
    nb = kh // tr

    def body(place_ref, p_ref, g_ref, o_ref):
        acc = p_ref[...].astype(F32)
        for j in range(N_CHIP - 1):
            acc = acc + g_ref[j].astype(F32)
        o_ref[...] = acc

    return _pcall(
        body, name=name,
        grid_spec=pltpu.PrefetchScalarGridSpec(
            num_scalar_prefetch=1, grid=(nb,),
            in_specs=[pl.BlockSpec((None, tr, n), lambda i, place_ref: (place_ref[1], i, 0)),
                      pl.BlockSpec((N_CHIP - 1, tr, n), lambda i, place_ref: (0, i, 0))],
            out_specs=pl.BlockSpec((tr, n), lambda i, place_ref: (place_ref[0] * nb + i, 0))),
        out_shape=_sds((2 * kh, n), F32),
        compiler_params=_cparams(("parallel",)),
    )(place, parts, got)


_ANY = pl.BlockSpec(memory_space=pl.ANY)


def _position():
    return lax.axis_index("x"), lax.axis_index("y"), lax.axis_index("c")


def _other_chips(x, y):
    return [(1 - x, y), (x, 1 - y), (1 - x, 1 - y)]


def _remote(src, dst, send_sem, recv_sem, to):
    return pltpu.make_async_remote_copy(src_ref=src, dst_ref=dst, send_sem=send_sem, recv_sem=recv_sem,
                                        device_id=to, device_id_type=MESH)


def _small_allgather(v, name):
    r = v.shape[0]

    def body(x_ref, out_ref, send_sems, recv_sems, local_sem):
        x, y, c = _position()
        me, sibling = (x, y, c), (x, y, 1 - c)
        chips = _other_chips(x, y)

        def slab(px, py, pc):
            return out_ref.at[4 * px + 2 * py + pc]

        def copy(k, block, to, src=None):
            return _remote(slab(*block) if src is None else src, slab(*block), send_sems.at[k], recv_sems.at[k], to)

        mine = pltpu.make_async_copy(x_ref, slab(*me), local_sem)
        mine.start()
        first = [copy(0, me, sibling, src=x_ref)]
        first += [copy(1 + j, me, (*chip, c), src=x_ref) for j, chip in enumerate(chips)]
        for cp in first:
            cp.start()
        passed = [copy(4 + j, (*chip, c), sibling) for j, chip in enumerate(chips)]
        for j, chip in enumerate(chips):
            copy(1 + j, (*chip, c), me).wait_recv()
            passed[j].start()
        copy(0, sibling, me).wait_recv()
        for j, chip in enumerate(chips):
            copy(4 + j, (*chip, 1 - c), me).wait_recv()
        for cp in first + passed:
            cp.wait_send()
        mine.wait()

    return _pcall(
        body, name=name,
        out_shape=_sds((N_DEV, r, 128), F32),
        in_specs=[pl.BlockSpec(memory_space=pltpu.VMEM)],
        out_specs=pl.BlockSpec(memory_space=pltpu.VMEM),
        scratch_shapes=[pltpu.SemaphoreType.DMA((7,)), pltpu.SemaphoreType.DMA((7,)), pltpu.SemaphoreType.DMA],
        compiler_params=_cparams(),
    )(v)


def _half_rows(core, kh):
    return pl.ds(pl.multiple_of(core * kh, 8), kh)


def _slab_half(ref, chip, core):
    return ref.at[chip, :, _half_rows(core, ref.shape[2] // 2), :]


def _gather_ici(out, send_sems, recv_sems):
    def copies():
        x, y, c = _position()
        for a in range(len(out)):
            for j, (px, py) in enumerate(_other_chips(x, y)):
                mine = _slab_half(out[a], 2 * x + y, c)
                landed = _slab_half(out[a], 2 * px + py, c)
                yield (_remote(mine, mine, send_sems.at[a, j], recv_sems.at[a, j], (px, py, c)),
                       _remote(landed, landed, send_sems.at[a, j], recv_sems.at[a, j], (px, py, c)))

    def start():
        for send, _ in copies():
            send.start()

    def wait():
        for send, recv in copies():
            recv.wait_recv()
            send.wait_send()

    return start, wait


def _gather_d2d(out, send_sems, recv_sems):
    def copies():
        x, y, c = _position()
        for a in range(len(out)):
            for j, (px, py) in enumerate(_other_chips(x, y)):
                landed = _slab_half(out[a], 2 * px + py, c)
                other = _slab_half(out[a], 2 * px + py, 1 - c)
                yield (_remote(landed, landed, send_sems.at[a, j], recv_sems.at[a, j], (x, y, 1 - c)),
                       _remote(other, other, send_sems.at[a, j], recv_sems.at[a, j], (x, y, 1 - c)))

    def start():
        for send, _ in copies():
            send.start()

    def wait():
        for send, recv in copies():
            recv.wait_recv()
            send.wait_send()

    return start, wait


def _gather_weights(slabs, name, ici=True):
    n = len(slabs)

    def body(*refs):
        out = refs[n:2 * n]
        sems = refs[2 * n:]
        if ici:
            start, wait = _gather_ici(out, sems[2], sems[3])
            start()
            wait()
        start, wait = _gather_d2d(out, sems[0], sems[1])
        start()
        wait()

    sem = pltpu.SemaphoreType.DMA((n, 3))
    return _pcall(
        body, name=name,
        out_shape=[_sds(s.shape, BF16) for s in slabs],
        in_specs=[_ANY] * n, out_specs=[_ANY] * n,
        input_output_aliases={a: a for a in range(n)},
        scratch_shapes=[sem, sem] + ([sem, sem] if ici else []),
        compiler_params=_cparams(),
    )(*slabs)


def _rs_halves(grads, out, send_sems, recv_sems):
    def copies():
        x, y, c = _position()
        for a in range(len(grads)):
            kh = grads[a].shape[1] // 2
            yield _remote(grads[a].at[:, _half_rows(1 - c, kh), :], out[a], send_sems.at[a], recv_sems.at[a],
                          (x, y, 1 - c))

    def start():
        for cp in copies():
            cp.start()

    def wait():
        for cp in copies():
            cp.wait()

    return start, wait


def _rs_halves_shapes(grads):
    return [_sds((N_CHIP, g.shape[1] // 2, g.shape[2]), F32) for g in grads]


def _rs_exchange_halves(grads, name):
    n = len(grads)

    def body(*refs):
        start, wait = _rs_halves(refs[:n], refs[n:2 * n], *refs[2 * n:])
        start()
        wait()

    return _pcall(
        body, name=name,
        out_shape=_rs_halves_shapes(grads),
        in_specs=[_ANY] * n, out_specs=[_ANY] * n,
        scratch_shapes=[pltpu.SemaphoreType.DMA((n,)), pltpu.SemaphoreType.DMA((n,))],
        compiler_params=_cparams(),
    )(*grads)


def _rs_chips(parts, out, send_sems, recv_sems):
    def copies():
        x, y, c = _position()
        for a in range(len(parts)):
            for j, (px, py) in enumerate(_other_chips(x, y)):
                got = out[a].at[j]
                yield (_remote(parts[a].at[2 * px + py], got, send_sems.at[a, j], recv_sems.at[a, j], (px, py, c)),
                       _remote(got, got, send_sems.at[a, j], recv_sems.at[a, j], (px, py, c)))

    def start():
        for send, _ in copies():
            send.start()

    def wait():
        for send, recv in copies():
            recv.wait_recv()
            send.wait_send()

    return start, wait


def _rs_chips_shapes(parts):
    return [_sds((N_CHIP - 1,) + p.shape[1:], BF16) for p in parts]


def _rs_join_halves(fulls, name):
    n = len(fulls)

    def body(*refs):
        out = refs[n:2 * n]
        send_sems, recv_sems = refs[2 * n:]
        x, y, c = _position()
        copies = []
        for a in range(n):
            kh = out[a].shape[0] // 2
            mine = out[a].at[_half_rows(c, kh), :]
            cp = _remote(mine, mine, send_sems.at[a], recv_sems.at[a], (x, y, 1 - c))
            cp.start()
            copies.append(cp)
        for a in range(n):
            kh = out[a].shape[0] // 2
            theirs = out[a].at[_half_rows(1 - c, kh), :]
            _remote(theirs, theirs, send_sems.at[a], recv_sems.at[a], (x, y, 1 - c)).wait_recv()
        for cp in copies:
            cp.wait_send()

    return _pcall(
        body, name=name,
        out_shape=[_sds(f.shape, F32) for f in fulls],
        in_specs=[_ANY] * n, out_specs=[_ANY] * n,
        input_output_aliases={a: a for a in range(n)},
        scratch_shapes=[pltpu.SemaphoreType.DMA((n,)), pltpu.SemaphoreType.DMA((n,))],
        compiler_params=_cparams(),
    )(*fulls)


_SMALL_ORDER = ("rel_bias", "ada_b", "norm_mix", "norm_ffn", "attn_q_gain", "attn_k_gain", "hgrn_gnorm",
                "hgrn_lower_bounds")
_WEIGHT_ORDER = ("rel_bias", "ada_w", "ada_b", "norm_mix", "norm_ffn", "attn_w_qkv", "attn_w_out", "attn_q_gain",
                 "attn_k_gain", "hgrn_w_in", "hgrn_w_out", "hgrn_gnorm", "hgrn_lower_bounds", "ffn_w1", "ffn_w3",
                 "ffn_w2")


def _qkv_group_map(t):
    return t // 4, t % 4


def _qkv_chip_map(t):
    return t // 9, t % 9


def _hin_map(t):
    return t // 2, t % 2


def _block_map(t):
    return t, 0


def _pack_rows(parts):
    return jnp.concatenate([p.reshape(-1, 128) for p in parts], axis=0)


def kernel(x, c, rel_bias, ada_w, ada_b, norm_mix, norm_ffn, attn_w_qkv, attn_w_out, attn_q_gain, attn_k_gain, hgrn_w_in, hgrn_w_out, hgrn_gnorm, hgrn_lower_bounds, ffn_w1, ffn_w3, ffn_w2, loss_target, m_rel_bias, m_ada_w, m_ada_b, m_norm_mix, m_norm_ffn, m_attn_w_qkv, m_attn_w_out, m_attn_q_gain, m_attn_k_gain, m_hgrn_w_in, m_hgrn_w_out, m_hgrn_gnorm, m_hgrn_lower_bounds, m_ffn_w1, m_ffn_w3, m_ffn_w2, v_rel_bias, v_ada_w, v_ada_b, v_norm_mix, v_norm_ffn, v_attn_w_qkv, v_attn_w_out, v_attn_q_gain, v_attn_k_gain, v_hgrn_w_in, v_hgrn_w_out, v_hgrn_gnorm, v_hgrn_lower_bounds, v_ffn_w1, v_ffn_w3, v_ffn_w2):
    weights = dict(rel_bias=rel_bias, ada_w=ada_w, ada_b=ada_b, norm_mix=norm_mix, norm_ffn=norm_ffn,
                   attn_w_qkv=attn_w_qkv, attn_w_out=attn_w_out, attn_q_gain=attn_q_gain, attn_k_gain=attn_k_gain,
                   hgrn_w_in=hgrn_w_in, hgrn_w_out=hgrn_w_out, hgrn_gnorm=hgrn_gnorm,
                   hgrn_lower_bounds=hgrn_lower_bounds, ffn_w1=ffn_w1, ffn_w3=ffn_w3, ffn_w2=ffn_w2)
    mom1 = dict(rel_bias=m_rel_bias, ada_w=m_ada_w, ada_b=m_ada_b, norm_mix=m_norm_mix, norm_ffn=m_norm_ffn,
                attn_w_qkv=m_attn_w_qkv, attn_w_out=m_attn_w_out, attn_q_gain=m_attn_q_gain,
                attn_k_gain=m_attn_k_gain, hgrn_w_in=m_hgrn_w_in, hgrn_w_out=m_hgrn_w_out, hgrn_gnorm=m_hgrn_gnorm,
                hgrn_lower_bounds=m_hgrn_lower_bounds, ffn_w1=m_ffn_w1, ffn_w3=m_ffn_w3, ffn_w2=m_ffn_w2)
    mom2 = dict(rel_bias=v_rel_bias, ada_w=v_ada_w, ada_b=v_ada_b, norm_mix=v_norm_mix, norm_ffn=v_norm_ffn,
                attn_w_qkv=v_attn_w_qkv, attn_w_out=v_attn_w_out, attn_q_gain=v_attn_q_gain,
                attn_k_gain=v_attn_k_gain, hgrn_w_in=v_hgrn_w_in, hgrn_w_out=v_hgrn_w_out, hgrn_gnorm=v_hgrn_gnorm,
                hgrn_lower_bounds=v_hgrn_lower_bounds, ffn_w1=v_ffn_w1, ffn_w3=v_ffn_w3, ffn_w2=v_ffn_w2)

    transposed = ("ffn_w1", "ffn_w3")
    for group in (weights, mom1, mom2):
        for k in transposed:
            group[k] = jnp.transpose(group[k], (0, 2, 1))

    xi, yi, ci = _position()
    chip = 2 * xi + yi
    dev = 4 * xi + 2 * yi + ci
    place = jnp.stack([ci, chip]).astype(jnp.int32)
    d = D_MODEL

    big_names = ("attn_w_qkv", "attn_w_out", "hgrn_w_in", "hgrn_w_out", "ffn_w1", "ffn_w3", "ffn_w2")
    early_names, late_names = big_names[:2], big_names[2:]
    slabs16 = {k: _cast_bf16(place, weights[k], "cast_" + k) for k in big_names}
    wg = dict(zip(early_names, _gather_weights([slabs16[k] for k in early_names], "gather_early")))

    c_all = _small_allgather(c.reshape(8, 128), "gather_c").reshape(N_DEV, d)
    ada_b_cols = lax.dynamic_slice(ada_b, (0, chip * ADA_SHARD), (DEPTH, ADA_SHARD)).reshape(DEPTH, 1, ADA_SHARD)
    mod_shard = _ada_fwd(c_all, ada_w, ada_b_cols, "ada_fwd")
    mod_all = _small_allgather(mod_shard.reshape(-1, 128), "gather_mod").reshape(N_DEV, DEPTH, N_DEV, ADA_SHARD)
    mod_mine = lax.dynamic_index_in_dim(mod_all[0::2], dev, axis=2, keepdims=False)
    mod = jnp.transpose(mod_mine, (1, 0, 2)).reshape(DEPTH, 6 * d)

    def mods(layer):
        return [mod[layer:layer + 1, j * d:(j + 1) * d] for j in range(6)]

    x0 = x.reshape(SEQ, d)
    target = loss_target.reshape(SEQ, d)
    qg = attn_q_gain.reshape(len(GROUPS), 1, HEAD_DIM)
    kg = attn_k_gain.reshape(len(GROUPS), 1, HEAD_DIM)
    bias = _attn_bias(rel_bias, "attn_bias")
    lb1 = _lower_bounds(hgrn_lower_bounds, "lower_bounds")[1:2]

    def ffn_fwd(layer, x_in, sc2, sh2, g2):
        hf = _norm_mod(x_in, norm_ffn[layer:layer + 1], sc2, sh2, f"l{layer}_norm_ffn")
        a1, a3, u = _ffn_up(hf, wg["ffn_w1"], wg["ffn_w3"], layer, f"l{layer}_ffn_up")
        z, x_out = _mm_rows(u, wg["ffn_w2"], layer, x_in, g2, f"l{layer}_ffn_down")
        return x_out, (hf, a1, a3, u, z)

    def ffn_bwd(layer, dx_out, x_in, sc2, sh2, g2, saved, halves=()):
        hf, a1, a3, u, z = saved
        dz, dg2 = _gate_bwd(dx_out, z, g2, f"l{layer}_ffn_gate_bwd")
        da1, da3, *recv = _ffn_down_bwd(dz, wg["ffn_w2"], layer, a1, a3, f"l{layer}_ffn_down_bwd", halves=halves)
        dw2 = _mm_rows_bwd_w(u, dz, f"l{layer}_dw2")
        dh = _ffn_up_bwd(da1, da3, wg["ffn_w1"], wg["ffn_w3"], layer, f"l{layer}_ffn_up_bwd")
        dw1 = _mm_rows_bwd_w(da1, hf, f"l{layer}_dw1")
        dw3 = _mm_rows_bwd_w(da3, hf, f"l{layer}_dw3")
        dx_in, dsc2, dsh2, dnf = _norm_mod_bwd(x_in, norm_ffn[layer:layer + 1], sc2, sh2, dh, dx_out,
                                               f"l{layer}_norm_ffn_bwd")
        return dx_in, (dw1, dw3, dw2), (dsh2, dsc2, dg2), dnf, recv

    def rs_add(tags, grads_in, recv):
        return [_rs_add_cast(place, g, r, f"rs_add_{k}_{layer}") for (k, layer), g, r in zip(tags, grads_in, recv)]

    sh1_0, sc1_0, g1_0, sh2_0, sc2_0, g2_0 = mods(0)
    h0 = _norm_mod(x0, norm_mix[0:1], sc1_0, sh1_0, "l0_norm_mix")
    w_qkv9 = _retile_cols(wg["attn_w_qkv"].reshape(N_CHIP, d, 2304), n_out=9, width_out=d, tn=256,
                          src_map=_qkv_chip_map, dst_map=_qkv_group_map, n_tiles=36,
                          name="regroup_w_qkv").reshape(9, 1, d, d)
    qkv9 = _mm_cols(h0, w_qkv9, 0, n_blocks=9, width=d, tn=d, act_map=_block_map, w_map=_block_map,
                    out_dtype=F32, name="l0_qkv")
    o4, lse, *late = _attn_fwd(qkv9, qg, kg, bias, "l0_attn", gather=[slabs16[k] for k in late_names])
    wg.update(zip(late_names, _gather_weights(late, "gather_late_siblings", ici=False)))
    y0, x1 = _mm_rows(o4, wg["attn_w_out"], 0, x0, g1_0, "l0_attn_out")
    x2, ffn0 = ffn_fwd(0, x1, sc2_0, sh2_0, g2_0)

    sh1_1, sc1_1, g1_1, sh2_1, sc2_1, g2_1 = mods(1)
    h1 = _norm_mod(x2, norm_mix[1:2], sc1_1, sh1_1, "l1_norm_mix")
    proj4 = _mm_cols(h1, wg["hgrn_w_in"], 0, n_blocks=4, width=d, tn=512, act_map=_hin_map, w_map=_hin_map,
                     out_dtype=F32, name="l1_hgrn_in")
    o_raw, yg4, states = _hgrn_fwd(proj4, lb1, hgrn_gnorm, "l1_hgrn")
    y1, x3 = _mm_rows(yg4, wg["hgrn_w_out"], 0, x2, g1_1, "l1_hgrn_out")
    x4, ffn1 = ffn_fwd(1, x3, sc2_1, sh2_1, g2_1)

    dx4, loss_part = _loss_head(x4, target, "loss_head")
    loss = lax.psum(loss_part[0, 0], ("x", "y", "c"))

    dx3, (dw1_1, dw3_1, dw2_1), dmod2_1, dnf_1, _ = ffn_bwd(1, dx4, x3, sc2_1, sh2_1, g2_1, ffn1)
    dzm1, dg1_1 = _gate_bwd(dx3, y1, g1_1, "l1_mix_gate_bwd")
    dyg4 = _mm_rows_bwd_a(dzm1, wg["hgrn_w_out"], 0, "l1_hgrn_out_bwd")
    dw_hout = _mm_rows_bwd_w(yg4, dzm1, "l1_dw_hgrn_out")
    dproj4, dlb_h, dgn_h = _hgrn_bwd(proj4, lb1, hgrn_gnorm, o_raw, dyg4, states, "l1_hgrn_bwd")
    dh1 = _mm_cols_bwd_a([(dproj4, wg["hgrn_w_in"], 0)], tn=512, act_map=_hin_map, w_map=_hin_map, n_tiles=8,
                         name="l1_hgrn_in_bwd")
    dw_hin = _mm_cols_bwd_w(h1, dproj4, ns=d, tn=512, act_map=_hin_map, w_map=_hin_map, n_tiles=8,
                            name="l1_dw_hgrn_in")
    dx2, dsc1_1, dsh1_1, dnm_1 = _norm_mod_bwd(x2, norm_mix[1:2], sc1_1, sh1_1, dh1, dx3, "l1_norm_mix_bwd")

    tags_1 = [("hgrn_w_in", 0), ("hgrn_w_out", 0), ("ffn_w1", 1), ("ffn_w3", 1), ("ffn_w2", 1)]
    grads_1 = [dw_hin, dw_hout, dw1_1, dw3_1, dw2_1]
    dx1, (dw1_0, dw3_0, dw2_0), dmod2_0, dnf_0, recv_1 = ffn_bwd(0, dx2, x1, sc2_0, sh2_0, g2_0, ffn0,
                                                                halves=grads_1)
    tags_0 = [("ffn_w1", 0), ("ffn_w3", 0), ("ffn_w2", 0)]
    grads_0 = [dw1_0, dw3_0, dw2_0]
    dzm0, dg1_0 = _gate_bwd(dx1, y0, g1_0, "l0_mix_gate_bwd")
    do4, *recv_0 = _mm_rows_bwd_a(dzm0, wg["attn_w_out"], 0, "l0_attn_out_bwd", halves=grads_0)
    dw_aout = _mm_rows_bwd_w(o4, dzm0, "l0_dw_attn_out")
    tags_a = tags_1 + tags_0
    parts_a = rs_add(tags_1, grads_1, recv_1) + rs_add(tags_0, grads_0, recv_0)
    dqkv, dqg_h, dkg_h, dbias, *got_a = _attn_bwd(qkv9, qg, kg, bias, do4, o4, lse, "l0_attn_bwd", scatter=parts_a)
    dqkv9 = dqkv.reshape(9, SEQ, d)
    dw_qkv9 = _mm_cols_bwd_w(h0, dqkv9, ns=d, tn=d, act_map=_block_map, w_map=_block_map, n_tiles=9,
                             name="l0_dw_qkv", tm=512, n_out=9)
    dw_qkv = _retile_cols(dw_qkv9, n_out=N_CHIP, width_out=2304, tn=256, src_map=_qkv_group_map,
                          dst_map=_qkv_chip_map, n_tiles=36, name="regroup_dw_qkv")
    tags_b = [("attn_w_qkv", 0), ("attn_w_out", 0)]
    grads_b = [dw_qkv, dw_aout]
    parts_b = rs_add(tags_b, grads_b, _rs_exchange_halves(grads_b, "rs_exchange_halves_b"))
    dh0, *got_b = _mm_cols_bwd_a([(dqkv9, w_qkv9, 0)], tn=d, act_map=_block_map, w_map=_block_map, n_tiles=9,
                                 name="l0_qkv_bwd", scatter=parts_b)
    dx0, dsc1_0, dsh1_0, dnm_0 = _norm_mod_bwd(x0, norm_mix[0:1], sc1_0, sh1_0, dh0, dx1, "l0_norm_mix_bwd")
    drb8 = _relbias_bwd(dbias, jnp.asarray(_bias_tables()), "rel_bias_bwd")

    small = _pack_rows([
        dsh1_0, dsc1_0, dg1_0, *dmod2_0, dsh1_1, dsc1_1, dg1_1, *dmod2_1,
        dnm_0, dnm_1, dnf_0, dnf_1,
        jnp.transpose(dqg_h, (1, 0, 2, 3)), jnp.transpose(dkg_h, (1, 0, 2, 3)), dgn_h, dlb_h, drb8])
    small_all = _small_allgather(small, "gather_small")
    main, gains, dlbnd, rbt = _small_totals(small_all, hgrn_lower_bounds.reshape(DEPTH, 8, 128), "small_totals")
    ng = len(GROUPS)
    grads = {
        "ada_b": main[_R_DMOD:_R_NMIX].reshape(DEPTH, 6 * d),
        "norm_mix": main[_R_NMIX:_R_NFFN].reshape(DEPTH, d),
        "norm_ffn": main[_R_NFFN:_R_QG].reshape(DEPTH, d),
        "attn_q_gain": gains[0:ng].reshape(1, ng, HEAD_DIM),
        "attn_k_gain": gains[ng:2 * ng].reshape(1, ng, HEAD_DIM),
        "hgrn_gnorm": gains[2 * ng:2 * ng + 1],
        "hgrn_lower_bounds": dlbnd.reshape(DEPTH, d),
        "rel_bias": jnp.transpose(rbt[:, :ng * NUM_BUCKETS].reshape(HEADS, ng, NUM_BUCKETS), (2, 1, 0))
                       .reshape(NUM_BUCKETS, ng * HEADS),
    }
    dmod_all = small_all[:, _R_DMOD:_R_NMIX].reshape(N_DEV, DEPTH, 6 * d)
    dmod_cols = jnp.transpose(lax.dynamic_slice(dmod_all, (0, 0, chip * ADA_SHARD), (N_DEV, DEPTH, ADA_SHARD)),
                              (1, 0, 2))
    grad_ada_w = _ada_bwd(c_all, dmod_cols, "ada_bwd")

    tags = tags_a + tags_b
    halves = [_rs_sum4(place, p, r, f"rs_sum_{k}_{layer}")
              for (k, layer), p, r in zip(tags, parts_a + parts_b, list(got_a) + list(got_b))]
    full = dict(zip(tags, _rs_join_halves(halves, "rs_join_halves")))

    out_g, out_d, out_m, out_v = {}, {}, {}, {}
    for k in big_names:
        gs = [full[(k, layer)] for layer in range(weights[k].shape[0])]
        out_g[k], out_d[k], out_m[k], out_v[k] = _adamw(weights[k], gs, mom1[k], mom2[k], "adamw_" + k)
    shp = (1, DEPTH * d, ADA_SHARD)
    res = _adamw(ada_w.reshape(shp), [grad_ada_w.reshape(shp[1:])], m_ada_w.reshape(shp), v_ada_w.reshape(shp),
                 "adamw_ada_w")
    out_g["ada_w"], out_d["ada_w"], out_m["ada_w"], out_v["ada_w"] = [r.reshape(ada_w.shape) for r in res]
    packed = [_pack_rows([src[k] for k in _SMALL_ORDER])[None] for src in (weights, grads, mom1, mom2)]
    res = _adamw(packed[0], [packed[1][0]], packed[2], packed[3], "adamw_small")
    offset = 0
    for k in _SMALL_ORDER:
        size = weights[k].size
        for dst, r in zip((out_g, out_d, out_m, out_v), res):
            dst[k] = r.reshape(-1)[offset:offset + size].reshape(weights[k].shape)
        offset += size
    for dst in (out_g, out_d, out_m, out_v):
        for k in transposed:
            dst[k] = jnp.transpose(dst[k], (0, 2, 1))

    return (loss, dx0.reshape(x.shape), *[out_g[k] for k in _WEIGHT_ORDER], *[out_d[k] for k in _WEIGHT_ORDER],
            *[out_m[k] for k in _WEIGHT_ORDER], *[out_v[k] for k in _WEIGHT_ORDER])
```

```python
import functools

import numpy as np
import jax
import jax.numpy as jnp
from jax import lax
from jax.experimental import pallas as pl
from jax.experimental.pallas import tpu as pltpu

F32 = jnp.float32
BF16 = jnp.bfloat16

D_MODEL = 1024
SEQ = 4096
N_DEV = 8
N_CHIP = 4
DEPTH = 2
HEADS = 8
HEAD_DIM = 128
GROUPS = ((128, 1), (512, 4), (2048, 16))
ATT_BLK = 128
ATT_WAYS = 4
ATT_STEPS = SEQ // ATT_BLK // ATT_WAYS
NUM_BUCKETS = 32
MAX_DISTANCE = 2048
FFN_HIDDEN = 2816
FFN_SHARD = FFN_HIDDEN // N_CHIP
HG_SUB = 16
HG_TC = 512
HG_HP = 4
RMS_EPS = 1e-6
NEG = -1e30
ATT_SCALE = HEAD_DIM ** -0.5
ADAM_LR, ADAM_B1, ADAM_B2, ADAM_EPS, ADAM_WD, ADAM_STEP = 0.001, 0.9, 0.999, 1e-08, 0.01, 10
VMEM_LIMIT = 56 * 1024 * 1024
MESH = pl.DeviceIdType.MESH


def _pcall(body, **kw):
    return pl.pallas_call(body, **kw)


def _cparams(sem=None):
    if sem is None:
        return pltpu.CompilerParams(vmem_limit_bytes=VMEM_LIMIT)
    return pltpu.CompilerParams(dimension_semantics=sem, vmem_limit_bytes=VMEM_LIMIT)


def _sds(shape, dtype):
    return jax.ShapeDtypeStruct(shape, dtype)


def _dot(a, b):
    return jnp.dot(a, b, preferred_element_type=F32)


def _dot_nt(a, b):
    return lax.dot_general(a, b, (((1,), (1,)), ((), ())), preferred_element_type=F32)


def _dot_tn(a, b):
    return lax.dot_general(a, b, (((0,), (0,)), ((), ())), preferred_element_type=F32)


def _sigmoid(x):
    return 1.0 / (1.0 + jnp.exp(-x))


def _silu(x):
    return x * _sigmoid(x)


def _dsilu(x):
    s = _sigmoid(x)
    return s * (1.0 + x * (1.0 - s))


def _norm_mod(x, gain, sc, sh, name):
    tm = 512

    def body(x_ref, g_ref, sc_ref, sh_ref, h_ref):
        xv = x_ref[...]
        rs = lax.rsqrt(jnp.mean(xv * xv, axis=-1, keepdims=True) + RMS_EPS)
        h_ref[...] = ((xv * rs * g_ref[...]) * (1.0 + sc_ref[...]) + sh_ref[...]).astype(BF16)

    vec = pl.BlockSpec((1, D_MODEL), lambda i: (0, 0))
    return _pcall(
        body, name=name, grid=(SEQ // tm,),
        in_specs=[pl.BlockSpec((tm, D_MODEL), lambda i: (i, 0)), vec, vec, vec],
        out_specs=pl.BlockSpec((tm, D_MODEL), lambda i: (i, 0)),
        out_shape=_sds((SEQ, D_MODEL), BF16),
        compiler_params=_cparams(("parallel",)),
    )(x, gain, sc, sh)


def _norm_mod_bwd(x, gain, sc, sh, dh, dres, name):
    tm = 512

    def body(x_ref, g_ref, sc_ref, sh_ref, dh_ref, dres_ref, dx_ref, dsc_ref, dsh_ref, dg_ref):
        @pl.when(pl.program_id(0) == 0)
        def _():
            dsc_ref[...] = jnp.zeros_like(dsc_ref)
            dsh_ref[...] = jnp.zeros_like(dsh_ref)
            dg_ref[...] = jnp.zeros_like(dg_ref)

        xv = x_ref[...]
        dhv = dh_ref[...]
        rs = lax.rsqrt(jnp.mean(xv * xv, axis=-1, keepdims=True) + RMS_EPS)
        xh = xv * rs
        dsc_ref[...] += jnp.sum(dhv * (xh * g_ref[...]), axis=0, keepdims=True)
        dsh_ref[...] += jnp.sum(dhv, axis=0, keepdims=True)
        dhn = dhv * (1.0 + sc_ref[...])
        dg_ref[...] += jnp.sum(dhn * xh, axis=0, keepdims=True)
        dxh = dhn * g_ref[...]
        dx_ref[...] = dres_ref[...] + rs * (dxh - xh * jnp.mean(dxh * xh, axis=-1, keepdims=True))

    vec = pl.BlockSpec((1, D_MODEL), lambda i: (0, 0))
    big = pl.BlockSpec((tm, D_MODEL), lambda i: (i, 0))
    return _pcall(
        body, name=name, grid=(SEQ // tm,),
        in_specs=[big, vec, vec, vec, big, big],
        out_specs=[big, vec, vec, vec],
        out_shape=[_sds((SEQ, D_MODEL), F32)] + [_sds((1, D_MODEL), F32)] * 3,
        compiler_params=_cparams(("arbitrary",)),
    )(x, gain, sc, sh, dh, dres)


def _mm_cols(a, wg, layer, *, n_blocks, width, tn, act_map, w_map, out_dtype, name, tm=1024):
    k = a.shape[1]
    n_tiles = n_blocks * width // tn

    def body(a_ref, w_ref, o_ref):
        o_ref[...] = _dot(a_ref[...], w_ref[...]).astype(o_ref.dtype)

    return _pcall(
        body, name=name, grid=(SEQ // tm, n_tiles),
        in_specs=[pl.BlockSpec((tm, k), lambda i, t: (i, 0)),
                  pl.BlockSpec((None, None, k, tn), lambda i, t: (w_map(t)[0], layer, 0, w_map(t)[1]))],
        out_specs=pl.BlockSpec((None, tm, tn), lambda i, t: (act_map(t)[0], i, act_map(t)[1])),
        out_shape=_sds((n_blocks, SEQ, width), out_dtype),
        compiler_params=_cparams(("parallel", "arbitrary")),
    )(a, wg)


def _mm_cols_bwd_a(dout, wg, layer, *, group, name, tm=1024, scatter=()):
    n_blocks, _, width = dout.shape
    k = wg.shape[2]
    n_s = len(scatter)
    n_rows = SEQ // tm
    n_steps = n_blocks // group

    def body(*refs):
        d_ref, w_ref = refs[:2]
        o_ref = refs[2 + n_s]
        if n_s:
            comm_start, comm_wait = _rs_chips(refs[2:2 + n_s], refs[3 + n_s:3 + 2 * n_s], *refs[3 + 2 * n_s:])
            pl.when((pl.program_id(0) == 0) & (pl.program_id(1) == 0))(comm_start)
        acc = _dot_nt(d_ref[0], w_ref[0])
        for b in range(1, group):
            acc += _dot_nt(d_ref[b], w_ref[b])
        if n_steps == 1:
            o_ref[...] = acc
        else:
            @pl.when(pl.program_id(1) == 0)
            def _():
                o_ref[...] = acc

            @pl.when(pl.program_id(1) > 0)
            def _():
                o_ref[...] += acc
        if n_s:
            pl.when((pl.program_id(0) == n_rows - 1) & (pl.program_id(1) == n_steps - 1))(comm_wait)

    sem = pltpu.SemaphoreType.DMA((max(n_s, 1), 3))
    res = _pcall(
        body, name=name, grid=(n_rows, n_steps),
        in_specs=[pl.BlockSpec((group, tm, width), lambda i, t: (t, i, 0)),
                  pl.BlockSpec((group, None, k, width), lambda i, t: (t, layer, 0, 0))] + [_ANY] * n_s,
        out_specs=[pl.BlockSpec((tm, k), lambda i, t: (i, 0))] + [_ANY] * n_s,
        out_shape=[_sds((SEQ, k), F32)] + _rs_chips_shapes(scatter),
        scratch_shapes=[sem, sem] if n_s else [],
        compiler_params=_cparams(("arbitrary", "arbitrary") if n_s else ("parallel", "arbitrary")),
    )(dout, wg, *scatter)
    return res if n_s else res[0]


def _mm_cols_bwd_w(a, dout, *, ns, tn, act_map, w_map, n_tiles, name, tm=1024, n_out=N_CHIP):
    k = a.shape[1]

    def body(a_ref, d_ref, o_ref):
        @pl.when(pl.program_id(1) == 0)
        def _():
            o_ref[...] = jnp.zeros_like(o_ref)

        o_ref[...] += _dot_tn(a_ref[...], d_ref[...])

    return _pcall(
        body, name=name, grid=(n_tiles, SEQ // tm),
        in_specs=[pl.BlockSpec((tm, k), lambda t, i: (i, 0)),
                  pl.BlockSpec((None, tm, tn), lambda t, i: (act_map(t)[0], i, act_map(t)[1]))],
        out_specs=pl.BlockSpec((None, k, tn), lambda t, i: (w_map(t)[0], 0, w_map(t)[1])),
        out_shape=_sds((n_out, k, ns), F32),
        compiler_params=_cparams(("parallel", "arbitrary")),
    )(a, dout)


def _retile_cols(src, *, n_out, width_out, tn, src_map, dst_map, n_tiles, name):
    k = src.shape[1]

    def body(s_ref, o_ref):
        o_ref[...] = s_ref[...]

    return _pcall(
        body, name=name, grid=(n_tiles,),
        in_specs=[pl.BlockSpec((None, k, tn), lambda t: (src_map(t)[0], 0, src_map(t)[1]))],
        out_specs=pl.BlockSpec((None, k, tn), lambda t: (dst_map(t)[0], 0, dst_map(t)[1])),
        out_shape=_sds((n_out, k, width_out), src.dtype),
        compiler_params=_cparams(("parallel",)),
    )(src)


def _mm_rows(a4, wg, layer, x, gate, name, tm=512):
    ks = a4.shape[2]
    n = wg.shape[3]

    def body(a_ref, w_ref, x_ref, g_ref, z_ref, xn_ref):
        z = _dot(a_ref[0], w_ref[0])
        for s in range(1, N_CHIP):
            z += _dot(a_ref[s], w_ref[s])
        z_ref[...] = z
        xn_ref[...] = x_ref[...] + g_ref[...] * z

    big = pl.BlockSpec((tm, n), lambda i: (i, 0))
    return _pcall(
        body, name=name, grid=(SEQ // tm,),
        in_specs=[pl.BlockSpec((N_CHIP, tm, ks), lambda i: (0, i, 0)),
                  pl.BlockSpec((N_CHIP, None, ks, n), lambda i: (0, layer, 0, 0)),
                  big, pl.BlockSpec((1, n), lambda i: (0, 0))],
        out_specs=[big, big],
        out_shape=[_sds((SEQ, n), F32), _sds((SEQ, n), F32)],
        compiler_params=_cparams(("parallel",)),
    )(a4, wg, x, gate)


def _gate_bwd(dx, z, gate, name):
    tm = 512

    def body(dx_ref, z_ref, g_ref, dz_ref, dg_ref):
        @pl.when(pl.program_id(0) == 0)
        def _():
            dg_ref[...] = jnp.zeros_like(dg_ref)

        dxv = dx_ref[...]
        dz_ref[...] = (dxv * g_ref[...]).astype(BF16)
        dg_ref[...] += jnp.sum(dxv * z_ref[...], axis=0, keepdims=True)

    big = pl.BlockSpec((tm, D_MODEL), lambda i: (i, 0))
    vec = pl.BlockSpec((1, D_MODEL), lambda i: (0, 0))
    return _pcall(
        body, name=name, grid=(SEQ // tm,),
        in_specs=[big, big, vec], out_specs=[big, vec],
        out_shape=[_sds((SEQ, D_MODEL), BF16), _sds((1, D_MODEL), F32)],
        compiler_params=_cparams(("arbitrary",)),
    )(dx, z, gate)


def _mm_rows_bwd_a(dz, wg, layer, name, tm=1024, halves=()):
    ks, n = wg.shape[2], wg.shape[3]
    n_h = len(halves)
    n_rows = SEQ // tm

    def body(*refs):
        dz_ref, w_ref = refs[:2]
        o_ref = refs[2 + n_h]
        if n_h:
            comm_start, comm_wait = _rs_halves(refs[2:2 + n_h], refs[3 + n_h:3 + 2 * n_h], *refs[3 + 2 * n_h:])
            pl.when((pl.program_id(0) == 0) & (pl.program_id(1) == 0))(comm_start)
        o_ref[...] = _dot_nt(dz_ref[...], w_ref[...])
        if n_h:
            pl.when((pl.program_id(0) == n_rows - 1) & (pl.program_id(1) == N_CHIP - 1))(comm_wait)

    sem = pltpu.SemaphoreType.DMA((max(n_h, 1),))
    res = _pcall(
        body, name=name, grid=(n_rows, N_CHIP),
        in_specs=[pl.BlockSpec((tm, n), lambda i, s: (i, 0)),
                  pl.BlockSpec((None, None, ks, n), lambda i, s: (s, layer, 0, 0))] + [_ANY] * n_h,
        out_specs=[pl.BlockSpec((None, tm, ks), lambda i, s: (s, i, 0))] + [_ANY] * n_h,
        out_shape=[_sds((N_CHIP, SEQ, ks), F32)] + _rs_halves_shapes(halves),
        scratch_shapes=[sem, sem] if n_h else [],
        compiler_params=_cparams(("arbitrary", "arbitrary") if n_h else ("parallel", "arbitrary")),
    )(dz, wg, *halves)
    return res if n_h else res[0]


def _mm_rows_bwd_w(a4, dz, name, tm=2048):
    ks = a4.shape[2]
    n = dz.shape[1]

    def body(a_ref, dz_ref, o_ref):
        @pl.when(pl.program_id(1) == 0)
        def _():
            o_ref[...] = jnp.zeros_like(o_ref)

        o_ref[...] += _dot_tn(a_ref[...], dz_ref[...])

    return _pcall(
        body, name=name, grid=(N_CHIP, SEQ // tm),
        in_specs=[pl.BlockSpec((None, tm, ks), lambda s, i: (s, i, 0)),
                  pl.BlockSpec((tm, n), lambda s, i: (i, 0))],
        out_specs=pl.BlockSpec((None, ks, n), lambda s, i: (s, 0, 0)),
        out_shape=_sds((N_CHIP, ks, n), F32),
        compiler_params=_cparams(("parallel", "arbitrary")),
    )(a4, dz)


def _ffn_up(h, w1g, w3g, layer, name, tm=1024):
    def body(h_ref, w1_ref, w3_ref, a1_ref, a3_ref, u_ref):
        hv = h_ref[...]
        a1 = _dot_nt(hv, w1_ref[...])
        a3 = _dot_nt(hv, w3_ref[...])
        a1_ref[...] = a1
        a3_ref[...] = a3
        u_ref[...] = (_silu(a1) * a3).astype(BF16)

    wspec = pl.BlockSpec((None, None, FFN_SHARD, D_MODEL), lambda i, s: (s, layer, 0, 0))
    ospec = pl.BlockSpec((None, tm, FFN_SHARD), lambda i, s: (s, i, 0))
    shp = (N_CHIP, SEQ, FFN_SHARD)
    return _pcall(
        body, name=name, grid=(SEQ // tm, N_CHIP),
        in_specs=[pl.BlockSpec((tm, D_MODEL), lambda i, s: (i, 0)), wspec, wspec],
        out_specs=[ospec, ospec, ospec],
        out_shape=[_sds(shp, F32), _sds(shp, F32), _sds(shp, BF16)],
        compiler_params=_cparams(("parallel", "arbitrary")),
    )(h, w1g, w3g)


def _ffn_up_bwd(da1, da3, w1g, w3g, layer, name, tm=512):
    def body(d1_ref, d3_ref, w1_ref, w3_ref, o_ref):
        acc = _dot(d1_ref[0], w1_ref[0]) + _dot(d3_ref[0], w3_ref[0])
        for s in range(1, N_CHIP):
            acc += _dot(d1_ref[s], w1_ref[s]) + _dot(d3_ref[s], w3_ref[s])
        o_ref[...] = acc

    wspec = pl.BlockSpec((N_CHIP, None, FFN_SHARD, D_MODEL), lambda i: (0, layer, 0, 0))
    dspec = pl.BlockSpec((N_CHIP, tm, FFN_SHARD), lambda i: (0, i, 0))
    return _pcall(
        body, name=name, grid=(SEQ // tm,),
        in_specs=[dspec, dspec, wspec, wspec],
        out_specs=pl.BlockSpec((tm, D_MODEL), lambda i: (i, 0)),
        out_shape=_sds((SEQ, D_MODEL), F32),
        compiler_params=_cparams(("parallel",)),
    )(da1, da3, w1g, w3g)


def _ffn_down_bwd(dz, w2g, layer, a1, a3, name, tm=1024, halves=()):
    n_h = len(halves)
    n_rows = SEQ // tm

    def body(*refs):
        dz_ref, w_ref, a1_ref, a3_ref = refs[:4]
        da1_ref, da3_ref = refs[4 + n_h:6 + n_h]
        if n_h:
            comm_start, comm_wait = _rs_halves(refs[4:4 + n_h], refs[6 + n_h:6 + 2 * n_h], *refs[6 + 2 * n_h:])
            pl.when((pl.program_id(0) == 0) & (pl.program_id(1) == 0))(comm_start)
        du = _dot_nt(dz_ref[...], w_ref[...])
        a1 = a1_ref[...]
        da1_ref[...] = (du * a3_ref[...] * _dsilu(a1)).astype(BF16)
        da3_ref[...] = (du * _silu(a1)).astype(BF16)
        if n_h:
            pl.when((pl.program_id(0) == n_rows - 1) & (pl.program_id(1) == N_CHIP - 1))(comm_wait)

    blk = pl.BlockSpec((None, tm, FFN_SHARD), lambda i, s: (s, i, 0))
    shp = (N_CHIP, SEQ, FFN_SHARD)
    sem = pltpu.SemaphoreType.DMA((max(n_h, 1),))
    return _pcall(
        body, name=name, grid=(n_rows, N_CHIP),
        in_specs=[pl.BlockSpec((tm, D_MODEL), lambda i, s: (i, 0)),
                  pl.BlockSpec((None, None, FFN_SHARD, D_MODEL), lambda i, s: (s, layer, 0, 0)),
                  blk, blk] + [_ANY] * n_h,
        out_specs=[blk, blk] + [_ANY] * n_h,
        out_shape=[_sds(shp, BF16), _sds(shp, BF16)] + _rs_halves_shapes(halves),
        scratch_shapes=[sem, sem] if n_h else [],
        compiler_params=_cparams(("arbitrary", "arbitrary") if n_h else ("parallel", "arbitrary")),
    )(dz, w2g, a1, a3, *halves)


def _loss_head(y, target, name):
    tm = 512

    def body(y_ref, t_ref, dy_ref, l_ref, acc_ref):
        @pl.when(pl.program_id(0) == 0)
        def _():
            acc_ref[...] = jnp.zeros_like(acc_ref)

        err = y_ref[...] - t_ref[...]
        dy_ref[...] = err * (1.0 / D_MODEL)
        acc_ref[...] += jnp.sum(jnp.mean(err * err, axis=-1, keepdims=True), axis=0, keepdims=True)

        @pl.when(pl.program_id(0) == pl.num_programs(0) - 1)
        def _():
            l_ref[...] = 0.5 * acc_ref[...]

    big = pl.BlockSpec((tm, D_MODEL), lambda i: (i, 0))
    return _pcall(
        body, name=name, grid=(SEQ // tm,),
        in_specs=[big, big],
        out_specs=[big, pl.BlockSpec((1, 1), lambda i: (0, 0))],
        out_shape=[_sds((SEQ, D_MODEL), F32), _sds((1, 1), F32)],
        scratch_shapes=[pltpu.VMEM((1, 1), F32)],
        compiler_params=_cparams(("arbitrary",)),
    )(y, target)


def _attn_rows(base, d):
    if d == 1:
        return pl.ds(pl.multiple_of(base, ATT_BLK), ATT_BLK)
    return pl.ds(base, ATT_BLK, stride=d)


def _attn_block_index(i, d):
    nb = SEQ // (ATT_BLK * d)
    r = i // nb
    n = i % nb
    base = r + n * (ATT_BLK * d)
    pbase = jnp.maximum(base - ATT_BLK * d, r)
    return n, _attn_rows(base, d), _attn_rows(pbase, d)


def _attn_two_blocks(ref, prow, rows):
    return jnp.concatenate([ref[prow, :].astype(BF16), ref[rows, :].astype(BF16)], axis=0)


def _attn_block_bias(b_ref, n):
    b = b_ref[...]
    prev_half = lax.broadcasted_iota(jnp.int32, b.shape, 1) < ATT_BLK
    return jnp.where(prev_half & (n == 0), NEG, b)


def _qk_normed(x):
    rs = lax.rsqrt(jnp.mean(x * x, axis=-1, keepdims=True) + RMS_EPS)
    return x * rs, rs


def _attn_fwd(qkv9, qgain, kgain, bias, name, gather=()):
    n_g = len(gather)

    def body(*refs):
        q_ref, k_ref, v_ref, qg_ref, kg_ref, b_ref = refs[:6]
        o_ref, lse_ref = refs[6 + n_g:8 + n_g]
        qn_s, kn_s, acc_s, m_s, l_s = refs[8 + 2 * n_g:13 + 2 * n_g]
        g = pl.program_id(1)
        if n_g:
            comm_start, comm_wait = _gather_ici(refs[8 + n_g:8 + 2 * n_g], *refs[13 + 2 * n_g:])
            pl.when((pl.program_id(0) == 0) & (g == 0))(comm_start)

        @pl.when(g == 0)
        def _():
            m_s[...] = jnp.full_like(m_s, NEG)
            l_s[...] = jnp.zeros_like(l_s)
            acc_s[...] = jnp.zeros_like(acc_s)

        qn_s[...] = _qk_normed(q_ref[...])[0] * qg_ref[...]
        kn_s[...] = _qk_normed(k_ref[...])[0] * kg_ref[...]

        for gi, (_, d) in enumerate(GROUPS):
            @pl.when(g == gi)
            def _(d=d):
                def block(n, qb, kk, vv, m_old, l_old, acc_old):
                    s = _dot_nt(qb, kk) * ATT_SCALE + _attn_block_bias(b_ref, n)
                    m_new = jnp.maximum(m_old, jnp.max(s, axis=-1, keepdims=True))
                    alpha = jnp.exp(m_old - m_new)
                    p = jnp.exp(s - m_new)
                    l_new = alpha * l_old + jnp.sum(p, axis=-1, keepdims=True)
                    acc_new = alpha * acc_old + _dot(p.astype(BF16), vv)
                    return m_new, l_new, acc_new

                def it(i, carry):
                    where, loaded = [], []
                    for way in range(ATT_WAYS):
                        n, rows, prow = _attn_block_index(i + way * ATT_STEPS, d)
                        where.append(rows)
                        loaded.append((n, qn_s[rows, :].astype(BF16), _attn_two_blocks(kn_s, prow, rows),
                                       _attn_two_blocks(v_ref, prow, rows), m_s[rows, :], l_s[rows, :],
                                       acc_s[rows, :]))
                    results = [block(*vals) for vals in loaded]
                    for rows, (m_new, l_new, acc_new) in zip(where, results):
                        m_s[rows, :] = m_new
                        l_s[rows, :] = l_new
                        acc_s[rows, :] = acc_new
                    return carry

                lax.fori_loop(0, ATT_STEPS, it, 0)

        @pl.when(g == len(GROUPS) - 1)
        def _():
            o_ref[...] = (acc_s[...] / l_s[...]).astype(BF16)
            lse_ref[...] = m_s[...] + jnp.log(l_s[...])

        if n_g:
            pl.when((pl.program_id(0) == HEADS - 1) & (g == len(GROUPS) - 1))(comm_wait)

    def col(j):
        return pl.BlockSpec((None, SEQ, HEAD_DIM), lambda h, g: (g * 3 + j, 0, h))

    gspec = pl.BlockSpec((None, 1, HEAD_DIM), lambda h, g: (g, 0, 0))
    sem = pltpu.SemaphoreType.DMA((max(n_g, 1), 3))
    return _pcall(
        body, name=name, grid=(HEADS, len(GROUPS)),
        in_specs=[col(0), col(1), col(2), gspec, gspec,
                  pl.BlockSpec((None, None, ATT_BLK, 2 * ATT_BLK), lambda h, g: (g, h, 0, 0))] + [_ANY] * n_g,
        out_specs=[pl.BlockSpec((None, SEQ, HEAD_DIM), lambda h, g: (h // 2, 0, h % 2)),
                   pl.BlockSpec((None, SEQ, 1), lambda h, g: (h, 0, 0))] + [_ANY] * n_g,
        out_shape=[_sds((N_CHIP, SEQ, 2 * HEAD_DIM), BF16), _sds((HEADS, SEQ, 1), F32)]
        + [_sds(s.shape, s.dtype) for s in gather],
        input_output_aliases={6 + a: 2 + a for a in range(n_g)},
        scratch_shapes=[pltpu.VMEM((SEQ, HEAD_DIM), F32)] * 3 + [pltpu.VMEM((SEQ, 1), F32)] * 2
        + ([sem, sem] if n_g else []),
        compiler_params=_cparams(("arbitrary", "arbitrary")),
    )(qkv9, qkv9, qkv9, qgain, kgain, bias, *gather)


def _attn_bwd(qkv9, qgain, kgain, bias, do4, o4, lse, name, scatter=()):
    n_s = len(scatter)

    def body(*refs):
        q_ref, k_ref, v_ref, qg_ref, kg_ref, b_ref, do_ref, o_ref, lse_ref = refs[:9]
        dqkv_ref, dqg_ref, dkg_ref, db_ref = refs[9 + n_s:13 + n_s]
        qn_s, kn_s, dq_s, dk_s, dv_s, dl_s = refs[13 + 2 * n_s:19 + 2 * n_s]
        g = pl.program_id(1)
        if n_s:
            comm_start, comm_wait = _rs_chips(refs[9:9 + n_s], refs[13 + n_s:13 + 2 * n_s], *refs[19 + 2 * n_s:])
            pl.when((pl.program_id(0) == 0) & (g == 0))(comm_start)
        qh, rq = _qk_normed(q_ref[...])
        kh, rk = _qk_normed(k_ref[...])
        qn_s[...] = qh * qg_ref[...]
        kn_s[...] = kh * kg_ref[...]
        @pl.when(g == 0)
        def _():
            dl_s[...] = jnp.sum(do_ref[...] * o_ref[...].astype(F32), axis=-1, keepdims=True)

        dk_s[...] = jnp.zeros_like(dk_s)
        dv_s[...] = jnp.zeros_like(dv_s)
        db_ref[...] = jnp.zeros_like(db_ref)

        for gi, (_, d) in enumerate(GROUPS):
            @pl.when(g == gi)
            def _(d=d):
                def block(n, qb, kk, vv, dob, lse_b, dl):
                    s = _dot_nt(qb, kk) * ATT_SCALE + _attn_block_bias(b_ref, n)
                    p = jnp.exp(s - lse_b)
                    ds = p * (_dot_nt(dob, vv) - dl)
                    ds16 = ds.astype(BF16)
                    return (ds, _dot(ds16, kk) * ATT_SCALE, _dot_tn(ds16, qb) * ATT_SCALE,
                            _dot_tn(p.astype(BF16), dob))

                def it(i, carry):
                    where, loaded, old = [], [], []
                    for way in range(ATT_WAYS):
                        n, rows, prow = _attn_block_index(i + way * ATT_STEPS, d)
                        where.append((rows, prow))
                        loaded.append((n, qn_s[rows, :].astype(BF16), _attn_two_blocks(kn_s, prow, rows),
                                       _attn_two_blocks(v_ref, prow, rows), do_ref[rows, :].astype(BF16),
                                       lse_ref[rows, :], dl_s[rows, :]))
                        old.append((dk_s[rows, :], dk_s[prow, :], dv_s[rows, :], dv_s[prow, :]))
                    results = [block(*vals) for vals in loaded]
                    db_ref[...] += functools.reduce(lambda a, b: a + b, [r[0] for r in results])
                    for (rows, prow), (dk_c, dk_p, dv_c, dv_p), (_, dq, dkk, dvv) in zip(where, old, results):
                        dq_s[rows, :] = dq
                        dk_s[prow, :] = dk_p + dkk[:ATT_BLK]
                        dv_s[prow, :] = dv_p + dvv[:ATT_BLK]
                        dk_s[rows, :] = dk_c + dkk[ATT_BLK:]
                        dv_s[rows, :] = dv_c + dvv[ATT_BLK:]
                    return carry

                lax.fori_loop(0, ATT_STEPS, it, 0)

        def norm_bwd(dn, xh, rs, gain):
            dgain = jnp.sum(dn * xh, axis=0, keepdims=True)
            dxh = dn * gain
            return rs * (dxh - xh * jnp.mean(dxh * xh, axis=-1, keepdims=True)), dgain

        dq, dqg = norm_bwd(dq_s[...], qh, rq, qg_ref[...])
        dk, dkg = norm_bwd(dk_s[...], kh, rk, kg_ref[...])
        dqkv_ref[0] = dq.astype(BF16)
        dqkv_ref[1] = dk.astype(BF16)
        dqkv_ref[2] = dv_s[...].astype(BF16)
        dqg_ref[...] = dqg
        dkg_ref[...] = dkg
        if n_s:
            pl.when((pl.program_id(0) == HEADS - 1) & (g == len(GROUPS) - 1))(comm_wait)

    def col(j):
        return pl.BlockSpec((None, SEQ, HEAD_DIM), lambda h, g: (g * 3 + j, 0, h))

    gspec = pl.BlockSpec((None, 1, HEAD_DIM), lambda h, g: (g, 0, 0))
    bspec = pl.BlockSpec((None, None, ATT_BLK, 2 * ATT_BLK), lambda h, g: (g, h, 0, 0))
    hcol = pl.BlockSpec((None, SEQ, HEAD_DIM), lambda h, g: (h // 2, 0, h % 2))
    dgspec = pl.BlockSpec((None, None, 1, HEAD_DIM), lambda h, g: (h, g, 0, 0))
    ng = len(GROUPS)
    sem = pltpu.SemaphoreType.DMA((max(n_s, 1), 3))
    return _pcall(
        body, name=name, grid=(HEADS, ng),
        in_specs=[col(0), col(1), col(2), gspec, gspec, bspec, hcol, hcol,
                  pl.BlockSpec((None, SEQ, 1), lambda h, g: (h, 0, 0))] + [_ANY] * n_s,
        out_specs=[pl.BlockSpec((None, 3, SEQ, HEAD_DIM), lambda h, g: (g, 0, 0, h)), dgspec, dgspec, bspec]
        + [_ANY] * n_s,
        out_shape=[_sds((ng, 3, SEQ, D_MODEL), BF16), _sds((HEADS, ng, 1, HEAD_DIM), F32),
                   _sds((HEADS, ng, 1, HEAD_DIM), F32), _sds((ng, HEADS, ATT_BLK, 2 * ATT_BLK), F32)]
        + _rs_chips_shapes(scatter),
        scratch_shapes=[pltpu.VMEM((SEQ, HEAD_DIM), F32)] * 5 + [pltpu.VMEM((SEQ, 1), F32)]
        + ([sem, sem] if n_s else []),
        compiler_params=_cparams(("arbitrary", "arbitrary")),
    )(qkv9, qkv9, qkv9, qgain, kgain, bias, do4, o4, lse, *scatter)


def _relbias_bwd(dbias, bucket_idx, name):
    ng = len(GROUPS)

    def body(db_ref, idx_ref, o_ref):
        lane = lax.broadcasted_iota(jnp.int32, (HEADS, 128), 1)
        acc = jnp.zeros((HEADS, 128), F32)
        for g in range(ng):
            dbg = db_ref[g]
            idx = idx_ref[g]
            for b in range(NUM_BUCKETS):
                sel = jnp.where((idx == b)[None], dbg, 0.0)
                part = jnp.sum(sel, axis=1)
                val = jnp.sum(part, axis=-1, keepdims=True)
                acc = jnp.where(lane == g * NUM_BUCKETS + b, val, acc)
        o_ref[...] = acc

    return _pcall(body, name=name, out_shape=_sds((HEADS, 128), F32), compiler_params=_cparams())(dbias, bucket_idx)


def _scan16(x, reverse=False):
    row = lax.broadcasted_iota(jnp.int32, x.shape, 0)
    for sh in (1, 2, 4, 8):
        if reverse:
            x = x + jnp.where(row < HG_SUB - sh, pltpu.roll(x, HG_SUB - sh, 0), 0.0)
        else:
            x = x + jnp.where(row >= sh, pltpu.roll(x, sh, 0), 0.0)
    return x


def _hgrn_gates(qr, fr, lbv):
    q = _silu(qr)
    sig = _sigmoid(fr)
    fg = lbv + (1.0 - lbv) * sig
    lf = jnp.log(fg)
    gcum = _scan16(lf)
    glast = jnp.sum(lf, axis=0, keepdims=True)
    return q, sig, fg, 1.0 - fg, gcum, glast


def _hgrn_intra(q, k, gcum, tri):
    e = jnp.exp(jnp.where(tri, gcum[:, None, :] - gcum[None, :, :], NEG))
    a = jnp.sum(q[:, None, :] * k[None, :, :] * e, axis=-1, keepdims=True)
    return e, a


def _hgrn_fwd(proj4, lb, gain, name):
    nsub = HG_TC // HG_SUB
    wide = HG_HP * HEAD_DIM

    def body(p_ref, lb_ref, gn_ref, o_ref, y_ref, st_ref, state_s):
        @pl.when(pl.program_id(1) == 0)
        def _():
            state_s[...] = jnp.zeros_like(state_s)

        gnv = gn_ref[...]
        shp = (HG_SUB, HG_SUB, HEAD_DIM)
        tri = lax.broadcasted_iota(jnp.int32, shp, 0) >= lax.broadcasted_iota(jnp.int32, shp, 1)

        def head(qr, fr, vv, gr, lbv, st):
            q, _, _, k, gcum, glast = _hgrn_gates(qr, fr, lbv)
            _, a = _hgrn_intra(q, k, gcum, tri)
            o = jnp.sum(a * vv[None, :, :], axis=1) + _dot_nt((q * jnp.exp(gcum)).astype(BF16), st.astype(BF16))
            kg = k * jnp.exp(glast - gcum)
            st_new = st * jnp.exp(glast) + _dot_tn(vv.astype(BF16), kg.astype(BF16))
            rs = lax.rsqrt(jnp.mean(o * o, axis=-1, keepdims=True) + RMS_EPS)
            return o, (o * rs * gnv * _silu(gr)).astype(BF16), st_new

        def it(i, carry):
            rows = pl.ds(pl.multiple_of(i * HG_SUB, HG_SUB), HG_SUB)
            loaded = []
            for hh in range(HG_HP):
                lanes = pl.ds(hh * HEAD_DIM, HEAD_DIM)
                loaded.append(([p_ref[j, rows, lanes] for j in range(4)], lb_ref[:, lanes], state_s[hh]))
            results = [head(blk[0], blk[1], blk[2], blk[3], lbv, st) for blk, lbv, st in loaded]
            for hh, ((_, _, st), (o, y, st_new)) in enumerate(zip(loaded, results)):
                lanes = pl.ds(hh * HEAD_DIM, HEAD_DIM)
                st_ref[hh, i] = st.astype(BF16)
                state_s[hh] = st_new
                o_ref[rows, lanes] = o
                y_ref[hh // 2, rows, pl.ds((hh % 2) * HEAD_DIM, HEAD_DIM)] = y
            return carry

        lax.fori_loop(0, nsub, it, 0)

    return _pcall(
        body, name=name, grid=(HEADS // HG_HP, SEQ // HG_TC),
        in_specs=[pl.BlockSpec((4, HG_TC, wide), lambda h, j: (0, j, h)),
                  pl.BlockSpec((1, wide), lambda h, j: (0, h)),
                  pl.BlockSpec((1, HEAD_DIM), lambda h, j: (0, 0))],
        out_specs=[pl.BlockSpec((HG_TC, wide), lambda h, j: (j, h)),
                   pl.BlockSpec((HG_HP // 2, HG_TC, 2 * HEAD_DIM), lambda h, j: (h, j, 0)),
                   pl.BlockSpec((HG_HP, nsub, HEAD_DIM, HEAD_DIM), lambda h, j: (h, j, 0, 0))],
        out_shape=[_sds((SEQ, D_MODEL), F32), _sds((N_CHIP, SEQ, 2 * HEAD_DIM), BF16),
                   _sds((HEADS, SEQ // HG_SUB, HEAD_DIM, HEAD_DIM), BF16)],
        scratch_shapes=[pltpu.VMEM((HG_HP, HEAD_DIM, HEAD_DIM), F32)],
        compiler_params=_cparams(("parallel", "arbitrary")),
    )(proj4, lb, gain)


def _hgrn_bwd(proj4, lb, gain, o_raw, dy4, states, name):
    nsub = HG_TC // HG_SUB
    nt = SEQ // HG_TC
    wide = HG_HP * HEAD_DIM

    def body(p_ref, lb_ref, gn_ref, o_ref, dy_ref, st_ref, dp_ref, dlb_ref, dgn_ref, dst_s):
        @pl.when(pl.program_id(1) == 0)
        def _():
            dst_s[...] = jnp.zeros_like(dst_s)
            dlb_ref[...] = jnp.zeros_like(dlb_ref)
            dgn_ref[...] = jnp.zeros_like(dgn_ref)

        gnv = gn_ref[...]
        shp = (HG_SUB, HG_SUB, HEAD_DIM)
        tri = lax.broadcasted_iota(jnp.int32, shp, 0) >= lax.broadcasted_iota(jnp.int32, shp, 1)

        def head(qr, fr, vv, gr, o, dy, lbv, st0, dst):
            q, sig, fg, k, gcum, glast = _hgrn_gates(qr, fr, lbv)
            rs = lax.rsqrt(jnp.mean(o * o, axis=-1, keepdims=True) + RMS_EPS)
            oh = o * rs
            don = dy * _silu(gr)
            dgn = jnp.sum(don * oh, axis=0, keepdims=True)
            dgr = dy * oh * gnv * _dsilu(gr)
            doh = don * gnv
            do = rs * (doh - oh * jnp.mean(doh * oh, axis=-1, keepdims=True))
            dst16 = dst.astype(BF16)
            do16 = do.astype(BF16)
            eg = jnp.exp(gcum)
            eb = jnp.exp(glast - gcum)
            e, a = _hgrn_intra(q, k, gcum, tri)
            da = jnp.sum(do[:, None, :] * vv[None, :, :], axis=-1, keepdims=True)
            dae = da * e
            dq = jnp.sum(dae * k[None, :, :], axis=1) + eg * _dot(do16, st0)
            dk_state = eb * _dot(vv.astype(BF16), dst16)
            dk = jnp.sum(dae * q[:, None, :], axis=0) + dk_state
            dv = jnp.sum(a * do[:, None, :], axis=0) + _dot_nt((k * eb).astype(BF16), dst16)
            eglast = jnp.exp(glast)
            dst_new = dst * eglast + _dot_tn(do16, (q * eg).astype(BF16))
            dglast = jnp.sum(k * dk_state, axis=0, keepdims=True) \
                + eglast * jnp.sum(dst * st0.astype(F32), axis=0, keepdims=True)
            dlf = _scan16(q * dq - k * dk, reverse=True) + dglast
            dfg = dlf / fg - dk
            dlb = jnp.sum(dfg * (1.0 - sig), axis=0, keepdims=True)
            dproj = ((dq * _dsilu(qr)).astype(BF16), (dfg * (1.0 - lbv) * sig * (1.0 - sig)).astype(BF16),
                     dv.astype(BF16), dgr.astype(BF16))
            return dproj, dst_new, dlb, dgn

        def it(ii, carry):
            i = nsub - 1 - ii
            rows = pl.ds(pl.multiple_of(i * HG_SUB, HG_SUB), HG_SUB)
            results = []
            for hh in range(HG_HP):
                lanes = pl.ds(hh * HEAD_DIM, HEAD_DIM)
                blk = [p_ref[j, rows, lanes] for j in range(4)]
                dy = dy_ref[hh // 2, rows, pl.ds((hh % 2) * HEAD_DIM, HEAD_DIM)]
                results.append(head(blk[0], blk[1], blk[2], blk[3], o_ref[rows, lanes], dy,
                                    lb_ref[:, lanes], st_ref[hh, i], dst_s[hh]))
            new_carry = []
            for hh, (dproj, dst_new, dlb, dgn) in enumerate(results):
                lanes = pl.ds(hh * HEAD_DIM, HEAD_DIM)
                dst_s[hh] = dst_new
                for j in range(4):
                    dp_ref[j, rows, lanes] = dproj[j]
                new_carry.append((carry[hh][0] + dlb, carry[hh][1] + dgn))
            return tuple(new_carry)

        zero = jnp.zeros((1, HEAD_DIM), F32)
        sums = lax.fori_loop(0, nsub, it, tuple((zero, zero) for _ in range(HG_HP)))
        for hh in range(HG_HP):
            dlb_ref[hh] += sums[hh][0]
            dgn_ref[hh] += sums[hh][1]

    vspec = pl.BlockSpec((HG_HP, 1, HEAD_DIM), lambda h, j: (h, 0, 0))
    return _pcall(
        body, name=name, grid=(HEADS // HG_HP, nt),
        in_specs=[pl.BlockSpec((4, HG_TC, wide), lambda h, j: (0, nt - 1 - j, h)),
                  pl.BlockSpec((1, wide), lambda h, j: (0, h)),
                  pl.BlockSpec((1, HEAD_DIM), lambda h, j: (0, 0)),
                  pl.BlockSpec((HG_TC, wide), lambda h, j: (nt - 1 - j, h)),
                  pl.BlockSpec((HG_HP // 2, HG_TC, 2 * HEAD_DIM), lambda h, j: (h, nt - 1 - j, 0)),
                  pl.BlockSpec((HG_HP, nsub, HEAD_DIM, HEAD_DIM), lambda h, j: (h, nt - 1 - j, 0, 0))],
        out_specs=[pl.BlockSpec((4, HG_TC, wide), lambda h, j: (0, nt - 1 - j, h)), vspec, vspec],
        out_shape=[_sds((4, SEQ, D_MODEL), BF16), _sds((HEADS, 1, HEAD_DIM), F32), _sds((HEADS, 1, HEAD_DIM), F32)],
        scratch_shapes=[pltpu.VMEM((HG_HP, HEAD_DIM, HEAD_DIM), F32)],
        compiler_params=_cparams(("parallel", "arbitrary")),
    )(proj4, lb, gain, o_raw, dy4, states)


def _t5_bucket(dist):
    n = np.asarray(dist, dtype=np.int64)
    max_exact = NUM_BUCKETS // 2
    large = max_exact + (np.log(np.maximum(n, 1) / max_exact) / np.log(MAX_DISTANCE / max_exact)
                         * (NUM_BUCKETS - max_exact)).astype(np.int64)
    large = np.minimum(large, NUM_BUCKETS - 1)
    return np.where(n < max_exact, n, large).astype(np.int32)


def _bias_tables():
    qi = np.arange(ATT_BLK)[:, None]
    ki = np.arange(2 * ATT_BLK)[None, :]
    j = ATT_BLK + qi - ki
    valid = (j >= 0) & (j <= ATT_BLK)
    return np.stack([np.where(valid, _t5_bucket(np.clip(j, 0, ATT_BLK) * d), -1) for _, d in GROUPS]).astype(np.int32)


def _attn_bias(rel_bias, name):
    idx = _bias_tables()
    ng = len(GROUPS)
    buckets = [sorted(set(idx[g][idx[g] >= 0].tolist())) for g in range(ng)]

    def body(rb_ref, idx_ref, o_ref):
        h = pl.program_id(0)
        for g in range(ng):
            ig = idx_ref[g]
            acc = jnp.full(ig.shape, NEG, F32)
            for b in buckets[g]:
                acc = jnp.where(ig == b, rb_ref[b, g * HEADS + h], acc)
            o_ref[g] = acc

    return _pcall(
        body, name=name, grid=(HEADS,),
        in_specs=[pl.BlockSpec(memory_space=pltpu.SMEM),
                  pl.BlockSpec((ng, ATT_BLK, 2 * ATT_BLK), lambda h: (0, 0, 0))],
        out_specs=pl.BlockSpec((ng, None, ATT_BLK, 2 * ATT_BLK), lambda h: (0, h, 0, 0)),
        out_shape=_sds((ng, HEADS, ATT_BLK, 2 * ATT_BLK), F32),
        compiler_params=_cparams(("parallel",)),
    )(rel_bias, jnp.asarray(idx))


ADA_SHARD = 6 * D_MODEL // N_CHIP
ADA_TN = 512


def _ada_fwd(c_all, ada_w, ada_b_cols, name):
    def body(c_ref, w_ref, b_ref, o_ref):
        ca = _silu(c_ref[...]).astype(BF16)
        o_ref[...] = _dot(ca, w_ref[...].astype(BF16)) + b_ref[...]

    return _pcall(
        body, name=name, grid=(DEPTH, ADA_SHARD // ADA_TN),
        in_specs=[pl.BlockSpec((N_DEV, D_MODEL), lambda l, j: (0, 0)),
                  pl.BlockSpec((None, D_MODEL, ADA_TN), lambda l, j: (l, 0, j)),
                  pl.BlockSpec((None, 1, ADA_TN), lambda l, j: (l, 0, j))],
        out_specs=pl.BlockSpec((None, N_DEV, ADA_TN), lambda l, j: (l, 0, j)),
        out_shape=_sds((DEPTH, N_DEV, ADA_SHARD), F32),
        compiler_params=_cparams(("parallel", "parallel")),
    )(c_all, ada_w, ada_b_cols)


def _ada_bwd(c_all, dmod_cols, name):
    def body(c_ref, d_ref, o_ref):
        ca = _silu(c_ref[...]).astype(BF16)
        o_ref[...] = _dot_tn(ca, d_ref[...].astype(BF16))

    return _pcall(
        body, name=name, grid=(DEPTH, ADA_SHARD // ADA_TN),
        in_specs=[pl.BlockSpec((N_DEV, D_MODEL), lambda l, j: (0, 0)),
                  pl.BlockSpec((None, N_DEV, ADA_TN), lambda l, j: (l, 0, j))],
        out_specs=pl.BlockSpec((None, D_MODEL, ADA_TN), lambda l, j: (l, 0, j)),
        out_shape=_sds((DEPTH, D_MODEL, ADA_SHARD), F32),
        compiler_params=_cparams(("parallel", "parallel")),
    )(c_all, dmod_cols)


def _lower_bounds(logits, name):
    def body(l_ref, o_ref):
        l0 = l_ref[0:1, :]
        l1 = l_ref[1:2, :]
        mx = jnp.maximum(l0, l1)
        e0 = jnp.exp(l0 - mx)
        e1 = jnp.exp(l1 - mx)
        p0 = e0 / (e0 + e1)
        p1 = e1 / (e0 + e1)
        o_ref[0:1, :] = p0 - p0
        o_ref[1:2, :] = (p0 + p1) - p0

    return _pcall(body, name=name, out_shape=_sds((DEPTH, D_MODEL), F32), compiler_params=_cparams())(logits)


_R_DMOD = 0
_R_NMIX = 96
_R_NFFN = 112
_R_QG = 128
_R_KG = 152
_R_GN = 176
_R_LB = 184
_R_RB = 192
SMALL_ROWS = 200


def _small_totals(gathered, logits8, name):
    ng = len(GROUPS)

    def body(g_ref, l_ref, main_ref, gains_ref, dlb_ref, rb_ref):
        tot = g_ref[0]
        for dev in range(1, N_DEV):
            tot = tot + g_ref[dev]
        main_ref[...] = tot[0:_R_QG]
        gains_ref[...] = jnp.zeros_like(gains_ref)
        for g in range(ng):
            gains_ref[g:g + 1, :] = jnp.sum(tot[_R_QG + 8 * g:_R_QG + 8 * g + 8], axis=0, keepdims=True)
            gains_ref[ng + g:ng + g + 1, :] = jnp.sum(tot[_R_KG + 8 * g:_R_KG + 8 * g + 8], axis=0, keepdims=True)
        gains_ref[2 * ng:2 * ng + 1, :] = jnp.sum(tot[_R_GN:_R_GN + 8], axis=0, keepdims=True)
        rb_ref[...] = tot[_R_RB:_R_RB + 8]
        dlb1 = tot[_R_LB:_R_LB + 8]
        l0 = l_ref[0]
        l1 = l_ref[1]
        mx = jnp.maximum(l0, l1)
        e0 = jnp.exp(l0 - mx)
        e1 = jnp.exp(l1 - mx)
        p0 = e0 / (e0 + e1)
        p1 = e1 / (e0 + e1)
        dlb_ref[0] = -p0 * p1 * dlb1
        dlb_ref[1] = p1 * (1.0 - p1) * dlb1

    return _pcall(
        body, name=name,
        out_shape=[_sds((_R_QG, 128), F32), _sds((8, 128), F32), _sds((DEPTH, 8, 128), F32), _sds((8, 128), F32)],
        compiler_params=_cparams(),
    )(gathered, logits8)


def _row_tile(rows):
    return 128 if rows % 128 == 0 else rows


def _adamw(w, grads, m, v, name):
    nl, r, cdim = w.shape
    tr = _row_tile(r)

    def body(*refs):
        g_refs = refs[:nl]
        w_ref, m_ref, v_ref, go_ref, d_ref, mo_ref, vo_ref = refs[nl:]

        def step(g):
            m2 = ADAM_B1 * m_ref[...] + (1.0 - ADAM_B1) * g
            v2 = ADAM_B2 * v_ref[...] + (1.0 - ADAM_B2) * (g * g)
            m_hat = m2 / (1.0 - ADAM_B1 ** ADAM_STEP)
            v_hat = v2 / (1.0 - ADAM_B2 ** ADAM_STEP)
            go_ref[...] = g
            d_ref[...] = -ADAM_LR * (m_hat / (jnp.sqrt(v_hat) + ADAM_EPS) + ADAM_WD * w_ref[...])
            mo_ref[...] = m2
            vo_ref[...] = v2

        if nl == 1:
            step(g_refs[0][...])
        else:
            for layer in range(nl):
                @pl.when(pl.program_id(0) == layer)
                def _(layer=layer):
                    step(g_refs[layer][...])

    big = pl.BlockSpec((None, tr, cdim), lambda l, i: (l, i, 0))
    g_specs = [pl.BlockSpec((tr, cdim), lambda l, i, layer=layer: (jnp.where(l == layer, i, 0), 0))
               for layer in range(nl)]
    shp = _sds((nl, r, cdim), F32)
    return _pcall(
        body, name=name, grid=(nl, r // tr),
        in_specs=g_specs + [big, big, big],
        out_specs=[big, big, big, big],
        out_shape=[shp, shp, shp, shp],
        compiler_params=_cparams(("parallel", "parallel")),
    )(*grads, w, m, v)


def _cast_bf16(place, w, name):
    nl, r, cdim = w.shape
    tr = _row_tile(r)

    def body(place_ref, w_ref, o_ref):
        o_ref[...] = w_ref[...].astype(BF16)

    return _pcall(
        body, name=name,
        grid_spec=pltpu.PrefetchScalarGridSpec(
            num_scalar_prefetch=1, grid=(nl, r // tr),
            in_specs=[pl.BlockSpec((None, tr, cdim), lambda l, i, place_ref: (l, i, 0))],
            out_specs=pl.BlockSpec((None, None, tr, cdim), lambda l, i, place_ref: (place_ref[1], l, i, 0))),
        out_shape=_sds((N_CHIP, nl, r, cdim), BF16),
        compiler_params=_cparams(("parallel", "parallel")),
    )(place, w)


def _rs_add_cast(place, grad, recv, name):
    _, k, n = grad.shape
    kh = k // 2
    tr = _row_tile(kh)
    nb = kh // tr

    def body(place_ref, g_ref, r_ref, o_ref):
        o_ref[...] = (g_ref[...] + r_ref[...]).astype(BF16)

    half = pl.BlockSpec((None, tr, n), lambda s, i, place_ref: (s, i, 0))
    return _pcall(
        body, name=name,
        grid_spec=pltpu.PrefetchScalarGridSpec(
            num_scalar_prefetch=1, grid=(N_CHIP, nb),
            in_specs=[pl.BlockSpec((None, tr, n), lambda s, i, place_ref: (s, place_ref[0] * nb + i, 0)), half],
            out_specs=half),
        out_shape=_sds((N_CHIP, kh, n), BF16),
        compiler_params=_cparams(("parallel", "parallel")),
    )(place, grad, recv)


def _rs_sum4(place, parts, got, name):
    _, kh, n = parts.shape
    tr = _row_tile(kh)
    nb = kh // tr

    def body(place_ref, p_ref, g_ref, o_ref):
        acc = p_ref[...].astype(F32)
        for j in range(N_CHIP - 1):
            acc = acc + g_ref[j].astype(F32)
        o_ref[...] = acc

    return _pcall(
        body, name=name,
        grid_spec=pltpu.PrefetchScalarGridSpec(
            num_scalar_prefetch=1, grid=(nb,),
            in_specs=[pl.BlockSpec((None, tr, n), lambda i, place_ref: (place_ref[1], i, 0)),
                      pl.BlockSpec((N_CHIP - 1, tr, n), lambda i, place_ref: (0, i, 0))],
            out_specs=pl.BlockSpec((tr, n), lambda i, place_ref: (place_ref[0] * nb + i, 0))),
        out_shape=_sds((2 * kh, n), F32),
        compiler_params=_cparams(("parallel",)),
    )(place, parts, got)


_ANY = pl.BlockSpec(memory_space=pl.ANY)


def _position():
    return lax.axis_index("x"), lax.axis_index("y"), lax.axis_index("c")


def _other_chips(x, y):
    return [(1 - x, y), (x, 1 - y), (1 - x, 1 - y)]


def _remote(src, dst, send_sem, recv_sem, to):
    return pltpu.make_async_remote_copy(src_ref=src, dst_ref=dst, send_sem=send_sem, recv_sem=recv_sem,
                                        device_id=to, device_id_type=MESH)


def _small_allgather(v, name):
    r = v.shape[0]

    def body(x_ref, out_ref, send_sems, recv_sems, local_sem):
        x, y, c = _position()
        me, sibling = (x, y, c), (x, y, 1 - c)
        chips = _other_chips(x, y)

        def slab(px, py, pc):
            return out_ref.at[4 * px + 2 * py + pc]

        def copy(k, block, to, src=None):
            return _remote(slab(*block) if src is None else src, slab(*block), send_sems.at[k], recv_sems.at[k], to)

        mine = pltpu.make_async_copy(x_ref, slab(*me), local_sem)
        mine.start()
        first = [copy(0, me, sibling, src=x_ref)]
        first += [copy(1 + j, me, (*chip, c), src=x_ref) for j, chip in enumerate(chips)]
        for cp in first:
            cp.start()
        passed = [copy(4 + j, (*chip, c), sibling) for j, chip in enumerate(chips)]
        for j, chip in enumerate(chips):
            copy(1 + j, (*chip, c), me).wait_recv()
            passed[j].start()
        copy(0, sibling, me).wait_recv()
        for j, chip in enumerate(chips):
            copy(4 + j, (*chip, 1 - c), me).wait_recv()
        for cp in first + passed:
            cp.wait_send()
        mine.wait()

    return _pcall(
        body, name=name,
        out_shape=_sds((N_DEV, r, 128), F32),
        in_specs=[pl.BlockSpec(memory_space=pltpu.VMEM)],
        out_specs=pl.BlockSpec(memory_space=pltpu.VMEM),
        scratch_shapes=[pltpu.SemaphoreType.DMA((7,)), pltpu.SemaphoreType.DMA((7,)), pltpu.SemaphoreType.DMA],
        compiler_params=_cparams(),
    )(v)


def _half_rows(core, kh):
    return pl.ds(pl.multiple_of(core * kh, 8), kh)


def _slab_half(ref, chip, core):
    return ref.at[chip, :, _half_rows(core, ref.shape[2] // 2), :]


def _gather_ici(out, send_sems, recv_sems):
    def copies():
        x, y, c = _position()
        for a in range(len(out)):
            for j, (px, py) in enumerate(_other_chips(x, y)):
                mine = _slab_half(out[a], 2 * x + y, c)
                landed = _slab_half(out[a], 2 * px + py, c)
                yield (_remote(mine, mine, send_sems.at[a, j], recv_sems.at[a, j], (px, py, c)),
                       _remote(landed, landed, send_sems.at[a, j], recv_sems.at[a, j], (px, py, c)))

    def start():
        for send, _ in copies():
            send.start()

    def wait():
        for send, recv in copies():
            recv.wait_recv()
            send.wait_send()

    return start, wait


def _gather_d2d(out, send_sems, recv_sems):
    def copies():
        x, y, c = _position()
        for a in range(len(out)):
            for j, (px, py) in enumerate(_other_chips(x, y)):
                landed = _slab_half(out[a], 2 * px + py, c)
                other = _slab_half(out[a], 2 * px + py, 1 - c)
                yield (_remote(landed, landed, send_sems.at[a, j], recv_sems.at[a, j], (x, y, 1 - c)),
                       _remote(other, other, send_sems.at[a, j], recv_sems.at[a, j], (x, y, 1 - c)))

    def start():
        for send, _ in copies():
            send.start()

    def wait():
        for send, recv in copies():
            recv.wait_recv()
            send.wait_send()

    return start, wait


def _gather_weights(slabs, name, ici=True):
    n = len(slabs)

    def body(*refs):
        out = refs[n:2 * n]
        sems = refs[2 * n:]
        if ici:
            start, wait = _gather_ici(out, sems[2], sems[3])
            start()
            wait()
        start, wait = _gather_d2d(out, sems[0], sems[1])
        start()
        wait()

    sem = pltpu.SemaphoreType.DMA((n, 3))
    return _pcall(
        body, name=name,
        out_shape=[_sds(s.shape, BF16) for s in slabs],
        in_specs=[_ANY] * n, out_specs=[_ANY] * n,
        input_output_aliases={a: a for a in range(n)},
        scratch_shapes=[sem, sem] + ([sem, sem] if ici else []),
        compiler_params=_cparams(),
    )(*slabs)


def _rs_halves(grads, out, send_sems, recv_sems):
    def copies():
        x, y, c = _position()
        for a in range(len(grads)):
            kh = grads[a].shape[1] // 2
            yield _remote(grads[a].at[:, _half_rows(1 - c, kh), :], out[a], send_sems.at[a], recv_sems.at[a],
                          (x, y, 1 - c))

    def start():
        for cp in copies():
            cp.start()

    def wait():
        for cp in copies():
            cp.wait()

    return start, wait


def _rs_halves_shapes(grads):
    return [_sds((N_CHIP, g.shape[1] // 2, g.shape[2]), F32) for g in grads]


def _rs_exchange_halves(grads, name):
    n = len(grads)

    def body(*refs):
        start, wait = _rs_halves(refs[:n], refs[n:2 * n], *refs[2 * n:])
        start()
        wait()

    return _pcall(
        body, name=name,
        out_shape=_rs_halves_shapes(grads),
        in_specs=[_ANY] * n, out_specs=[_ANY] * n,
        scratch_shapes=[pltpu.SemaphoreType.DMA((n,)), pltpu.SemaphoreType.DMA((n,))],
        compiler_params=_cparams(),
    )(*grads)


def _rs_chips(parts, out, send_sems, recv_sems):
    def copies():
        x, y, c = _position()
        for a in range(len(parts)):
            for j, (px, py) in enumerate(_other_chips(x, y)):
                got = out[a].at[j]
                yield (_remote(parts[a].at[2 * px + py], got, send_sems.at[a, j], recv_sems.at[a, j], (px, py, c)),
                       _remote(got, got, send_sems.at[a, j], recv_sems.at[a, j], (px, py, c)))

    def start():
        for send, _ in copies():
            send.start()

    def wait():
        for send, recv in copies():
            recv.wait_recv()
            send.wait_send()

    return start, wait


def _rs_chips_shapes(parts):
    return [_sds((N_CHIP - 1,) + p.shape[1:], BF16) for p in parts]


def _rs_join_halves(fulls, name):
    n = len(fulls)

    def body(*refs):
        out = refs[n:2 * n]
        send_sems, recv_sems = refs[2 * n:]
        x, y, c = _position()
        copies = []
        for a in range(n):
            kh = out[a].shape[0] // 2
            mine = out[a].at[_half_rows(c, kh), :]
            cp = _remote(mine, mine, send_sems.at[a], recv_sems.at[a], (x, y, 1 - c))
            cp.start()
            copies.append(cp)
        for a in range(n):
            kh = out[a].shape[0] // 2
            theirs = out[a].at[_half_rows(1 - c, kh), :]
            _remote(theirs, theirs, send_sems.at[a], recv_sems.at[a], (x, y, 1 - c)).wait_recv()
        for cp in copies:
            cp.wait_send()

    return _pcall(
        body, name=name,
        out_shape=[_sds(f.shape, F32) for f in fulls],
        in_specs=[_ANY] * n, out_specs=[_ANY] * n,
        input_output_aliases={a: a for a in range(n)},
        scratch_shapes=[pltpu.SemaphoreType.DMA((n,)), pltpu.SemaphoreType.DMA((n,))],
        compiler_params=_cparams(),
    )(*fulls)


_SMALL_ORDER = ("rel_bias", "ada_b", "norm_mix", "norm_ffn", "attn_q_gain", "attn_k_gain", "hgrn_gnorm",
                "hgrn_lower_bounds")
_WEIGHT_ORDER = ("rel_bias", "ada_w", "ada_b", "norm_mix", "norm_ffn", "attn_w_qkv", "attn_w_out", "attn_q_gain",
                 "attn_k_gain", "hgrn_w_in", "hgrn_w_out", "hgrn_gnorm", "hgrn_lower_bounds", "ffn_w1", "ffn_w3",
                 "ffn_w2")


def _qkv_group_map(t):
    return t // 4, t % 4


def _qkv_chip_map(t):
    return t // 9, t % 9


def _hin_map(t):
    return t // 2, t % 2


def _block_map(t):
    return t, 0


def _pack_rows(parts):
    return jnp.concatenate([p.reshape(-1, 128) for p in parts], axis=0)


def kernel(x, c, rel_bias, ada_w, ada_b, norm_mix, norm_ffn, attn_w_qkv, attn_w_out, attn_q_gain, attn_k_gain, hgrn_w_in, hgrn_w_out, hgrn_gnorm, hgrn_lower_bounds, ffn_w1, ffn_w3, ffn_w2, loss_target, m_rel_bias, m_ada_w, m_ada_b, m_norm_mix, m_norm_ffn, m_attn_w_qkv, m_attn_w_out, m_attn_q_gain, m_attn_k_gain, m_hgrn_w_in, m_hgrn_w_out, m_hgrn_gnorm, m_hgrn_lower_bounds, m_ffn_w1, m_ffn_w3, m_ffn_w2, v_rel_bias, v_ada_w, v_ada_b, v_norm_mix, v_norm_ffn, v_attn_w_qkv, v_attn_w_out, v_attn_q_gain, v_attn_k_gain, v_hgrn_w_in, v_hgrn_w_out, v_hgrn_gnorm, v_hgrn_lower_bounds, v_ffn_w1, v_ffn_w3, v_ffn_w2):
    weights = dict(rel_bias=rel_bias, ada_w=ada_w, ada_b=ada_b, norm_mix=norm_mix, norm_ffn=norm_ffn,
                   attn_w_qkv=attn_w_qkv, attn_w_out=attn_w_out, attn_q_gain=attn_q_gain, attn_k_gain=attn_k_gain,
                   hgrn_w_in=hgrn_w_in, hgrn_w_out=hgrn_w_out, hgrn_gnorm=hgrn_gnorm,
                   hgrn_lower_bounds=hgrn_lower_bounds, ffn_w1=ffn_w1, ffn_w3=ffn_w3, ffn_w2=ffn_w2)
    mom1 = dict(rel_bias=m_rel_bias, ada_w=m_ada_w, ada_b=m_ada_b, norm_mix=m_norm_mix, norm_ffn=m_norm_ffn,
                attn_w_qkv=m_attn_w_qkv, attn_w_out=m_attn_w_out, attn_q_gain=m_attn_q_gain,
                attn_k_gain=m_attn_k_gain, hgrn_w_in=m_hgrn_w_in, hgrn_w_out=m_hgrn_w_out, hgrn_gnorm=m_hgrn_gnorm,
                hgrn_lower_bounds=m_hgrn_lower_bounds, ffn_w1=m_ffn_w1, ffn_w3=m_ffn_w3, ffn_w2=m_ffn_w2)
    mom2 = dict(rel_bias=v_rel_bias, ada_w=v_ada_w, ada_b=v_ada_b, norm_mix=v_norm_mix, norm_ffn=v_norm_ffn,
                attn_w_qkv=v_attn_w_qkv, attn_w_out=v_attn_w_out, attn_q_gain=v_attn_q_gain,
                attn_k_gain=v_attn_k_gain, hgrn_w_in=v_hgrn_w_in, hgrn_w_out=v_hgrn_w_out, hgrn_gnorm=v_hgrn_gnorm,
                hgrn_lower_bounds=v_hgrn_lower_bounds, ffn_w1=v_ffn_w1, ffn_w3=v_ffn_w3, ffn_w2=v_ffn_w2)

    transposed = ("ffn_w1", "ffn_w3")
    for group in (weights, mom1, mom2):
        for k in transposed:
            group[k] = jnp.transpose(group[k], (0, 2, 1))

    xi, yi, ci = _position()
    chip = 2 * xi + yi
    dev = 4 * xi + 2 * yi + ci
    place = jnp.stack([ci, chip]).astype(jnp.int32)
    d = D_MODEL

    big_names = ("attn_w_qkv", "attn_w_out", "hgrn_w_in", "hgrn_w_out", "ffn_w1", "ffn_w3", "ffn_w2")
    early_names, late_names = big_names[:1], big_names[1:]
    slabs16 = {k: _cast_bf16(place, weights[k], "cast_" + k) for k in big_names}
    wg = dict(zip(early_names, _gather_weights([slabs16[k] for k in early_names], "gather_early")))

    c_all = _small_allgather(c.reshape(8, 128), "gather_c").reshape(N_DEV, d)
    ada_b_cols = lax.dynamic_slice(ada_b, (0, chip * ADA_SHARD), (DEPTH, ADA_SHARD)).reshape(DEPTH, 1, ADA_SHARD)
    mod_shard = _ada_fwd(c_all, ada_w, ada_b_cols, "ada_fwd")
    mod_all = _small_allgather(mod_shard.reshape(-1, 128), "gather_mod").reshape(N_DEV, DEPTH, N_DEV, ADA_SHARD)
    mod_mine = lax.dynamic_index_in_dim(mod_all[0::2], dev, axis=2, keepdims=False)
    mod = jnp.transpose(mod_mine, (1, 0, 2)).reshape(DEPTH, 6 * d)

    def mods(layer):
        return [mod[layer:layer + 1, j * d:(j + 1) * d] for j in range(6)]

    x0 = x.reshape(SEQ, d)
    target = loss_target.reshape(SEQ, d)
    qg = attn_q_gain.reshape(len(GROUPS), 1, HEAD_DIM)
    kg = attn_k_gain.reshape(len(GROUPS), 1, HEAD_DIM)
    bias = _attn_bias(rel_bias, "attn_bias")
    lb1 = _lower_bounds(hgrn_lower_bounds, "lower_bounds")[1:2]

    def ffn_fwd(layer, x_in, sc2, sh2, g2):
        hf = _norm_mod(x_in, norm_ffn[layer:layer + 1], sc2, sh2, f"l{layer}_norm_ffn")
        a1, a3, u = _ffn_up(hf, wg["ffn_w1"], wg["ffn_w3"], layer, f"l{layer}_ffn_up")
        z, x_out = _mm_rows(u, wg["ffn_w2"], layer, x_in, g2, f"l{layer}_ffn_down")
        return x_out, (hf, a1, a3, u, z)

    def ffn_bwd(layer, dx_out, x_in, sc2, sh2, g2, saved, halves=()):
        hf, a1, a3, u, z = saved
        dz, dg2 = _gate_bwd(dx_out, z, g2, f"l{layer}_ffn_gate_bwd")
        da1, da3, *recv = _ffn_down_bwd(dz, wg["ffn_w2"], layer, a1, a3, f"l{layer}_ffn_down_bwd", halves=halves)
        dw2 = _mm_rows_bwd_w(u, dz, f"l{layer}_dw2")
        dh = _ffn_up_bwd(da1, da3, wg["ffn_w1"], wg["ffn_w3"], layer, f"l{layer}_ffn_up_bwd")
        dw1 = _mm_rows_bwd_w(da1, hf, f"l{layer}_dw1")
        dw3 = _mm_rows_bwd_w(da3, hf, f"l{layer}_dw3")
        dx_in, dsc2, dsh2, dnf = _norm_mod_bwd(x_in, norm_ffn[layer:layer + 1], sc2, sh2, dh, dx_out,
                                               f"l{layer}_norm_ffn_bwd")
        return dx_in, (dw1, dw3, dw2), (dsh2, dsc2, dg2), dnf, recv

    def rs_add(tags, grads_in, recv):
        return [_rs_add_cast(place, g, r, f"rs_add_{k}_{layer}") for (k, layer), g, r in zip(tags, grads_in, recv)]

    sh1_0, sc1_0, g1_0, sh2_0, sc2_0, g2_0 = mods(0)
    h0 = _norm_mod(x0, norm_mix[0:1], sc1_0, sh1_0, "l0_norm_mix")
    w_qkv9 = _retile_cols(wg["attn_w_qkv"].reshape(N_CHIP, d, 2304), n_out=9, width_out=d, tn=256,
                          src_map=_qkv_chip_map, dst_map=_qkv_group_map, n_tiles=36,
                          name="regroup_w_qkv").reshape(9, 1, d, d)
    qkv9 = _mm_cols(h0, w_qkv9, 0, n_blocks=9, width=d, tn=d, act_map=_block_map, w_map=_block_map,
                    out_dtype=F32, name="l0_qkv")
    o4, lse, *late = _attn_fwd(qkv9, qg, kg, bias, "l0_attn", gather=[slabs16[k] for k in late_names])
    wg.update(zip(late_names, _gather_weights(late, "gather_late_siblings", ici=False)))
    y0, x1 = _mm_rows(o4, wg["attn_w_out"], 0, x0, g1_0, "l0_attn_out")
    x2, ffn0 = ffn_fwd(0, x1, sc2_0, sh2_0, g2_0)

    sh1_1, sc1_1, g1_1, sh2_1, sc2_1, g2_1 = mods(1)
    h1 = _norm_mod(x2, norm_mix[1:2], sc1_1, sh1_1, "l1_norm_mix")
    proj4 = _mm_cols(h1, wg["hgrn_w_in"], 0, n_blocks=4, width=d, tn=512, act_map=_hin_map, w_map=_hin_map,
                     out_dtype=F32, name="l1_hgrn_in")
    o_raw, yg4, states = _hgrn_fwd(proj4, lb1, hgrn_gnorm, "l1_hgrn")
    y1, x3 = _mm_rows(yg4, wg["hgrn_w_out"], 0, x2, g1_1, "l1_hgrn_out")
    x4, ffn1 = ffn_fwd(1, x3, sc2_1, sh2_1, g2_1)

    dx4, loss_part = _loss_head(x4, target, "loss_head")
    loss = lax.psum(loss_part[0, 0], ("x", "y", "c"))

    dx3, (dw1_1, dw3_1, dw2_1), dmod2_1, dnf_1, _ = ffn_bwd(1, dx4, x3, sc2_1, sh2_1, g2_1, ffn1)
    dzm1, dg1_1 = _gate_bwd(dx3, y1, g1_1, "l1_mix_gate_bwd")
    dyg4 = _mm_rows_bwd_a(dzm1, wg["hgrn_w_out"], 0, "l1_hgrn_out_bwd")
    dw_hout = _mm_rows_bwd_w(yg4, dzm1, "l1_dw_hgrn_out")
    dproj4, dlb_h, dgn_h = _hgrn_bwd(proj4, lb1, hgrn_gnorm, o_raw, dyg4, states, "l1_hgrn_bwd")
    dh1 = _mm_cols_bwd_a(dproj4, wg["hgrn_w_in"], 0, group=N_CHIP, name="l1_hgrn_in_bwd", tm=512)
    dw_hin = _mm_cols_bwd_w(h1, dproj4, ns=d, tn=d, act_map=_block_map, w_map=_block_map, n_tiles=N_CHIP,
                            name="l1_dw_hgrn_in", tm=2048)
    dx2, dsc1_1, dsh1_1, dnm_1 = _norm_mod_bwd(x2, norm_mix[1:2], sc1_1, sh1_1, dh1, dx3, "l1_norm_mix_bwd")

    tags_1 = [("hgrn_w_in", 0), ("hgrn_w_out", 0), ("ffn_w1", 1), ("ffn_w3", 1), ("ffn_w2", 1)]
    grads_1 = [dw_hin, dw_hout, dw1_1, dw3_1, dw2_1]
    dx1, (dw1_0, dw3_0, dw2_0), dmod2_0, dnf_0, recv_1 = ffn_bwd(0, dx2, x1, sc2_0, sh2_0, g2_0, ffn0,
                                                                halves=grads_1)
    tags_0 = [("ffn_w1", 0), ("ffn_w3", 0), ("ffn_w2", 0)]
    grads_0 = [dw1_0, dw3_0, dw2_0]
    dzm0, dg1_0 = _gate_bwd(dx1, y0, g1_0, "l0_mix_gate_bwd")
    do4, *recv_0 = _mm_rows_bwd_a(dzm0, wg["attn_w_out"], 0, "l0_attn_out_bwd", halves=grads_0)
    dw_aout = _mm_rows_bwd_w(o4, dzm0, "l0_dw_attn_out")
    tags_a = tags_1 + tags_0
    parts_a = rs_add(tags_1, grads_1, recv_1) + rs_add(tags_0, grads_0, recv_0)
    dqkv, dqg_h, dkg_h, dbias, *got_a = _attn_bwd(qkv9, qg, kg, bias, do4, o4, lse, "l0_attn_bwd", scatter=parts_a)
    dqkv9 = dqkv.reshape(9, SEQ, d)
    dw_qkv9 = _mm_cols_bwd_w(h0, dqkv9, ns=d, tn=d, act_map=_block_map, w_map=_block_map, n_tiles=9,
                             name="l0_dw_qkv", tm=2048, n_out=9)
    dw_qkv = _retile_cols(dw_qkv9, n_out=N_CHIP, width_out=2304, tn=256, src_map=_qkv_group_map,
                          dst_map=_qkv_chip_map, n_tiles=36, name="regroup_dw_qkv")
    tags_b = [("attn_w_qkv", 0), ("attn_w_out", 0)]
    grads_b = [dw_qkv, dw_aout]
    parts_b = rs_add(tags_b, grads_b, _rs_exchange_halves(grads_b, "rs_exchange_halves_b"))
    dh0, *got_b = _mm_cols_bwd_a(dqkv9, w_qkv9, 0, group=3, name="l0_qkv_bwd", scatter=parts_b)
    dx0, dsc1_0, dsh1_0, dnm_0 = _norm_mod_bwd(x0, norm_mix[0:1], sc1_0, sh1_0, dh0, dx1, "l0_norm_mix_bwd")
    drb8 = _relbias_bwd(dbias, jnp.asarray(_bias_tables()), "rel_bias_bwd")

    small = _pack_rows([
        dsh1_0, dsc1_0, dg1_0, *dmod2_0, dsh1_1, dsc1_1, dg1_1, *dmod2_1,
        dnm_0, dnm_1, dnf_0, dnf_1,
        jnp.transpose(dqg_h, (1, 0, 2, 3)), jnp.transpose(dkg_h, (1, 0, 2, 3)), dgn_h, dlb_h, drb8])
    small_all = _small_allgather(small, "gather_small")
    main, gains, dlbnd, rbt = _small_totals(small_all, hgrn_lower_bounds.reshape(DEPTH, 8, 128), "small_totals")
    ng = len(GROUPS)
    grads = {
        "ada_b": main[_R_DMOD:_R_NMIX].reshape(DEPTH, 6 * d),
        "norm_mix": main[_R_NMIX:_R_NFFN].reshape(DEPTH, d),
        "norm_ffn": main[_R_NFFN:_R_QG].reshape(DEPTH, d),
        "attn_q_gain": gains[0:ng].reshape(1, ng, HEAD_DIM),
        "attn_k_gain": gains[ng:2 * ng].reshape(1, ng, HEAD_DIM),
        "hgrn_gnorm": gains[2 * ng:2 * ng + 1],
        "hgrn_lower_bounds": dlbnd.reshape(DEPTH, d),
        "rel_bias": jnp.transpose(rbt[:, :ng * NUM_BUCKETS].reshape(HEADS, ng, NUM_BUCKETS), (2, 1, 0))
                       .reshape(NUM_BUCKETS, ng * HEADS),
    }
    dmod_all = small_all[:, _R_DMOD:_R_NMIX].reshape(N_DEV, DEPTH, 6 * d)
    dmod_cols = jnp.transpose(lax.dynamic_slice(dmod_all, (0, 0, chip * ADA_SHARD), (N_DEV, DEPTH, ADA_SHARD)),
                              (1, 0, 2))
    grad_ada_w = _ada_bwd(c_all, dmod_cols, "ada_bwd")

    tags = tags_a + tags_b
    halves = [_rs_sum4(place, p, r, f"rs_sum_{k}_{layer}")
              for (k, layer), p, r in zip(tags, parts_a + parts_b, list(got_a) + list(got_b))]
    full = dict(zip(tags, _rs_join_halves(halves, "rs_join_halves")))

    out_g, out_d, out_m, out_v = {}, {}, {}, {}
    for k in big_names:
        gs = [full[(k, layer)] for layer in range(weights[k].shape[0])]
        out_g[k], out_d[k], out_m[k], out_v[k] = _adamw(weights[k], gs, mom1[k], mom2[k], "adamw_" + k)
    shp = (1, DEPTH * d, ADA_SHARD)
    res = _adamw(ada_w.reshape(shp), [grad_ada_w.reshape(shp[1:])], m_ada_w.reshape(shp), v_ada_w.reshape(shp),
                 "adamw_ada_w")
    out_g["ada_w"], out_d["ada_w"], out_m["ada_w"], out_v["ada_w"] = [r.reshape(ada_w.shape) for r in res]
    packed = [_pack_rows([src[k] for k in _SMALL_ORDER])[None] for src in (weights, grads, mom1, mom2)]
    res = _adamw(packed[0], [packed[1][0]], packed[2], packed[3], "adamw_small")
    offset = 0
    for k in _SMALL_ORDER:
        size = weights[k].size
        for dst, r in zip((out_g, out_d, out_m, out_v), res):
            dst[k] = r.reshape(-1)[offset:offset + size].reshape(weights[k].shape)
        offset += size
    for dst in (out_g, out_d, out_m, out_v):
        for k in transposed:
            dst[k] = jnp.transpose(dst[k], (0, 2, 1))

    return (loss, dx0.reshape(x.shape), *[out_g[k] for k in _WEIGHT_ORDER], *[out_d[k] for k in _WEIGHT_ORDER],
            *[out_m[k] for k in _WEIGHT_ORDER], *[out_v[k] for k in _WEIGHT_ORDER])
```

```python
import functools

import numpy as np
import jax
import jax.numpy as jnp
from jax import lax
from jax.experimental import pallas as pl
from jax.experimental.pallas import tpu as pltpu

F32 = jnp.float32
BF16 = jnp.bfloat16

D_MODEL = 1024
SEQ = 4096
N_DEV = 8
N_CHIP = 4
DEPTH = 2
HEADS = 8
HEAD_DIM = 128
GROUPS = ((128, 1), (512, 4), (2048, 16))
ATT_BLK = 128
ATT_WAYS = 4
ATT_STEPS = SEQ // ATT_BLK // ATT_WAYS
NUM_BUCKETS = 32
MAX_DISTANCE = 2048
FFN_HIDDEN = 2816
FFN_SHARD = FFN_HIDDEN // N_CHIP
HG_SUB = 16
HG_TC = 512
HG_HP = 4
RMS_EPS = 1e-6
NEG = -1e30
ATT_SCALE = HEAD_DIM ** -0.5
ADAM_LR, ADAM_B1, ADAM_B2, ADAM_EPS, ADAM_WD, ADAM_STEP = 0.001, 0.9, 0.999, 1e-08, 0.01, 10
VMEM_LIMIT = 56 * 1024 * 1024
MESH = pl.DeviceIdType.MESH


def _pcall(body, **kw):
    return pl.pallas_call(body, **kw)


def _cparams(sem=None):
    if sem is None:
        return pltpu.CompilerParams(vmem_limit_bytes=VMEM_LIMIT)
    return pltpu.CompilerParams(dimension_semantics=sem, vmem_limit_bytes=VMEM_LIMIT)


def _sds(shape, dtype):
    return jax.ShapeDtypeStruct(shape, dtype)


def _dot(a, b):
    return jnp.dot(a, b, preferred_element_type=F32)


def _dot_nt(a, b):
    return lax.dot_general(a, b, (((1,), (1,)), ((), ())), preferred_element_type=F32)


def _dot_tn(a, b):
    return lax.dot_general(a, b, (((0,), (0,)), ((), ())), preferred_element_type=F32)


def _sigmoid(x):
    return 1.0 / (1.0 + jnp.exp(-x))


def _silu(x):
    return x * _sigmoid(x)


def _dsilu(x):
    s = _sigmoid(x)
    return s * (1.0 + x * (1.0 - s))


def _norm_mod(x, gain, sc, sh, name):
    tm = 512

    def body(x_ref, g_ref, sc_ref, sh_ref, h_ref):
        xv = x_ref[...]
        rs = lax.rsqrt(jnp.mean(xv * xv, axis=-1, keepdims=True) + RMS_EPS)
        h_ref[...] = ((xv * rs * g_ref[...]) * (1.0 + sc_ref[...]) + sh_ref[...]).astype(BF16)

    vec = pl.BlockSpec((1, D_MODEL), lambda i: (0, 0))
    return _pcall(
        body, name=name, grid=(SEQ // tm,),
        in_specs=[pl.BlockSpec((tm, D_MODEL), lambda i: (i, 0)), vec, vec, vec],
        out_specs=pl.BlockSpec((tm, D_MODEL), lambda i: (i, 0)),
        out_shape=_sds((SEQ, D_MODEL), BF16),
        compiler_params=_cparams(("parallel",)),
    )(x, gain, sc, sh)


def _gated_branch_bwd(dx, z_ref, gate_ref, dz_ref, dgate_ref):
    dz_ref[...] = (dx * gate_ref[...]).astype(BF16)
    dgate_ref[...] += jnp.sum(dx * z_ref[...], axis=0, keepdims=True)


def _norm_mod_bwd(x, gain, sc, sh, dh, dres, name, branch=None):
    tm = 512
    n_b = 2 if branch else 0

    def body(*refs):
        x_ref, g_ref, sc_ref, sh_ref, dh_ref, dres_ref = refs[:6]
        dx_ref, dsc_ref, dsh_ref, dg_ref = refs[6 + n_b:10 + n_b]

        @pl.when(pl.program_id(0) == 0)
        def _():
            dsc_ref[...] = jnp.zeros_like(dsc_ref)
            dsh_ref[...] = jnp.zeros_like(dsh_ref)
            dg_ref[...] = jnp.zeros_like(dg_ref)
            if branch:
                refs[11 + n_b][...] = jnp.zeros_like(refs[11 + n_b])

        xv = x_ref[...]
        dhv = dh_ref[...]
        rs = lax.rsqrt(jnp.mean(xv * xv, axis=-1, keepdims=True) + RMS_EPS)
        xh = xv * rs
        dsc_ref[...] += jnp.sum(dhv * (xh * g_ref[...]), axis=0, keepdims=True)
        dsh_ref[...] += jnp.sum(dhv, axis=0, keepdims=True)
        dhn = dhv * (1.0 + sc_ref[...])
        dg_ref[...] += jnp.sum(dhn * xh, axis=0, keepdims=True)
        dxh = dhn * g_ref[...]
        dx = dres_ref[...] + rs * (dxh - xh * jnp.mean(dxh * xh, axis=-1, keepdims=True))
        dx_ref[...] = dx
        if branch:
            _gated_branch_bwd(dx, refs[6], refs[7], refs[10 + n_b], refs[11 + n_b])

    vec = pl.BlockSpec((1, D_MODEL), lambda i: (0, 0))
    big = pl.BlockSpec((tm, D_MODEL), lambda i: (i, 0))
    return _pcall(
        body, name=name, grid=(SEQ // tm,),
        in_specs=[big, vec, vec, vec, big, big] + ([big, vec] if branch else []),
        out_specs=[big, vec, vec, vec] + ([big, vec] if branch else []),
        out_shape=[_sds((SEQ, D_MODEL), F32)] + [_sds((1, D_MODEL), F32)] * 3
        + ([_sds((SEQ, D_MODEL), BF16), _sds((1, D_MODEL), F32)] if branch else []),
        compiler_params=_cparams(("arbitrary",)),
    )(x, gain, sc, sh, dh, dres, *(branch or ()))


def _mm_cols(a, wg, layer, *, n_blocks, width, tn, act_map, w_map, out_dtype, name, tm=1024):
    k = a.shape[1]
    n_tiles = n_blocks * width // tn

    def body(a_ref, w_ref, o_ref):
        o_ref[...] = _dot(a_ref[...], w_ref[...]).astype(o_ref.dtype)

    return _pcall(
        body, name=name, grid=(SEQ // tm, n_tiles),
        in_specs=[pl.BlockSpec((tm, k), lambda i, t: (i, 0)),
                  pl.BlockSpec((None, None, k, tn), lambda i, t: (w_map(t)[0], layer, 0, w_map(t)[1]))],
        out_specs=pl.BlockSpec((None, tm, tn), lambda i, t: (act_map(t)[0], i, act_map(t)[1])),
        out_shape=_sds((n_blocks, SEQ, width), out_dtype),
        compiler_params=_cparams(("parallel", "arbitrary")),
    )(a, wg)


def _mm_cols_bwd_a(dout, wg, layer, *, group, name, tm=1024, scatter=()):
    n_blocks, _, width = dout.shape
    k = wg.shape[2]
    n_s = len(scatter)
    n_rows = SEQ // tm
    n_steps = n_blocks // group

    def body(*refs):
        d_ref, w_ref = refs[:2]
        o_ref = refs[2 + n_s]
        if n_s:
            comm_start, comm_wait = _rs_chips(refs[2:2 + n_s], refs[3 + n_s:3 + 2 * n_s], *refs[3 + 2 * n_s:])
            pl.when((pl.program_id(0) == 0) & (pl.program_id(1) == 0))(comm_start)
        acc = _dot_nt(d_ref[0], w_ref[0])
        for b in range(1, group):
            acc += _dot_nt(d_ref[b], w_ref[b])
        if n_steps == 1:
            o_ref[...] = acc
        else:
            @pl.when(pl.program_id(1) == 0)
            def _():
                o_ref[...] = acc

            @pl.when(pl.program_id(1) > 0)
            def _():
                o_ref[...] += acc
        if n_s:
            pl.when((pl.program_id(0) == n_rows - 1) & (pl.program_id(1) == n_steps - 1))(comm_wait)

    sem = pltpu.SemaphoreType.DMA((max(n_s, 1), 3))
    res = _pcall(
        body, name=name, grid=(n_rows, n_steps),
        in_specs=[pl.BlockSpec((group, tm, width), lambda i, t: (t, i, 0)),
                  pl.BlockSpec((group, None, k, width), lambda i, t: (t, layer, 0, 0))] + [_ANY] * n_s,
        out_specs=[pl.BlockSpec((tm, k), lambda i, t: (i, 0))] + [_ANY] * n_s,
        out_shape=[_sds((SEQ, k), F32)] + _rs_chips_shapes(scatter),
        scratch_shapes=[sem, sem] if n_s else [],
        compiler_params=_cparams(("arbitrary", "arbitrary") if n_s else ("parallel", "arbitrary")),
    )(dout, wg, *scatter)
    return res if n_s else res[0]


def _mm_cols_bwd_w(a, dout, *, ns, tn, act_map, w_map, n_tiles, name, tm=1024, n_out=N_CHIP):
    k = a.shape[1]

    def body(a_ref, d_ref, o_ref):
        @pl.when(pl.program_id(1) == 0)
        def _():
            o_ref[...] = jnp.zeros_like(o_ref)

        o_ref[...] += _dot_tn(a_ref[...], d_ref[...])

    return _pcall(
        body, name=name, grid=(n_tiles, SEQ // tm),
        in_specs=[pl.BlockSpec((tm, k), lambda t, i: (i, 0)),
                  pl.BlockSpec((None, tm, tn), lambda t, i: (act_map(t)[0], i, act_map(t)[1]))],
        out_specs=pl.BlockSpec((None, k, tn), lambda t, i: (w_map(t)[0], 0, w_map(t)[1])),
        out_shape=_sds((n_out, k, ns), F32),
        compiler_params=_cparams(("parallel", "arbitrary")),
    )(a, dout)


def _retile_cols(src, *, n_out, width_out, tn, src_map, dst_map, n_tiles, name):
    k = src.shape[1]

    def body(s_ref, o_ref):
        o_ref[...] = s_ref[...]

    return _pcall(
        body, name=name, grid=(n_tiles,),
        in_specs=[pl.BlockSpec((None, k, tn), lambda t: (src_map(t)[0], 0, src_map(t)[1]))],
        out_specs=pl.BlockSpec((None, k, tn), lambda t: (dst_map(t)[0], 0, dst_map(t)[1])),
        out_shape=_sds((n_out, k, width_out), src.dtype),
        compiler_params=_cparams(("parallel",)),
    )(src)


def _mm_rows(a4, wg, layer, x, gate, name, tm=512):
    ks = a4.shape[2]
    n = wg.shape[3]

    def body(a_ref, w_ref, x_ref, g_ref, z_ref, xn_ref):
        z = _dot(a_ref[0], w_ref[0])
        for s in range(1, N_CHIP):
            z += _dot(a_ref[s], w_ref[s])
        z_ref[...] = z
        xn_ref[...] = x_ref[...] + g_ref[...] * z

    big = pl.BlockSpec((tm, n), lambda i: (i, 0))
    return _pcall(
        body, name=name, grid=(SEQ // tm,),
        in_specs=[pl.BlockSpec((N_CHIP, tm, ks), lambda i: (0, i, 0)),
                  pl.BlockSpec((N_CHIP, None, ks, n), lambda i: (0, layer, 0, 0)),
                  big, pl.BlockSpec((1, n), lambda i: (0, 0))],
        out_specs=[big, big],
        out_shape=[_sds((SEQ, n), F32), _sds((SEQ, n), F32)],
        compiler_params=_cparams(("parallel",)),
    )(a4, wg, x, gate)


def _mm_rows_bwd_a(dz, wg, layer, name, tm=1024, halves=()):
    ks, n = wg.shape[2], wg.shape[3]
    n_h = len(halves)
    n_rows = SEQ // tm

    def body(*refs):
        dz_ref, w_ref = refs[:2]
        o_ref = refs[2 + n_h]
        if n_h:
            comm_start, comm_wait = _rs_halves(refs[2:2 + n_h], refs[3 + n_h:3 + 2 * n_h], *refs[3 + 2 * n_h:])
            pl.when((pl.program_id(0) == 0) & (pl.program_id(1) == 0))(comm_start)
        o_ref[...] = _dot_nt(dz_ref[...], w_ref[...])
        if n_h:
            pl.when((pl.program_id(0) == n_rows - 1) & (pl.program_id(1) == N_CHIP - 1))(comm_wait)

    sem = pltpu.SemaphoreType.DMA((max(n_h, 1),))
    res = _pcall(
        body, name=name, grid=(n_rows, N_CHIP),
        in_specs=[pl.BlockSpec((tm, n), lambda i, s: (i, 0)),
                  pl.BlockSpec((None, None, ks, n), lambda i, s: (s, layer, 0, 0))] + [_ANY] * n_h,
        out_specs=[pl.BlockSpec((None, tm, ks), lambda i, s: (s, i, 0))] + [_ANY] * n_h,
        out_shape=[_sds((N_CHIP, SEQ, ks), F32)] + _rs_halves_shapes(halves),
        scratch_shapes=[sem, sem] if n_h else [],
        compiler_params=_cparams(("arbitrary", "arbitrary") if n_h else ("parallel", "arbitrary")),
    )(dz, wg, *halves)
    return res if n_h else res[0]


def _mm_rows_bwd_w(a4, dz, name, tm=2048):
    ks = a4.shape[2]
    n = dz.shape[1]

    def body(a_ref, dz_ref, o_ref):
        @pl.when(pl.program_id(1) == 0)
        def _():
            o_ref[...] = jnp.zeros_like(o_ref)

        o_ref[...] += _dot_tn(a_ref[...], dz_ref[...])

    return _pcall(
        body, name=name, grid=(N_CHIP, SEQ // tm),
        in_specs=[pl.BlockSpec((None, tm, ks), lambda s, i: (s, i, 0)),
                  pl.BlockSpec((tm, n), lambda s, i: (i, 0))],
        out_specs=pl.BlockSpec((None, ks, n), lambda s, i: (s, 0, 0)),
        out_shape=_sds((N_CHIP, ks, n), F32),
        compiler_params=_cparams(("parallel", "arbitrary")),
    )(a4, dz)


def _ffn_up(h, w1g, w3g, layer, name, tm=1024):
    def body(h_ref, w1_ref, w3_ref, a1_ref, a3_ref, u_ref):
        hv = h_ref[...]
        a1 = _dot_nt(hv, w1_ref[...])
        a3 = _dot_nt(hv, w3_ref[...])
        a1_ref[...] = a1
        a3_ref[...] = a3
        u_ref[...] = (_silu(a1) * a3).astype(BF16)

    wspec = pl.BlockSpec((None, None, FFN_SHARD, D_MODEL), lambda i, s: (s, layer, 0, 0))
    ospec = pl.BlockSpec((None, tm, FFN_SHARD), lambda i, s: (s, i, 0))
    shp = (N_CHIP, SEQ, FFN_SHARD)
    return _pcall(
        body, name=name, grid=(SEQ // tm, N_CHIP),
        in_specs=[pl.BlockSpec((tm, D_MODEL), lambda i, s: (i, 0)), wspec, wspec],
        out_specs=[ospec, ospec, ospec],
        out_shape=[_sds(shp, F32), _sds(shp, F32), _sds(shp, BF16)],
        compiler_params=_cparams(("parallel", "arbitrary")),
    )(h, w1g, w3g)


def _ffn_up_bwd(da1, da3, w1g, w3g, layer, name, tm=512):
    def body(d1_ref, d3_ref, w1_ref, w3_ref, o_ref):
        acc = _dot(d1_ref[0], w1_ref[0]) + _dot(d3_ref[0], w3_ref[0])
        for s in range(1, N_CHIP):
            acc += _dot(d1_ref[s], w1_ref[s]) + _dot(d3_ref[s], w3_ref[s])
        o_ref[...] = acc

    wspec = pl.BlockSpec((N_CHIP, None, FFN_SHARD, D_MODEL), lambda i: (0, layer, 0, 0))
    dspec = pl.BlockSpec((N_CHIP, tm, FFN_SHARD), lambda i: (0, i, 0))
    return _pcall(
        body, name=name, grid=(SEQ // tm,),
        in_specs=[dspec, dspec, wspec, wspec],
        out_specs=pl.BlockSpec((tm, D_MODEL), lambda i: (i, 0)),
        out_shape=_sds((SEQ, D_MODEL), F32),
        compiler_params=_cparams(("parallel",)),
    )(da1, da3, w1g, w3g)


def _ffn_down_bwd(dz, w2g, layer, a1, a3, name, tm=1024, halves=()):
    n_h = len(halves)
    n_rows = SEQ // tm

    def body(*refs):
        dz_ref, w_ref, a1_ref, a3_ref = refs[:4]
        da1_ref, da3_ref = refs[4 + n_h:6 + n_h]
        if n_h:
            comm_start, comm_wait = _rs_halves(refs[4:4 + n_h], refs[6 + n_h:6 + 2 * n_h], *refs[6 + 2 * n_h:])
            pl.when((pl.program_id(0) == 0) & (pl.program_id(1) == 0))(comm_start)
        du = _dot_nt(dz_ref[...], w_ref[...])
        a1 = a1_ref[...]
        da1_ref[...] = (du * a3_ref[...] * _dsilu(a1)).astype(BF16)
        da3_ref[...] = (du * _silu(a1)).astype(BF16)
        if n_h:
            pl.when((pl.program_id(0) == n_rows - 1) & (pl.program_id(1) == N_CHIP - 1))(comm_wait)

    blk = pl.BlockSpec((None, tm, FFN_SHARD), lambda i, s: (s, i, 0))
    shp = (N_CHIP, SEQ, FFN_SHARD)
    sem = pltpu.SemaphoreType.DMA((max(n_h, 1),))
    return _pcall(
        body, name=name, grid=(n_rows, N_CHIP),
        in_specs=[pl.BlockSpec((tm, D_MODEL), lambda i, s: (i, 0)),
                  pl.BlockSpec((None, None, FFN_SHARD, D_MODEL), lambda i, s: (s, layer, 0, 0)),
                  blk, blk] + [_ANY] * n_h,
        out_specs=[blk, blk] + [_ANY] * n_h,
        out_shape=[_sds(shp, BF16), _sds(shp, BF16)] + _rs_halves_shapes(halves),
        scratch_shapes=[sem, sem] if n_h else [],
        compiler_params=_cparams(("arbitrary", "arbitrary") if n_h else ("parallel", "arbitrary")),
    )(dz, w2g, a1, a3, *halves)


def _loss_head(y, target, z, gate, name):
    tm = 512
    n_steps = SEQ // tm

    def body(y_ref, t_ref, z_ref, gate_ref, dy_ref, l_ref, dz_ref, dgate_ref, acc_ref):
        @pl.when(pl.program_id(0) == 0)
        def _():
            acc_ref[...] = jnp.zeros_like(acc_ref)
            dgate_ref[...] = jnp.zeros_like(dgate_ref)

        err = y_ref[...] - t_ref[...]
        dy = err * (1.0 / D_MODEL)
        dy_ref[...] = dy
        acc_ref[...] += jnp.sum(jnp.mean(err * err, axis=-1, keepdims=True), axis=0, keepdims=True)
        _gated_branch_bwd(dy, z_ref, gate_ref, dz_ref, dgate_ref)

        @pl.when(pl.program_id(0) == n_steps - 1)
        def _():
            l_ref[...] = 0.5 * acc_ref[...]

    big = pl.BlockSpec((tm, D_MODEL), lambda i: (i, 0))
    vec = pl.BlockSpec((1, D_MODEL), lambda i: (0, 0))
    return _pcall(
        body, name=name, grid=(n_steps,),
        in_specs=[big, big, big, vec],
        out_specs=[big, pl.BlockSpec((1, 1), lambda i: (0, 0)), big, vec],
        out_shape=[_sds((SEQ, D_MODEL), F32), _sds((1, 1), F32), _sds((SEQ, D_MODEL), BF16),
                   _sds((1, D_MODEL), F32)],
        scratch_shapes=[pltpu.VMEM((1, 1), F32)],
        compiler_params=_cparams(("arbitrary",)),
    )(y, target, z, gate)


def _attn_rows(base, d):
    if d == 1:
        return pl.ds(pl.multiple_of(base, ATT_BLK), ATT_BLK)
    return pl.ds(base, ATT_BLK, stride=d)


def _attn_block_index(i, d):
    nb = SEQ // (ATT_BLK * d)
    r = i // nb
    n = i % nb
    base = r + n * (ATT_BLK * d)
    pbase = jnp.maximum(base - ATT_BLK * d, r)
    return n, _attn_rows(base, d), _attn_rows(pbase, d)


def _attn_two_blocks(ref, prow, rows):
    return jnp.concatenate([ref[prow, :].astype(BF16), ref[rows, :].astype(BF16)], axis=0)


def _attn_block_bias(b_ref, n):
    b = b_ref[...]
    prev_half = lax.broadcasted_iota(jnp.int32, b.shape, 1) < ATT_BLK
    return jnp.where(prev_half & (n == 0), NEG, b)


def _qk_normed(x):
    rs = lax.rsqrt(jnp.mean(x * x, axis=-1, keepdims=True) + RMS_EPS)
    return x * rs, rs


def _attn_fwd(qkv9, qgain, kgain, bias, name, gather=()):
    n_g = len(gather)

    def body(*refs):
        q_ref, k_ref, v_ref, qg_ref, kg_ref, b_ref = refs[:6]
        o_ref, lse_ref = refs[6 + n_g:8 + n_g]
        qn_s, kn_s, acc_s, m_s, l_s = refs[8 + 2 * n_g:13 + 2 * n_g]
        g = pl.program_id(1)
        if n_g:
            comm_start, comm_wait = _gather_ici(refs[8 + n_g:8 + 2 * n_g], *refs[13 + 2 * n_g:])
            pl.when((pl.program_id(0) == 0) & (g == 0))(comm_start)

        @pl.when(g == 0)
        def _():
            m_s[...] = jnp.full_like(m_s, NEG)
            l_s[...] = jnp.zeros_like(l_s)
            acc_s[...] = jnp.zeros_like(acc_s)

        qn_s[...] = _qk_normed(q_ref[...])[0] * qg_ref[...]
        kn_s[...] = _qk_normed(k_ref[...])[0] * kg_ref[...]

        for gi, (_, d) in enumerate(GROUPS):
            @pl.when(g == gi)
            def _(d=d):
                def block(n, qb, kk, vv, m_old, l_old, acc_old):
                    s = _dot_nt(qb, kk) * ATT_SCALE + _attn_block_bias(b_ref, n)
                    m_new = jnp.maximum(m_old, jnp.max(s, axis=-1, keepdims=True))
                    alpha = jnp.exp(m_old - m_new)
                    p = jnp.exp(s - m_new)
                    l_new = alpha * l_old + jnp.sum(p, axis=-1, keepdims=True)
                    acc_new = alpha * acc_old + _dot(p.astype(BF16), vv)
                    return m_new, l_new, acc_new

                def it(i, carry):
                    where, loaded = [], []
                    for way in range(ATT_WAYS):
                        n, rows, prow = _attn_block_index(i + way * ATT_STEPS, d)
                        where.append(rows)
                        loaded.append((n, qn_s[rows, :].astype(BF16), _attn_two_blocks(kn_s, prow, rows),
                                       _attn_two_blocks(v_ref, prow, rows), m_s[rows, :], l_s[rows, :],
                                       acc_s[rows, :]))
                    results = [block(*vals) for vals in loaded]
                    for rows, (m_new, l_new, acc_new) in zip(where, results):
                        m_s[rows, :] = m_new
                        l_s[rows, :] = l_new
                        acc_s[rows, :] = acc_new
                    return carry

                lax.fori_loop(0, ATT_STEPS, it, 0)

        @pl.when(g == len(GROUPS) - 1)
        def _():
            o_ref[...] = (acc_s[...] / l_s[...]).astype(BF16)
            lse_ref[...] = m_s[...] + jnp.log(l_s[...])

        if n_g:
            pl.when((pl.program_id(0) == HEADS - 1) & (g == len(GROUPS) - 1))(comm_wait)

    def col(j):
        return pl.BlockSpec((None, SEQ, HEAD_DIM), lambda h, g: (g * 3 + j, 0, h))

    gspec = pl.BlockSpec((None, 1, HEAD_DIM), lambda h, g: (g, 0, 0))
    sem = pltpu.SemaphoreType.DMA((max(n_g, 1), 3))
    return _pcall(
        body, name=name, grid=(HEADS, len(GROUPS)),
        in_specs=[col(0), col(1), col(2), gspec, gspec,
                  pl.BlockSpec((None, None, ATT_BLK, 2 * ATT_BLK), lambda h, g: (g, h, 0, 0))] + [_ANY] * n_g,
        out_specs=[pl.BlockSpec((None, SEQ, HEAD_DIM), lambda h, g: (h // 2, 0, h % 2)),
                   pl.BlockSpec((None, SEQ, 1), lambda h, g: (h, 0, 0))] + [_ANY] * n_g,
        out_shape=[_sds((N_CHIP, SEQ, 2 * HEAD_DIM), BF16), _sds((HEADS, SEQ, 1), F32)]
        + [_sds(s.shape, s.dtype) for s in gather],
        input_output_aliases={6 + a: 2 + a for a in range(n_g)},
        scratch_shapes=[pltpu.VMEM((SEQ, HEAD_DIM), F32)] * 3 + [pltpu.VMEM((SEQ, 1), F32)] * 2
        + ([sem, sem] if n_g else []),
        compiler_params=_cparams(("arbitrary", "arbitrary")),
    )(qkv9, qkv9, qkv9, qgain, kgain, bias, *gather)


def _attn_bwd(qkv9, qgain, kgain, bias, do4, o4, lse, name, scatter=()):
    n_s = len(scatter)

    def body(*refs):
        q_ref, k_ref, v_ref, qg_ref, kg_ref, b_ref, do_ref, o_ref, lse_ref = refs[:9]
        dqkv_ref, dqg_ref, dkg_ref, db_ref = refs[9 + n_s:13 + n_s]
        qn_s, kn_s, dq_s, dk_s, dv_s, dl_s = refs[13 + 2 * n_s:19 + 2 * n_s]
        g = pl.program_id(1)
        if n_s:
            comm_start, comm_wait = _rs_chips(refs[9:9 + n_s], refs[13 + n_s:13 + 2 * n_s], *refs[19 + 2 * n_s:])
            pl.when((pl.program_id(0) == 0) & (g == 0))(comm_start)
        qh, rq = _qk_normed(q_ref[...])
        kh, rk = _qk_normed(k_ref[...])
        qn_s[...] = qh * qg_ref[...]
        kn_s[...] = kh * kg_ref[...]
        @pl.when(g == 0)
        def _():
            dl_s[...] = jnp.sum(do_ref[...] * o_ref[...].astype(F32), axis=-1, keepdims=True)

        dk_s[...] = jnp.zeros_like(dk_s)
        dv_s[...] = jnp.zeros_like(dv_s)
        db_ref[...] = jnp.zeros_like(db_ref)

        for gi, (_, d) in enumerate(GROUPS):
            @pl.when(g == gi)
            def _(d=d):
                def block(n, qb, kk, vv, dob, lse_b, dl):
                    s = _dot_nt(qb, kk) * ATT_SCALE + _attn_block_bias(b_ref, n)
                    p = jnp.exp(s - lse_b)
                    ds = p * (_dot_nt(dob, vv) - dl)
                    ds16 = ds.astype(BF16)
                    return (ds, _dot(ds16, kk) * ATT_SCALE, _dot_tn(ds16, qb) * ATT_SCALE,
                            _dot_tn(p.astype(BF16), dob))

                def it(i, carry):
                    where, loaded, old = [], [], []
                    for way in range(ATT_WAYS):
                        n, rows, prow = _attn_block_index(i + way * ATT_STEPS, d)
                        where.append((rows, prow))
                        loaded.append((n, qn_s[rows, :].astype(BF16), _attn_two_blocks(kn_s, prow, rows),
                                       _attn_two_blocks(v_ref, prow, rows), do_ref[rows, :].astype(BF16),
                                       lse_ref[rows, :], dl_s[rows, :]))
                        old.append((dk_s[rows, :], dk_s[prow, :], dv_s[rows, :], dv_s[prow, :]))
                    results = [block(*vals) for vals in loaded]
                    db_ref[...] += functools.reduce(lambda a, b: a + b, [r[0] for r in results])
                    for (rows, prow), (dk_c, dk_p, dv_c, dv_p), (_, dq, dkk, dvv) in zip(where, old, results):
                        dq_s[rows, :] = dq
                        dk_s[prow, :] = dk_p + dkk[:ATT_BLK]
                        dv_s[prow, :] = dv_p + dvv[:ATT_BLK]
                        dk_s[rows, :] = dk_c + dkk[ATT_BLK:]
                        dv_s[rows, :] = dv_c + dvv[ATT_BLK:]
                    return carry

                lax.fori_loop(0, ATT_STEPS, it, 0)

        def norm_bwd(dn, xh, rs, gain):
            dgain = jnp.sum(dn * xh, axis=0, keepdims=True)
            dxh = dn * gain
            return rs * (dxh - xh * jnp.mean(dxh * xh, axis=-1, keepdims=True)), dgain

        dq, dqg = norm_bwd(dq_s[...], qh, rq, qg_ref[...])
        dk, dkg = norm_bwd(dk_s[...], kh, rk, kg_ref[...])
        dqkv_ref[0] = dq.astype(BF16)
        dqkv_ref[1] = dk.astype(BF16)
        dqkv_ref[2] = dv_s[...].astype(BF16)
        dqg_ref[...] = dqg
        dkg_ref[...] = dkg
        if n_s:
            pl.when((pl.program_id(0) == HEADS - 1) & (g == len(GROUPS) - 1))(comm_wait)

    def col(j):
        return pl.BlockSpec((None, SEQ, HEAD_DIM), lambda h, g: (g * 3 + j, 0, h))

    gspec = pl.BlockSpec((None, 1, HEAD_DIM), lambda h, g: (g, 0, 0))
    bspec = pl.BlockSpec((None, None, ATT_BLK, 2 * ATT_BLK), lambda h, g: (g, h, 0, 0))
    hcol = pl.BlockSpec((None, SEQ, HEAD_DIM), lambda h, g: (h // 2, 0, h % 2))
    dgspec = pl.BlockSpec((None, None, 1, HEAD_DIM), lambda h, g: (h, g, 0, 0))
    ng = len(GROUPS)
    sem = pltpu.SemaphoreType.DMA((max(n_s, 1), 3))
    return _pcall(
        body, name=name, grid=(HEADS, ng),
        in_specs=[col(0), col(1), col(2), gspec, gspec, bspec, hcol, hcol,
                  pl.BlockSpec((None, SEQ, 1), lambda h, g: (h, 0, 0))] + [_ANY] * n_s,
        out_specs=[pl.BlockSpec((None, 3, SEQ, HEAD_DIM), lambda h, g: (g, 0, 0, h)), dgspec, dgspec, bspec]
        + [_ANY] * n_s,
        out_shape=[_sds((ng, 3, SEQ, D_MODEL), BF16), _sds((HEADS, ng, 1, HEAD_DIM), F32),
                   _sds((HEADS, ng, 1, HEAD_DIM), F32), _sds((ng, HEADS, ATT_BLK, 2 * ATT_BLK), F32)]
        + _rs_chips_shapes(scatter),
        scratch_shapes=[pltpu.VMEM((SEQ, HEAD_DIM), F32)] * 5 + [pltpu.VMEM((SEQ, 1), F32)]
        + ([sem, sem] if n_s else []),
        compiler_params=_cparams(("arbitrary", "arbitrary")),
    )(qkv9, qkv9, qkv9, qgain, kgain, bias, do4, o4, lse, *scatter)


def _relbias_bwd(dbias, bucket_idx, name):
    ng = len(GROUPS)

    def body(db_ref, idx_ref, o_ref):
        lane = lax.broadcasted_iota(jnp.int32, (HEADS, 128), 1)
        acc = jnp.zeros((HEADS, 128), F32)
        for g in range(ng):
            dbg = db_ref[g]
            idx = idx_ref[g]
            for b in range(NUM_BUCKETS):
                sel = jnp.where((idx == b)[None], dbg, 0.0)
                part = jnp.sum(sel, axis=1)
                val = jnp.sum(part, axis=-1, keepdims=True)
                acc = jnp.where(lane == g * NUM_BUCKETS + b, val, acc)
        o_ref[...] = acc

    return _pcall(body, name=name, out_shape=_sds((HEADS, 128), F32), compiler_params=_cparams())(dbias, bucket_idx)


def _scan16(x, reverse=False):
    row = lax.broadcasted_iota(jnp.int32, x.shape, 0)
    for sh in (1, 2, 4, 8):
        if reverse:
            x = x + jnp.where(row < HG_SUB - sh, pltpu.roll(x, HG_SUB - sh, 0), 0.0)
        else:
            x = x + jnp.where(row >= sh, pltpu.roll(x, sh, 0), 0.0)
    return x


def _hgrn_gates(qr, fr, lbv):
    q = _silu(qr)
    sig = _sigmoid(fr)
    fg = lbv + (1.0 - lbv) * sig
    lf = jnp.log(fg)
    gcum = _scan16(lf)
    glast = jnp.sum(lf, axis=0, keepdims=True)
    return q, sig, fg, 1.0 - fg, gcum, glast


def _hgrn_intra(q, k, gcum, tri):
    e = jnp.exp(jnp.where(tri, gcum[:, None, :] - gcum[None, :, :], NEG))
    a = jnp.sum(q[:, None, :] * k[None, :, :] * e, axis=-1, keepdims=True)
    return e, a


def _hgrn_fwd(proj4, lb, gain, name):
    nsub = HG_TC // HG_SUB
    wide = HG_HP * HEAD_DIM

    def body(p_ref, lb_ref, gn_ref, o_ref, y_ref, st_ref, state_s):
        @pl.when(pl.program_id(1) == 0)
        def _():
            state_s[...] = jnp.zeros_like(state_s)

        gnv = gn_ref[...]
        shp = (HG_SUB, HG_SUB, HEAD_DIM)
        tri = lax.broadcasted_iota(jnp.int32, shp, 0) >= lax.broadcasted_iota(jnp.int32, shp, 1)

        def head(qr, fr, vv, gr, lbv, st):
            q, _, _, k, gcum, glast = _hgrn_gates(qr, fr, lbv)
            _, a = _hgrn_intra(q, k, gcum, tri)
            o = jnp.sum(a * vv[None, :, :], axis=1) + _dot_nt((q * jnp.exp(gcum)).astype(BF16), st.astype(BF16))
            kg = k * jnp.exp(glast - gcum)
            st_new = st * jnp.exp(glast) + _dot_tn(vv.astype(BF16), kg.astype(BF16))
            rs = lax.rsqrt(jnp.mean(o * o, axis=-1, keepdims=True) + RMS_EPS)
            return o, (o * rs * gnv * _silu(gr)).astype(BF16), st_new

        def it(i, carry):
            rows = pl.ds(pl.multiple_of(i * HG_SUB, HG_SUB), HG_SUB)
            loaded = []
            for hh in range(HG_HP):
                lanes = pl.ds(hh * HEAD_DIM, HEAD_DIM)
                loaded.append(([p_ref[j, rows, lanes] for j in range(4)], lb_ref[:, lanes], state_s[hh]))
            results = [head(blk[0], blk[1], blk[2], blk[3], lbv, st) for blk, lbv, st in loaded]
            for hh, ((_, _, st), (o, y, st_new)) in enumerate(zip(loaded, results)):
                lanes = pl.ds(hh * HEAD_DIM, HEAD_DIM)
                st_ref[hh, i] = st.astype(BF16)
                state_s[hh] = st_new
                o_ref[rows, lanes] = o
                y_ref[hh // 2, rows, pl.ds((hh % 2) * HEAD_DIM, HEAD_DIM)] = y
            return carry

        lax.fori_loop(0, nsub, it, 0)

    return _pcall(
        body, name=name, grid=(HEADS // HG_HP, SEQ // HG_TC),
        in_specs=[pl.BlockSpec((4, HG_TC, wide), lambda h, j: (0, j, h)),
                  pl.BlockSpec((1, wide), lambda h, j: (0, h)),
                  pl.BlockSpec((1, HEAD_DIM), lambda h, j: (0, 0))],
        out_specs=[pl.BlockSpec((HG_TC, wide), lambda h, j: (j, h)),
                   pl.BlockSpec((HG_HP // 2, HG_TC, 2 * HEAD_DIM), lambda h, j: (h, j, 0)),
                   pl.BlockSpec((HG_HP, nsub, HEAD_DIM, HEAD_DIM), lambda h, j: (h, j, 0, 0))],
        out_shape=[_sds((SEQ, D_MODEL), F32), _sds((N_CHIP, SEQ, 2 * HEAD_DIM), BF16),
                   _sds((HEADS, SEQ // HG_SUB, HEAD_DIM, HEAD_DIM), BF16)],
        scratch_shapes=[pltpu.VMEM((HG_HP, HEAD_DIM, HEAD_DIM), F32)],
        compiler_params=_cparams(("parallel", "arbitrary")),
    )(proj4, lb, gain)


def _hgrn_bwd(proj4, lb, gain, o_raw, dy4, states, name):
    nsub = HG_TC // HG_SUB
    nt = SEQ // HG_TC
    wide = HG_HP * HEAD_DIM

    def body(p_ref, lb_ref, gn_ref, o_ref, dy_ref, st_ref, dp_ref, dlb_ref, dgn_ref, dst_s):
        @pl.when(pl.program_id(1) == 0)
        def _():
            dst_s[...] = jnp.zeros_like(dst_s)
            dlb_ref[...] = jnp.zeros_like(dlb_ref)
            dgn_ref[...] = jnp.zeros_like(dgn_ref)

        gnv = gn_ref[...]
        shp = (HG_SUB, HG_SUB, HEAD_DIM)
        tri = lax.broadcasted_iota(jnp.int32, shp, 0) >= lax.broadcasted_iota(jnp.int32, shp, 1)

        def head(qr, fr, vv, gr, o, dy, lbv, st0, dst):
            q, sig, fg, k, gcum, glast = _hgrn_gates(qr, fr, lbv)
            rs = lax.rsqrt(jnp.mean(o * o, axis=-1, keepdims=True) + RMS_EPS)
            oh = o * rs
            don = dy * _silu(gr)
            dgn = jnp.sum(don * oh, axis=0, keepdims=True)
            dgr = dy * oh * gnv * _dsilu(gr)
            doh = don * gnv
            do = rs * (doh - oh * jnp.mean(doh * oh, axis=-1, keepdims=True))
            dst16 = dst.astype(BF16)
            do16 = do.astype(BF16)
            eg = jnp.exp(gcum)
            eb = jnp.exp(glast - gcum)
            e, a = _hgrn_intra(q, k, gcum, tri)
            da = jnp.sum(do[:, None, :] * vv[None, :, :], axis=-1, keepdims=True)
            dae = da * e
            dq = jnp.sum(dae * k[None, :, :], axis=1) + eg * _dot(do16, st0)
            dk_state = eb * _dot(vv.astype(BF16), dst16)
            dk = jnp.sum(dae * q[:, None, :], axis=0) + dk_state
            dv = jnp.sum(a * do[:, None, :], axis=0) + _dot_nt((k * eb).astype(BF16), dst16)
            eglast = jnp.exp(glast)
            dst_new = dst * eglast + _dot_tn(do16, (q * eg).astype(BF16))
            dglast = jnp.sum(k * dk_state, axis=0, keepdims=True) \
                + eglast * jnp.sum(dst * st0.astype(F32), axis=0, keepdims=True)
            dlf = _scan16(q * dq - k * dk, reverse=True) + dglast
            dfg = dlf / fg - dk
            dlb = jnp.sum(dfg * (1.0 - sig), axis=0, keepdims=True)
            dproj = ((dq * _dsilu(qr)).astype(BF16), (dfg * (1.0 - lbv) * sig * (1.0 - sig)).astype(BF16),
                     dv.astype(BF16), dgr.astype(BF16))
            return dproj, dst_new, dlb, dgn

        def it(ii, carry):
            i = nsub - 1 - ii
            rows = pl.ds(pl.multiple_of(i * HG_SUB, HG_SUB), HG_SUB)
            results = []
            for hh in range(HG_HP):
                lanes = pl.ds(hh * HEAD_DIM, HEAD_DIM)
                blk = [p_ref[j, rows, lanes] for j in range(4)]
                dy = dy_ref[hh // 2, rows, pl.ds((hh % 2) * HEAD_DIM, HEAD_DIM)]
                results.append(head(blk[0], blk[1], blk[2], blk[3], o_ref[rows, lanes], dy,
                                    lb_ref[:, lanes], st_ref[hh, i], dst_s[hh]))
            new_carry = []
            for hh, (dproj, dst_new, dlb, dgn) in enumerate(results):
                lanes = pl.ds(hh * HEAD_DIM, HEAD_DIM)
                dst_s[hh] = dst_new
                for j in range(4):
                    dp_ref[j, rows, lanes] = dproj[j]
                new_carry.append((carry[hh][0] + dlb, carry[hh][1] + dgn))
            return tuple(new_carry)

        zero = jnp.zeros((1, HEAD_DIM), F32)
        sums = lax.fori_loop(0, nsub, it, tuple((zero, zero) for _ in range(HG_HP)))
        for hh in range(HG_HP):
            dlb_ref[hh] += sums[hh][0]
            dgn_ref[hh] += sums[hh][1]

    vspec = pl.BlockSpec((HG_HP, 1, HEAD_DIM), lambda h, j: (h, 0, 0))
    return _pcall(
        body, name=name, grid=(HEADS // HG_HP, nt),
        in_specs=[pl.BlockSpec((4, HG_TC, wide), lambda h, j: (0, nt - 1 - j, h)),
                  pl.BlockSpec((1, wide), lambda h, j: (0, h)),
                  pl.BlockSpec((1, HEAD_DIM), lambda h, j: (0, 0)),
                  pl.BlockSpec((HG_TC, wide), lambda h, j: (nt - 1 - j, h)),
                  pl.BlockSpec((HG_HP // 2, HG_TC, 2 * HEAD_DIM), lambda h, j: (h, nt - 1 - j, 0)),
                  pl.BlockSpec((HG_HP, nsub, HEAD_DIM, HEAD_DIM), lambda h, j: (h, nt - 1 - j, 0, 0))],
        out_specs=[pl.BlockSpec((4, HG_TC, wide), lambda h, j: (0, nt - 1 - j, h)), vspec, vspec],
        out_shape=[_sds((4, SEQ, D_MODEL), BF16), _sds((HEADS, 1, HEAD_DIM), F32), _sds((HEADS, 1, HEAD_DIM), F32)],
        scratch_shapes=[pltpu.VMEM((HG_HP, HEAD_DIM, HEAD_DIM), F32)],
        compiler_params=_cparams(("parallel", "arbitrary")),
    )(proj4, lb, gain, o_raw, dy4, states)


def _t5_bucket(dist):
    n = np.asarray(dist, dtype=np.int64)
    max_exact = NUM_BUCKETS // 2
    large = max_exact + (np.log(np.maximum(n, 1) / max_exact) / np.log(MAX_DISTANCE / max_exact)
                         * (NUM_BUCKETS - max_exact)).astype(np.int64)
    large = np.minimum(large, NUM_BUCKETS - 1)
    return np.where(n < max_exact, n, large).astype(np.int32)


def _bias_tables():
    qi = np.arange(ATT_BLK)[:, None]
    ki = np.arange(2 * ATT_BLK)[None, :]
    j = ATT_BLK + qi - ki
    valid = (j >= 0) & (j <= ATT_BLK)
    return np.stack([np.where(valid, _t5_bucket(np.clip(j, 0, ATT_BLK) * d), -1) for _, d in GROUPS]).astype(np.int32)


def _attn_bias(rel_bias, name):
    idx = _bias_tables()
    ng = len(GROUPS)
    buckets = [sorted(set(idx[g][idx[g] >= 0].tolist())) for g in range(ng)]

    def body(rb_ref, idx_ref, o_ref):
        h = pl.program_id(0)
        for g in range(ng):
            ig = idx_ref[g]
            acc = jnp.full(ig.shape, NEG, F32)
            for b in buckets[g]:
                acc = jnp.where(ig == b, rb_ref[b, g * HEADS + h], acc)
            o_ref[g] = acc

    return _pcall(
        body, name=name, grid=(HEADS,),
        in_specs=[pl.BlockSpec(memory_space=pltpu.SMEM),
                  pl.BlockSpec((ng, ATT_BLK, 2 * ATT_BLK), lambda h: (0, 0, 0))],
        out_specs=pl.BlockSpec((ng, None, ATT_BLK, 2 * ATT_BLK), lambda h: (0, h, 0, 0)),
        out_shape=_sds((ng, HEADS, ATT_BLK, 2 * ATT_BLK), F32),
        compiler_params=_cparams(("parallel",)),
    )(rel_bias, jnp.asarray(idx))


ADA_SHARD = 6 * D_MODEL // N_CHIP
ADA_TN = 512


def _ada_fwd(c_all, ada_w, ada_b_cols, name):
    def body(c_ref, w_ref, b_ref, o_ref):
        ca = _silu(c_ref[...]).astype(BF16)
        o_ref[...] = _dot(ca, w_ref[...].astype(BF16)) + b_ref[...]

    return _pcall(
        body, name=name, grid=(DEPTH, ADA_SHARD // ADA_TN),
        in_specs=[pl.BlockSpec((N_DEV, D_MODEL), lambda l, j: (0, 0)),
                  pl.BlockSpec((None, D_MODEL, ADA_TN), lambda l, j: (l, 0, j)),
                  pl.BlockSpec((None, 1, ADA_TN), lambda l, j: (l, 0, j))],
        out_specs=pl.BlockSpec((None, N_DEV, ADA_TN), lambda l, j: (l, 0, j)),
        out_shape=_sds((DEPTH, N_DEV, ADA_SHARD), F32),
        compiler_params=_cparams(("parallel", "parallel")),
    )(c_all, ada_w, ada_b_cols)


def _ada_bwd(c_all, dmod_cols, name):
    def body(c_ref, d_ref, o_ref):
        ca = _silu(c_ref[...]).astype(BF16)
        o_ref[...] = _dot_tn(ca, d_ref[...].astype(BF16))

    return _pcall(
        body, name=name, grid=(DEPTH, ADA_SHARD // ADA_TN),
        in_specs=[pl.BlockSpec((N_DEV, D_MODEL), lambda l, j: (0, 0)),
                  pl.BlockSpec((None, N_DEV, ADA_TN), lambda l, j: (l, 0, j))],
        out_specs=pl.BlockSpec((None, D_MODEL, ADA_TN), lambda l, j: (l, 0, j)),
        out_shape=_sds((DEPTH, D_MODEL, ADA_SHARD), F32),
        compiler_params=_cparams(("parallel", "parallel")),
    )(c_all, dmod_cols)


def _lower_bounds(logits, name):
    def body(l_ref, o_ref):
        l0 = l_ref[0:1, :]
        l1 = l_ref[1:2, :]
        mx = jnp.maximum(l0, l1)
        e0 = jnp.exp(l0 - mx)
        e1 = jnp.exp(l1 - mx)
        p0 = e0 / (e0 + e1)
        p1 = e1 / (e0 + e1)
        o_ref[0:1, :] = p0 - p0
        o_ref[1:2, :] = (p0 + p1) - p0

    return _pcall(body, name=name, out_shape=_sds((DEPTH, D_MODEL), F32), compiler_params=_cparams())(logits)


_R_DMOD = 0
_R_NMIX = 96
_R_NFFN = 112
_R_QG = 128
_R_KG = 152
_R_GN = 176
_R_LB = 184
_R_RB = 192
SMALL_ROWS = 200


def _small_totals(gathered, logits8, name):
    ng = len(GROUPS)

    def body(g_ref, l_ref, main_ref, gains_ref, dlb_ref, rb_ref):
        tot = g_ref[0]
        for dev in range(1, N_DEV):
            tot = tot + g_ref[dev]
        main_ref[...] = tot[0:_R_QG]
        gains_ref[...] = jnp.zeros_like(gains_ref)
        for g in range(ng):
            gains_ref[g:g + 1, :] = jnp.sum(tot[_R_QG + 8 * g:_R_QG + 8 * g + 8], axis=0, keepdims=True)
            gains_ref[ng + g:ng + g + 1, :] = jnp.sum(tot[_R_KG + 8 * g:_R_KG + 8 * g + 8], axis=0, keepdims=True)
        gains_ref[2 * ng:2 * ng + 1, :] = jnp.sum(tot[_R_GN:_R_GN + 8], axis=0, keepdims=True)
        rb_ref[...] = tot[_R_RB:_R_RB + 8]
        dlb1 = tot[_R_LB:_R_LB + 8]
        l0 = l_ref[0]
        l1 = l_ref[1]
        mx = jnp.maximum(l0, l1)
        e0 = jnp.exp(l0 - mx)
        e1 = jnp.exp(l1 - mx)
        p0 = e0 / (e0 + e1)
        p1 = e1 / (e0 + e1)
        dlb_ref[0] = -p0 * p1 * dlb1
        dlb_ref[1] = p1 * (1.0 - p1) * dlb1

    return _pcall(
        body, name=name,
        out_shape=[_sds((_R_QG, 128), F32), _sds((8, 128), F32), _sds((DEPTH, 8, 128), F32), _sds((8, 128), F32)],
        compiler_params=_cparams(),
    )(gathered, logits8)


def _row_tile(rows):
    return 128 if rows % 128 == 0 else rows


def _adamw(w, grads, m, v, name, join=()):
    nl, r, cdim = w.shape
    tr = _row_tile(r)
    n_j = len(join)
    n_tiles = r // tr

    def body(*refs):
        g_refs = refs[:nl]
        w_ref, m_ref, v_ref = refs[nl:nl + 3]
        go_ref, d_ref, mo_ref, vo_ref = refs[nl + 3 + n_j:nl + 7 + n_j]
        if n_j:
            comm_start, comm_wait = _rs_join(refs[nl + 7 + n_j:nl + 7 + 2 * n_j], *refs[nl + 7 + 2 * n_j:])
            pl.when((pl.program_id(0) == 0) & (pl.program_id(1) == 0))(comm_start)

        def step(g):
            m2 = ADAM_B1 * m_ref[...] + (1.0 - ADAM_B1) * g
            v2 = ADAM_B2 * v_ref[...] + (1.0 - ADAM_B2) * (g * g)
            m_hat = m2 / (1.0 - ADAM_B1 ** ADAM_STEP)
            v_hat = v2 / (1.0 - ADAM_B2 ** ADAM_STEP)
            go_ref[...] = g
            d_ref[...] = -ADAM_LR * (m_hat / (jnp.sqrt(v_hat) + ADAM_EPS) + ADAM_WD * w_ref[...])
            mo_ref[...] = m2
            vo_ref[...] = v2

        if nl == 1:
            step(g_refs[0][...])
        else:
            for layer in range(nl):
                @pl.when(pl.program_id(0) == layer)
                def _(layer=layer):
                    step(g_refs[layer][...])
        if n_j:
            pl.when((pl.program_id(0) == nl - 1) & (pl.program_id(1) == n_tiles - 1))(comm_wait)

    big = pl.BlockSpec((None, tr, cdim), lambda l, i: (l, i, 0))
    g_specs = [pl.BlockSpec((tr, cdim), lambda l, i, layer=layer: (jnp.where(l == layer, i, 0), 0))
               for layer in range(nl)]
    shp = _sds((nl, r, cdim), F32)
    sem = pltpu.SemaphoreType.DMA((max(n_j, 1),))
    return _pcall(
        body, name=name, grid=(nl, n_tiles),
        in_specs=g_specs + [big, big, big] + [_ANY] * n_j,
        out_specs=[big, big, big, big] + [_ANY] * n_j,
        out_shape=[shp, shp, shp, shp] + [_sds(f.shape, F32) for f in join],
        input_output_aliases={nl + 3 + a: 4 + a for a in range(n_j)},
        scratch_shapes=[sem, sem] if n_j else [],
        compiler_params=_cparams(("arbitrary", "arbitrary") if n_j else ("parallel", "parallel")),
    )(*grads, w, m, v, *join)


def _cast_bf16(place, w, name):
    nl, r, cdim = w.shape
    tr = _row_tile(r)

    def body(place_ref, w_ref, o_ref):
        o_ref[...] = w_ref[...].astype(BF16)

    return _pcall(
        body, name=name,
        grid_spec=pltpu.PrefetchScalarGridSpec(
            num_scalar_prefetch=1, grid=(nl, r // tr),
            in_specs=[pl.BlockSpec((None, tr, cdim), lambda l, i, place_ref: (l, i, 0))],
            out_specs=pl.BlockSpec((None, None, tr, cdim), lambda l, i, place_ref: (place_ref[1], l, i, 0))),
        out_shape=_sds((N_CHIP, nl, r, cdim), BF16),
        compiler_params=_cparams(("parallel", "parallel")),
    )(place, w)


def _rs_add_cast(place, grad, recv, name):
    _, k, n = grad.shape
    kh = k // 2
    tr = _row_tile(kh)
    nb = kh // tr

    def body(place_ref, g_ref, r_ref, o_ref):
        o_ref[...] = (g_ref[...] + r_ref[...]).astype(BF16)

    half = pl.BlockSpec((None, tr, n), lambda s, i, place_ref: (s, i, 0))
    return _pcall(
        body, name=name,
        grid_spec=pltpu.PrefetchScalarGridSpec(
            num_scalar_prefetch=1, grid=(N_CHIP, nb),
            in_specs=[pl.BlockSpec((None, tr, n), lambda s, i, place_ref: (s, place_ref[0] * nb + i, 0)), half],
            out_specs=half),
        out_shape=_sds((N_CHIP, kh, n), BF16),
        compiler_params=_cparams(("parallel", "parallel")),
    )(place, grad, recv)


def _rs_sum4(place, parts, got, name):
    _, kh, n = parts.shape
    tr = _row_tile(kh)
    nb = kh // tr

    def body(place_ref, p_ref, g_ref, o_ref):
        acc = p_ref[...].astype(F32)
        for j in range(N_CHIP - 1):
            acc = acc + g_ref[j].astype(F32)
        o_ref[...] = acc

    return _pcall(
        body, name=name,
        grid_spec=pltpu.PrefetchScalarGridSpec(
            num_scalar_prefetch=1, grid=(nb,),
            in_specs=[pl.BlockSpec((None, tr, n), lambda i, place_ref: (place_ref[1], i, 0)),
                      pl.BlockSpec((N_CHIP - 1, tr, n), lambda i, place_ref: (0, i, 0))],
            out_specs=pl.BlockSpec((tr, n), lambda i, place_ref: (place_ref[0] * nb + i, 0))),
        out_shape=_sds((2 * kh, n), F32),
        compiler_params=_cparams(("parallel",)),
    )(place, parts, got)


_ANY = pl.BlockSpec(memory_space=pl.ANY)


def _position():
    return lax.axis_index("x"), lax.axis_index("y"), lax.axis_index("c")


def _other_chips(x, y):
    return [(1 - x, y), (x, 1 - y), (1 - x, 1 - y)]


def _remote(src, dst, send_sem, recv_sem, to):
    return pltpu.make_async_remote_copy(src_ref=src, dst_ref=dst, send_sem=send_sem, recv_sem=recv_sem,
                                        device_id=to, device_id_type=MESH)


def _small_allgather(v, name):
    r = v.shape[0]

    def body(x_ref, out_ref, send_sems, recv_sems, local_sem):
        x, y, c = _position()
        me, sibling = (x, y, c), (x, y, 1 - c)
        chips = _other_chips(x, y)

        def slab(px, py, pc):
            return out_ref.at[4 * px + 2 * py + pc]

        def copy(k, block, to, src=None):
            return _remote(slab(*block) if src is None else src, slab(*block), send_sems.at[k], recv_sems.at[k], to)

        mine = pltpu.make_async_copy(x_ref, slab(*me), local_sem)
        mine.start()
        first = [copy(0, me, sibling, src=x_ref)]
        first += [copy(1 + j, me, (*chip, c), src=x_ref) for j, chip in enumerate(chips)]
        for cp in first:
            cp.start()
        passed = [copy(4 + j, (*chip, c), sibling) for j, chip in enumerate(chips)]
        for j, chip in enumerate(chips):
            copy(1 + j, (*chip, c), me).wait_recv()
            passed[j].start()
        copy(0, sibling, me).wait_recv()
        for j, chip in enumerate(chips):
            copy(4 + j, (*chip, 1 - c), me).wait_recv()
        for cp in first + passed:
            cp.wait_send()
        mine.wait()

    return _pcall(
        body, name=name,
        out_shape=_sds((N_DEV, r, 128), F32),
        in_specs=[pl.BlockSpec(memory_space=pltpu.VMEM)],
        out_specs=pl.BlockSpec(memory_space=pltpu.VMEM),
        scratch_shapes=[pltpu.SemaphoreType.DMA((7,)), pltpu.SemaphoreType.DMA((7,)), pltpu.SemaphoreType.DMA],
        compiler_params=_cparams(),
    )(v)


def _half_rows(core, kh):
    return pl.ds(pl.multiple_of(core * kh, 8), kh)


def _slab_half(ref, chip, core):
    return ref.at[chip, :, _half_rows(core, ref.shape[2] // 2), :]


def _gather_ici(out, send_sems, recv_sems):
    def copies():
        x, y, c = _position()
        for a in range(len(out)):
            for j, (px, py) in enumerate(_other_chips(x, y)):
                mine = _slab_half(out[a], 2 * x + y, c)
                landed = _slab_half(out[a], 2 * px + py, c)
                yield (_remote(mine, mine, send_sems.at[a, j], recv_sems.at[a, j], (px, py, c)),
                       _remote(landed, landed, send_sems.at[a, j], recv_sems.at[a, j], (px, py, c)))

    def start():
        for send, _ in copies():
            send.start()

    def wait():
        for send, recv in copies():
            recv.wait_recv()
            send.wait_send()

    return start, wait


def _gather_d2d(out, send_sems, recv_sems):
    def copies():
        x, y, c = _position()
        for a in range(len(out)):
            for j, (px, py) in enumerate(_other_chips(x, y)):
                landed = _slab_half(out[a], 2 * px + py, c)
                other = _slab_half(out[a], 2 * px + py, 1 - c)
                yield (_remote(landed, landed, send_sems.at[a, j], recv_sems.at[a, j], (x, y, 1 - c)),
                       _remote(other, other, send_sems.at[a, j], recv_sems.at[a, j], (x, y, 1 - c)))

    def start():
        for send, _ in copies():
            send.start()

    def wait():
        for send, recv in copies():
            recv.wait_recv()
            send.wait_send()

    return start, wait


def _gather_weights(slabs, name, ici=True):
    n = len(slabs)

    def body(*refs):
        out = refs[n:2 * n]
        sems = refs[2 * n:]
        if ici:
            start, wait = _gather_ici(out, sems[2], sems[3])
            start()
            wait()
        start, wait = _gather_d2d(out, sems[0], sems[1])
        start()
        wait()

    sem = pltpu.SemaphoreType.DMA((n, 3))
    return _pcall(
        body, name=name,
        out_shape=[_sds(s.shape, BF16) for s in slabs],
        in_specs=[_ANY] * n, out_specs=[_ANY] * n,
        input_output_aliases={a: a for a in range(n)},
        scratch_shapes=[sem, sem] + ([sem, sem] if ici else []),
        compiler_params=_cparams(),
    )(*slabs)


def _rs_halves(grads, out, send_sems, recv_sems):
    def copies():
        x, y, c = _position()
        for a in range(len(grads)):
            kh = grads[a].shape[1] // 2
            yield _remote(grads[a].at[:, _half_rows(1 - c, kh), :], out[a], send_sems.at[a], recv_sems.at[a],
                          (x, y, 1 - c))

    def start():
        for cp in copies():
            cp.start()

    def wait():
        for cp in copies():
            cp.wait()

    return start, wait


def _rs_halves_shapes(grads):
    return [_sds((N_CHIP, g.shape[1] // 2, g.shape[2]), F32) for g in grads]


def _rs_exchange_halves(grads, name):
    n = len(grads)

    def body(*refs):
        start, wait = _rs_halves(refs[:n], refs[n:2 * n], *refs[2 * n:])
        start()
        wait()

    return _pcall(
        body, name=name,
        out_shape=_rs_halves_shapes(grads),
        in_specs=[_ANY] * n, out_specs=[_ANY] * n,
        scratch_shapes=[pltpu.SemaphoreType.DMA((n,)), pltpu.SemaphoreType.DMA((n,))],
        compiler_params=_cparams(),
    )(*grads)


def _rs_chips(parts, out, send_sems, recv_sems):
    def copies():
        x, y, c = _position()
        for a in range(len(parts)):
            for j, (px, py) in enumerate(_other_chips(x, y)):
                got = out[a].at[j]
                yield (_remote(parts[a].at[2 * px + py], got, send_sems.at[a, j], recv_sems.at[a, j], (px, py, c)),
                       _remote(got, got, send_sems.at[a, j], recv_sems.at[a, j], (px, py, c)))

    def start():
        for send, _ in copies():
            send.start()

    def wait():
        for send, recv in copies():
            recv.wait_recv()
            send.wait_send()

    return start, wait


def _rs_chips_shapes(parts):
    return [_sds((N_CHIP - 1,) + p.shape[1:], BF16) for p in parts]


def _rs_join(out, send_sems, recv_sems):
    def copies():
        x, y, c = _position()
        for a in range(len(out)):
            kh = out[a].shape[0] // 2
            mine = out[a].at[_half_rows(c, kh), :]
            theirs = out[a].at[_half_rows(1 - c, kh), :]
            yield (_remote(mine, mine, send_sems.at[a], recv_sems.at[a], (x, y, 1 - c)),
                   _remote(theirs, theirs, send_sems.at[a], recv_sems.at[a], (x, y, 1 - c)))

    def start():
        for send, _ in copies():
            send.start()

    def wait():
        for send, recv in copies():
            recv.wait_recv()
            send.wait_send()

    return start, wait


_SMALL_ORDER = ("rel_bias", "ada_b", "norm_mix", "norm_ffn", "attn_q_gain", "attn_k_gain", "hgrn_gnorm",
                "hgrn_lower_bounds")
_WEIGHT_ORDER = ("rel_bias", "ada_w", "ada_b", "norm_mix", "norm_ffn", "attn_w_qkv", "attn_w_out", "attn_q_gain",
                 "attn_k_gain", "hgrn_w_in", "hgrn_w_out", "hgrn_gnorm", "hgrn_lower_bounds", "ffn_w1", "ffn_w3",
                 "ffn_w2")


def _qkv_group_map(t):
    return t // 4, t % 4


def _qkv_chip_map(t):
    return t // 9, t % 9


def _hin_map(t):
    return t // 2, t % 2


def _block_map(t):
    return t, 0


def _pack_rows(parts):
    return jnp.concatenate([p.reshape(-1, 128) for p in parts], axis=0)


def kernel(x, c, rel_bias, ada_w, ada_b, norm_mix, norm_ffn, attn_w_qkv, attn_w_out, attn_q_gain, attn_k_gain, hgrn_w_in, hgrn_w_out, hgrn_gnorm, hgrn_lower_bounds, ffn_w1, ffn_w3, ffn_w2, loss_target, m_rel_bias, m_ada_w, m_ada_b, m_norm_mix, m_norm_ffn, m_attn_w_qkv, m_attn_w_out, m_attn_q_gain, m_attn_k_gain, m_hgrn_w_in, m_hgrn_w_out, m_hgrn_gnorm, m_hgrn_lower_bounds, m_ffn_w1, m_ffn_w3, m_ffn_w2, v_rel_bias, v_ada_w, v_ada_b, v_norm_mix, v_norm_ffn, v_attn_w_qkv, v_attn_w_out, v_attn_q_gain, v_attn_k_gain, v_hgrn_w_in, v_hgrn_w_out, v_hgrn_gnorm, v_hgrn_lower_bounds, v_ffn_w1, v_ffn_w3, v_ffn_w2):
    weights = dict(rel_bias=rel_bias, ada_w=ada_w, ada_b=ada_b, norm_mix=norm_mix, norm_ffn=norm_ffn,
                   attn_w_qkv=attn_w_qkv, attn_w_out=attn_w_out, attn_q_gain=attn_q_gain, attn_k_gain=attn_k_gain,
                   hgrn_w_in=hgrn_w_in, hgrn_w_out=hgrn_w_out, hgrn_gnorm=hgrn_gnorm,
                   hgrn_lower_bounds=hgrn_lower_bounds, ffn_w1=ffn_w1, ffn_w3=ffn_w3, ffn_w2=ffn_w2)
    mom1 = dict(rel_bias=m_rel_bias, ada_w=m_ada_w, ada_b=m_ada_b, norm_mix=m_norm_mix, norm_ffn=m_norm_ffn,
                attn_w_qkv=m_attn_w_qkv, attn_w_out=m_attn_w_out, attn_q_gain=m_attn_q_gain,
                attn_k_gain=m_attn_k_gain, hgrn_w_in=m_hgrn_w_in, hgrn_w_out=m_hgrn_w_out, hgrn_gnorm=m_hgrn_gnorm,
                hgrn_lower_bounds=m_hgrn_lower_bounds, ffn_w1=m_ffn_w1, ffn_w3=m_ffn_w3, ffn_w2=m_ffn_w2)
    mom2 = dict(rel_bias=v_rel_bias, ada_w=v_ada_w, ada_b=v_ada_b, norm_mix=v_norm_mix, norm_ffn=v_norm_ffn,
                attn_w_qkv=v_attn_w_qkv, attn_w_out=v_attn_w_out, attn_q_gain=v_attn_q_gain,
                attn_k_gain=v_attn_k_gain, hgrn_w_in=v_hgrn_w_in, hgrn_w_out=v_hgrn_w_out, hgrn_gnorm=v_hgrn_gnorm,
                hgrn_lower_bounds=v_hgrn_lower_bounds, ffn_w1=v_ffn_w1, ffn_w3=v_ffn_w3, ffn_w2=v_ffn_w2)

    transposed = ("ffn_w1", "ffn_w3")
    for group in (weights, mom1, mom2):
        for k in transposed:
            group[k] = jnp.transpose(group[k], (0, 2, 1))

    xi, yi, ci = _position()
    chip = 2 * xi + yi
    dev = 4 * xi + 2 * yi + ci
    place = jnp.stack([ci, chip]).astype(jnp.int32)
    d = D_MODEL

    big_names = ("attn_w_qkv", "attn_w_out", "hgrn_w_in", "hgrn_w_out", "ffn_w1", "ffn_w3", "ffn_w2")
    early_names, late_names = big_names[:1], big_names[1:]
    slabs16 = {k: _cast_bf16(place, weights[k], "cast_" + k) for k in big_names}
    wg = dict(zip(early_names, _gather_weights([slabs16[k] for k in early_names], "gather_early")))

    c_all = _small_allgather(c.reshape(8, 128), "gather_c").reshape(N_DEV, d)
    ada_b_cols = lax.dynamic_slice(ada_b, (0, chip * ADA_SHARD), (DEPTH, ADA_SHARD)).reshape(DEPTH, 1, ADA_SHARD)
    mod_shard = _ada_fwd(c_all, ada_w, ada_b_cols, "ada_fwd")
    mod_all = _small_allgather(mod_shard.reshape(-1, 128), "gather_mod").reshape(N_DEV, DEPTH, N_DEV, ADA_SHARD)
    mod_mine = lax.dynamic_index_in_dim(mod_all[0::2], dev, axis=2, keepdims=False)
    mod = jnp.transpose(mod_mine, (1, 0, 2)).reshape(DEPTH, 6 * d)

    def mods(layer):
        return [mod[layer:layer + 1, j * d:(j + 1) * d] for j in range(6)]

    x0 = x.reshape(SEQ, d)
    target = loss_target.reshape(SEQ, d)
    qg = attn_q_gain.reshape(len(GROUPS), 1, HEAD_DIM)
    kg = attn_k_gain.reshape(len(GROUPS), 1, HEAD_DIM)
    bias = _attn_bias(rel_bias, "attn_bias")
    lb1 = _lower_bounds(hgrn_lower_bounds, "lower_bounds")[1:2]

    def ffn_fwd(layer, x_in, sc2, sh2, g2):
        hf = _norm_mod(x_in, norm_ffn[layer:layer + 1], sc2, sh2, f"l{layer}_norm_ffn")
        a1, a3, u = _ffn_up(hf, wg["ffn_w1"], wg["ffn_w3"], layer, f"l{layer}_ffn_up")
        z, x_out = _mm_rows(u, wg["ffn_w2"], layer, x_in, g2, f"l{layer}_ffn_down")
        return x_out, (hf, a1, a3, u, z)

    def ffn_bwd(layer, dz, dg2, dx_out, x_in, sc2, sh2, saved, mixer_branch, halves=()):
        hf, a1, a3, u, _ = saved
        da1, da3, *recv = _ffn_down_bwd(dz, wg["ffn_w2"], layer, a1, a3, f"l{layer}_ffn_down_bwd", halves=halves)
        dw2 = _mm_rows_bwd_w(u, dz, f"l{layer}_dw2")
        dh = _ffn_up_bwd(da1, da3, wg["ffn_w1"], wg["ffn_w3"], layer, f"l{layer}_ffn_up_bwd")
        dw1 = _mm_rows_bwd_w(da1, hf, f"l{layer}_dw1")
        dw3 = _mm_rows_bwd_w(da3, hf, f"l{layer}_dw3")
        dx_in, dsc2, dsh2, dnf, dz_mix, dg_mix = _norm_mod_bwd(x_in, norm_ffn[layer:layer + 1], sc2, sh2, dh, dx_out,
                                                               f"l{layer}_norm_ffn_bwd", branch=mixer_branch)
        return dx_in, (dw1, dw3, dw2), (dsh2, dsc2, dg2), dnf, recv, dz_mix, dg_mix

    def rs_add(tags, grads_in, recv):
        return [_rs_add_cast(place, g, r, f"rs_add_{k}_{layer}") for (k, layer), g, r in zip(tags, grads_in, recv)]

    sh1_0, sc1_0, g1_0, sh2_0, sc2_0, g2_0 = mods(0)
    h0 = _norm_mod(x0, norm_mix[0:1], sc1_0, sh1_0, "l0_norm_mix")
    w_qkv9 = _retile_cols(wg["attn_w_qkv"].reshape(N_CHIP, d, 2304), n_out=9, width_out=d, tn=256,
                          src_map=_qkv_chip_map, dst_map=_qkv_group_map, n_tiles=36,
                          name="regroup_w_qkv").reshape(9, 1, d, d)
    qkv9 = _mm_cols(h0, w_qkv9, 0, n_blocks=9, width=d, tn=d, act_map=_block_map, w_map=_block_map,
                    out_dtype=F32, name="l0_qkv")
    o4, lse, *late = _attn_fwd(qkv9, qg, kg, bias, "l0_attn", gather=[slabs16[k] for k in late_names])
    wg.update(zip(late_names, _gather_weights(late, "gather_late_siblings", ici=False)))
    y0, x1 = _mm_rows(o4, wg["attn_w_out"], 0, x0, g1_0, "l0_attn_out")
    x2, ffn0 = ffn_fwd(0, x1, sc2_0, sh2_0, g2_0)

    sh1_1, sc1_1, g1_1, sh2_1, sc2_1, g2_1 = mods(1)
    h1 = _norm_mod(x2, norm_mix[1:2], sc1_1, sh1_1, "l1_norm_mix")
    proj4 = _mm_cols(h1, wg["hgrn_w_in"], 0, n_blocks=4, width=d, tn=512, act_map=_hin_map, w_map=_hin_map,
                     out_dtype=F32, name="l1_hgrn_in")
    o_raw, yg4, states = _hgrn_fwd(proj4, lb1, hgrn_gnorm, "l1_hgrn")
    y1, x3 = _mm_rows(yg4, wg["hgrn_w_out"], 0, x2, g1_1, "l1_hgrn_out")
    x4, ffn1 = ffn_fwd(1, x3, sc2_1, sh2_1, g2_1)

    dx4, loss_part, dz_ffn1, dg2_1 = _loss_head(x4, target, ffn1[4], g2_1, "loss_head")
    loss = lax.psum(loss_part[0, 0], ("x", "y", "c"))

    dx3, (dw1_1, dw3_1, dw2_1), dmod2_1, dnf_1, _, dzm1, dg1_1 = ffn_bwd(
        1, dz_ffn1, dg2_1, dx4, x3, sc2_1, sh2_1, ffn1, (y1, g1_1))
    dyg4 = _mm_rows_bwd_a(dzm1, wg["hgrn_w_out"], 0, "l1_hgrn_out_bwd")
    dw_hout = _mm_rows_bwd_w(yg4, dzm1, "l1_dw_hgrn_out")
    dproj4, dlb_h, dgn_h = _hgrn_bwd(proj4, lb1, hgrn_gnorm, o_raw, dyg4, states, "l1_hgrn_bwd")
    dh1 = _mm_cols_bwd_a(dproj4, wg["hgrn_w_in"], 0, group=N_CHIP, name="l1_hgrn_in_bwd", tm=512)
    dw_hin = _mm_cols_bwd_w(h1, dproj4, ns=d, tn=d, act_map=_block_map, w_map=_block_map, n_tiles=N_CHIP,
                            name="l1_dw_hgrn_in", tm=2048)
    dx2, dsc1_1, dsh1_1, dnm_1, dz_ffn0, dg2_0 = _norm_mod_bwd(x2, norm_mix[1:2], sc1_1, sh1_1, dh1, dx3,
                                                               "l1_norm_mix_bwd", branch=(ffn0[4], g2_0))

    tags_1 = [("hgrn_w_in", 0), ("hgrn_w_out", 0), ("ffn_w1", 1), ("ffn_w3", 1), ("ffn_w2", 1)]
    grads_1 = [dw_hin, dw_hout, dw1_1, dw3_1, dw2_1]
    dx1, (dw1_0, dw3_0, dw2_0), dmod2_0, dnf_0, recv_1, dzm0, dg1_0 = ffn_bwd(
        0, dz_ffn0, dg2_0, dx2, x1, sc2_0, sh2_0, ffn0, (y0, g1_0), halves=grads_1)
    tags_0 = [("ffn_w1", 0), ("ffn_w3", 0), ("ffn_w2", 0)]
    grads_0 = [dw1_0, dw3_0, dw2_0]
    do4, *recv_0 = _mm_rows_bwd_a(dzm0, wg["attn_w_out"], 0, "l0_attn_out_bwd", halves=grads_0)
    dw_aout = _mm_rows_bwd_w(o4, dzm0, "l0_dw_attn_out")
    tags_a = tags_1 + tags_0
    parts_a = rs_add(tags_1, grads_1, recv_1) + rs_add(tags_0, grads_0, recv_0)
    dqkv, dqg_h, dkg_h, dbias, *got_a = _attn_bwd(qkv9, qg, kg, bias, do4, o4, lse, "l0_attn_bwd", scatter=parts_a)
    dqkv9 = dqkv.reshape(9, SEQ, d)
    dw_qkv9 = _mm_cols_bwd_w(h0, dqkv9, ns=d, tn=d, act_map=_block_map, w_map=_block_map, n_tiles=9,
                             name="l0_dw_qkv", tm=2048, n_out=9)
    dw_qkv = _retile_cols(dw_qkv9, n_out=N_CHIP, width_out=2304, tn=256, src_map=_qkv_group_map,
                          dst_map=_qkv_chip_map, n_tiles=36, name="regroup_dw_qkv")
    tags_b = [("attn_w_qkv", 0), ("attn_w_out", 0)]
    grads_b = [dw_qkv, dw_aout]
    parts_b = rs_add(tags_b, grads_b, _rs_exchange_halves(grads_b, "rs_exchange_halves_b"))
    dh0, *got_b = _mm_cols_bwd_a(dqkv9, w_qkv9, 0, group=3, name="l0_qkv_bwd", scatter=parts_b)
    dx0, dsc1_0, dsh1_0, dnm_0 = _norm_mod_bwd(x0, norm_mix[0:1], sc1_0, sh1_0, dh0, dx1, "l0_norm_mix_bwd")
    drb8 = _relbias_bwd(dbias, jnp.asarray(_bias_tables()), "rel_bias_bwd")

    small = _pack_rows([
        dsh1_0, dsc1_0, dg1_0, *dmod2_0, dsh1_1, dsc1_1, dg1_1, *dmod2_1,
        dnm_0, dnm_1, dnf_0, dnf_1,
        jnp.transpose(dqg_h, (1, 0, 2, 3)), jnp.transpose(dkg_h, (1, 0, 2, 3)), dgn_h, dlb_h, drb8])
    small_all = _small_allgather(small, "gather_small")
    main, gains, dlbnd, rbt = _small_totals(small_all, hgrn_lower_bounds.reshape(DEPTH, 8, 128), "small_totals")
    ng = len(GROUPS)
    grads = {
        "ada_b": main[_R_DMOD:_R_NMIX].reshape(DEPTH, 6 * d),
        "norm_mix": main[_R_NMIX:_R_NFFN].reshape(DEPTH, d),
        "norm_ffn": main[_R_NFFN:_R_QG].reshape(DEPTH, d),
        "attn_q_gain": gains[0:ng].reshape(1, ng, HEAD_DIM),
        "attn_k_gain": gains[ng:2 * ng].reshape(1, ng, HEAD_DIM),
        "hgrn_gnorm": gains[2 * ng:2 * ng + 1],
        "hgrn_lower_bounds": dlbnd.reshape(DEPTH, d),
        "rel_bias": jnp.transpose(rbt[:, :ng * NUM_BUCKETS].reshape(HEADS, ng, NUM_BUCKETS), (2, 1, 0))
                       .reshape(NUM_BUCKETS, ng * HEADS),
    }
    dmod_all = small_all[:, _R_DMOD:_R_NMIX].reshape(N_DEV, DEPTH, 6 * d)
    dmod_cols = jnp.transpose(lax.dynamic_slice(dmod_all, (0, 0, chip * ADA_SHARD), (N_DEV, DEPTH, ADA_SHARD)),
                              (1, 0, 2))
    grad_ada_w = _ada_bwd(c_all, dmod_cols, "ada_bwd")

    tags = tags_a + tags_b
    halves = [_rs_sum4(place, p, r, f"rs_sum_{k}_{layer}")
              for (k, layer), p, r in zip(tags, parts_a + parts_b, list(got_a) + list(got_b))]

    out_g, out_d, out_m, out_v = {}, {}, {}, {}
    shp = (1, DEPTH * d, ADA_SHARD)
    res = _adamw(ada_w.reshape(shp), [grad_ada_w.reshape(shp[1:])], m_ada_w.reshape(shp), v_ada_w.reshape(shp),
                 "adamw_ada_w", join=halves)
    out_g["ada_w"], out_d["ada_w"], out_m["ada_w"], out_v["ada_w"] = [r.reshape(ada_w.shape) for r in res[:4]]
    full = dict(zip(tags, res[4:]))
    for k in big_names:
        gs = [full[(k, layer)] for layer in range(weights[k].shape[0])]
        out_g[k], out_d[k], out_m[k], out_v[k] = _adamw(weights[k], gs, mom1[k], mom2[k], "adamw_" + k)
    packed = [_pack_rows([src[k] for k in _SMALL_ORDER])[None] for src in (weights, grads, mom1, mom2)]
    res = _adamw(packed[0], [packed[1][0]], packed[2], packed[3], "adamw_small")
    offset = 0
    for k in _SMALL_ORDER:
        size = weights[k].size
        for dst, r in zip((out_g, out_d, out_m, out_v), res):
            dst[k] = r.reshape(-1)[offset:offset + size].reshape(weights[k].shape)
        offset += size
    for dst in (out_g, out_d, out_m, out_v):
        for k in transposed:
            dst[k] = jnp.transpose(dst[k], (0, 2, 1))

    return (loss, dx0.reshape(x.shape), *[out_g[k] for k in _WEIGHT_ORDER], *[out_d[k] for k in _WEIGHT_ORDER],
            *[out_m[k] for k in _WEIGHT_ORDER], *[out_v[k] for k in _WEIGHT_ORDER])
```

```python
import functools

import numpy as np
import jax
import jax.numpy as jnp
from jax import lax
from jax.experimental import pallas as pl
from jax.experimental.pallas import tpu as pltpu

F32 = jnp.float32
BF16 = jnp.bfloat16

D_MODEL = 1024
SEQ = 4096
N_DEV = 8
N_CHIP = 4
DEPTH = 2
HEADS = 8
HEAD_DIM = 128
GROUPS = ((128, 1), (512, 4), (2048, 16))
ATT_BLK = 128
ATT_WAYS = 4
ATT_STEPS = SEQ // ATT_BLK // ATT_WAYS
NUM_BUCKETS = 32
MAX_DISTANCE = 2048
FFN_HIDDEN = 2816
FFN_SHARD = FFN_HIDDEN // N_CHIP
HG_SUB = 16
HG_TC = 512
HG_HP = 4
RMS_EPS = 1e-6
NEG = -1e30
ATT_SCALE = HEAD_DIM ** -0.5
ADAM_LR, ADAM_B1, ADAM_B2, ADAM_EPS, ADAM_WD, ADAM_STEP = 0.001, 0.9, 0.999, 1e-08, 0.01, 10
VMEM_LIMIT = 56 * 1024 * 1024
MESH = pl.DeviceIdType.MESH


def _pcall(body, **kw):
    return pl.pallas_call(body, **kw)


def _cparams(sem=None):
    if sem is None:
        return pltpu.CompilerParams(vmem_limit_bytes=VMEM_LIMIT)
    return pltpu.CompilerParams(dimension_semantics=sem, vmem_limit_bytes=VMEM_LIMIT)


def _sds(shape, dtype):
    return jax.ShapeDtypeStruct(shape, dtype)


def _dot(a, b):
    return jnp.dot(a, b, preferred_element_type=F32)


def _dot_nt(a, b):
    return lax.dot_general(a, b, (((1,), (1,)), ((), ())), preferred_element_type=F32)


def _dot_tn(a, b):
    return lax.dot_general(a, b, (((0,), (0,)), ((), ())), preferred_element_type=F32)


def _sigmoid(x):
    return 1.0 / (1.0 + jnp.exp(-x))


def _silu(x):
    return x * _sigmoid(x)


def _dsilu(x):
    s = _sigmoid(x)
    return s * (1.0 + x * (1.0 - s))


def _norm_mod(x, gain, sc, sh, name):
    tm = 512

    def body(x_ref, g_ref, sc_ref, sh_ref, h_ref):
        xv = x_ref[...]
        rs = lax.rsqrt(jnp.mean(xv * xv, axis=-1, keepdims=True) + RMS_EPS)
        h_ref[...] = ((xv * rs * g_ref[...]) * (1.0 + sc_ref[...]) + sh_ref[...]).astype(BF16)

    vec = pl.BlockSpec((1, D_MODEL), lambda i: (0, 0))
    return _pcall(
        body, name=name, grid=(SEQ // tm,),
        in_specs=[pl.BlockSpec((tm, D_MODEL), lambda i: (i, 0)), vec, vec, vec],
        out_specs=pl.BlockSpec((tm, D_MODEL), lambda i: (i, 0)),
        out_shape=_sds((SEQ, D_MODEL), BF16),
        compiler_params=_cparams(("parallel",)),
    )(x, gain, sc, sh)


def _gated_branch_bwd(dx, z_ref, gate_ref, dz_ref, dgate_ref):
    dz_ref[...] = (dx * gate_ref[...]).astype(BF16)
    dgate_ref[...] += jnp.sum(dx * z_ref[...], axis=0, keepdims=True)


def _norm_mod_bwd(x, gain, sc, sh, dh, dres, name, branch=None):
    tm = 512
    n_b = 2 if branch else 0

    def body(*refs):
        x_ref, g_ref, sc_ref, sh_ref, dh_ref, dres_ref = refs[:6]
        dx_ref, dsc_ref, dsh_ref, dg_ref = refs[6 + n_b:10 + n_b]

        @pl.when(pl.program_id(0) == 0)
        def _():
            dsc_ref[...] = jnp.zeros_like(dsc_ref)
            dsh_ref[...] = jnp.zeros_like(dsh_ref)
            dg_ref[...] = jnp.zeros_like(dg_ref)
            if branch:
                refs[11 + n_b][...] = jnp.zeros_like(refs[11 + n_b])

        xv = x_ref[...]
        dhv = dh_ref[...]
        rs = lax.rsqrt(jnp.mean(xv * xv, axis=-1, keepdims=True) + RMS_EPS)
        xh = xv * rs
        dsc_ref[...] += jnp.sum(dhv * (xh * g_ref[...]), axis=0, keepdims=True)
        dsh_ref[...] += jnp.sum(dhv, axis=0, keepdims=True)
        dhn = dhv * (1.0 + sc_ref[...])
        dg_ref[...] += jnp.sum(dhn * xh, axis=0, keepdims=True)
        dxh = dhn * g_ref[...]
        dx = dres_ref[...] + rs * (dxh - xh * jnp.mean(dxh * xh, axis=-1, keepdims=True))
        dx_ref[...] = dx
        if branch:
            _gated_branch_bwd(dx, refs[6], refs[7], refs[10 + n_b], refs[11 + n_b])

    vec = pl.BlockSpec((1, D_MODEL), lambda i: (0, 0))
    big = pl.BlockSpec((tm, D_MODEL), lambda i: (i, 0))
    return _pcall(
        body, name=name, grid=(SEQ // tm,),
        in_specs=[big, vec, vec, vec, big, big] + ([big, vec] if branch else []),
        out_specs=[big, vec, vec, vec] + ([big, vec] if branch else []),
        out_shape=[_sds((SEQ, D_MODEL), F32)] + [_sds((1, D_MODEL), F32)] * 3
        + ([_sds((SEQ, D_MODEL), BF16), _sds((1, D_MODEL), F32)] if branch else []),
        compiler_params=_cparams(("arbitrary",)),
    )(x, gain, sc, sh, dh, dres, *(branch or ()))


def _mm_cols(a, wg, layer, *, n_blocks, width, tn, act_map, w_map, out_dtype, name, tm=1024):
    k = a.shape[1]
    n_tiles = n_blocks * width // tn

    def body(a_ref, w_ref, o_ref):
        o_ref[...] = _dot(a_ref[...], w_ref[...]).astype(o_ref.dtype)

    return _pcall(
        body, name=name, grid=(SEQ // tm, n_tiles),
        in_specs=[pl.BlockSpec((tm, k), lambda i, t: (i, 0)),
                  pl.BlockSpec((None, None, k, tn), lambda i, t: (w_map(t)[0], layer, 0, w_map(t)[1]))],
        out_specs=pl.BlockSpec((None, tm, tn), lambda i, t: (act_map(t)[0], i, act_map(t)[1])),
        out_shape=_sds((n_blocks, SEQ, width), out_dtype),
        compiler_params=_cparams(("parallel", "arbitrary")),
    )(a, wg)


def _mm_cols_bwd_a(dout, wg, layer, *, group, name, tm=1024, scatter=()):
    n_blocks, _, width = dout.shape
    k = wg.shape[2]
    n_s = len(scatter)
    n_rows = SEQ // tm
    n_steps = n_blocks // group

    def body(*refs):
        d_ref, w_ref = refs[:2]
        o_ref = refs[2 + n_s]
        if n_s:
            comm_start, comm_wait = _rs_chips(refs[2:2 + n_s], refs[3 + n_s:3 + 2 * n_s], *refs[3 + 2 * n_s:])
            pl.when((pl.program_id(0) == 0) & (pl.program_id(1) == 0))(comm_start)
        acc = _dot_nt(d_ref[0], w_ref[0])
        for b in range(1, group):
            acc += _dot_nt(d_ref[b], w_ref[b])
        if n_steps == 1:
            o_ref[...] = acc
        else:
            @pl.when(pl.program_id(1) == 0)
            def _():
                o_ref[...] = acc

            @pl.when(pl.program_id(1) > 0)
            def _():
                o_ref[...] += acc
        if n_s:
            pl.when((pl.program_id(0) == n_rows - 1) & (pl.program_id(1) == n_steps - 1))(comm_wait)

    sem = pltpu.SemaphoreType.DMA((max(n_s, 1), 3))
    res = _pcall(
        body, name=name, grid=(n_rows, n_steps),
        in_specs=[pl.BlockSpec((group, tm, width), lambda i, t: (t, i, 0)),
                  pl.BlockSpec((group, None, k, width), lambda i, t: (t, layer, 0, 0))] + [_ANY] * n_s,
        out_specs=[pl.BlockSpec((tm, k), lambda i, t: (i, 0))] + [_ANY] * n_s,
        out_shape=[_sds((SEQ, k), F32)] + _rs_chips_shapes(scatter),
        scratch_shapes=[sem, sem] if n_s else [],
        compiler_params=_cparams(("arbitrary", "arbitrary") if n_s else ("parallel", "arbitrary")),
    )(dout, wg, *scatter)
    return res if n_s else res[0]


def _mm_cols_bwd_w(a, dout, *, ns, tn, act_map, w_map, n_tiles, name, tm=1024, n_out=N_CHIP):
    k = a.shape[1]

    def body(a_ref, d_ref, o_ref):
        @pl.when(pl.program_id(1) == 0)
        def _():
            o_ref[...] = jnp.zeros_like(o_ref)

        o_ref[...] += _dot_tn(a_ref[...], d_ref[...])

    return _pcall(
        body, name=name, grid=(n_tiles, SEQ // tm),
        in_specs=[pl.BlockSpec((tm, k), lambda t, i: (i, 0)),
                  pl.BlockSpec((None, tm, tn), lambda t, i: (act_map(t)[0], i, act_map(t)[1]))],
        out_specs=pl.BlockSpec((None, k, tn), lambda t, i: (w_map(t)[0], 0, w_map(t)[1])),
        out_shape=_sds((n_out, k, ns), F32),
        compiler_params=_cparams(("parallel", "arbitrary")),
    )(a, dout)


def _retile_cols(src, *, n_out, width_out, tn, src_map, dst_map, n_tiles, name):
    k = src.shape[1]

    def body(s_ref, o_ref):
        o_ref[...] = s_ref[...]

    return _pcall(
        body, name=name, grid=(n_tiles,),
        in_specs=[pl.BlockSpec((None, k, tn), lambda t: (src_map(t)[0], 0, src_map(t)[1]))],
        out_specs=pl.BlockSpec((None, k, tn), lambda t: (dst_map(t)[0], 0, dst_map(t)[1])),
        out_shape=_sds((n_out, k, width_out), src.dtype),
        compiler_params=_cparams(("parallel",)),
    )(src)


def _mm_rows(a4, wg, layer, x, gate, name, tm=512):
    ks = a4.shape[2]
    n = wg.shape[3]

    def body(a_ref, w_ref, x_ref, g_ref, z_ref, xn_ref):
        z = _dot(a_ref[0], w_ref[0])
        for s in range(1, N_CHIP):
            z += _dot(a_ref[s], w_ref[s])
        z_ref[...] = z
        xn_ref[...] = x_ref[...] + g_ref[...] * z

    big = pl.BlockSpec((tm, n), lambda i: (i, 0))
    return _pcall(
        body, name=name, grid=(SEQ // tm,),
        in_specs=[pl.BlockSpec((N_CHIP, tm, ks), lambda i: (0, i, 0)),
                  pl.BlockSpec((N_CHIP, None, ks, n), lambda i: (0, layer, 0, 0)),
                  big, pl.BlockSpec((1, n), lambda i: (0, 0))],
        out_specs=[big, big],
        out_shape=[_sds((SEQ, n), F32), _sds((SEQ, n), F32)],
        compiler_params=_cparams(("parallel",)),
    )(a4, wg, x, gate)


def _mm_rows_bwd_a(dz, wg, layer, name, tm=1024, halves=()):
    ks, n = wg.shape[2], wg.shape[3]
    n_h = len(halves)
    n_rows = SEQ // tm

    def body(*refs):
        dz_ref, w_ref = refs[:2]
        o_ref = refs[2 + n_h]
        if n_h:
            comm_start, comm_wait = _rs_halves(refs[2:2 + n_h], refs[3 + n_h:3 + 2 * n_h], *refs[3 + 2 * n_h:])
            pl.when((pl.program_id(0) == 0) & (pl.program_id(1) == 0))(comm_start)
        o_ref[...] = _dot_nt(dz_ref[...], w_ref[...])
        if n_h:
            pl.when((pl.program_id(0) == n_rows - 1) & (pl.program_id(1) == N_CHIP - 1))(comm_wait)

    sem = pltpu.SemaphoreType.DMA((max(n_h, 1),))
    res = _pcall(
        body, name=name, grid=(n_rows, N_CHIP),
        in_specs=[pl.BlockSpec((tm, n), lambda i, s: (i, 0)),
                  pl.BlockSpec((None, None, ks, n), lambda i, s: (s, layer, 0, 0))] + [_ANY] * n_h,
        out_specs=[pl.BlockSpec((None, tm, ks), lambda i, s: (s, i, 0))] + [_ANY] * n_h,
        out_shape=[_sds((N_CHIP, SEQ, ks), F32)] + _rs_halves_shapes(halves),
        scratch_shapes=[sem, sem] if n_h else [],
        compiler_params=_cparams(("arbitrary", "arbitrary") if n_h else ("parallel", "arbitrary")),
    )(dz, wg, *halves)
    return res if n_h else res[0]


def _mm_rows_bwd_w(a4, dz, name, tm=2048):
    ks = a4.shape[2]
    n = dz.shape[1]

    def body(a_ref, dz_ref, o_ref):
        @pl.when(pl.program_id(1) == 0)
        def _():
            o_ref[...] = jnp.zeros_like(o_ref)

        o_ref[...] += _dot_tn(a_ref[...], dz_ref[...])

    return _pcall(
        body, name=name, grid=(N_CHIP, SEQ // tm),
        in_specs=[pl.BlockSpec((None, tm, ks), lambda s, i: (s, i, 0)),
                  pl.BlockSpec((tm, n), lambda s, i: (i, 0))],
        out_specs=pl.BlockSpec((None, ks, n), lambda s, i: (s, 0, 0)),
        out_shape=_sds((N_CHIP, ks, n), F32),
        compiler_params=_cparams(("parallel", "arbitrary")),
    )(a4, dz)


def _ffn_up(h, w1g, w3g, layer, name, tm=1024):
    def body(h_ref, w1_ref, w3_ref, a1_ref, a3_ref, u_ref):
        hv = h_ref[...]
        a1 = _dot_nt(hv, w1_ref[...])
        a3 = _dot_nt(hv, w3_ref[...])
        a1_ref[...] = a1
        a3_ref[...] = a3
        u_ref[...] = (_silu(a1) * a3).astype(BF16)

    wspec = pl.BlockSpec((None, None, FFN_SHARD, D_MODEL), lambda i, s: (s, layer, 0, 0))
    ospec = pl.BlockSpec((None, tm, FFN_SHARD), lambda i, s: (s, i, 0))
    shp = (N_CHIP, SEQ, FFN_SHARD)
    return _pcall(
        body, name=name, grid=(SEQ // tm, N_CHIP),
        in_specs=[pl.BlockSpec((tm, D_MODEL), lambda i, s: (i, 0)), wspec, wspec],
        out_specs=[ospec, ospec, ospec],
        out_shape=[_sds(shp, F32), _sds(shp, F32), _sds(shp, BF16)],
        compiler_params=_cparams(("parallel", "arbitrary")),
    )(h, w1g, w3g)


def _ffn_up_bwd(da1, da3, w1g, w3g, layer, name, tm=512):
    def body(d1_ref, d3_ref, w1_ref, w3_ref, o_ref):
        acc = _dot(d1_ref[0], w1_ref[0]) + _dot(d3_ref[0], w3_ref[0])
        for s in range(1, N_CHIP):
            acc += _dot(d1_ref[s], w1_ref[s]) + _dot(d3_ref[s], w3_ref[s])
        o_ref[...] = acc

    wspec = pl.BlockSpec((N_CHIP, None, FFN_SHARD, D_MODEL), lambda i: (0, layer, 0, 0))
    dspec = pl.BlockSpec((N_CHIP, tm, FFN_SHARD), lambda i: (0, i, 0))
    return _pcall(
        body, name=name, grid=(SEQ // tm,),
        in_specs=[dspec, dspec, wspec, wspec],
        out_specs=pl.BlockSpec((tm, D_MODEL), lambda i: (i, 0)),
        out_shape=_sds((SEQ, D_MODEL), F32),
        compiler_params=_cparams(("parallel",)),
    )(da1, da3, w1g, w3g)


def _ffn_down_bwd(dz, w2g, layer, a1, a3, name, tm=1024, halves=()):
    n_h = len(halves)
    n_rows = SEQ // tm

    def body(*refs):
        dz_ref, w_ref, a1_ref, a3_ref = refs[:4]
        da1_ref, da3_ref = refs[4 + n_h:6 + n_h]
        if n_h:
            comm_start, comm_wait = _rs_halves(refs[4:4 + n_h], refs[6 + n_h:6 + 2 * n_h], *refs[6 + 2 * n_h:])
            pl.when((pl.program_id(0) == 0) & (pl.program_id(1) == 0))(comm_start)
        du = _dot_nt(dz_ref[...], w_ref[...])
        a1 = a1_ref[...]
        da1_ref[...] = (du * a3_ref[...] * _dsilu(a1)).astype(BF16)
        da3_ref[...] = (du * _silu(a1)).astype(BF16)
        if n_h:
            pl.when((pl.program_id(0) == n_rows - 1) & (pl.program_id(1) == N_CHIP - 1))(comm_wait)

    blk = pl.BlockSpec((None, tm, FFN_SHARD), lambda i, s: (s, i, 0))
    shp = (N_CHIP, SEQ, FFN_SHARD)
    sem = pltpu.SemaphoreType.DMA((max(n_h, 1),))
    return _pcall(
        body, name=name, grid=(n_rows, N_CHIP),
        in_specs=[pl.BlockSpec((tm, D_MODEL), lambda i, s: (i, 0)),
                  pl.BlockSpec((None, None, FFN_SHARD, D_MODEL), lambda i, s: (s, layer, 0, 0)),
                  blk, blk] + [_ANY] * n_h,
        out_specs=[blk, blk] + [_ANY] * n_h,
        out_shape=[_sds(shp, BF16), _sds(shp, BF16)] + _rs_halves_shapes(halves),
        scratch_shapes=[sem, sem] if n_h else [],
        compiler_params=_cparams(("arbitrary", "arbitrary") if n_h else ("parallel", "arbitrary")),
    )(dz, w2g, a1, a3, *halves)


def _loss_head(y, target, z, gate, name):
    tm = 512
    n_steps = SEQ // tm

    def body(y_ref, t_ref, z_ref, gate_ref, dy_ref, l_ref, dz_ref, dgate_ref, acc_ref):
        @pl.when(pl.program_id(0) == 0)
        def _():
            acc_ref[...] = jnp.zeros_like(acc_ref)
            dgate_ref[...] = jnp.zeros_like(dgate_ref)

        err = y_ref[...] - t_ref[...]
        dy = err * (1.0 / D_MODEL)
        dy_ref[...] = dy
        acc_ref[...] += jnp.sum(jnp.mean(err * err, axis=-1, keepdims=True), axis=0, keepdims=True)
        _gated_branch_bwd(dy, z_ref, gate_ref, dz_ref, dgate_ref)

        @pl.when(pl.program_id(0) == n_steps - 1)
        def _():
            l_ref[...] = 0.5 * acc_ref[...]

    big = pl.BlockSpec((tm, D_MODEL), lambda i: (i, 0))
    vec = pl.BlockSpec((1, D_MODEL), lambda i: (0, 0))
    return _pcall(
        body, name=name, grid=(n_steps,),
        in_specs=[big, big, big, vec],
        out_specs=[big, pl.BlockSpec((1, 1), lambda i: (0, 0)), big, vec],
        out_shape=[_sds((SEQ, D_MODEL), F32), _sds((1, 1), F32), _sds((SEQ, D_MODEL), BF16),
                   _sds((1, D_MODEL), F32)],
        scratch_shapes=[pltpu.VMEM((1, 1), F32)],
        compiler_params=_cparams(("arbitrary",)),
    )(y, target, z, gate)


def _attn_rows(base, d):
    if d == 1:
        return pl.ds(pl.multiple_of(base, ATT_BLK), ATT_BLK)
    return pl.ds(base, ATT_BLK, stride=d)


def _attn_block_index(i, d):
    nb = SEQ // (ATT_BLK * d)
    r = i // nb
    n = i % nb
    base = r + n * (ATT_BLK * d)
    pbase = jnp.maximum(base - ATT_BLK * d, r)
    return n, _attn_rows(base, d), _attn_rows(pbase, d)


def _attn_two_blocks(ref, prow, rows):
    return jnp.concatenate([ref[prow, :].astype(BF16), ref[rows, :].astype(BF16)], axis=0)


def _attn_block_bias(b_ref, n):
    b = b_ref[...]
    prev_half = lax.broadcasted_iota(jnp.int32, b.shape, 1) < ATT_BLK
    return jnp.where(prev_half & (n == 0), NEG, b)


def _qk_normed(x):
    rs = lax.rsqrt(jnp.mean(x * x, axis=-1, keepdims=True) + RMS_EPS)
    return x * rs, rs


def _attn_fwd(qkv9, qgain, kgain, bias, name, gather=()):
    n_g = len(gather)

    def body(*refs):
        q_ref, k_ref, v_ref, qg_ref, kg_ref, b_ref = refs[:6]
        o_ref, lse_ref = refs[6 + n_g:8 + n_g]
        qn_s, kn_s, acc_s, m_s, l_s = refs[8 + 2 * n_g:13 + 2 * n_g]
        g = pl.program_id(1)
        if n_g:
            comm_start, comm_wait = _gather_ici(refs[8 + n_g:8 + 2 * n_g], *refs[13 + 2 * n_g:])
            pl.when((pl.program_id(0) == 0) & (g == 0))(comm_start)

        @pl.when(g == 0)
        def _():
            m_s[...] = jnp.full_like(m_s, NEG)
            l_s[...] = jnp.zeros_like(l_s)
            acc_s[...] = jnp.zeros_like(acc_s)

        qn_s[...] = _qk_normed(q_ref[...])[0] * qg_ref[...]
        kn_s[...] = _qk_normed(k_ref[...])[0] * kg_ref[...]

        for gi, (_, d) in enumerate(GROUPS):
            @pl.when(g == gi)
            def _(d=d):
                def block(n, qb, kk, vv, m_old, l_old, acc_old):
                    s = _dot_nt(qb, kk) * ATT_SCALE + _attn_block_bias(b_ref, n)
                    m_new = jnp.maximum(m_old, jnp.max(s, axis=-1, keepdims=True))
                    alpha = jnp.exp(m_old - m_new)
                    p = jnp.exp(s - m_new)
                    l_new = alpha * l_old + jnp.sum(p, axis=-1, keepdims=True)
                    acc_new = alpha * acc_old + _dot(p.astype(BF16), vv)
                    return m_new, l_new, acc_new

                def it(i, carry):
                    where, loaded = [], []
                    for way in range(ATT_WAYS):
                        n, rows, prow = _attn_block_index(i + way * ATT_STEPS, d)
                        where.append(rows)
                        loaded.append((n, qn_s[rows, :].astype(BF16), _attn_two_blocks(kn_s, prow, rows),
                                       _attn_two_blocks(v_ref, prow, rows), m_s[rows, :], l_s[rows, :],
                                       acc_s[rows, :]))
                    results = [block(*vals) for vals in loaded]
                    for rows, (m_new, l_new, acc_new) in zip(where, results):
                        m_s[rows, :] = m_new
                        l_s[rows, :] = l_new
                        acc_s[rows, :] = acc_new
                    return carry

                lax.fori_loop(0, ATT_STEPS, it, 0)

        @pl.when(g == len(GROUPS) - 1)
        def _():
            o_ref[...] = (acc_s[...] / l_s[...]).astype(BF16)
            lse_ref[...] = m_s[...] + jnp.log(l_s[...])

        if n_g:
            pl.when((pl.program_id(0) == HEADS - 1) & (g == len(GROUPS) - 1))(comm_wait)

    def col(j):
        return pl.BlockSpec((None, SEQ, HEAD_DIM), lambda h, g: (g * 3 + j, 0, h))

    gspec = pl.BlockSpec((None, 1, HEAD_DIM), lambda h, g: (g, 0, 0))
    sem = pltpu.SemaphoreType.DMA((max(n_g, 1), 3))
    return _pcall(
        body, name=name, grid=(HEADS, len(GROUPS)),
        in_specs=[col(0), col(1), col(2), gspec, gspec,
                  pl.BlockSpec((None, None, ATT_BLK, 2 * ATT_BLK), lambda h, g: (g, h, 0, 0))] + [_ANY] * n_g,
        out_specs=[pl.BlockSpec((None, SEQ, HEAD_DIM), lambda h, g: (h // 2, 0, h % 2)),
                   pl.BlockSpec((None, SEQ, 1), lambda h, g: (h, 0, 0))] + [_ANY] * n_g,
        out_shape=[_sds((N_CHIP, SEQ, 2 * HEAD_DIM), BF16), _sds((HEADS, SEQ, 1), F32)]
        + [_sds(s.shape, s.dtype) for s in gather],
        input_output_aliases={6 + a: 2 + a for a in range(n_g)},
        scratch_shapes=[pltpu.VMEM((SEQ, HEAD_DIM), F32)] * 3 + [pltpu.VMEM((SEQ, 1), F32)] * 2
        + ([sem, sem] if n_g else []),
        compiler_params=_cparams(("arbitrary", "arbitrary")),
    )(qkv9, qkv9, qkv9, qgain, kgain, bias, *gather)


def _attn_bwd(qkv9, qgain, kgain, bias, do4, o4, lse, name, scatter=()):
    n_s = len(scatter)

    def body(*refs):
        q_ref, k_ref, v_ref, qg_ref, kg_ref, b_ref, do_ref, o_ref, lse_ref = refs[:9]
        dqkv_ref, dqg_ref, dkg_ref, db_ref = refs[9 + n_s:13 + n_s]
        qn_s, kn_s, dq_s, dk_s, dv_s, dl_s = refs[13 + 2 * n_s:19 + 2 * n_s]
        g = pl.program_id(1)
        if n_s:
            comm_start, comm_wait = _rs_chips(refs[9:9 + n_s], refs[13 + n_s:13 + 2 * n_s], *refs[19 + 2 * n_s:])
            pl.when((pl.program_id(0) == 0) & (g == 0))(comm_start)
        qh, rq = _qk_normed(q_ref[...])
        kh, rk = _qk_normed(k_ref[...])
        qn_s[...] = qh * qg_ref[...]
        kn_s[...] = kh * kg_ref[...]
        @pl.when(g == 0)
        def _():
            dl_s[...] = jnp.sum(do_ref[...] * o_ref[...].astype(F32), axis=-1, keepdims=True)

        dk_s[...] = jnp.zeros_like(dk_s)
        dv_s[...] = jnp.zeros_like(dv_s)
        db_ref[...] = jnp.zeros_like(db_ref)

        for gi, (_, d) in enumerate(GROUPS):
            @pl.when(g == gi)
            def _(d=d):
                def block(n, qb, kk, vv, dob, lse_b, dl):
                    s = _dot_nt(qb, kk) * ATT_SCALE + _attn_block_bias(b_ref, n)
                    p = jnp.exp(s - lse_b)
                    ds = p * (_dot_nt(dob, vv) - dl)
                    ds16 = ds.astype(BF16)
                    return (ds, _dot(ds16, kk) * ATT_SCALE, _dot_tn(ds16, qb) * ATT_SCALE,
                            _dot_tn(p.astype(BF16), dob))

                def it(i, carry):
                    where, loaded, old = [], [], []
                    for way in range(ATT_WAYS):
                        n, rows, prow = _attn_block_index(i + way * ATT_STEPS, d)
                        where.append((rows, prow))
                        loaded.append((n, qn_s[rows, :].astype(BF16), _attn_two_blocks(kn_s, prow, rows),
                                       _attn_two_blocks(v_ref, prow, rows), do_ref[rows, :].astype(BF16),
                                       lse_ref[rows, :], dl_s[rows, :]))
                        old.append((dk_s[rows, :], dk_s[prow, :], dv_s[rows, :], dv_s[prow, :]))
                    results = [block(*vals) for vals in loaded]
                    db_ref[...] += functools.reduce(lambda a, b: a + b, [r[0] for r in results])
                    for (rows, prow), (dk_c, dk_p, dv_c, dv_p), (_, dq, dkk, dvv) in zip(where, old, results):
                        dq_s[rows, :] = dq
                        dk_s[prow, :] = dk_p + dkk[:ATT_BLK]
                        dv_s[prow, :] = dv_p + dvv[:ATT_BLK]
                        dk_s[rows, :] = dk_c + dkk[ATT_BLK:]
                        dv_s[rows, :] = dv_c + dvv[ATT_BLK:]
                    return carry

                lax.fori_loop(0, ATT_STEPS, it, 0)

        def norm_bwd(dn, xh, rs, gain):
            dgain = jnp.sum(dn * xh, axis=0, keepdims=True)
            dxh = dn * gain
            return rs * (dxh - xh * jnp.mean(dxh * xh, axis=-1, keepdims=True)), dgain

        dq, dqg = norm_bwd(dq_s[...], qh, rq, qg_ref[...])
        dk, dkg = norm_bwd(dk_s[...], kh, rk, kg_ref[...])
        dqkv_ref[0] = dq.astype(BF16)
        dqkv_ref[1] = dk.astype(BF16)
        dqkv_ref[2] = dv_s[...].astype(BF16)
        dqg_ref[...] = dqg
        dkg_ref[...] = dkg
        if n_s:
            pl.when((pl.program_id(0) == HEADS - 1) & (g == len(GROUPS) - 1))(comm_wait)

    def col(j):
        return pl.BlockSpec((None, SEQ, HEAD_DIM), lambda h, g: (g * 3 + j, 0, h))

    gspec = pl.BlockSpec((None, 1, HEAD_DIM), lambda h, g: (g, 0, 0))
    bspec = pl.BlockSpec((None, None, ATT_BLK, 2 * ATT_BLK), lambda h, g: (g, h, 0, 0))
    hcol = pl.BlockSpec((None, SEQ, HEAD_DIM), lambda h, g: (h // 2, 0, h % 2))
    dgspec = pl.BlockSpec((None, None, 1, HEAD_DIM), lambda h, g: (h, g, 0, 0))
    ng = len(GROUPS)
    sem = pltpu.SemaphoreType.DMA((max(n_s, 1), 3))
    return _pcall(
        body, name=name, grid=(HEADS, ng),
        in_specs=[col(0), col(1), col(2), gspec, gspec, bspec, hcol, hcol,
                  pl.BlockSpec((None, SEQ, 1), lambda h, g: (h, 0, 0))] + [_ANY] * n_s,
        out_specs=[pl.BlockSpec((None, 3, SEQ, HEAD_DIM), lambda h, g: (g, 0, 0, h)), dgspec, dgspec, bspec]
        + [_ANY] * n_s,
        out_shape=[_sds((ng, 3, SEQ, D_MODEL), BF16), _sds((HEADS, ng, 1, HEAD_DIM), F32),
                   _sds((HEADS, ng, 1, HEAD_DIM), F32), _sds((ng, HEADS, ATT_BLK, 2 * ATT_BLK), F32)]
        + _rs_chips_shapes(scatter),
        scratch_shapes=[pltpu.VMEM((SEQ, HEAD_DIM), F32)] * 5 + [pltpu.VMEM((SEQ, 1), F32)]
        + ([sem, sem] if n_s else []),
        compiler_params=_cparams(("arbitrary", "arbitrary")),
    )(qkv9, qkv9, qkv9, qgain, kgain, bias, do4, o4, lse, *scatter)


def _relbias_bwd(dbias, bucket_idx, name):
    ng = len(GROUPS)

    def body(db_ref, idx_ref, o_ref):
        lane = lax.broadcasted_iota(jnp.int32, (HEADS, 128), 1)
        acc = jnp.zeros((HEADS, 128), F32)
        for g in range(ng):
            dbg = db_ref[g]
            idx = idx_ref[g]
            for b in range(NUM_BUCKETS):
                sel = jnp.where((idx == b)[None], dbg, 0.0)
                part = jnp.sum(sel, axis=1)
                val = jnp.sum(part, axis=-1, keepdims=True)
                acc = jnp.where(lane == g * NUM_BUCKETS + b, val, acc)
        o_ref[...] = acc

    return _pcall(body, name=name, out_shape=_sds((HEADS, 128), F32), compiler_params=_cparams())(dbias, bucket_idx)


def _scan16(x, reverse=False):
    row = lax.broadcasted_iota(jnp.int32, x.shape, 0)
    for sh in (1, 2, 4, 8):
        if reverse:
            x = x + jnp.where(row < HG_SUB - sh, pltpu.roll(x, HG_SUB - sh, 0), 0.0)
        else:
            x = x + jnp.where(row >= sh, pltpu.roll(x, sh, 0), 0.0)
    return x


def _hgrn_gates(qr, fr, lbv):
    q = _silu(qr)
    sig = _sigmoid(fr)
    fg = lbv + (1.0 - lbv) * sig
    lf = jnp.log(fg)
    gcum = _scan16(lf)
    glast = jnp.sum(lf, axis=0, keepdims=True)
    return q, sig, fg, 1.0 - fg, gcum, glast


def _hgrn_intra(q, k, gcum, tri):
    e = jnp.exp(jnp.where(tri, gcum[:, None, :] - gcum[None, :, :], NEG))
    a = jnp.sum(q[:, None, :] * k[None, :, :] * e, axis=-1, keepdims=True)
    return e, a


def _hgrn_fwd(proj4, lb, gain, name):
    nsub = HG_TC // HG_SUB
    wide = HG_HP * HEAD_DIM

    def body(p_ref, lb_ref, gn_ref, o_ref, y_ref, st_ref, state_s):
        @pl.when(pl.program_id(1) == 0)
        def _():
            state_s[...] = jnp.zeros_like(state_s)

        gnv = gn_ref[...]
        shp = (HG_SUB, HG_SUB, HEAD_DIM)
        tri = lax.broadcasted_iota(jnp.int32, shp, 0) >= lax.broadcasted_iota(jnp.int32, shp, 1)

        def head(qr, fr, vv, gr, lbv, st):
            q, _, _, k, gcum, glast = _hgrn_gates(qr, fr, lbv)
            _, a = _hgrn_intra(q, k, gcum, tri)
            o = jnp.sum(a * vv[None, :, :], axis=1) + _dot_nt((q * jnp.exp(gcum)).astype(BF16), st.astype(BF16))
            kg = k * jnp.exp(glast - gcum)
            st_new = st * jnp.exp(glast) + _dot_tn(vv.astype(BF16), kg.astype(BF16))
            rs = lax.rsqrt(jnp.mean(o * o, axis=-1, keepdims=True) + RMS_EPS)
            return o, (o * rs * gnv * _silu(gr)).astype(BF16), st_new

        def it(i, carry):
            rows = pl.ds(pl.multiple_of(i * HG_SUB, HG_SUB), HG_SUB)
            loaded = []
            for hh in range(HG_HP):
                lanes = pl.ds(hh * HEAD_DIM, HEAD_DIM)
                loaded.append(([p_ref[j, rows, lanes] for j in range(4)], lb_ref[:, lanes], state_s[hh]))
            results = [head(blk[0], blk[1], blk[2], blk[3], lbv, st) for blk, lbv, st in loaded]
            for hh, ((_, _, st), (o, y, st_new)) in enumerate(zip(loaded, results)):
                lanes = pl.ds(hh * HEAD_DIM, HEAD_DIM)
                st_ref[hh, i] = st.astype(BF16)
                state_s[hh] = st_new
                o_ref[rows, lanes] = o
                y_ref[hh // 2, rows, pl.ds((hh % 2) * HEAD_DIM, HEAD_DIM)] = y
            return carry

        lax.fori_loop(0, nsub, it, 0)

    return _pcall(
        body, name=name, grid=(HEADS // HG_HP, SEQ // HG_TC),
        in_specs=[pl.BlockSpec((4, HG_TC, wide), lambda h, j: (0, j, h)),
                  pl.BlockSpec((1, wide), lambda h, j: (0, h)),
                  pl.BlockSpec((1, HEAD_DIM), lambda h, j: (0, 0))],
        out_specs=[pl.BlockSpec((HG_TC, wide), lambda h, j: (j, h)),
                   pl.BlockSpec((HG_HP // 2, HG_TC, 2 * HEAD_DIM), lambda h, j: (h, j, 0)),
                   pl.BlockSpec((HG_HP, nsub, HEAD_DIM, HEAD_DIM), lambda h, j: (h, j, 0, 0))],
        out_shape=[_sds((SEQ, D_MODEL), F32), _sds((N_CHIP, SEQ, 2 * HEAD_DIM), BF16),
                   _sds((HEADS, SEQ // HG_SUB, HEAD_DIM, HEAD_DIM), BF16)],
        scratch_shapes=[pltpu.VMEM((HG_HP, HEAD_DIM, HEAD_DIM), F32)],
        compiler_params=_cparams(("parallel", "arbitrary")),
    )(proj4, lb, gain)


def _hgrn_bwd(proj4, lb, gain, o_raw, dy4, states, name):
    nsub = HG_TC // HG_SUB
    nt = SEQ // HG_TC
    wide = HG_HP * HEAD_DIM

    def body(p_ref, lb_ref, gn_ref, o_ref, dy_ref, st_ref, dp_ref, dlb_ref, dgn_ref, dst_s):
        @pl.when(pl.program_id(1) == 0)
        def _():
            dst_s[...] = jnp.zeros_like(dst_s)
            dlb_ref[...] = jnp.zeros_like(dlb_ref)
            dgn_ref[...] = jnp.zeros_like(dgn_ref)

        gnv = gn_ref[...]
        shp = (HG_SUB, HG_SUB, HEAD_DIM)
        tri = lax.broadcasted_iota(jnp.int32, shp, 0) >= lax.broadcasted_iota(jnp.int32, shp, 1)

        def head(qr, fr, vv, gr, o, dy, lbv, st0, dst):
            q, sig, fg, k, gcum, glast = _hgrn_gates(qr, fr, lbv)
            rs = lax.rsqrt(jnp.mean(o * o, axis=-1, keepdims=True) + RMS_EPS)
            oh = o * rs
            don = dy * _silu(gr)
            dgn = jnp.sum(don * oh, axis=0, keepdims=True)
            dgr = dy * oh * gnv * _dsilu(gr)
            doh = don * gnv
            do = rs * (doh - oh * jnp.mean(doh * oh, axis=-1, keepdims=True))
            dst16 = dst.astype(BF16)
            do16 = do.astype(BF16)
            eg = jnp.exp(gcum)
            eb = jnp.exp(glast - gcum)
            e, a = _hgrn_intra(q, k, gcum, tri)
            da = jnp.sum(do[:, None, :] * vv[None, :, :], axis=-1, keepdims=True)
            dae = da * e
            dq = jnp.sum(dae * k[None, :, :], axis=1) + eg * _dot(do16, st0)
            dk_state = eb * _dot(vv.astype(BF16), dst16)
            dk = jnp.sum(dae * q[:, None, :], axis=0) + dk_state
            dv = jnp.sum(a * do[:, None, :], axis=0) + _dot_nt((k * eb).astype(BF16), dst16)
            eglast = jnp.exp(glast)
            dst_new = dst * eglast + _dot_tn(do16, (q * eg).astype(BF16))
            dglast = jnp.sum(k * dk_state, axis=0, keepdims=True) \
                + eglast * jnp.sum(dst * st0.astype(F32), axis=0, keepdims=True)
            dlf = _scan16(q * dq - k * dk, reverse=True) + dglast
            dfg = dlf / fg - dk
            dlb = jnp.sum(dfg * (1.0 - sig), axis=0, keepdims=True)
            dproj = ((dq * _dsilu(qr)).astype(BF16), (dfg * (1.0 - lbv) * sig * (1.0 - sig)).astype(BF16),
                     dv.astype(BF16), dgr.astype(BF16))
            return dproj, dst_new, dlb, dgn

        def it(ii, carry):
            i = nsub - 1 - ii
            rows = pl.ds(pl.multiple_of(i * HG_SUB, HG_SUB), HG_SUB)
            results = []
            for hh in range(HG_HP):
                lanes = pl.ds(hh * HEAD_DIM, HEAD_DIM)
                blk = [p_ref[j, rows, lanes] for j in range(4)]
                dy = dy_ref[hh // 2, rows, pl.ds((hh % 2) * HEAD_DIM, HEAD_DIM)]
                results.append(head(blk[0], blk[1], blk[2], blk[3], o_ref[rows, lanes], dy,
                                    lb_ref[:, lanes], st_ref[hh, i], dst_s[hh]))
            new_carry = []
            for hh, (dproj, dst_new, dlb, dgn) in enumerate(results):
                lanes = pl.ds(hh * HEAD_DIM, HEAD_DIM)
                dst_s[hh] = dst_new
                for j in range(4):
                    dp_ref[j, rows, lanes] = dproj[j]
                new_carry.append((carry[hh][0] + dlb, carry[hh][1] + dgn))
            return tuple(new_carry)

        zero = jnp.zeros((1, HEAD_DIM), F32)
        sums = lax.fori_loop(0, nsub, it, tuple((zero, zero) for _ in range(HG_HP)))
        for hh in range(HG_HP):
            dlb_ref[hh] += sums[hh][0]
            dgn_ref[hh] += sums[hh][1]

    vspec = pl.BlockSpec((HG_HP, 1, HEAD_DIM), lambda h, j: (h, 0, 0))
    return _pcall(
        body, name=name, grid=(HEADS // HG_HP, nt),
        in_specs=[pl.BlockSpec((4, HG_TC, wide), lambda h, j: (0, nt - 1 - j, h)),
                  pl.BlockSpec((1, wide), lambda h, j: (0, h)),
                  pl.BlockSpec((1, HEAD_DIM), lambda h, j: (0, 0)),
                  pl.BlockSpec((HG_TC, wide), lambda h, j: (nt - 1 - j, h)),
                  pl.BlockSpec((HG_HP // 2, HG_TC, 2 * HEAD_DIM), lambda h, j: (h, nt - 1 - j, 0)),
                  pl.BlockSpec((HG_HP, nsub, HEAD_DIM, HEAD_DIM), lambda h, j: (h, nt - 1 - j, 0, 0))],
        out_specs=[pl.BlockSpec((4, HG_TC, wide), lambda h, j: (0, nt - 1 - j, h)), vspec, vspec],
        out_shape=[_sds((4, SEQ, D_MODEL), BF16), _sds((HEADS, 1, HEAD_DIM), F32), _sds((HEADS, 1, HEAD_DIM), F32)],
        scratch_shapes=[pltpu.VMEM((HG_HP, HEAD_DIM, HEAD_DIM), F32)],
        compiler_params=_cparams(("parallel", "arbitrary")),
    )(proj4, lb, gain, o_raw, dy4, states)


def _t5_bucket(dist):
    n = np.asarray(dist, dtype=np.int64)
    max_exact = NUM_BUCKETS // 2
    large = max_exact + (np.log(np.maximum(n, 1) / max_exact) / np.log(MAX_DISTANCE / max_exact)
                         * (NUM_BUCKETS - max_exact)).astype(np.int64)
    large = np.minimum(large, NUM_BUCKETS - 1)
    return np.where(n < max_exact, n, large).astype(np.int32)


def _bias_tables():
    qi = np.arange(ATT_BLK)[:, None]
    ki = np.arange(2 * ATT_BLK)[None, :]
    j = ATT_BLK + qi - ki
    valid = (j >= 0) & (j <= ATT_BLK)
    return np.stack([np.where(valid, _t5_bucket(np.clip(j, 0, ATT_BLK) * d), -1) for _, d in GROUPS]).astype(np.int32)


def _attn_bias(rel_bias, name):
    idx = _bias_tables()
    ng = len(GROUPS)
    buckets = [sorted(set(idx[g][idx[g] >= 0].tolist())) for g in range(ng)]

    def body(rb_ref, idx_ref, o_ref):
        h = pl.program_id(0)
        for g in range(ng):
            ig = idx_ref[g]
            acc = jnp.full(ig.shape, NEG, F32)
            for b in buckets[g]:
                acc = jnp.where(ig == b, rb_ref[b, g * HEADS + h], acc)
            o_ref[g] = acc

    return _pcall(
        body, name=name, grid=(HEADS,),
        in_specs=[pl.BlockSpec(memory_space=pltpu.SMEM),
                  pl.BlockSpec((ng, ATT_BLK, 2 * ATT_BLK), lambda h: (0, 0, 0))],
        out_specs=pl.BlockSpec((ng, None, ATT_BLK, 2 * ATT_BLK), lambda h: (0, h, 0, 0)),
        out_shape=_sds((ng, HEADS, ATT_BLK, 2 * ATT_BLK), F32),
        compiler_params=_cparams(("parallel",)),
    )(rel_bias, jnp.asarray(idx))


ADA_SHARD = 6 * D_MODEL // N_CHIP
ADA_TN = 512


def _ada_fwd(c_all, ada_w, ada_b_cols, name):
    def body(c_ref, w_ref, b_ref, o_ref):
        ca = _silu(c_ref[...]).astype(BF16)
        o_ref[...] = _dot(ca, w_ref[...].astype(BF16)) + b_ref[...]

    return _pcall(
        body, name=name, grid=(DEPTH, ADA_SHARD // ADA_TN),
        in_specs=[pl.BlockSpec((N_DEV, D_MODEL), lambda l, j: (0, 0)),
                  pl.BlockSpec((None, D_MODEL, ADA_TN), lambda l, j: (l, 0, j)),
                  pl.BlockSpec((None, 1, ADA_TN), lambda l, j: (l, 0, j))],
        out_specs=pl.BlockSpec((None, N_DEV, ADA_TN), lambda l, j: (l, 0, j)),
        out_shape=_sds((DEPTH, N_DEV, ADA_SHARD), F32),
        compiler_params=_cparams(("parallel", "parallel")),
    )(c_all, ada_w, ada_b_cols)


def _ada_bwd(c_all, dmod_cols, name):
    def body(c_ref, d_ref, o_ref):
        ca = _silu(c_ref[...]).astype(BF16)
        o_ref[...] = _dot_tn(ca, d_ref[...].astype(BF16))

    return _pcall(
        body, name=name, grid=(DEPTH, ADA_SHARD // ADA_TN),
        in_specs=[pl.BlockSpec((N_DEV, D_MODEL), lambda l, j: (0, 0)),
                  pl.BlockSpec((None, N_DEV, ADA_TN), lambda l, j: (l, 0, j))],
        out_specs=pl.BlockSpec((None, D_MODEL, ADA_TN), lambda l, j: (l, 0, j)),
        out_shape=_sds((DEPTH, D_MODEL, ADA_SHARD), F32),
        compiler_params=_cparams(("parallel", "parallel")),
    )(c_all, dmod_cols)


def _lower_bounds(logits, name):
    def body(l_ref, o_ref):
        l0 = l_ref[0:1, :]
        l1 = l_ref[1:2, :]
        mx = jnp.maximum(l0, l1)
        e0 = jnp.exp(l0 - mx)
        e1 = jnp.exp(l1 - mx)
        p0 = e0 / (e0 + e1)
        p1 = e1 / (e0 + e1)
        o_ref[0:1, :] = p0 - p0
        o_ref[1:2, :] = (p0 + p1) - p0

    return _pcall(body, name=name, out_shape=_sds((DEPTH, D_MODEL), F32), compiler_params=_cparams())(logits)


_R_DMOD = 0
_R_NMIX = 96
_R_NFFN = 112
_R_QG = 128
_R_KG = 152
_R_GN = 176
_R_LB = 184
_R_RB = 192
SMALL_ROWS = 200


def _small_totals(gathered, logits8, name):
    ng = len(GROUPS)

    def body(g_ref, l_ref, main_ref, gains_ref, dlb_ref, rb_ref):
        tot = g_ref[0]
        for dev in range(1, N_DEV):
            tot = tot + g_ref[dev]
        main_ref[...] = tot[0:_R_QG]
        gains_ref[...] = jnp.zeros_like(gains_ref)
        for g in range(ng):
            gains_ref[g:g + 1, :] = jnp.sum(tot[_R_QG + 8 * g:_R_QG + 8 * g + 8], axis=0, keepdims=True)
            gains_ref[ng + g:ng + g + 1, :] = jnp.sum(tot[_R_KG + 8 * g:_R_KG + 8 * g + 8], axis=0, keepdims=True)
        gains_ref[2 * ng:2 * ng + 1, :] = jnp.sum(tot[_R_GN:_R_GN + 8], axis=0, keepdims=True)
        rb_ref[...] = tot[_R_RB:_R_RB + 8]
        dlb1 = tot[_R_LB:_R_LB + 8]
        l0 = l_ref[0]
        l1 = l_ref[1]
        mx = jnp.maximum(l0, l1)
        e0 = jnp.exp(l0 - mx)
        e1 = jnp.exp(l1 - mx)
        p0 = e0 / (e0 + e1)
        p1 = e1 / (e0 + e1)
        dlb_ref[0] = -p0 * p1 * dlb1
        dlb_ref[1] = p1 * (1.0 - p1) * dlb1

    return _pcall(
        body, name=name,
        out_shape=[_sds((_R_QG, 128), F32), _sds((8, 128), F32), _sds((DEPTH, 8, 128), F32), _sds((8, 128), F32)],
        compiler_params=_cparams(),
    )(gathered, logits8)


def _row_tile(rows):
    return 128 if rows % 128 == 0 else rows


def _adamw(w, grads, m, v, name):
    nl, r, cdim = w.shape
    tr = _row_tile(r)

    def body(*refs):
        g_refs = refs[:nl]
        w_ref, m_ref, v_ref, go_ref, d_ref, mo_ref, vo_ref = refs[nl:]

        def step(g):
            m2 = ADAM_B1 * m_ref[...] + (1.0 - ADAM_B1) * g
            v2 = ADAM_B2 * v_ref[...] + (1.0 - ADAM_B2) * (g * g)
            m_hat = m2 / (1.0 - ADAM_B1 ** ADAM_STEP)
            v_hat = v2 / (1.0 - ADAM_B2 ** ADAM_STEP)
            go_ref[...] = g
            d_ref[...] = -ADAM_LR * (m_hat / (jnp.sqrt(v_hat) + ADAM_EPS) + ADAM_WD * w_ref[...])
            mo_ref[...] = m2
            vo_ref[...] = v2

        if nl == 1:
            step(g_refs[0][...])
        else:
            for layer in range(nl):
                @pl.when(pl.program_id(0) == layer)
                def _(layer=layer):
                    step(g_refs[layer][...])

    big = pl.BlockSpec((None, tr, cdim), lambda l, i: (l, i, 0))
    g_specs = [pl.BlockSpec((tr, cdim), lambda l, i, layer=layer: (jnp.where(l == layer, i, 0), 0))
               for layer in range(nl)]
    shp = _sds((nl, r, cdim), F32)
    return _pcall(
        body, name=name, grid=(nl, r // tr),
        in_specs=g_specs + [big, big, big],
        out_specs=[big, big, big, big],
        out_shape=[shp, shp, shp, shp],
        compiler_params=_cparams(("parallel", "parallel")),
    )(*grads, w, m, v)


def _cast_bf16(place, w, name):
    nl, r, cdim = w.shape
    tr = _row_tile(r)

    def body(place_ref, w_ref, o_ref):
        o_ref[...] = w_ref[...].astype(BF16)

    return _pcall(
        body, name=name,
        grid_spec=pltpu.PrefetchScalarGridSpec(
            num_scalar_prefetch=1, grid=(nl, r // tr),
            in_specs=[pl.BlockSpec((None, tr, cdim), lambda l, i, place_ref: (l, i, 0))],
            out_specs=pl.BlockSpec((None, None, tr, cdim), lambda l, i, place_ref: (place_ref[1], l, i, 0))),
        out_shape=_sds((N_CHIP, nl, r, cdim), BF16),
        compiler_params=_cparams(("parallel", "parallel")),
    )(place, w)


def _rs_add_cast(place, grad, recv, name):
    _, k, n = grad.shape
    kh = k // 2
    tr = _row_tile(kh)
    nb = kh // tr

    def body(place_ref, g_ref, r_ref, o_ref):
        o_ref[...] = (g_ref[...] + r_ref[...]).astype(BF16)

    half = pl.BlockSpec((None, tr, n), lambda s, i, place_ref: (s, i, 0))
    return _pcall(
        body, name=name,
        grid_spec=pltpu.PrefetchScalarGridSpec(
            num_scalar_prefetch=1, grid=(N_CHIP, nb),
            in_specs=[pl.BlockSpec((None, tr, n), lambda s, i, place_ref: (s, place_ref[0] * nb + i, 0)), half],
            out_specs=half),
        out_shape=_sds((N_CHIP, kh, n), BF16),
        compiler_params=_cparams(("parallel", "parallel")),
    )(place, grad, recv)


def _rs_sum4(place, parts, got, name):
    _, kh, n = parts.shape
    tr = _row_tile(kh)
    nb = kh // tr

    def body(place_ref, p_ref, g_ref, o_ref):
        acc = p_ref[...].astype(F32)
        for j in range(N_CHIP - 1):
            acc = acc + g_ref[j].astype(F32)
        o_ref[...] = acc

    return _pcall(
        body, name=name,
        grid_spec=pltpu.PrefetchScalarGridSpec(
            num_scalar_prefetch=1, grid=(nb,),
            in_specs=[pl.BlockSpec((None, tr, n), lambda i, place_ref: (place_ref[1], i, 0)),
                      pl.BlockSpec((N_CHIP - 1, tr, n), lambda i, place_ref: (0, i, 0))],
            out_specs=pl.BlockSpec((tr, n), lambda i, place_ref: (place_ref[0] * nb + i, 0))),
        out_shape=_sds((2 * kh, n), F32),
        compiler_params=_cparams(("parallel",)),
    )(place, parts, got)


_ANY = pl.BlockSpec(memory_space=pl.ANY)


def _position():
    return lax.axis_index("x"), lax.axis_index("y"), lax.axis_index("c")


def _other_chips(x, y):
    return [(1 - x, y), (x, 1 - y), (1 - x, 1 - y)]


def _remote(src, dst, send_sem, recv_sem, to):
    return pltpu.make_async_remote_copy(src_ref=src, dst_ref=dst, send_sem=send_sem, recv_sem=recv_sem,
                                        device_id=to, device_id_type=MESH)


def _small_allgather(v, name):
    r = v.shape[0]

    def body(x_ref, out_ref, send_sems, recv_sems, local_sem):
        x, y, c = _position()
        me, sibling = (x, y, c), (x, y, 1 - c)
        chips = _other_chips(x, y)

        def slab(px, py, pc):
            return out_ref.at[4 * px + 2 * py + pc]

        def copy(k, block, to, src=None):
            return _remote(slab(*block) if src is None else src, slab(*block), send_sems.at[k], recv_sems.at[k], to)

        mine = pltpu.make_async_copy(x_ref, slab(*me), local_sem)
        mine.start()
        first = [copy(0, me, sibling, src=x_ref)]
        first += [copy(1 + j, me, (*chip, c), src=x_ref) for j, chip in enumerate(chips)]
        for cp in first:
            cp.start()
        passed = [copy(4 + j, (*chip, c), sibling) for j, chip in enumerate(chips)]
        for j, chip in enumerate(chips):
            copy(1 + j, (*chip, c), me).wait_recv()
            passed[j].start()
        copy(0, sibling, me).wait_recv()
        for j, chip in enumerate(chips):
            copy(4 + j, (*chip, 1 - c), me).wait_recv()
        for cp in first + passed:
            cp.wait_send()
        mine.wait()

    return _pcall(
        body, name=name,
        out_shape=_sds((N_DEV, r, 128), F32),
        in_specs=[pl.BlockSpec(memory_space=pltpu.VMEM)],
        out_specs=pl.BlockSpec(memory_space=pltpu.VMEM),
        scratch_shapes=[pltpu.SemaphoreType.DMA((7,)), pltpu.SemaphoreType.DMA((7,)), pltpu.SemaphoreType.DMA],
        compiler_params=_cparams(),
    )(v)


def _half_rows(core, kh):
    return pl.ds(pl.multiple_of(core * kh, 8), kh)


def _slab_half(ref, chip, core):
    return ref.at[chip, :, _half_rows(core, ref.shape[2] // 2), :]


def _gather_ici(out, send_sems, recv_sems):
    def copies():
        x, y, c = _position()
        for a in range(len(out)):
            for j, (px, py) in enumerate(_other_chips(x, y)):
                mine = _slab_half(out[a], 2 * x + y, c)
                landed = _slab_half(out[a], 2 * px + py, c)
                yield (_remote(mine, mine, send_sems.at[a, j], recv_sems.at[a, j], (px, py, c)),
                       _remote(landed, landed, send_sems.at[a, j], recv_sems.at[a, j], (px, py, c)))

    def start():
        for send, _ in copies():
            send.start()

    def wait():
        for send, recv in copies():
            recv.wait_recv()
            send.wait_send()

    return start, wait


def _gather_d2d(out, send_sems, recv_sems):
    def copies():
        x, y, c = _position()
        for a in range(len(out)):
            for j, (px, py) in enumerate(_other_chips(x, y)):
                landed = _slab_half(out[a], 2 * px + py, c)
                other = _slab_half(out[a], 2 * px + py, 1 - c)
                yield (_remote(landed, landed, send_sems.at[a, j], recv_sems.at[a, j], (x, y, 1 - c)),
                       _remote(other, other, send_sems.at[a, j], recv_sems.at[a, j], (x, y, 1 - c)))

    def start():
        for send, _ in copies():
            send.start()

    def wait():
        for send, recv in copies():
            recv.wait_recv()
            send.wait_send()

    return start, wait


def _gather_weights(slabs, name, ici=True):
    n = len(slabs)

    def body(*refs):
        out = refs[n:2 * n]
        sems = refs[2 * n:]
        if ici:
            start, wait = _gather_ici(out, sems[2], sems[3])
            start()
            wait()
        start, wait = _gather_d2d(out, sems[0], sems[1])
        start()
        wait()

    sem = pltpu.SemaphoreType.DMA((n, 3))
    return _pcall(
        body, name=name,
        out_shape=[_sds(s.shape, BF16) for s in slabs],
        in_specs=[_ANY] * n, out_specs=[_ANY] * n,
        input_output_aliases={a: a for a in range(n)},
        scratch_shapes=[sem, sem] + ([sem, sem] if ici else []),
        compiler_params=_cparams(),
    )(*slabs)


def _rs_halves(grads, out, send_sems, recv_sems):
    def copies():
        x, y, c = _position()
        for a in range(len(grads)):
            kh = grads[a].shape[1] // 2
            yield _remote(grads[a].at[:, _half_rows(1 - c, kh), :], out[a], send_sems.at[a], recv_sems.at[a],
                          (x, y, 1 - c))

    def start():
        for cp in copies():
            cp.start()

    def wait():
        for cp in copies():
            cp.wait()

    return start, wait


def _rs_halves_shapes(grads):
    return [_sds((N_CHIP, g.shape[1] // 2, g.shape[2]), F32) for g in grads]


def _rs_exchange_halves(grads, name):
    n = len(grads)

    def body(*refs):
        start, wait = _rs_halves(refs[:n], refs[n:2 * n], *refs[2 * n:])
        start()
        wait()

    return _pcall(
        body, name=name,
        out_shape=_rs_halves_shapes(grads),
        in_specs=[_ANY] * n, out_specs=[_ANY] * n,
        scratch_shapes=[pltpu.SemaphoreType.DMA((n,)), pltpu.SemaphoreType.DMA((n,))],
        compiler_params=_cparams(),
    )(*grads)


def _rs_chips(parts, out, send_sems, recv_sems):
    def copies():
        x, y, c = _position()
        for a in range(len(parts)):
            for j, (px, py) in enumerate(_other_chips(x, y)):
                got = out[a].at[j]
                yield (_remote(parts[a].at[2 * px + py], got, send_sems.at[a, j], recv_sems.at[a, j], (px, py, c)),
                       _remote(got, got, send_sems.at[a, j], recv_sems.at[a, j], (px, py, c)))

    def start():
        for send, _ in copies():
            send.start()

    def wait():
        for send, recv in copies():
            recv.wait_recv()
            send.wait_send()

    return start, wait


def _rs_chips_shapes(parts):
    return [_sds((N_CHIP - 1,) + p.shape[1:], BF16) for p in parts]


def _rs_join(out, send_sems, recv_sems):
    def copies():
        x, y, c = _position()
        for a in range(len(out)):
            kh = out[a].shape[0] // 2
            mine = out[a].at[_half_rows(c, kh), :]
            theirs = out[a].at[_half_rows(1 - c, kh), :]
            yield (_remote(mine, mine, send_sems.at[a], recv_sems.at[a], (x, y, 1 - c)),
                   _remote(theirs, theirs, send_sems.at[a], recv_sems.at[a], (x, y, 1 - c)))

    def start():
        for send, _ in copies():
            send.start()

    def wait():
        for send, recv in copies():
            recv.wait_recv()
            send.wait_send()

    return start, wait


def _rs_join_halves(fulls, name):
    n = len(fulls)

    def body(*refs):
        start, wait = _rs_join(refs[n:2 * n], *refs[2 * n:])
        start()
        wait()

    return _pcall(
        body, name=name,
        out_shape=[_sds(f.shape, F32) for f in fulls],
        in_specs=[_ANY] * n, out_specs=[_ANY] * n,
        input_output_aliases={a: a for a in range(n)},
        scratch_shapes=[pltpu.SemaphoreType.DMA((n,)), pltpu.SemaphoreType.DMA((n,))],
        compiler_params=_cparams(),
    )(*fulls)


_SMALL_ORDER = ("rel_bias", "ada_b", "norm_mix", "norm_ffn", "attn_q_gain", "attn_k_gain", "hgrn_gnorm",
                "hgrn_lower_bounds")
_WEIGHT_ORDER = ("rel_bias", "ada_w", "ada_b", "norm_mix", "norm_ffn", "attn_w_qkv", "attn_w_out", "attn_q_gain",
                 "attn_k_gain", "hgrn_w_in", "hgrn_w_out", "hgrn_gnorm", "hgrn_lower_bounds", "ffn_w1", "ffn_w3",
                 "ffn_w2")


def _qkv_group_map(t):
    return t // 4, t % 4


def _qkv_chip_map(t):
    return t // 9, t % 9


def _hin_map(t):
    return t // 2, t % 2


def _block_map(t):
    return t, 0


def _pack_rows(parts):
    return jnp.concatenate([p.reshape(-1, 128) for p in parts], axis=0)


def kernel(x, c, rel_bias, ada_w, ada_b, norm_mix, norm_ffn, attn_w_qkv, attn_w_out, attn_q_gain, attn_k_gain, hgrn_w_in, hgrn_w_out, hgrn_gnorm, hgrn_lower_bounds, ffn_w1, ffn_w3, ffn_w2, loss_target, m_rel_bias, m_ada_w, m_ada_b, m_norm_mix, m_norm_ffn, m_attn_w_qkv, m_attn_w_out, m_attn_q_gain, m_attn_k_gain, m_hgrn_w_in, m_hgrn_w_out, m_hgrn_gnorm, m_hgrn_lower_bounds, m_ffn_w1, m_ffn_w3, m_ffn_w2, v_rel_bias, v_ada_w, v_ada_b, v_norm_mix, v_norm_ffn, v_attn_w_qkv, v_attn_w_out, v_attn_q_gain, v_attn_k_gain, v_hgrn_w_in, v_hgrn_w_out, v_hgrn_gnorm, v_hgrn_lower_bounds, v_ffn_w1, v_ffn_w3, v_ffn_w2):
    weights = dict(rel_bias=rel_bias, ada_w=ada_w, ada_b=ada_b, norm_mix=norm_mix, norm_ffn=norm_ffn,
                   attn_w_qkv=attn_w_qkv, attn_w_out=attn_w_out, attn_q_gain=attn_q_gain, attn_k_gain=attn_k_gain,
                   hgrn_w_in=hgrn_w_in, hgrn_w_out=hgrn_w_out, hgrn_gnorm=hgrn_gnorm,
                   hgrn_lower_bounds=hgrn_lower_bounds, ffn_w1=ffn_w1, ffn_w3=ffn_w3, ffn_w2=ffn_w2)
    mom1 = dict(rel_bias=m_rel_bias, ada_w=m_ada_w, ada_b=m_ada_b, norm_mix=m_norm_mix, norm_ffn=m_norm_ffn,
                attn_w_qkv=m_attn_w_qkv, attn_w_out=m_attn_w_out, attn_q_gain=m_attn_q_gain,
                attn_k_gain=m_attn_k_gain, hgrn_w_in=m_hgrn_w_in, hgrn_w_out=m_hgrn_w_out, hgrn_gnorm=m_hgrn_gnorm,
                hgrn_lower_bounds=m_hgrn_lower_bounds, ffn_w1=m_ffn_w1, ffn_w3=m_ffn_w3, ffn_w2=m_ffn_w2)
    mom2 = dict(rel_bias=v_rel_bias, ada_w=v_ada_w, ada_b=v_ada_b, norm_mix=v_norm_mix, norm_ffn=v_norm_ffn,
                attn_w_qkv=v_attn_w_qkv, attn_w_out=v_attn_w_out, attn_q_gain=v_attn_q_gain,
                attn_k_gain=v_attn_k_gain, hgrn_w_in=v_hgrn_w_in, hgrn_w_out=v_hgrn_w_out, hgrn_gnorm=v_hgrn_gnorm,
                hgrn_lower_bounds=v_hgrn_lower_bounds, ffn_w1=v_ffn_w1, ffn_w3=v_ffn_w3, ffn_w2=v_ffn_w2)

    transposed = ("ffn_w1", "ffn_w3")
    for group in (weights, mom1, mom2):
        for k in transposed:
            group[k] = jnp.transpose(group[k], (0, 2, 1))

    xi, yi, ci = _position()
    chip = 2 * xi + yi
    dev = 4 * xi + 2 * yi + ci
    place = jnp.stack([ci, chip]).astype(jnp.int32)
    d = D_MODEL

    big_names = ("attn_w_qkv", "attn_w_out", "hgrn_w_in", "hgrn_w_out", "ffn_w1", "ffn_w3", "ffn_w2")
    early_names, late_names = big_names[:1], big_names[1:]
    slabs16 = {k: _cast_bf16(place, weights[k], "cast_" + k) for k in big_names}
    wg = dict(zip(early_names, _gather_weights([slabs16[k] for k in early_names], "gather_early")))

    c_all = _small_allgather(c.reshape(8, 128), "gather_c").reshape(N_DEV, d)
    ada_b_cols = lax.dynamic_slice(ada_b, (0, chip * ADA_SHARD), (DEPTH, ADA_SHARD)).reshape(DEPTH, 1, ADA_SHARD)
    mod_shard = _ada_fwd(c_all, ada_w, ada_b_cols, "ada_fwd")
    mod_all = _small_allgather(mod_shard.reshape(-1, 128), "gather_mod").reshape(N_DEV, DEPTH, N_DEV, ADA_SHARD)
    mod_mine = lax.dynamic_index_in_dim(mod_all[0::2], dev, axis=2, keepdims=False)
    mod = jnp.transpose(mod_mine, (1, 0, 2)).reshape(DEPTH, 6 * d)

    def mods(layer):
        return [mod[layer:layer + 1, j * d:(j + 1) * d] for j in range(6)]

    x0 = x.reshape(SEQ, d)
    target = loss_target.reshape(SEQ, d)
    qg = attn_q_gain.reshape(len(GROUPS), 1, HEAD_DIM)
    kg = attn_k_gain.reshape(len(GROUPS), 1, HEAD_DIM)
    bias = _attn_bias(rel_bias, "attn_bias")
    lb1 = _lower_bounds(hgrn_lower_bounds, "lower_bounds")[1:2]

    def ffn_fwd(layer, x_in, sc2, sh2, g2):
        hf = _norm_mod(x_in, norm_ffn[layer:layer + 1], sc2, sh2, f"l{layer}_norm_ffn")
        a1, a3, u = _ffn_up(hf, wg["ffn_w1"], wg["ffn_w3"], layer, f"l{layer}_ffn_up")
        z, x_out = _mm_rows(u, wg["ffn_w2"], layer, x_in, g2, f"l{layer}_ffn_down")
        return x_out, (hf, a1, a3, u, z)

    def ffn_bwd(layer, dz, dg2, dx_out, x_in, sc2, sh2, saved, mixer_branch, halves=()):
        hf, a1, a3, u, _ = saved
        da1, da3, *recv = _ffn_down_bwd(dz, wg["ffn_w2"], layer, a1, a3, f"l{layer}_ffn_down_bwd", halves=halves)
        dw2 = _mm_rows_bwd_w(u, dz, f"l{layer}_dw2")
        dh = _ffn_up_bwd(da1, da3, wg["ffn_w1"], wg["ffn_w3"], layer, f"l{layer}_ffn_up_bwd")
        dw1 = _mm_rows_bwd_w(da1, hf, f"l{layer}_dw1")
        dw3 = _mm_rows_bwd_w(da3, hf, f"l{layer}_dw3")
        dx_in, dsc2, dsh2, dnf, dz_mix, dg_mix = _norm_mod_bwd(x_in, norm_ffn[layer:layer + 1], sc2, sh2, dh, dx_out,
                                                               f"l{layer}_norm_ffn_bwd", branch=mixer_branch)
        return dx_in, (dw1, dw3, dw2), (dsh2, dsc2, dg2), dnf, recv, dz_mix, dg_mix

    def rs_add(tags, grads_in, recv):
        return [_rs_add_cast(place, g, r, f"rs_add_{k}_{layer}") for (k, layer), g, r in zip(tags, grads_in, recv)]

    sh1_0, sc1_0, g1_0, sh2_0, sc2_0, g2_0 = mods(0)
    h0 = _norm_mod(x0, norm_mix[0:1], sc1_0, sh1_0, "l0_norm_mix")
    w_qkv9 = _retile_cols(wg["attn_w_qkv"].reshape(N_CHIP, d, 2304), n_out=9, width_out=d, tn=256,
                          src_map=_qkv_chip_map, dst_map=_qkv_group_map, n_tiles=36,
                          name="regroup_w_qkv").reshape(9, 1, d, d)
    qkv9 = _mm_cols(h0, w_qkv9, 0, n_blocks=9, width=d, tn=d, act_map=_block_map, w_map=_block_map,
                    out_dtype=F32, name="l0_qkv")
    o4, lse, *late = _attn_fwd(qkv9, qg, kg, bias, "l0_attn", gather=[slabs16[k] for k in late_names])
    wg.update(zip(late_names, _gather_weights(late, "gather_late_siblings", ici=False)))
    y0, x1 = _mm_rows(o4, wg["attn_w_out"], 0, x0, g1_0, "l0_attn_out")
    x2, ffn0 = ffn_fwd(0, x1, sc2_0, sh2_0, g2_0)

    sh1_1, sc1_1, g1_1, sh2_1, sc2_1, g2_1 = mods(1)
    h1 = _norm_mod(x2, norm_mix[1:2], sc1_1, sh1_1, "l1_norm_mix")
    proj4 = _mm_cols(h1, wg["hgrn_w_in"], 0, n_blocks=4, width=d, tn=512, act_map=_hin_map, w_map=_hin_map,
                     out_dtype=F32, name="l1_hgrn_in")
    o_raw, yg4, states = _hgrn_fwd(proj4, lb1, hgrn_gnorm, "l1_hgrn")
    y1, x3 = _mm_rows(yg4, wg["hgrn_w_out"], 0, x2, g1_1, "l1_hgrn_out")
    x4, ffn1 = ffn_fwd(1, x3, sc2_1, sh2_1, g2_1)

    dx4, loss_part, dz_ffn1, dg2_1 = _loss_head(x4, target, ffn1[4], g2_1, "loss_head")
    loss = lax.psum(loss_part[0, 0], ("x", "y", "c"))

    dx3, (dw1_1, dw3_1, dw2_1), dmod2_1, dnf_1, _, dzm1, dg1_1 = ffn_bwd(
        1, dz_ffn1, dg2_1, dx4, x3, sc2_1, sh2_1, ffn1, (y1, g1_1))
    dyg4 = _mm_rows_bwd_a(dzm1, wg["hgrn_w_out"], 0, "l1_hgrn_out_bwd")
    dw_hout = _mm_rows_bwd_w(yg4, dzm1, "l1_dw_hgrn_out")
    dproj4, dlb_h, dgn_h = _hgrn_bwd(proj4, lb1, hgrn_gnorm, o_raw, dyg4, states, "l1_hgrn_bwd")
    dh1 = _mm_cols_bwd_a(dproj4, wg["hgrn_w_in"], 0, group=N_CHIP, name="l1_hgrn_in_bwd", tm=512)
    dw_hin = _mm_cols_bwd_w(h1, dproj4, ns=d, tn=d, act_map=_block_map, w_map=_block_map, n_tiles=N_CHIP,
                            name="l1_dw_hgrn_in", tm=2048)
    dx2, dsc1_1, dsh1_1, dnm_1, dz_ffn0, dg2_0 = _norm_mod_bwd(x2, norm_mix[1:2], sc1_1, sh1_1, dh1, dx3,
                                                               "l1_norm_mix_bwd", branch=(ffn0[4], g2_0))

    tags_1 = [("hgrn_w_in", 0), ("hgrn_w_out", 0), ("ffn_w1", 1), ("ffn_w3", 1), ("ffn_w2", 1)]
    grads_1 = [dw_hin, dw_hout, dw1_1, dw3_1, dw2_1]
    dx1, (dw1_0, dw3_0, dw2_0), dmod2_0, dnf_0, recv_1, dzm0, dg1_0 = ffn_bwd(
        0, dz_ffn0, dg2_0, dx2, x1, sc2_0, sh2_0, ffn0, (y0, g1_0), halves=grads_1)
    tags_0 = [("ffn_w1", 0), ("ffn_w3", 0), ("ffn_w2", 0)]
    grads_0 = [dw1_0, dw3_0, dw2_0]
    do4, *recv_0 = _mm_rows_bwd_a(dzm0, wg["attn_w_out"], 0, "l0_attn_out_bwd", halves=grads_0)
    dw_aout = _mm_rows_bwd_w(o4, dzm0, "l0_dw_attn_out")
    tags_a = tags_1 + tags_0
    parts_a = rs_add(tags_1, grads_1, recv_1) + rs_add(tags_0, grads_0, recv_0)
    dqkv, dqg_h, dkg_h, dbias, *got_a = _attn_bwd(qkv9, qg, kg, bias, do4, o4, lse, "l0_attn_bwd", scatter=parts_a)
    dqkv9 = dqkv.reshape(9, SEQ, d)
    dw_qkv9 = _mm_cols_bwd_w(h0, dqkv9, ns=d, tn=d, act_map=_block_map, w_map=_block_map, n_tiles=9,
                             name="l0_dw_qkv", tm=2048, n_out=9)
    dw_qkv = _retile_cols(dw_qkv9, n_out=N_CHIP, width_out=2304, tn=256, src_map=_qkv_group_map,
                          dst_map=_qkv_chip_map, n_tiles=36, name="regroup_dw_qkv")
    tags_b = [("attn_w_qkv", 0), ("attn_w_out", 0)]
    grads_b = [dw_qkv, dw_aout]
    parts_b = rs_add(tags_b, grads_b, _rs_exchange_halves(grads_b, "rs_exchange_halves_b"))
    dh0, *got_b = _mm_cols_bwd_a(dqkv9, w_qkv9, 0, group=3, name="l0_qkv_bwd", scatter=parts_b)
    dx0, dsc1_0, dsh1_0, dnm_0 = _norm_mod_bwd(x0, norm_mix[0:1], sc1_0, sh1_0, dh0, dx1, "l0_norm_mix_bwd")
    drb8 = _relbias_bwd(dbias, jnp.asarray(_bias_tables()), "rel_bias_bwd")

    small = _pack_rows([
        dsh1_0, dsc1_0, dg1_0, *dmod2_0, dsh1_1, dsc1_1, dg1_1, *dmod2_1,
        dnm_0, dnm_1, dnf_0, dnf_1,
        jnp.transpose(dqg_h, (1, 0, 2, 3)), jnp.transpose(dkg_h, (1, 0, 2, 3)), dgn_h, dlb_h, drb8])
    small_all = _small_allgather(small, "gather_small")
    main, gains, dlbnd, rbt = _small_totals(small_all, hgrn_lower_bounds.reshape(DEPTH, 8, 128), "small_totals")
    ng = len(GROUPS)
    grads = {
        "ada_b": main[_R_DMOD:_R_NMIX].reshape(DEPTH, 6 * d),
        "norm_mix": main[_R_NMIX:_R_NFFN].reshape(DEPTH, d),
        "norm_ffn": main[_R_NFFN:_R_QG].reshape(DEPTH, d),
        "attn_q_gain": gains[0:ng].reshape(1, ng, HEAD_DIM),
        "attn_k_gain": gains[ng:2 * ng].reshape(1, ng, HEAD_DIM),
        "hgrn_gnorm": gains[2 * ng:2 * ng + 1],
        "hgrn_lower_bounds": dlbnd.reshape(DEPTH, d),
        "rel_bias": jnp.transpose(rbt[:, :ng * NUM_BUCKETS].reshape(HEADS, ng, NUM_BUCKETS), (2, 1, 0))
                       .reshape(NUM_BUCKETS, ng * HEADS),
    }
    dmod_all = small_all[:, _R_DMOD:_R_NMIX].reshape(N_DEV, DEPTH, 6 * d)
    dmod_cols = jnp.transpose(lax.dynamic_slice(dmod_all, (0, 0, chip * ADA_SHARD), (N_DEV, DEPTH, ADA_SHARD)),
                              (1, 0, 2))
    grad_ada_w = _ada_bwd(c_all, dmod_cols, "ada_bwd")

    tags = tags_a + tags_b
    halves = [_rs_sum4(place, p, r, f"rs_sum_{k}_{layer}")
              for (k, layer), p, r in zip(tags, parts_a + parts_b, list(got_a) + list(got_b))]
    full = dict(zip(tags, _rs_join_halves(halves, "rs_join_halves")))

    out_g, out_d, out_m, out_v = {}, {}, {}, {}
    for k in big_names:
        gs = [full[(k, layer)] for layer in range(weights[k].shape[0])]
        out_g[k], out_d[k], out_m[k], out_v[k] = _adamw(weights[k], gs, mom1[k], mom2[k], "adamw_" + k)
    shp = (1, DEPTH * d, ADA_SHARD)
    res = _adamw(ada_w.reshape(shp), [grad_ada_w.reshape(shp[1:])], m_ada_w.reshape(shp), v_ada_w.reshape(shp),
                 "adamw_ada_w")
    out_g["ada_w"], out_d["ada_w"], out_m["ada_w"], out_v["ada_w"] = [r.reshape(ada_w.shape) for r in res]
    packed = [_pack_rows([src[k] for k in _SMALL_ORDER])[None] for src in (weights, grads, mom1, mom2)]
    res = _adamw(packed[0], [packed[1][0]], packed[2], packed[3], "adamw_small")
    offset = 0
    for k in _SMALL_ORDER:
        size = weights[k].size
        for dst, r in zip((out_g, out_d, out_m, out_v), res):
            dst[k] = r.reshape(-1)[offset:offset + size].reshape(weights[k].shape)
        offset += size
    for dst in (out_g, out_d, out_m, out_v):
        for k in transposed:
            dst[k] = jnp.transpose(dst[k], (0, 2, 1))

    return (loss, dx0.reshape(x.shape), *[out_g[k] for k in _WEIGHT_ORDER], *[out_d[k] for k in _WEIGHT_ORDER],
            *[out_m[k] for k in _WEIGHT_ORDER], *[out_v[k] for k in _WEIGHT_ORDER])
```

```python
import functools

import numpy as np
import jax
import jax.numpy as jnp
from jax import lax
from jax.experimental import pallas as pl
from jax.experimental.pallas import tpu as pltpu

F32 = jnp.float32
BF16 = jnp.bfloat16

D_MODEL = 1024
SEQ = 4096
N_DEV = 8
N_CHIP = 4
DEPTH = 2
HEADS = 8
HEAD_DIM = 128
GROUPS = ((128, 1), (512, 4), (2048, 16))
ATT_BLK = 128
ATT_WAYS = 4
ATT_STEPS = SEQ // ATT_BLK // ATT_WAYS
NUM_BUCKETS = 32
MAX_DISTANCE = 2048
FFN_HIDDEN = 2816
FFN_SHARD = FFN_HIDDEN // N_CHIP
HG_SUB = 16
HG_TC = 512
HG_HP = 4
RMS_EPS = 1e-6
NEG = -1e30
ATT_SCALE = HEAD_DIM ** -0.5
ADAM_LR, ADAM_B1, ADAM_B2, ADAM_EPS, ADAM_WD, ADAM_STEP = 0.001, 0.9, 0.999, 1e-08, 0.01, 10
VMEM_LIMIT = 56 * 1024 * 1024
MESH = pl.DeviceIdType.MESH


def _pcall(body, **kw):
    return pl.pallas_call(body, **kw)


def _cparams(sem=None):
    if sem is None:
        return pltpu.CompilerParams(vmem_limit_bytes=VMEM_LIMIT)
    return pltpu.CompilerParams(dimension_semantics=sem, vmem_limit_bytes=VMEM_LIMIT)


def _sds(shape, dtype):
    return jax.ShapeDtypeStruct(shape, dtype)


def _dot(a, b):
    return jnp.dot(a, b, preferred_element_type=F32)


def _dot_nt(a, b):
    return lax.dot_general(a, b, (((1,), (1,)), ((), ())), preferred_element_type=F32)


def _dot_tn(a, b):
    return lax.dot_general(a, b, (((0,), (0,)), ((), ())), preferred_element_type=F32)


def _sigmoid(x):
    return 1.0 / (1.0 + jnp.exp(-x))


def _silu(x):
    return x * _sigmoid(x)


def _dsilu(x):
    s = _sigmoid(x)
    return s * (1.0 + x * (1.0 - s))


def _norm_mod(x, gain, sc, sh, name):
    tm = 512

    def body(x_ref, g_ref, sc_ref, sh_ref, h_ref):
        xv = x_ref[...]
        rs = lax.rsqrt(jnp.mean(xv * xv, axis=-1, keepdims=True) + RMS_EPS)
        h_ref[...] = ((xv * rs * g_ref[...]) * (1.0 + sc_ref[...]) + sh_ref[...]).astype(BF16)

    vec = pl.BlockSpec((1, D_MODEL), lambda i: (0, 0))
    return _pcall(
        body, name=name, grid=(SEQ // tm,),
        in_specs=[pl.BlockSpec((tm, D_MODEL), lambda i: (i, 0)), vec, vec, vec],
        out_specs=pl.BlockSpec((tm, D_MODEL), lambda i: (i, 0)),
        out_shape=_sds((SEQ, D_MODEL), BF16),
        compiler_params=_cparams(("parallel",)),
    )(x, gain, sc, sh)


def _gated_branch_bwd(dx, z_ref, gate_ref, dz_ref, dgate_ref):
    dz_ref[...] = (dx * gate_ref[...]).astype(BF16)
    dgate_ref[...] += jnp.sum(dx * z_ref[...], axis=0, keepdims=True)


def _norm_mod_bwd(x, gain, sc, sh, dh, dres, name, branch=None):
    tm = 512
    n_b = 2 if branch else 0

    def body(*refs):
        x_ref, g_ref, sc_ref, sh_ref, dh_ref, dres_ref = refs[:6]
        dx_ref, dsc_ref, dsh_ref, dg_ref = refs[6 + n_b:10 + n_b]

        @pl.when(pl.program_id(0) == 0)
        def _():
            dsc_ref[...] = jnp.zeros_like(dsc_ref)
            dsh_ref[...] = jnp.zeros_like(dsh_ref)
            dg_ref[...] = jnp.zeros_like(dg_ref)
            if branch:
                refs[11 + n_b][...] = jnp.zeros_like(refs[11 + n_b])

        xv = x_ref[...]
        dhv = dh_ref[...]
        rs = lax.rsqrt(jnp.mean(xv * xv, axis=-1, keepdims=True) + RMS_EPS)
        xh = xv * rs
        dsc_ref[...] += jnp.sum(dhv * (xh * g_ref[...]), axis=0, keepdims=True)
        dsh_ref[...] += jnp.sum(dhv, axis=0, keepdims=True)
        dhn = dhv * (1.0 + sc_ref[...])
        dg_ref[...] += jnp.sum(dhn * xh, axis=0, keepdims=True)
        dxh = dhn * g_ref[...]
        dx = dres_ref[...] + rs * (dxh - xh * jnp.mean(dxh * xh, axis=-1, keepdims=True))
        dx_ref[...] = dx
        if branch:
            _gated_branch_bwd(dx, refs[6], refs[7], refs[10 + n_b], refs[11 + n_b])

    vec = pl.BlockSpec((1, D_MODEL), lambda i: (0, 0))
    big = pl.BlockSpec((tm, D_MODEL), lambda i: (i, 0))
    return _pcall(
        body, name=name, grid=(SEQ // tm,),
        in_specs=[big, vec, vec, vec, big, big] + ([big, vec] if branch else []),
        out_specs=[big, vec, vec, vec] + ([big, vec] if branch else []),
        out_shape=[_sds((SEQ, D_MODEL), F32)] + [_sds((1, D_MODEL), F32)] * 3
        + ([_sds((SEQ, D_MODEL), BF16), _sds((1, D_MODEL), F32)] if branch else []),
        compiler_params=_cparams(("arbitrary",)),
    )(x, gain, sc, sh, dh, dres, *(branch or ()))


def _mm_cols(a, wg, layer, *, n_blocks, width, tn, act_map, w_map, out_dtype, name, tm=1024):
    k = a.shape[1]
    n_tiles = n_blocks * width // tn

    def body(a_ref, w_ref, o_ref):
        o_ref[...] = _dot(a_ref[...], w_ref[...]).astype(o_ref.dtype)

    return _pcall(
        body, name=name, grid=(SEQ // tm, n_tiles),
        in_specs=[pl.BlockSpec((tm, k), lambda i, t: (i, 0)),
                  pl.BlockSpec((None, None, k, tn), lambda i, t: (w_map(t)[0], layer, 0, w_map(t)[1]))],
        out_specs=pl.BlockSpec((None, tm, tn), lambda i, t: (act_map(t)[0], i, act_map(t)[1])),
        out_shape=_sds((n_blocks, SEQ, width), out_dtype),
        compiler_params=_cparams(("parallel", "arbitrary")),
    )(a, wg)


def _mm_cols_bwd_a(dout, wg, layer, *, group, name, tm=1024, scatter=()):
    n_blocks, _, width = dout.shape
    k = wg.shape[2]
    n_s = len(scatter)
    n_rows = SEQ // tm
    n_steps = n_blocks // group

    def body(*refs):
        d_ref, w_ref = refs[:2]
        o_ref = refs[2 + n_s]
        if n_s:
            comm_start, comm_wait = _rs_chips(refs[2:2 + n_s], refs[3 + n_s:3 + 2 * n_s], *refs[3 + 2 * n_s:])
            pl.when((pl.program_id(0) == 0) & (pl.program_id(1) == 0))(comm_start)
        acc = _dot_nt(d_ref[0], w_ref[0])
        for b in range(1, group):
            acc += _dot_nt(d_ref[b], w_ref[b])
        if n_steps == 1:
            o_ref[...] = acc
        else:
            @pl.when(pl.program_id(1) == 0)
            def _():
                o_ref[...] = acc

            @pl.when(pl.program_id(1) > 0)
            def _():
                o_ref[...] += acc
        if n_s:
            pl.when((pl.program_id(0) == n_rows - 1) & (pl.program_id(1) == n_steps - 1))(comm_wait)

    sem = pltpu.SemaphoreType.DMA((max(n_s, 1), 3))
    res = _pcall(
        body, name=name, grid=(n_rows, n_steps),
        in_specs=[pl.BlockSpec((group, tm, width), lambda i, t: (t, i, 0)),
                  pl.BlockSpec((group, None, k, width), lambda i, t: (t, layer, 0, 0))] + [_ANY] * n_s,
        out_specs=[pl.BlockSpec((tm, k), lambda i, t: (i, 0))] + [_ANY] * n_s,
        out_shape=[_sds((SEQ, k), F32)] + _rs_chips_shapes(scatter),
        scratch_shapes=[sem, sem] if n_s else [],
        compiler_params=_cparams(("arbitrary", "arbitrary") if n_s else ("parallel", "arbitrary")),
    )(dout, wg, *scatter)
    return res if n_s else res[0]


def _mm_cols_bwd_w(a, dout, *, ns, tn, act_map, w_map, n_tiles, name, tm=1024, n_out=N_CHIP):
    k = a.shape[1]

    def body(a_ref, d_ref, o_ref):
        @pl.when(pl.program_id(1) == 0)
        def _():
            o_ref[...] = jnp.zeros_like(o_ref)

        o_ref[...] += _dot_tn(a_ref[...], d_ref[...])

    return _pcall(
        body, name=name, grid=(n_tiles, SEQ // tm),
        in_specs=[pl.BlockSpec((tm, k), lambda t, i: (i, 0)),
                  pl.BlockSpec((None, tm, tn), lambda t, i: (act_map(t)[0], i, act_map(t)[1]))],
        out_specs=pl.BlockSpec((None, k, tn), lambda t, i: (w_map(t)[0], 0, w_map(t)[1])),
        out_shape=_sds((n_out, k, ns), F32),
        compiler_params=_cparams(("parallel", "arbitrary")),
    )(a, dout)


def _retile_cols(src, *, n_out, width_out, tn, src_map, dst_map, n_tiles, name):
    k = src.shape[1]

    def body(s_ref, o_ref):
        o_ref[...] = s_ref[...]

    return _pcall(
        body, name=name, grid=(n_tiles,),
        in_specs=[pl.BlockSpec((None, k, tn), lambda t: (src_map(t)[0], 0, src_map(t)[1]))],
        out_specs=pl.BlockSpec((None, k, tn), lambda t: (dst_map(t)[0], 0, dst_map(t)[1])),
        out_shape=_sds((n_out, k, width_out), src.dtype),
        compiler_params=_cparams(("parallel",)),
    )(src)


def _mm_rows(a4, wg, layer, x, gate, name, tm=512):
    ks = a4.shape[2]
    n = wg.shape[3]

    def body(a_ref, w_ref, x_ref, g_ref, z_ref, xn_ref):
        z = _dot(a_ref[0], w_ref[0])
        for s in range(1, N_CHIP):
            z += _dot(a_ref[s], w_ref[s])
        z_ref[...] = z.astype(BF16)
        xn_ref[...] = x_ref[...] + g_ref[...] * z

    big = pl.BlockSpec((tm, n), lambda i: (i, 0))
    return _pcall(
        body, name=name, grid=(SEQ // tm,),
        in_specs=[pl.BlockSpec((N_CHIP, tm, ks), lambda i: (0, i, 0)),
                  pl.BlockSpec((N_CHIP, None, ks, n), lambda i: (0, layer, 0, 0)),
                  big, pl.BlockSpec((1, n), lambda i: (0, 0))],
        out_specs=[big, big],
        out_shape=[_sds((SEQ, n), BF16), _sds((SEQ, n), F32)],
        compiler_params=_cparams(("parallel",)),
    )(a4, wg, x, gate)


def _mm_rows_bwd_a(dz, wg, layer, name, tm=1024, halves=()):
    ks, n = wg.shape[2], wg.shape[3]
    n_h = len(halves)
    n_rows = SEQ // tm

    def body(*refs):
        dz_ref, w_ref = refs[:2]
        o_ref = refs[2 + n_h]
        if n_h:
            comm_start, comm_wait = _rs_halves(refs[2:2 + n_h], refs[3 + n_h:3 + 2 * n_h], *refs[3 + 2 * n_h:])
            pl.when((pl.program_id(0) == 0) & (pl.program_id(1) == 0))(comm_start)
        o_ref[...] = _dot_nt(dz_ref[...], w_ref[...])
        if n_h:
            pl.when((pl.program_id(0) == n_rows - 1) & (pl.program_id(1) == N_CHIP - 1))(comm_wait)

    sem = pltpu.SemaphoreType.DMA((max(n_h, 1),))
    res = _pcall(
        body, name=name, grid=(n_rows, N_CHIP),
        in_specs=[pl.BlockSpec((tm, n), lambda i, s: (i, 0)),
                  pl.BlockSpec((None, None, ks, n), lambda i, s: (s, layer, 0, 0))] + [_ANY] * n_h,
        out_specs=[pl.BlockSpec((None, tm, ks), lambda i, s: (s, i, 0))] + [_ANY] * n_h,
        out_shape=[_sds((N_CHIP, SEQ, ks), F32)] + _rs_halves_shapes(halves),
        scratch_shapes=[sem, sem] if n_h else [],
        compiler_params=_cparams(("arbitrary", "arbitrary") if n_h else ("parallel", "arbitrary")),
    )(dz, wg, *halves)
    return res if n_h else res[0]


def _mm_rows_bwd_w(a4, dz, name, tm=2048):
    ks = a4.shape[2]
    n = dz.shape[1]

    def body(a_ref, dz_ref, o_ref):
        @pl.when(pl.program_id(1) == 0)
        def _():
            o_ref[...] = jnp.zeros_like(o_ref)

        o_ref[...] += _dot_tn(a_ref[...], dz_ref[...])

    return _pcall(
        body, name=name, grid=(N_CHIP, SEQ // tm),
        in_specs=[pl.BlockSpec((None, tm, ks), lambda s, i: (s, i, 0)),
                  pl.BlockSpec((tm, n), lambda s, i: (i, 0))],
        out_specs=pl.BlockSpec((None, ks, n), lambda s, i: (s, 0, 0)),
        out_shape=_sds((N_CHIP, ks, n), F32),
        compiler_params=_cparams(("parallel", "arbitrary")),
    )(a4, dz)


def _ffn_up(h, w1g, w3g, layer, name, tm=1024):
    def body(h_ref, w1_ref, w3_ref, a1_ref, a3_ref, u_ref):
        hv = h_ref[...]
        a1 = _dot_nt(hv, w1_ref[...])
        a3 = _dot_nt(hv, w3_ref[...])
        a1_ref[...] = a1.astype(BF16)
        a3_ref[...] = a3.astype(BF16)
        u_ref[...] = (_silu(a1) * a3).astype(BF16)

    wspec = pl.BlockSpec((None, None, FFN_SHARD, D_MODEL), lambda i, s: (s, layer, 0, 0))
    ospec = pl.BlockSpec((None, tm, FFN_SHARD), lambda i, s: (s, i, 0))
    shp = (N_CHIP, SEQ, FFN_SHARD)
    return _pcall(
        body, name=name, grid=(SEQ // tm, N_CHIP),
        in_specs=[pl.BlockSpec((tm, D_MODEL), lambda i, s: (i, 0)), wspec, wspec],
        out_specs=[ospec, ospec, ospec],
        out_shape=[_sds(shp, BF16), _sds(shp, BF16), _sds(shp, BF16)],
        compiler_params=_cparams(("parallel", "arbitrary")),
    )(h, w1g, w3g)


def _ffn_up_bwd(da1, da3, w1g, w3g, layer, name, tm=512):
    def body(d1_ref, d3_ref, w1_ref, w3_ref, o_ref):
        acc = _dot(d1_ref[0], w1_ref[0]) + _dot(d3_ref[0], w3_ref[0])
        for s in range(1, N_CHIP):
            acc += _dot(d1_ref[s], w1_ref[s]) + _dot(d3_ref[s], w3_ref[s])
        o_ref[...] = acc

    wspec = pl.BlockSpec((N_CHIP, None, FFN_SHARD, D_MODEL), lambda i: (0, layer, 0, 0))
    dspec = pl.BlockSpec((N_CHIP, tm, FFN_SHARD), lambda i: (0, i, 0))
    return _pcall(
        body, name=name, grid=(SEQ // tm,),
        in_specs=[dspec, dspec, wspec, wspec],
        out_specs=pl.BlockSpec((tm, D_MODEL), lambda i: (i, 0)),
        out_shape=_sds((SEQ, D_MODEL), F32),
        compiler_params=_cparams(("parallel",)),
    )(da1, da3, w1g, w3g)


def _ffn_down_bwd(dz, w2g, layer, a1, a3, name, tm=1024, halves=()):
    n_h = len(halves)
    n_rows = SEQ // tm

    def body(*refs):
        dz_ref, w_ref, a1_ref, a3_ref = refs[:4]
        da1_ref, da3_ref = refs[4 + n_h:6 + n_h]
        if n_h:
            comm_start, comm_wait = _rs_halves(refs[4:4 + n_h], refs[6 + n_h:6 + 2 * n_h], *refs[6 + 2 * n_h:])
            pl.when((pl.program_id(0) == 0) & (pl.program_id(1) == 0))(comm_start)
        du = _dot_nt(dz_ref[...], w_ref[...])
        a1 = a1_ref[...].astype(F32)
        da1_ref[...] = (du * a3_ref[...].astype(F32) * _dsilu(a1)).astype(BF16)
        da3_ref[...] = (du * _silu(a1)).astype(BF16)
        if n_h:
            pl.when((pl.program_id(0) == n_rows - 1) & (pl.program_id(1) == N_CHIP - 1))(comm_wait)

    blk = pl.BlockSpec((None, tm, FFN_SHARD), lambda i, s: (s, i, 0))
    shp = (N_CHIP, SEQ, FFN_SHARD)
    sem = pltpu.SemaphoreType.DMA((max(n_h, 1),))
    return _pcall(
        body, name=name, grid=(n_rows, N_CHIP),
        in_specs=[pl.BlockSpec((tm, D_MODEL), lambda i, s: (i, 0)),
                  pl.BlockSpec((None, None, FFN_SHARD, D_MODEL), lambda i, s: (s, layer, 0, 0)),
                  blk, blk] + [_ANY] * n_h,
        out_specs=[blk, blk] + [_ANY] * n_h,
        out_shape=[_sds(shp, BF16), _sds(shp, BF16)] + _rs_halves_shapes(halves),
        scratch_shapes=[sem, sem] if n_h else [],
        compiler_params=_cparams(("arbitrary", "arbitrary") if n_h else ("parallel", "arbitrary")),
    )(dz, w2g, a1, a3, *halves)


def _loss_head(y, target, z, gate, name):
    tm = 512
    n_steps = SEQ // tm

    def body(y_ref, t_ref, z_ref, gate_ref, dy_ref, l_ref, dz_ref, dgate_ref, acc_ref):
        @pl.when(pl.program_id(0) == 0)
        def _():
            acc_ref[...] = jnp.zeros_like(acc_ref)
            dgate_ref[...] = jnp.zeros_like(dgate_ref)

        err = y_ref[...] - t_ref[...]
        dy = err * (1.0 / D_MODEL)
        dy_ref[...] = dy
        acc_ref[...] += jnp.sum(jnp.mean(err * err, axis=-1, keepdims=True), axis=0, keepdims=True)
        _gated_branch_bwd(dy, z_ref, gate_ref, dz_ref, dgate_ref)

        @pl.when(pl.program_id(0) == n_steps - 1)
        def _():
            l_ref[...] = 0.5 * acc_ref[...]

    big = pl.BlockSpec((tm, D_MODEL), lambda i: (i, 0))
    vec = pl.BlockSpec((1, D_MODEL), lambda i: (0, 0))
    return _pcall(
        body, name=name, grid=(n_steps,),
        in_specs=[big, big, big, vec],
        out_specs=[big, pl.BlockSpec((1, 1), lambda i: (0, 0)), big, vec],
        out_shape=[_sds((SEQ, D_MODEL), F32), _sds((1, 1), F32), _sds((SEQ, D_MODEL), BF16),
                   _sds((1, D_MODEL), F32)],
        scratch_shapes=[pltpu.VMEM((1, 1), F32)],
        compiler_params=_cparams(("arbitrary",)),
    )(y, target, z, gate)


def _attn_rows(base, d):
    if d == 1:
        return pl.ds(pl.multiple_of(base, ATT_BLK), ATT_BLK)
    return pl.ds(base, ATT_BLK, stride=d)


def _attn_block_index(i, d):
    nb = SEQ // (ATT_BLK * d)
    r = i // nb
    n = i % nb
    base = r + n * (ATT_BLK * d)
    pbase = jnp.maximum(base - ATT_BLK * d, r)
    return n, _attn_rows(base, d), _attn_rows(pbase, d)


def _attn_two_blocks(ref, prow, rows):
    return jnp.concatenate([ref[prow, :].astype(BF16), ref[rows, :].astype(BF16)], axis=0)


def _attn_block_bias(b_ref, n):
    b = b_ref[...]
    prev_half = lax.broadcasted_iota(jnp.int32, b.shape, 1) < ATT_BLK
    return jnp.where(prev_half & (n == 0), NEG, b)


def _qk_normed(x):
    rs = lax.rsqrt(jnp.mean(x * x, axis=-1, keepdims=True) + RMS_EPS)
    return x * rs, rs


def _attn_fwd(qkv9, qgain, kgain, bias, name, gather=()):
    n_g = len(gather)

    def body(*refs):
        q_ref, k_ref, v_ref, qg_ref, kg_ref, b_ref = refs[:6]
        o_ref, lse_ref = refs[6 + n_g:8 + n_g]
        qn_s, kn_s, acc_s, m_s, l_s = refs[8 + 2 * n_g:13 + 2 * n_g]
        g = pl.program_id(1)
        if n_g:
            comm_start, comm_wait = _gather_ici(refs[8 + n_g:8 + 2 * n_g], *refs[13 + 2 * n_g:])
            pl.when((pl.program_id(0) == 0) & (g == 0))(comm_start)

        @pl.when(g == 0)
        def _():
            m_s[...] = jnp.full_like(m_s, NEG)
            l_s[...] = jnp.zeros_like(l_s)
            acc_s[...] = jnp.zeros_like(acc_s)

        qn_s[...] = _qk_normed(q_ref[...])[0] * qg_ref[...]
        kn_s[...] = _qk_normed(k_ref[...])[0] * kg_ref[...]

        for gi, (_, d) in enumerate(GROUPS):
            @pl.when(g == gi)
            def _(d=d):
                def block(n, qb, kk, vv, m_old, l_old, acc_old):
                    s = _dot_nt(qb, kk) * ATT_SCALE + _attn_block_bias(b_ref, n)
                    m_new = jnp.maximum(m_old, jnp.max(s, axis=-1, keepdims=True))
                    alpha = jnp.exp(m_old - m_new)
                    p = jnp.exp(s - m_new)
                    l_new = alpha * l_old + jnp.sum(p, axis=-1, keepdims=True)
                    acc_new = alpha * acc_old + _dot(p.astype(BF16), vv)
                    return m_new, l_new, acc_new

                def it(i, carry):
                    where, loaded = [], []
                    for way in range(ATT_WAYS):
                        n, rows, prow = _attn_block_index(i + way * ATT_STEPS, d)
                        where.append(rows)
                        loaded.append((n, qn_s[rows, :].astype(BF16), _attn_two_blocks(kn_s, prow, rows),
                                       _attn_two_blocks(v_ref, prow, rows), m_s[rows, :], l_s[rows, :],
                                       acc_s[rows, :]))
                    results = [block(*vals) for vals in loaded]
                    for rows, (m_new, l_new, acc_new) in zip(where, results):
                        m_s[rows, :] = m_new
                        l_s[rows, :] = l_new
                        acc_s[rows, :] = acc_new
                    return carry

                lax.fori_loop(0, ATT_STEPS, it, 0)

        @pl.when(g == len(GROUPS) - 1)
        def _():
            o_ref[...] = (acc_s[...] / l_s[...]).astype(BF16)
            lse_ref[...] = m_s[...] + jnp.log(l_s[...])

        if n_g:
            pl.when((pl.program_id(0) == HEADS - 1) & (g == len(GROUPS) - 1))(comm_wait)

    def col(j):
        return pl.BlockSpec((None, SEQ, HEAD_DIM), lambda h, g: (g * 3 + j, 0, h))

    gspec = pl.BlockSpec((None, 1, HEAD_DIM), lambda h, g: (g, 0, 0))
    sem = pltpu.SemaphoreType.DMA((max(n_g, 1), 3))
    return _pcall(
        body, name=name, grid=(HEADS, len(GROUPS)),
        in_specs=[col(0), col(1), col(2), gspec, gspec,
                  pl.BlockSpec((None, None, ATT_BLK, 2 * ATT_BLK), lambda h, g: (g, h, 0, 0))] + [_ANY] * n_g,
        out_specs=[pl.BlockSpec((None, SEQ, HEAD_DIM), lambda h, g: (h // 2, 0, h % 2)),
                   pl.BlockSpec((None, SEQ, 1), lambda h, g: (h, 0, 0))] + [_ANY] * n_g,
        out_shape=[_sds((N_CHIP, SEQ, 2 * HEAD_DIM), BF16), _sds((HEADS, SEQ, 1), F32)]
        + [_sds(s.shape, s.dtype) for s in gather],
        input_output_aliases={6 + a: 2 + a for a in range(n_g)},
        scratch_shapes=[pltpu.VMEM((SEQ, HEAD_DIM), F32)] * 3 + [pltpu.VMEM((SEQ, 1), F32)] * 2
        + ([sem, sem] if n_g else []),
        compiler_params=_cparams(("arbitrary", "arbitrary")),
    )(qkv9, qkv9, qkv9, qgain, kgain, bias, *gather)


def _attn_bwd(qkv9, qgain, kgain, bias, do4, o4, lse, name, scatter=()):
    n_s = len(scatter)

    def body(*refs):
        q_ref, k_ref, v_ref, qg_ref, kg_ref, b_ref, do_ref, o_ref, lse_ref = refs[:9]
        dqkv_ref, dqg_ref, dkg_ref, db_ref = refs[9 + n_s:13 + n_s]
        qn_s, kn_s, dq_s, dk_s, dv_s, dl_s = refs[13 + 2 * n_s:19 + 2 * n_s]
        g = pl.program_id(1)
        if n_s:
            comm_start, comm_wait = _rs_chips(refs[9:9 + n_s], refs[13 + n_s:13 + 2 * n_s], *refs[19 + 2 * n_s:])
            pl.when((pl.program_id(0) == 0) & (g == 0))(comm_start)
        qh, rq = _qk_normed(q_ref[...])
        kh, rk = _qk_normed(k_ref[...])
        qn_s[...] = qh * qg_ref[...]
        kn_s[...] = kh * kg_ref[...]
        @pl.when(g == 0)
        def _():
            dl_s[...] = jnp.sum(do_ref[...] * o_ref[...].astype(F32), axis=-1, keepdims=True)

        dk_s[...] = jnp.zeros_like(dk_s)
        dv_s[...] = jnp.zeros_like(dv_s)
        db_ref[...] = jnp.zeros_like(db_ref)

        for gi, (_, d) in enumerate(GROUPS):
            @pl.when(g == gi)
            def _(d=d):
                def block(n, qb, kk, vv, dob, lse_b, dl):
                    s = _dot_nt(qb, kk) * ATT_SCALE + _attn_block_bias(b_ref, n)
                    p = jnp.exp(s - lse_b)
                    ds = p * (_dot_nt(dob, vv) - dl)
                    ds16 = ds.astype(BF16)
                    return (ds, _dot(ds16, kk) * ATT_SCALE, _dot_tn(ds16, qb) * ATT_SCALE,
                            _dot_tn(p.astype(BF16), dob))

                def it(i, carry):
                    where, loaded, old = [], [], []
                    for way in range(ATT_WAYS):
                        n, rows, prow = _attn_block_index(i + way * ATT_STEPS, d)
                        where.append((rows, prow))
                        loaded.append((n, qn_s[rows, :].astype(BF16), _attn_two_blocks(kn_s, prow, rows),
                                       _attn_two_blocks(v_ref, prow, rows), do_ref[rows, :].astype(BF16),
                                       lse_ref[rows, :], dl_s[rows, :]))
                        old.append((dk_s[rows, :], dk_s[prow, :], dv_s[rows, :], dv_s[prow, :]))
                    results = [block(*vals) for vals in loaded]
                    db_ref[...] += functools.reduce(lambda a, b: a + b, [r[0] for r in results])
                    for (rows, prow), (dk_c, dk_p, dv_c, dv_p), (_, dq, dkk, dvv) in zip(where, old, results):
                        dq_s[rows, :] = dq
                        dk_s[prow, :] = dk_p + dkk[:ATT_BLK]
                        dv_s[prow, :] = dv_p + dvv[:ATT_BLK]
                        dk_s[rows, :] = dk_c + dkk[ATT_BLK:]
                        dv_s[rows, :] = dv_c + dvv[ATT_BLK:]
                    return carry

                lax.fori_loop(0, ATT_STEPS, it, 0)

        def norm_bwd(dn, xh, rs, gain):
            dgain = jnp.sum(dn * xh, axis=0, keepdims=True)
            dxh = dn * gain
            return rs * (dxh - xh * jnp.mean(dxh * xh, axis=-1, keepdims=True)), dgain

        dq, dqg = norm_bwd(dq_s[...], qh, rq, qg_ref[...])
        dk, dkg = norm_bwd(dk_s[...], kh, rk, kg_ref[...])
        dqkv_ref[0] = dq.astype(BF16)
        dqkv_ref[1] = dk.astype(BF16)
        dqkv_ref[2] = dv_s[...].astype(BF16)
        dqg_ref[...] = dqg
        dkg_ref[...] = dkg
        if n_s:
            pl.when((pl.program_id(0) == HEADS - 1) & (g == len(GROUPS) - 1))(comm_wait)

    def col(j):
        return pl.BlockSpec((None, SEQ, HEAD_DIM), lambda h, g: (g * 3 + j, 0, h))

    gspec = pl.BlockSpec((None, 1, HEAD_DIM), lambda h, g: (g, 0, 0))
    bspec = pl.BlockSpec((None, None, ATT_BLK, 2 * ATT_BLK), lambda h, g: (g, h, 0, 0))
    hcol = pl.BlockSpec((None, SEQ, HEAD_DIM), lambda h, g: (h // 2, 0, h % 2))
    dgspec = pl.BlockSpec((None, None, 1, HEAD_DIM), lambda h, g: (h, g, 0, 0))
    ng = len(GROUPS)
    sem = pltpu.SemaphoreType.DMA((max(n_s, 1), 3))
    return _pcall(
        body, name=name, grid=(HEADS, ng),
        in_specs=[col(0), col(1), col(2), gspec, gspec, bspec, hcol, hcol,
                  pl.BlockSpec((None, SEQ, 1), lambda h, g: (h, 0, 0))] + [_ANY] * n_s,
        out_specs=[pl.BlockSpec((None, 3, SEQ, HEAD_DIM), lambda h, g: (g, 0, 0, h)), dgspec, dgspec, bspec]
        + [_ANY] * n_s,
        out_shape=[_sds((ng, 3, SEQ, D_MODEL), BF16), _sds((HEADS, ng, 1, HEAD_DIM), F32),
                   _sds((HEADS, ng, 1, HEAD_DIM), F32), _sds((ng, HEADS, ATT_BLK, 2 * ATT_BLK), F32)]
        + _rs_chips_shapes(scatter),
        scratch_shapes=[pltpu.VMEM((SEQ, HEAD_DIM), F32)] * 5 + [pltpu.VMEM((SEQ, 1), F32)]
        + ([sem, sem] if n_s else []),
        compiler_params=_cparams(("arbitrary", "arbitrary")),
    )(qkv9, qkv9, qkv9, qgain, kgain, bias, do4, o4, lse, *scatter)


def _relbias_bwd(dbias, bucket_idx, name):
    ng = len(GROUPS)

    def body(db_ref, idx_ref, o_ref):
        lane = lax.broadcasted_iota(jnp.int32, (HEADS, 128), 1)
        acc = jnp.zeros((HEADS, 128), F32)
        for g in range(ng):
            dbg = db_ref[g]
            idx = idx_ref[g]
            for b in range(NUM_BUCKETS):
                sel = jnp.where((idx == b)[None], dbg, 0.0)
                part = jnp.sum(sel, axis=1)
                val = jnp.sum(part, axis=-1, keepdims=True)
                acc = jnp.where(lane == g * NUM_BUCKETS + b, val, acc)
        o_ref[...] = acc

    return _pcall(body, name=name, out_shape=_sds((HEADS, 128), F32), compiler_params=_cparams())(dbias, bucket_idx)


def _scan16(x, reverse=False):
    row = lax.broadcasted_iota(jnp.int32, x.shape, 0)
    for sh in (1, 2, 4, 8):
        if reverse:
            x = x + jnp.where(row < HG_SUB - sh, pltpu.roll(x, HG_SUB - sh, 0), 0.0)
        else:
            x = x + jnp.where(row >= sh, pltpu.roll(x, sh, 0), 0.0)
    return x


def _hgrn_gates(qr, fr, lbv):
    q = _silu(qr)
    sig = _sigmoid(fr)
    fg = lbv + (1.0 - lbv) * sig
    lf = jnp.log(fg)
    gcum = _scan16(lf)
    glast = jnp.sum(lf, axis=0, keepdims=True)
    return q, sig, fg, 1.0 - fg, gcum, glast


def _hgrn_intra(q, k, gcum, tri):
    e = jnp.exp(jnp.where(tri, gcum[:, None, :] - gcum[None, :, :], NEG))
    a = jnp.sum(q[:, None, :] * k[None, :, :] * e, axis=-1, keepdims=True)
    return e, a


def _hgrn_fwd(proj4, lb, gain, name):
    nsub = HG_TC // HG_SUB
    wide = HG_HP * HEAD_DIM

    def body(p_ref, lb_ref, gn_ref, o_ref, y_ref, st_ref, state_s):
        @pl.when(pl.program_id(1) == 0)
        def _():
            state_s[...] = jnp.zeros_like(state_s)

        gnv = gn_ref[...]
        shp = (HG_SUB, HG_SUB, HEAD_DIM)
        tri = lax.broadcasted_iota(jnp.int32, shp, 0) >= lax.broadcasted_iota(jnp.int32, shp, 1)

        def head(qr, fr, vv, gr, lbv, st):
            q, _, _, k, gcum, glast = _hgrn_gates(qr, fr, lbv)
            _, a = _hgrn_intra(q, k, gcum, tri)
            o = jnp.sum(a * vv[None, :, :], axis=1) + _dot_nt((q * jnp.exp(gcum)).astype(BF16), st.astype(BF16))
            kg = k * jnp.exp(glast - gcum)
            st_new = st * jnp.exp(glast) + _dot_tn(vv.astype(BF16), kg.astype(BF16))
            rs = lax.rsqrt(jnp.mean(o * o, axis=-1, keepdims=True) + RMS_EPS)
            return o, (o * rs * gnv * _silu(gr)).astype(BF16), st_new

        def it(i, carry):
            rows = pl.ds(pl.multiple_of(i * HG_SUB, HG_SUB), HG_SUB)
            loaded = []
            for hh in range(HG_HP):
                lanes = pl.ds(hh * HEAD_DIM, HEAD_DIM)
                loaded.append(([p_ref[j, rows, lanes] for j in range(4)], lb_ref[:, lanes], state_s[hh]))
            results = [head(blk[0], blk[1], blk[2], blk[3], lbv, st) for blk, lbv, st in loaded]
            for hh, ((_, _, st), (o, y, st_new)) in enumerate(zip(loaded, results)):
                lanes = pl.ds(hh * HEAD_DIM, HEAD_DIM)
                st_ref[hh, i] = st.astype(BF16)
                state_s[hh] = st_new
                o_ref[rows, lanes] = o
                y_ref[hh // 2, rows, pl.ds((hh % 2) * HEAD_DIM, HEAD_DIM)] = y
            return carry

        lax.fori_loop(0, nsub, it, 0)

    return _pcall(
        body, name=name, grid=(HEADS // HG_HP, SEQ // HG_TC),
        in_specs=[pl.BlockSpec((4, HG_TC, wide), lambda h, j: (0, j, h)),
                  pl.BlockSpec((1, wide), lambda h, j: (0, h)),
                  pl.BlockSpec((1, HEAD_DIM), lambda h, j: (0, 0))],
        out_specs=[pl.BlockSpec((HG_TC, wide), lambda h, j: (j, h)),
                   pl.BlockSpec((HG_HP // 2, HG_TC, 2 * HEAD_DIM), lambda h, j: (h, j, 0)),
                   pl.BlockSpec((HG_HP, nsub, HEAD_DIM, HEAD_DIM), lambda h, j: (h, j, 0, 0))],
        out_shape=[_sds((SEQ, D_MODEL), F32), _sds((N_CHIP, SEQ, 2 * HEAD_DIM), BF16),
                   _sds((HEADS, SEQ // HG_SUB, HEAD_DIM, HEAD_DIM), BF16)],
        scratch_shapes=[pltpu.VMEM((HG_HP, HEAD_DIM, HEAD_DIM), F32)],
        compiler_params=_cparams(("parallel", "arbitrary")),
    )(proj4, lb, gain)


def _hgrn_bwd(proj4, lb, gain, o_raw, dy4, states, name):
    nsub = HG_TC // HG_SUB
    nt = SEQ // HG_TC
    wide = HG_HP * HEAD_DIM

    def body(p_ref, lb_ref, gn_ref, o_ref, dy_ref, st_ref, dp_ref, dlb_ref, dgn_ref, dst_s):
        @pl.when(pl.program_id(1) == 0)
        def _():
            dst_s[...] = jnp.zeros_like(dst_s)
            dlb_ref[...] = jnp.zeros_like(dlb_ref)
            dgn_ref[...] = jnp.zeros_like(dgn_ref)

        gnv = gn_ref[...]
        shp = (HG_SUB, HG_SUB, HEAD_DIM)
        tri = lax.broadcasted_iota(jnp.int32, shp, 0) >= lax.broadcasted_iota(jnp.int32, shp, 1)

        def head(qr, fr, vv, gr, o, dy, lbv, st0, dst):
            q, sig, fg, k, gcum, glast = _hgrn_gates(qr, fr, lbv)
            rs = lax.rsqrt(jnp.mean(o * o, axis=-1, keepdims=True) + RMS_EPS)
            oh = o * rs
            don = dy * _silu(gr)
            dgn = jnp.sum(don * oh, axis=0, keepdims=True)
            dgr = dy * oh * gnv * _dsilu(gr)
            doh = don * gnv
            do = rs * (doh - oh * jnp.mean(doh * oh, axis=-1, keepdims=True))
            dst16 = dst.astype(BF16)
            do16 = do.astype(BF16)
            eg = jnp.exp(gcum)
            eb = jnp.exp(glast - gcum)
            e, a = _hgrn_intra(q, k, gcum, tri)
            da = jnp.sum(do[:, None, :] * vv[None, :, :], axis=-1, keepdims=True)
            dae = da * e
            dq = jnp.sum(dae * k[None, :, :], axis=1) + eg * _dot(do16, st0)
            dk_state = eb * _dot(vv.astype(BF16), dst16)
            dk = jnp.sum(dae * q[:, None, :], axis=0) + dk_state
            dv = jnp.sum(a * do[:, None, :], axis=0) + _dot_nt((k * eb).astype(BF16), dst16)
            eglast = jnp.exp(glast)
            dst_new = dst * eglast + _dot_tn(do16, (q * eg).astype(BF16))
            dglast = jnp.sum(k * dk_state, axis=0, keepdims=True) \
                + eglast * jnp.sum(dst * st0.astype(F32), axis=0, keepdims=True)
            dlf = _scan16(q * dq - k * dk, reverse=True) + dglast
            dfg = dlf / fg - dk
            dlb = jnp.sum(dfg * (1.0 - sig), axis=0, keepdims=True)
            dproj = ((dq * _dsilu(qr)).astype(BF16), (dfg * (1.0 - lbv) * sig * (1.0 - sig)).astype(BF16),
                     dv.astype(BF16), dgr.astype(BF16))
            return dproj, dst_new, dlb, dgn

        def it(ii, carry):
            i = nsub - 1 - ii
            rows = pl.ds(pl.multiple_of(i * HG_SUB, HG_SUB), HG_SUB)
            results = []
            for hh in range(HG_HP):
                lanes = pl.ds(hh * HEAD_DIM, HEAD_DIM)
                blk = [p_ref[j, rows, lanes] for j in range(4)]
                dy = dy_ref[hh // 2, rows, pl.ds((hh % 2) * HEAD_DIM, HEAD_DIM)]
                results.append(head(blk[0], blk[1], blk[2], blk[3], o_ref[rows, lanes], dy,
                                    lb_ref[:, lanes], st_ref[hh, i], dst_s[hh]))
            new_carry = []
            for hh, (dproj, dst_new, dlb, dgn) in enumerate(results):
                lanes = pl.ds(hh * HEAD_DIM, HEAD_DIM)
                dst_s[hh] = dst_new
                for j in range(4):
                    dp_ref[j, rows, lanes] = dproj[j]
                new_carry.append((carry[hh][0] + dlb, carry[hh][1] + dgn))
            return tuple(new_carry)

        zero = jnp.zeros((1, HEAD_DIM), F32)
        sums = lax.fori_loop(0, nsub, it, tuple((zero, zero) for _ in range(HG_HP)))
        for hh in range(HG_HP):
            dlb_ref[hh] += sums[hh][0]
            dgn_ref[hh] += sums[hh][1]

    vspec = pl.BlockSpec((HG_HP, 1, HEAD_DIM), lambda h, j: (h, 0, 0))
    return _pcall(
        body, name=name, grid=(HEADS // HG_HP, nt),
        in_specs=[pl.BlockSpec((4, HG_TC, wide), lambda h, j: (0, nt - 1 - j, h)),
                  pl.BlockSpec((1, wide), lambda h, j: (0, h)),
                  pl.BlockSpec((1, HEAD_DIM), lambda h, j: (0, 0)),
                  pl.BlockSpec((HG_TC, wide), lambda h, j: (nt - 1 - j, h)),
                  pl.BlockSpec((HG_HP // 2, HG_TC, 2 * HEAD_DIM), lambda h, j: (h, nt - 1 - j, 0)),
                  pl.BlockSpec((HG_HP, nsub, HEAD_DIM, HEAD_DIM), lambda h, j: (h, nt - 1 - j, 0, 0))],
        out_specs=[pl.BlockSpec((4, HG_TC, wide), lambda h, j: (0, nt - 1 - j, h)), vspec, vspec],
        out_shape=[_sds((4, SEQ, D_MODEL), BF16), _sds((HEADS, 1, HEAD_DIM), F32), _sds((HEADS, 1, HEAD_DIM), F32)],
        scratch_shapes=[pltpu.VMEM((HG_HP, HEAD_DIM, HEAD_DIM), F32)],
        compiler_params=_cparams(("parallel", "arbitrary")),
    )(proj4, lb, gain, o_raw, dy4, states)


def _t5_bucket(dist):
    n = np.asarray(dist, dtype=np.int64)
    max_exact = NUM_BUCKETS // 2
    large = max_exact + (np.log(np.maximum(n, 1) / max_exact) / np.log(MAX_DISTANCE / max_exact)
                         * (NUM_BUCKETS - max_exact)).astype(np.int64)
    large = np.minimum(large, NUM_BUCKETS - 1)
    return np.where(n < max_exact, n, large).astype(np.int32)


def _bias_tables():
    qi = np.arange(ATT_BLK)[:, None]
    ki = np.arange(2 * ATT_BLK)[None, :]
    j = ATT_BLK + qi - ki
    valid = (j >= 0) & (j <= ATT_BLK)
    return np.stack([np.where(valid, _t5_bucket(np.clip(j, 0, ATT_BLK) * d), -1) for _, d in GROUPS]).astype(np.int32)


def _attn_bias(rel_bias, name):
    idx = _bias_tables()
    ng = len(GROUPS)
    buckets = [sorted(set(idx[g][idx[g] >= 0].tolist())) for g in range(ng)]

    def body(rb_ref, idx_ref, o_ref):
        h = pl.program_id(0)
        for g in range(ng):
            ig = idx_ref[g]
            acc = jnp.full(ig.shape, NEG, F32)
            for b in buckets[g]:
                acc = jnp.where(ig == b, rb_ref[b, g * HEADS + h], acc)
            o_ref[g] = acc

    return _pcall(
        body, name=name, grid=(HEADS,),
        in_specs=[pl.BlockSpec(memory_space=pltpu.SMEM),
                  pl.BlockSpec((ng, ATT_BLK, 2 * ATT_BLK), lambda h: (0, 0, 0))],
        out_specs=pl.BlockSpec((ng, None, ATT_BLK, 2 * ATT_BLK), lambda h: (0, h, 0, 0)),
        out_shape=_sds((ng, HEADS, ATT_BLK, 2 * ATT_BLK), F32),
        compiler_params=_cparams(("parallel",)),
    )(rel_bias, jnp.asarray(idx))


ADA_SHARD = 6 * D_MODEL // N_CHIP
ADA_TN = 512


def _ada_fwd(c_all, ada_w, ada_b_cols, name):
    def body(c_ref, w_ref, b_ref, o_ref):
        ca = _silu(c_ref[...]).astype(BF16)
        o_ref[...] = _dot(ca, w_ref[...].astype(BF16)) + b_ref[...]

    return _pcall(
        body, name=name, grid=(DEPTH, ADA_SHARD // ADA_TN),
        in_specs=[pl.BlockSpec((N_DEV, D_MODEL), lambda l, j: (0, 0)),
                  pl.BlockSpec((None, D_MODEL, ADA_TN), lambda l, j: (l, 0, j)),
                  pl.BlockSpec((None, 1, ADA_TN), lambda l, j: (l, 0, j))],
        out_specs=pl.BlockSpec((None, N_DEV, ADA_TN), lambda l, j: (l, 0, j)),
        out_shape=_sds((DEPTH, N_DEV, ADA_SHARD), F32),
        compiler_params=_cparams(("parallel", "parallel")),
    )(c_all, ada_w, ada_b_cols)


def _ada_bwd(c_all, dmod_cols, name):
    def body(c_ref, d_ref, o_ref):
        ca = _silu(c_ref[...]).astype(BF16)
        o_ref[...] = _dot_tn(ca, d_ref[...].astype(BF16))

    return _pcall(
        body, name=name, grid=(DEPTH, ADA_SHARD // ADA_TN),
        in_specs=[pl.BlockSpec((N_DEV, D_MODEL), lambda l, j: (0, 0)),
                  pl.BlockSpec((None, N_DEV, ADA_TN), lambda l, j: (l, 0, j))],
        out_specs=pl.BlockSpec((None, D_MODEL, ADA_TN), lambda l, j: (l, 0, j)),
        out_shape=_sds((DEPTH, D_MODEL, ADA_SHARD), F32),
        compiler_params=_cparams(("parallel", "parallel")),
    )(c_all, dmod_cols)


def _lower_bounds(logits, name):
    def body(l_ref, o_ref):
        l0 = l_ref[0:1, :]
        l1 = l_ref[1:2, :]
        mx = jnp.maximum(l0, l1)
        e0 = jnp.exp(l0 - mx)
        e1 = jnp.exp(l1 - mx)
        p0 = e0 / (e0 + e1)
        p1 = e1 / (e0 + e1)
        o_ref[0:1, :] = p0 - p0
        o_ref[1:2, :] = (p0 + p1) - p0

    return _pcall(body, name=name, out_shape=_sds((DEPTH, D_MODEL), F32), compiler_params=_cparams())(logits)


_R_DMOD = 0
_R_NMIX = 96
_R_NFFN = 112
_R_QG = 128
_R_KG = 152
_R_GN = 176
_R_LB = 184
_R_RB = 192
SMALL_ROWS = 200


def _small_totals(gathered, logits8, name):
    ng = len(GROUPS)

    def body(g_ref, l_ref, main_ref, gains_ref, dlb_ref, rb_ref):
        tot = g_ref[0]
        for dev in range(1, N_DEV):
            tot = tot + g_ref[dev]
        main_ref[...] = tot[0:_R_QG]
        gains_ref[...] = jnp.zeros_like(gains_ref)
        for g in range(ng):
            gains_ref[g:g + 1, :] = jnp.sum(tot[_R_QG + 8 * g:_R_QG + 8 * g + 8], axis=0, keepdims=True)
            gains_ref[ng + g:ng + g + 1, :] = jnp.sum(tot[_R_KG + 8 * g:_R_KG + 8 * g + 8], axis=0, keepdims=True)
        gains_ref[2 * ng:2 * ng + 1, :] = jnp.sum(tot[_R_GN:_R_GN + 8], axis=0, keepdims=True)
        rb_ref[...] = tot[_R_RB:_R_RB + 8]
        dlb1 = tot[_R_LB:_R_LB + 8]
        l0 = l_ref[0]
        l1 = l_ref[1]
        mx = jnp.maximum(l0, l1)
        e0 = jnp.exp(l0 - mx)
        e1 = jnp.exp(l1 - mx)
        p0 = e0 / (e0 + e1)
        p1 = e1 / (e0 + e1)
        dlb_ref[0] = -p0 * p1 * dlb1
        dlb_ref[1] = p1 * (1.0 - p1) * dlb1

    return _pcall(
        body, name=name,
        out_shape=[_sds((_R_QG, 128), F32), _sds((8, 128), F32), _sds((DEPTH, 8, 128), F32), _sds((8, 128), F32)],
        compiler_params=_cparams(),
    )(gathered, logits8)


def _row_tile(rows):
    return 128 if rows % 128 == 0 else rows


def _adamw(w, grads, m, v, name):
    nl, r, cdim = w.shape
    tr = _row_tile(r)

    def body(*refs):
        g_refs = refs[:nl]
        w_ref, m_ref, v_ref, go_ref, d_ref, mo_ref, vo_ref = refs[nl:]

        def step(g):
            m2 = ADAM_B1 * m_ref[...] + (1.0 - ADAM_B1) * g
            v2 = ADAM_B2 * v_ref[...] + (1.0 - ADAM_B2) * (g * g)
            m_hat = m2 / (1.0 - ADAM_B1 ** ADAM_STEP)
            v_hat = v2 / (1.0 - ADAM_B2 ** ADAM_STEP)
            go_ref[...] = g
            d_ref[...] = -ADAM_LR * (m_hat / (jnp.sqrt(v_hat) + ADAM_EPS) + ADAM_WD * w_ref[...])
            mo_ref[...] = m2
            vo_ref[...] = v2

        if nl == 1:
            step(g_refs[0][...])
        else:
            for layer in range(nl):
                @pl.when(pl.program_id(0) == layer)
                def _(layer=layer):
                    step(g_refs[layer][...])

    big = pl.BlockSpec((None, tr, cdim), lambda l, i: (l, i, 0))
    g_specs = [pl.BlockSpec((tr, cdim), lambda l, i, layer=layer: (jnp.where(l == layer, i, 0), 0))
               for layer in range(nl)]
    shp = _sds((nl, r, cdim), F32)
    return _pcall(
        body, name=name, grid=(nl, r // tr),
        in_specs=g_specs + [big, big, big],
        out_specs=[big, big, big, big],
        out_shape=[shp, shp, shp, shp],
        compiler_params=_cparams(("parallel", "parallel")),
    )(*grads, w, m, v)


def _cast_bf16(place, w, name):
    nl, r, cdim = w.shape
    tr = _row_tile(r)

    def body(place_ref, w_ref, o_ref):
        o_ref[...] = w_ref[...].astype(BF16)

    return _pcall(
        body, name=name,
        grid_spec=pltpu.PrefetchScalarGridSpec(
            num_scalar_prefetch=1, grid=(nl, r // tr),
            in_specs=[pl.BlockSpec((None, tr, cdim), lambda l, i, place_ref: (l, i, 0))],
            out_specs=pl.BlockSpec((None, None, tr, cdim), lambda l, i, place_ref: (place_ref[1], l, i, 0))),
        out_shape=_sds((N_CHIP, nl, r, cdim), BF16),
        compiler_params=_cparams(("parallel", "parallel")),
    )(place, w)


def _rs_add_cast(place, grad, recv, name):
    _, k, n = grad.shape
    kh = k // 2
    tr = _row_tile(kh)
    nb = kh // tr

    def body(place_ref, g_ref, r_ref, o_ref):
        o_ref[...] = (g_ref[...] + r_ref[...]).astype(BF16)

    half = pl.BlockSpec((None, tr, n), lambda s, i, place_ref: (s, i, 0))
    return _pcall(
        body, name=name,
        grid_spec=pltpu.PrefetchScalarGridSpec(
            num_scalar_prefetch=1, grid=(N_CHIP, nb),
            in_specs=[pl.BlockSpec((None, tr, n), lambda s, i, place_ref: (s, place_ref[0] * nb + i, 0)), half],
            out_specs=half),
        out_shape=_sds((N_CHIP, kh, n), BF16),
        compiler_params=_cparams(("parallel", "parallel")),
    )(place, grad, recv)


def _rs_sum4(place, parts, got, name):
    _, kh, n = parts.shape
    tr = _row_tile(kh)
    nb = kh // tr

    def body(place_ref, p_ref, g_ref, o_ref):
        acc = p_ref[...].astype(F32)
        for j in range(N_CHIP - 1):
            acc = acc + g_ref[j].astype(F32)
        o_ref[...] = acc

    return _pcall(
        body, name=name,
        grid_spec=pltpu.PrefetchScalarGridSpec(
            num_scalar_prefetch=1, grid=(nb,),
            in_specs=[pl.BlockSpec((None, tr, n), lambda i, place_ref: (place_ref[1], i, 0)),
                      pl.BlockSpec((N_CHIP - 1, tr, n), lambda i, place_ref: (0, i, 0))],
            out_specs=pl.BlockSpec((tr, n), lambda i, place_ref: (place_ref[0] * nb + i, 0))),
        out_shape=_sds((2 * kh, n), F32),
        compiler_params=_cparams(("parallel",)),
    )(place, parts, got)


_ANY = pl.BlockSpec(memory_space=pl.ANY)


def _position():
    return lax.axis_index("x"), lax.axis_index("y"), lax.axis_index("c")


def _other_chips(x, y):
    return [(1 - x, y), (x, 1 - y), (1 - x, 1 - y)]


def _remote(src, dst, send_sem, recv_sem, to):
    return pltpu.make_async_remote_copy(src_ref=src, dst_ref=dst, send_sem=send_sem, recv_sem=recv_sem,
                                        device_id=to, device_id_type=MESH)


def _small_allgather(v, name):
    r = v.shape[0]

    def body(x_ref, out_ref, send_sems, recv_sems, local_sem):
        x, y, c = _position()
        me, sibling = (x, y, c), (x, y, 1 - c)
        chips = _other_chips(x, y)

        def slab(px, py, pc):
            return out_ref.at[4 * px + 2 * py + pc]

        def copy(k, block, to, src=None):
            return _remote(slab(*block) if src is None else src, slab(*block), send_sems.at[k], recv_sems.at[k], to)

        mine = pltpu.make_async_copy(x_ref, slab(*me), local_sem)
        mine.start()
        first = [copy(0, me, sibling, src=x_ref)]
        first += [copy(1 + j, me, (*chip, c), src=x_ref) for j, chip in enumerate(chips)]
        for cp in first:
            cp.start()
        passed = [copy(4 + j, (*chip, c), sibling) for j, chip in enumerate(chips)]
        for j, chip in enumerate(chips):
            copy(1 + j, (*chip, c), me).wait_recv()
            passed[j].start()
        copy(0, sibling, me).wait_recv()
        for j, chip in enumerate(chips):
            copy(4 + j, (*chip, 1 - c), me).wait_recv()
        for cp in first + passed:
            cp.wait_send()
        mine.wait()

    return _pcall(
        body, name=name,
        out_shape=_sds((N_DEV, r, 128), F32),
        in_specs=[pl.BlockSpec(memory_space=pltpu.VMEM)],
        out_specs=pl.BlockSpec(memory_space=pltpu.VMEM),
        scratch_shapes=[pltpu.SemaphoreType.DMA((7,)), pltpu.SemaphoreType.DMA((7,)), pltpu.SemaphoreType.DMA],
        compiler_params=_cparams(),
    )(v)


def _half_rows(core, kh):
    return pl.ds(pl.multiple_of(core * kh, 8), kh)


def _slab_half(ref, chip, core):
    return ref.at[chip, :, _half_rows(core, ref.shape[2] // 2), :]


def _gather_ici(out, send_sems, recv_sems):
    def copies():
        x, y, c = _position()
        for a in range(len(out)):
            for j, (px, py) in enumerate(_other_chips(x, y)):
                mine = _slab_half(out[a], 2 * x + y, c)
                landed = _slab_half(out[a], 2 * px + py, c)
                yield (_remote(mine, mine, send_sems.at[a, j], recv_sems.at[a, j], (px, py, c)),
                       _remote(landed, landed, send_sems.at[a, j], recv_sems.at[a, j], (px, py, c)))

    def start():
        for send, _ in copies():
            send.start()

    def wait():
        for send, recv in copies():
            recv.wait_recv()
            send.wait_send()

    return start, wait


def _gather_d2d(out, send_sems, recv_sems):
    def copies():
        x, y, c = _position()
        for a in range(len(out)):
            for j, (px, py) in enumerate(_other_chips(x, y)):
                landed = _slab_half(out[a], 2 * px + py, c)
                other = _slab_half(out[a], 2 * px + py, 1 - c)
                yield (_remote(landed, landed, send_sems.at[a, j], recv_sems.at[a, j], (x, y, 1 - c)),
                       _remote(other, other, send_sems.at[a, j], recv_sems.at[a, j], (x, y, 1 - c)))

    def start():
        for send, _ in copies():
            send.start()

    def wait():
        for send, recv in copies():
            recv.wait_recv()
            send.wait_send()

    return start, wait


def _gather_weights(slabs, name, ici=True):
    n = len(slabs)

    def body(*refs):
        out = refs[n:2 * n]
        sems = refs[2 * n:]
        if ici:
            start, wait = _gather_ici(out, sems[2], sems[3])
            start()
            wait()
        start, wait = _gather_d2d(out, sems[0], sems[1])
        start()
        wait()

    sem = pltpu.SemaphoreType.DMA((n, 3))
    return _pcall(
        body, name=name,
        out_shape=[_sds(s.shape, BF16) for s in slabs],
        in_specs=[_ANY] * n, out_specs=[_ANY] * n,
        input_output_aliases={a: a for a in range(n)},
        scratch_shapes=[sem, sem] + ([sem, sem] if ici else []),
        compiler_params=_cparams(),
    )(*slabs)


def _rs_halves(grads, out, send_sems, recv_sems):
    def copies():
        x, y, c = _position()
        for a in range(len(grads)):
            kh = grads[a].shape[1] // 2
            yield _remote(grads[a].at[:, _half_rows(1 - c, kh), :], out[a], send_sems.at[a], recv_sems.at[a],
                          (x, y, 1 - c))

    def start():
        for cp in copies():
            cp.start()

    def wait():
        for cp in copies():
            cp.wait()

    return start, wait


def _rs_halves_shapes(grads):
    return [_sds((N_CHIP, g.shape[1] // 2, g.shape[2]), F32) for g in grads]


def _rs_exchange_halves(grads, name):
    n = len(grads)

    def body(*refs):
        start, wait = _rs_halves(refs[:n], refs[n:2 * n], *refs[2 * n:])
        start()
        wait()

    return _pcall(
        body, name=name,
        out_shape=_rs_halves_shapes(grads),
        in_specs=[_ANY] * n, out_specs=[_ANY] * n,
        scratch_shapes=[pltpu.SemaphoreType.DMA((n,)), pltpu.SemaphoreType.DMA((n,))],
        compiler_params=_cparams(),
    )(*grads)


def _rs_chips(parts, out, send_sems, recv_sems):
    def copies():
        x, y, c = _position()
        for a in range(len(parts)):
            for j, (px, py) in enumerate(_other_chips(x, y)):
                got = out[a].at[j]
                yield (_remote(parts[a].at[2 * px + py], got, send_sems.at[a, j], recv_sems.at[a, j], (px, py, c)),
                       _remote(got, got, send_sems.at[a, j], recv_sems.at[a, j], (px, py, c)))

    def start():
        for send, _ in copies():
            send.start()

    def wait():
        for send, recv in copies():
            recv.wait_recv()
            send.wait_send()

    return start, wait


def _rs_chips_shapes(parts):
    return [_sds((N_CHIP - 1,) + p.shape[1:], BF16) for p in parts]


def _rs_join(out, send_sems, recv_sems):
    def copies():
        x, y, c = _position()
        for a in range(len(out)):
            kh = out[a].shape[0] // 2
            mine = out[a].at[_half_rows(c, kh), :]
            theirs = out[a].at[_half_rows(1 - c, kh), :]
            yield (_remote(mine, mine, send_sems.at[a], recv_sems.at[a], (x, y, 1 - c)),
                   _remote(theirs, theirs, send_sems.at[a], recv_sems.at[a], (x, y, 1 - c)))

    def start():
        for send, _ in copies():
            send.start()

    def wait():
        for send, recv in copies():
            recv.wait_recv()
            send.wait_send()

    return start, wait


def _rs_join_halves(fulls, name):
    n = len(fulls)

    def body(*refs):
        start, wait = _rs_join(refs[n:2 * n], *refs[2 * n:])
        start()
        wait()

    return _pcall(
        body, name=name,
        out_shape=[_sds(f.shape, F32) for f in fulls],
        in_specs=[_ANY] * n, out_specs=[_ANY] * n,
        input_output_aliases={a: a for a in range(n)},
        scratch_shapes=[pltpu.SemaphoreType.DMA((n,)), pltpu.SemaphoreType.DMA((n,))],
        compiler_params=_cparams(),
    )(*fulls)


_SMALL_ORDER = ("rel_bias", "ada_b", "norm_mix", "norm_ffn", "attn_q_gain", "attn_k_gain", "hgrn_gnorm",
                "hgrn_lower_bounds")
_WEIGHT_ORDER = ("rel_bias", "ada_w", "ada_b", "norm_mix", "norm_ffn", "attn_w_qkv", "attn_w_out", "attn_q_gain",
                 "attn_k_gain", "hgrn_w_in", "hgrn_w_out", "hgrn_gnorm", "hgrn_lower_bounds", "ffn_w1", "ffn_w3",
                 "ffn_w2")


def _qkv_group_map(t):
    return t // 4, t % 4


def _qkv_chip_map(t):
    return t // 9, t % 9


def _hin_map(t):
    return t // 2, t % 2


def _block_map(t):
    return t, 0


def _pack_rows(parts):
    return jnp.concatenate([p.reshape(-1, 128) for p in parts], axis=0)


def kernel(x, c, rel_bias, ada_w, ada_b, norm_mix, norm_ffn, attn_w_qkv, attn_w_out, attn_q_gain, attn_k_gain, hgrn_w_in, hgrn_w_out, hgrn_gnorm, hgrn_lower_bounds, ffn_w1, ffn_w3, ffn_w2, loss_target, m_rel_bias, m_ada_w, m_ada_b, m_norm_mix, m_norm_ffn, m_attn_w_qkv, m_attn_w_out, m_attn_q_gain, m_attn_k_gain, m_hgrn_w_in, m_hgrn_w_out, m_hgrn_gnorm, m_hgrn_lower_bounds, m_ffn_w1, m_ffn_w3, m_ffn_w2, v_rel_bias, v_ada_w, v_ada_b, v_norm_mix, v_norm_ffn, v_attn_w_qkv, v_attn_w_out, v_attn_q_gain, v_attn_k_gain, v_hgrn_w_in, v_hgrn_w_out, v_hgrn_gnorm, v_hgrn_lower_bounds, v_ffn_w1, v_ffn_w3, v_ffn_w2):
    weights = dict(rel_bias=rel_bias, ada_w=ada_w, ada_b=ada_b, norm_mix=norm_mix, norm_ffn=norm_ffn,
                   attn_w_qkv=attn_w_qkv, attn_w_out=attn_w_out, attn_q_gain=attn_q_gain, attn_k_gain=attn_k_gain,
                   hgrn_w_in=hgrn_w_in, hgrn_w_out=hgrn_w_out, hgrn_gnorm=hgrn_gnorm,
                   hgrn_lower_bounds=hgrn_lower_bounds, ffn_w1=ffn_w1, ffn_w3=ffn_w3, ffn_w2=ffn_w2)
    mom1 = dict(rel_bias=m_rel_bias, ada_w=m_ada_w, ada_b=m_ada_b, norm_mix=m_norm_mix, norm_ffn=m_norm_ffn,
                attn_w_qkv=m_attn_w_qkv, attn_w_out=m_attn_w_out, attn_q_gain=m_attn_q_gain,
                attn_k_gain=m_attn_k_gain, hgrn_w_in=m_hgrn_w_in, hgrn_w_out=m_hgrn_w_out, hgrn_gnorm=m_hgrn_gnorm,
                hgrn_lower_bounds=m_hgrn_lower_bounds, ffn_w1=m_ffn_w1, ffn_w3=m_ffn_w3, ffn_w2=m_ffn_w2)
    mom2 = dict(rel_bias=v_rel_bias, ada_w=v_ada_w, ada_b=v_ada_b, norm_mix=v_norm_mix, norm_ffn=v_norm_ffn,
                attn_w_qkv=v_attn_w_qkv, attn_w_out=v_attn_w_out, attn_q_gain=v_attn_q_gain,
                attn_k_gain=v_attn_k_gain, hgrn_w_in=v_hgrn_w_in, hgrn_w_out=v_hgrn_w_out, hgrn_gnorm=v_hgrn_gnorm,
                hgrn_lower_bounds=v_hgrn_lower_bounds, ffn_w1=v_ffn_w1, ffn_w3=v_ffn_w3, ffn_w2=v_ffn_w2)

    transposed = ("ffn_w1", "ffn_w3")
    for group in (weights, mom1, mom2):
        for k in transposed:
            group[k] = jnp.transpose(group[k], (0, 2, 1))

    xi, yi, ci = _position()
    chip = 2 * xi + yi
    dev = 4 * xi + 2 * yi + ci
    place = jnp.stack([ci, chip]).astype(jnp.int32)
    d = D_MODEL

    big_names = ("attn_w_qkv", "attn_w_out", "hgrn_w_in", "hgrn_w_out", "ffn_w1", "ffn_w3", "ffn_w2")
    early_names, late_names = big_names[:1], big_names[1:]
    slabs16 = {k: _cast_bf16(place, weights[k], "cast_" + k) for k in big_names}
    wg = dict(zip(early_names, _gather_weights([slabs16[k] for k in early_names], "gather_early")))

    c_all = _small_allgather(c.reshape(8, 128), "gather_c").reshape(N_DEV, d)
    ada_b_cols = lax.dynamic_slice(ada_b, (0, chip * ADA_SHARD), (DEPTH, ADA_SHARD)).reshape(DEPTH, 1, ADA_SHARD)
    mod_shard = _ada_fwd(c_all, ada_w, ada_b_cols, "ada_fwd")
    mod_all = _small_allgather(mod_shard.reshape(-1, 128), "gather_mod").reshape(N_DEV, DEPTH, N_DEV, ADA_SHARD)
    mod_mine = lax.dynamic_index_in_dim(mod_all[0::2], dev, axis=2, keepdims=False)
    mod = jnp.transpose(mod_mine, (1, 0, 2)).reshape(DEPTH, 6 * d)

    def mods(layer):
        return [mod[layer:layer + 1, j * d:(j + 1) * d] for j in range(6)]

    x0 = x.reshape(SEQ, d)
    target = loss_target.reshape(SEQ, d)
    qg = attn_q_gain.reshape(len(GROUPS), 1, HEAD_DIM)
    kg = attn_k_gain.reshape(len(GROUPS), 1, HEAD_DIM)
    bias = _attn_bias(rel_bias, "attn_bias")
    lb1 = _lower_bounds(hgrn_lower_bounds, "lower_bounds")[1:2]

    def ffn_fwd(layer, x_in, sc2, sh2, g2):
        hf = _norm_mod(x_in, norm_ffn[layer:layer + 1], sc2, sh2, f"l{layer}_norm_ffn")
        a1, a3, u = _ffn_up(hf, wg["ffn_w1"], wg["ffn_w3"], layer, f"l{layer}_ffn_up")
        z, x_out = _mm_rows(u, wg["ffn_w2"], layer, x_in, g2, f"l{layer}_ffn_down")
        return x_out, (hf, a1, a3, u, z)

    def ffn_bwd(layer, dz, dg2, dx_out, x_in, sc2, sh2, saved, mixer_branch, halves=()):
        hf, a1, a3, u, _ = saved
        da1, da3, *recv = _ffn_down_bwd(dz, wg["ffn_w2"], layer, a1, a3, f"l{layer}_ffn_down_bwd", halves=halves)
        dw2 = _mm_rows_bwd_w(u, dz, f"l{layer}_dw2")
        dh = _ffn_up_bwd(da1, da3, wg["ffn_w1"], wg["ffn_w3"], layer, f"l{layer}_ffn_up_bwd")
        dw1 = _mm_rows_bwd_w(da1, hf, f"l{layer}_dw1")
        dw3 = _mm_rows_bwd_w(da3, hf, f"l{layer}_dw3")
        dx_in, dsc2, dsh2, dnf, dz_mix, dg_mix = _norm_mod_bwd(x_in, norm_ffn[layer:layer + 1], sc2, sh2, dh, dx_out,
                                                               f"l{layer}_norm_ffn_bwd", branch=mixer_branch)
        return dx_in, (dw1, dw3, dw2), (dsh2, dsc2, dg2), dnf, recv, dz_mix, dg_mix

    def rs_add(tags, grads_in, recv):
        return [_rs_add_cast(place, g, r, f"rs_add_{k}_{layer}") for (k, layer), g, r in zip(tags, grads_in, recv)]

    sh1_0, sc1_0, g1_0, sh2_0, sc2_0, g2_0 = mods(0)
    h0 = _norm_mod(x0, norm_mix[0:1], sc1_0, sh1_0, "l0_norm_mix")
    w_qkv9 = _retile_cols(wg["attn_w_qkv"].reshape(N_CHIP, d, 2304), n_out=9, width_out=d, tn=256,
                          src_map=_qkv_chip_map, dst_map=_qkv_group_map, n_tiles=36,
                          name="regroup_w_qkv").reshape(9, 1, d, d)
    qkv9 = _mm_cols(h0, w_qkv9, 0, n_blocks=9, width=d, tn=d, act_map=_block_map, w_map=_block_map,
                    out_dtype=F32, name="l0_qkv")
    o4, lse, *late = _attn_fwd(qkv9, qg, kg, bias, "l0_attn", gather=[slabs16[k] for k in late_names])
    wg.update(zip(late_names, _gather_weights(late, "gather_late_siblings", ici=False)))
    y0, x1 = _mm_rows(o4, wg["attn_w_out"], 0, x0, g1_0, "l0_attn_out")
    x2, ffn0 = ffn_fwd(0, x1, sc2_0, sh2_0, g2_0)

    sh1_1, sc1_1, g1_1, sh2_1, sc2_1, g2_1 = mods(1)
    h1 = _norm_mod(x2, norm_mix[1:2], sc1_1, sh1_1, "l1_norm_mix")
    proj4 = _mm_cols(h1, wg["hgrn_w_in"], 0, n_blocks=4, width=d, tn=512, act_map=_hin_map, w_map=_hin_map,
                     out_dtype=F32, name="l1_hgrn_in")
    o_raw, yg4, states = _hgrn_fwd(proj4, lb1, hgrn_gnorm, "l1_hgrn")
    y1, x3 = _mm_rows(yg4, wg["hgrn_w_out"], 0, x2, g1_1, "l1_hgrn_out")
    x4, ffn1 = ffn_fwd(1, x3, sc2_1, sh2_1, g2_1)

    dx4, loss_part, dz_ffn1, dg2_1 = _loss_head(x4, target, ffn1[4], g2_1, "loss_head")
    loss = lax.psum(loss_part[0, 0], ("x", "y", "c"))

    dx3, (dw1_1, dw3_1, dw2_1), dmod2_1, dnf_1, _, dzm1, dg1_1 = ffn_bwd(
        1, dz_ffn1, dg2_1, dx4, x3, sc2_1, sh2_1, ffn1, (y1, g1_1))
    dyg4 = _mm_rows_bwd_a(dzm1, wg["hgrn_w_out"], 0, "l1_hgrn_out_bwd")
    dw_hout = _mm_rows_bwd_w(yg4, dzm1, "l1_dw_hgrn_out")
    dproj4, dlb_h, dgn_h = _hgrn_bwd(proj4, lb1, hgrn_gnorm, o_raw, dyg4, states, "l1_hgrn_bwd")
    dh1 = _mm_cols_bwd_a(dproj4, wg["hgrn_w_in"], 0, group=N_CHIP, name="l1_hgrn_in_bwd", tm=512)
    dw_hin = _mm_cols_bwd_w(h1, dproj4, ns=d, tn=d, act_map=_block_map, w_map=_block_map, n_tiles=N_CHIP,
                            name="l1_dw_hgrn_in", tm=2048)
    dx2, dsc1_1, dsh1_1, dnm_1, dz_ffn0, dg2_0 = _norm_mod_bwd(x2, norm_mix[1:2], sc1_1, sh1_1, dh1, dx3,
                                                               "l1_norm_mix_bwd", branch=(ffn0[4], g2_0))

    tags_1 = [("hgrn_w_in", 0), ("hgrn_w_out", 0), ("ffn_w1", 1), ("ffn_w3", 1), ("ffn_w2", 1)]
    grads_1 = [dw_hin, dw_hout, dw1_1, dw3_1, dw2_1]
    dx1, (dw1_0, dw3_0, dw2_0), dmod2_0, dnf_0, recv_1, dzm0, dg1_0 = ffn_bwd(
        0, dz_ffn0, dg2_0, dx2, x1, sc2_0, sh2_0, ffn0, (y0, g1_0), halves=grads_1)
    tags_0 = [("ffn_w1", 0), ("ffn_w3", 0), ("ffn_w2", 0)]
    grads_0 = [dw1_0, dw3_0, dw2_0]
    do4, *recv_0 = _mm_rows_bwd_a(dzm0, wg["attn_w_out"], 0, "l0_attn_out_bwd", halves=grads_0)
    dw_aout = _mm_rows_bwd_w(o4, dzm0, "l0_dw_attn_out")
    tags_a = tags_1 + tags_0
    parts_a = rs_add(tags_1, grads_1, recv_1) + rs_add(tags_0, grads_0, recv_0)
    dqkv, dqg_h, dkg_h, dbias, *got_a = _attn_bwd(qkv9, qg, kg, bias, do4, o4, lse, "l0_attn_bwd", scatter=parts_a)
    dqkv9 = dqkv.reshape(9, SEQ, d)
    dw_qkv9 = _mm_cols_bwd_w(h0, dqkv9, ns=d, tn=d, act_map=_block_map, w_map=_block_map, n_tiles=9,
                             name="l0_dw_qkv", tm=2048, n_out=9)
    dw_qkv = _retile_cols(dw_qkv9, n_out=N_CHIP, width_out=2304, tn=256, src_map=_qkv_group_map,
                          dst_map=_qkv_chip_map, n_tiles=36, name="regroup_dw_qkv")
    tags_b = [("attn_w_qkv", 0), ("attn_w_out", 0)]
    grads_b = [dw_qkv, dw_aout]
    parts_b = rs_add(tags_b, grads_b, _rs_exchange_halves(grads_b, "rs_exchange_halves_b"))
    dh0, *got_b = _mm_cols_bwd_a(dqkv9, w_qkv9, 0, group=3, name="l0_qkv_bwd", scatter=parts_b)
    dx0, dsc1_0, dsh1_0, dnm_0 = _norm_mod_bwd(x0, norm_mix[0:1], sc1_0, sh1_0, dh0, dx1, "l0_norm_mix_bwd")
    drb8 = _relbias_bwd(dbias, jnp.asarray(_bias_tables()), "rel_bias_bwd")

    small = _pack_rows([
        dsh1_0, dsc1_0, dg1_0, *dmod2_0, dsh1_1, dsc1_1, dg1_1, *dmod2_1,
        dnm_0, dnm_1, dnf_0, dnf_1,
        jnp.transpose(dqg_h, (1, 0, 2, 3)), jnp.transpose(dkg_h, (1, 0, 2, 3)), dgn_h, dlb_h, drb8])
    small_all = _small_allgather(small, "gather_small")
    main, gains, dlbnd, rbt = _small_totals(small_all, hgrn_lower_bounds.reshape(DEPTH, 8, 128), "small_totals")
    ng = len(GROUPS)
    grads = {
        "ada_b": main[_R_DMOD:_R_NMIX].reshape(DEPTH, 6 * d),
        "norm_mix": main[_R_NMIX:_R_NFFN].reshape(DEPTH, d),
        "norm_ffn": main[_R_NFFN:_R_QG].reshape(DEPTH, d),
        "attn_q_gain": gains[0:ng].reshape(1, ng, HEAD_DIM),
        "attn_k_gain": gains[ng:2 * ng].reshape(1, ng, HEAD_DIM),
        "hgrn_gnorm": gains[2 * ng:2 * ng + 1],
        "hgrn_lower_bounds": dlbnd.reshape(DEPTH, d),
        "rel_bias": jnp.transpose(rbt[:, :ng * NUM_BUCKETS].reshape(HEADS, ng, NUM_BUCKETS), (2, 1, 0))
                       .reshape(NUM_BUCKETS, ng * HEADS),
    }
    dmod_all = small_all[:, _R_DMOD:_R_NMIX].reshape(N_DEV, DEPTH, 6 * d)
    dmod_cols = jnp.transpose(lax.dynamic_slice(dmod_all, (0, 0, chip * ADA_SHARD), (N_DEV, DEPTH, ADA_SHARD)),
                              (1, 0, 2))
    grad_ada_w = _ada_bwd(c_all, dmod_cols, "ada_bwd")

    tags = tags_a + tags_b
    halves = [_rs_sum4(place, p, r, f"rs_sum_{k}_{layer}")
              for (k, layer), p, r in zip(tags, parts_a + parts_b, list(got_a) + list(got_b))]
    full = dict(zip(tags, _rs_join_halves(halves, "rs_join_halves")))

    out_g, out_d, out_m, out_v = {}, {}, {}, {}
    for k in big_names:
        gs = [full[(k, layer)] for layer in range(weights[k].shape[0])]
        out_g[k], out_d[k], out_m[k], out_v[k] = _adamw(weights[k], gs, mom1[k], mom2[k], "adamw_" + k)
    shp = (1, DEPTH * d, ADA_SHARD)
    res = _adamw(ada_w.reshape(shp), [grad_ada_w.reshape(shp[1:])], m_ada_w.reshape(shp), v_ada_w.reshape(shp),
                 "adamw_ada_w")
    out_g["ada_w"], out_d["ada_w"], out_m["ada_w"], out_v["ada_w"] = [r.reshape(ada_w.shape) for r in res]
    packed = [_pack_rows([src[k] for k in _SMALL_ORDER])[None] for src in (weights, grads, mom1, mom2)]
    res = _adamw(packed[0], [packed[1][0]], packed[2], packed[3], "adamw_small")
    offset = 0
    for k in _SMALL_ORDER:
        size = weights[k].size
        for dst, r in zip((out_g, out_d, out_m, out_v), res):
            dst[k] = r.reshape(-1)[offset:offset + size].reshape(weights[k].shape)
        offset += size
    for dst in (out_g, out_d, out_m, out_v):
        for k in transposed:
            dst[k] = jnp.transpose(dst[k], (0, 2, 1))

    return (loss, dx0.reshape(x.shape), *[out_g[k] for k in _WEIGHT_ORDER], *[out_d[k] for k in _WEIGHT_ORDER],
            *[out_m[k] for k in _WEIGHT_ORDER], *[out_v[k] for k in _WEIGHT_ORDER])
```

```python
import functools

import numpy as np
import jax
import jax.numpy as jnp
from jax import lax
from jax.experimental import pallas as pl
from jax.experimental.pallas import tpu as pltpu

F32 = jnp.float32
BF16 = jnp.bfloat16

D_MODEL = 1024
SEQ = 4096
N_DEV = 8
N_CHIP = 4
DEPTH = 2
HEADS = 8
HEAD_DIM = 128
GROUPS = ((128, 1), (512, 4), (2048, 16))
ATT_BLK = 128
ATT_WAYS = 4
ATT_STEPS = SEQ // ATT_BLK // ATT_WAYS
NUM_BUCKETS = 32
MAX_DISTANCE = 2048
FFN_HIDDEN = 2816
FFN_SHARD = FFN_HIDDEN // N_CHIP
HG_SUB = 16
HG_TC = 512
HG_HP = 4
RMS_EPS = 1e-6
NEG = -1e30
ATT_SCALE = HEAD_DIM ** -0.5
LOG2_E = 1.4426950408889634
ADAM_LR, ADAM_B1, ADAM_B2, ADAM_EPS, ADAM_WD, ADAM_STEP = 0.001, 0.9, 0.999, 1e-08, 0.01, 10
VMEM_LIMIT = 56 * 1024 * 1024
MESH = pl.DeviceIdType.MESH


def _pcall(body, **kw):
    return pl.pallas_call(body, **kw)


def _cparams(sem=None):
    if sem is None:
        return pltpu.CompilerParams(vmem_limit_bytes=VMEM_LIMIT)
    return pltpu.CompilerParams(dimension_semantics=sem, vmem_limit_bytes=VMEM_LIMIT)


def _sds(shape, dtype):
    return jax.ShapeDtypeStruct(shape, dtype)


def _dot(a, b):
    return jnp.dot(a, b, preferred_element_type=F32)


def _dot_nt(a, b):
    return lax.dot_general(a, b, (((1,), (1,)), ((), ())), preferred_element_type=F32)


def _dot_tn(a, b):
    return lax.dot_general(a, b, (((0,), (0,)), ((), ())), preferred_element_type=F32)


def _sigmoid(x):
    return 1.0 / (1.0 + jnp.exp(-x))


def _silu(x):
    return x * _sigmoid(x)


def _dsilu(x):
    s = _sigmoid(x)
    return s * (1.0 + x * (1.0 - s))


def _norm_mod(x, gain, sc, sh, name):
    tm = 512

    def body(x_ref, g_ref, sc_ref, sh_ref, h_ref):
        xv = x_ref[...]
        rs = lax.rsqrt(jnp.mean(xv * xv, axis=-1, keepdims=True) + RMS_EPS)
        h_ref[...] = ((xv * rs * g_ref[...]) * (1.0 + sc_ref[...]) + sh_ref[...]).astype(BF16)

    vec = pl.BlockSpec((1, D_MODEL), lambda i: (0, 0))
    return _pcall(
        body, name=name, grid=(SEQ // tm,),
        in_specs=[pl.BlockSpec((tm, D_MODEL), lambda i: (i, 0)), vec, vec, vec],
        out_specs=pl.BlockSpec((tm, D_MODEL), lambda i: (i, 0)),
        out_shape=_sds((SEQ, D_MODEL), BF16),
        compiler_params=_cparams(("parallel",)),
    )(x, gain, sc, sh)


def _gated_branch_bwd(dx, z_ref, gate_ref, dz_ref, dgate_ref):
    dz_ref[...] = (dx * gate_ref[...]).astype(BF16)
    dgate_ref[...] += jnp.sum(dx * z_ref[...], axis=0, keepdims=True)


def _norm_mod_bwd(x, gain, sc, sh, dh, dres, name, branch=None):
    tm = 512
    n_b = 2 if branch else 0

    def body(*refs):
        x_ref, g_ref, sc_ref, sh_ref, dh_ref, dres_ref = refs[:6]
        dx_ref, dsc_ref, dsh_ref, dg_ref = refs[6 + n_b:10 + n_b]

        @pl.when(pl.program_id(0) == 0)
        def _():
            dsc_ref[...] = jnp.zeros_like(dsc_ref)
            dsh_ref[...] = jnp.zeros_like(dsh_ref)
            dg_ref[...] = jnp.zeros_like(dg_ref)
            if branch:
                refs[11 + n_b][...] = jnp.zeros_like(refs[11 + n_b])

        xv = x_ref[...]
        dhv = dh_ref[...]
        rs = lax.rsqrt(jnp.mean(xv * xv, axis=-1, keepdims=True) + RMS_EPS)
        xh = xv * rs
        dsc_ref[...] += jnp.sum(dhv * (xh * g_ref[...]), axis=0, keepdims=True)
        dsh_ref[...] += jnp.sum(dhv, axis=0, keepdims=True)
        dhn = dhv * (1.0 + sc_ref[...])
        dg_ref[...] += jnp.sum(dhn * xh, axis=0, keepdims=True)
        dxh = dhn * g_ref[...]
        dx = dres_ref[...] + rs * (dxh - xh * jnp.mean(dxh * xh, axis=-1, keepdims=True))
        dx_ref[...] = dx
        if branch:
            _gated_branch_bwd(dx, refs[6], refs[7], refs[10 + n_b], refs[11 + n_b])

    vec = pl.BlockSpec((1, D_MODEL), lambda i: (0, 0))
    big = pl.BlockSpec((tm, D_MODEL), lambda i: (i, 0))
    return _pcall(
        body, name=name, grid=(SEQ // tm,),
        in_specs=[big, vec, vec, vec, big, big] + ([big, vec] if branch else []),
        out_specs=[big, vec, vec, vec] + ([big, vec] if branch else []),
        out_shape=[_sds((SEQ, D_MODEL), F32)] + [_sds((1, D_MODEL), F32)] * 3
        + ([_sds((SEQ, D_MODEL), BF16), _sds((1, D_MODEL), F32)] if branch else []),
        compiler_params=_cparams(("arbitrary",)),
    )(x, gain, sc, sh, dh, dres, *(branch or ()))


def _mm_cols(a, wg, layer, *, n_blocks, width, tn, act_map, w_map, out_dtype, name, tm=1024):
    k = a.shape[1]
    n_tiles = n_blocks * width // tn

    def body(a_ref, w_ref, o_ref):
        o_ref[...] = _dot(a_ref[...], w_ref[...]).astype(o_ref.dtype)

    return _pcall(
        body, name=name, grid=(SEQ // tm, n_tiles),
        in_specs=[pl.BlockSpec((tm, k), lambda i, t: (i, 0)),
                  pl.BlockSpec((None, None, k, tn), lambda i, t: (w_map(t)[0], layer, 0, w_map(t)[1]))],
        out_specs=pl.BlockSpec((None, tm, tn), lambda i, t: (act_map(t)[0], i, act_map(t)[1])),
        out_shape=_sds((n_blocks, SEQ, width), out_dtype),
        compiler_params=_cparams(("parallel", "arbitrary")),
    )(a, wg)


def _mm_cols_bwd_a(dout, wg, layer, *, group, name, tm=1024, scatter=()):
    n_blocks, _, width = dout.shape
    k = wg.shape[2]
    n_s = len(scatter)
    n_rows = SEQ // tm
    n_steps = n_blocks // group

    def body(*refs):
        d_ref, w_ref = refs[:2]
        o_ref = refs[2 + n_s]
        if n_s:
            comm_start, comm_wait = _rs_chips(refs[2:2 + n_s], refs[3 + n_s:3 + 2 * n_s], *refs[3 + 2 * n_s:])
            pl.when((pl.program_id(0) == 0) & (pl.program_id(1) == 0))(comm_start)
        acc = _dot_nt(d_ref[0], w_ref[0])
        for b in range(1, group):
            acc += _dot_nt(d_ref[b], w_ref[b])
        if n_steps == 1:
            o_ref[...] = acc
        else:
            @pl.when(pl.program_id(1) == 0)
            def _():
                o_ref[...] = acc

            @pl.when(pl.program_id(1) > 0)
            def _():
                o_ref[...] += acc
        if n_s:
            pl.when((pl.program_id(0) == n_rows - 1) & (pl.program_id(1) == n_steps - 1))(comm_wait)

    sem = pltpu.SemaphoreType.DMA((max(n_s, 1), 3))
    res = _pcall(
        body, name=name, grid=(n_rows, n_steps),
        in_specs=[pl.BlockSpec((group, tm, width), lambda i, t: (t, i, 0)),
                  pl.BlockSpec((group, None, k, width), lambda i, t: (t, layer, 0, 0))] + [_ANY] * n_s,
        out_specs=[pl.BlockSpec((tm, k), lambda i, t: (i, 0))] + [_ANY] * n_s,
        out_shape=[_sds((SEQ, k), F32)] + _rs_chips_shapes(scatter),
        scratch_shapes=[sem, sem] if n_s else [],
        compiler_params=_cparams(("arbitrary", "arbitrary") if n_s else ("parallel", "arbitrary")),
    )(dout, wg, *scatter)
    return res if n_s else res[0]


def _mm_cols_bwd_w(a, dout, *, ns, tn, act_map, w_map, n_tiles, name, tm=1024, n_out=N_CHIP):
    k = a.shape[1]

    def body(a_ref, d_ref, o_ref):
        @pl.when(pl.program_id(1) == 0)
        def _():
            o_ref[...] = jnp.zeros_like(o_ref)

        o_ref[...] += _dot_tn(a_ref[...], d_ref[...])

    return _pcall(
        body, name=name, grid=(n_tiles, SEQ // tm),
        in_specs=[pl.BlockSpec((tm, k), lambda t, i: (i, 0)),
                  pl.BlockSpec((None, tm, tn), lambda t, i: (act_map(t)[0], i, act_map(t)[1]))],
        out_specs=pl.BlockSpec((None, k, tn), lambda t, i: (w_map(t)[0], 0, w_map(t)[1])),
        out_shape=_sds((n_out, k, ns), F32),
        compiler_params=_cparams(("parallel", "arbitrary")),
    )(a, dout)


def _retile_cols(src, *, n_out, width_out, tn, src_map, dst_map, n_tiles, name):
    k = src.shape[1]

    def body(s_ref, o_ref):
        o_ref[...] = s_ref[...]

    return _pcall(
        body, name=name, grid=(n_tiles,),
        in_specs=[pl.BlockSpec((None, k, tn), lambda t: (src_map(t)[0], 0, src_map(t)[1]))],
        out_specs=pl.BlockSpec((None, k, tn), lambda t: (dst_map(t)[0], 0, dst_map(t)[1])),
        out_shape=_sds((n_out, k, width_out), src.dtype),
        compiler_params=_cparams(("parallel",)),
    )(src)


def _mm_rows(a4, wg, layer, x, gate, name, tm=512):
    ks = a4.shape[2]
    n = wg.shape[3]

    def body(a_ref, w_ref, x_ref, g_ref, z_ref, xn_ref):
        z = _dot(a_ref[0], w_ref[0])
        for s in range(1, N_CHIP):
            z += _dot(a_ref[s], w_ref[s])
        z_ref[...] = z.astype(BF16)
        xn_ref[...] = x_ref[...] + g_ref[...] * z

    big = pl.BlockSpec((tm, n), lambda i: (i, 0))
    return _pcall(
        body, name=name, grid=(SEQ // tm,),
        in_specs=[pl.BlockSpec((N_CHIP, tm, ks), lambda i: (0, i, 0)),
                  pl.BlockSpec((N_CHIP, None, ks, n), lambda i: (0, layer, 0, 0)),
                  big, pl.BlockSpec((1, n), lambda i: (0, 0))],
        out_specs=[big, big],
        out_shape=[_sds((SEQ, n), BF16), _sds((SEQ, n), F32)],
        compiler_params=_cparams(("parallel",)),
    )(a4, wg, x, gate)


def _mm_rows_bwd_a(dz, wg, layer, name, tm=1024, halves=()):
    ks, n = wg.shape[2], wg.shape[3]
    n_h = len(halves)
    n_rows = SEQ // tm

    def body(*refs):
        dz_ref, w_ref = refs[:2]
        o_ref = refs[2 + n_h]
        if n_h:
            comm_start, comm_wait = _rs_halves(refs[2:2 + n_h], refs[3 + n_h:3 + 2 * n_h], *refs[3 + 2 * n_h:])
            pl.when((pl.program_id(0) == 0) & (pl.program_id(1) == 0))(comm_start)
        o_ref[...] = _dot_nt(dz_ref[...], w_ref[...])
        if n_h:
            pl.when((pl.program_id(0) == n_rows - 1) & (pl.program_id(1) == N_CHIP - 1))(comm_wait)

    sem = pltpu.SemaphoreType.DMA((max(n_h, 1),))
    res = _pcall(
        body, name=name, grid=(n_rows, N_CHIP),
        in_specs=[pl.BlockSpec((tm, n), lambda i, s: (i, 0)),
                  pl.BlockSpec((None, None, ks, n), lambda i, s: (s, layer, 0, 0))] + [_ANY] * n_h,
        out_specs=[pl.BlockSpec((None, tm, ks), lambda i, s: (s, i, 0))] + [_ANY] * n_h,
        out_shape=[_sds((N_CHIP, SEQ, ks), F32)] + _rs_halves_shapes(halves),
        scratch_shapes=[sem, sem] if n_h else [],
        compiler_params=_cparams(("arbitrary", "arbitrary") if n_h else ("parallel", "arbitrary")),
    )(dz, wg, *halves)
    return res if n_h else res[0]


def _mm_rows_bwd_w(a4, dz, name, tm=2048):
    ks = a4.shape[2]
    n = dz.shape[1]

    def body(a_ref, dz_ref, o_ref):
        @pl.when(pl.program_id(1) == 0)
        def _():
            o_ref[...] = jnp.zeros_like(o_ref)

        o_ref[...] += _dot_tn(a_ref[...], dz_ref[...])

    return _pcall(
        body, name=name, grid=(N_CHIP, SEQ // tm),
        in_specs=[pl.BlockSpec((None, tm, ks), lambda s, i: (s, i, 0)),
                  pl.BlockSpec((tm, n), lambda s, i: (i, 0))],
        out_specs=pl.BlockSpec((None, ks, n), lambda s, i: (s, 0, 0)),
        out_shape=_sds((N_CHIP, ks, n), F32),
        compiler_params=_cparams(("parallel", "arbitrary")),
    )(a4, dz)


def _ffn_up(h, w1g, w3g, layer, name, tm=1024):
    def body(h_ref, w1_ref, w3_ref, a1_ref, a3_ref, u_ref):
        hv = h_ref[...]
        a1 = _dot_nt(hv, w1_ref[...])
        a3 = _dot_nt(hv, w3_ref[...])
        a1_ref[...] = a1.astype(BF16)
        a3_ref[...] = a3.astype(BF16)
        u_ref[...] = (_silu(a1) * a3).astype(BF16)

    wspec = pl.BlockSpec((None, None, FFN_SHARD, D_MODEL), lambda i, s: (s, layer, 0, 0))
    ospec = pl.BlockSpec((None, tm, FFN_SHARD), lambda i, s: (s, i, 0))
    shp = (N_CHIP, SEQ, FFN_SHARD)
    return _pcall(
        body, name=name, grid=(SEQ // tm, N_CHIP),
        in_specs=[pl.BlockSpec((tm, D_MODEL), lambda i, s: (i, 0)), wspec, wspec],
        out_specs=[ospec, ospec, ospec],
        out_shape=[_sds(shp, BF16), _sds(shp, BF16), _sds(shp, BF16)],
        compiler_params=_cparams(("parallel", "arbitrary")),
    )(h, w1g, w3g)


def _ffn_up_bwd(da1, da3, w1g, w3g, layer, name, tm=512):
    def body(d1_ref, d3_ref, w1_ref, w3_ref, o_ref):
        acc = _dot(d1_ref[0], w1_ref[0]) + _dot(d3_ref[0], w3_ref[0])
        for s in range(1, N_CHIP):
            acc += _dot(d1_ref[s], w1_ref[s]) + _dot(d3_ref[s], w3_ref[s])
        o_ref[...] = acc

    wspec = pl.BlockSpec((N_CHIP, None, FFN_SHARD, D_MODEL), lambda i: (0, layer, 0, 0))
    dspec = pl.BlockSpec((N_CHIP, tm, FFN_SHARD), lambda i: (0, i, 0))
    return _pcall(
        body, name=name, grid=(SEQ // tm,),
        in_specs=[dspec, dspec, wspec, wspec],
        out_specs=pl.BlockSpec((tm, D_MODEL), lambda i: (i, 0)),
        out_shape=_sds((SEQ, D_MODEL), F32),
        compiler_params=_cparams(("parallel",)),
    )(da1, da3, w1g, w3g)


def _ffn_down_bwd(dz, w2g, layer, a1, a3, name, tm=1024, halves=()):
    n_h = len(halves)
    n_rows = SEQ // tm

    def body(*refs):
        dz_ref, w_ref, a1_ref, a3_ref = refs[:4]
        da1_ref, da3_ref = refs[4 + n_h:6 + n_h]
        if n_h:
            comm_start, comm_wait = _rs_halves(refs[4:4 + n_h], refs[6 + n_h:6 + 2 * n_h], *refs[6 + 2 * n_h:])
            pl.when((pl.program_id(0) == 0) & (pl.program_id(1) == 0))(comm_start)
        du = _dot_nt(dz_ref[...], w_ref[...])
        a1 = a1_ref[...].astype(F32)
        da1_ref[...] = (du * a3_ref[...].astype(F32) * _dsilu(a1)).astype(BF16)
        da3_ref[...] = (du * _silu(a1)).astype(BF16)
        if n_h:
            pl.when((pl.program_id(0) == n_rows - 1) & (pl.program_id(1) == N_CHIP - 1))(comm_wait)

    blk = pl.BlockSpec((None, tm, FFN_SHARD), lambda i, s: (s, i, 0))
    shp = (N_CHIP, SEQ, FFN_SHARD)
    sem = pltpu.SemaphoreType.DMA((max(n_h, 1),))
    return _pcall(
        body, name=name, grid=(n_rows, N_CHIP),
        in_specs=[pl.BlockSpec((tm, D_MODEL), lambda i, s: (i, 0)),
                  pl.BlockSpec((None, None, FFN_SHARD, D_MODEL), lambda i, s: (s, layer, 0, 0)),
                  blk, blk] + [_ANY] * n_h,
        out_specs=[blk, blk] + [_ANY] * n_h,
        out_shape=[_sds(shp, BF16), _sds(shp, BF16)] + _rs_halves_shapes(halves),
        scratch_shapes=[sem, sem] if n_h else [],
        compiler_params=_cparams(("arbitrary", "arbitrary") if n_h else ("parallel", "arbitrary")),
    )(dz, w2g, a1, a3, *halves)


def _loss_head(y, target, z, gate, name):
    tm = 512
    n_steps = SEQ // tm

    def body(y_ref, t_ref, z_ref, gate_ref, dy_ref, l_ref, dz_ref, dgate_ref, acc_ref):
        @pl.when(pl.program_id(0) == 0)
        def _():
            acc_ref[...] = jnp.zeros_like(acc_ref)
            dgate_ref[...] = jnp.zeros_like(dgate_ref)

        err = y_ref[...] - t_ref[...]
        dy = err * (1.0 / D_MODEL)
        dy_ref[...] = dy
        acc_ref[...] += jnp.sum(jnp.mean(err * err, axis=-1, keepdims=True), axis=0, keepdims=True)
        _gated_branch_bwd(dy, z_ref, gate_ref, dz_ref, dgate_ref)

        @pl.when(pl.program_id(0) == n_steps - 1)
        def _():
            l_ref[...] = 0.5 * acc_ref[...]

    big = pl.BlockSpec((tm, D_MODEL), lambda i: (i, 0))
    vec = pl.BlockSpec((1, D_MODEL), lambda i: (0, 0))
    return _pcall(
        body, name=name, grid=(n_steps,),
        in_specs=[big, big, big, vec],
        out_specs=[big, pl.BlockSpec((1, 1), lambda i: (0, 0)), big, vec],
        out_shape=[_sds((SEQ, D_MODEL), F32), _sds((1, 1), F32), _sds((SEQ, D_MODEL), BF16),
                   _sds((1, D_MODEL), F32)],
        scratch_shapes=[pltpu.VMEM((1, 1), F32)],
        compiler_params=_cparams(("arbitrary",)),
    )(y, target, z, gate)


def _attn_rows(base, d):
    if d == 1:
        return pl.ds(pl.multiple_of(base, ATT_BLK), ATT_BLK)
    return pl.ds(base, ATT_BLK, stride=d)


def _attn_block_index(i, d):
    nb = SEQ // (ATT_BLK * d)
    r = i // nb
    n = i % nb
    base = r + n * (ATT_BLK * d)
    pbase = jnp.maximum(base - ATT_BLK * d, r)
    return n, _attn_rows(base, d), _attn_rows(pbase, d)


def _attn_two_blocks(ref, prow, rows):
    return jnp.concatenate([ref[prow, :].astype(BF16), ref[rows, :].astype(BF16)], axis=0)


def _attn_block_bias(b_ref, n):
    b = b_ref[...]
    prev_half = lax.broadcasted_iota(jnp.int32, b.shape, 1) < ATT_BLK
    return jnp.where(prev_half & (n == 0), NEG, b)


def _qk_normed(x):
    rs = lax.rsqrt(jnp.mean(x * x, axis=-1, keepdims=True) + RMS_EPS)
    return x * rs, rs


def _attn_fwd(qkv9, qgain, kgain, bias, name, gather=()):
    n_g = len(gather)

    def body(*refs):
        q_ref, k_ref, v_ref, qg_ref, kg_ref, b_ref = refs[:6]
        o_ref, lse_ref = refs[6 + n_g:8 + n_g]
        qn_s, kn_s, acc_s, m_s, l_s = refs[8 + 2 * n_g:13 + 2 * n_g]
        g = pl.program_id(1)
        if n_g:
            comm_start, comm_wait = _gather_ici(refs[8 + n_g:8 + 2 * n_g], *refs[13 + 2 * n_g:])
            pl.when((pl.program_id(0) == 0) & (g == 0))(comm_start)

        @pl.when(g == 0)
        def _():
            m_s[...] = jnp.full_like(m_s, NEG)
            l_s[...] = jnp.zeros_like(l_s)
            acc_s[...] = jnp.zeros_like(acc_s)

        qn_s[...] = _qk_normed(q_ref[...])[0] * qg_ref[...]
        kn_s[...] = _qk_normed(k_ref[...])[0] * kg_ref[...]

        for gi, (_, d) in enumerate(GROUPS):
            @pl.when(g == gi)
            def _(d=d):
                def block(n, qb, kk, vv, m_old, l_old, acc_old):
                    s = _dot_nt(qb, kk) * ATT_SCALE + _attn_block_bias(b_ref, n)
                    m_new = jnp.maximum(m_old, jnp.max(s, axis=-1, keepdims=True))
                    alpha = jnp.exp(m_old - m_new)
                    p = jnp.exp(s - m_new)
                    l_new = alpha * l_old + jnp.sum(p, axis=-1, keepdims=True)
                    acc_new = alpha * acc_old + _dot(p.astype(BF16), vv)
                    return m_new, l_new, acc_new

                def it(i, carry):
                    where, loaded = [], []
                    for way in range(ATT_WAYS):
                        n, rows, prow = _attn_block_index(i + way * ATT_STEPS, d)
                        where.append(rows)
                        loaded.append((n, qn_s[rows, :].astype(BF16), _attn_two_blocks(kn_s, prow, rows),
                                       _attn_two_blocks(v_ref, prow, rows), m_s[rows, :], l_s[rows, :],
                                       acc_s[rows, :]))
                    results = [block(*vals) for vals in loaded]
                    for rows, (m_new, l_new, acc_new) in zip(where, results):
                        m_s[rows, :] = m_new
                        l_s[rows, :] = l_new
                        acc_s[rows, :] = acc_new
                    return carry

                lax.fori_loop(0, ATT_STEPS, it, 0)

        @pl.when(g == len(GROUPS) - 1)
        def _():
            o_ref[...] = (acc_s[...] / l_s[...]).astype(BF16)
            lse_ref[...] = m_s[...] + jnp.log(l_s[...])

        if n_g:
            pl.when((pl.program_id(0) == HEADS - 1) & (g == len(GROUPS) - 1))(comm_wait)

    def col(j):
        return pl.BlockSpec((None, SEQ, HEAD_DIM), lambda h, g: (g * 3 + j, 0, h))

    gspec = pl.BlockSpec((None, 1, HEAD_DIM), lambda h, g: (g, 0, 0))
    sem = pltpu.SemaphoreType.DMA((max(n_g, 1), 3))
    return _pcall(
        body, name=name, grid=(HEADS, len(GROUPS)),
        in_specs=[col(0), col(1), col(2), gspec, gspec,
                  pl.BlockSpec((None, None, ATT_BLK, 2 * ATT_BLK), lambda h, g: (g, h, 0, 0))] + [_ANY] * n_g,
        out_specs=[pl.BlockSpec((None, SEQ, HEAD_DIM), lambda h, g: (h // 2, 0, h % 2)),
                   pl.BlockSpec((None, SEQ, 1), lambda h, g: (h, 0, 0))] + [_ANY] * n_g,
        out_shape=[_sds((N_CHIP, SEQ, 2 * HEAD_DIM), BF16), _sds((HEADS, SEQ, 1), F32)]
        + [_sds(s.shape, s.dtype) for s in gather],
        input_output_aliases={6 + a: 2 + a for a in range(n_g)},
        scratch_shapes=[pltpu.VMEM((SEQ, HEAD_DIM), F32)] * 3 + [pltpu.VMEM((SEQ, 1), F32)] * 2
        + ([sem, sem] if n_g else []),
        compiler_params=_cparams(("arbitrary", "arbitrary")),
    )(qkv9, qkv9, qkv9, qgain, kgain, bias, *gather)


def _attn_bwd(qkv9, qgain, kgain, bias, do4, o4, lse, name, scatter=()):
    n_s = len(scatter)

    def body(*refs):
        q_ref, k_ref, v_ref, qg_ref, kg_ref, b_ref, do_ref, o_ref, lse_ref = refs[:9]
        dqkv_ref, dqg_ref, dkg_ref, db_ref = refs[9 + n_s:13 + n_s]
        qn_s, kn_s, dq_s, dk_s, dv_s, dl_s = refs[13 + 2 * n_s:19 + 2 * n_s]
        g = pl.program_id(1)
        if n_s:
            comm_start, comm_wait = _rs_chips(refs[9:9 + n_s], refs[13 + n_s:13 + 2 * n_s], *refs[19 + 2 * n_s:])
            pl.when((pl.program_id(0) == 0) & (g == 0))(comm_start)
        qh, rq = _qk_normed(q_ref[...])
        kh, rk = _qk_normed(k_ref[...])
        qn_s[...] = qh * qg_ref[...]
        kn_s[...] = kh * kg_ref[...]
        @pl.when(g == 0)
        def _():
            dl_s[...] = jnp.sum(do_ref[...] * o_ref[...].astype(F32), axis=-1, keepdims=True)

        dk_s[...] = jnp.zeros_like(dk_s)
        dv_s[...] = jnp.zeros_like(dv_s)
        db_ref[...] = jnp.zeros_like(db_ref)

        for gi, (_, d) in enumerate(GROUPS):
            @pl.when(g == gi)
            def _(d=d):
                def block(n, qb, kk, vv, dob, lse_b, dl):
                    s = _dot_nt(qb, kk) * ATT_SCALE + _attn_block_bias(b_ref, n)
                    p = jnp.exp(s - lse_b)
                    ds = p * (_dot_nt(dob, vv) - dl)
                    ds16 = ds.astype(BF16)
                    return (ds, _dot(ds16, kk) * ATT_SCALE, _dot_tn(ds16, qb) * ATT_SCALE,
                            _dot_tn(p.astype(BF16), dob))

                def it(i, carry):
                    where, loaded, old = [], [], []
                    for way in range(ATT_WAYS):
                        n, rows, prow = _attn_block_index(i + way * ATT_STEPS, d)
                        where.append((rows, prow))
                        loaded.append((n, qn_s[rows, :].astype(BF16), _attn_two_blocks(kn_s, prow, rows),
                                       _attn_two_blocks(v_ref, prow, rows), do_ref[rows, :].astype(BF16),
                                       lse_ref[rows, :], dl_s[rows, :]))
                        old.append((dk_s[rows, :], dk_s[prow, :], dv_s[rows, :], dv_s[prow, :]))
                    results = [block(*vals) for vals in loaded]
                    db_ref[...] += functools.reduce(lambda a, b: a + b, [r[0] for r in results])
                    for (rows, prow), (dk_c, dk_p, dv_c, dv_p), (_, dq, dkk, dvv) in zip(where, old, results):
                        dq_s[rows, :] = dq
                        dk_s[prow, :] = dk_p + dkk[:ATT_BLK]
                        dv_s[prow, :] = dv_p + dvv[:ATT_BLK]
                        dk_s[rows, :] = dk_c + dkk[ATT_BLK:]
                        dv_s[rows, :] = dv_c + dvv[ATT_BLK:]
                    return carry

                lax.fori_loop(0, ATT_STEPS, it, 0)

        def norm_bwd(dn, xh, rs, gain):
            dgain = jnp.sum(dn * xh, axis=0, keepdims=True)
            dxh = dn * gain
            return rs * (dxh - xh * jnp.mean(dxh * xh, axis=-1, keepdims=True)), dgain

        dq, dqg = norm_bwd(dq_s[...], qh, rq, qg_ref[...])
        dk, dkg = norm_bwd(dk_s[...], kh, rk, kg_ref[...])
        dqkv_ref[0] = dq.astype(BF16)
        dqkv_ref[1] = dk.astype(BF16)
        dqkv_ref[2] = dv_s[...].astype(BF16)
        dqg_ref[...] = dqg
        dkg_ref[...] = dkg
        if n_s:
            pl.when((pl.program_id(0) == HEADS - 1) & (g == len(GROUPS) - 1))(comm_wait)

    def col(j):
        return pl.BlockSpec((None, SEQ, HEAD_DIM), lambda h, g: (g * 3 + j, 0, h))

    gspec = pl.BlockSpec((None, 1, HEAD_DIM), lambda h, g: (g, 0, 0))
    bspec = pl.BlockSpec((None, None, ATT_BLK, 2 * ATT_BLK), lambda h, g: (g, h, 0, 0))
    hcol = pl.BlockSpec((None, SEQ, HEAD_DIM), lambda h, g: (h // 2, 0, h % 2))
    dgspec = pl.BlockSpec((None, None, 1, HEAD_DIM), lambda h, g: (h, g, 0, 0))
    ng = len(GROUPS)
    sem = pltpu.SemaphoreType.DMA((max(n_s, 1), 3))
    return _pcall(
        body, name=name, grid=(HEADS, ng),
        in_specs=[col(0), col(1), col(2), gspec, gspec, bspec, hcol, hcol,
                  pl.BlockSpec((None, SEQ, 1), lambda h, g: (h, 0, 0))] + [_ANY] * n_s,
        out_specs=[pl.BlockSpec((None, 3, SEQ, HEAD_DIM), lambda h, g: (g, 0, 0, h)), dgspec, dgspec, bspec]
        + [_ANY] * n_s,
        out_shape=[_sds((ng, 3, SEQ, D_MODEL), BF16), _sds((HEADS, ng, 1, HEAD_DIM), F32),
                   _sds((HEADS, ng, 1, HEAD_DIM), F32), _sds((ng, HEADS, ATT_BLK, 2 * ATT_BLK), F32)]
        + _rs_chips_shapes(scatter),
        scratch_shapes=[pltpu.VMEM((SEQ, HEAD_DIM), F32)] * 5 + [pltpu.VMEM((SEQ, 1), F32)]
        + ([sem, sem] if n_s else []),
        compiler_params=_cparams(("arbitrary", "arbitrary")),
    )(qkv9, qkv9, qkv9, qgain, kgain, bias, do4, o4, lse, *scatter)


def _relbias_bwd(dbias, bucket_idx, name):
    ng = len(GROUPS)

    def body(db_ref, idx_ref, o_ref):
        lane = lax.broadcasted_iota(jnp.int32, (HEADS, 128), 1)
        acc = jnp.zeros((HEADS, 128), F32)
        for g in range(ng):
            dbg = db_ref[g]
            idx = idx_ref[g]
            for b in range(NUM_BUCKETS):
                sel = jnp.where((idx == b)[None], dbg, 0.0)
                part = jnp.sum(sel, axis=1)
                val = jnp.sum(part, axis=-1, keepdims=True)
                acc = jnp.where(lane == g * NUM_BUCKETS + b, val, acc)
        o_ref[...] = acc

    return _pcall(body, name=name, out_shape=_sds((HEADS, 128), F32), compiler_params=_cparams())(dbias, bucket_idx)


def _scan16(x, reverse=False):
    row = lax.broadcasted_iota(jnp.int32, x.shape, 0)
    for sh in (1, 2, 4, 8):
        if reverse:
            x = x + jnp.where(row < HG_SUB - sh, pltpu.roll(x, HG_SUB - sh, 0), 0.0)
        else:
            x = x + jnp.where(row >= sh, pltpu.roll(x, sh, 0), 0.0)
    return x


def _hgrn_gates(qr, fr, lbv):
    q = _silu(qr)
    sig = _sigmoid(fr)
    fg = lbv + (1.0 - lbv) * sig
    lf = jnp.log(fg) * LOG2_E
    gcum = _scan16(lf)
    glast = jnp.sum(lf, axis=0, keepdims=True)
    return q, sig, fg, 1.0 - fg, gcum, glast


def _hgrn_intra(q, k, gcum, tri):
    e = jnp.exp2(jnp.where(tri, gcum[:, None, :] - gcum[None, :, :], NEG))
    a = jnp.sum(q[:, None, :] * k[None, :, :] * e, axis=-1, keepdims=True)
    return e, a


def _hgrn_fwd(proj4, lb, gain, name):
    nsub = HG_TC // HG_SUB
    wide = HG_HP * HEAD_DIM

    def body(p_ref, lb_ref, gn_ref, o_ref, y_ref, st_ref, state_s):
        @pl.when(pl.program_id(1) == 0)
        def _():
            state_s[...] = jnp.zeros_like(state_s)

        gnv = gn_ref[...]
        shp = (HG_SUB, HG_SUB, HEAD_DIM)
        tri = lax.broadcasted_iota(jnp.int32, shp, 0) >= lax.broadcasted_iota(jnp.int32, shp, 1)

        def head(qr, fr, vv, gr, lbv, st):
            q, _, _, k, gcum, glast = _hgrn_gates(qr, fr, lbv)
            _, a = _hgrn_intra(q, k, gcum, tri)
            o = jnp.sum(a * vv[None, :, :], axis=1) + _dot_nt((q * jnp.exp2(gcum)).astype(BF16), st.astype(BF16))
            kg = k * jnp.exp2(glast - gcum)
            st_new = st * jnp.exp2(glast) + _dot_tn(vv.astype(BF16), kg.astype(BF16))
            rs = lax.rsqrt(jnp.mean(o * o, axis=-1, keepdims=True) + RMS_EPS)
            return o, (o * rs * gnv * _silu(gr)).astype(BF16), st_new

        def it(i, carry):
            rows = pl.ds(pl.multiple_of(i * HG_SUB, HG_SUB), HG_SUB)
            loaded = []
            for hh in range(HG_HP):
                lanes = pl.ds(hh * HEAD_DIM, HEAD_DIM)
                loaded.append(([p_ref[j, rows, lanes] for j in range(4)], lb_ref[:, lanes], state_s[hh]))
            results = [head(blk[0], blk[1], blk[2], blk[3], lbv, st) for blk, lbv, st in loaded]
            for hh, ((_, _, st), (o, y, st_new)) in enumerate(zip(loaded, results)):
                lanes = pl.ds(hh * HEAD_DIM, HEAD_DIM)
                st_ref[hh, i] = st.astype(BF16)
                state_s[hh] = st_new
                o_ref[rows, lanes] = o
                y_ref[hh // 2, rows, pl.ds((hh % 2) * HEAD_DIM, HEAD_DIM)] = y
            return carry

        lax.fori_loop(0, nsub, it, 0)

    return _pcall(
        body, name=name, grid=(HEADS // HG_HP, SEQ // HG_TC),
        in_specs=[pl.BlockSpec((4, HG_TC, wide), lambda h, j: (0, j, h)),
                  pl.BlockSpec((1, wide), lambda h, j: (0, h)),
                  pl.BlockSpec((1, HEAD_DIM), lambda h, j: (0, 0))],
        out_specs=[pl.BlockSpec((HG_TC, wide), lambda h, j: (j, h)),
                   pl.BlockSpec((HG_HP // 2, HG_TC, 2 * HEAD_DIM), lambda h, j: (h, j, 0)),
                   pl.BlockSpec((HG_HP, nsub, HEAD_DIM, HEAD_DIM), lambda h, j: (h, j, 0, 0))],
        out_shape=[_sds((SEQ, D_MODEL), F32), _sds((N_CHIP, SEQ, 2 * HEAD_DIM), BF16),
                   _sds((HEADS, SEQ // HG_SUB, HEAD_DIM, HEAD_DIM), BF16)],
        scratch_shapes=[pltpu.VMEM((HG_HP, HEAD_DIM, HEAD_DIM), F32)],
        compiler_params=_cparams(("parallel", "arbitrary")),
    )(proj4, lb, gain)


def _hgrn_bwd(proj4, lb, gain, o_raw, dy4, states, name):
    nsub = HG_TC // HG_SUB
    nt = SEQ // HG_TC
    wide = HG_HP * HEAD_DIM

    def body(p_ref, lb_ref, gn_ref, o_ref, dy_ref, st_ref, dp_ref, dlb_ref, dgn_ref, dst_s):
        @pl.when(pl.program_id(1) == 0)
        def _():
            dst_s[...] = jnp.zeros_like(dst_s)
            dlb_ref[...] = jnp.zeros_like(dlb_ref)
            dgn_ref[...] = jnp.zeros_like(dgn_ref)

        gnv = gn_ref[...]
        shp = (HG_SUB, HG_SUB, HEAD_DIM)
        tri = lax.broadcasted_iota(jnp.int32, shp, 0) >= lax.broadcasted_iota(jnp.int32, shp, 1)

        def head(qr, fr, vv, gr, o, dy, lbv, st0, dst):
            q, sig, fg, k, gcum, glast = _hgrn_gates(qr, fr, lbv)
            rs = lax.rsqrt(jnp.mean(o * o, axis=-1, keepdims=True) + RMS_EPS)
            oh = o * rs
            don = dy * _silu(gr)
            dgn = jnp.sum(don * oh, axis=0, keepdims=True)
            dgr = dy * oh * gnv * _dsilu(gr)
            doh = don * gnv
            do = rs * (doh - oh * jnp.mean(doh * oh, axis=-1, keepdims=True))
            dst16 = dst.astype(BF16)
            do16 = do.astype(BF16)
            eg = jnp.exp2(gcum)
            eb = jnp.exp2(glast - gcum)
            e, a = _hgrn_intra(q, k, gcum, tri)
            da = jnp.sum(do[:, None, :] * vv[None, :, :], axis=-1, keepdims=True)
            dae = da * e
            dq = jnp.sum(dae * k[None, :, :], axis=1) + eg * _dot(do16, st0)
            dk_state = eb * _dot(vv.astype(BF16), dst16)
            dk = jnp.sum(dae * q[:, None, :], axis=0) + dk_state
            dv = jnp.sum(a * do[:, None, :], axis=0) + _dot_nt((k * eb).astype(BF16), dst16)
            eglast = jnp.exp2(glast)
            dst_new = dst * eglast + _dot_tn(do16, (q * eg).astype(BF16))
            dglast = jnp.sum(k * dk_state, axis=0, keepdims=True) \
                + eglast * jnp.sum(dst * st0.astype(F32), axis=0, keepdims=True)
            dlf = _scan16(q * dq - k * dk, reverse=True) + dglast
            dfg = dlf / fg - dk
            dlb = jnp.sum(dfg * (1.0 - sig), axis=0, keepdims=True)
            dproj = ((dq * _dsilu(qr)).astype(BF16), (dfg * (1.0 - lbv) * sig * (1.0 - sig)).astype(BF16),
                     dv.astype(BF16), dgr.astype(BF16))
            return dproj, dst_new, dlb, dgn

        def it(ii, carry):
            i = nsub - 1 - ii
            rows = pl.ds(pl.multiple_of(i * HG_SUB, HG_SUB), HG_SUB)
            results = []
            for hh in range(HG_HP):
                lanes = pl.ds(hh * HEAD_DIM, HEAD_DIM)
                blk = [p_ref[j, rows, lanes] for j in range(4)]
                dy = dy_ref[hh // 2, rows, pl.ds((hh % 2) * HEAD_DIM, HEAD_DIM)]
                results.append(head(blk[0], blk[1], blk[2], blk[3], o_ref[rows, lanes], dy,
                                    lb_ref[:, lanes], st_ref[hh, i], dst_s[hh]))
            new_carry = []
            for hh, (dproj, dst_new, dlb, dgn) in enumerate(results):
                lanes = pl.ds(hh * HEAD_DIM, HEAD_DIM)
                dst_s[hh] = dst_new
                for j in range(4):
                    dp_ref[j, rows, lanes] = dproj[j]
                new_carry.append((carry[hh][0] + dlb, carry[hh][1] + dgn))
            return tuple(new_carry)

        zero = jnp.zeros((1, HEAD_DIM), F32)
        sums = lax.fori_loop(0, nsub, it, tuple((zero, zero) for _ in range(HG_HP)))
        for hh in range(HG_HP):
            dlb_ref[hh] += sums[hh][0]
            dgn_ref[hh] += sums[hh][1]

    vspec = pl.BlockSpec((HG_HP, 1, HEAD_DIM), lambda h, j: (h, 0, 0))
    return _pcall(
        body, name=name, grid=(HEADS // HG_HP, nt),
        in_specs=[pl.BlockSpec((4, HG_TC, wide), lambda h, j: (0, nt - 1 - j, h)),
                  pl.BlockSpec((1, wide), lambda h, j: (0, h)),
                  pl.BlockSpec((1, HEAD_DIM), lambda h, j: (0, 0)),
                  pl.BlockSpec((HG_TC, wide), lambda h, j: (nt - 1 - j, h)),
                  pl.BlockSpec((HG_HP // 2, HG_TC, 2 * HEAD_DIM), lambda h, j: (h, nt - 1 - j, 0)),
                  pl.BlockSpec((HG_HP, nsub, HEAD_DIM, HEAD_DIM), lambda h, j: (h, nt - 1 - j, 0, 0))],
        out_specs=[pl.BlockSpec((4, HG_TC, wide), lambda h, j: (0, nt - 1 - j, h)), vspec, vspec],
        out_shape=[_sds((4, SEQ, D_MODEL), BF16), _sds((HEADS, 1, HEAD_DIM), F32), _sds((HEADS, 1, HEAD_DIM), F32)],
        scratch_shapes=[pltpu.VMEM((HG_HP, HEAD_DIM, HEAD_DIM), F32)],
        compiler_params=_cparams(("parallel", "arbitrary")),
    )(proj4, lb, gain, o_raw, dy4, states)


def _t5_bucket(dist):
    n = np.asarray(dist, dtype=np.int64)
    max_exact = NUM_BUCKETS // 2
    large = max_exact + (np.log(np.maximum(n, 1) / max_exact) / np.log(MAX_DISTANCE / max_exact)
                         * (NUM_BUCKETS - max_exact)).astype(np.int64)
    large = np.minimum(large, NUM_BUCKETS - 1)
    return np.where(n < max_exact, n, large).astype(np.int32)


def _bias_tables():
    qi = np.arange(ATT_BLK)[:, None]
    ki = np.arange(2 * ATT_BLK)[None, :]
    j = ATT_BLK + qi - ki
    valid = (j >= 0) & (j <= ATT_BLK)
    return np.stack([np.where(valid, _t5_bucket(np.clip(j, 0, ATT_BLK) * d), -1) for _, d in GROUPS]).astype(np.int32)


def _attn_bias(rel_bias, name):
    idx = _bias_tables()
    ng = len(GROUPS)
    buckets = [sorted(set(idx[g][idx[g] >= 0].tolist())) for g in range(ng)]

    def body(rb_ref, idx_ref, o_ref):
        h = pl.program_id(0)
        for g in range(ng):
            ig = idx_ref[g]
            acc = jnp.full(ig.shape, NEG, F32)
            for b in buckets[g]:
                acc = jnp.where(ig == b, rb_ref[b, g * HEADS + h], acc)
            o_ref[g] = acc

    return _pcall(
        body, name=name, grid=(HEADS,),
        in_specs=[pl.BlockSpec(memory_space=pltpu.SMEM),
                  pl.BlockSpec((ng, ATT_BLK, 2 * ATT_BLK), lambda h: (0, 0, 0))],
        out_specs=pl.BlockSpec((ng, None, ATT_BLK, 2 * ATT_BLK), lambda h: (0, h, 0, 0)),
        out_shape=_sds((ng, HEADS, ATT_BLK, 2 * ATT_BLK), F32),
        compiler_params=_cparams(("parallel",)),
    )(rel_bias, jnp.asarray(idx))


ADA_SHARD = 6 * D_MODEL // N_CHIP
ADA_TN = 512


def _ada_fwd(c_all, ada_w, ada_b_cols, name):
    def body(c_ref, w_ref, b_ref, o_ref):
        ca = _silu(c_ref[...]).astype(BF16)
        o_ref[...] = _dot(ca, w_ref[...].astype(BF16)) + b_ref[...]

    return _pcall(
        body, name=name, grid=(DEPTH, ADA_SHARD // ADA_TN),
        in_specs=[pl.BlockSpec((N_DEV, D_MODEL), lambda l, j: (0, 0)),
                  pl.BlockSpec((None, D_MODEL, ADA_TN), lambda l, j: (l, 0, j)),
                  pl.BlockSpec((None, 1, ADA_TN), lambda l, j: (l, 0, j))],
        out_specs=pl.BlockSpec((None, N_DEV, ADA_TN), lambda l, j: (l, 0, j)),
        out_shape=_sds((DEPTH, N_DEV, ADA_SHARD), F32),
        compiler_params=_cparams(("parallel", "parallel")),
    )(c_all, ada_w, ada_b_cols)


def _ada_bwd(c_all, dmod_cols, name):
    def body(c_ref, d_ref, o_ref):
        ca = _silu(c_ref[...]).astype(BF16)
        o_ref[...] = _dot_tn(ca, d_ref[...].astype(BF16))

    return _pcall(
        body, name=name, grid=(DEPTH, ADA_SHARD // ADA_TN),
        in_specs=[pl.BlockSpec((N_DEV, D_MODEL), lambda l, j: (0, 0)),
                  pl.BlockSpec((None, N_DEV, ADA_TN), lambda l, j: (l, 0, j))],
        out_specs=pl.BlockSpec((None, D_MODEL, ADA_TN), lambda l, j: (l, 0, j)),
        out_shape=_sds((DEPTH, D_MODEL, ADA_SHARD), F32),
        compiler_params=_cparams(("parallel", "parallel")),
    )(c_all, dmod_cols)


def _lower_bounds(logits, name):
    def body(l_ref, o_ref):
        l0 = l_ref[0:1, :]
        l1 = l_ref[1:2, :]
        mx = jnp.maximum(l0, l1)
        e0 = jnp.exp(l0 - mx)
        e1 = jnp.exp(l1 - mx)
        p0 = e0 / (e0 + e1)
        p1 = e1 / (e0 + e1)
        o_ref[0:1, :] = p0 - p0
        o_ref[1:2, :] = (p0 + p1) - p0

    return _pcall(body, name=name, out_shape=_sds((DEPTH, D_MODEL), F32), compiler_params=_cparams())(logits)


_R_DMOD = 0
_R_NMIX = 96
_R_NFFN = 112
_R_QG = 128
_R_KG = 152
_R_GN = 176
_R_LB = 184
_R_RB = 192
SMALL_ROWS = 200


def _small_totals(gathered, logits8, name):
    ng = len(GROUPS)

    def body(g_ref, l_ref, main_ref, gains_ref, dlb_ref, rb_ref):
        tot = g_ref[0]
        for dev in range(1, N_DEV):
            tot = tot + g_ref[dev]
        main_ref[...] = tot[0:_R_QG]
        gains_ref[...] = jnp.zeros_like(gains_ref)
        for g in range(ng):
            gains_ref[g:g + 1, :] = jnp.sum(tot[_R_QG + 8 * g:_R_QG + 8 * g + 8], axis=0, keepdims=True)
            gains_ref[ng + g:ng + g + 1, :] = jnp.sum(tot[_R_KG + 8 * g:_R_KG + 8 * g + 8], axis=0, keepdims=True)
        gains_ref[2 * ng:2 * ng + 1, :] = jnp.sum(tot[_R_GN:_R_GN + 8], axis=0, keepdims=True)
        rb_ref[...] = tot[_R_RB:_R_RB + 8]
        dlb1 = tot[_R_LB:_R_LB + 8]
        l0 = l_ref[0]
        l1 = l_ref[1]
        mx = jnp.maximum(l0, l1)
        e0 = jnp.exp(l0 - mx)
        e1 = jnp.exp(l1 - mx)
        p0 = e0 / (e0 + e1)
        p1 = e1 / (e0 + e1)
        dlb_ref[0] = -p0 * p1 * dlb1
        dlb_ref[1] = p1 * (1.0 - p1) * dlb1

    return _pcall(
        body, name=name,
        out_shape=[_sds((_R_QG, 128), F32), _sds((8, 128), F32), _sds((DEPTH, 8, 128), F32), _sds((8, 128), F32)],
        compiler_params=_cparams(),
    )(gathered, logits8)


def _row_tile(rows):
    return 128 if rows % 128 == 0 else rows


def _adamw(w, grads, m, v, name):
    nl, r, cdim = w.shape
    tr = _row_tile(r)

    def body(*refs):
        g_refs = refs[:nl]
        w_ref, m_ref, v_ref, go_ref, d_ref, mo_ref, vo_ref = refs[nl:]

        def step(g):
            m2 = ADAM_B1 * m_ref[...] + (1.0 - ADAM_B1) * g
            v2 = ADAM_B2 * v_ref[...] + (1.0 - ADAM_B2) * (g * g)
            m_hat = m2 / (1.0 - ADAM_B1 ** ADAM_STEP)
            v_hat = v2 / (1.0 - ADAM_B2 ** ADAM_STEP)
            go_ref[...] = g
            d_ref[...] = -ADAM_LR * (m_hat / (jnp.sqrt(v_hat) + ADAM_EPS) + ADAM_WD * w_ref[...])
            mo_ref[...] = m2
            vo_ref[...] = v2

        if nl == 1:
            step(g_refs[0][...])
        else:
            for layer in range(nl):
                @pl.when(pl.program_id(0) == layer)
                def _(layer=layer):
                    step(g_refs[layer][...])

    big = pl.BlockSpec((None, tr, cdim), lambda l, i: (l, i, 0))
    g_specs = [pl.BlockSpec((tr, cdim), lambda l, i, layer=layer: (jnp.where(l == layer, i, 0), 0))
               for layer in range(nl)]
    shp = _sds((nl, r, cdim), F32)
    return _pcall(
        body, name=name, grid=(nl, r // tr),
        in_specs=g_specs + [big, big, big],
        out_specs=[big, big, big, big],
        out_shape=[shp, shp, shp, shp],
        compiler_params=_cparams(("parallel", "parallel")),
    )(*grads, w, m, v)


def _cast_bf16(place, w, name):
    nl, r, cdim = w.shape
    tr = _row_tile(r)

    def body(place_ref, w_ref, o_ref):
        o_ref[...] = w_ref[...].astype(BF16)

    return _pcall(
        body, name=name,
        grid_spec=pltpu.PrefetchScalarGridSpec(
            num_scalar_prefetch=1, grid=(nl, r // tr),
            in_specs=[pl.BlockSpec((None, tr, cdim), lambda l, i, place_ref: (l, i, 0))],
            out_specs=pl.BlockSpec((None, None, tr, cdim), lambda l, i, place_ref: (place_ref[1], l, i, 0))),
        out_shape=_sds((N_CHIP, nl, r, cdim), BF16),
        compiler_params=_cparams(("parallel", "parallel")),
    )(place, w)


def _rs_add_cast(place, grads, recvs, name):
    n_a = len(grads)
    _, k, n = grads[0].shape
    kh = k // 2
    tr = _row_tile(kh)
    nb = kh // tr

    def body(place_ref, *refs):
        for a in range(n_a):
            refs[2 * n_a + a][...] = (refs[a][...] + refs[n_a + a][...]).astype(BF16)

    half = pl.BlockSpec((None, tr, n), lambda s, i, place_ref: (s, i, 0))
    mine = pl.BlockSpec((None, tr, n), lambda s, i, place_ref: (s, place_ref[0] * nb + i, 0))
    return _pcall(
        body, name=name,
        grid_spec=pltpu.PrefetchScalarGridSpec(
            num_scalar_prefetch=1, grid=(N_CHIP, nb),
            in_specs=[mine] * n_a + [half] * n_a,
            out_specs=[half] * n_a),
        out_shape=[_sds((N_CHIP, kh, n), BF16)] * n_a,
        compiler_params=_cparams(("parallel", "parallel")),
    )(place, *grads, *recvs)


def _rs_sum4(place, parts, gots, name):
    n_a = len(parts)
    _, kh, n = parts[0].shape
    tr = _row_tile(kh)
    nb = kh // tr

    def body(place_ref, *refs):
        for a in range(n_a):
            acc = refs[a][...].astype(F32)
            for j in range(N_CHIP - 1):
                acc = acc + refs[n_a + a][j].astype(F32)
            refs[2 * n_a + a][...] = acc

    return _pcall(
        body, name=name,
        grid_spec=pltpu.PrefetchScalarGridSpec(
            num_scalar_prefetch=1, grid=(nb,),
            in_specs=[pl.BlockSpec((None, tr, n), lambda i, place_ref: (place_ref[1], i, 0))] * n_a
            + [pl.BlockSpec((N_CHIP - 1, tr, n), lambda i, place_ref: (0, i, 0))] * n_a,
            out_specs=[pl.BlockSpec((tr, n), lambda i, place_ref: (place_ref[0] * nb + i, 0))] * n_a),
        out_shape=[_sds((2 * kh, n), F32)] * n_a,
        compiler_params=_cparams(("parallel",)),
    )(place, *parts, *gots)


_ANY = pl.BlockSpec(memory_space=pl.ANY)


def _position():
    return lax.axis_index("x"), lax.axis_index("y"), lax.axis_index("c")


def _other_chips(x, y):
    return [(1 - x, y), (x, 1 - y), (1 - x, 1 - y)]


def _remote(src, dst, send_sem, recv_sem, to):
    return pltpu.make_async_remote_copy(src_ref=src, dst_ref=dst, send_sem=send_sem, recv_sem=recv_sem,
                                        device_id=to, device_id_type=MESH)


def _small_allgather(v, name):
    r = v.shape[0]

    def body(x_ref, out_ref, send_sems, recv_sems, local_sem):
        x, y, c = _position()
        me, sibling = (x, y, c), (x, y, 1 - c)
        chips = _other_chips(x, y)

        def slab(px, py, pc):
            return out_ref.at[4 * px + 2 * py + pc]

        def copy(k, block, to, src=None):
            return _remote(slab(*block) if src is None else src, slab(*block), send_sems.at[k], recv_sems.at[k], to)

        mine = pltpu.make_async_copy(x_ref, slab(*me), local_sem)
        mine.start()
        first = [copy(0, me, sibling, src=x_ref)]
        first += [copy(1 + j, me, (*chip, c), src=x_ref) for j, chip in enumerate(chips)]
        for cp in first:
            cp.start()
        passed = [copy(4 + j, (*chip, c), sibling) for j, chip in enumerate(chips)]
        for j, chip in enumerate(chips):
            copy(1 + j, (*chip, c), me).wait_recv()
            passed[j].start()
        copy(0, sibling, me).wait_recv()
        for j, chip in enumerate(chips):
            copy(4 + j, (*chip, 1 - c), me).wait_recv()
        for cp in first + passed:
            cp.wait_send()
        mine.wait()

    return _pcall(
        body, name=name,
        out_shape=_sds((N_DEV, r, 128), F32),
        in_specs=[pl.BlockSpec(memory_space=pltpu.VMEM)],
        out_specs=pl.BlockSpec(memory_space=pltpu.VMEM),
        scratch_shapes=[pltpu.SemaphoreType.DMA((7,)), pltpu.SemaphoreType.DMA((7,)), pltpu.SemaphoreType.DMA],
        compiler_params=_cparams(),
    )(v)


def _half_rows(core, kh):
    return pl.ds(pl.multiple_of(core * kh, 8), kh)


def _slab_half(ref, chip, core):
    return ref.at[chip, :, _half_rows(core, ref.shape[2] // 2), :]


def _gather_ici(out, send_sems, recv_sems):
    def copies():
        x, y, c = _position()
        for a in range(len(out)):
            for j, (px, py) in enumerate(_other_chips(x, y)):
                mine = _slab_half(out[a], 2 * x + y, c)
                landed = _slab_half(out[a], 2 * px + py, c)
                yield (_remote(mine, mine, send_sems.at[a, j], recv_sems.at[a, j], (px, py, c)),
                       _remote(landed, landed, send_sems.at[a, j], recv_sems.at[a, j], (px, py, c)))

    def start():
        for send, _ in copies():
            send.start()

    def wait():
        for send, recv in copies():
            recv.wait_recv()
            send.wait_send()

    return start, wait


def _gather_d2d(out, send_sems, recv_sems):
    def copies():
        x, y, c = _position()
        for a in range(len(out)):
            for j, (px, py) in enumerate(_other_chips(x, y)):
                landed = _slab_half(out[a], 2 * px + py, c)
                other = _slab_half(out[a], 2 * px + py, 1 - c)
                yield (_remote(landed, landed, send_sems.at[a, j], recv_sems.at[a, j], (x, y, 1 - c)),
                       _remote(other, other, send_sems.at[a, j], recv_sems.at[a, j], (x, y, 1 - c)))

    def start():
        for send, _ in copies():
            send.start()

    def wait():
        for send, recv in copies():
            recv.wait_recv()
            send.wait_send()

    return start, wait


def _gather_weights(slabs, name, ici=True):
    n = len(slabs)

    def body(*refs):
        out = refs[n:2 * n]
        sems = refs[2 * n:]
        if ici:
            start, wait = _gather_ici(out, sems[2], sems[3])
            start()
            wait()
        start, wait = _gather_d2d(out, sems[0], sems[1])
        start()
        wait()

    sem = pltpu.SemaphoreType.DMA((n, 3))
    return _pcall(
        body, name=name,
        out_shape=[_sds(s.shape, BF16) for s in slabs],
        in_specs=[_ANY] * n, out_specs=[_ANY] * n,
        input_output_aliases={a: a for a in range(n)},
        scratch_shapes=[sem, sem] + ([sem, sem] if ici else []),
        compiler_params=_cparams(),
    )(*slabs)


def _rs_halves(grads, out, send_sems, recv_sems):
    def copies():
        x, y, c = _position()
        for a in range(len(grads)):
            kh = grads[a].shape[1] // 2
            yield _remote(grads[a].at[:, _half_rows(1 - c, kh), :], out[a], send_sems.at[a], recv_sems.at[a],
                          (x, y, 1 - c))

    def start():
        for cp in copies():
            cp.start()

    def wait():
        for cp in copies():
            cp.wait()

    return start, wait


def _rs_halves_shapes(grads):
    return [_sds((N_CHIP, g.shape[1] // 2, g.shape[2]), F32) for g in grads]


def _rs_exchange_halves(grads, name):
    n = len(grads)

    def body(*refs):
        start, wait = _rs_halves(refs[:n], refs[n:2 * n], *refs[2 * n:])
        start()
        wait()

    return _pcall(
        body, name=name,
        out_shape=_rs_halves_shapes(grads),
        in_specs=[_ANY] * n, out_specs=[_ANY] * n,
        scratch_shapes=[pltpu.SemaphoreType.DMA((n,)), pltpu.SemaphoreType.DMA((n,))],
        compiler_params=_cparams(),
    )(*grads)


def _rs_chips(parts, out, send_sems, recv_sems):
    def copies():
        x, y, c = _position()
        for a in range(len(parts)):
            for j, (px, py) in enumerate(_other_chips(x, y)):
                got = out[a].at[j]
                yield (_remote(parts[a].at[2 * px + py], got, send_sems.at[a, j], recv_sems.at[a, j], (px, py, c)),
                       _remote(got, got, send_sems.at[a, j], recv_sems.at[a, j], (px, py, c)))

    def start():
        for send, _ in copies():
            send.start()

    def wait():
        for send, recv in copies():
            recv.wait_recv()
            send.wait_send()

    return start, wait


def _rs_chips_shapes(parts):
    return [_sds((N_CHIP - 1,) + p.shape[1:], BF16) for p in parts]


def _rs_join(out, send_sems, recv_sems):
    def copies():
        x, y, c = _position()
        for a in range(len(out)):
            kh = out[a].shape[0] // 2
            mine = out[a].at[_half_rows(c, kh), :]
            theirs = out[a].at[_half_rows(1 - c, kh), :]
            yield (_remote(mine, mine, send_sems.at[a], recv_sems.at[a], (x, y, 1 - c)),
                   _remote(theirs, theirs, send_sems.at[a], recv_sems.at[a], (x, y, 1 - c)))

    def start():
        for send, _ in copies():
            send.start()

    def wait():
        for send, recv in copies():
            recv.wait_recv()
            send.wait_send()

    return start, wait


def _rs_join_halves(fulls, name):
    n = len(fulls)

    def body(*refs):
        start, wait = _rs_join(refs[n:2 * n], *refs[2 * n:])
        start()
        wait()

    return _pcall(
        body, name=name,
        out_shape=[_sds(f.shape, F32) for f in fulls],
        in_specs=[_ANY] * n, out_specs=[_ANY] * n,
        input_output_aliases={a: a for a in range(n)},
        scratch_shapes=[pltpu.SemaphoreType.DMA((n,)), pltpu.SemaphoreType.DMA((n,))],
        compiler_params=_cparams(),
    )(*fulls)


_SMALL_ORDER = ("rel_bias", "ada_b", "norm_mix", "norm_ffn", "attn_q_gain", "attn_k_gain", "hgrn_gnorm",
                "hgrn_lower_bounds")
_WEIGHT_ORDER = ("rel_bias", "ada_w", "ada_b", "norm_mix", "norm_ffn", "attn_w_qkv", "attn_w_out", "attn_q_gain",
                 "attn_k_gain", "hgrn_w_in", "hgrn_w_out", "hgrn_gnorm", "hgrn_lower_bounds", "ffn_w1", "ffn_w3",
                 "ffn_w2")


def _qkv_group_map(t):
    return t // 4, t % 4


def _qkv_chip_map(t):
    return t // 9, t % 9


def _hin_map(t):
    return t // 2, t % 2


def _block_map(t):
    return t, 0


def _pack_rows(parts):
    return jnp.concatenate([p.reshape(-1, 128) for p in parts], axis=0)


def kernel(x, c, rel_bias, ada_w, ada_b, norm_mix, norm_ffn, attn_w_qkv, attn_w_out, attn_q_gain, attn_k_gain, hgrn_w_in, hgrn_w_out, hgrn_gnorm, hgrn_lower_bounds, ffn_w1, ffn_w3, ffn_w2, loss_target, m_rel_bias, m_ada_w, m_ada_b, m_norm_mix, m_norm_ffn, m_attn_w_qkv, m_attn_w_out, m_attn_q_gain, m_attn_k_gain, m_hgrn_w_in, m_hgrn_w_out, m_hgrn_gnorm, m_hgrn_lower_bounds, m_ffn_w1, m_ffn_w3, m_ffn_w2, v_rel_bias, v_ada_w, v_ada_b, v_norm_mix, v_norm_ffn, v_attn_w_qkv, v_attn_w_out, v_attn_q_gain, v_attn_k_gain, v_hgrn_w_in, v_hgrn_w_out, v_hgrn_gnorm, v_hgrn_lower_bounds, v_ffn_w1, v_ffn_w3, v_ffn_w2):
    weights = dict(rel_bias=rel_bias, ada_w=ada_w, ada_b=ada_b, norm_mix=norm_mix, norm_ffn=norm_ffn,
                   attn_w_qkv=attn_w_qkv, attn_w_out=attn_w_out, attn_q_gain=attn_q_gain, attn_k_gain=attn_k_gain,
                   hgrn_w_in=hgrn_w_in, hgrn_w_out=hgrn_w_out, hgrn_gnorm=hgrn_gnorm,
                   hgrn_lower_bounds=hgrn_lower_bounds, ffn_w1=ffn_w1, ffn_w3=ffn_w3, ffn_w2=ffn_w2)
    mom1 = dict(rel_bias=m_rel_bias, ada_w=m_ada_w, ada_b=m_ada_b, norm_mix=m_norm_mix, norm_ffn=m_norm_ffn,
                attn_w_qkv=m_attn_w_qkv, attn_w_out=m_attn_w_out, attn_q_gain=m_attn_q_gain,
                attn_k_gain=m_attn_k_gain, hgrn_w_in=m_hgrn_w_in, hgrn_w_out=m_hgrn_w_out, hgrn_gnorm=m_hgrn_gnorm,
                hgrn_lower_bounds=m_hgrn_lower_bounds, ffn_w1=m_ffn_w1, ffn_w3=m_ffn_w3, ffn_w2=m_ffn_w2)
    mom2 = dict(rel_bias=v_rel_bias, ada_w=v_ada_w, ada_b=v_ada_b, norm_mix=v_norm_mix, norm_ffn=v_norm_ffn,
                attn_w_qkv=v_attn_w_qkv, attn_w_out=v_attn_w_out, attn_q_gain=v_attn_q_gain,
                attn_k_gain=v_attn_k_gain, hgrn_w_in=v_hgrn_w_in, hgrn_w_out=v_hgrn_w_out, hgrn_gnorm=v_hgrn_gnorm,
                hgrn_lower_bounds=v_hgrn_lower_bounds, ffn_w1=v_ffn_w1, ffn_w3=v_ffn_w3, ffn_w2=v_ffn_w2)

    transposed = ("ffn_w1", "ffn_w3")
    for group in (weights, mom1, mom2):
        for k in transposed:
            group[k] = jnp.transpose(group[k], (0, 2, 1))

    xi, yi, ci = _position()
    chip = 2 * xi + yi
    dev = 4 * xi + 2 * yi + ci
    place = jnp.stack([ci, chip]).astype(jnp.int32)
    d = D_MODEL

    big_names = ("attn_w_qkv", "attn_w_out", "hgrn_w_in", "hgrn_w_out", "ffn_w1", "ffn_w3", "ffn_w2")
    early_names, late_names = big_names[:1], big_names[1:]
    slabs16 = {k: _cast_bf16(place, weights[k], "cast_" + k) for k in big_names}
    wg = dict(zip(early_names, _gather_weights([slabs16[k] for k in early_names], "gather_early")))

    c_all = _small_allgather(c.reshape(8, 128), "gather_c").reshape(N_DEV, d)
    ada_b_cols = lax.dynamic_slice(ada_b, (0, chip * ADA_SHARD), (DEPTH, ADA_SHARD)).reshape(DEPTH, 1, ADA_SHARD)
    mod_shard = _ada_fwd(c_all, ada_w, ada_b_cols, "ada_fwd")
    mod_all = _small_allgather(mod_shard.reshape(-1, 128), "gather_mod").reshape(N_DEV, DEPTH, N_DEV, ADA_SHARD)
    mod_mine = lax.dynamic_index_in_dim(mod_all[0::2], dev, axis=2, keepdims=False)
    mod = jnp.transpose(mod_mine, (1, 0, 2)).reshape(DEPTH, 6 * d)

    def mods(layer):
        return [mod[layer:layer + 1, j * d:(j + 1) * d] for j in range(6)]

    x0 = x.reshape(SEQ, d)
    target = loss_target.reshape(SEQ, d)
    qg = attn_q_gain.reshape(len(GROUPS), 1, HEAD_DIM)
    kg = attn_k_gain.reshape(len(GROUPS), 1, HEAD_DIM)
    bias = _attn_bias(rel_bias, "attn_bias")
    lb1 = _lower_bounds(hgrn_lower_bounds, "lower_bounds")[1:2]

    def ffn_fwd(layer, x_in, sc2, sh2, g2):
        hf = _norm_mod(x_in, norm_ffn[layer:layer + 1], sc2, sh2, f"l{layer}_norm_ffn")
        a1, a3, u = _ffn_up(hf, wg["ffn_w1"], wg["ffn_w3"], layer, f"l{layer}_ffn_up")
        z, x_out = _mm_rows(u, wg["ffn_w2"], layer, x_in, g2, f"l{layer}_ffn_down")
        return x_out, (hf, a1, a3, u, z)

    def ffn_bwd(layer, dz, dg2, dx_out, x_in, sc2, sh2, saved, mixer_branch, halves=()):
        hf, a1, a3, u, _ = saved
        da1, da3, *recv = _ffn_down_bwd(dz, wg["ffn_w2"], layer, a1, a3, f"l{layer}_ffn_down_bwd", halves=halves)
        dw2 = _mm_rows_bwd_w(u, dz, f"l{layer}_dw2")
        dh = _ffn_up_bwd(da1, da3, wg["ffn_w1"], wg["ffn_w3"], layer, f"l{layer}_ffn_up_bwd")
        dw1 = _mm_rows_bwd_w(da1, hf, f"l{layer}_dw1")
        dw3 = _mm_rows_bwd_w(da3, hf, f"l{layer}_dw3")
        dx_in, dsc2, dsh2, dnf, dz_mix, dg_mix = _norm_mod_bwd(x_in, norm_ffn[layer:layer + 1], sc2, sh2, dh, dx_out,
                                                               f"l{layer}_norm_ffn_bwd", branch=mixer_branch)
        return dx_in, (dw1, dw3, dw2), (dsh2, dsc2, dg2), dnf, recv, dz_mix, dg_mix

    def rs_batched(fn, prefix, tags, *columns):
        out = [None] * len(tags)
        by_shape = {}
        for idx, arr in enumerate(columns[0]):
            by_shape.setdefault(arr.shape, []).append(idx)
        for idxs in by_shape.values():
            for lo in range(0, len(idxs), 3):
                sel = idxs[lo:lo + 3]
                k, layer = tags[sel[0]]
                res = fn(place, *[[col[i] for i in sel] for col in columns], f"{prefix}_{k}_{layer}_x{len(sel)}")
                for i, r in zip(sel, res):
                    out[i] = r
        return out

    def rs_add(tags, grads_in, recv):
        return rs_batched(_rs_add_cast, "rs_add", tags, grads_in, list(recv))

    sh1_0, sc1_0, g1_0, sh2_0, sc2_0, g2_0 = mods(0)
    h0 = _norm_mod(x0, norm_mix[0:1], sc1_0, sh1_0, "l0_norm_mix")
    w_qkv9 = _retile_cols(wg["attn_w_qkv"].reshape(N_CHIP, d, 2304), n_out=9, width_out=d, tn=256,
                          src_map=_qkv_chip_map, dst_map=_qkv_group_map, n_tiles=36,
                          name="regroup_w_qkv").reshape(9, 1, d, d)
    qkv9 = _mm_cols(h0, w_qkv9, 0, n_blocks=9, width=d, tn=d, act_map=_block_map, w_map=_block_map,
                    out_dtype=F32, name="l0_qkv")
    o4, lse, *late = _attn_fwd(qkv9, qg, kg, bias, "l0_attn", gather=[slabs16[k] for k in late_names])
    wg.update(zip(late_names, _gather_weights(late, "gather_late_siblings", ici=False)))
    y0, x1 = _mm_rows(o4, wg["attn_w_out"], 0, x0, g1_0, "l0_attn_out")
    x2, ffn0 = ffn_fwd(0, x1, sc2_0, sh2_0, g2_0)

    sh1_1, sc1_1, g1_1, sh2_1, sc2_1, g2_1 = mods(1)
    h1 = _norm_mod(x2, norm_mix[1:2], sc1_1, sh1_1, "l1_norm_mix")
    proj4 = _mm_cols(h1, wg["hgrn_w_in"], 0, n_blocks=4, width=d, tn=512, act_map=_hin_map, w_map=_hin_map,
                     out_dtype=F32, name="l1_hgrn_in")
    o_raw, yg4, states = _hgrn_fwd(proj4, lb1, hgrn_gnorm, "l1_hgrn")
    y1, x3 = _mm_rows(yg4, wg["hgrn_w_out"], 0, x2, g1_1, "l1_hgrn_out")
    x4, ffn1 = ffn_fwd(1, x3, sc2_1, sh2_1, g2_1)

    dx4, loss_part, dz_ffn1, dg2_1 = _loss_head(x4, target, ffn1[4], g2_1, "loss_head")
    loss = lax.psum(loss_part[0, 0], ("x", "y", "c"))

    dx3, (dw1_1, dw3_1, dw2_1), dmod2_1, dnf_1, _, dzm1, dg1_1 = ffn_bwd(
        1, dz_ffn1, dg2_1, dx4, x3, sc2_1, sh2_1, ffn1, (y1, g1_1))
    dyg4 = _mm_rows_bwd_a(dzm1, wg["hgrn_w_out"], 0, "l1_hgrn_out_bwd")
    dw_hout = _mm_rows_bwd_w(yg4, dzm1, "l1_dw_hgrn_out")
    dproj4, dlb_h, dgn_h = _hgrn_bwd(proj4, lb1, hgrn_gnorm, o_raw, dyg4, states, "l1_hgrn_bwd")
    dh1 = _mm_cols_bwd_a(dproj4, wg["hgrn_w_in"], 0, group=N_CHIP, name="l1_hgrn_in_bwd", tm=512)
    dw_hin = _mm_cols_bwd_w(h1, dproj4, ns=d, tn=d, act_map=_block_map, w_map=_block_map, n_tiles=N_CHIP,
                            name="l1_dw_hgrn_in", tm=2048)
    dx2, dsc1_1, dsh1_1, dnm_1, dz_ffn0, dg2_0 = _norm_mod_bwd(x2, norm_mix[1:2], sc1_1, sh1_1, dh1, dx3,
                                                               "l1_norm_mix_bwd", branch=(ffn0[4], g2_0))

    tags_1 = [("hgrn_w_in", 0), ("hgrn_w_out", 0), ("ffn_w1", 1), ("ffn_w3", 1), ("ffn_w2", 1)]
    grads_1 = [dw_hin, dw_hout, dw1_1, dw3_1, dw2_1]
    dx1, (dw1_0, dw3_0, dw2_0), dmod2_0, dnf_0, recv_1, dzm0, dg1_0 = ffn_bwd(
        0, dz_ffn0, dg2_0, dx2, x1, sc2_0, sh2_0, ffn0, (y0, g1_0), halves=grads_1)
    tags_0 = [("ffn_w1", 0), ("ffn_w3", 0), ("ffn_w2", 0)]
    grads_0 = [dw1_0, dw3_0, dw2_0]
    do4, *recv_0 = _mm_rows_bwd_a(dzm0, wg["attn_w_out"], 0, "l0_attn_out_bwd", halves=grads_0)
    dw_aout = _mm_rows_bwd_w(o4, dzm0, "l0_dw_attn_out")
    tags_a = tags_1 + tags_0
    parts_a = rs_add(tags_1, grads_1, recv_1) + rs_add(tags_0, grads_0, recv_0)
    dqkv, dqg_h, dkg_h, dbias, *got_a = _attn_bwd(qkv9, qg, kg, bias, do4, o4, lse, "l0_attn_bwd", scatter=parts_a)
    dqkv9 = dqkv.reshape(9, SEQ, d)
    dw_qkv9 = _mm_cols_bwd_w(h0, dqkv9, ns=d, tn=d, act_map=_block_map, w_map=_block_map, n_tiles=9,
                             name="l0_dw_qkv", tm=2048, n_out=9)
    dw_qkv = _retile_cols(dw_qkv9, n_out=N_CHIP, width_out=2304, tn=256, src_map=_qkv_group_map,
                          dst_map=_qkv_chip_map, n_tiles=36, name="regroup_dw_qkv")
    tags_b = [("attn_w_qkv", 0), ("attn_w_out", 0)]
    grads_b = [dw_qkv, dw_aout]
    parts_b = rs_add(tags_b, grads_b, _rs_exchange_halves(grads_b, "rs_exchange_halves_b"))
    dh0, *got_b = _mm_cols_bwd_a(dqkv9, w_qkv9, 0, group=3, name="l0_qkv_bwd", scatter=parts_b)
    dx0, dsc1_0, dsh1_0, dnm_0 = _norm_mod_bwd(x0, norm_mix[0:1], sc1_0, sh1_0, dh0, dx1, "l0_norm_mix_bwd")
    drb8 = _relbias_bwd(dbias, jnp.asarray(_bias_tables()), "rel_bias_bwd")

    small = _pack_rows([
        dsh1_0, dsc1_0, dg1_0, *dmod2_0, dsh1_1, dsc1_1, dg1_1, *dmod2_1,
        dnm_0, dnm_1, dnf_0, dnf_1,
        jnp.transpose(dqg_h, (1, 0, 2, 3)), jnp.transpose(dkg_h, (1, 0, 2, 3)), dgn_h, dlb_h, drb8])
    small_all = _small_allgather(small, "gather_small")
    main, gains, dlbnd, rbt = _small_totals(small_all, hgrn_lower_bounds.reshape(DEPTH, 8, 128), "small_totals")
    ng = len(GROUPS)
    grads = {
        "ada_b": main[_R_DMOD:_R_NMIX].reshape(DEPTH, 6 * d),
        "norm_mix": main[_R_NMIX:_R_NFFN].reshape(DEPTH, d),
        "norm_ffn": main[_R_NFFN:_R_QG].reshape(DEPTH, d),
        "attn_q_gain": gains[0:ng].reshape(1, ng, HEAD_DIM),
        "attn_k_gain": gains[ng:2 * ng].reshape(1, ng, HEAD_DIM),
        "hgrn_gnorm": gains[2 * ng:2 * ng + 1],
        "hgrn_lower_bounds": dlbnd.reshape(DEPTH, d),
        "rel_bias": jnp.transpose(rbt[:, :ng * NUM_BUCKETS].reshape(HEADS, ng, NUM_BUCKETS), (2, 1, 0))
                       .reshape(NUM_BUCKETS, ng * HEADS),
    }
    dmod_all = small_all[:, _R_DMOD:_R_NMIX].reshape(N_DEV, DEPTH, 6 * d)
    dmod_cols = jnp.transpose(lax.dynamic_slice(dmod_all, (0, 0, chip * ADA_SHARD), (N_DEV, DEPTH, ADA_SHARD)),
                              (1, 0, 2))
    grad_ada_w = _ada_bwd(c_all, dmod_cols, "ada_bwd")

    tags = tags_a + tags_b
    halves = rs_batched(_rs_sum4, "rs_sum", tags, parts_a + parts_b, list(got_a) + list(got_b))
    full = dict(zip(tags, _rs_join_halves(halves, "rs_join_halves")))

    out_g, out_d, out_m, out_v = {}, {}, {}, {}
    for k in big_names:
        gs = [full[(k, layer)] for layer in range(weights[k].shape[0])]
        out_g[k], out_d[k], out_m[k], out_v[k] = _adamw(weights[k], gs, mom1[k], mom2[k], "adamw_" + k)
    shp = (1, DEPTH * d, ADA_SHARD)
    res = _adamw(ada_w.reshape(shp), [grad_ada_w.reshape(shp[1:])], m_ada_w.reshape(shp), v_ada_w.reshape(shp),
                 "adamw_ada_w")
    out_g["ada_w"], out_d["ada_w"], out_m["ada_w"], out_v["ada_w"] = [r.reshape(ada_w.shape) for r in res]
    packed = [_pack_rows([src[k] for k in _SMALL_ORDER])[None] for src in (weights, grads, mom1, mom2)]
    res = _adamw(packed[0], [packed[1][0]], packed[2], packed[3], "adamw_small")
    offset = 0
    for k in _SMALL_ORDER:
        size = weights[k].size
        for dst, r in zip((out_g, out_d, out_m, out_v), res):
            dst[k] = r.reshape(-1)[offset:offset + size].reshape(weights[k].shape)
        offset += size
    for dst in (out_g, out_d, out_m, out_v):
        for k in transposed:
            dst[k] = jnp.transpose(dst[k], (0, 2, 1))

    return (loss, dx0.reshape(x.shape), *[out_g[k] for k in _WEIGHT_ORDER], *[out_d[k] for k in _WEIGHT_ORDER],
            *[out_m[k] for k in _WEIGHT_ORDER], *[out_v[k] for k in _WEIGHT_ORDER])
```

```python
import functools

import numpy as np
import jax
import jax.numpy as jnp
from jax import lax
from jax.experimental import pallas as pl
from jax.experimental.pallas import tpu as pltpu

F32 = jnp.float32
BF16 = jnp.bfloat16

D_MODEL = 1024
SEQ = 4096
N_DEV = 8
N_CHIP = 4
DEPTH = 2
HEADS = 8
HEAD_DIM = 128
GROUPS = ((128, 1), (512, 4), (2048, 16))
ATT_BLK = 128
ATT_WAYS = 4
ATT_STEPS = SEQ // ATT_BLK // ATT_WAYS
NUM_BUCKETS = 32
MAX_DISTANCE = 2048
FFN_HIDDEN = 2816
FFN_SHARD = FFN_HIDDEN // N_CHIP
HG_SUB = 16
HG_TC = 512
HG_HP = 4
RMS_EPS = 1e-6
NEG = -1e30
ATT_SCALE = HEAD_DIM ** -0.5
LOG2_E = 1.4426950408889634
ADAM_LR, ADAM_B1, ADAM_B2, ADAM_EPS, ADAM_WD, ADAM_STEP = 0.001, 0.9, 0.999, 1e-08, 0.01, 10
VMEM_LIMIT = 56 * 1024 * 1024
MESH = pl.DeviceIdType.MESH


def _pcall(body, **kw):
    return pl.pallas_call(body, **kw)


def _cparams(sem=None):
    if sem is None:
        return pltpu.CompilerParams(vmem_limit_bytes=VMEM_LIMIT)
    return pltpu.CompilerParams(dimension_semantics=sem, vmem_limit_bytes=VMEM_LIMIT)


def _sds(shape, dtype):
    return jax.ShapeDtypeStruct(shape, dtype)


def _dot(a, b):
    return jnp.dot(a, b, preferred_element_type=F32)


def _dot_nt(a, b):
    return lax.dot_general(a, b, (((1,), (1,)), ((), ())), preferred_element_type=F32)


def _dot_tn(a, b):
    return lax.dot_general(a, b, (((0,), (0,)), ((), ())), preferred_element_type=F32)


def _sigmoid(x):
    return 1.0 / (1.0 + jnp.exp(-x))


def _silu(x):
    return x * _sigmoid(x)


def _dsilu(x):
    s = _sigmoid(x)
    return s * (1.0 + x * (1.0 - s))


def _norm_mod(x, gain, sc, sh, name):
    tm = 512

    def body(x_ref, g_ref, sc_ref, sh_ref, h_ref):
        xv = x_ref[...]
        rs = lax.rsqrt(jnp.mean(xv * xv, axis=-1, keepdims=True) + RMS_EPS)
        h_ref[...] = ((xv * rs * g_ref[...]) * (1.0 + sc_ref[...]) + sh_ref[...]).astype(BF16)

    vec = pl.BlockSpec((1, D_MODEL), lambda i: (0, 0))
    return _pcall(
        body, name=name, grid=(SEQ // tm,),
        in_specs=[pl.BlockSpec((tm, D_MODEL), lambda i: (i, 0)), vec, vec, vec],
        out_specs=pl.BlockSpec((tm, D_MODEL), lambda i: (i, 0)),
        out_shape=_sds((SEQ, D_MODEL), BF16),
        compiler_params=_cparams(("parallel",)),
    )(x, gain, sc, sh)


def _gated_branch_bwd(dx, z_ref, gate_ref, dz_ref, dgate_ref):
    dz_ref[...] = (dx * gate_ref[...]).astype(BF16)
    dgate_ref[...] += jnp.sum(dx * z_ref[...], axis=0, keepdims=True)


def _norm_mod_bwd(x, gain, sc, sh, dh, dres, name, branch=None):
    tm = 512
    n_b = 2 if branch else 0

    def body(*refs):
        x_ref, g_ref, sc_ref, sh_ref, dh_ref, dres_ref = refs[:6]
        dx_ref, dsc_ref, dsh_ref, dg_ref = refs[6 + n_b:10 + n_b]

        @pl.when(pl.program_id(0) == 0)
        def _():
            dsc_ref[...] = jnp.zeros_like(dsc_ref)
            dsh_ref[...] = jnp.zeros_like(dsh_ref)
            dg_ref[...] = jnp.zeros_like(dg_ref)
            if branch:
                refs[11 + n_b][...] = jnp.zeros_like(refs[11 + n_b])

        xv = x_ref[...]
        dhv = dh_ref[...]
        rs = lax.rsqrt(jnp.mean(xv * xv, axis=-1, keepdims=True) + RMS_EPS)
        xh = xv * rs
        dsc_ref[...] += jnp.sum(dhv * (xh * g_ref[...]), axis=0, keepdims=True)
        dsh_ref[...] += jnp.sum(dhv, axis=0, keepdims=True)
        dhn = dhv * (1.0 + sc_ref[...])
        dg_ref[...] += jnp.sum(dhn * xh, axis=0, keepdims=True)
        dxh = dhn * g_ref[...]
        dx = dres_ref[...] + rs * (dxh - xh * jnp.mean(dxh * xh, axis=-1, keepdims=True))
        dx_ref[...] = dx
        if branch:
            _gated_branch_bwd(dx, refs[6], refs[7], refs[10 + n_b], refs[11 + n_b])

    vec = pl.BlockSpec((1, D_MODEL), lambda i: (0, 0))
    big = pl.BlockSpec((tm, D_MODEL), lambda i: (i, 0))
    return _pcall(
        body, name=name, grid=(SEQ // tm,),
        in_specs=[big, vec, vec, vec, big, big] + ([big, vec] if branch else []),
        out_specs=[big, vec, vec, vec] + ([big, vec] if branch else []),
        out_shape=[_sds((SEQ, D_MODEL), F32)] + [_sds((1, D_MODEL), F32)] * 3
        + ([_sds((SEQ, D_MODEL), BF16), _sds((1, D_MODEL), F32)] if branch else []),
        compiler_params=_cparams(("arbitrary",)),
    )(x, gain, sc, sh, dh, dres, *(branch or ()))


def _mm_cols(a, wg, layer, *, n_blocks, width, tn, act_map, w_map, out_dtype, name, tm=1024):
    k = a.shape[1]
    n_tiles = n_blocks * width // tn

    def body(a_ref, w_ref, o_ref):
        o_ref[...] = _dot(a_ref[...], w_ref[...]).astype(o_ref.dtype)

    return _pcall(
        body, name=name, grid=(SEQ // tm, n_tiles),
        in_specs=[pl.BlockSpec((tm, k), lambda i, t: (i, 0)),
                  pl.BlockSpec((None, None, k, tn), lambda i, t: (w_map(t)[0], layer, 0, w_map(t)[1]))],
        out_specs=pl.BlockSpec((None, tm, tn), lambda i, t: (act_map(t)[0], i, act_map(t)[1])),
        out_shape=_sds((n_blocks, SEQ, width), out_dtype),
        compiler_params=_cparams(("parallel", "arbitrary")),
    )(a, wg)


def _mm_cols_bwd_a(dout, wg, layer, *, group, name, tm=1024, scatter=()):
    n_blocks, _, width = dout.shape
    k = wg.shape[2]
    n_s = len(scatter)
    n_rows = SEQ // tm
    n_steps = n_blocks // group

    def body(*refs):
        d_ref, w_ref = refs[:2]
        o_ref = refs[2 + n_s]
        if n_s:
            comm_start, comm_wait = _rs_chips(refs[2:2 + n_s], refs[3 + n_s:3 + 2 * n_s], *refs[3 + 2 * n_s:])
            pl.when((pl.program_id(0) == 0) & (pl.program_id(1) == 0))(comm_start)
        acc = _dot_nt(d_ref[0], w_ref[0])
        for b in range(1, group):
            acc += _dot_nt(d_ref[b], w_ref[b])
        if n_steps == 1:
            o_ref[...] = acc
        else:
            @pl.when(pl.program_id(1) == 0)
            def _():
                o_ref[...] = acc

            @pl.when(pl.program_id(1) > 0)
            def _():
                o_ref[...] += acc
        if n_s:
            pl.when((pl.program_id(0) == n_rows - 1) & (pl.program_id(1) == n_steps - 1))(comm_wait)

    sem = pltpu.SemaphoreType.DMA((max(n_s, 1), 3))
    res = _pcall(
        body, name=name, grid=(n_rows, n_steps),
        in_specs=[pl.BlockSpec((group, tm, width), lambda i, t: (t, i, 0)),
                  pl.BlockSpec((group, None, k, width), lambda i, t: (t, layer, 0, 0))] + [_ANY] * n_s,
        out_specs=[pl.BlockSpec((tm, k), lambda i, t: (i, 0))] + [_ANY] * n_s,
        out_shape=[_sds((SEQ, k), F32)] + _rs_chips_shapes(scatter),
        scratch_shapes=[sem, sem] if n_s else [],
        compiler_params=_cparams(("arbitrary", "arbitrary") if n_s else ("parallel", "arbitrary")),
    )(dout, wg, *scatter)
    return res if n_s else res[0]


def _mm_cols_bwd_w(a, dout, *, ns, tn, act_map, w_map, n_tiles, name, tm=1024, n_out=N_CHIP):
    k = a.shape[1]

    def body(a_ref, d_ref, o_ref):
        @pl.when(pl.program_id(1) == 0)
        def _():
            o_ref[...] = jnp.zeros_like(o_ref)

        o_ref[...] += _dot_tn(a_ref[...], d_ref[...])

    return _pcall(
        body, name=name, grid=(n_tiles, SEQ // tm),
        in_specs=[pl.BlockSpec((tm, k), lambda t, i: (i, 0)),
                  pl.BlockSpec((None, tm, tn), lambda t, i: (act_map(t)[0], i, act_map(t)[1]))],
        out_specs=pl.BlockSpec((None, k, tn), lambda t, i: (w_map(t)[0], 0, w_map(t)[1])),
        out_shape=_sds((n_out, k, ns), F32),
        compiler_params=_cparams(("parallel", "arbitrary")),
    )(a, dout)


def _retile_cols(src, *, n_out, width_out, tn, src_map, dst_map, n_tiles, name):
    k = src.shape[1]

    def body(s_ref, o_ref):
        o_ref[...] = s_ref[...]

    return _pcall(
        body, name=name, grid=(n_tiles,),
        in_specs=[pl.BlockSpec((None, k, tn), lambda t: (src_map(t)[0], 0, src_map(t)[1]))],
        out_specs=pl.BlockSpec((None, k, tn), lambda t: (dst_map(t)[0], 0, dst_map(t)[1])),
        out_shape=_sds((n_out, k, width_out), src.dtype),
        compiler_params=_cparams(("parallel",)),
    )(src)


def _mm_rows(a4, wg, layer, x, gate, name, tm=512):
    ks = a4.shape[2]
    n = wg.shape[3]

    def body(a_ref, w_ref, x_ref, g_ref, z_ref, xn_ref):
        z = _dot(a_ref[0], w_ref[0])
        for s in range(1, N_CHIP):
            z += _dot(a_ref[s], w_ref[s])
        z_ref[...] = z.astype(BF16)
        xn_ref[...] = x_ref[...] + g_ref[...] * z

    big = pl.BlockSpec((tm, n), lambda i: (i, 0))
    return _pcall(
        body, name=name, grid=(SEQ // tm,),
        in_specs=[pl.BlockSpec((N_CHIP, tm, ks), lambda i: (0, i, 0)),
                  pl.BlockSpec((N_CHIP, None, ks, n), lambda i: (0, layer, 0, 0)),
                  big, pl.BlockSpec((1, n), lambda i: (0, 0))],
        out_specs=[big, big],
        out_shape=[_sds((SEQ, n), BF16), _sds((SEQ, n), F32)],
        compiler_params=_cparams(("parallel",)),
    )(a4, wg, x, gate)


def _mm_rows_bwd_a(dz, wg, layer, name, tm=1024, halves=()):
    ks, n = wg.shape[2], wg.shape[3]
    n_h = len(halves)
    n_rows = SEQ // tm

    def body(*refs):
        dz_ref, w_ref = refs[:2]
        o_ref = refs[2 + n_h]
        if n_h:
            comm_start, comm_wait = _rs_halves(refs[2:2 + n_h], refs[3 + n_h:3 + 2 * n_h], *refs[3 + 2 * n_h:])
            pl.when((pl.program_id(0) == 0) & (pl.program_id(1) == 0))(comm_start)
        o_ref[...] = _dot_nt(dz_ref[...], w_ref[...])
        if n_h:
            pl.when((pl.program_id(0) == n_rows - 1) & (pl.program_id(1) == N_CHIP - 1))(comm_wait)

    sem = pltpu.SemaphoreType.DMA((max(n_h, 1),))
    res = _pcall(
        body, name=name, grid=(n_rows, N_CHIP),
        in_specs=[pl.BlockSpec((tm, n), lambda i, s: (i, 0)),
                  pl.BlockSpec((None, None, ks, n), lambda i, s: (s, layer, 0, 0))] + [_ANY] * n_h,
        out_specs=[pl.BlockSpec((None, tm, ks), lambda i, s: (s, i, 0))] + [_ANY] * n_h,
        out_shape=[_sds((N_CHIP, SEQ, ks), F32)] + _rs_halves_shapes(halves),
        scratch_shapes=[sem, sem] if n_h else [],
        compiler_params=_cparams(("arbitrary", "arbitrary") if n_h else ("parallel", "arbitrary")),
    )(dz, wg, *halves)
    return res if n_h else res[0]


def _mm_rows_bwd_w(a4, dz, name, tm=2048):
    return _mm_rows_bwd_w_multi([a4], dz, name, tm)[0]


def _mm_rows_bwd_w_multi(a4s, dz, name, tm=2048):
    n_a = len(a4s)
    ks = a4s[0].shape[2]
    n = dz.shape[1]

    def body(*refs):
        dz_ref = refs[n_a]

        for j in range(n_a):
            o_ref = refs[n_a + 1 + j]

            @pl.when(pl.program_id(1) == 0)
            def _(o_ref=o_ref):
                o_ref[...] = jnp.zeros_like(o_ref)

            o_ref[...] += _dot_tn(refs[j][...], dz_ref[...])

    return _pcall(
        body, name=name, grid=(N_CHIP, SEQ // tm),
        in_specs=[pl.BlockSpec((None, tm, ks), lambda s, i: (s, i, 0))] * n_a
        + [pl.BlockSpec((tm, n), lambda s, i: (i, 0))],
        out_specs=[pl.BlockSpec((None, ks, n), lambda s, i: (s, 0, 0))] * n_a,
        out_shape=[_sds((N_CHIP, ks, n), F32)] * n_a,
        compiler_params=_cparams(("parallel", "arbitrary")),
    )(*a4s, dz)


def _ffn_up(h, w1g, w3g, layer, name, tm=1024):
    def body(h_ref, w1_ref, w3_ref, a1_ref, a3_ref, u_ref):
        hv = h_ref[...]
        a1 = _dot_nt(hv, w1_ref[...])
        a3 = _dot_nt(hv, w3_ref[...])
        a1_ref[...] = a1.astype(BF16)
        a3_ref[...] = a3.astype(BF16)
        u_ref[...] = (_silu(a1) * a3).astype(BF16)

    wspec = pl.BlockSpec((None, None, FFN_SHARD, D_MODEL), lambda i, s: (s, layer, 0, 0))
    ospec = pl.BlockSpec((None, tm, FFN_SHARD), lambda i, s: (s, i, 0))
    shp = (N_CHIP, SEQ, FFN_SHARD)
    return _pcall(
        body, name=name, grid=(SEQ // tm, N_CHIP),
        in_specs=[pl.BlockSpec((tm, D_MODEL), lambda i, s: (i, 0)), wspec, wspec],
        out_specs=[ospec, ospec, ospec],
        out_shape=[_sds(shp, BF16), _sds(shp, BF16), _sds(shp, BF16)],
        compiler_params=_cparams(("parallel", "arbitrary")),
    )(h, w1g, w3g)


def _ffn_up_bwd(da1, da3, w1g, w3g, layer, name, tm=512):
    def body(d1_ref, d3_ref, w1_ref, w3_ref, o_ref):
        acc = _dot(d1_ref[0], w1_ref[0]) + _dot(d3_ref[0], w3_ref[0])
        for s in range(1, N_CHIP):
            acc += _dot(d1_ref[s], w1_ref[s]) + _dot(d3_ref[s], w3_ref[s])
        o_ref[...] = acc

    wspec = pl.BlockSpec((N_CHIP, None, FFN_SHARD, D_MODEL), lambda i: (0, layer, 0, 0))
    dspec = pl.BlockSpec((N_CHIP, tm, FFN_SHARD), lambda i: (0, i, 0))
    return _pcall(
        body, name=name, grid=(SEQ // tm,),
        in_specs=[dspec, dspec, wspec, wspec],
        out_specs=pl.BlockSpec((tm, D_MODEL), lambda i: (i, 0)),
        out_shape=_sds((SEQ, D_MODEL), F32),
        compiler_params=_cparams(("parallel",)),
    )(da1, da3, w1g, w3g)


def _ffn_down_bwd(dz, w2g, layer, a1, a3, name, tm=1024, halves=()):
    n_h = len(halves)
    n_rows = SEQ // tm

    def body(*refs):
        dz_ref, w_ref, a1_ref, a3_ref = refs[:4]
        da1_ref, da3_ref = refs[4 + n_h:6 + n_h]
        if n_h:
            comm_start, comm_wait = _rs_halves(refs[4:4 + n_h], refs[6 + n_h:6 + 2 * n_h], *refs[6 + 2 * n_h:])
            pl.when((pl.program_id(0) == 0) & (pl.program_id(1) == 0))(comm_start)
        du = _dot_nt(dz_ref[...], w_ref[...])
        a1 = a1_ref[...].astype(F32)
        da1_ref[...] = (du * a3_ref[...].astype(F32) * _dsilu(a1)).astype(BF16)
        da3_ref[...] = (du * _silu(a1)).astype(BF16)
        if n_h:
            pl.when((pl.program_id(0) == n_rows - 1) & (pl.program_id(1) == N_CHIP - 1))(comm_wait)

    blk = pl.BlockSpec((None, tm, FFN_SHARD), lambda i, s: (s, i, 0))
    shp = (N_CHIP, SEQ, FFN_SHARD)
    sem = pltpu.SemaphoreType.DMA((max(n_h, 1),))
    return _pcall(
        body, name=name, grid=(n_rows, N_CHIP),
        in_specs=[pl.BlockSpec((tm, D_MODEL), lambda i, s: (i, 0)),
                  pl.BlockSpec((None, None, FFN_SHARD, D_MODEL), lambda i, s: (s, layer, 0, 0)),
                  blk, blk] + [_ANY] * n_h,
        out_specs=[blk, blk] + [_ANY] * n_h,
        out_shape=[_sds(shp, BF16), _sds(shp, BF16)] + _rs_halves_shapes(halves),
        scratch_shapes=[sem, sem] if n_h else [],
        compiler_params=_cparams(("arbitrary", "arbitrary") if n_h else ("parallel", "arbitrary")),
    )(dz, w2g, a1, a3, *halves)


def _loss_head(y, target, z, gate, name):
    tm = 512
    n_steps = SEQ // tm

    def body(y_ref, t_ref, z_ref, gate_ref, dy_ref, l_ref, dz_ref, dgate_ref, acc_ref):
        @pl.when(pl.program_id(0) == 0)
        def _():
            acc_ref[...] = jnp.zeros_like(acc_ref)
            dgate_ref[...] = jnp.zeros_like(dgate_ref)

        err = y_ref[...] - t_ref[...]
        dy = err * (1.0 / D_MODEL)
        dy_ref[...] = dy
        acc_ref[...] += jnp.sum(jnp.mean(err * err, axis=-1, keepdims=True), axis=0, keepdims=True)
        _gated_branch_bwd(dy, z_ref, gate_ref, dz_ref, dgate_ref)

        @pl.when(pl.program_id(0) == n_steps - 1)
        def _():
            l_ref[...] = 0.5 * acc_ref[...]

    big = pl.BlockSpec((tm, D_MODEL), lambda i: (i, 0))
    vec = pl.BlockSpec((1, D_MODEL), lambda i: (0, 0))
    return _pcall(
        body, name=name, grid=(n_steps,),
        in_specs=[big, big, big, vec],
        out_specs=[big, pl.BlockSpec((1, 1), lambda i: (0, 0)), big, vec],
        out_shape=[_sds((SEQ, D_MODEL), F32), _sds((1, 1), F32), _sds((SEQ, D_MODEL), BF16),
                   _sds((1, D_MODEL), F32)],
        scratch_shapes=[pltpu.VMEM((1, 1), F32)],
        compiler_params=_cparams(("arbitrary",)),
    )(y, target, z, gate)


def _attn_rows(base, d):
    if d == 1:
        return pl.ds(pl.multiple_of(base, ATT_BLK), ATT_BLK)
    return pl.ds(base, ATT_BLK, stride=d)


def _attn_block_index(i, d):
    nb = SEQ // (ATT_BLK * d)
    r = i // nb
    n = i % nb
    base = r + n * (ATT_BLK * d)
    pbase = jnp.maximum(base - ATT_BLK * d, r)
    return n, _attn_rows(base, d), _attn_rows(pbase, d)


def _attn_two_blocks(ref, prow, rows):
    return jnp.concatenate([ref[prow, :].astype(BF16), ref[rows, :].astype(BF16)], axis=0)


def _attn_block_bias(b_ref, n):
    b = b_ref[...]
    prev_half = lax.broadcasted_iota(jnp.int32, b.shape, 1) < ATT_BLK
    return jnp.where(prev_half & (n == 0), NEG, b)


def _qk_normed(x):
    rs = lax.rsqrt(jnp.mean(x * x, axis=-1, keepdims=True) + RMS_EPS)
    return x * rs, rs


def _attn_fwd(qkv9, qgain, kgain, bias, name, gather=()):
    n_g = len(gather)

    def body(*refs):
        q_ref, k_ref, v_ref, qg_ref, kg_ref, b_ref = refs[:6]
        o_ref, lse_ref = refs[6 + n_g:8 + n_g]
        qn_s, kn_s, acc_s, m_s, l_s = refs[8 + 2 * n_g:13 + 2 * n_g]
        g = pl.program_id(1)
        if n_g:
            comm_start, comm_wait = _gather_ici(refs[8 + n_g:8 + 2 * n_g], *refs[13 + 2 * n_g:])
            pl.when((pl.program_id(0) == 0) & (g == 0))(comm_start)

        @pl.when(g == 0)
        def _():
            m_s[...] = jnp.full_like(m_s, NEG)
            l_s[...] = jnp.zeros_like(l_s)
            acc_s[...] = jnp.zeros_like(acc_s)

        qn_s[...] = _qk_normed(q_ref[...])[0] * qg_ref[...]
        kn_s[...] = _qk_normed(k_ref[...])[0] * kg_ref[...]

        for gi, (_, d) in enumerate(GROUPS):
            @pl.when(g == gi)
            def _(d=d):
                def block(n, qb, kk, vv, m_old, l_old, acc_old):
                    s = _dot_nt(qb, kk) * ATT_SCALE + _attn_block_bias(b_ref, n)
                    m_new = jnp.maximum(m_old, jnp.max(s, axis=-1, keepdims=True))
                    alpha = jnp.exp(m_old - m_new)
                    p = jnp.exp(s - m_new)
                    l_new = alpha * l_old + jnp.sum(p, axis=-1, keepdims=True)
                    acc_new = alpha * acc_old + _dot(p.astype(BF16), vv)
                    return m_new, l_new, acc_new

                def it(i, carry):
                    where, loaded = [], []
                    for way in range(ATT_WAYS):
                        n, rows, prow = _attn_block_index(i + way * ATT_STEPS, d)
                        where.append(rows)
                        loaded.append((n, qn_s[rows, :].astype(BF16), _attn_two_blocks(kn_s, prow, rows),
                                       _attn_two_blocks(v_ref, prow, rows), m_s[rows, :], l_s[rows, :],
                                       acc_s[rows, :]))
                    results = [block(*vals) for vals in loaded]
                    for rows, (m_new, l_new, acc_new) in zip(where, results):
                        m_s[rows, :] = m_new
                        l_s[rows, :] = l_new
                        acc_s[rows, :] = acc_new
                    return carry

                lax.fori_loop(0, ATT_STEPS, it, 0)

        @pl.when(g == len(GROUPS) - 1)
        def _():
            o_ref[...] = (acc_s[...] / l_s[...]).astype(BF16)
            lse_ref[...] = m_s[...] + jnp.log(l_s[...])

        if n_g:
            pl.when((pl.program_id(0) == HEADS - 1) & (g == len(GROUPS) - 1))(comm_wait)

    def col(j):
        return pl.BlockSpec((None, SEQ, HEAD_DIM), lambda h, g: (g * 3 + j, 0, h))

    gspec = pl.BlockSpec((None, 1, HEAD_DIM), lambda h, g: (g, 0, 0))
    sem = pltpu.SemaphoreType.DMA((max(n_g, 1), 3))
    return _pcall(
        body, name=name, grid=(HEADS, len(GROUPS)),
        in_specs=[col(0), col(1), col(2), gspec, gspec,
                  pl.BlockSpec((None, None, ATT_BLK, 2 * ATT_BLK), lambda h, g: (g, h, 0, 0))] + [_ANY] * n_g,
        out_specs=[pl.BlockSpec((None, SEQ, HEAD_DIM), lambda h, g: (h // 2, 0, h % 2)),
                   pl.BlockSpec((None, SEQ, 1), lambda h, g: (h, 0, 0))] + [_ANY] * n_g,
        out_shape=[_sds((N_CHIP, SEQ, 2 * HEAD_DIM), BF16), _sds((HEADS, SEQ, 1), F32)]
        + [_sds(s.shape, s.dtype) for s in gather],
        input_output_aliases={6 + a: 2 + a for a in range(n_g)},
        scratch_shapes=[pltpu.VMEM((SEQ, HEAD_DIM), F32)] * 3 + [pltpu.VMEM((SEQ, 1), F32)] * 2
        + ([sem, sem] if n_g else []),
        compiler_params=_cparams(("arbitrary", "arbitrary")),
    )(qkv9, qkv9, qkv9, qgain, kgain, bias, *gather)


def _attn_bwd(qkv9, qgain, kgain, bias, do4, o4, lse, name, scatter=()):
    n_s = len(scatter)

    def body(*refs):
        q_ref, k_ref, v_ref, qg_ref, kg_ref, b_ref, do_ref, o_ref, lse_ref = refs[:9]
        dqkv_ref, dqg_ref, dkg_ref, db_ref = refs[9 + n_s:13 + n_s]
        qn_s, kn_s, dq_s, dk_s, dv_s, dl_s = refs[13 + 2 * n_s:19 + 2 * n_s]
        g = pl.program_id(1)
        if n_s:
            comm_start, comm_wait = _rs_chips(refs[9:9 + n_s], refs[13 + n_s:13 + 2 * n_s], *refs[19 + 2 * n_s:])
            pl.when((pl.program_id(0) == 0) & (g == 0))(comm_start)
        qh, rq = _qk_normed(q_ref[...])
        kh, rk = _qk_normed(k_ref[...])
        qn_s[...] = qh * qg_ref[...]
        kn_s[...] = kh * kg_ref[...]
        @pl.when(g == 0)
        def _():
            dl_s[...] = jnp.sum(do_ref[...] * o_ref[...].astype(F32), axis=-1, keepdims=True)

        dk_s[...] = jnp.zeros_like(dk_s)
        dv_s[...] = jnp.zeros_like(dv_s)
        db_ref[...] = jnp.zeros_like(db_ref)

        for gi, (_, d) in enumerate(GROUPS):
            @pl.when(g == gi)
            def _(d=d):
                def block(n, qb, kk, vv, dob, lse_b, dl):
                    s = _dot_nt(qb, kk) * ATT_SCALE + _attn_block_bias(b_ref, n)
                    p = jnp.exp(s - lse_b)
                    ds = p * (_dot_nt(dob, vv) - dl)
                    ds16 = ds.astype(BF16)
                    return (ds, _dot(ds16, kk) * ATT_SCALE, _dot_tn(ds16, qb) * ATT_SCALE,
                            _dot_tn(p.astype(BF16), dob))

                def it(i, carry):
                    where, loaded, old = [], [], []
                    for way in range(ATT_WAYS):
                        n, rows, prow = _attn_block_index(i + way * ATT_STEPS, d)
                        where.append((rows, prow))
                        loaded.append((n, qn_s[rows, :].astype(BF16), _attn_two_blocks(kn_s, prow, rows),
                                       _attn_two_blocks(v_ref, prow, rows), do_ref[rows, :].astype(BF16),
                                       lse_ref[rows, :], dl_s[rows, :]))
                        old.append((dk_s[rows, :], dk_s[prow, :], dv_s[rows, :], dv_s[prow, :]))
                    results = [block(*vals) for vals in loaded]
                    db_ref[...] += functools.reduce(lambda a, b: a + b, [r[0] for r in results])
                    for (rows, prow), (dk_c, dk_p, dv_c, dv_p), (_, dq, dkk, dvv) in zip(where, old, results):
                        dq_s[rows, :] = dq
                        dk_s[prow, :] = dk_p + dkk[:ATT_BLK]
                        dv_s[prow, :] = dv_p + dvv[:ATT_BLK]
                        dk_s[rows, :] = dk_c + dkk[ATT_BLK:]
                        dv_s[rows, :] = dv_c + dvv[ATT_BLK:]
                    return carry

                lax.fori_loop(0, ATT_STEPS, it, 0)

        def norm_bwd(dn, xh, rs, gain):
            dgain = jnp.sum(dn * xh, axis=0, keepdims=True)
            dxh = dn * gain
            return rs * (dxh - xh * jnp.mean(dxh * xh, axis=-1, keepdims=True)), dgain

        dq, dqg = norm_bwd(dq_s[...], qh, rq, qg_ref[...])
        dk, dkg = norm_bwd(dk_s[...], kh, rk, kg_ref[...])
        dqkv_ref[0] = dq.astype(BF16)
        dqkv_ref[1] = dk.astype(BF16)
        dqkv_ref[2] = dv_s[...].astype(BF16)
        dqg_ref[...] = dqg
        dkg_ref[...] = dkg
        if n_s:
            pl.when((pl.program_id(0) == HEADS - 1) & (g == len(GROUPS) - 1))(comm_wait)

    def col(j):
        return pl.BlockSpec((None, SEQ, HEAD_DIM), lambda h, g: (g * 3 + j, 0, h))

    gspec = pl.BlockSpec((None, 1, HEAD_DIM), lambda h, g: (g, 0, 0))
    bspec = pl.BlockSpec((None, None, ATT_BLK, 2 * ATT_BLK), lambda h, g: (g, h, 0, 0))
    hcol = pl.BlockSpec((None, SEQ, HEAD_DIM), lambda h, g: (h // 2, 0, h % 2))
    dgspec = pl.BlockSpec((None, None, 1, HEAD_DIM), lambda h, g: (h, g, 0, 0))
    ng = len(GROUPS)
    sem = pltpu.SemaphoreType.DMA((max(n_s, 1), 3))
    return _pcall(
        body, name=name, grid=(HEADS, ng),
        in_specs=[col(0), col(1), col(2), gspec, gspec, bspec, hcol, hcol,
                  pl.BlockSpec((None, SEQ, 1), lambda h, g: (h, 0, 0))] + [_ANY] * n_s,
        out_specs=[pl.BlockSpec((None, 3, SEQ, HEAD_DIM), lambda h, g: (g, 0, 0, h)), dgspec, dgspec, bspec]
        + [_ANY] * n_s,
        out_shape=[_sds((ng, 3, SEQ, D_MODEL), BF16), _sds((HEADS, ng, 1, HEAD_DIM), F32),
                   _sds((HEADS, ng, 1, HEAD_DIM), F32), _sds((ng, HEADS, ATT_BLK, 2 * ATT_BLK), F32)]
        + _rs_chips_shapes(scatter),
        scratch_shapes=[pltpu.VMEM((SEQ, HEAD_DIM), F32)] * 5 + [pltpu.VMEM((SEQ, 1), F32)]
        + ([sem, sem] if n_s else []),
        compiler_params=_cparams(("arbitrary", "arbitrary")),
    )(qkv9, qkv9, qkv9, qgain, kgain, bias, do4, o4, lse, *scatter)


def _relbias_bwd(dbias, bucket_idx, name):
    ng = len(GROUPS)

    def body(db_ref, idx_ref, o_ref):
        lane = lax.broadcasted_iota(jnp.int32, (HEADS, 128), 1)
        acc = jnp.zeros((HEADS, 128), F32)
        for g in range(ng):
            dbg = db_ref[g]
            idx = idx_ref[g]
            for b in range(NUM_BUCKETS):
                sel = jnp.where((idx == b)[None], dbg, 0.0)
                part = jnp.sum(sel, axis=1)
                val = jnp.sum(part, axis=-1, keepdims=True)
                acc = jnp.where(lane == g * NUM_BUCKETS + b, val, acc)
        o_ref[...] = acc

    return _pcall(body, name=name, out_shape=_sds((HEADS, 128), F32), compiler_params=_cparams())(dbias, bucket_idx)


def _scan16(x, reverse=False):
    row = lax.broadcasted_iota(jnp.int32, x.shape, 0)
    for sh in (1, 2, 4, 8):
        if reverse:
            x = x + jnp.where(row < HG_SUB - sh, pltpu.roll(x, HG_SUB - sh, 0), 0.0)
        else:
            x = x + jnp.where(row >= sh, pltpu.roll(x, sh, 0), 0.0)
    return x


def _hgrn_gates(qr, fr, lbv):
    q = _silu(qr)
    sig = _sigmoid(fr)
    fg = lbv + (1.0 - lbv) * sig
    lf = jnp.log(fg) * LOG2_E
    gcum = _scan16(lf)
    glast = jnp.sum(lf, axis=0, keepdims=True)
    return q, sig, fg, 1.0 - fg, gcum, glast


def _hgrn_intra(q, k, gcum, tri):
    e = jnp.exp2(jnp.where(tri, gcum[:, None, :] - gcum[None, :, :], NEG))
    a = jnp.sum(q[:, None, :] * k[None, :, :] * e, axis=-1, keepdims=True)
    return e, a


def _hgrn_fwd(proj4, lb, gain, name):
    nsub = HG_TC // HG_SUB
    wide = HG_HP * HEAD_DIM

    def body(p_ref, lb_ref, gn_ref, o_ref, y_ref, st_ref, state_s):
        @pl.when(pl.program_id(1) == 0)
        def _():
            state_s[...] = jnp.zeros_like(state_s)

        gnv = gn_ref[...]
        shp = (HG_SUB, HG_SUB, HEAD_DIM)
        tri = lax.broadcasted_iota(jnp.int32, shp, 0) >= lax.broadcasted_iota(jnp.int32, shp, 1)

        def head(qr, fr, vv, gr, lbv, st):
            q, _, _, k, gcum, glast = _hgrn_gates(qr, fr, lbv)
            _, a = _hgrn_intra(q, k, gcum, tri)
            o = jnp.sum(a * vv[None, :, :], axis=1) + _dot_nt((q * jnp.exp2(gcum)).astype(BF16), st.astype(BF16))
            kg = k * jnp.exp2(glast - gcum)
            st_new = st * jnp.exp2(glast) + _dot_tn(vv.astype(BF16), kg.astype(BF16))
            rs = lax.rsqrt(jnp.mean(o * o, axis=-1, keepdims=True) + RMS_EPS)
            return o, (o * rs * gnv * _silu(gr)).astype(BF16), st_new

        def it(i, carry):
            rows = pl.ds(pl.multiple_of(i * HG_SUB, HG_SUB), HG_SUB)
            loaded = []
            for hh in range(HG_HP):
                lanes = pl.ds(hh * HEAD_DIM, HEAD_DIM)
                loaded.append(([p_ref[j, rows, lanes] for j in range(4)], lb_ref[:, lanes], state_s[hh]))
            results = [head(blk[0], blk[1], blk[2], blk[3], lbv, st) for blk, lbv, st in loaded]
            for hh, ((_, _, st), (o, y, st_new)) in enumerate(zip(loaded, results)):
                lanes = pl.ds(hh * HEAD_DIM, HEAD_DIM)
                st_ref[hh, i] = st.astype(BF16)
                state_s[hh] = st_new
                o_ref[rows, lanes] = o
                y_ref[hh // 2, rows, pl.ds((hh % 2) * HEAD_DIM, HEAD_DIM)] = y
            return carry

        lax.fori_loop(0, nsub, it, 0)

    return _pcall(
        body, name=name, grid=(HEADS // HG_HP, SEQ // HG_TC),
        in_specs=[pl.BlockSpec((4, HG_TC, wide), lambda h, j: (0, j, h)),
                  pl.BlockSpec((1, wide), lambda h, j: (0, h)),
                  pl.BlockSpec((1, HEAD_DIM), lambda h, j: (0, 0))],
        out_specs=[pl.BlockSpec((HG_TC, wide), lambda h, j: (j, h)),
                   pl.BlockSpec((HG_HP // 2, HG_TC, 2 * HEAD_DIM), lambda h, j: (h, j, 0)),
                   pl.BlockSpec((HG_HP, nsub, HEAD_DIM, HEAD_DIM), lambda h, j: (h, j, 0, 0))],
        out_shape=[_sds((SEQ, D_MODEL), F32), _sds((N_CHIP, SEQ, 2 * HEAD_DIM), BF16),
                   _sds((HEADS, SEQ // HG_SUB, HEAD_DIM, HEAD_DIM), BF16)],
        scratch_shapes=[pltpu.VMEM((HG_HP, HEAD_DIM, HEAD_DIM), F32)],
        compiler_params=_cparams(("parallel", "arbitrary")),
    )(proj4, lb, gain)


def _hgrn_bwd(proj4, lb, gain, o_raw, dy4, states, name):
    nsub = HG_TC // HG_SUB
    nt = SEQ // HG_TC
    wide = HG_HP * HEAD_DIM

    def body(p_ref, lb_ref, gn_ref, o_ref, dy_ref, st_ref, dp_ref, dlb_ref, dgn_ref, dst_s):
        @pl.when(pl.program_id(1) == 0)
        def _():
            dst_s[...] = jnp.zeros_like(dst_s)
            dlb_ref[...] = jnp.zeros_like(dlb_ref)
            dgn_ref[...] = jnp.zeros_like(dgn_ref)

        gnv = gn_ref[...]
        shp = (HG_SUB, HG_SUB, HEAD_DIM)
        tri = lax.broadcasted_iota(jnp.int32, shp, 0) >= lax.broadcasted_iota(jnp.int32, shp, 1)

        def head(qr, fr, vv, gr, o, dy, lbv, st0, dst):
            q, sig, fg, k, gcum, glast = _hgrn_gates(qr, fr, lbv)
            rs = lax.rsqrt(jnp.mean(o * o, axis=-1, keepdims=True) + RMS_EPS)
            oh = o * rs
            don = dy * _silu(gr)
            dgn = jnp.sum(don * oh, axis=0, keepdims=True)
            dgr = dy * oh * gnv * _dsilu(gr)
            doh = don * gnv
            do = rs * (doh - oh * jnp.mean(doh * oh, axis=-1, keepdims=True))
            dst16 = dst.astype(BF16)
            do16 = do.astype(BF16)
            eg = jnp.exp2(gcum)
            eb = jnp.exp2(glast - gcum)
            e, a = _hgrn_intra(q, k, gcum, tri)
            da = jnp.sum(do[:, None, :] * vv[None, :, :], axis=-1, keepdims=True)
            dae = da * e
            dq = jnp.sum(dae * k[None, :, :], axis=1) + eg * _dot(do16, st0)
            dk_state = eb * _dot(vv.astype(BF16), dst16)
            dk = jnp.sum(dae * q[:, None, :], axis=0) + dk_state
            dv = jnp.sum(a * do[:, None, :], axis=0) + _dot_nt((k * eb).astype(BF16), dst16)
            eglast = jnp.exp2(glast)
            dst_new = dst * eglast + _dot_tn(do16, (q * eg).astype(BF16))
            dglast = jnp.sum(k * dk_state, axis=0, keepdims=True) \
                + eglast * jnp.sum(dst * st0.astype(F32), axis=0, keepdims=True)
            dlf = _scan16(q * dq - k * dk, reverse=True) + dglast
            dfg = dlf / fg - dk
            dlb = jnp.sum(dfg * (1.0 - sig), axis=0, keepdims=True)
            dproj = ((dq * _dsilu(qr)).astype(BF16), (dfg * (1.0 - lbv) * sig * (1.0 - sig)).astype(BF16),
                     dv.astype(BF16), dgr.astype(BF16))
            return dproj, dst_new, dlb, dgn

        def it(ii, carry):
            i = nsub - 1 - ii
            rows = pl.ds(pl.multiple_of(i * HG_SUB, HG_SUB), HG_SUB)
            results = []
            for hh in range(HG_HP):
                lanes = pl.ds(hh * HEAD_DIM, HEAD_DIM)
                blk = [p_ref[j, rows, lanes] for j in range(4)]
                dy = dy_ref[hh // 2, rows, pl.ds((hh % 2) * HEAD_DIM, HEAD_DIM)]
                results.append(head(blk[0], blk[1], blk[2], blk[3], o_ref[rows, lanes], dy,
                                    lb_ref[:, lanes], st_ref[hh, i], dst_s[hh]))
            new_carry = []
            for hh, (dproj, dst_new, dlb, dgn) in enumerate(results):
                lanes = pl.ds(hh * HEAD_DIM, HEAD_DIM)
                dst_s[hh] = dst_new
                for j in range(4):
                    dp_ref[j, rows, lanes] = dproj[j]
                new_carry.append((carry[hh][0] + dlb, carry[hh][1] + dgn))
            return tuple(new_carry)

        zero = jnp.zeros((1, HEAD_DIM), F32)
        sums = lax.fori_loop(0, nsub, it, tuple((zero, zero) for _ in range(HG_HP)))
        for hh in range(HG_HP):
            dlb_ref[hh] += sums[hh][0]
            dgn_ref[hh] += sums[hh][1]

    vspec = pl.BlockSpec((HG_HP, 1, HEAD_DIM), lambda h, j: (h, 0, 0))
    return _pcall(
        body, name=name, grid=(HEADS // HG_HP, nt),
        in_specs=[pl.BlockSpec((4, HG_TC, wide), lambda h, j: (0, nt - 1 - j, h)),
                  pl.BlockSpec((1, wide), lambda h, j: (0, h)),
                  pl.BlockSpec((1, HEAD_DIM), lambda h, j: (0, 0)),
                  pl.BlockSpec((HG_TC, wide), lambda h, j: (nt - 1 - j, h)),
                  pl.BlockSpec((HG_HP // 2, HG_TC, 2 * HEAD_DIM), lambda h, j: (h, nt - 1 - j, 0)),
                  pl.BlockSpec((HG_HP, nsub, HEAD_DIM, HEAD_DIM), lambda h, j: (h, nt - 1 - j, 0, 0))],
        out_specs=[pl.BlockSpec((4, HG_TC, wide), lambda h, j: (0, nt - 1 - j, h)), vspec, vspec],
        out_shape=[_sds((4, SEQ, D_MODEL), BF16), _sds((HEADS, 1, HEAD_DIM), F32), _sds((HEADS, 1, HEAD_DIM), F32)],
        scratch_shapes=[pltpu.VMEM((HG_HP, HEAD_DIM, HEAD_DIM), F32)],
        compiler_params=_cparams(("parallel", "arbitrary")),
    )(proj4, lb, gain, o_raw, dy4, states)


def _t5_bucket(dist):
    n = np.asarray(dist, dtype=np.int64)
    max_exact = NUM_BUCKETS // 2
    large = max_exact + (np.log(np.maximum(n, 1) / max_exact) / np.log(MAX_DISTANCE / max_exact)
                         * (NUM_BUCKETS - max_exact)).astype(np.int64)
    large = np.minimum(large, NUM_BUCKETS - 1)
    return np.where(n < max_exact, n, large).astype(np.int32)


def _bias_tables():
    qi = np.arange(ATT_BLK)[:, None]
    ki = np.arange(2 * ATT_BLK)[None, :]
    j = ATT_BLK + qi - ki
    valid = (j >= 0) & (j <= ATT_BLK)
    return np.stack([np.where(valid, _t5_bucket(np.clip(j, 0, ATT_BLK) * d), -1) for _, d in GROUPS]).astype(np.int32)


def _attn_bias(rel_bias, name):
    idx = _bias_tables()
    ng = len(GROUPS)
    buckets = [sorted(set(idx[g][idx[g] >= 0].tolist())) for g in range(ng)]

    def body(rb_ref, idx_ref, o_ref):
        h = pl.program_id(0)
        for g in range(ng):
            ig = idx_ref[g]
            acc = jnp.full(ig.shape, NEG, F32)
            for b in buckets[g]:
                acc = jnp.where(ig == b, rb_ref[b, g * HEADS + h], acc)
            o_ref[g] = acc

    return _pcall(
        body, name=name, grid=(HEADS,),
        in_specs=[pl.BlockSpec(memory_space=pltpu.SMEM),
                  pl.BlockSpec((ng, ATT_BLK, 2 * ATT_BLK), lambda h: (0, 0, 0))],
        out_specs=pl.BlockSpec((ng, None, ATT_BLK, 2 * ATT_BLK), lambda h: (0, h, 0, 0)),
        out_shape=_sds((ng, HEADS, ATT_BLK, 2 * ATT_BLK), F32),
        compiler_params=_cparams(("parallel",)),
    )(rel_bias, jnp.asarray(idx))


ADA_SHARD = 6 * D_MODEL // N_CHIP
ADA_TN = 512


def _ada_fwd(c_all, ada_w, ada_b_cols, name):
    def body(c_ref, w_ref, b_ref, o_ref):
        ca = _silu(c_ref[...]).astype(BF16)
        o_ref[...] = _dot(ca, w_ref[...].astype(BF16)) + b_ref[...]

    return _pcall(
        body, name=name, grid=(DEPTH, ADA_SHARD // ADA_TN),
        in_specs=[pl.BlockSpec((N_DEV, D_MODEL), lambda l, j: (0, 0)),
                  pl.BlockSpec((None, D_MODEL, ADA_TN), lambda l, j: (l, 0, j)),
                  pl.BlockSpec((None, 1, ADA_TN), lambda l, j: (l, 0, j))],
        out_specs=pl.BlockSpec((None, N_DEV, ADA_TN), lambda l, j: (l, 0, j)),
        out_shape=_sds((DEPTH, N_DEV, ADA_SHARD), F32),
        compiler_params=_cparams(("parallel", "parallel")),
    )(c_all, ada_w, ada_b_cols)


def _ada_bwd(c_all, dmod_cols, name):
    def body(c_ref, d_ref, o_ref):
        ca = _silu(c_ref[...]).astype(BF16)
        o_ref[...] = _dot_tn(ca, d_ref[...].astype(BF16))

    return _pcall(
        body, name=name, grid=(DEPTH, ADA_SHARD // ADA_TN),
        in_specs=[pl.BlockSpec((N_DEV, D_MODEL), lambda l, j: (0, 0)),
                  pl.BlockSpec((None, N_DEV, ADA_TN), lambda l, j: (l, 0, j))],
        out_specs=pl.BlockSpec((None, D_MODEL, ADA_TN), lambda l, j: (l, 0, j)),
        out_shape=_sds((DEPTH, D_MODEL, ADA_SHARD), F32),
        compiler_params=_cparams(("parallel", "parallel")),
    )(c_all, dmod_cols)


def _lower_bounds(logits, name):
    def body(l_ref, o_ref):
        l0 = l_ref[0:1, :]
        l1 = l_ref[1:2, :]
        mx = jnp.maximum(l0, l1)
        e0 = jnp.exp(l0 - mx)
        e1 = jnp.exp(l1 - mx)
        p0 = e0 / (e0 + e1)
        p1 = e1 / (e0 + e1)
        o_ref[0:1, :] = p0 - p0
        o_ref[1:2, :] = (p0 + p1) - p0

    return _pcall(body, name=name, out_shape=_sds((DEPTH, D_MODEL), F32), compiler_params=_cparams())(logits)


_R_DMOD = 0
_R_NMIX = 96
_R_NFFN = 112
_R_QG = 128
_R_KG = 152
_R_GN = 176
_R_LB = 184
_R_RB = 192
SMALL_ROWS = 200


def _small_totals(gathered, logits8, name):
    ng = len(GROUPS)

    def body(g_ref, l_ref, main_ref, gains_ref, dlb_ref, rb_ref):
        tot = g_ref[0]
        for dev in range(1, N_DEV):
            tot = tot + g_ref[dev]
        main_ref[...] = tot[0:_R_QG]
        gains_ref[...] = jnp.zeros_like(gains_ref)
        for g in range(ng):
            gains_ref[g:g + 1, :] = jnp.sum(tot[_R_QG + 8 * g:_R_QG + 8 * g + 8], axis=0, keepdims=True)
            gains_ref[ng + g:ng + g + 1, :] = jnp.sum(tot[_R_KG + 8 * g:_R_KG + 8 * g + 8], axis=0, keepdims=True)
        gains_ref[2 * ng:2 * ng + 1, :] = jnp.sum(tot[_R_GN:_R_GN + 8], axis=0, keepdims=True)
        rb_ref[...] = tot[_R_RB:_R_RB + 8]
        dlb1 = tot[_R_LB:_R_LB + 8]
        l0 = l_ref[0]
        l1 = l_ref[1]
        mx = jnp.maximum(l0, l1)
        e0 = jnp.exp(l0 - mx)
        e1 = jnp.exp(l1 - mx)
        p0 = e0 / (e0 + e1)
        p1 = e1 / (e0 + e1)
        dlb_ref[0] = -p0 * p1 * dlb1
        dlb_ref[1] = p1 * (1.0 - p1) * dlb1

    return _pcall(
        body, name=name,
        out_shape=[_sds((_R_QG, 128), F32), _sds((8, 128), F32), _sds((DEPTH, 8, 128), F32), _sds((8, 128), F32)],
        compiler_params=_cparams(),
    )(gathered, logits8)


def _row_tile(rows):
    return 128 if rows % 128 == 0 else rows


def _adamw(w, grads, m, v, name):
    nl, r, cdim = w.shape
    tr = _row_tile(r)

    def body(*refs):
        g_refs = refs[:nl]
        w_ref, m_ref, v_ref, go_ref, d_ref, mo_ref, vo_ref = refs[nl:]

        def step(g):
            m2 = ADAM_B1 * m_ref[...] + (1.0 - ADAM_B1) * g
            v2 = ADAM_B2 * v_ref[...] + (1.0 - ADAM_B2) * (g * g)
            m_hat = m2 / (1.0 - ADAM_B1 ** ADAM_STEP)
            v_hat = v2 / (1.0 - ADAM_B2 ** ADAM_STEP)
            go_ref[...] = g
            d_ref[...] = -ADAM_LR * (m_hat / (jnp.sqrt(v_hat) + ADAM_EPS) + ADAM_WD * w_ref[...])
            mo_ref[...] = m2
            vo_ref[...] = v2

        if nl == 1:
            step(g_refs[0][...])
        else:
            for layer in range(nl):
                @pl.when(pl.program_id(0) == layer)
                def _(layer=layer):
                    step(g_refs[layer][...])

    big = pl.BlockSpec((None, tr, cdim), lambda l, i: (l, i, 0))
    g_specs = [pl.BlockSpec((tr, cdim), lambda l, i, layer=layer: (jnp.where(l == layer, i, 0), 0))
               for layer in range(nl)]
    shp = _sds((nl, r, cdim), F32)
    return _pcall(
        body, name=name, grid=(nl, r // tr),
        in_specs=g_specs + [big, big, big],
        out_specs=[big, big, big, big],
        out_shape=[shp, shp, shp, shp],
        compiler_params=_cparams(("parallel", "parallel")),
    )(*grads, w, m, v)


def _cast_bf16(place, w, name):
    nl, r, cdim = w.shape
    tr = _row_tile(r)

    def body(place_ref, w_ref, o_ref):
        o_ref[...] = w_ref[...].astype(BF16)

    return _pcall(
        body, name=name,
        grid_spec=pltpu.PrefetchScalarGridSpec(
            num_scalar_prefetch=1, grid=(nl, r // tr),
            in_specs=[pl.BlockSpec((None, tr, cdim), lambda l, i, place_ref: (l, i, 0))],
            out_specs=pl.BlockSpec((None, None, tr, cdim), lambda l, i, place_ref: (place_ref[1], l, i, 0))),
        out_shape=_sds((N_CHIP, nl, r, cdim), BF16),
        compiler_params=_cparams(("parallel", "parallel")),
    )(place, w)


def _rs_add_cast(place, grads, recvs, name):
    n_a = len(grads)
    _, k, n = grads[0].shape
    kh = k // 2
    tr = _row_tile(kh)
    nb = kh // tr

    def body(place_ref, *refs):
        for a in range(n_a):
            refs[2 * n_a + a][...] = (refs[a][...] + refs[n_a + a][...]).astype(BF16)

    half = pl.BlockSpec((None, tr, n), lambda s, i, place_ref: (s, i, 0))
    mine = pl.BlockSpec((None, tr, n), lambda s, i, place_ref: (s, place_ref[0] * nb + i, 0))
    return _pcall(
        body, name=name,
        grid_spec=pltpu.PrefetchScalarGridSpec(
            num_scalar_prefetch=1, grid=(N_CHIP, nb),
            in_specs=[mine] * n_a + [half] * n_a,
            out_specs=[half] * n_a),
        out_shape=[_sds((N_CHIP, kh, n), BF16)] * n_a,
        compiler_params=_cparams(("parallel", "parallel")),
    )(place, *grads, *recvs)


def _rs_sum4(place, parts, gots, name):
    n_a = len(parts)
    _, kh, n = parts[0].shape
    tr = _row_tile(kh)
    nb = kh // tr

    def body(place_ref, *refs):
        for a in range(n_a):
            acc = refs[a][...].astype(F32)
            for j in range(N_CHIP - 1):
                acc = acc + refs[n_a + a][j].astype(F32)
            refs[2 * n_a + a][...] = acc

    return _pcall(
        body, name=name,
        grid_spec=pltpu.PrefetchScalarGridSpec(
            num_scalar_prefetch=1, grid=(nb,),
            in_specs=[pl.BlockSpec((None, tr, n), lambda i, place_ref: (place_ref[1], i, 0))] * n_a
            + [pl.BlockSpec((N_CHIP - 1, tr, n), lambda i, place_ref: (0, i, 0))] * n_a,
            out_specs=[pl.BlockSpec((tr, n), lambda i, place_ref: (place_ref[0] * nb + i, 0))] * n_a),
        out_shape=[_sds((2 * kh, n), F32)] * n_a,
        compiler_params=_cparams(("parallel",)),
    )(place, *parts, *gots)


_ANY = pl.BlockSpec(memory_space=pl.ANY)


def _position():
    return lax.axis_index("x"), lax.axis_index("y"), lax.axis_index("c")


def _other_chips(x, y):
    return [(1 - x, y), (x, 1 - y), (1 - x, 1 - y)]


def _remote(src, dst, send_sem, recv_sem, to):
    return pltpu.make_async_remote_copy(src_ref=src, dst_ref=dst, send_sem=send_sem, recv_sem=recv_sem,
                                        device_id=to, device_id_type=MESH)


def _small_allgather(v, name):
    r = v.shape[0]

    def body(x_ref, out_ref, send_sems, recv_sems, local_sem):
        x, y, c = _position()
        me, sibling = (x, y, c), (x, y, 1 - c)
        chips = _other_chips(x, y)

        def slab(px, py, pc):
            return out_ref.at[4 * px + 2 * py + pc]

        def copy(k, block, to, src=None):
            return _remote(slab(*block) if src is None else src, slab(*block), send_sems.at[k], recv_sems.at[k], to)

        mine = pltpu.make_async_copy(x_ref, slab(*me), local_sem)
        mine.start()
        first = [copy(0, me, sibling, src=x_ref)]
        first += [copy(1 + j, me, (*chip, c), src=x_ref) for j, chip in enumerate(chips)]
        for cp in first:
            cp.start()
        passed = [copy(4 + j, (*chip, c), sibling) for j, chip in enumerate(chips)]
        for j, chip in enumerate(chips):
            copy(1 + j, (*chip, c), me).wait_recv()
            passed[j].start()
        copy(0, sibling, me).wait_recv()
        for j, chip in enumerate(chips):
            copy(4 + j, (*chip, 1 - c), me).wait_recv()
        for cp in first + passed:
            cp.wait_send()
        mine.wait()

    return _pcall(
        body, name=name,
        out_shape=_sds((N_DEV, r, 128), F32),
        in_specs=[pl.BlockSpec(memory_space=pltpu.VMEM)],
        out_specs=pl.BlockSpec(memory_space=pltpu.VMEM),
        scratch_shapes=[pltpu.SemaphoreType.DMA((7,)), pltpu.SemaphoreType.DMA((7,)), pltpu.SemaphoreType.DMA],
        compiler_params=_cparams(),
    )(v)


def _half_rows(core, kh):
    return pl.ds(pl.multiple_of(core * kh, 8), kh)


def _slab_half(ref, chip, core):
    return ref.at[chip, :, _half_rows(core, ref.shape[2] // 2), :]


def _gather_ici(out, send_sems, recv_sems):
    def copies():
        x, y, c = _position()
        for a in range(len(out)):
            for j, (px, py) in enumerate(_other_chips(x, y)):
                mine = _slab_half(out[a], 2 * x + y, c)
                landed = _slab_half(out[a], 2 * px + py, c)
                yield (_remote(mine, mine, send_sems.at[a, j], recv_sems.at[a, j], (px, py, c)),
                       _remote(landed, landed, send_sems.at[a, j], recv_sems.at[a, j], (px, py, c)))

    def start():
        for send, _ in copies():
            send.start()

    def wait():
        for send, recv in copies():
            recv.wait_recv()
            send.wait_send()

    return start, wait


def _gather_d2d(out, send_sems, recv_sems):
    def copies():
        x, y, c = _position()
        for a in range(len(out)):
            for j, (px, py) in enumerate(_other_chips(x, y)):
                landed = _slab_half(out[a], 2 * px + py, c)
                other = _slab_half(out[a], 2 * px + py, 1 - c)
                yield (_remote(landed, landed, send_sems.at[a, j], recv_sems.at[a, j], (x, y, 1 - c)),
                       _remote(other, other, send_sems.at[a, j], recv_sems.at[a, j], (x, y, 1 - c)))

    def start():
        for send, _ in copies():
            send.start()

    def wait():
        for send, recv in copies():
            recv.wait_recv()
            send.wait_send()

    return start, wait


def _gather_weights(slabs, name, ici=True):
    n = len(slabs)

    def body(*refs):
        out = refs[n:2 * n]
        sems = refs[2 * n:]
        if ici:
            start, wait = _gather_ici(out, sems[2], sems[3])
            start()
            wait()
        start, wait = _gather_d2d(out, sems[0], sems[1])
        start()
        wait()

    sem = pltpu.SemaphoreType.DMA((n, 3))
    return _pcall(
        body, name=name,
        out_shape=[_sds(s.shape, BF16) for s in slabs],
        in_specs=[_ANY] * n, out_specs=[_ANY] * n,
        input_output_aliases={a: a for a in range(n)},
        scratch_shapes=[sem, sem] + ([sem, sem] if ici else []),
        compiler_params=_cparams(),
    )(*slabs)


def _rs_halves(grads, out, send_sems, recv_sems):
    def copies():
        x, y, c = _position()
        for a in range(len(grads)):
            kh = grads[a].shape[1] // 2
            yield _remote(grads[a].at[:, _half_rows(1 - c, kh), :], out[a], send_sems.at[a], recv_sems.at[a],
                          (x, y, 1 - c))

    def start():
        for cp in copies():
            cp.start()

    def wait():
        for cp in copies():
            cp.wait()

    return start, wait


def _rs_halves_shapes(grads):
    return [_sds((N_CHIP, g.shape[1] // 2, g.shape[2]), F32) for g in grads]


def _rs_exchange_halves(grads, name):
    n = len(grads)

    def body(*refs):
        start, wait = _rs_halves(refs[:n], refs[n:2 * n], *refs[2 * n:])
        start()
        wait()

    return _pcall(
        body, name=name,
        out_shape=_rs_halves_shapes(grads),
        in_specs=[_ANY] * n, out_specs=[_ANY] * n,
        scratch_shapes=[pltpu.SemaphoreType.DMA((n,)), pltpu.SemaphoreType.DMA((n,))],
        compiler_params=_cparams(),
    )(*grads)


def _rs_chips(parts, out, send_sems, recv_sems):
    def copies():
        x, y, c = _position()
        for a in range(len(parts)):
            for j, (px, py) in enumerate(_other_chips(x, y)):
                got = out[a].at[j]
                yield (_remote(parts[a].at[2 * px + py], got, send_sems.at[a, j], recv_sems.at[a, j], (px, py, c)),
                       _remote(got, got, send_sems.at[a, j], recv_sems.at[a, j], (px, py, c)))

    def start():
        for send, _ in copies():
            send.start()

    def wait():
        for send, recv in copies():
            recv.wait_recv()
            send.wait_send()

    return start, wait


def _rs_chips_shapes(parts):
    return [_sds((N_CHIP - 1,) + p.shape[1:], BF16) for p in parts]


def _rs_join(out, send_sems, recv_sems):
    def copies():
        x, y, c = _position()
        for a in range(len(out)):
            kh = out[a].shape[0] // 2
            mine = out[a].at[_half_rows(c, kh), :]
            theirs = out[a].at[_half_rows(1 - c, kh), :]
            yield (_remote(mine, mine, send_sems.at[a], recv_sems.at[a], (x, y, 1 - c)),
                   _remote(theirs, theirs, send_sems.at[a], recv_sems.at[a], (x, y, 1 - c)))

    def start():
        for send, _ in copies():
            send.start()

    def wait():
        for send, recv in copies():
            recv.wait_recv()
            send.wait_send()

    return start, wait


def _rs_join_halves(fulls, name):
    n = len(fulls)

    def body(*refs):
        start, wait = _rs_join(refs[n:2 * n], *refs[2 * n:])
        start()
        wait()

    return _pcall(
        body, name=name,
        out_shape=[_sds(f.shape, F32) for f in fulls],
        in_specs=[_ANY] * n, out_specs=[_ANY] * n,
        input_output_aliases={a: a for a in range(n)},
        scratch_shapes=[pltpu.SemaphoreType.DMA((n,)), pltpu.SemaphoreType.DMA((n,))],
        compiler_params=_cparams(),
    )(*fulls)


_SMALL_ORDER = ("rel_bias", "ada_b", "norm_mix", "norm_ffn", "attn_q_gain", "attn_k_gain", "hgrn_gnorm",
                "hgrn_lower_bounds")
_WEIGHT_ORDER = ("rel_bias", "ada_w", "ada_b", "norm_mix", "norm_ffn", "attn_w_qkv", "attn_w_out", "attn_q_gain",
                 "attn_k_gain", "hgrn_w_in", "hgrn_w_out", "hgrn_gnorm", "hgrn_lower_bounds", "ffn_w1", "ffn_w3",
                 "ffn_w2")


def _qkv_group_map(t):
    return t // 4, t % 4


def _qkv_chip_map(t):
    return t // 9, t % 9


def _hin_map(t):
    return t // 2, t % 2


def _block_map(t):
    return t, 0


def _pack_rows(parts):
    return jnp.concatenate([p.reshape(-1, 128) for p in parts], axis=0)


def kernel(x, c, rel_bias, ada_w, ada_b, norm_mix, norm_ffn, attn_w_qkv, attn_w_out, attn_q_gain, attn_k_gain, hgrn_w_in, hgrn_w_out, hgrn_gnorm, hgrn_lower_bounds, ffn_w1, ffn_w3, ffn_w2, loss_target, m_rel_bias, m_ada_w, m_ada_b, m_norm_mix, m_norm_ffn, m_attn_w_qkv, m_attn_w_out, m_attn_q_gain, m_attn_k_gain, m_hgrn_w_in, m_hgrn_w_out, m_hgrn_gnorm, m_hgrn_lower_bounds, m_ffn_w1, m_ffn_w3, m_ffn_w2, v_rel_bias, v_ada_w, v_ada_b, v_norm_mix, v_norm_ffn, v_attn_w_qkv, v_attn_w_out, v_attn_q_gain, v_attn_k_gain, v_hgrn_w_in, v_hgrn_w_out, v_hgrn_gnorm, v_hgrn_lower_bounds, v_ffn_w1, v_ffn_w3, v_ffn_w2):
    weights = dict(rel_bias=rel_bias, ada_w=ada_w, ada_b=ada_b, norm_mix=norm_mix, norm_ffn=norm_ffn,
                   attn_w_qkv=attn_w_qkv, attn_w_out=attn_w_out, attn_q_gain=attn_q_gain, attn_k_gain=attn_k_gain,
                   hgrn_w_in=hgrn_w_in, hgrn_w_out=hgrn_w_out, hgrn_gnorm=hgrn_gnorm,
                   hgrn_lower_bounds=hgrn_lower_bounds, ffn_w1=ffn_w1, ffn_w3=ffn_w3, ffn_w2=ffn_w2)
    mom1 = dict(rel_bias=m_rel_bias, ada_w=m_ada_w, ada_b=m_ada_b, norm_mix=m_norm_mix, norm_ffn=m_norm_ffn,
                attn_w_qkv=m_attn_w_qkv, attn_w_out=m_attn_w_out, attn_q_gain=m_attn_q_gain,
                attn_k_gain=m_attn_k_gain, hgrn_w_in=m_hgrn_w_in, hgrn_w_out=m_hgrn_w_out, hgrn_gnorm=m_hgrn_gnorm,
                hgrn_lower_bounds=m_hgrn_lower_bounds, ffn_w1=m_ffn_w1, ffn_w3=m_ffn_w3, ffn_w2=m_ffn_w2)
    mom2 = dict(rel_bias=v_rel_bias, ada_w=v_ada_w, ada_b=v_ada_b, norm_mix=v_norm_mix, norm_ffn=v_norm_ffn,
                attn_w_qkv=v_attn_w_qkv, attn_w_out=v_attn_w_out, attn_q_gain=v_attn_q_gain,
                attn_k_gain=v_attn_k_gain, hgrn_w_in=v_hgrn_w_in, hgrn_w_out=v_hgrn_w_out, hgrn_gnorm=v_hgrn_gnorm,
                hgrn_lower_bounds=v_hgrn_lower_bounds, ffn_w1=v_ffn_w1, ffn_w3=v_ffn_w3, ffn_w2=v_ffn_w2)

    transposed = ("ffn_w1", "ffn_w3")
    for group in (weights, mom1, mom2):
        for k in transposed:
            group[k] = jnp.transpose(group[k], (0, 2, 1))

    xi, yi, ci = _position()
    chip = 2 * xi + yi
    dev = 4 * xi + 2 * yi + ci
    place = jnp.stack([ci, chip]).astype(jnp.int32)
    d = D_MODEL

    big_names = ("attn_w_qkv", "attn_w_out", "hgrn_w_in", "hgrn_w_out", "ffn_w1", "ffn_w3", "ffn_w2")
    early_names, late_names = big_names[:1], big_names[1:]
    slabs16 = {k: _cast_bf16(place, weights[k], "cast_" + k) for k in big_names}
    wg = dict(zip(early_names, _gather_weights([slabs16[k] for k in early_names], "gather_early")))

    c_all = _small_allgather(c.reshape(8, 128), "gather_c").reshape(N_DEV, d)
    ada_b_cols = lax.dynamic_slice(ada_b, (0, chip * ADA_SHARD), (DEPTH, ADA_SHARD)).reshape(DEPTH, 1, ADA_SHARD)
    mod_shard = _ada_fwd(c_all, ada_w, ada_b_cols, "ada_fwd")
    mod_all = _small_allgather(mod_shard.reshape(-1, 128), "gather_mod").reshape(N_DEV, DEPTH, N_DEV, ADA_SHARD)
    mod_mine = lax.dynamic_index_in_dim(mod_all[0::2], dev, axis=2, keepdims=False)
    mod = jnp.transpose(mod_mine, (1, 0, 2)).reshape(DEPTH, 6 * d)

    def mods(layer):
        return [mod[layer:layer + 1, j * d:(j + 1) * d] for j in range(6)]

    x0 = x.reshape(SEQ, d)
    target = loss_target.reshape(SEQ, d)
    qg = attn_q_gain.reshape(len(GROUPS), 1, HEAD_DIM)
    kg = attn_k_gain.reshape(len(GROUPS), 1, HEAD_DIM)
    bias = _attn_bias(rel_bias, "attn_bias")
    lb1 = _lower_bounds(hgrn_lower_bounds, "lower_bounds")[1:2]

    def ffn_fwd(layer, x_in, sc2, sh2, g2):
        hf = _norm_mod(x_in, norm_ffn[layer:layer + 1], sc2, sh2, f"l{layer}_norm_ffn")
        a1, a3, u = _ffn_up(hf, wg["ffn_w1"], wg["ffn_w3"], layer, f"l{layer}_ffn_up")
        z, x_out = _mm_rows(u, wg["ffn_w2"], layer, x_in, g2, f"l{layer}_ffn_down")
        return x_out, (hf, a1, a3, u, z)

    def ffn_bwd(layer, dz, dg2, dx_out, x_in, sc2, sh2, saved, mixer_branch, halves=()):
        hf, a1, a3, u, _ = saved
        da1, da3, *recv = _ffn_down_bwd(dz, wg["ffn_w2"], layer, a1, a3, f"l{layer}_ffn_down_bwd", halves=halves)
        dw2 = _mm_rows_bwd_w(u, dz, f"l{layer}_dw2")
        dh = _ffn_up_bwd(da1, da3, wg["ffn_w1"], wg["ffn_w3"], layer, f"l{layer}_ffn_up_bwd")
        dw1, dw3 = _mm_rows_bwd_w_multi([da1, da3], hf, f"l{layer}_dw13")
        dx_in, dsc2, dsh2, dnf, dz_mix, dg_mix = _norm_mod_bwd(x_in, norm_ffn[layer:layer + 1], sc2, sh2, dh, dx_out,
                                                               f"l{layer}_norm_ffn_bwd", branch=mixer_branch)
        return dx_in, (dw1, dw3, dw2), (dsh2, dsc2, dg2), dnf, recv, dz_mix, dg_mix

    def rs_batched(fn, prefix, tags, *columns):
        out = [None] * len(tags)
        by_shape = {}
        for idx, arr in enumerate(columns[0]):
            by_shape.setdefault(arr.shape, []).append(idx)
        for idxs in by_shape.values():
            for lo in range(0, len(idxs), 3):
                sel = idxs[lo:lo + 3]
                k, layer = tags[sel[0]]
                res = fn(place, *[[col[i] for i in sel] for col in columns], f"{prefix}_{k}_{layer}_x{len(sel)}")
                for i, r in zip(sel, res):
                    out[i] = r
        return out

    def rs_add(tags, grads_in, recv):
        return rs_batched(_rs_add_cast, "rs_add", tags, grads_in, list(recv))

    sh1_0, sc1_0, g1_0, sh2_0, sc2_0, g2_0 = mods(0)
    h0 = _norm_mod(x0, norm_mix[0:1], sc1_0, sh1_0, "l0_norm_mix")
    w_qkv9 = _retile_cols(wg["attn_w_qkv"].reshape(N_CHIP, d, 2304), n_out=9, width_out=d, tn=256,
                          src_map=_qkv_chip_map, dst_map=_qkv_group_map, n_tiles=36,
                          name="regroup_w_qkv").reshape(9, 1, d, d)
    qkv9 = _mm_cols(h0, w_qkv9, 0, n_blocks=9, width=d, tn=d, act_map=_block_map, w_map=_block_map,
                    out_dtype=F32, name="l0_qkv")
    o4, lse, *late = _attn_fwd(qkv9, qg, kg, bias, "l0_attn", gather=[slabs16[k] for k in late_names])
    wg.update(zip(late_names, _gather_weights(late, "gather_late_siblings", ici=False)))
    y0, x1 = _mm_rows(o4, wg["attn_w_out"], 0, x0, g1_0, "l0_attn_out")
    x2, ffn0 = ffn_fwd(0, x1, sc2_0, sh2_0, g2_0)

    sh1_1, sc1_1, g1_1, sh2_1, sc2_1, g2_1 = mods(1)
    h1 = _norm_mod(x2, norm_mix[1:2], sc1_1, sh1_1, "l1_norm_mix")
    proj4 = _mm_cols(h1, wg["hgrn_w_in"], 0, n_blocks=4, width=d, tn=512, act_map=_hin_map, w_map=_hin_map,
                     out_dtype=F32, name="l1_hgrn_in")
    o_raw, yg4, states = _hgrn_fwd(proj4, lb1, hgrn_gnorm, "l1_hgrn")
    y1, x3 = _mm_rows(yg4, wg["hgrn_w_out"], 0, x2, g1_1, "l1_hgrn_out")
    x4, ffn1 = ffn_fwd(1, x3, sc2_1, sh2_1, g2_1)

    dx4, loss_part, dz_ffn1, dg2_1 = _loss_head(x4, target, ffn1[4], g2_1, "loss_head")
    loss = lax.psum(loss_part[0, 0], ("x", "y", "c"))

    dx3, (dw1_1, dw3_1, dw2_1), dmod2_1, dnf_1, _, dzm1, dg1_1 = ffn_bwd(
        1, dz_ffn1, dg2_1, dx4, x3, sc2_1, sh2_1, ffn1, (y1, g1_1))
    dyg4 = _mm_rows_bwd_a(dzm1, wg["hgrn_w_out"], 0, "l1_hgrn_out_bwd")
    dw_hout = _mm_rows_bwd_w(yg4, dzm1, "l1_dw_hgrn_out")
    dproj4, dlb_h, dgn_h = _hgrn_bwd(proj4, lb1, hgrn_gnorm, o_raw, dyg4, states, "l1_hgrn_bwd")
    dh1 = _mm_cols_bwd_a(dproj4, wg["hgrn_w_in"], 0, group=N_CHIP, name="l1_hgrn_in_bwd", tm=512)
    dw_hin = _mm_cols_bwd_w(h1, dproj4, ns=d, tn=d, act_map=_block_map, w_map=_block_map, n_tiles=N_CHIP,
                            name="l1_dw_hgrn_in", tm=2048)
    dx2, dsc1_1, dsh1_1, dnm_1, dz_ffn0, dg2_0 = _norm_mod_bwd(x2, norm_mix[1:2], sc1_1, sh1_1, dh1, dx3,
                                                               "l1_norm_mix_bwd", branch=(ffn0[4], g2_0))

    tags_1 = [("hgrn_w_in", 0), ("hgrn_w_out", 0), ("ffn_w1", 1), ("ffn_w3", 1), ("ffn_w2", 1)]
    grads_1 = [dw_hin, dw_hout, dw1_1, dw3_1, dw2_1]
    dx1, (dw1_0, dw3_0, dw2_0), dmod2_0, dnf_0, recv_1, dzm0, dg1_0 = ffn_bwd(
        0, dz_ffn0, dg2_0, dx2, x1, sc2_0, sh2_0, ffn0, (y0, g1_0), halves=grads_1)
    tags_0 = [("ffn_w1", 0), ("ffn_w3", 0), ("ffn_w2", 0)]
    grads_0 = [dw1_0, dw3_0, dw2_0]
    do4, *recv_0 = _mm_rows_bwd_a(dzm0, wg["attn_w_out"], 0, "l0_attn_out_bwd", halves=grads_0)
    dw_aout = _mm_rows_bwd_w(o4, dzm0, "l0_dw_attn_out")
    tags_a = tags_1 + tags_0
    parts_a = rs_add(tags_1, grads_1, recv_1) + rs_add(tags_0, grads_0, recv_0)
    dqkv, dqg_h, dkg_h, dbias, *got_a = _attn_bwd(qkv9, qg, kg, bias, do4, o4, lse, "l0_attn_bwd", scatter=parts_a)
    dqkv9 = dqkv.reshape(9, SEQ, d)
    dw_qkv9 = _mm_cols_bwd_w(h0, dqkv9, ns=d, tn=d, act_map=_block_map, w_map=_block_map, n_tiles=9,
                             name="l0_dw_qkv", tm=2048, n_out=9)
    dw_qkv = _retile_cols(dw_qkv9, n_out=N_CHIP, width_out=2304, tn=256, src_map=_qkv_group_map,
                          dst_map=_qkv_chip_map, n_tiles=36, name="regroup_dw_qkv")
    tags_b = [("attn_w_qkv", 0), ("attn_w_out", 0)]
    grads_b = [dw_qkv, dw_aout]
    parts_b = rs_add(tags_b, grads_b, _rs_exchange_halves(grads_b, "rs_exchange_halves_b"))
    dh0, *got_b = _mm_cols_bwd_a(dqkv9, w_qkv9, 0, group=3, name="l0_qkv_bwd", scatter=parts_b)
    dx0, dsc1_0, dsh1_0, dnm_0 = _norm_mod_bwd(x0, norm_mix[0:1], sc1_0, sh1_0, dh0, dx1, "l0_norm_mix_bwd")
    drb8 = _relbias_bwd(dbias, jnp.asarray(_bias_tables()), "rel_bias_bwd")

    small = _pack_rows([
        dsh1_0, dsc1_0, dg1_0, *dmod2_0, dsh1_1, dsc1_1, dg1_1, *dmod2_1,
        dnm_0, dnm_1, dnf_0, dnf_1,
        jnp.transpose(dqg_h, (1, 0, 2, 3)), jnp.transpose(dkg_h, (1, 0, 2, 3)), dgn_h, dlb_h, drb8])
    small_all = _small_allgather(small, "gather_small")
    main, gains, dlbnd, rbt = _small_totals(small_all, hgrn_lower_bounds.reshape(DEPTH, 8, 128), "small_totals")
    ng = len(GROUPS)
    grads = {
        "ada_b": main[_R_DMOD:_R_NMIX].reshape(DEPTH, 6 * d),
        "norm_mix": main[_R_NMIX:_R_NFFN].reshape(DEPTH, d),
        "norm_ffn": main[_R_NFFN:_R_QG].reshape(DEPTH, d),
        "attn_q_gain": gains[0:ng].reshape(1, ng, HEAD_DIM),
        "attn_k_gain": gains[ng:2 * ng].reshape(1, ng, HEAD_DIM),
        "hgrn_gnorm": gains[2 * ng:2 * ng + 1],
        "hgrn_lower_bounds": dlbnd.reshape(DEPTH, d),
        "rel_bias": jnp.transpose(rbt[:, :ng * NUM_BUCKETS].reshape(HEADS, ng, NUM_BUCKETS), (2, 1, 0))
                       .reshape(NUM_BUCKETS, ng * HEADS),
    }
    dmod_all = small_all[:, _R_DMOD:_R_NMIX].reshape(N_DEV, DEPTH, 6 * d)
    dmod_cols = jnp.transpose(lax.dynamic_slice(dmod_all, (0, 0, chip * ADA_SHARD), (N_DEV, DEPTH, ADA_SHARD)),
                              (1, 0, 2))
    grad_ada_w = _ada_bwd(c_all, dmod_cols, "ada_bwd")

    tags = tags_a + tags_b
    halves = rs_batched(_rs_sum4, "rs_sum", tags, parts_a + parts_b, list(got_a) + list(got_b))
    full = dict(zip(tags, _rs_join_halves(halves, "rs_join_halves")))

    out_g, out_d, out_m, out_v = {}, {}, {}, {}
    for k in big_names:
        gs = [full[(k, layer)] for layer in range(weights[k].shape[0])]
        out_g[k], out_d[k], out_m[k], out_v[k] = _adamw(weights[k], gs, mom1[k], mom2[k], "adamw_" + k)
    shp = (1, DEPTH * d, ADA_SHARD)
    res = _adamw(ada_w.reshape(shp), [grad_ada_w.reshape(shp[1:])], m_ada_w.reshape(shp), v_ada_w.reshape(shp),
                 "adamw_ada_w")
    out_g["ada_w"], out_d["ada_w"], out_m["ada_w"], out_v["ada_w"] = [r.reshape(ada_w.shape) for r in res]
    for k in _SMALL_ORDER:
        shp = (1, weights[k].size // weights[k].shape[-1], weights[k].shape[-1])
        res = _adamw(weights[k].reshape(shp), [grads[k].reshape(shp[1:])], mom1[k].reshape(shp),
                     mom2[k].reshape(shp), "adamw_" + k)
        out_g[k], out_d[k], out_m[k], out_v[k] = [r.reshape(weights[k].shape) for r in res]
    for dst in (out_g, out_d, out_m, out_v):
        for k in transposed:
            dst[k] = jnp.transpose(dst[k], (0, 2, 1))

    return (loss, dx0.reshape(x.shape), *[out_g[k] for k in _WEIGHT_ORDER], *[out_d[k] for k in _WEIGHT_ORDER],
            *[out_m[k] for k in _WEIGHT_ORDER], *[out_v[k] for k in _WEIGHT_ORDER])
```

```python
import functools

import numpy as np
import jax
import jax.numpy as jnp
from jax import lax
from jax.experimental import pallas as pl
from jax.experimental.pallas import tpu as pltpu

F32 = jnp.float32
BF16 = jnp.bfloat16

D_MODEL = 1024
SEQ = 4096
N_DEV = 8
N_CHIP = 4
DEPTH = 2
HEADS = 8
HEAD_DIM = 128
GROUPS = ((128, 1), (512, 4), (2048, 16))
ATT_BLK = 128
ATT_WAYS = 4
ATT_STEPS = SEQ // ATT_BLK // ATT_WAYS
NUM_BUCKETS = 32
MAX_DISTANCE = 2048
FFN_HIDDEN = 2816
FFN_SHARD = FFN_HIDDEN // N_CHIP
HG_SUB = 16
HG_TC = 512
HG_HP = 4
RMS_EPS = 1e-6
NEG = -1e30
ATT_SCALE = HEAD_DIM ** -0.5
LOG2_E = 1.4426950408889634
ADAM_LR, ADAM_B1, ADAM_B2, ADAM_EPS, ADAM_WD, ADAM_STEP = 0.001, 0.9, 0.999, 1e-08, 0.01, 10
VMEM_LIMIT = 56 * 1024 * 1024
MESH = pl.DeviceIdType.MESH


def _pcall(body, **kw):
    return pl.pallas_call(body, **kw)


def _cparams(sem=None):
    if sem is None:
        return pltpu.CompilerParams(vmem_limit_bytes=VMEM_LIMIT)
    return pltpu.CompilerParams(dimension_semantics=sem, vmem_limit_bytes=VMEM_LIMIT)


def _sds(shape, dtype):
    return jax.ShapeDtypeStruct(shape, dtype)


def _dot(a, b):
    return jnp.dot(a, b, preferred_element_type=F32)


def _dot_nt(a, b):
    return lax.dot_general(a, b, (((1,), (1,)), ((), ())), preferred_element_type=F32)


def _dot_tn(a, b):
    return lax.dot_general(a, b, (((0,), (0,)), ((), ())), preferred_element_type=F32)


def _sigmoid(x):
    return 1.0 / (1.0 + jnp.exp(-x))


def _silu(x):
    return x * _sigmoid(x)


def _dsilu(x):
    s = _sigmoid(x)
    return s * (1.0 + x * (1.0 - s))


def _norm_mod(x_ref, g_ref, sc_ref, sh_ref):
    xv = x_ref[...]
    rs = lax.rsqrt(jnp.mean(xv * xv, axis=-1, keepdims=True) + RMS_EPS)
    return ((xv * rs * g_ref[...]) * (1.0 + sc_ref[...]) + sh_ref[...]).astype(BF16)


_NORM_SPECS = [pl.BlockSpec((1, D_MODEL), lambda i, t: (0, 0))] * 3


def _gated_branch_bwd(dx, z_ref, gate_ref, dz_ref, dgate_ref):
    dz_ref[...] = (dx * gate_ref[...]).astype(BF16)
    dgate_ref[...] += jnp.sum(dx * z_ref[...], axis=0, keepdims=True)


def _norm_mod_bwd(x, gain, sc, sh, dh, dres, name, branch=None):
    tm = 512
    n_b = 2 if branch else 0

    def body(*refs):
        x_ref, g_ref, sc_ref, sh_ref, dh_ref, dres_ref = refs[:6]
        dx_ref, dsc_ref, dsh_ref, dg_ref = refs[6 + n_b:10 + n_b]

        @pl.when(pl.program_id(0) == 0)
        def _():
            dsc_ref[...] = jnp.zeros_like(dsc_ref)
            dsh_ref[...] = jnp.zeros_like(dsh_ref)
            dg_ref[...] = jnp.zeros_like(dg_ref)
            if branch:
                refs[11 + n_b][...] = jnp.zeros_like(refs[11 + n_b])

        xv = x_ref[...]
        dhv = dh_ref[...]
        rs = lax.rsqrt(jnp.mean(xv * xv, axis=-1, keepdims=True) + RMS_EPS)
        xh = xv * rs
        dsc_ref[...] += jnp.sum(dhv * (xh * g_ref[...]), axis=0, keepdims=True)
        dsh_ref[...] += jnp.sum(dhv, axis=0, keepdims=True)
        dhn = dhv * (1.0 + sc_ref[...])
        dg_ref[...] += jnp.sum(dhn * xh, axis=0, keepdims=True)
        dxh = dhn * g_ref[...]
        dx = dres_ref[...] + rs * (dxh - xh * jnp.mean(dxh * xh, axis=-1, keepdims=True))
        dx_ref[...] = dx
        if branch:
            _gated_branch_bwd(dx, refs[6], refs[7], refs[10 + n_b], refs[11 + n_b])

    vec = pl.BlockSpec((1, D_MODEL), lambda i: (0, 0))
    big = pl.BlockSpec((tm, D_MODEL), lambda i: (i, 0))
    return _pcall(
        body, name=name, grid=(SEQ // tm,),
        in_specs=[big, vec, vec, vec, big, big] + ([big, vec] if branch else []),
        out_specs=[big, vec, vec, vec] + ([big, vec] if branch else []),
        out_shape=[_sds((SEQ, D_MODEL), F32)] + [_sds((1, D_MODEL), F32)] * 3
        + ([_sds((SEQ, D_MODEL), BF16), _sds((1, D_MODEL), F32)] if branch else []),
        compiler_params=_cparams(("arbitrary",)),
    )(x, gain, sc, sh, dh, dres, *(branch or ()))


def _mm_cols(x, gain, sc, sh, wg, layer, *, n_blocks, width, tn, act_map, w_map, out_dtype, name, tm=1024):
    k = x.shape[1]
    n_tiles = n_blocks * width // tn

    def body(x_ref, g_ref, sc_ref, sh_ref, w_ref, o_ref, h_ref):
        @pl.when(pl.program_id(1) == 0)
        def _():
            h_ref[...] = _norm_mod(x_ref, g_ref, sc_ref, sh_ref)

        o_ref[...] = _dot(h_ref[...], w_ref[...]).astype(o_ref.dtype)

    rows = pl.BlockSpec((tm, k), lambda i, t: (i, 0))
    return _pcall(
        body, name=name, grid=(SEQ // tm, n_tiles),
        in_specs=[rows] + _NORM_SPECS
        + [pl.BlockSpec((None, None, k, tn), lambda i, t: (w_map(t)[0], layer, 0, w_map(t)[1]))],
        out_specs=[pl.BlockSpec((None, tm, tn), lambda i, t: (act_map(t)[0], i, act_map(t)[1])), rows],
        out_shape=[_sds((n_blocks, SEQ, width), out_dtype), _sds((SEQ, k), BF16)],
        compiler_params=_cparams(("parallel", "arbitrary")),
    )(x, gain, sc, sh, wg)


def _mm_cols_bwd_a(dout, wg, layer, *, group, name, tm=1024, scatter=()):
    n_blocks, _, width = dout.shape
    k = wg.shape[2]
    n_s = len(scatter)
    n_rows = SEQ // tm
    n_steps = n_blocks // group

    def body(*refs):
        d_ref, w_ref = refs[:2]
        o_ref = refs[2 + n_s]
        if n_s:
            comm_start, comm_wait = _rs_chips(refs[2:2 + n_s], refs[3 + n_s:3 + 2 * n_s], *refs[3 + 2 * n_s:])
            pl.when((pl.program_id(0) == 0) & (pl.program_id(1) == 0))(comm_start)
        acc = _dot_nt(d_ref[0], w_ref[0])
        for b in range(1, group):
            acc += _dot_nt(d_ref[b], w_ref[b])
        if n_steps == 1:
            o_ref[...] = acc
        else:
            @pl.when(pl.program_id(1) == 0)
            def _():
                o_ref[...] = acc

            @pl.when(pl.program_id(1) > 0)
            def _():
                o_ref[...] += acc
        if n_s:
            pl.when((pl.program_id(0) == n_rows - 1) & (pl.program_id(1) == n_steps - 1))(comm_wait)

    sem = pltpu.SemaphoreType.DMA((max(n_s, 1), 3))
    res = _pcall(
        body, name=name, grid=(n_rows, n_steps),
        in_specs=[pl.BlockSpec((group, tm, width), lambda i, t: (t, i, 0)),
                  pl.BlockSpec((group, None, k, width), lambda i, t: (t, layer, 0, 0))] + [_ANY] * n_s,
        out_specs=[pl.BlockSpec((tm, k), lambda i, t: (i, 0))] + [_ANY] * n_s,
        out_shape=[_sds((SEQ, k), F32)] + _rs_chips_shapes(scatter),
        scratch_shapes=[sem, sem] if n_s else [],
        compiler_params=_cparams(("arbitrary", "arbitrary") if n_s else ("parallel", "arbitrary")),
    )(dout, wg, *scatter)
    return res if n_s else res[0]


def _mm_cols_bwd_w(a, dout, *, ns, tn, act_map, w_map, n_tiles, name, tm=1024, n_out=N_CHIP):
    k = a.shape[1]

    def body(a_ref, d_ref, o_ref):
        @pl.when(pl.program_id(1) == 0)
        def _():
            o_ref[...] = jnp.zeros_like(o_ref)

        o_ref[...] += _dot_tn(a_ref[...], d_ref[...])

    return _pcall(
        body, name=name, grid=(n_tiles, SEQ // tm),
        in_specs=[pl.BlockSpec((tm, k), lambda t, i: (i, 0)),
                  pl.BlockSpec((None, tm, tn), lambda t, i: (act_map(t)[0], i, act_map(t)[1]))],
        out_specs=pl.BlockSpec((None, k, tn), lambda t, i: (w_map(t)[0], 0, w_map(t)[1])),
        out_shape=_sds((n_out, k, ns), F32),
        compiler_params=_cparams(("parallel", "arbitrary")),
    )(a, dout)


def _retile_cols(src, *, n_out, width_out, tn, src_map, dst_map, n_tiles, name):
    k = src.shape[1]

    def body(s_ref, o_ref):
        o_ref[...] = s_ref[...]

    return _pcall(
        body, name=name, grid=(n_tiles,),
        in_specs=[pl.BlockSpec((None, k, tn), lambda t: (src_map(t)[0], 0, src_map(t)[1]))],
        out_specs=pl.BlockSpec((None, k, tn), lambda t: (dst_map(t)[0], 0, dst_map(t)[1])),
        out_shape=_sds((n_out, k, width_out), src.dtype),
        compiler_params=_cparams(("parallel",)),
    )(src)


def _mm_rows(a4, wg, layer, x, gate, name, tm=512):
    ks = a4.shape[2]
    n = wg.shape[3]

    def body(a_ref, w_ref, x_ref, g_ref, z_ref, xn_ref):
        z = _dot(a_ref[0], w_ref[0])
        for s in range(1, N_CHIP):
            z += _dot(a_ref[s], w_ref[s])
        z_ref[...] = z.astype(BF16)
        xn_ref[...] = x_ref[...] + g_ref[...] * z

    big = pl.BlockSpec((tm, n), lambda i: (i, 0))
    return _pcall(
        body, name=name, grid=(SEQ // tm,),
        in_specs=[pl.BlockSpec((N_CHIP, tm, ks), lambda i: (0, i, 0)),
                  pl.BlockSpec((N_CHIP, None, ks, n), lambda i: (0, layer, 0, 0)),
                  big, pl.BlockSpec((1, n), lambda i: (0, 0))],
        out_specs=[big, big],
        out_shape=[_sds((SEQ, n), BF16), _sds((SEQ, n), F32)],
        compiler_params=_cparams(("parallel",)),
    )(a4, wg, x, gate)


def _mm_rows_bwd_a(dz, wg, layer, name, tm=1024, halves=()):
    ks, n = wg.shape[2], wg.shape[3]
    n_h = len(halves)
    n_rows = SEQ // tm

    def body(*refs):
        dz_ref, w_ref = refs[:2]
        o_ref = refs[2 + n_h]
        if n_h:
            comm_start, comm_wait = _rs_halves(refs[2:2 + n_h], refs[3 + n_h:3 + 2 * n_h], *refs[3 + 2 * n_h:])
            pl.when((pl.program_id(0) == 0) & (pl.program_id(1) == 0))(comm_start)
        o_ref[...] = _dot_nt(dz_ref[...], w_ref[...])
        if n_h:
            pl.when((pl.program_id(0) == n_rows - 1) & (pl.program_id(1) == N_CHIP - 1))(comm_wait)

    sem = pltpu.SemaphoreType.DMA((max(n_h, 1),))
    res = _pcall(
        body, name=name, grid=(n_rows, N_CHIP),
        in_specs=[pl.BlockSpec((tm, n), lambda i, s: (i, 0)),
                  pl.BlockSpec((None, None, ks, n), lambda i, s: (s, layer, 0, 0))] + [_ANY] * n_h,
        out_specs=[pl.BlockSpec((None, tm, ks), lambda i, s: (s, i, 0))] + [_ANY] * n_h,
        out_shape=[_sds((N_CHIP, SEQ, ks), F32)] + _rs_halves_shapes(halves),
        scratch_shapes=[sem, sem] if n_h else [],
        compiler_params=_cparams(("arbitrary", "arbitrary") if n_h else ("parallel", "arbitrary")),
    )(dz, wg, *halves)
    return res if n_h else res[0]


def _mm_rows_bwd_w(a4, dz, name, tm=2048):
    return _mm_rows_bwd_w_multi([a4], dz, name, tm)[0]


def _mm_rows_bwd_w_multi(a4s, dz, name, tm=2048):
    n_a = len(a4s)
    ks = a4s[0].shape[2]
    n = dz.shape[1]

    def body(*refs):
        dz_ref = refs[n_a]

        for j in range(n_a):
            o_ref = refs[n_a + 1 + j]

            @pl.when(pl.program_id(1) == 0)
            def _(o_ref=o_ref):
                o_ref[...] = jnp.zeros_like(o_ref)

            o_ref[...] += _dot_tn(refs[j][...], dz_ref[...])

    return _pcall(
        body, name=name, grid=(N_CHIP, SEQ // tm),
        in_specs=[pl.BlockSpec((None, tm, ks), lambda s, i: (s, i, 0))] * n_a
        + [pl.BlockSpec((tm, n), lambda s, i: (i, 0))],
        out_specs=[pl.BlockSpec((None, ks, n), lambda s, i: (s, 0, 0))] * n_a,
        out_shape=[_sds((N_CHIP, ks, n), F32)] * n_a,
        compiler_params=_cparams(("parallel", "arbitrary")),
    )(*a4s, dz)


def _ffn_up(x, gain, sc, sh, w1g, w3g, layer, name, tm=1024):
    def body(x_ref, g_ref, sc_ref, sh_ref, w1_ref, w3_ref, a1_ref, a3_ref, u_ref, h_ref):
        @pl.when(pl.program_id(1) == 0)
        def _():
            h_ref[...] = _norm_mod(x_ref, g_ref, sc_ref, sh_ref)

        hv = h_ref[...]
        a1 = _dot_nt(hv, w1_ref[...])
        a3 = _dot_nt(hv, w3_ref[...])
        a1_ref[...] = a1.astype(BF16)
        a3_ref[...] = a3.astype(BF16)
        u_ref[...] = (_silu(a1) * a3).astype(BF16)

    wspec = pl.BlockSpec((None, None, FFN_SHARD, D_MODEL), lambda i, s: (s, layer, 0, 0))
    ospec = pl.BlockSpec((None, tm, FFN_SHARD), lambda i, s: (s, i, 0))
    shp = (N_CHIP, SEQ, FFN_SHARD)
    rows = pl.BlockSpec((tm, D_MODEL), lambda i, s: (i, 0))
    return _pcall(
        body, name=name, grid=(SEQ // tm, N_CHIP),
        in_specs=[rows] + _NORM_SPECS + [wspec, wspec],
        out_specs=[ospec, ospec, ospec, rows],
        out_shape=[_sds(shp, BF16), _sds(shp, BF16), _sds(shp, BF16), _sds((SEQ, D_MODEL), BF16)],
        compiler_params=_cparams(("parallel", "arbitrary")),
    )(x, gain, sc, sh, w1g, w3g)


def _ffn_up_bwd(da1, da3, w1g, w3g, layer, name, tm=512):
    def body(d1_ref, d3_ref, w1_ref, w3_ref, o_ref):
        acc = _dot(d1_ref[0], w1_ref[0]) + _dot(d3_ref[0], w3_ref[0])
        for s in range(1, N_CHIP):
            acc += _dot(d1_ref[s], w1_ref[s]) + _dot(d3_ref[s], w3_ref[s])
        o_ref[...] = acc

    wspec = pl.BlockSpec((N_CHIP, None, FFN_SHARD, D_MODEL), lambda i: (0, layer, 0, 0))
    dspec = pl.BlockSpec((N_CHIP, tm, FFN_SHARD), lambda i: (0, i, 0))
    return _pcall(
        body, name=name, grid=(SEQ // tm,),
        in_specs=[dspec, dspec, wspec, wspec],
        out_specs=pl.BlockSpec((tm, D_MODEL), lambda i: (i, 0)),
        out_shape=_sds((SEQ, D_MODEL), F32),
        compiler_params=_cparams(("parallel",)),
    )(da1, da3, w1g, w3g)


def _ffn_down_bwd(dz, w2g, layer, a1, a3, name, tm=1024, halves=()):
    n_h = len(halves)
    n_rows = SEQ // tm

    def body(*refs):
        dz_ref, w_ref, a1_ref, a3_ref = refs[:4]
        da1_ref, da3_ref = refs[4 + n_h:6 + n_h]
        if n_h:
            comm_start, comm_wait = _rs_halves(refs[4:4 + n_h], refs[6 + n_h:6 + 2 * n_h], *refs[6 + 2 * n_h:])
            pl.when((pl.program_id(0) == 0) & (pl.program_id(1) == 0))(comm_start)
        du = _dot_nt(dz_ref[...], w_ref[...])
        a1 = a1_ref[...].astype(F32)
        da1_ref[...] = (du * a3_ref[...].astype(F32) * _dsilu(a1)).astype(BF16)
        da3_ref[...] = (du * _silu(a1)).astype(BF16)
        if n_h:
            pl.when((pl.program_id(0) == n_rows - 1) & (pl.program_id(1) == N_CHIP - 1))(comm_wait)

    blk = pl.BlockSpec((None, tm, FFN_SHARD), lambda i, s: (s, i, 0))
    shp = (N_CHIP, SEQ, FFN_SHARD)
    sem = pltpu.SemaphoreType.DMA((max(n_h, 1),))
    return _pcall(
        body, name=name, grid=(n_rows, N_CHIP),
        in_specs=[pl.BlockSpec((tm, D_MODEL), lambda i, s: (i, 0)),
                  pl.BlockSpec((None, None, FFN_SHARD, D_MODEL), lambda i, s: (s, layer, 0, 0)),
                  blk, blk] + [_ANY] * n_h,
        out_specs=[blk, blk] + [_ANY] * n_h,
        out_shape=[_sds(shp, BF16), _sds(shp, BF16)] + _rs_halves_shapes(halves),
        scratch_shapes=[sem, sem] if n_h else [],
        compiler_params=_cparams(("arbitrary", "arbitrary") if n_h else ("parallel", "arbitrary")),
    )(dz, w2g, a1, a3, *halves)


def _loss_head(y, target, z, gate, name):
    tm = 512
    n_steps = SEQ // tm

    def body(y_ref, t_ref, z_ref, gate_ref, dy_ref, l_ref, dz_ref, dgate_ref, acc_ref):
        @pl.when(pl.program_id(0) == 0)
        def _():
            acc_ref[...] = jnp.zeros_like(acc_ref)
            dgate_ref[...] = jnp.zeros_like(dgate_ref)

        err = y_ref[...] - t_ref[...]
        dy = err * (1.0 / D_MODEL)
        dy_ref[...] = dy
        acc_ref[...] += jnp.sum(jnp.mean(err * err, axis=-1, keepdims=True), axis=0, keepdims=True)
        _gated_branch_bwd(dy, z_ref, gate_ref, dz_ref, dgate_ref)

        @pl.when(pl.program_id(0) == n_steps - 1)
        def _():
            l_ref[...] = 0.5 * acc_ref[...]

    big = pl.BlockSpec((tm, D_MODEL), lambda i: (i, 0))
    vec = pl.BlockSpec((1, D_MODEL), lambda i: (0, 0))
    return _pcall(
        body, name=name, grid=(n_steps,),
        in_specs=[big, big, big, vec],
        out_specs=[big, pl.BlockSpec((1, 1), lambda i: (0, 0)), big, vec],
        out_shape=[_sds((SEQ, D_MODEL), F32), _sds((1, 1), F32), _sds((SEQ, D_MODEL), BF16),
                   _sds((1, D_MODEL), F32)],
        scratch_shapes=[pltpu.VMEM((1, 1), F32)],
        compiler_params=_cparams(("arbitrary",)),
    )(y, target, z, gate)


def _attn_rows(base, d):
    if d == 1:
        return pl.ds(pl.multiple_of(base, ATT_BLK), ATT_BLK)
    return pl.ds(base, ATT_BLK, stride=d)


def _attn_block_index(i, d):
    nb = SEQ // (ATT_BLK * d)
    r = i // nb
    n = i % nb
    base = r + n * (ATT_BLK * d)
    pbase = jnp.maximum(base - ATT_BLK * d, r)
    return n, _attn_rows(base, d), _attn_rows(pbase, d)


def _attn_two_blocks(ref, prow, rows):
    return jnp.concatenate([ref[prow, :].astype(BF16), ref[rows, :].astype(BF16)], axis=0)


def _attn_block_bias(b_ref, n):
    b = b_ref[...]
    prev_half = lax.broadcasted_iota(jnp.int32, b.shape, 1) < ATT_BLK
    return jnp.where(prev_half & (n == 0), NEG, b)


def _qk_normed(x):
    rs = lax.rsqrt(jnp.mean(x * x, axis=-1, keepdims=True) + RMS_EPS)
    return x * rs, rs


def _attn_fwd(qkv9, qgain, kgain, bias, name, gather=()):
    n_g = len(gather)

    def body(*refs):
        q_ref, k_ref, v_ref, qg_ref, kg_ref, b_ref = refs[:6]
        o_ref, lse_ref = refs[6 + n_g:8 + n_g]
        qn_s, kn_s, acc_s, m_s, l_s = refs[8 + 2 * n_g:13 + 2 * n_g]
        g = pl.program_id(1)
        if n_g:
            comm_start, comm_wait = _gather_ici(refs[8 + n_g:8 + 2 * n_g], *refs[13 + 2 * n_g:])
            pl.when((pl.program_id(0) == 0) & (g == 0))(comm_start)

        @pl.when(g == 0)
        def _():
            m_s[...] = jnp.full_like(m_s, NEG)
            l_s[...] = jnp.zeros_like(l_s)
            acc_s[...] = jnp.zeros_like(acc_s)

        qn_s[...] = _qk_normed(q_ref[...])[0] * qg_ref[...]
        kn_s[...] = _qk_normed(k_ref[...])[0] * kg_ref[...]

        for gi, (_, d) in enumerate(GROUPS):
            @pl.when(g == gi)
            def _(d=d):
                def block(n, qb, kk, vv, m_old, l_old, acc_old):
                    s = _dot_nt(qb, kk) * ATT_SCALE + _attn_block_bias(b_ref, n)
                    m_new = jnp.maximum(m_old, jnp.max(s, axis=-1, keepdims=True))
                    alpha = jnp.exp(m_old - m_new)
                    p = jnp.exp(s - m_new)
                    l_new = alpha * l_old + jnp.sum(p, axis=-1, keepdims=True)
                    acc_new = alpha * acc_old + _dot(p.astype(BF16), vv)
                    return m_new, l_new, acc_new

                def it(i, carry):
                    where, loaded = [], []
                    for way in range(ATT_WAYS):
                        n, rows, prow = _attn_block_index(i + way * ATT_STEPS, d)
                        where.append(rows)
                        loaded.append((n, qn_s[rows, :].astype(BF16), _attn_two_blocks(kn_s, prow, rows),
                                       _attn_two_blocks(v_ref, prow, rows), m_s[rows, :], l_s[rows, :],
                                       acc_s[rows, :]))
                    results = [block(*vals) for vals in loaded]
                    for rows, (m_new, l_new, acc_new) in zip(where, results):
                        m_s[rows, :] = m_new
                        l_s[rows, :] = l_new
                        acc_s[rows, :] = acc_new
                    return carry

                lax.fori_loop(0, ATT_STEPS, it, 0)

        @pl.when(g == len(GROUPS) - 1)
        def _():
            o_ref[...] = (acc_s[...] / l_s[...]).astype(BF16)
            lse_ref[...] = m_s[...] + jnp.log(l_s[...])

        if n_g:
            pl.when((pl.program_id(0) == HEADS - 1) & (g == len(GROUPS) - 1))(comm_wait)

    def col(j):
        return pl.BlockSpec((None, SEQ, HEAD_DIM), lambda h, g: (g * 3 + j, 0, h))

    gspec = pl.BlockSpec((None, 1, HEAD_DIM), lambda h, g: (g, 0, 0))
    sem = pltpu.SemaphoreType.DMA((max(n_g, 1), 3))
    return _pcall(
        body, name=name, grid=(HEADS, len(GROUPS)),
        in_specs=[col(0), col(1), col(2), gspec, gspec,
                  pl.BlockSpec((None, None, ATT_BLK, 2 * ATT_BLK), lambda h, g: (g, h, 0, 0))] + [_ANY] * n_g,
        out_specs=[pl.BlockSpec((None, SEQ, HEAD_DIM), lambda h, g: (h // 2, 0, h % 2)),
                   pl.BlockSpec((None, SEQ, 1), lambda h, g: (h, 0, 0))] + [_ANY] * n_g,
        out_shape=[_sds((N_CHIP, SEQ, 2 * HEAD_DIM), BF16), _sds((HEADS, SEQ, 1), F32)]
        + [_sds(s.shape, s.dtype) for s in gather],
        input_output_aliases={6 + a: 2 + a for a in range(n_g)},
        scratch_shapes=[pltpu.VMEM((SEQ, HEAD_DIM), F32)] * 3 + [pltpu.VMEM((SEQ, 1), F32)] * 2
        + ([sem, sem] if n_g else []),
        compiler_params=_cparams(("arbitrary", "arbitrary")),
    )(qkv9, qkv9, qkv9, qgain, kgain, bias, *gather)


def _attn_bwd(qkv9, qgain, kgain, bias, do4, o4, lse, name, scatter=()):
    n_s = len(scatter)

    def body(*refs):
        q_ref, k_ref, v_ref, qg_ref, kg_ref, b_ref, do_ref, o_ref, lse_ref = refs[:9]
        dqkv_ref, dqg_ref, dkg_ref, db_ref = refs[9 + n_s:13 + n_s]
        qn_s, kn_s, dq_s, dk_s, dv_s, dl_s = refs[13 + 2 * n_s:19 + 2 * n_s]
        g = pl.program_id(1)
        if n_s:
            comm_start, comm_wait = _rs_chips(refs[9:9 + n_s], refs[13 + n_s:13 + 2 * n_s], *refs[19 + 2 * n_s:])
            pl.when((pl.program_id(0) == 0) & (g == 0))(comm_start)
        qh, rq = _qk_normed(q_ref[...])
        kh, rk = _qk_normed(k_ref[...])
        qn_s[...] = qh * qg_ref[...]
        kn_s[...] = kh * kg_ref[...]
        @pl.when(g == 0)
        def _():
            dl_s[...] = jnp.sum(do_ref[...] * o_ref[...].astype(F32), axis=-1, keepdims=True)

        dk_s[...] = jnp.zeros_like(dk_s)
        dv_s[...] = jnp.zeros_like(dv_s)
        db_ref[...] = jnp.zeros_like(db_ref)

        for gi, (_, d) in enumerate(GROUPS):
            @pl.when(g == gi)
            def _(d=d):
                def block(n, qb, kk, vv, dob, lse_b, dl):
                    s = _dot_nt(qb, kk) * ATT_SCALE + _attn_block_bias(b_ref, n)
                    p = jnp.exp(s - lse_b)
                    ds = p * (_dot_nt(dob, vv) - dl)
                    ds16 = ds.astype(BF16)
                    return (ds, _dot(ds16, kk) * ATT_SCALE, _dot_tn(ds16, qb) * ATT_SCALE,
                            _dot_tn(p.astype(BF16), dob))

                def it(i, carry):
                    where, loaded, old = [], [], []
                    for way in range(ATT_WAYS):
                        n, rows, prow = _attn_block_index(i + way * ATT_STEPS, d)
                        where.append((rows, prow))
                        loaded.append((n, qn_s[rows, :].astype(BF16), _attn_two_blocks(kn_s, prow, rows),
                                       _attn_two_blocks(v_ref, prow, rows), do_ref[rows, :].astype(BF16),
                                       lse_ref[rows, :], dl_s[rows, :]))
                        old.append((dk_s[rows, :], dk_s[prow, :], dv_s[rows, :], dv_s[prow, :]))
                    results = [block(*vals) for vals in loaded]
                    db_ref[...] += functools.reduce(lambda a, b: a + b, [r[0] for r in results])
                    for (rows, prow), (dk_c, dk_p, dv_c, dv_p), (_, dq, dkk, dvv) in zip(where, old, results):
                        dq_s[rows, :] = dq
                        dk_s[prow, :] = dk_p + dkk[:ATT_BLK]
                        dv_s[prow, :] = dv_p + dvv[:ATT_BLK]
                        dk_s[rows, :] = dk_c + dkk[ATT_BLK:]
                        dv_s[rows, :] = dv_c + dvv[ATT_BLK:]
                    return carry

                lax.fori_loop(0, ATT_STEPS, it, 0)

        def norm_bwd(dn, xh, rs, gain):
            dgain = jnp.sum(dn * xh, axis=0, keepdims=True)
            dxh = dn * gain
            return rs * (dxh - xh * jnp.mean(dxh * xh, axis=-1, keepdims=True)), dgain

        dq, dqg = norm_bwd(dq_s[...], qh, rq, qg_ref[...])
        dk, dkg = norm_bwd(dk_s[...], kh, rk, kg_ref[...])
        dqkv_ref[0] = dq.astype(BF16)
        dqkv_ref[1] = dk.astype(BF16)
        dqkv_ref[2] = dv_s[...].astype(BF16)
        dqg_ref[...] = dqg
        dkg_ref[...] = dkg
        if n_s:
            pl.when((pl.program_id(0) == HEADS - 1) & (g == len(GROUPS) - 1))(comm_wait)

    def col(j):
        return pl.BlockSpec((None, SEQ, HEAD_DIM), lambda h, g: (g * 3 + j, 0, h))

    gspec = pl.BlockSpec((None, 1, HEAD_DIM), lambda h, g: (g, 0, 0))
    bspec = pl.BlockSpec((None, None, ATT_BLK, 2 * ATT_BLK), lambda h, g: (g, h, 0, 0))
    hcol = pl.BlockSpec((None, SEQ, HEAD_DIM), lambda h, g: (h // 2, 0, h % 2))
    dgspec = pl.BlockSpec((None, None, 1, HEAD_DIM), lambda h, g: (h, g, 0, 0))
    ng = len(GROUPS)
    sem = pltpu.SemaphoreType.DMA((max(n_s, 1), 3))
    return _pcall(
        body, name=name, grid=(HEADS, ng),
        in_specs=[col(0), col(1), col(2), gspec, gspec, bspec, hcol, hcol,
                  pl.BlockSpec((None, SEQ, 1), lambda h, g: (h, 0, 0))] + [_ANY] * n_s,
        out_specs=[pl.BlockSpec((None, 3, SEQ, HEAD_DIM), lambda h, g: (g, 0, 0, h)), dgspec, dgspec, bspec]
        + [_ANY] * n_s,
        out_shape=[_sds((ng, 3, SEQ, D_MODEL), BF16), _sds((HEADS, ng, 1, HEAD_DIM), F32),
                   _sds((HEADS, ng, 1, HEAD_DIM), F32), _sds((ng, HEADS, ATT_BLK, 2 * ATT_BLK), F32)]
        + _rs_chips_shapes(scatter),
        scratch_shapes=[pltpu.VMEM((SEQ, HEAD_DIM), F32)] * 5 + [pltpu.VMEM((SEQ, 1), F32)]
        + ([sem, sem] if n_s else []),
        compiler_params=_cparams(("arbitrary", "arbitrary")),
    )(qkv9, qkv9, qkv9, qgain, kgain, bias, do4, o4, lse, *scatter)


def _relbias_bwd(dbias, bucket_idx, name):
    ng = len(GROUPS)

    def body(db_ref, idx_ref, o_ref):
        lane = lax.broadcasted_iota(jnp.int32, (HEADS, 128), 1)
        acc = jnp.zeros((HEADS, 128), F32)
        for g in range(ng):
            dbg = db_ref[g]
            idx = idx_ref[g]
            for b in range(NUM_BUCKETS):
                sel = jnp.where((idx == b)[None], dbg, 0.0)
                part = jnp.sum(sel, axis=1)
                val = jnp.sum(part, axis=-1, keepdims=True)
                acc = jnp.where(lane == g * NUM_BUCKETS + b, val, acc)
        o_ref[...] = acc

    return _pcall(body, name=name, out_shape=_sds((HEADS, 128), F32), compiler_params=_cparams())(dbias, bucket_idx)


def _scan16(x, reverse=False):
    row = lax.broadcasted_iota(jnp.int32, x.shape, 0)
    for sh in (1, 2, 4, 8):
        if reverse:
            x = x + jnp.where(row < HG_SUB - sh, pltpu.roll(x, HG_SUB - sh, 0), 0.0)
        else:
            x = x + jnp.where(row >= sh, pltpu.roll(x, sh, 0), 0.0)
    return x


def _hgrn_gates(qr, fr, lbv):
    q = _silu(qr)
    sig = _sigmoid(fr)
    fg = lbv + (1.0 - lbv) * sig
    lf = jnp.log(fg) * LOG2_E
    gcum = _scan16(lf)
    glast = jnp.sum(lf, axis=0, keepdims=True)
    return q, sig, fg, 1.0 - fg, gcum, glast


def _hgrn_intra(q, k, gcum, tri):
    e = jnp.exp2(jnp.where(tri, gcum[:, None, :] - gcum[None, :, :], NEG))
    a = jnp.sum(q[:, None, :] * k[None, :, :] * e, axis=-1, keepdims=True)
    return e, a


def _hgrn_fwd(proj4, lb, gain, name):
    nsub = HG_TC // HG_SUB
    wide = HG_HP * HEAD_DIM

    def body(p_ref, lb_ref, gn_ref, o_ref, y_ref, st_ref, state_s):
        @pl.when(pl.program_id(1) == 0)
        def _():
            state_s[...] = jnp.zeros_like(state_s)

        gnv = gn_ref[...]
        shp = (HG_SUB, HG_SUB, HEAD_DIM)
        tri = lax.broadcasted_iota(jnp.int32, shp, 0) >= lax.broadcasted_iota(jnp.int32, shp, 1)

        def head(qr, fr, vv, gr, lbv, st):
            q, _, _, k, gcum, glast = _hgrn_gates(qr, fr, lbv)
            _, a = _hgrn_intra(q, k, gcum, tri)
            o = jnp.sum(a * vv[None, :, :], axis=1) + _dot_nt((q * jnp.exp2(gcum)).astype(BF16), st.astype(BF16))
            kg = k * jnp.exp2(glast - gcum)
            st_new = st * jnp.exp2(glast) + _dot_tn(vv.astype(BF16), kg.astype(BF16))
            rs = lax.rsqrt(jnp.mean(o * o, axis=-1, keepdims=True) + RMS_EPS)
            return o, (o * rs * gnv * _silu(gr)).astype(BF16), st_new

        def it(i, carry):
            rows = pl.ds(pl.multiple_of(i * HG_SUB, HG_SUB), HG_SUB)
            loaded = []
            for hh in range(HG_HP):
                lanes = pl.ds(hh * HEAD_DIM, HEAD_DIM)
                loaded.append(([p_ref[j, rows, lanes] for j in range(4)], lb_ref[:, lanes], state_s[hh]))
            results = [head(blk[0], blk[1], blk[2], blk[3], lbv, st) for blk, lbv, st in loaded]
            for hh, ((_, _, st), (o, y, st_new)) in enumerate(zip(loaded, results)):
                lanes = pl.ds(hh * HEAD_DIM, HEAD_DIM)
                st_ref[hh, i] = st.astype(BF16)
                state_s[hh] = st_new
                o_ref[rows, lanes] = o
                y_ref[hh // 2, rows, pl.ds((hh % 2) * HEAD_DIM, HEAD_DIM)] = y
            return carry

        lax.fori_loop(0, nsub, it, 0)

    return _pcall(
        body, name=name, grid=(HEADS // HG_HP, SEQ // HG_TC),
        in_specs=[pl.BlockSpec((4, HG_TC, wide), lambda h, j: (0, j, h)),
                  pl.BlockSpec((1, wide), lambda h, j: (0, h)),
                  pl.BlockSpec((1, HEAD_DIM), lambda h, j: (0, 0))],
        out_specs=[pl.BlockSpec((HG_TC, wide), lambda h, j: (j, h)),
                   pl.BlockSpec((HG_HP // 2, HG_TC, 2 * HEAD_DIM), lambda h, j: (h, j, 0)),
                   pl.BlockSpec((HG_HP, nsub, HEAD_DIM, HEAD_DIM), lambda h, j: (h, j, 0, 0))],
        out_shape=[_sds((SEQ, D_MODEL), F32), _sds((N_CHIP, SEQ, 2 * HEAD_DIM), BF16),
                   _sds((HEADS, SEQ // HG_SUB, HEAD_DIM, HEAD_DIM), BF16)],
        scratch_shapes=[pltpu.VMEM((HG_HP, HEAD_DIM, HEAD_DIM), F32)],
        compiler_params=_cparams(("parallel", "arbitrary")),
    )(proj4, lb, gain)


def _hgrn_bwd(proj4, lb, gain, o_raw, dy4, states, name):
    nsub = HG_TC // HG_SUB
    nt = SEQ // HG_TC
    wide = HG_HP * HEAD_DIM

    def body(p_ref, lb_ref, gn_ref, o_ref, dy_ref, st_ref, dp_ref, dlb_ref, dgn_ref, dst_s):
        @pl.when(pl.program_id(1) == 0)
        def _():
            dst_s[...] = jnp.zeros_like(dst_s)
            dlb_ref[...] = jnp.zeros_like(dlb_ref)
            dgn_ref[...] = jnp.zeros_like(dgn_ref)

        gnv = gn_ref[...]
        shp = (HG_SUB, HG_SUB, HEAD_DIM)
        tri = lax.broadcasted_iota(jnp.int32, shp, 0) >= lax.broadcasted_iota(jnp.int32, shp, 1)

        def head(qr, fr, vv, gr, o, dy, lbv, st0, dst):
            q, sig, fg, k, gcum, glast = _hgrn_gates(qr, fr, lbv)
            rs = lax.rsqrt(jnp.mean(o * o, axis=-1, keepdims=True) + RMS_EPS)
            oh = o * rs
            don = dy * _silu(gr)
            dgn = jnp.sum(don * oh, axis=0, keepdims=True)
            dgr = dy * oh * gnv * _dsilu(gr)
            doh = don * gnv
            do = rs * (doh - oh * jnp.mean(doh * oh, axis=-1, keepdims=True))
            dst16 = dst.astype(BF16)
            do16 = do.astype(BF16)
            eg = jnp.exp2(gcum)
            eb = jnp.exp2(glast - gcum)
            e, a = _hgrn_intra(q, k, gcum, tri)
            da = jnp.sum(do[:, None, :] * vv[None, :, :], axis=-1, keepdims=True)
            dae = da * e
            dq = jnp.sum(dae * k[None, :, :], axis=1) + eg * _dot(do16, st0)
            dk_state = eb * _dot(vv.astype(BF16), dst16)
            dk = jnp.sum(dae * q[:, None, :], axis=0) + dk_state
            dv = jnp.sum(a * do[:, None, :], axis=0) + _dot_nt((k * eb).astype(BF16), dst16)
            eglast = jnp.exp2(glast)
            dst_new = dst * eglast + _dot_tn(do16, (q * eg).astype(BF16))
            dglast = jnp.sum(k * dk_state, axis=0, keepdims=True) \
                + eglast * jnp.sum(dst * st0.astype(F32), axis=0, keepdims=True)
            dlf = _scan16(q * dq - k * dk, reverse=True) + dglast
            dfg = dlf / fg - dk
            dlb = jnp.sum(dfg * (1.0 - sig), axis=0, keepdims=True)
            dproj = ((dq * _dsilu(qr)).astype(BF16), (dfg * (1.0 - lbv) * sig * (1.0 - sig)).astype(BF16),
                     dv.astype(BF16), dgr.astype(BF16))
            return dproj, dst_new, dlb, dgn

        def it(ii, carry):
            i = nsub - 1 - ii
            rows = pl.ds(pl.multiple_of(i * HG_SUB, HG_SUB), HG_SUB)
            results = []
            for hh in range(HG_HP):
                lanes = pl.ds(hh * HEAD_DIM, HEAD_DIM)
                blk = [p_ref[j, rows, lanes] for j in range(4)]
                dy = dy_ref[hh // 2, rows, pl.ds((hh % 2) * HEAD_DIM, HEAD_DIM)]
                results.append(head(blk[0], blk[1], blk[2], blk[3], o_ref[rows, lanes], dy,
                                    lb_ref[:, lanes], st_ref[hh, i], dst_s[hh]))
            new_carry = []
            for hh, (dproj, dst_new, dlb, dgn) in enumerate(results):
                lanes = pl.ds(hh * HEAD_DIM, HEAD_DIM)
                dst_s[hh] = dst_new
                for j in range(4):
                    dp_ref[j, rows, lanes] = dproj[j]
                new_carry.append((carry[hh][0] + dlb, carry[hh][1] + dgn))
            return tuple(new_carry)

        zero = jnp.zeros((1, HEAD_DIM), F32)
        sums = lax.fori_loop(0, nsub, it, tuple((zero, zero) for _ in range(HG_HP)))
        for hh in range(HG_HP):
            dlb_ref[hh] += sums[hh][0]
            dgn_ref[hh] += sums[hh][1]

    vspec = pl.BlockSpec((HG_HP, 1, HEAD_DIM), lambda h, j: (h, 0, 0))
    return _pcall(
        body, name=name, grid=(HEADS // HG_HP, nt),
        in_specs=[pl.BlockSpec((4, HG_TC, wide), lambda h, j: (0, nt - 1 - j, h)),
                  pl.BlockSpec((1, wide), lambda h, j: (0, h)),
                  pl.BlockSpec((1, HEAD_DIM), lambda h, j: (0, 0)),
                  pl.BlockSpec((HG_TC, wide), lambda h, j: (nt - 1 - j, h)),
                  pl.BlockSpec((HG_HP // 2, HG_TC, 2 * HEAD_DIM), lambda h, j: (h, nt - 1 - j, 0)),
                  pl.BlockSpec((HG_HP, nsub, HEAD_DIM, HEAD_DIM), lambda h, j: (h, nt - 1 - j, 0, 0))],
        out_specs=[pl.BlockSpec((4, HG_TC, wide), lambda h, j: (0, nt - 1 - j, h)), vspec, vspec],
        out_shape=[_sds((4, SEQ, D_MODEL), BF16), _sds((HEADS, 1, HEAD_DIM), F32), _sds((HEADS, 1, HEAD_DIM), F32)],
        scratch_shapes=[pltpu.VMEM((HG_HP, HEAD_DIM, HEAD_DIM), F32)],
        compiler_params=_cparams(("parallel", "arbitrary")),
    )(proj4, lb, gain, o_raw, dy4, states)


def _t5_bucket(dist):
    n = np.asarray(dist, dtype=np.int64)
    max_exact = NUM_BUCKETS // 2
    large = max_exact + (np.log(np.maximum(n, 1) / max_exact) / np.log(MAX_DISTANCE / max_exact)
                         * (NUM_BUCKETS - max_exact)).astype(np.int64)
    large = np.minimum(large, NUM_BUCKETS - 1)
    return np.where(n < max_exact, n, large).astype(np.int32)


def _bias_tables():
    qi = np.arange(ATT_BLK)[:, None]
    ki = np.arange(2 * ATT_BLK)[None, :]
    j = ATT_BLK + qi - ki
    valid = (j >= 0) & (j <= ATT_BLK)
    return np.stack([np.where(valid, _t5_bucket(np.clip(j, 0, ATT_BLK) * d), -1) for _, d in GROUPS]).astype(np.int32)


def _attn_bias(rel_bias, name):
    idx = _bias_tables()
    ng = len(GROUPS)
    buckets = [sorted(set(idx[g][idx[g] >= 0].tolist())) for g in range(ng)]

    def body(rb_ref, idx_ref, o_ref):
        h = pl.program_id(0)
        for g in range(ng):
            ig = idx_ref[g]
            acc = jnp.full(ig.shape, NEG, F32)
            for b in buckets[g]:
                acc = jnp.where(ig == b, rb_ref[b, g * HEADS + h], acc)
            o_ref[g] = acc

    return _pcall(
        body, name=name, grid=(HEADS,),
        in_specs=[pl.BlockSpec(memory_space=pltpu.SMEM),
                  pl.BlockSpec((ng, ATT_BLK, 2 * ATT_BLK), lambda h: (0, 0, 0))],
        out_specs=pl.BlockSpec((ng, None, ATT_BLK, 2 * ATT_BLK), lambda h: (0, h, 0, 0)),
        out_shape=_sds((ng, HEADS, ATT_BLK, 2 * ATT_BLK), F32),
        compiler_params=_cparams(("parallel",)),
    )(rel_bias, jnp.asarray(idx))


ADA_SHARD = 6 * D_MODEL // N_CHIP
ADA_TN = 512


def _ada_fwd(c_all, ada_w, ada_b_cols, name):
    def body(c_ref, w_ref, b_ref, o_ref):
        ca = _silu(c_ref[...]).astype(BF16)
        o_ref[...] = _dot(ca, w_ref[...].astype(BF16)) + b_ref[...]

    return _pcall(
        body, name=name, grid=(DEPTH, ADA_SHARD // ADA_TN),
        in_specs=[pl.BlockSpec((N_DEV, D_MODEL), lambda l, j: (0, 0)),
                  pl.BlockSpec((None, D_MODEL, ADA_TN), lambda l, j: (l, 0, j)),
                  pl.BlockSpec((None, 1, ADA_TN), lambda l, j: (l, 0, j))],
        out_specs=pl.BlockSpec((None, N_DEV, ADA_TN), lambda l, j: (l, 0, j)),
        out_shape=_sds((DEPTH, N_DEV, ADA_SHARD), F32),
        compiler_params=_cparams(("parallel", "parallel")),
    )(c_all, ada_w, ada_b_cols)


def _ada_bwd(c_all, dmod_cols, name):
    def body(c_ref, d_ref, o_ref):
        ca = _silu(c_ref[...]).astype(BF16)
        o_ref[...] = _dot_tn(ca, d_ref[...].astype(BF16))

    return _pcall(
        body, name=name, grid=(DEPTH, ADA_SHARD // ADA_TN),
        in_specs=[pl.BlockSpec((N_DEV, D_MODEL), lambda l, j: (0, 0)),
                  pl.BlockSpec((None, N_DEV, ADA_TN), lambda l, j: (l, 0, j))],
        out_specs=pl.BlockSpec((None, D_MODEL, ADA_TN), lambda l, j: (l, 0, j)),
        out_shape=_sds((DEPTH, D_MODEL, ADA_SHARD), F32),
        compiler_params=_cparams(("parallel", "parallel")),
    )(c_all, dmod_cols)


def _lower_bounds(logits, name):
    def body(l_ref, o_ref):
        l0 = l_ref[0:1, :]
        l1 = l_ref[1:2, :]
        mx = jnp.maximum(l0, l1)
        e0 = jnp.exp(l0 - mx)
        e1 = jnp.exp(l1 - mx)
        p0 = e0 / (e0 + e1)
        p1 = e1 / (e0 + e1)
        o_ref[0:1, :] = p0 - p0
        o_ref[1:2, :] = (p0 + p1) - p0

    return _pcall(body, name=name, out_shape=_sds((DEPTH, D_MODEL), F32), compiler_params=_cparams())(logits)


_R_DMOD = 0
_R_NMIX = 96
_R_NFFN = 112
_R_QG = 128
_R_KG = 152
_R_GN = 176
_R_LB = 184
_R_RB = 192
SMALL_ROWS = 200


def _small_totals(gathered, logits8, name):
    ng = len(GROUPS)

    def body(g_ref, l_ref, main_ref, gains_ref, dlb_ref, rb_ref):
        tot = g_ref[0]
        for dev in range(1, N_DEV):
            tot = tot + g_ref[dev]
        main_ref[...] = tot[0:_R_QG]
        gains_ref[...] = jnp.zeros_like(gains_ref)
        for g in range(ng):
            gains_ref[g:g + 1, :] = jnp.sum(tot[_R_QG + 8 * g:_R_QG + 8 * g + 8], axis=0, keepdims=True)
            gains_ref[ng + g:ng + g + 1, :] = jnp.sum(tot[_R_KG + 8 * g:_R_KG + 8 * g + 8], axis=0, keepdims=True)
        gains_ref[2 * ng:2 * ng + 1, :] = jnp.sum(tot[_R_GN:_R_GN + 8], axis=0, keepdims=True)
        rb_ref[...] = tot[_R_RB:_R_RB + 8]
        dlb1 = tot[_R_LB:_R_LB + 8]
        l0 = l_ref[0]
        l1 = l_ref[1]
        mx = jnp.maximum(l0, l1)
        e0 = jnp.exp(l0 - mx)
        e1 = jnp.exp(l1 - mx)
        p0 = e0 / (e0 + e1)
        p1 = e1 / (e0 + e1)
        dlb_ref[0] = -p0 * p1 * dlb1
        dlb_ref[1] = p1 * (1.0 - p1) * dlb1

    return _pcall(
        body, name=name,
        out_shape=[_sds((_R_QG, 128), F32), _sds((8, 128), F32), _sds((DEPTH, 8, 128), F32), _sds((8, 128), F32)],
        compiler_params=_cparams(),
    )(gathered, logits8)


def _row_tile(rows):
    return 128 if rows % 128 == 0 else rows


def _adamw(w, grads, m, v, name):
    nl, r, cdim = w.shape
    tr = _row_tile(r)

    def body(*refs):
        g_refs = refs[:nl]
        w_ref, m_ref, v_ref, go_ref, d_ref, mo_ref, vo_ref = refs[nl:]

        def step(g):
            m2 = ADAM_B1 * m_ref[...] + (1.0 - ADAM_B1) * g
            v2 = ADAM_B2 * v_ref[...] + (1.0 - ADAM_B2) * (g * g)
            m_hat = m2 / (1.0 - ADAM_B1 ** ADAM_STEP)
            v_hat = v2 / (1.0 - ADAM_B2 ** ADAM_STEP)
            go_ref[...] = g
            d_ref[...] = -ADAM_LR * (m_hat / (jnp.sqrt(v_hat) + ADAM_EPS) + ADAM_WD * w_ref[...])
            mo_ref[...] = m2
            vo_ref[...] = v2

        if nl == 1:
            step(g_refs[0][...])
        else:
            for layer in range(nl):
                @pl.when(pl.program_id(0) == layer)
                def _(layer=layer):
                    step(g_refs[layer][...])

    big = pl.BlockSpec((None, tr, cdim), lambda l, i: (l, i, 0))
    g_specs = [pl.BlockSpec((tr, cdim), lambda l, i, layer=layer: (jnp.where(l == layer, i, 0), 0))
               for layer in range(nl)]
    shp = _sds((nl, r, cdim), F32)
    return _pcall(
        body, name=name, grid=(nl, r // tr),
        in_specs=g_specs + [big, big, big],
        out_specs=[big, big, big, big],
        out_shape=[shp, shp, shp, shp],
        compiler_params=_cparams(("parallel", "parallel")),
    )(*grads, w, m, v)


def _cast_bf16(place, w, name):
    nl, r, cdim = w.shape
    tr = _row_tile(r)

    def body(place_ref, w_ref, o_ref):
        o_ref[...] = w_ref[...].astype(BF16)

    return _pcall(
        body, name=name,
        grid_spec=pltpu.PrefetchScalarGridSpec(
            num_scalar_prefetch=1, grid=(nl, r // tr),
            in_specs=[pl.BlockSpec((None, tr, cdim), lambda l, i, place_ref: (l, i, 0))],
            out_specs=pl.BlockSpec((None, None, tr, cdim), lambda l, i, place_ref: (place_ref[1], l, i, 0))),
        out_shape=_sds((N_CHIP, nl, r, cdim), BF16),
        compiler_params=_cparams(("parallel", "parallel")),
    )(place, w)


def _rs_add_cast(place, grads, recvs, name):
    n_a = len(grads)
    _, k, n = grads[0].shape
    kh = k // 2
    tr = _row_tile(kh)
    nb = kh // tr

    def body(place_ref, *refs):
        for a in range(n_a):
            refs[2 * n_a + a][...] = (refs[a][...] + refs[n_a + a][...]).astype(BF16)

    half = pl.BlockSpec((None, tr, n), lambda s, i, place_ref: (s, i, 0))
    mine = pl.BlockSpec((None, tr, n), lambda s, i, place_ref: (s, place_ref[0] * nb + i, 0))
    return _pcall(
        body, name=name,
        grid_spec=pltpu.PrefetchScalarGridSpec(
            num_scalar_prefetch=1, grid=(N_CHIP, nb),
            in_specs=[mine] * n_a + [half] * n_a,
            out_specs=[half] * n_a),
        out_shape=[_sds((N_CHIP, kh, n), BF16)] * n_a,
        compiler_params=_cparams(("parallel", "parallel")),
    )(place, *grads, *recvs)


def _rs_sum4(place, parts, gots, name):
    n_a = len(parts)
    _, kh, n = parts[0].shape
    tr = _row_tile(kh)
    nb = kh // tr

    def body(place_ref, *refs):
        for a in range(n_a):
            acc = refs[a][...].astype(F32)
            for j in range(N_CHIP - 1):
                acc = acc + refs[n_a + a][j].astype(F32)
            refs[2 * n_a + a][...] = acc

    return _pcall(
        body, name=name,
        grid_spec=pltpu.PrefetchScalarGridSpec(
            num_scalar_prefetch=1, grid=(nb,),
            in_specs=[pl.BlockSpec((None, tr, n), lambda i, place_ref: (place_ref[1], i, 0))] * n_a
            + [pl.BlockSpec((N_CHIP - 1, tr, n), lambda i, place_ref: (0, i, 0))] * n_a,
            out_specs=[pl.BlockSpec((tr, n), lambda i, place_ref: (place_ref[0] * nb + i, 0))] * n_a),
        out_shape=[_sds((2 * kh, n), F32)] * n_a,
        compiler_params=_cparams(("parallel",)),
    )(place, *parts, *gots)


_ANY = pl.BlockSpec(memory_space=pl.ANY)


def _position():
    return lax.axis_index("x"), lax.axis_index("y"), lax.axis_index("c")


def _other_chips(x, y):
    return [(1 - x, y), (x, 1 - y), (1 - x, 1 - y)]


def _remote(src, dst, send_sem, recv_sem, to):
    return pltpu.make_async_remote_copy(src_ref=src, dst_ref=dst, send_sem=send_sem, recv_sem=recv_sem,
                                        device_id=to, device_id_type=MESH)


def _small_allgather(v, name):
    r = v.shape[0]

    def body(x_ref, out_ref, send_sems, recv_sems, local_sem):
        x, y, c = _position()
        me, sibling = (x, y, c), (x, y, 1 - c)
        chips = _other_chips(x, y)

        def slab(px, py, pc):
            return out_ref.at[4 * px + 2 * py + pc]

        def copy(k, block, to, src=None):
            return _remote(slab(*block) if src is None else src, slab(*block), send_sems.at[k], recv_sems.at[k], to)

        mine = pltpu.make_async_copy(x_ref, slab(*me), local_sem)
        mine.start()
        first = [copy(0, me, sibling, src=x_ref)]
        first += [copy(1 + j, me, (*chip, c), src=x_ref) for j, chip in enumerate(chips)]
        for cp in first:
            cp.start()
        passed = [copy(4 + j, (*chip, c), sibling) for j, chip in enumerate(chips)]
        for j, chip in enumerate(chips):
            copy(1 + j, (*chip, c), me).wait_recv()
            passed[j].start()
        copy(0, sibling, me).wait_recv()
        for j, chip in enumerate(chips):
            copy(4 + j, (*chip, 1 - c), me).wait_recv()
        for cp in first + passed:
            cp.wait_send()
        mine.wait()

    return _pcall(
        body, name=name,
        out_shape=_sds((N_DEV, r, 128), F32),
        in_specs=[pl.BlockSpec(memory_space=pltpu.VMEM)],
        out_specs=pl.BlockSpec(memory_space=pltpu.VMEM),
        scratch_shapes=[pltpu.SemaphoreType.DMA((7,)), pltpu.SemaphoreType.DMA((7,)), pltpu.SemaphoreType.DMA],
        compiler_params=_cparams(),
    )(v)


def _half_rows(core, kh):
    return pl.ds(pl.multiple_of(core * kh, 8), kh)


def _slab_half(ref, chip, core):
    return ref.at[chip, :, _half_rows(core, ref.shape[2] // 2), :]


def _gather_ici(out, send_sems, recv_sems):
    def copies():
        x, y, c = _position()
        for a in range(len(out)):
            for j, (px, py) in enumerate(_other_chips(x, y)):
                mine = _slab_half(out[a], 2 * x + y, c)
                landed = _slab_half(out[a], 2 * px + py, c)
                yield (_remote(mine, mine, send_sems.at[a, j], recv_sems.at[a, j], (px, py, c)),
                       _remote(landed, landed, send_sems.at[a, j], recv_sems.at[a, j], (px, py, c)))

    def start():
        for send, _ in copies():
            send.start()

    def wait():
        for send, recv in copies():
            recv.wait_recv()
            send.wait_send()

    return start, wait


def _gather_d2d(out, send_sems, recv_sems):
    def copies():
        x, y, c = _position()
        for a in range(len(out)):
            for j, (px, py) in enumerate(_other_chips(x, y)):
                landed = _slab_half(out[a], 2 * px + py, c)
                other = _slab_half(out[a], 2 * px + py, 1 - c)
                yield (_remote(landed, landed, send_sems.at[a, j], recv_sems.at[a, j], (x, y, 1 - c)),
                       _remote(other, other, send_sems.at[a, j], recv_sems.at[a, j], (x, y, 1 - c)))

    def start():
        for send, _ in copies():
            send.start()

    def wait():
        for send, recv in copies():
            recv.wait_recv()
            send.wait_send()

    return start, wait


def _gather_weights(slabs, name, ici=True):
    n = len(slabs)

    def body(*refs):
        out = refs[n:2 * n]
        sems = refs[2 * n:]
        if ici:
            start, wait = _gather_ici(out, sems[2], sems[3])
            start()
            wait()
        start, wait = _gather_d2d(out, sems[0], sems[1])
        start()
        wait()

    sem = pltpu.SemaphoreType.DMA((n, 3))
    return _pcall(
        body, name=name,
        out_shape=[_sds(s.shape, BF16) for s in slabs],
        in_specs=[_ANY] * n, out_specs=[_ANY] * n,
        input_output_aliases={a: a for a in range(n)},
        scratch_shapes=[sem, sem] + ([sem, sem] if ici else []),
        compiler_params=_cparams(),
    )(*slabs)


def _rs_halves(grads, out, send_sems, recv_sems):
    def copies():
        x, y, c = _position()
        for a in range(len(grads)):
            kh = grads[a].shape[1] // 2
            yield _remote(grads[a].at[:, _half_rows(1 - c, kh), :], out[a], send_sems.at[a], recv_sems.at[a],
                          (x, y, 1 - c))

    def start():
        for cp in copies():
            cp.start()

    def wait():
        for cp in copies():
            cp.wait()

    return start, wait


def _rs_halves_shapes(grads):
    return [_sds((N_CHIP, g.shape[1] // 2, g.shape[2]), F32) for g in grads]


def _rs_exchange_halves(grads, name):
    n = len(grads)

    def body(*refs):
        start, wait = _rs_halves(refs[:n], refs[n:2 * n], *refs[2 * n:])
        start()
        wait()

    return _pcall(
        body, name=name,
        out_shape=_rs_halves_shapes(grads),
        in_specs=[_ANY] * n, out_specs=[_ANY] * n,
        scratch_shapes=[pltpu.SemaphoreType.DMA((n,)), pltpu.SemaphoreType.DMA((n,))],
        compiler_params=_cparams(),
    )(*grads)


def _rs_chips(parts, out, send_sems, recv_sems):
    def copies():
        x, y, c = _position()
        for a in range(len(parts)):
            for j, (px, py) in enumerate(_other_chips(x, y)):
                got = out[a].at[j]
                yield (_remote(parts[a].at[2 * px + py], got, send_sems.at[a, j], recv_sems.at[a, j], (px, py, c)),
                       _remote(got, got, send_sems.at[a, j], recv_sems.at[a, j], (px, py, c)))

    def start():
        for send, _ in copies():
            send.start()

    def wait():
        for send, recv in copies():
            recv.wait_recv()
            send.wait_send()

    return start, wait


def _rs_chips_shapes(parts):
    return [_sds((N_CHIP - 1,) + p.shape[1:], BF16) for p in parts]


def _rs_join(out, send_sems, recv_sems):
    def copies():
        x, y, c = _position()
        for a in range(len(out)):
            kh = out[a].shape[0] // 2
            mine = out[a].at[_half_rows(c, kh), :]
            theirs = out[a].at[_half_rows(1 - c, kh), :]
            yield (_remote(mine, mine, send_sems.at[a], recv_sems.at[a], (x, y, 1 - c)),
                   _remote(theirs, theirs, send_sems.at[a], recv_sems.at[a], (x, y, 1 - c)))

    def start():
        for send, _ in copies():
            send.start()

    def wait():
        for send, recv in copies():
            recv.wait_recv()
            send.wait_send()

    return start, wait


def _rs_join_halves(fulls, name):
    n = len(fulls)

    def body(*refs):
        start, wait = _rs_join(refs[n:2 * n], *refs[2 * n:])
        start()
        wait()

    return _pcall(
        body, name=name,
        out_shape=[_sds(f.shape, F32) for f in fulls],
        in_specs=[_ANY] * n, out_specs=[_ANY] * n,
        input_output_aliases={a: a for a in range(n)},
        scratch_shapes=[pltpu.SemaphoreType.DMA((n,)), pltpu.SemaphoreType.DMA((n,))],
        compiler_params=_cparams(),
    )(*fulls)


_SMALL_ORDER = ("rel_bias", "ada_b", "norm_mix", "norm_ffn", "attn_q_gain", "attn_k_gain", "hgrn_gnorm",
                "hgrn_lower_bounds")
_WEIGHT_ORDER = ("rel_bias", "ada_w", "ada_b", "norm_mix", "norm_ffn", "attn_w_qkv", "attn_w_out", "attn_q_gain",
                 "attn_k_gain", "hgrn_w_in", "hgrn_w_out", "hgrn_gnorm", "hgrn_lower_bounds", "ffn_w1", "ffn_w3",
                 "ffn_w2")


def _qkv_group_map(t):
    return t // 4, t % 4


def _qkv_chip_map(t):
    return t // 9, t % 9


def _hin_map(t):
    return t // 2, t % 2


def _block_map(t):
    return t, 0


def _pack_rows(parts):
    return jnp.concatenate([p.reshape(-1, 128) for p in parts], axis=0)


def kernel(x, c, rel_bias, ada_w, ada_b, norm_mix, norm_ffn, attn_w_qkv, attn_w_out, attn_q_gain, attn_k_gain, hgrn_w_in, hgrn_w_out, hgrn_gnorm, hgrn_lower_bounds, ffn_w1, ffn_w3, ffn_w2, loss_target, m_rel_bias, m_ada_w, m_ada_b, m_norm_mix, m_norm_ffn, m_attn_w_qkv, m_attn_w_out, m_attn_q_gain, m_attn_k_gain, m_hgrn_w_in, m_hgrn_w_out, m_hgrn_gnorm, m_hgrn_lower_bounds, m_ffn_w1, m_ffn_w3, m_ffn_w2, v_rel_bias, v_ada_w, v_ada_b, v_norm_mix, v_norm_ffn, v_attn_w_qkv, v_attn_w_out, v_attn_q_gain, v_attn_k_gain, v_hgrn_w_in, v_hgrn_w_out, v_hgrn_gnorm, v_hgrn_lower_bounds, v_ffn_w1, v_ffn_w3, v_ffn_w2):
    weights = dict(rel_bias=rel_bias, ada_w=ada_w, ada_b=ada_b, norm_mix=norm_mix, norm_ffn=norm_ffn,
                   attn_w_qkv=attn_w_qkv, attn_w_out=attn_w_out, attn_q_gain=attn_q_gain, attn_k_gain=attn_k_gain,
                   hgrn_w_in=hgrn_w_in, hgrn_w_out=hgrn_w_out, hgrn_gnorm=hgrn_gnorm,
                   hgrn_lower_bounds=hgrn_lower_bounds, ffn_w1=ffn_w1, ffn_w3=ffn_w3, ffn_w2=ffn_w2)
    mom1 = dict(rel_bias=m_rel_bias, ada_w=m_ada_w, ada_b=m_ada_b, norm_mix=m_norm_mix, norm_ffn=m_norm_ffn,
                attn_w_qkv=m_attn_w_qkv, attn_w_out=m_attn_w_out, attn_q_gain=m_attn_q_gain,
                attn_k_gain=m_attn_k_gain, hgrn_w_in=m_hgrn_w_in, hgrn_w_out=m_hgrn_w_out, hgrn_gnorm=m_hgrn_gnorm,
                hgrn_lower_bounds=m_hgrn_lower_bounds, ffn_w1=m_ffn_w1, ffn_w3=m_ffn_w3, ffn_w2=m_ffn_w2)
    mom2 = dict(rel_bias=v_rel_bias, ada_w=v_ada_w, ada_b=v_ada_b, norm_mix=v_norm_mix, norm_ffn=v_norm_ffn,
                attn_w_qkv=v_attn_w_qkv, attn_w_out=v_attn_w_out, attn_q_gain=v_attn_q_gain,
                attn_k_gain=v_attn_k_gain, hgrn_w_in=v_hgrn_w_in, hgrn_w_out=v_hgrn_w_out, hgrn_gnorm=v_hgrn_gnorm,
                hgrn_lower_bounds=v_hgrn_lower_bounds, ffn_w1=v_ffn_w1, ffn_w3=v_ffn_w3, ffn_w2=v_ffn_w2)

    transposed = ("ffn_w1", "ffn_w3")
    for group in (weights, mom1, mom2):
        for k in transposed:
            group[k] = jnp.transpose(group[k], (0, 2, 1))

    xi, yi, ci = _position()
    chip = 2 * xi + yi
    dev = 4 * xi + 2 * yi + ci
    place = jnp.stack([ci, chip]).astype(jnp.int32)
    d = D_MODEL

    big_names = ("attn_w_qkv", "attn_w_out", "hgrn_w_in", "hgrn_w_out", "ffn_w1", "ffn_w3", "ffn_w2")
    early_names, late_names = big_names[:1], big_names[1:]
    slabs16 = {k: _cast_bf16(place, weights[k], "cast_" + k) for k in big_names}
    wg = dict(zip(early_names, _gather_weights([slabs16[k] for k in early_names], "gather_early")))

    c_all = _small_allgather(c.reshape(8, 128), "gather_c").reshape(N_DEV, d)
    ada_b_cols = lax.dynamic_slice(ada_b, (0, chip * ADA_SHARD), (DEPTH, ADA_SHARD)).reshape(DEPTH, 1, ADA_SHARD)
    mod_shard = _ada_fwd(c_all, ada_w, ada_b_cols, "ada_fwd")
    mod_all = _small_allgather(mod_shard.reshape(-1, 128), "gather_mod").reshape(N_DEV, DEPTH, N_DEV, ADA_SHARD)
    mod_mine = lax.dynamic_index_in_dim(mod_all[0::2], dev, axis=2, keepdims=False)
    mod = jnp.transpose(mod_mine, (1, 0, 2)).reshape(DEPTH, 6 * d)

    def mods(layer):
        return [mod[layer:layer + 1, j * d:(j + 1) * d] for j in range(6)]

    x0 = x.reshape(SEQ, d)
    target = loss_target.reshape(SEQ, d)
    qg = attn_q_gain.reshape(len(GROUPS), 1, HEAD_DIM)
    kg = attn_k_gain.reshape(len(GROUPS), 1, HEAD_DIM)
    bias = _attn_bias(rel_bias, "attn_bias")
    lb1 = _lower_bounds(hgrn_lower_bounds, "lower_bounds")[1:2]

    def ffn_fwd(layer, x_in, sc2, sh2, g2):
        a1, a3, u, hf = _ffn_up(x_in, norm_ffn[layer:layer + 1], sc2, sh2, wg["ffn_w1"], wg["ffn_w3"], layer,
                                f"l{layer}_ffn_up")
        z, x_out = _mm_rows(u, wg["ffn_w2"], layer, x_in, g2, f"l{layer}_ffn_down")
        return x_out, (hf, a1, a3, u, z)

    def ffn_bwd(layer, dz, dg2, dx_out, x_in, sc2, sh2, saved, mixer_branch, halves=()):
        hf, a1, a3, u, _ = saved
        da1, da3, *recv = _ffn_down_bwd(dz, wg["ffn_w2"], layer, a1, a3, f"l{layer}_ffn_down_bwd", halves=halves)
        dw2 = _mm_rows_bwd_w(u, dz, f"l{layer}_dw2")
        dh = _ffn_up_bwd(da1, da3, wg["ffn_w1"], wg["ffn_w3"], layer, f"l{layer}_ffn_up_bwd")
        dw1, dw3 = _mm_rows_bwd_w_multi([da1, da3], hf, f"l{layer}_dw13")
        dx_in, dsc2, dsh2, dnf, dz_mix, dg_mix = _norm_mod_bwd(x_in, norm_ffn[layer:layer + 1], sc2, sh2, dh, dx_out,
                                                               f"l{layer}_norm_ffn_bwd", branch=mixer_branch)
        return dx_in, (dw1, dw3, dw2), (dsh2, dsc2, dg2), dnf, recv, dz_mix, dg_mix

    def rs_batched(fn, prefix, tags, *columns):
        out = [None] * len(tags)
        by_shape = {}
        for idx, arr in enumerate(columns[0]):
            by_shape.setdefault(arr.shape, []).append(idx)
        for idxs in by_shape.values():
            for lo in range(0, len(idxs), 3):
                sel = idxs[lo:lo + 3]
                k, layer = tags[sel[0]]
                res = fn(place, *[[col[i] for i in sel] for col in columns], f"{prefix}_{k}_{layer}_x{len(sel)}")
                for i, r in zip(sel, res):
                    out[i] = r
        return out

    def rs_add(tags, grads_in, recv):
        return rs_batched(_rs_add_cast, "rs_add", tags, grads_in, list(recv))

    sh1_0, sc1_0, g1_0, sh2_0, sc2_0, g2_0 = mods(0)
    w_qkv9 = _retile_cols(wg["attn_w_qkv"].reshape(N_CHIP, d, 2304), n_out=9, width_out=d, tn=256,
                          src_map=_qkv_chip_map, dst_map=_qkv_group_map, n_tiles=36,
                          name="regroup_w_qkv").reshape(9, 1, d, d)
    qkv9, h0 = _mm_cols(x0, norm_mix[0:1], sc1_0, sh1_0, w_qkv9, 0, n_blocks=9, width=d, tn=d,
                        act_map=_block_map, w_map=_block_map, out_dtype=F32, name="l0_qkv")
    o4, lse, *late = _attn_fwd(qkv9, qg, kg, bias, "l0_attn", gather=[slabs16[k] for k in late_names])
    wg.update(zip(late_names, _gather_weights(late, "gather_late_siblings", ici=False)))
    y0, x1 = _mm_rows(o4, wg["attn_w_out"], 0, x0, g1_0, "l0_attn_out")
    x2, ffn0 = ffn_fwd(0, x1, sc2_0, sh2_0, g2_0)

    sh1_1, sc1_1, g1_1, sh2_1, sc2_1, g2_1 = mods(1)
    proj4, h1 = _mm_cols(x2, norm_mix[1:2], sc1_1, sh1_1, wg["hgrn_w_in"], 0, n_blocks=4, width=d, tn=512,
                         act_map=_hin_map, w_map=_hin_map, out_dtype=F32, name="l1_hgrn_in")
    o_raw, yg4, states = _hgrn_fwd(proj4, lb1, hgrn_gnorm, "l1_hgrn")
    y1, x3 = _mm_rows(yg4, wg["hgrn_w_out"], 0, x2, g1_1, "l1_hgrn_out")
    x4, ffn1 = ffn_fwd(1, x3, sc2_1, sh2_1, g2_1)

    dx4, loss_part, dz_ffn1, dg2_1 = _loss_head(x4, target, ffn1[4], g2_1, "loss_head")
    loss = lax.psum(loss_part[0, 0], ("x", "y", "c"))

    dx3, (dw1_1, dw3_1, dw2_1), dmod2_1, dnf_1, _, dzm1, dg1_1 = ffn_bwd(
        1, dz_ffn1, dg2_1, dx4, x3, sc2_1, sh2_1, ffn1, (y1, g1_1))
    dyg4 = _mm_rows_bwd_a(dzm1, wg["hgrn_w_out"], 0, "l1_hgrn_out_bwd")
    dw_hout = _mm_rows_bwd_w(yg4, dzm1, "l1_dw_hgrn_out")
    dproj4, dlb_h, dgn_h = _hgrn_bwd(proj4, lb1, hgrn_gnorm, o_raw, dyg4, states, "l1_hgrn_bwd")
    dh1 = _mm_cols_bwd_a(dproj4, wg["hgrn_w_in"], 0, group=N_CHIP, name="l1_hgrn_in_bwd", tm=512)
    dw_hin = _mm_cols_bwd_w(h1, dproj4, ns=d, tn=d, act_map=_block_map, w_map=_block_map, n_tiles=N_CHIP,
                            name="l1_dw_hgrn_in", tm=2048)
    dx2, dsc1_1, dsh1_1, dnm_1, dz_ffn0, dg2_0 = _norm_mod_bwd(x2, norm_mix[1:2], sc1_1, sh1_1, dh1, dx3,
                                                               "l1_norm_mix_bwd", branch=(ffn0[4], g2_0))

    tags_1 = [("hgrn_w_in", 0), ("hgrn_w_out", 0), ("ffn_w1", 1), ("ffn_w3", 1), ("ffn_w2", 1)]
    grads_1 = [dw_hin, dw_hout, dw1_1, dw3_1, dw2_1]
    dx1, (dw1_0, dw3_0, dw2_0), dmod2_0, dnf_0, recv_1, dzm0, dg1_0 = ffn_bwd(
        0, dz_ffn0, dg2_0, dx2, x1, sc2_0, sh2_0, ffn0, (y0, g1_0), halves=grads_1)
    tags_0 = [("ffn_w1", 0), ("ffn_w3", 0), ("ffn_w2", 0)]
    grads_0 = [dw1_0, dw3_0, dw2_0]
    do4, *recv_0 = _mm_rows_bwd_a(dzm0, wg["attn_w_out"], 0, "l0_attn_out_bwd", halves=grads_0)
    dw_aout = _mm_rows_bwd_w(o4, dzm0, "l0_dw_attn_out")
    tags_a = tags_1 + tags_0
    parts_a = rs_add(tags_1, grads_1, recv_1) + rs_add(tags_0, grads_0, recv_0)
    dqkv, dqg_h, dkg_h, dbias, *got_a = _attn_bwd(qkv9, qg, kg, bias, do4, o4, lse, "l0_attn_bwd", scatter=parts_a)
    dqkv9 = dqkv.reshape(9, SEQ, d)
    dw_qkv9 = _mm_cols_bwd_w(h0, dqkv9, ns=d, tn=d, act_map=_block_map, w_map=_block_map, n_tiles=9,
                             name="l0_dw_qkv", tm=2048, n_out=9)
    dw_qkv = _retile_cols(dw_qkv9, n_out=N_CHIP, width_out=2304, tn=256, src_map=_qkv_group_map,
                          dst_map=_qkv_chip_map, n_tiles=36, name="regroup_dw_qkv")
    tags_b = [("attn_w_qkv", 0), ("attn_w_out", 0)]
    grads_b = [dw_qkv, dw_aout]
    parts_b = rs_add(tags_b, grads_b, _rs_exchange_halves(grads_b, "rs_exchange_halves_b"))
    dh0, *got_b = _mm_cols_bwd_a(dqkv9, w_qkv9, 0, group=3, name="l0_qkv_bwd", scatter=parts_b)
    dx0, dsc1_0, dsh1_0, dnm_0 = _norm_mod_bwd(x0, norm_mix[0:1], sc1_0, sh1_0, dh0, dx1, "l0_norm_mix_bwd")
    drb8 = _relbias_bwd(dbias, jnp.asarray(_bias_tables()), "rel_bias_bwd")

    small = _pack_rows([
        dsh1_0, dsc1_0, dg1_0, *dmod2_0, dsh1_1, dsc1_1, dg1_1, *dmod2_1,
        dnm_0, dnm_1, dnf_0, dnf_1,
        jnp.transpose(dqg_h, (1, 0, 2, 3)), jnp.transpose(dkg_h, (1, 0, 2, 3)), dgn_h, dlb_h, drb8])
    small_all = _small_allgather(small, "gather_small")
    main, gains, dlbnd, rbt = _small_totals(small_all, hgrn_lower_bounds.reshape(DEPTH, 8, 128), "small_totals")
    ng = len(GROUPS)
    grads = {
        "ada_b": main[_R_DMOD:_R_NMIX].reshape(DEPTH, 6 * d),
        "norm_mix": main[_R_NMIX:_R_NFFN].reshape(DEPTH, d),
        "norm_ffn": main[_R_NFFN:_R_QG].reshape(DEPTH, d),
        "attn_q_gain": gains[0:ng].reshape(1, ng, HEAD_DIM),
        "attn_k_gain": gains[ng:2 * ng].reshape(1, ng, HEAD_DIM),
        "hgrn_gnorm": gains[2 * ng:2 * ng + 1],
        "hgrn_lower_bounds": dlbnd.reshape(DEPTH, d),
        "rel_bias": jnp.transpose(rbt[:, :ng * NUM_BUCKETS].reshape(HEADS, ng, NUM_BUCKETS), (2, 1, 0))
                       .reshape(NUM_BUCKETS, ng * HEADS),
    }
    dmod_all = small_all[:, _R_DMOD:_R_NMIX].reshape(N_DEV, DEPTH, 6 * d)
    dmod_cols = jnp.transpose(lax.dynamic_slice(dmod_all, (0, 0, chip * ADA_SHARD), (N_DEV, DEPTH, ADA_SHARD)),
                              (1, 0, 2))
    grad_ada_w = _ada_bwd(c_all, dmod_cols, "ada_bwd")

    tags = tags_a + tags_b
    halves = rs_batched(_rs_sum4, "rs_sum", tags, parts_a + parts_b, list(got_a) + list(got_b))
    full = dict(zip(tags, _rs_join_halves(halves, "rs_join_halves")))

    out_g, out_d, out_m, out_v = {}, {}, {}, {}
    for k in big_names:
        gs = [full[(k, layer)] for layer in range(weights[k].shape[0])]
        out_g[k], out_d[k], out_m[k], out_v[k] = _adamw(weights[k], gs, mom1[k], mom2[k], "adamw_" + k)
    shp = (1, DEPTH * d, ADA_SHARD)
    res = _adamw(ada_w.reshape(shp), [grad_ada_w.reshape(shp[1:])], m_ada_w.reshape(shp), v_ada_w.reshape(shp),
                 "adamw_ada_w")
    out_g["ada_w"], out_d["ada_w"], out_m["ada_w"], out_v["ada_w"] = [r.reshape(ada_w.shape) for r in res]
    for k in _SMALL_ORDER:
        shp = (1, weights[k].size // weights[k].shape[-1], weights[k].shape[-1])
        res = _adamw(weights[k].reshape(shp), [grads[k].reshape(shp[1:])], mom1[k].reshape(shp),
                     mom2[k].reshape(shp), "adamw_" + k)
        out_g[k], out_d[k], out_m[k], out_v[k] = [r.reshape(weights[k].shape) for r in res]
    for dst in (out_g, out_d, out_m, out_v):
        for k in transposed:
            dst[k] = jnp.transpose(dst[k], (0, 2, 1))

    return (loss, dx0.reshape(x.shape), *[out_g[k] for k in _WEIGHT_ORDER], *[out_d[k] for k in _WEIGHT_ORDER],
            *[out_m[k] for k in _WEIGHT_ORDER], *[out_v[k] for k in _WEIGHT_ORDER])
```

```python
import functools

import numpy as np
import jax
import jax.numpy as jnp
from jax import lax
from jax.experimental import pallas as pl
from jax.experimental.pallas import tpu as pltpu

F32 = jnp.float32
BF16 = jnp.bfloat16

D_MODEL = 1024
SEQ = 4096
N_DEV = 8
N_CHIP = 4
DEPTH = 2
HEADS = 8
HEAD_DIM = 128
GROUPS = ((128, 1), (512, 4), (2048, 16))
ATT_BLK = 128
ATT_WAYS = 4
ATT_STEPS = SEQ // ATT_BLK // ATT_WAYS
NUM_BUCKETS = 32
MAX_DISTANCE = 2048
FFN_HIDDEN = 2816
FFN_SHARD = FFN_HIDDEN // N_CHIP
HG_SUB = 16
HG_TC = 512
HG_HP = 4
RMS_EPS = 1e-6
NEG = -1e30
ATT_SCALE = HEAD_DIM ** -0.5
LOG2_E = 1.4426950408889634
ADAM_LR, ADAM_B1, ADAM_B2, ADAM_EPS, ADAM_WD, ADAM_STEP = 0.001, 0.9, 0.999, 1e-08, 0.01, 10
VMEM_LIMIT = 56 * 1024 * 1024
MESH = pl.DeviceIdType.MESH


def _pcall(body, **kw):
    return pl.pallas_call(body, **kw)


def _cparams(sem=None):
    if sem is None:
        return pltpu.CompilerParams(vmem_limit_bytes=VMEM_LIMIT)
    return pltpu.CompilerParams(dimension_semantics=sem, vmem_limit_bytes=VMEM_LIMIT)


def _sds(shape, dtype):
    return jax.ShapeDtypeStruct(shape, dtype)


def _dot(a, b):
    return jnp.dot(a, b, preferred_element_type=F32)


def _dot_nt(a, b):
    return lax.dot_general(a, b, (((1,), (1,)), ((), ())), preferred_element_type=F32)


def _dot_tn(a, b):
    return lax.dot_general(a, b, (((0,), (0,)), ((), ())), preferred_element_type=F32)


def _sigmoid(x):
    return 1.0 / (1.0 + jnp.exp(-x))


def _silu(x):
    return x * _sigmoid(x)


def _dsilu(x):
    s = _sigmoid(x)
    return s * (1.0 + x * (1.0 - s))


def _norm_mod(x_ref, g_ref, sc_ref, sh_ref):
    xv = x_ref[...]
    rs = lax.rsqrt(jnp.mean(xv * xv, axis=-1, keepdims=True) + RMS_EPS)
    return ((xv * rs * g_ref[...]) * (1.0 + sc_ref[...]) + sh_ref[...]).astype(BF16)


_NORM_SPECS = [pl.BlockSpec((1, D_MODEL), lambda i, t: (0, 0))] * 3


def _gated_branch_bwd(dx, z_ref, gate_ref, dz_ref, dgate_ref):
    dz_ref[...] = (dx * gate_ref[...]).astype(BF16)
    dgate_ref[...] += jnp.sum(dx * z_ref[...], axis=0, keepdims=True)


def _norm_mod_bwd(x, gain, sc, sh, dh, dres, name, branch=None):
    tm = 512
    n_b = 2 if branch else 0

    def body(*refs):
        x_ref, g_ref, sc_ref, sh_ref, dh_ref, dres_ref = refs[:6]
        dx_ref, dsc_ref, dsh_ref, dg_ref = refs[6 + n_b:10 + n_b]

        @pl.when(pl.program_id(0) == 0)
        def _():
            dsc_ref[...] = jnp.zeros_like(dsc_ref)
            dsh_ref[...] = jnp.zeros_like(dsh_ref)
            dg_ref[...] = jnp.zeros_like(dg_ref)
            if branch:
                refs[11 + n_b][...] = jnp.zeros_like(refs[11 + n_b])

        xv = x_ref[...]
        dhv = dh_ref[...]
        rs = lax.rsqrt(jnp.mean(xv * xv, axis=-1, keepdims=True) + RMS_EPS)
        xh = xv * rs
        dsc_ref[...] += jnp.sum(dhv * (xh * g_ref[...]), axis=0, keepdims=True)
        dsh_ref[...] += jnp.sum(dhv, axis=0, keepdims=True)
        dhn = dhv * (1.0 + sc_ref[...])
        dg_ref[...] += jnp.sum(dhn * xh, axis=0, keepdims=True)
        dxh = dhn * g_ref[...]
        dx = dres_ref[...] + rs * (dxh - xh * jnp.mean(dxh * xh, axis=-1, keepdims=True))
        dx_ref[...] = dx
        if branch:
            _gated_branch_bwd(dx, refs[6], refs[7], refs[10 + n_b], refs[11 + n_b])

    vec = pl.BlockSpec((1, D_MODEL), lambda i: (0, 0))
    big = pl.BlockSpec((tm, D_MODEL), lambda i: (i, 0))
    return _pcall(
        body, name=name, grid=(SEQ // tm,),
        in_specs=[big, vec, vec, vec, big, big] + ([big, vec] if branch else []),
        out_specs=[big, vec, vec, vec] + ([big, vec] if branch else []),
        out_shape=[_sds((SEQ, D_MODEL), F32)] + [_sds((1, D_MODEL), F32)] * 3
        + ([_sds((SEQ, D_MODEL), BF16), _sds((1, D_MODEL), F32)] if branch else []),
        compiler_params=_cparams(("arbitrary",)),
    )(x, gain, sc, sh, dh, dres, *(branch or ()))


def _mm_cols(x, gain, sc, sh, wg, layer, *, n_blocks, width, tn, act_map, w_map, out_dtype, name, tm=1024):
    k = x.shape[1]
    n_tiles = n_blocks * width // tn

    def body(x_ref, g_ref, sc_ref, sh_ref, w_ref, o_ref, h_ref):
        @pl.when(pl.program_id(1) == 0)
        def _():
            h_ref[...] = _norm_mod(x_ref, g_ref, sc_ref, sh_ref)

        o_ref[...] = _dot(h_ref[...], w_ref[...]).astype(o_ref.dtype)

    rows = pl.BlockSpec((tm, k), lambda i, t: (i, 0))
    return _pcall(
        body, name=name, grid=(SEQ // tm, n_tiles),
        in_specs=[rows] + _NORM_SPECS
        + [pl.BlockSpec((None, None, k, tn), lambda i, t: (w_map(t)[0], layer, 0, w_map(t)[1]))],
        out_specs=[pl.BlockSpec((None, tm, tn), lambda i, t: (act_map(t)[0], i, act_map(t)[1])), rows],
        out_shape=[_sds((n_blocks, SEQ, width), out_dtype), _sds((SEQ, k), BF16)],
        compiler_params=_cparams(("parallel", "arbitrary")),
    )(x, gain, sc, sh, wg)


def _mm_cols_bwd_a(dout, wg, layer, *, group, name, tm=1024, scatter=()):
    n_blocks, _, width = dout.shape
    k = wg.shape[2]
    n_s = len(scatter)
    n_rows = SEQ // tm
    n_steps = n_blocks // group

    def body(*refs):
        d_ref, w_ref = refs[:2]
        o_ref = refs[2 + n_s]
        if n_s:
            comm_start, comm_wait = _rs_chips(refs[2:2 + n_s], refs[3 + n_s:3 + 2 * n_s], *refs[3 + 2 * n_s:])
            pl.when((pl.program_id(0) == 0) & (pl.program_id(1) == 0))(comm_start)
        acc = _dot_nt(d_ref[0], w_ref[0])
        for b in range(1, group):
            acc += _dot_nt(d_ref[b], w_ref[b])
        if n_steps == 1:
            o_ref[...] = acc
        else:
            @pl.when(pl.program_id(1) == 0)
            def _():
                o_ref[...] = acc

            @pl.when(pl.program_id(1) > 0)
            def _():
                o_ref[...] += acc
        if n_s:
            pl.when((pl.program_id(0) == n_rows - 1) & (pl.program_id(1) == n_steps - 1))(comm_wait)

    sem = pltpu.SemaphoreType.DMA((max(n_s, 1), 3))
    res = _pcall(
        body, name=name, grid=(n_rows, n_steps),
        in_specs=[pl.BlockSpec((group, tm, width), lambda i, t: (t, i, 0)),
                  pl.BlockSpec((group, None, k, width), lambda i, t: (t, layer, 0, 0))] + [_ANY] * n_s,
        out_specs=[pl.BlockSpec((tm, k), lambda i, t: (i, 0))] + [_ANY] * n_s,
        out_shape=[_sds((SEQ, k), F32)] + _rs_chips_shapes(scatter),
        scratch_shapes=[sem, sem] if n_s else [],
        compiler_params=_cparams(("arbitrary", "arbitrary") if n_s else ("parallel", "arbitrary")),
    )(dout, wg, *scatter)
    return res if n_s else res[0]


def _mm_cols_bwd_w(a, dout, *, ns, tn, act_map, w_map, n_tiles, name, tm=1024, n_out=N_CHIP):
    k = a.shape[1]

    def body(a_ref, d_ref, o_ref):
        @pl.when(pl.program_id(1) == 0)
        def _():
            o_ref[...] = jnp.zeros_like(o_ref)

        o_ref[...] += _dot_tn(a_ref[...], d_ref[...])

    return _pcall(
        body, name=name, grid=(n_tiles, SEQ // tm),
        in_specs=[pl.BlockSpec((tm, k), lambda t, i: (i, 0)),
                  pl.BlockSpec((None, tm, tn), lambda t, i: (act_map(t)[0], i, act_map(t)[1]))],
        out_specs=pl.BlockSpec((None, k, tn), lambda t, i: (w_map(t)[0], 0, w_map(t)[1])),
        out_shape=_sds((n_out, k, ns), F32),
        compiler_params=_cparams(("parallel", "arbitrary")),
    )(a, dout)


def _retile_cols(src, *, n_out, width_out, tn, src_map, dst_map, n_tiles, name):
    k = src.shape[1]

    def body(s_ref, o_ref):
        o_ref[...] = s_ref[...]

    return _pcall(
        body, name=name, grid=(n_tiles,),
        in_specs=[pl.BlockSpec((None, k, tn), lambda t: (src_map(t)[0], 0, src_map(t)[1]))],
        out_specs=pl.BlockSpec((None, k, tn), lambda t: (dst_map(t)[0], 0, dst_map(t)[1])),
        out_shape=_sds((n_out, k, width_out), src.dtype),
        compiler_params=_cparams(("parallel",)),
    )(src)


def _mm_rows(a4, wg, layer, x, gate, name, tm=512):
    ks = a4.shape[2]
    n = wg.shape[3]

    def body(a_ref, w_ref, x_ref, g_ref, z_ref, xn_ref):
        z = _dot(a_ref[0], w_ref[0])
        for s in range(1, N_CHIP):
            z += _dot(a_ref[s], w_ref[s])
        z_ref[...] = z.astype(BF16)
        xn_ref[...] = x_ref[...] + g_ref[...] * z

    big = pl.BlockSpec((tm, n), lambda i: (i, 0))
    return _pcall(
        body, name=name, grid=(SEQ // tm,),
        in_specs=[pl.BlockSpec((N_CHIP, tm, ks), lambda i: (0, i, 0)),
                  pl.BlockSpec((N_CHIP, None, ks, n), lambda i: (0, layer, 0, 0)),
                  big, pl.BlockSpec((1, n), lambda i: (0, 0))],
        out_specs=[big, big],
        out_shape=[_sds((SEQ, n), BF16), _sds((SEQ, n), F32)],
        compiler_params=_cparams(("parallel",)),
    )(a4, wg, x, gate)


def _mm_rows_bwd_a(dz, wg, layer, name, tm=1024, halves=()):
    ks, n = wg.shape[2], wg.shape[3]
    n_h = len(halves)
    n_rows = SEQ // tm

    def body(*refs):
        dz_ref, w_ref = refs[:2]
        o_ref = refs[2 + n_h]
        if n_h:
            comm_start, comm_wait = _rs_halves(refs[2:2 + n_h], refs[3 + n_h:3 + 2 * n_h], *refs[3 + 2 * n_h:])
            pl.when((pl.program_id(0) == 0) & (pl.program_id(1) == 0))(comm_start)
        o_ref[...] = _dot_nt(dz_ref[...], w_ref[...])
        if n_h:
            pl.when((pl.program_id(0) == n_rows - 1) & (pl.program_id(1) == N_CHIP - 1))(comm_wait)

    sem = pltpu.SemaphoreType.DMA((max(n_h, 1),))
    res = _pcall(
        body, name=name, grid=(n_rows, N_CHIP),
        in_specs=[pl.BlockSpec((tm, n), lambda i, s: (i, 0)),
                  pl.BlockSpec((None, None, ks, n), lambda i, s: (s, layer, 0, 0))] + [_ANY] * n_h,
        out_specs=[pl.BlockSpec((None, tm, ks), lambda i, s: (s, i, 0))] + [_ANY] * n_h,
        out_shape=[_sds((N_CHIP, SEQ, ks), F32)] + _rs_halves_shapes(halves),
        scratch_shapes=[sem, sem] if n_h else [],
        compiler_params=_cparams(("arbitrary", "arbitrary") if n_h else ("parallel", "arbitrary")),
    )(dz, wg, *halves)
    return res if n_h else res[0]


def _mm_rows_bwd_w(a4, dz, name, tm=2048):
    return _mm_rows_bwd_w_multi([a4], dz, name, tm)[0]


def _mm_rows_bwd_w_multi(a4s, dz, name, tm=2048):
    n_a = len(a4s)
    ks = a4s[0].shape[2]
    n = dz.shape[1]

    def body(*refs):
        dz_ref = refs[n_a]

        for j in range(n_a):
            o_ref = refs[n_a + 1 + j]

            @pl.when(pl.program_id(1) == 0)
            def _(o_ref=o_ref):
                o_ref[...] = jnp.zeros_like(o_ref)

            o_ref[...] += _dot_tn(refs[j][...], dz_ref[...])

    return _pcall(
        body, name=name, grid=(N_CHIP, SEQ // tm),
        in_specs=[pl.BlockSpec((None, tm, ks), lambda s, i: (s, i, 0))] * n_a
        + [pl.BlockSpec((tm, n), lambda s, i: (i, 0))],
        out_specs=[pl.BlockSpec((None, ks, n), lambda s, i: (s, 0, 0))] * n_a,
        out_shape=[_sds((N_CHIP, ks, n), F32)] * n_a,
        compiler_params=_cparams(("parallel", "arbitrary")),
    )(*a4s, dz)


def _ffn_up(x, gain, sc, sh, w1g, w3g, layer, name, tm=1024):
    def body(x_ref, g_ref, sc_ref, sh_ref, w1_ref, w3_ref, a1_ref, a3_ref, u_ref, h_ref):
        @pl.when(pl.program_id(1) == 0)
        def _():
            h_ref[...] = _norm_mod(x_ref, g_ref, sc_ref, sh_ref)

        hv = h_ref[...]
        a1 = _dot_nt(hv, w1_ref[...])
        a3 = _dot_nt(hv, w3_ref[...])
        a1_ref[...] = a1.astype(BF16)
        a3_ref[...] = a3.astype(BF16)
        u_ref[...] = (_silu(a1) * a3).astype(BF16)

    wspec = pl.BlockSpec((None, None, FFN_SHARD, D_MODEL), lambda i, s: (s, layer, 0, 0))
    ospec = pl.BlockSpec((None, tm, FFN_SHARD), lambda i, s: (s, i, 0))
    shp = (N_CHIP, SEQ, FFN_SHARD)
    rows = pl.BlockSpec((tm, D_MODEL), lambda i, s: (i, 0))
    return _pcall(
        body, name=name, grid=(SEQ // tm, N_CHIP),
        in_specs=[rows] + _NORM_SPECS + [wspec, wspec],
        out_specs=[ospec, ospec, ospec, rows],
        out_shape=[_sds(shp, BF16), _sds(shp, BF16), _sds(shp, BF16), _sds((SEQ, D_MODEL), BF16)],
        compiler_params=_cparams(("parallel", "arbitrary")),
    )(x, gain, sc, sh, w1g, w3g)


def _ffn_up_bwd(da1, da3, w1g, w3g, layer, name, tm=512):
    def body(d1_ref, d3_ref, w1_ref, w3_ref, o_ref):
        acc = _dot(d1_ref[0], w1_ref[0]) + _dot(d3_ref[0], w3_ref[0])
        for s in range(1, N_CHIP):
            acc += _dot(d1_ref[s], w1_ref[s]) + _dot(d3_ref[s], w3_ref[s])
        o_ref[...] = acc

    wspec = pl.BlockSpec((N_CHIP, None, FFN_SHARD, D_MODEL), lambda i: (0, layer, 0, 0))
    dspec = pl.BlockSpec((N_CHIP, tm, FFN_SHARD), lambda i: (0, i, 0))
    return _pcall(
        body, name=name, grid=(SEQ // tm,),
        in_specs=[dspec, dspec, wspec, wspec],
        out_specs=pl.BlockSpec((tm, D_MODEL), lambda i: (i, 0)),
        out_shape=_sds((SEQ, D_MODEL), F32),
        compiler_params=_cparams(("parallel",)),
    )(da1, da3, w1g, w3g)


def _ffn_down_bwd(dz, w2g, layer, a1, a3, name, tm=1024, halves=()):
    n_h = len(halves)
    n_rows = SEQ // tm

    def body(*refs):
        dz_ref, w_ref, a1_ref, a3_ref = refs[:4]
        da1_ref, da3_ref = refs[4 + n_h:6 + n_h]
        if n_h:
            comm_start, comm_wait = _rs_halves(refs[4:4 + n_h], refs[6 + n_h:6 + 2 * n_h], *refs[6 + 2 * n_h:])
            pl.when((pl.program_id(0) == 0) & (pl.program_id(1) == 0))(comm_start)
        du = _dot_nt(dz_ref[...], w_ref[...])
        a1 = a1_ref[...].astype(F32)
        da1_ref[...] = (du * a3_ref[...].astype(F32) * _dsilu(a1)).astype(BF16)
        da3_ref[...] = (du * _silu(a1)).astype(BF16)
        if n_h:
            pl.when((pl.program_id(0) == n_rows - 1) & (pl.program_id(1) == N_CHIP - 1))(comm_wait)

    blk = pl.BlockSpec((None, tm, FFN_SHARD), lambda i, s: (s, i, 0))
    shp = (N_CHIP, SEQ, FFN_SHARD)
    sem = pltpu.SemaphoreType.DMA((max(n_h, 1),))
    return _pcall(
        body, name=name, grid=(n_rows, N_CHIP),
        in_specs=[pl.BlockSpec((tm, D_MODEL), lambda i, s: (i, 0)),
                  pl.BlockSpec((None, None, FFN_SHARD, D_MODEL), lambda i, s: (s, layer, 0, 0)),
                  blk, blk] + [_ANY] * n_h,
        out_specs=[blk, blk] + [_ANY] * n_h,
        out_shape=[_sds(shp, BF16), _sds(shp, BF16)] + _rs_halves_shapes(halves),
        scratch_shapes=[sem, sem] if n_h else [],
        compiler_params=_cparams(("arbitrary", "arbitrary") if n_h else ("parallel", "arbitrary")),
    )(dz, w2g, a1, a3, *halves)


def _loss_head(y, target, z, gate, name):
    tm = 512
    n_steps = SEQ // tm

    def body(y_ref, t_ref, z_ref, gate_ref, dy_ref, l_ref, dz_ref, dgate_ref, acc_ref):
        @pl.when(pl.program_id(0) == 0)
        def _():
            acc_ref[...] = jnp.zeros_like(acc_ref)
            dgate_ref[...] = jnp.zeros_like(dgate_ref)

        err = y_ref[...] - t_ref[...]
        dy = err * (1.0 / D_MODEL)
        dy_ref[...] = dy
        acc_ref[...] += jnp.sum(jnp.mean(err * err, axis=-1, keepdims=True), axis=0, keepdims=True)
        _gated_branch_bwd(dy, z_ref, gate_ref, dz_ref, dgate_ref)

        @pl.when(pl.program_id(0) == n_steps - 1)
        def _():
            l_ref[...] = 0.5 * acc_ref[...]

    big = pl.BlockSpec((tm, D_MODEL), lambda i: (i, 0))
    vec = pl.BlockSpec((1, D_MODEL), lambda i: (0, 0))
    return _pcall(
        body, name=name, grid=(n_steps,),
        in_specs=[big, big, big, vec],
        out_specs=[big, pl.BlockSpec((1, 1), lambda i: (0, 0)), big, vec],
        out_shape=[_sds((SEQ, D_MODEL), F32), _sds((1, 1), F32), _sds((SEQ, D_MODEL), BF16),
                   _sds((1, D_MODEL), F32)],
        scratch_shapes=[pltpu.VMEM((1, 1), F32)],
        compiler_params=_cparams(("arbitrary",)),
    )(y, target, z, gate)


def _attn_rows(base, d):
    if d == 1:
        return pl.ds(pl.multiple_of(base, ATT_BLK), ATT_BLK)
    return pl.ds(base, ATT_BLK, stride=d)


def _attn_block_index(i, d):
    nb = SEQ // (ATT_BLK * d)
    r = i // nb
    n = i % nb
    base = r + n * (ATT_BLK * d)
    pbase = jnp.maximum(base - ATT_BLK * d, r)
    return n, _attn_rows(base, d), _attn_rows(pbase, d)


def _attn_two_blocks(ref, prow, rows):
    return jnp.concatenate([ref[prow, :].astype(BF16), ref[rows, :].astype(BF16)], axis=0)


def _attn_block_bias(b_ref, n):
    b = b_ref[...]
    prev_half = lax.broadcasted_iota(jnp.int32, b.shape, 1) < ATT_BLK
    return jnp.where(prev_half & (n == 0), NEG, b)


def _qk_normed(x):
    rs = lax.rsqrt(jnp.mean(x * x, axis=-1, keepdims=True) + RMS_EPS)
    return x * rs, rs


def _attn_fwd(qkv9, qgain, kgain, bias, name, gather=()):
    n_g = len(gather)

    def body(*refs):
        q_ref, k_ref, v_ref, qg_ref, kg_ref, b_ref = refs[:6]
        o_ref, lse_ref = refs[6 + n_g:8 + n_g]
        qn_s, kn_s, acc_s, m_s, l_s = refs[8 + 2 * n_g:13 + 2 * n_g]
        g = pl.program_id(1)
        if n_g:
            comm_start, comm_wait = _gather_ici(refs[8 + n_g:8 + 2 * n_g], *refs[13 + 2 * n_g:])
            pl.when((pl.program_id(0) == 0) & (g == 0))(comm_start)

        @pl.when(g == 0)
        def _():
            m_s[...] = jnp.full_like(m_s, NEG)
            l_s[...] = jnp.zeros_like(l_s)
            acc_s[...] = jnp.zeros_like(acc_s)

        qn_s[...] = _qk_normed(q_ref[...])[0] * qg_ref[...]
        kn_s[...] = _qk_normed(k_ref[...])[0] * kg_ref[...]

        for gi, (_, d) in enumerate(GROUPS):
            @pl.when(g == gi)
            def _(d=d):
                def block(n, qb, kk, vv, m_old, l_old, acc_old):
                    s = _dot_nt(qb, kk) * ATT_SCALE + _attn_block_bias(b_ref, n)
                    m_new = jnp.maximum(m_old, jnp.max(s, axis=-1, keepdims=True))
                    alpha = jnp.exp(m_old - m_new)
                    p = jnp.exp(s - m_new)
                    l_new = alpha * l_old + jnp.sum(p, axis=-1, keepdims=True)
                    acc_new = alpha * acc_old + _dot(p.astype(BF16), vv)
                    return m_new, l_new, acc_new

                def it(i, carry):
                    where, loaded = [], []
                    for way in range(ATT_WAYS):
                        n, rows, prow = _attn_block_index(i + way * ATT_STEPS, d)
                        where.append(rows)
                        loaded.append((n, qn_s[rows, :].astype(BF16), _attn_two_blocks(kn_s, prow, rows),
                                       _attn_two_blocks(v_ref, prow, rows), m_s[rows, :], l_s[rows, :],
                                       acc_s[rows, :]))
                    results = [block(*vals) for vals in loaded]
                    for rows, (m_new, l_new, acc_new) in zip(where, results):
                        m_s[rows, :] = m_new
                        l_s[rows, :] = l_new
                        acc_s[rows, :] = acc_new
                    return carry

                lax.fori_loop(0, ATT_STEPS, it, 0)

        @pl.when(g == len(GROUPS) - 1)
        def _():
            o_ref[...] = (acc_s[...] / l_s[...]).astype(BF16)
            lse_ref[...] = m_s[...] + jnp.log(l_s[...])

        if n_g:
            pl.when((pl.program_id(0) == HEADS - 1) & (g == len(GROUPS) - 1))(comm_wait)

    def col(j):
        return pl.BlockSpec((None, SEQ, HEAD_DIM), lambda h, g: (g * 3 + j, 0, h))

    gspec = pl.BlockSpec((None, 1, HEAD_DIM), lambda h, g: (g, 0, 0))
    sem = pltpu.SemaphoreType.DMA((max(n_g, 1), 3))
    return _pcall(
        body, name=name, grid=(HEADS, len(GROUPS)),
        in_specs=[col(0), col(1), col(2), gspec, gspec,
                  pl.BlockSpec((None, None, ATT_BLK, 2 * ATT_BLK), lambda h, g: (g, h, 0, 0))] + [_ANY] * n_g,
        out_specs=[pl.BlockSpec((None, SEQ, HEAD_DIM), lambda h, g: (h // 2, 0, h % 2)),
                   pl.BlockSpec((None, SEQ, 1), lambda h, g: (h, 0, 0))] + [_ANY] * n_g,
        out_shape=[_sds((N_CHIP, SEQ, 2 * HEAD_DIM), BF16), _sds((HEADS, SEQ, 1), F32)]
        + [_sds(s.shape, s.dtype) for s in gather],
        input_output_aliases={6 + a: 2 + a for a in range(n_g)},
        scratch_shapes=[pltpu.VMEM((SEQ, HEAD_DIM), F32)] * 3 + [pltpu.VMEM((SEQ, 1), F32)] * 2
        + ([sem, sem] if n_g else []),
        compiler_params=_cparams(("arbitrary", "arbitrary")),
    )(qkv9, qkv9, qkv9, qgain, kgain, bias, *gather)


def _attn_bwd(qkv9, qgain, kgain, bias, do4, o4, lse, name, scatter=()):
    n_s = len(scatter)

    def body(*refs):
        q_ref, k_ref, v_ref, qg_ref, kg_ref, b_ref, do_ref, o_ref, lse_ref = refs[:9]
        dqkv_ref, dqg_ref, dkg_ref, db_ref = refs[9 + n_s:13 + n_s]
        qn_s, kn_s, dq_s, dk_s, dv_s, dl_s = refs[13 + 2 * n_s:19 + 2 * n_s]
        g = pl.program_id(1)
        if n_s:
            comm_start, comm_wait = _rs_chips(refs[9:9 + n_s], refs[13 + n_s:13 + 2 * n_s], *refs[19 + 2 * n_s:])
            pl.when((pl.program_id(0) == 0) & (g == 0))(comm_start)
        qh, rq = _qk_normed(q_ref[...])
        kh, rk = _qk_normed(k_ref[...])
        qn_s[...] = qh * qg_ref[...]
        kn_s[...] = kh * kg_ref[...]
        @pl.when(g == 0)
        def _():
            dl_s[...] = jnp.sum(do_ref[...] * o_ref[...].astype(F32), axis=-1, keepdims=True)

        dk_s[...] = jnp.zeros_like(dk_s)
        dv_s[...] = jnp.zeros_like(dv_s)
        db_ref[...] = jnp.zeros_like(db_ref)

        for gi, (_, d) in enumerate(GROUPS):
            @pl.when(g == gi)
            def _(d=d):
                def block(n, qb, kk, vv, dob, lse_b, dl):
                    s = _dot_nt(qb, kk) * ATT_SCALE + _attn_block_bias(b_ref, n)
                    p = jnp.exp(s - lse_b)
                    ds = p * (_dot_nt(dob, vv) - dl)
                    ds16 = ds.astype(BF16)
                    return (ds, _dot(ds16, kk) * ATT_SCALE, _dot_tn(ds16, qb) * ATT_SCALE,
                            _dot_tn(p.astype(BF16), dob))

                def it(i, carry):
                    where, loaded, old = [], [], []
                    for way in range(ATT_WAYS):
                        n, rows, prow = _attn_block_index(i + way * ATT_STEPS, d)
                        where.append((rows, prow))
                        loaded.append((n, qn_s[rows, :].astype(BF16), _attn_two_blocks(kn_s, prow, rows),
                                       _attn_two_blocks(v_ref, prow, rows), do_ref[rows, :].astype(BF16),
                                       lse_ref[rows, :], dl_s[rows, :]))
                        old.append((dk_s[rows, :], dk_s[prow, :], dv_s[rows, :], dv_s[prow, :]))
                    results = [block(*vals) for vals in loaded]
                    db_ref[...] += functools.reduce(lambda a, b: a + b, [r[0] for r in results])
                    for (rows, prow), (dk_c, dk_p, dv_c, dv_p), (_, dq, dkk, dvv) in zip(where, old, results):
                        dq_s[rows, :] = dq
                        dk_s[prow, :] = dk_p + dkk[:ATT_BLK]
                        dv_s[prow, :] = dv_p + dvv[:ATT_BLK]
                        dk_s[rows, :] = dk_c + dkk[ATT_BLK:]
                        dv_s[rows, :] = dv_c + dvv[ATT_BLK:]
                    return carry

                lax.fori_loop(0, ATT_STEPS, it, 0)

        def norm_bwd(dn, xh, rs, gain):
            dgain = jnp.sum(dn * xh, axis=0, keepdims=True)
            dxh = dn * gain
            return rs * (dxh - xh * jnp.mean(dxh * xh, axis=-1, keepdims=True)), dgain

        dq, dqg = norm_bwd(dq_s[...], qh, rq, qg_ref[...])
        dk, dkg = norm_bwd(dk_s[...], kh, rk, kg_ref[...])
        dqkv_ref[0] = dq.astype(BF16)
        dqkv_ref[1] = dk.astype(BF16)
        dqkv_ref[2] = dv_s[...].astype(BF16)
        dqg_ref[...] = dqg
        dkg_ref[...] = dkg
        if n_s:
            pl.when((pl.program_id(0) == HEADS - 1) & (g == len(GROUPS) - 1))(comm_wait)

    def col(j):
        return pl.BlockSpec((None, SEQ, HEAD_DIM), lambda h, g: (g * 3 + j, 0, h))

    gspec = pl.BlockSpec((None, 1, HEAD_DIM), lambda h, g: (g, 0, 0))
    bspec = pl.BlockSpec((None, None, ATT_BLK, 2 * ATT_BLK), lambda h, g: (g, h, 0, 0))
    hcol = pl.BlockSpec((None, SEQ, HEAD_DIM), lambda h, g: (h // 2, 0, h % 2))
    dgspec = pl.BlockSpec((None, None, 1, HEAD_DIM), lambda h, g: (h, g, 0, 0))
    ng = len(GROUPS)
    sem = pltpu.SemaphoreType.DMA((max(n_s, 1), 3))
    return _pcall(
        body, name=name, grid=(HEADS, ng),
        in_specs=[col(0), col(1), col(2), gspec, gspec, bspec, hcol, hcol,
                  pl.BlockSpec((None, SEQ, 1), lambda h, g: (h, 0, 0))] + [_ANY] * n_s,
        out_specs=[pl.BlockSpec((None, 3, SEQ, HEAD_DIM), lambda h, g: (g, 0, 0, h)), dgspec, dgspec, bspec]
        + [_ANY] * n_s,
        out_shape=[_sds((ng, 3, SEQ, D_MODEL), BF16), _sds((HEADS, ng, 1, HEAD_DIM), F32),
                   _sds((HEADS, ng, 1, HEAD_DIM), F32), _sds((ng, HEADS, ATT_BLK, 2 * ATT_BLK), F32)]
        + _rs_chips_shapes(scatter),
        scratch_shapes=[pltpu.VMEM((SEQ, HEAD_DIM), F32)] * 5 + [pltpu.VMEM((SEQ, 1), F32)]
        + ([sem, sem] if n_s else []),
        compiler_params=_cparams(("arbitrary", "arbitrary")),
    )(qkv9, qkv9, qkv9, qgain, kgain, bias, do4, o4, lse, *scatter)


def _relbias_bwd(dbias, bucket_idx, name):
    ng = len(GROUPS)

    def body(db_ref, idx_ref, o_ref):
        lane = lax.broadcasted_iota(jnp.int32, (HEADS, 128), 1)
        acc = jnp.zeros((HEADS, 128), F32)
        for g in range(ng):
            dbg = db_ref[g]
            idx = idx_ref[g]
            for b in range(NUM_BUCKETS):
                sel = jnp.where((idx == b)[None], dbg, 0.0)
                part = jnp.sum(sel, axis=1)
                val = jnp.sum(part, axis=-1, keepdims=True)
                acc = jnp.where(lane == g * NUM_BUCKETS + b, val, acc)
        o_ref[...] = acc

    return _pcall(body, name=name, out_shape=_sds((HEADS, 128), F32), compiler_params=_cparams())(dbias, bucket_idx)


def _scan16(x, reverse=False):
    row = lax.broadcasted_iota(jnp.int32, x.shape, 0)
    for sh in (1, 2, 4, 8):
        if reverse:
            x = x + jnp.where(row < HG_SUB - sh, pltpu.roll(x, HG_SUB - sh, 0), 0.0)
        else:
            x = x + jnp.where(row >= sh, pltpu.roll(x, sh, 0), 0.0)
    return x


def _hgrn_gates(qr, fr, lbv):
    q = _silu(qr)
    sig = _sigmoid(fr)
    fg = lbv + (1.0 - lbv) * sig
    lf = jnp.log(fg) * LOG2_E
    gcum = _scan16(lf)
    glast = jnp.sum(lf, axis=0, keepdims=True)
    return q, sig, fg, 1.0 - fg, gcum, glast


def _hgrn_intra(q, k, gcum, tri):
    e = jnp.exp2(jnp.where(tri, gcum[:, None, :] - gcum[None, :, :], NEG))
    a = jnp.sum(q[:, None, :] * k[None, :, :] * e, axis=-1, keepdims=True)
    return e, a


def _hgrn_fwd(proj4, lb, gain, name):
    nsub = HG_TC // HG_SUB
    wide = HG_HP * HEAD_DIM

    def body(p_ref, lb_ref, gn_ref, o_ref, y_ref, st_ref, state_s):
        @pl.when(pl.program_id(1) == 0)
        def _():
            state_s[...] = jnp.zeros_like(state_s)

        gnv = gn_ref[...]
        shp = (HG_SUB, HG_SUB, HEAD_DIM)
        tri = lax.broadcasted_iota(jnp.int32, shp, 0) >= lax.broadcasted_iota(jnp.int32, shp, 1)

        def head(qr, fr, vv, gr, lbv, st):
            q, _, _, k, gcum, glast = _hgrn_gates(qr, fr, lbv)
            _, a = _hgrn_intra(q, k, gcum, tri)
            o = jnp.sum(a * vv[None, :, :], axis=1) + _dot_nt((q * jnp.exp2(gcum)).astype(BF16), st.astype(BF16))
            kg = k * jnp.exp2(glast - gcum)
            st_new = st * jnp.exp2(glast) + _dot_tn(vv.astype(BF16), kg.astype(BF16))
            rs = lax.rsqrt(jnp.mean(o * o, axis=-1, keepdims=True) + RMS_EPS)
            return o, (o * rs * gnv * _silu(gr)).astype(BF16), st_new

        def it(i, carry):
            rows = pl.ds(pl.multiple_of(i * HG_SUB, HG_SUB), HG_SUB)
            loaded = []
            for hh in range(HG_HP):
                lanes = pl.ds(hh * HEAD_DIM, HEAD_DIM)
                loaded.append(([p_ref[j, rows, lanes] for j in range(4)], lb_ref[:, lanes], state_s[hh]))
            results = [head(blk[0], blk[1], blk[2], blk[3], lbv, st) for blk, lbv, st in loaded]
            for hh, ((_, _, st), (o, y, st_new)) in enumerate(zip(loaded, results)):
                lanes = pl.ds(hh * HEAD_DIM, HEAD_DIM)
                st_ref[hh, i] = st.astype(BF16)
                state_s[hh] = st_new
                o_ref[rows, lanes] = o
                y_ref[hh // 2, rows, pl.ds((hh % 2) * HEAD_DIM, HEAD_DIM)] = y
            return carry

        lax.fori_loop(0, nsub, it, 0)

    return _pcall(
        body, name=name, grid=(HEADS // HG_HP, SEQ // HG_TC),
        in_specs=[pl.BlockSpec((4, HG_TC, wide), lambda h, j: (0, j, h)),
                  pl.BlockSpec((1, wide), lambda h, j: (0, h)),
                  pl.BlockSpec((1, HEAD_DIM), lambda h, j: (0, 0))],
        out_specs=[pl.BlockSpec((HG_TC, wide), lambda h, j: (j, h)),
                   pl.BlockSpec((HG_HP // 2, HG_TC, 2 * HEAD_DIM), lambda h, j: (h, j, 0)),
                   pl.BlockSpec((HG_HP, nsub, HEAD_DIM, HEAD_DIM), lambda h, j: (h, j, 0, 0))],
        out_shape=[_sds((SEQ, D_MODEL), F32), _sds((N_CHIP, SEQ, 2 * HEAD_DIM), BF16),
                   _sds((HEADS, SEQ // HG_SUB, HEAD_DIM, HEAD_DIM), BF16)],
        scratch_shapes=[pltpu.VMEM((HG_HP, HEAD_DIM, HEAD_DIM), F32)],
        compiler_params=_cparams(("parallel", "arbitrary")),
    )(proj4, lb, gain)


def _hgrn_bwd(proj4, lb, gain, o_raw, dy4, states, name):
    nsub = HG_TC // HG_SUB
    nt = SEQ // HG_TC
    wide = HG_HP * HEAD_DIM

    def body(p_ref, lb_ref, gn_ref, o_ref, dy_ref, st_ref, dp_ref, dlb_ref, dgn_ref, dst_s):
        @pl.when(pl.program_id(1) == 0)
        def _():
            dst_s[...] = jnp.zeros_like(dst_s)
            dlb_ref[...] = jnp.zeros_like(dlb_ref)
            dgn_ref[...] = jnp.zeros_like(dgn_ref)

        gnv = gn_ref[...]
        shp = (HG_SUB, HG_SUB, HEAD_DIM)
        tri = lax.broadcasted_iota(jnp.int32, shp, 0) >= lax.broadcasted_iota(jnp.int32, shp, 1)

        def head(qr, fr, vv, gr, o, dy, lbv, st0, dst):
            q, sig, fg, k, gcum, glast = _hgrn_gates(qr, fr, lbv)
            rs = lax.rsqrt(jnp.mean(o * o, axis=-1, keepdims=True) + RMS_EPS)
            oh = o * rs
            don = dy * _silu(gr)
            dgn = jnp.sum(don * oh, axis=0, keepdims=True)
            dgr = dy * oh * gnv * _dsilu(gr)
            doh = don * gnv
            do = rs * (doh - oh * jnp.mean(doh * oh, axis=-1, keepdims=True))
            dst16 = dst.astype(BF16)
            do16 = do.astype(BF16)
            eg = jnp.exp2(gcum)
            eb = jnp.exp2(glast - gcum)
            e, a = _hgrn_intra(q, k, gcum, tri)
            da = jnp.sum(do[:, None, :] * vv[None, :, :], axis=-1, keepdims=True)
            dae = da * e
            dq = jnp.sum(dae * k[None, :, :], axis=1) + eg * _dot(do16, st0)
            dk_state = eb * _dot(vv.astype(BF16), dst16)
            dk = jnp.sum(dae * q[:, None, :], axis=0) + dk_state
            dv = jnp.sum(a * do[:, None, :], axis=0) + _dot_nt((k * eb).astype(BF16), dst16)
            eglast = jnp.exp2(glast)
            dst_new = dst * eglast + _dot_tn(do16, (q * eg).astype(BF16))
            dglast = jnp.sum(k * dk_state, axis=0, keepdims=True) \
                + eglast * jnp.sum(dst * st0.astype(F32), axis=0, keepdims=True)
            dlf = _scan16(q * dq - k * dk, reverse=True) + dglast
            dfg = dlf / fg - dk
            dlb = jnp.sum(dfg * (1.0 - sig), axis=0, keepdims=True)
            dproj = ((dq * _dsilu(qr)).astype(BF16), (dfg * (1.0 - lbv) * sig * (1.0 - sig)).astype(BF16),
                     dv.astype(BF16), dgr.astype(BF16))
            return dproj, dst_new, dlb, dgn

        def it(ii, carry):
            i = nsub - 1 - ii
            rows = pl.ds(pl.multiple_of(i * HG_SUB, HG_SUB), HG_SUB)
            results = []
            for hh in range(HG_HP):
                lanes = pl.ds(hh * HEAD_DIM, HEAD_DIM)
                blk = [p_ref[j, rows, lanes] for j in range(4)]
                dy = dy_ref[hh // 2, rows, pl.ds((hh % 2) * HEAD_DIM, HEAD_DIM)]
                results.append(head(blk[0], blk[1], blk[2], blk[3], o_ref[rows, lanes], dy,
                                    lb_ref[:, lanes], st_ref[hh, i], dst_s[hh]))
            new_carry = []
            for hh, (dproj, dst_new, dlb, dgn) in enumerate(results):
                lanes = pl.ds(hh * HEAD_DIM, HEAD_DIM)
                dst_s[hh] = dst_new
                for j in range(4):
                    dp_ref[j, rows, lanes] = dproj[j]
                new_carry.append((carry[hh][0] + dlb, carry[hh][1] + dgn))
            return tuple(new_carry)

        zero = jnp.zeros((1, HEAD_DIM), F32)
        sums = lax.fori_loop(0, nsub, it, tuple((zero, zero) for _ in range(HG_HP)))
        for hh in range(HG_HP):
            dlb_ref[hh] += sums[hh][0]
            dgn_ref[hh] += sums[hh][1]

    vspec = pl.BlockSpec((HG_HP, 1, HEAD_DIM), lambda h, j: (h, 0, 0))
    return _pcall(
        body, name=name, grid=(HEADS // HG_HP, nt),
        in_specs=[pl.BlockSpec((4, HG_TC, wide), lambda h, j: (0, nt - 1 - j, h)),
                  pl.BlockSpec((1, wide), lambda h, j: (0, h)),
                  pl.BlockSpec((1, HEAD_DIM), lambda h, j: (0, 0)),
                  pl.BlockSpec((HG_TC, wide), lambda h, j: (nt - 1 - j, h)),
                  pl.BlockSpec((HG_HP // 2, HG_TC, 2 * HEAD_DIM), lambda h, j: (h, nt - 1 - j, 0)),
                  pl.BlockSpec((HG_HP, nsub, HEAD_DIM, HEAD_DIM), lambda h, j: (h, nt - 1 - j, 0, 0))],
        out_specs=[pl.BlockSpec((4, HG_TC, wide), lambda h, j: (0, nt - 1 - j, h)), vspec, vspec],
        out_shape=[_sds((4, SEQ, D_MODEL), BF16), _sds((HEADS, 1, HEAD_DIM), F32), _sds((HEADS, 1, HEAD_DIM), F32)],
        scratch_shapes=[pltpu.VMEM((HG_HP, HEAD_DIM, HEAD_DIM), F32)],
        compiler_params=_cparams(("parallel", "arbitrary")),
    )(proj4, lb, gain, o_raw, dy4, states)


def _t5_bucket(dist):
    n = np.asarray(dist, dtype=np.int64)
    max_exact = NUM_BUCKETS // 2
    large = max_exact + (np.log(np.maximum(n, 1) / max_exact) / np.log(MAX_DISTANCE / max_exact)
                         * (NUM_BUCKETS - max_exact)).astype(np.int64)
    large = np.minimum(large, NUM_BUCKETS - 1)
    return np.where(n < max_exact, n, large).astype(np.int32)


def _bias_tables():
    qi = np.arange(ATT_BLK)[:, None]
    ki = np.arange(2 * ATT_BLK)[None, :]
    j = ATT_BLK + qi - ki
    valid = (j >= 0) & (j <= ATT_BLK)
    return np.stack([np.where(valid, _t5_bucket(np.clip(j, 0, ATT_BLK) * d), -1) for _, d in GROUPS]).astype(np.int32)


def _attn_bias(rel_bias, name):
    idx = _bias_tables()
    ng = len(GROUPS)
    buckets = [sorted(set(idx[g][idx[g] >= 0].tolist())) for g in range(ng)]

    def body(rb_ref, idx_ref, o_ref):
        h = pl.program_id(0)
        for g in range(ng):
            ig = idx_ref[g]
            acc = jnp.full(ig.shape, NEG, F32)
            for b in buckets[g]:
                acc = jnp.where(ig == b, rb_ref[b, g * HEADS + h], acc)
            o_ref[g] = acc

    return _pcall(
        body, name=name, grid=(HEADS,),
        in_specs=[pl.BlockSpec(memory_space=pltpu.SMEM),
                  pl.BlockSpec((ng, ATT_BLK, 2 * ATT_BLK), lambda h: (0, 0, 0))],
        out_specs=pl.BlockSpec((ng, None, ATT_BLK, 2 * ATT_BLK), lambda h: (0, h, 0, 0)),
        out_shape=_sds((ng, HEADS, ATT_BLK, 2 * ATT_BLK), F32),
        compiler_params=_cparams(("parallel",)),
    )(rel_bias, jnp.asarray(idx))


ADA_SHARD = 6 * D_MODEL // N_CHIP
ADA_TN = 512


def _ada_fwd(c_all, ada_w, ada_b_cols, name):
    def body(c_ref, w_ref, b_ref, o_ref):
        ca = _silu(c_ref[...]).astype(BF16)
        o_ref[...] = _dot(ca, w_ref[...].astype(BF16)) + b_ref[...]

    return _pcall(
        body, name=name, grid=(DEPTH, ADA_SHARD // ADA_TN),
        in_specs=[pl.BlockSpec((N_DEV, D_MODEL), lambda l, j: (0, 0)),
                  pl.BlockSpec((None, D_MODEL, ADA_TN), lambda l, j: (l, 0, j)),
                  pl.BlockSpec((None, 1, ADA_TN), lambda l, j: (l, 0, j))],
        out_specs=pl.BlockSpec((None, N_DEV, ADA_TN), lambda l, j: (l, 0, j)),
        out_shape=_sds((DEPTH, N_DEV, ADA_SHARD), F32),
        compiler_params=_cparams(("parallel", "parallel")),
    )(c_all, ada_w, ada_b_cols)


def _ada_bwd(c_all, dmod_cols, name):
    def body(c_ref, d_ref, o_ref):
        ca = _silu(c_ref[...]).astype(BF16)
        o_ref[...] = _dot_tn(ca, d_ref[...].astype(BF16))

    return _pcall(
        body, name=name, grid=(DEPTH, ADA_SHARD // ADA_TN),
        in_specs=[pl.BlockSpec((N_DEV, D_MODEL), lambda l, j: (0, 0)),
                  pl.BlockSpec((None, N_DEV, ADA_TN), lambda l, j: (l, 0, j))],
        out_specs=pl.BlockSpec((None, D_MODEL, ADA_TN), lambda l, j: (l, 0, j)),
        out_shape=_sds((DEPTH, D_MODEL, ADA_SHARD), F32),
        compiler_params=_cparams(("parallel", "parallel")),
    )(c_all, dmod_cols)


def _lower_bounds(logits, name):
    def body(l_ref, o_ref):
        l0 = l_ref[0:1, :]
        l1 = l_ref[1:2, :]
        mx = jnp.maximum(l0, l1)
        e0 = jnp.exp(l0 - mx)
        e1 = jnp.exp(l1 - mx)
        p0 = e0 / (e0 + e1)
        p1 = e1 / (e0 + e1)
        o_ref[0:1, :] = p0 - p0
        o_ref[1:2, :] = (p0 + p1) - p0

    return _pcall(body, name=name, out_shape=_sds((DEPTH, D_MODEL), F32), compiler_params=_cparams())(logits)


_R_DMOD = 0
_R_NMIX = 96
_R_NFFN = 112
_R_QG = 128
_R_KG = 152
_R_GN = 176
_R_LB = 184
_R_RB = 192
SMALL_ROWS = 200


def _small_totals(gathered, logits8, name):
    ng = len(GROUPS)

    def body(g_ref, l_ref, main_ref, gains_ref, dlb_ref, rb_ref):
        tot = g_ref[0]
        for dev in range(1, N_DEV):
            tot = tot + g_ref[dev]
        main_ref[...] = tot[0:_R_QG]
        gains_ref[...] = jnp.zeros_like(gains_ref)
        for g in range(ng):
            gains_ref[g:g + 1, :] = jnp.sum(tot[_R_QG + 8 * g:_R_QG + 8 * g + 8], axis=0, keepdims=True)
            gains_ref[ng + g:ng + g + 1, :] = jnp.sum(tot[_R_KG + 8 * g:_R_KG + 8 * g + 8], axis=0, keepdims=True)
        gains_ref[2 * ng:2 * ng + 1, :] = jnp.sum(tot[_R_GN:_R_GN + 8], axis=0, keepdims=True)
        rb_ref[...] = tot[_R_RB:_R_RB + 8]
        dlb1 = tot[_R_LB:_R_LB + 8]
        l0 = l_ref[0]
        l1 = l_ref[1]
        mx = jnp.maximum(l0, l1)
        e0 = jnp.exp(l0 - mx)
        e1 = jnp.exp(l1 - mx)
        p0 = e0 / (e0 + e1)
        p1 = e1 / (e0 + e1)
        dlb_ref[0] = -p0 * p1 * dlb1
        dlb_ref[1] = p1 * (1.0 - p1) * dlb1

    return _pcall(
        body, name=name,
        out_shape=[_sds((_R_QG, 128), F32), _sds((8, 128), F32), _sds((DEPTH, 8, 128), F32), _sds((8, 128), F32)],
        compiler_params=_cparams(),
    )(gathered, logits8)


def _row_tile(rows):
    return 128 if rows % 128 == 0 else rows


def _adamw(w, grads, m, v, name):
    nl, r, cdim = w.shape
    tr = _row_tile(r)

    def body(*refs):
        g_refs = refs[:nl]
        w_ref, m_ref, v_ref, go_ref, d_ref, mo_ref, vo_ref = refs[nl:]

        def step(g):
            m2 = ADAM_B1 * m_ref[...] + (1.0 - ADAM_B1) * g
            v2 = ADAM_B2 * v_ref[...] + (1.0 - ADAM_B2) * (g * g)
            m_hat = m2 / (1.0 - ADAM_B1 ** ADAM_STEP)
            v_hat = v2 / (1.0 - ADAM_B2 ** ADAM_STEP)
            go_ref[...] = g
            d_ref[...] = -ADAM_LR * (m_hat / (jnp.sqrt(v_hat) + ADAM_EPS) + ADAM_WD * w_ref[...])
            mo_ref[...] = m2
            vo_ref[...] = v2

        if nl == 1:
            step(g_refs[0][...])
        else:
            for layer in range(nl):
                @pl.when(pl.program_id(0) == layer)
                def _(layer=layer):
                    step(g_refs[layer][...])

    big = pl.BlockSpec((None, tr, cdim), lambda l, i: (l, i, 0))
    g_specs = [pl.BlockSpec((tr, cdim), lambda l, i, layer=layer: (jnp.where(l == layer, i, 0), 0))
               for layer in range(nl)]
    shp = _sds((nl, r, cdim), F32)
    return _pcall(
        body, name=name, grid=(nl, r // tr),
        in_specs=g_specs + [big, big, big],
        out_specs=[big, big, big, big],
        out_shape=[shp, shp, shp, shp],
        compiler_params=_cparams(("parallel", "parallel")),
    )(*grads, w, m, v)


def _cast_bf16(place, w, name):
    nl, r, cdim = w.shape
    tr = _row_tile(r)

    def body(place_ref, w_ref, o_ref):
        o_ref[...] = w_ref[...].astype(BF16)

    return _pcall(
        body, name=name,
        grid_spec=pltpu.PrefetchScalarGridSpec(
            num_scalar_prefetch=1, grid=(nl, r // tr),
            in_specs=[pl.BlockSpec((None, tr, cdim), lambda l, i, place_ref: (l, i, 0))],
            out_specs=pl.BlockSpec((None, None, tr, cdim), lambda l, i, place_ref: (place_ref[1], l, i, 0))),
        out_shape=_sds((N_CHIP, nl, r, cdim), BF16),
        compiler_params=_cparams(("parallel", "parallel")),
    )(place, w)


def _rs_add_cast(place, grads, recvs, name):
    n_a = len(grads)
    _, k, n = grads[0].shape
    kh = k // 2
    tr = _row_tile(kh)
    nb = kh // tr

    def body(place_ref, *refs):
        for a in range(n_a):
            refs[2 * n_a + a][...] = (refs[a][...] + refs[n_a + a][...]).astype(BF16)

    half = pl.BlockSpec((None, tr, n), lambda s, i, place_ref: (s, i, 0))
    mine = pl.BlockSpec((None, tr, n), lambda s, i, place_ref: (s, place_ref[0] * nb + i, 0))
    return _pcall(
        body, name=name,
        grid_spec=pltpu.PrefetchScalarGridSpec(
            num_scalar_prefetch=1, grid=(N_CHIP, nb),
            in_specs=[mine] * n_a + [half] * n_a,
            out_specs=[half] * n_a),
        out_shape=[_sds((N_CHIP, kh, n), BF16)] * n_a,
        compiler_params=_cparams(("parallel", "parallel")),
    )(place, *grads, *recvs)


def _rs_sum4(place, parts, gots, name):
    n_a = len(parts)
    _, kh, n = parts[0].shape
    tr = _row_tile(kh)
    nb = kh // tr

    def body(place_ref, *refs):
        for a in range(n_a):
            acc = refs[a][...].astype(F32)
            for j in range(N_CHIP - 1):
                acc = acc + refs[n_a + a][j].astype(F32)
            refs[2 * n_a + a][...] = acc

    return _pcall(
        body, name=name,
        grid_spec=pltpu.PrefetchScalarGridSpec(
            num_scalar_prefetch=1, grid=(nb,),
            in_specs=[pl.BlockSpec((None, tr, n), lambda i, place_ref: (place_ref[1], i, 0))] * n_a
            + [pl.BlockSpec((N_CHIP - 1, tr, n), lambda i, place_ref: (0, i, 0))] * n_a,
            out_specs=[pl.BlockSpec((tr, n), lambda i, place_ref: (place_ref[0] * nb + i, 0))] * n_a),
        out_shape=[_sds((2 * kh, n), F32)] * n_a,
        compiler_params=_cparams(("parallel",)),
    )(place, *parts, *gots)


_ANY = pl.BlockSpec(memory_space=pl.ANY)


def _position():
    return lax.axis_index("x"), lax.axis_index("y"), lax.axis_index("c")


def _other_chips(x, y):
    return [(1 - x, y), (x, 1 - y), (1 - x, 1 - y)]


def _remote(src, dst, send_sem, recv_sem, to):
    return pltpu.make_async_remote_copy(src_ref=src, dst_ref=dst, send_sem=send_sem, recv_sem=recv_sem,
                                        device_id=to, device_id_type=MESH)


def _small_allgather(v, name):
    r = v.shape[0]

    def body(x_ref, out_ref, send_sems, recv_sems, local_sem):
        x, y, c = _position()
        me, sibling = (x, y, c), (x, y, 1 - c)
        chips = _other_chips(x, y)

        def slab(px, py, pc):
            return out_ref.at[4 * px + 2 * py + pc]

        def copy(k, block, to, src=None):
            return _remote(slab(*block) if src is None else src, slab(*block), send_sems.at[k], recv_sems.at[k], to)

        mine = pltpu.make_async_copy(x_ref, slab(*me), local_sem)
        mine.start()
        first = [copy(0, me, sibling, src=x_ref)]
        first += [copy(1 + j, me, (*chip, c), src=x_ref) for j, chip in enumerate(chips)]
        for cp in first:
            cp.start()
        passed = [copy(4 + j, (*chip, c), sibling) for j, chip in enumerate(chips)]
        for j, chip in enumerate(chips):
            copy(1 + j, (*chip, c), me).wait_recv()
            passed[j].start()
        copy(0, sibling, me).wait_recv()
        for j, chip in enumerate(chips):
            copy(4 + j, (*chip, 1 - c), me).wait_recv()
        for cp in first + passed:
            cp.wait_send()
        mine.wait()

    return _pcall(
        body, name=name,
        out_shape=_sds((N_DEV, r, 128), F32),
        in_specs=[pl.BlockSpec(memory_space=pltpu.VMEM)],
        out_specs=pl.BlockSpec(memory_space=pltpu.VMEM),
        scratch_shapes=[pltpu.SemaphoreType.DMA((7,)), pltpu.SemaphoreType.DMA((7,)), pltpu.SemaphoreType.DMA],
        compiler_params=_cparams(),
    )(v)


def _half_rows(core, kh):
    return pl.ds(pl.multiple_of(core * kh, 8), kh)


def _slab_half(ref, chip, core):
    return ref.at[chip, :, _half_rows(core, ref.shape[2] // 2), :]


def _gather_ici(out, send_sems, recv_sems):
    def copies():
        x, y, c = _position()
        for a in range(len(out)):
            for j, (px, py) in enumerate(_other_chips(x, y)):
                mine = _slab_half(out[a], 2 * x + y, c)
                landed = _slab_half(out[a], 2 * px + py, c)
                yield (_remote(mine, mine, send_sems.at[a, j], recv_sems.at[a, j], (px, py, c)),
                       _remote(landed, landed, send_sems.at[a, j], recv_sems.at[a, j], (px, py, c)))

    def start():
        for send, _ in copies():
            send.start()

    def wait():
        for send, recv in copies():
            recv.wait_recv()
            send.wait_send()

    return start, wait


def _gather_d2d(out, send_sems, recv_sems):
    def copies():
        x, y, c = _position()
        for a in range(len(out)):
            for j, (px, py) in enumerate(_other_chips(x, y)):
                landed = _slab_half(out[a], 2 * px + py, c)
                other = _slab_half(out[a], 2 * px + py, 1 - c)
                yield (_remote(landed, landed, send_sems.at[a, j], recv_sems.at[a, j], (x, y, 1 - c)),
                       _remote(other, other, send_sems.at[a, j], recv_sems.at[a, j], (x, y, 1 - c)))

    def start():
        for send, _ in copies():
            send.start()

    def wait():
        for send, recv in copies():
            recv.wait_recv()
            send.wait_send()

    return start, wait


def _gather_weights(slabs, name, ici=True):
    n = len(slabs)

    def body(*refs):
        out = refs[n:2 * n]
        sems = refs[2 * n:]
        if ici:
            start, wait = _gather_ici(out, sems[2], sems[3])
            start()
            wait()
        start, wait = _gather_d2d(out, sems[0], sems[1])
        start()
        wait()

    sem = pltpu.SemaphoreType.DMA((n, 3))
    return _pcall(
        body, name=name,
        out_shape=[_sds(s.shape, BF16) for s in slabs],
        in_specs=[_ANY] * n, out_specs=[_ANY] * n,
        input_output_aliases={a: a for a in range(n)},
        scratch_shapes=[sem, sem] + ([sem, sem] if ici else []),
        compiler_params=_cparams(),
    )(*slabs)


def _rs_halves(grads, out, send_sems, recv_sems):
    def copies():
        x, y, c = _position()
        for a in range(len(grads)):
            kh = grads[a].shape[1] // 2
            yield _remote(grads[a].at[:, _half_rows(1 - c, kh), :], out[a], send_sems.at[a], recv_sems.at[a],
                          (x, y, 1 - c))

    def start():
        for cp in copies():
            cp.start()

    def wait():
        for cp in copies():
            cp.wait()

    return start, wait


def _rs_halves_shapes(grads):
    return [_sds((N_CHIP, g.shape[1] // 2, g.shape[2]), F32) for g in grads]


def _rs_exchange_halves(grads, name):
    n = len(grads)

    def body(*refs):
        start, wait = _rs_halves(refs[:n], refs[n:2 * n], *refs[2 * n:])
        start()
        wait()

    return _pcall(
        body, name=name,
        out_shape=_rs_halves_shapes(grads),
        in_specs=[_ANY] * n, out_specs=[_ANY] * n,
        scratch_shapes=[pltpu.SemaphoreType.DMA((n,)), pltpu.SemaphoreType.DMA((n,))],
        compiler_params=_cparams(),
    )(*grads)


def _rs_chips(parts, out, send_sems, recv_sems):
    def copies():
        x, y, c = _position()
        for a in range(len(parts)):
            for j, (px, py) in enumerate(_other_chips(x, y)):
                got = out[a].at[j]
                yield (_remote(parts[a].at[2 * px + py], got, send_sems.at[a, j], recv_sems.at[a, j], (px, py, c)),
                       _remote(got, got, send_sems.at[a, j], recv_sems.at[a, j], (px, py, c)))

    def start():
        for send, _ in copies():
            send.start()

    def wait():
        for send, recv in copies():
            recv.wait_recv()
            send.wait_send()

    return start, wait


def _rs_chips_shapes(parts):
    return [_sds((N_CHIP - 1,) + p.shape[1:], BF16) for p in parts]


def _rs_join(out, send_sems, recv_sems):
    def copies():
        x, y, c = _position()
        for a in range(len(out)):
            kh = out[a].shape[0] // 2
            mine = out[a].at[_half_rows(c, kh), :]
            theirs = out[a].at[_half_rows(1 - c, kh), :]
            yield (_remote(mine, mine, send_sems.at[a], recv_sems.at[a], (x, y, 1 - c)),
                   _remote(theirs, theirs, send_sems.at[a], recv_sems.at[a], (x, y, 1 - c)))

    def start():
        for send, _ in copies():
            send.start()

    def wait():
        for send, recv in copies():
            recv.wait_recv()
            send.wait_send()

    return start, wait


def _rs_join_halves(fulls, name):
    n = len(fulls)

    def body(*refs):
        start, wait = _rs_join(refs[n:2 * n], *refs[2 * n:])
        start()
        wait()

    return _pcall(
        body, name=name,
        out_shape=[_sds(f.shape, F32) for f in fulls],
        in_specs=[_ANY] * n, out_specs=[_ANY] * n,
        input_output_aliases={a: a for a in range(n)},
        scratch_shapes=[pltpu.SemaphoreType.DMA((n,)), pltpu.SemaphoreType.DMA((n,))],
        compiler_params=_cparams(),
    )(*fulls)


_SMALL_ORDER = ("rel_bias", "ada_b", "norm_mix", "norm_ffn", "attn_q_gain", "attn_k_gain", "hgrn_gnorm",
                "hgrn_lower_bounds")
_WEIGHT_ORDER = ("rel_bias", "ada_w", "ada_b", "norm_mix", "norm_ffn", "attn_w_qkv", "attn_w_out", "attn_q_gain",
                 "attn_k_gain", "hgrn_w_in", "hgrn_w_out", "hgrn_gnorm", "hgrn_lower_bounds", "ffn_w1", "ffn_w3",
                 "ffn_w2")


def _qkv_group_map(t):
    return t // 4, t % 4


def _qkv_chip_map(t):
    return t // 9, t % 9


def _block_map(t):
    return t, 0


def _pack_rows(parts):
    return jnp.concatenate([p.reshape(-1, 128) for p in parts], axis=0)


def kernel(x, c, rel_bias, ada_w, ada_b, norm_mix, norm_ffn, attn_w_qkv, attn_w_out, attn_q_gain, attn_k_gain, hgrn_w_in, hgrn_w_out, hgrn_gnorm, hgrn_lower_bounds, ffn_w1, ffn_w3, ffn_w2, loss_target, m_rel_bias, m_ada_w, m_ada_b, m_norm_mix, m_norm_ffn, m_attn_w_qkv, m_attn_w_out, m_attn_q_gain, m_attn_k_gain, m_hgrn_w_in, m_hgrn_w_out, m_hgrn_gnorm, m_hgrn_lower_bounds, m_ffn_w1, m_ffn_w3, m_ffn_w2, v_rel_bias, v_ada_w, v_ada_b, v_norm_mix, v_norm_ffn, v_attn_w_qkv, v_attn_w_out, v_attn_q_gain, v_attn_k_gain, v_hgrn_w_in, v_hgrn_w_out, v_hgrn_gnorm, v_hgrn_lower_bounds, v_ffn_w1, v_ffn_w3, v_ffn_w2):
    weights = dict(rel_bias=rel_bias, ada_w=ada_w, ada_b=ada_b, norm_mix=norm_mix, norm_ffn=norm_ffn,
                   attn_w_qkv=attn_w_qkv, attn_w_out=attn_w_out, attn_q_gain=attn_q_gain, attn_k_gain=attn_k_gain,
                   hgrn_w_in=hgrn_w_in, hgrn_w_out=hgrn_w_out, hgrn_gnorm=hgrn_gnorm,
                   hgrn_lower_bounds=hgrn_lower_bounds, ffn_w1=ffn_w1, ffn_w3=ffn_w3, ffn_w2=ffn_w2)
    mom1 = dict(rel_bias=m_rel_bias, ada_w=m_ada_w, ada_b=m_ada_b, norm_mix=m_norm_mix, norm_ffn=m_norm_ffn,
                attn_w_qkv=m_attn_w_qkv, attn_w_out=m_attn_w_out, attn_q_gain=m_attn_q_gain,
                attn_k_gain=m_attn_k_gain, hgrn_w_in=m_hgrn_w_in, hgrn_w_out=m_hgrn_w_out, hgrn_gnorm=m_hgrn_gnorm,
                hgrn_lower_bounds=m_hgrn_lower_bounds, ffn_w1=m_ffn_w1, ffn_w3=m_ffn_w3, ffn_w2=m_ffn_w2)
    mom2 = dict(rel_bias=v_rel_bias, ada_w=v_ada_w, ada_b=v_ada_b, norm_mix=v_norm_mix, norm_ffn=v_norm_ffn,
                attn_w_qkv=v_attn_w_qkv, attn_w_out=v_attn_w_out, attn_q_gain=v_attn_q_gain,
                attn_k_gain=v_attn_k_gain, hgrn_w_in=v_hgrn_w_in, hgrn_w_out=v_hgrn_w_out, hgrn_gnorm=v_hgrn_gnorm,
                hgrn_lower_bounds=v_hgrn_lower_bounds, ffn_w1=v_ffn_w1, ffn_w3=v_ffn_w3, ffn_w2=v_ffn_w2)

    transposed = ("ffn_w1", "ffn_w3")
    for group in (weights, mom1, mom2):
        for k in transposed:
            group[k] = jnp.transpose(group[k], (0, 2, 1))

    xi, yi, ci = _position()
    chip = 2 * xi + yi
    dev = 4 * xi + 2 * yi + ci
    place = jnp.stack([ci, chip]).astype(jnp.int32)
    d = D_MODEL

    big_names = ("attn_w_qkv", "attn_w_out", "hgrn_w_in", "hgrn_w_out", "ffn_w1", "ffn_w3", "ffn_w2")
    early_names, late_names = big_names[:1], big_names[1:]
    slabs16 = {k: _cast_bf16(place, weights[k], "cast_" + k) for k in big_names}
    wg = dict(zip(early_names, _gather_weights([slabs16[k] for k in early_names], "gather_early")))

    c_all = _small_allgather(c.reshape(8, 128), "gather_c").reshape(N_DEV, d)
    ada_b_cols = lax.dynamic_slice(ada_b, (0, chip * ADA_SHARD), (DEPTH, ADA_SHARD)).reshape(DEPTH, 1, ADA_SHARD)
    mod_shard = _ada_fwd(c_all, ada_w, ada_b_cols, "ada_fwd")
    mod_all = _small_allgather(mod_shard.reshape(-1, 128), "gather_mod").reshape(N_DEV, DEPTH, N_DEV, ADA_SHARD)
    mod_mine = lax.dynamic_index_in_dim(mod_all[0::2], dev, axis=2, keepdims=False)
    mod = jnp.transpose(mod_mine, (1, 0, 2)).reshape(DEPTH, 6 * d)

    def mods(layer):
        return [mod[layer:layer + 1, j * d:(j + 1) * d] for j in range(6)]

    x0 = x.reshape(SEQ, d)
    target = loss_target.reshape(SEQ, d)
    qg = attn_q_gain.reshape(len(GROUPS), 1, HEAD_DIM)
    kg = attn_k_gain.reshape(len(GROUPS), 1, HEAD_DIM)
    bias = _attn_bias(rel_bias, "attn_bias")
    lb1 = _lower_bounds(hgrn_lower_bounds, "lower_bounds")[1:2]

    def ffn_fwd(layer, x_in, sc2, sh2, g2):
        a1, a3, u, hf = _ffn_up(x_in, norm_ffn[layer:layer + 1], sc2, sh2, wg["ffn_w1"], wg["ffn_w3"], layer,
                                f"l{layer}_ffn_up")
        z, x_out = _mm_rows(u, wg["ffn_w2"], layer, x_in, g2, f"l{layer}_ffn_down")
        return x_out, (hf, a1, a3, u, z)

    def ffn_bwd(layer, dz, dg2, dx_out, x_in, sc2, sh2, saved, mixer_branch, halves=()):
        hf, a1, a3, u, _ = saved
        da1, da3, *recv = _ffn_down_bwd(dz, wg["ffn_w2"], layer, a1, a3, f"l{layer}_ffn_down_bwd", halves=halves)
        dw2 = _mm_rows_bwd_w(u, dz, f"l{layer}_dw2")
        dh = _ffn_up_bwd(da1, da3, wg["ffn_w1"], wg["ffn_w3"], layer, f"l{layer}_ffn_up_bwd")
        dw1, dw3 = _mm_rows_bwd_w_multi([da1, da3], hf, f"l{layer}_dw13")
        dx_in, dsc2, dsh2, dnf, dz_mix, dg_mix = _norm_mod_bwd(x_in, norm_ffn[layer:layer + 1], sc2, sh2, dh, dx_out,
                                                               f"l{layer}_norm_ffn_bwd", branch=mixer_branch)
        return dx_in, (dw1, dw3, dw2), (dsh2, dsc2, dg2), dnf, recv, dz_mix, dg_mix

    def rs_batched(fn, prefix, tags, *columns):
        out = [None] * len(tags)
        by_shape = {}
        for idx, arr in enumerate(columns[0]):
            by_shape.setdefault(arr.shape, []).append(idx)
        for idxs in by_shape.values():
            for lo in range(0, len(idxs), 3):
                sel = idxs[lo:lo + 3]
                k, layer = tags[sel[0]]
                res = fn(place, *[[col[i] for i in sel] for col in columns], f"{prefix}_{k}_{layer}_x{len(sel)}")
                for i, r in zip(sel, res):
                    out[i] = r
        return out

    def rs_add(tags, grads_in, recv):
        return rs_batched(_rs_add_cast, "rs_add", tags, grads_in, list(recv))

    sh1_0, sc1_0, g1_0, sh2_0, sc2_0, g2_0 = mods(0)
    w_qkv9 = _retile_cols(wg["attn_w_qkv"].reshape(N_CHIP, d, 2304), n_out=9, width_out=d, tn=256,
                          src_map=_qkv_chip_map, dst_map=_qkv_group_map, n_tiles=36,
                          name="regroup_w_qkv").reshape(9, 1, d, d)
    qkv9, h0 = _mm_cols(x0, norm_mix[0:1], sc1_0, sh1_0, w_qkv9, 0, n_blocks=9, width=d, tn=d,
                        act_map=_block_map, w_map=_block_map, out_dtype=F32, name="l0_qkv")
    o4, lse, *late = _attn_fwd(qkv9, qg, kg, bias, "l0_attn", gather=[slabs16[k] for k in late_names])
    wg.update(zip(late_names, _gather_weights(late, "gather_late_siblings", ici=False)))
    y0, x1 = _mm_rows(o4, wg["attn_w_out"], 0, x0, g1_0, "l0_attn_out")
    x2, ffn0 = ffn_fwd(0, x1, sc2_0, sh2_0, g2_0)

    sh1_1, sc1_1, g1_1, sh2_1, sc2_1, g2_1 = mods(1)
    proj4, h1 = _mm_cols(x2, norm_mix[1:2], sc1_1, sh1_1, wg["hgrn_w_in"], 0, n_blocks=4, width=d, tn=d,
                         act_map=_block_map, w_map=_block_map, out_dtype=F32, name="l1_hgrn_in")
    o_raw, yg4, states = _hgrn_fwd(proj4, lb1, hgrn_gnorm, "l1_hgrn")
    y1, x3 = _mm_rows(yg4, wg["hgrn_w_out"], 0, x2, g1_1, "l1_hgrn_out")
    x4, ffn1 = ffn_fwd(1, x3, sc2_1, sh2_1, g2_1)

    dx4, loss_part, dz_ffn1, dg2_1 = _loss_head(x4, target, ffn1[4], g2_1, "loss_head")
    loss = lax.psum(loss_part[0, 0], ("x", "y", "c"))

    dx3, (dw1_1, dw3_1, dw2_1), dmod2_1, dnf_1, _, dzm1, dg1_1 = ffn_bwd(
        1, dz_ffn1, dg2_1, dx4, x3, sc2_1, sh2_1, ffn1, (y1, g1_1))
    dyg4 = _mm_rows_bwd_a(dzm1, wg["hgrn_w_out"], 0, "l1_hgrn_out_bwd")
    dw_hout = _mm_rows_bwd_w(yg4, dzm1, "l1_dw_hgrn_out")
    dproj4, dlb_h, dgn_h = _hgrn_bwd(proj4, lb1, hgrn_gnorm, o_raw, dyg4, states, "l1_hgrn_bwd")
    dh1 = _mm_cols_bwd_a(dproj4, wg["hgrn_w_in"], 0, group=N_CHIP, name="l1_hgrn_in_bwd", tm=512)
    dw_hin = _mm_cols_bwd_w(h1, dproj4, ns=d, tn=d, act_map=_block_map, w_map=_block_map, n_tiles=N_CHIP,
                            name="l1_dw_hgrn_in", tm=2048)
    dx2, dsc1_1, dsh1_1, dnm_1, dz_ffn0, dg2_0 = _norm_mod_bwd(x2, norm_mix[1:2], sc1_1, sh1_1, dh1, dx3,
                                                               "l1_norm_mix_bwd", branch=(ffn0[4], g2_0))

    tags_1 = [("hgrn_w_in", 0), ("hgrn_w_out", 0), ("ffn_w1", 1), ("ffn_w3", 1), ("ffn_w2", 1)]
    grads_1 = [dw_hin, dw_hout, dw1_1, dw3_1, dw2_1]
    dx1, (dw1_0, dw3_0, dw2_0), dmod2_0, dnf_0, recv_1, dzm0, dg1_0 = ffn_bwd(
        0, dz_ffn0, dg2_0, dx2, x1, sc2_0, sh2_0, ffn0, (y0, g1_0), halves=grads_1)
    tags_0 = [("ffn_w1", 0), ("ffn_w3", 0), ("ffn_w2", 0)]
    grads_0 = [dw1_0, dw3_0, dw2_0]
    do4, *recv_0 = _mm_rows_bwd_a(dzm0, wg["attn_w_out"], 0, "l0_attn_out_bwd", halves=grads_0)
    dw_aout = _mm_rows_bwd_w(o4, dzm0, "l0_dw_attn_out")
    tags_a = tags_1 + tags_0
    parts_a = rs_add(tags_1, grads_1, recv_1) + rs_add(tags_0, grads_0, recv_0)
    dqkv, dqg_h, dkg_h, dbias, *got_a = _attn_bwd(qkv9, qg, kg, bias, do4, o4, lse, "l0_attn_bwd", scatter=parts_a)
    dqkv9 = dqkv.reshape(9, SEQ, d)
    dw_qkv9 = _mm_cols_bwd_w(h0, dqkv9, ns=d, tn=d, act_map=_block_map, w_map=_block_map, n_tiles=9,
                             name="l0_dw_qkv", tm=2048, n_out=9)
    dw_qkv = _retile_cols(dw_qkv9, n_out=N_CHIP, width_out=2304, tn=256, src_map=_qkv_group_map,
                          dst_map=_qkv_chip_map, n_tiles=36, name="regroup_dw_qkv")
    tags_b = [("attn_w_qkv", 0), ("attn_w_out", 0)]
    grads_b = [dw_qkv, dw_aout]
    parts_b = rs_add(tags_b, grads_b, _rs_exchange_halves(grads_b, "rs_exchange_halves_b"))
    dh0, *got_b = _mm_cols_bwd_a(dqkv9, w_qkv9, 0, group=3, name="l0_qkv_bwd", scatter=parts_b)
    dx0, dsc1_0, dsh1_0, dnm_0 = _norm_mod_bwd(x0, norm_mix[0:1], sc1_0, sh1_0, dh0, dx1, "l0_norm_mix_bwd")
    drb8 = _relbias_bwd(dbias, jnp.asarray(_bias_tables()), "rel_bias_bwd")

    small = _pack_rows([
        dsh1_0, dsc1_0, dg1_0, *dmod2_0, dsh1_1, dsc1_1, dg1_1, *dmod2_1,
        dnm_0, dnm_1, dnf_0, dnf_1,
        jnp.transpose(dqg_h, (1, 0, 2, 3)), jnp.transpose(dkg_h, (1, 0, 2, 3)), dgn_h, dlb_h, drb8])
    small_all = _small_allgather(small, "gather_small")
    main, gains, dlbnd, rbt = _small_totals(small_all, hgrn_lower_bounds.reshape(DEPTH, 8, 128), "small_totals")
    ng = len(GROUPS)
    grads = {
        "ada_b": main[_R_DMOD:_R_NMIX].reshape(DEPTH, 6 * d),
        "norm_mix": main[_R_NMIX:_R_NFFN].reshape(DEPTH, d),
        "norm_ffn": main[_R_NFFN:_R_QG].reshape(DEPTH, d),
        "attn_q_gain": gains[0:ng].reshape(1, ng, HEAD_DIM),
        "attn_k_gain": gains[ng:2 * ng].reshape(1, ng, HEAD_DIM),
        "hgrn_gnorm": gains[2 * ng:2 * ng + 1],
        "hgrn_lower_bounds": dlbnd.reshape(DEPTH, d),
        "rel_bias": jnp.transpose(rbt[:, :ng * NUM_BUCKETS].reshape(HEADS, ng, NUM_BUCKETS), (2, 1, 0))
                       .reshape(NUM_BUCKETS, ng * HEADS),
    }
    dmod_all = small_all[:, _R_DMOD:_R_NMIX].reshape(N_DEV, DEPTH, 6 * d)
    dmod_cols = jnp.transpose(lax.dynamic_slice(dmod_all, (0, 0, chip * ADA_SHARD), (N_DEV, DEPTH, ADA_SHARD)),
                              (1, 0, 2))
    grad_ada_w = _ada_bwd(c_all, dmod_cols, "ada_bwd")

    tags = tags_a + tags_b
    halves = rs_batched(_rs_sum4, "rs_sum", tags, parts_a + parts_b, list(got_a) + list(got_b))
    full = dict(zip(tags, _rs_join_halves(halves, "rs_join_halves")))

    out_g, out_d, out_m, out_v = {}, {}, {}, {}
    for k in big_names:
        gs = [full[(k, layer)] for layer in range(weights[k].shape[0])]
        out_g[k], out_d[k], out_m[k], out_v[k] = _adamw(weights[k], gs, mom1[k], mom2[k], "adamw_" + k)
    shp = (1, DEPTH * d, ADA_SHARD)
    res = _adamw(ada_w.reshape(shp), [grad_ada_w.reshape(shp[1:])], m_ada_w.reshape(shp), v_ada_w.reshape(shp),
                 "adamw_ada_w")
    out_g["ada_w"], out_d["ada_w"], out_m["ada_w"], out_v["ada_w"] = [r.reshape(ada_w.shape) for r in res]
    for k in _SMALL_ORDER:
        shp = (1, weights[k].size // weights[k].shape[-1], weights[k].shape[-1])
        res = _adamw(weights[k].reshape(shp), [grads[k].reshape(shp[1:])], mom1[k].reshape(shp),
                     mom2[k].reshape(shp), "adamw_" + k)
        out_g[k], out_d[k], out_m[k], out_v[k] = [r.reshape(weights[k].shape) for r in res]
    for dst in (out_g, out_d, out_m, out_v):
        for k in transposed:
            dst[k] = jnp.transpose(dst[k], (0, 2, 1))

    return (loss, dx0.reshape(x.shape), *[out_g[k] for k in _WEIGHT_ORDER], *[out_d[k] for k in _WEIGHT_ORDER],
            *[out_m[k] for k in _WEIGHT_ORDER], *[out_v[k] for k in _WEIGHT_ORDER])
```

```python
import functools

import numpy as np
import jax
import jax.numpy as jnp
from jax import lax
from jax.experimental import pallas as pl
from jax.experimental.pallas import tpu as pltpu

F32 = jnp.float32
BF16 = jnp.bfloat16

D_MODEL = 1024
SEQ = 4096
N_DEV = 8
N_CHIP = 4
DEPTH = 2
HEADS = 8
HEAD_DIM = 128
GROUPS = ((128, 1), (512, 4), (2048, 16))
ATT_BLK = 128
ATT_WAYS = 4
ATT_STEPS = SEQ // ATT_BLK // ATT_WAYS
NUM_BUCKETS = 32
MAX_DISTANCE = 2048
FFN_HIDDEN = 2816
FFN_SHARD = FFN_HIDDEN // N_CHIP
HG_SUB = 16
HG_TC = 512
HG_HP = 4
RMS_EPS = 1e-6
NEG = -1e30
ATT_SCALE = HEAD_DIM ** -0.5
LOG2_E = 1.4426950408889634
ADAM_LR, ADAM_B1, ADAM_B2, ADAM_EPS, ADAM_WD, ADAM_STEP = 0.001, 0.9, 0.999, 1e-08, 0.01, 10
VMEM_LIMIT = 56 * 1024 * 1024
MESH = pl.DeviceIdType.MESH


def _pcall(body, **kw):
    return pl.pallas_call(body, **kw)


def _cparams(sem=None):
    if sem is None:
        return pltpu.CompilerParams(vmem_limit_bytes=VMEM_LIMIT)
    return pltpu.CompilerParams(dimension_semantics=sem, vmem_limit_bytes=VMEM_LIMIT)


def _sds(shape, dtype):
    return jax.ShapeDtypeStruct(shape, dtype)


def _dot(a, b):
    return jnp.dot(a, b, preferred_element_type=F32)


def _dot_nt(a, b):
    return lax.dot_general(a, b, (((1,), (1,)), ((), ())), preferred_element_type=F32)


def _dot_tn(a, b):
    return lax.dot_general(a, b, (((0,), (0,)), ((), ())), preferred_element_type=F32)


def _sigmoid(x):
    return 1.0 / (1.0 + jnp.exp(-x))


def _silu(x):
    return x * _sigmoid(x)


def _dsilu(x):
    s = _sigmoid(x)
    return s * (1.0 + x * (1.0 - s))


def _norm_mod(x_ref, g_ref, sc_ref, sh_ref):
    xv = x_ref[...]
    rs = lax.rsqrt(jnp.mean(xv * xv, axis=-1, keepdims=True) + RMS_EPS)
    return ((xv * rs * g_ref[...]) * (1.0 + sc_ref[...]) + sh_ref[...]).astype(BF16)


_NORM_SPECS = [pl.BlockSpec((1, D_MODEL), lambda i, t: (0, 0))] * 3


def _gated_branch_bwd(dx, z_ref, gate_ref, dz_ref, dgate_ref):
    dz_ref[...] = (dx * gate_ref[...]).astype(BF16)
    dgate_ref[...] += jnp.sum(dx * z_ref[...], axis=0, keepdims=True)


def _norm_mod_bwd(x, gain, sc, sh, dh, dres, name, branch=None):
    tm = 512
    n_b = 2 if branch else 0

    def body(*refs):
        x_ref, g_ref, sc_ref, sh_ref, dh_ref, dres_ref = refs[:6]
        dx_ref, dsc_ref, dsh_ref, dg_ref = refs[6 + n_b:10 + n_b]

        @pl.when(pl.program_id(0) == 0)
        def _():
            dsc_ref[...] = jnp.zeros_like(dsc_ref)
            dsh_ref[...] = jnp.zeros_like(dsh_ref)
            dg_ref[...] = jnp.zeros_like(dg_ref)
            if branch:
                refs[11 + n_b][...] = jnp.zeros_like(refs[11 + n_b])

        xv = x_ref[...]
        dhv = dh_ref[...]
        rs = lax.rsqrt(jnp.mean(xv * xv, axis=-1, keepdims=True) + RMS_EPS)
        xh = xv * rs
        dsc_ref[...] += jnp.sum(dhv * (xh * g_ref[...]), axis=0, keepdims=True)
        dsh_ref[...] += jnp.sum(dhv, axis=0, keepdims=True)
        dhn = dhv * (1.0 + sc_ref[...])
        dg_ref[...] += jnp.sum(dhn * xh, axis=0, keepdims=True)
        dxh = dhn * g_ref[...]
        dx = dres_ref[...] + rs * (dxh - xh * jnp.mean(dxh * xh, axis=-1, keepdims=True))
        dx_ref[...] = dx
        if branch:
            _gated_branch_bwd(dx, refs[6], refs[7], refs[10 + n_b], refs[11 + n_b])

    vec = pl.BlockSpec((1, D_MODEL), lambda i: (0, 0))
    big = pl.BlockSpec((tm, D_MODEL), lambda i: (i, 0))
    return _pcall(
        body, name=name, grid=(SEQ // tm,),
        in_specs=[big, vec, vec, vec, big, big] + ([big, vec] if branch else []),
        out_specs=[big, vec, vec, vec] + ([big, vec] if branch else []),
        out_shape=[_sds((SEQ, D_MODEL), F32)] + [_sds((1, D_MODEL), F32)] * 3
        + ([_sds((SEQ, D_MODEL), BF16), _sds((1, D_MODEL), F32)] if branch else []),
        compiler_params=_cparams(("arbitrary",)),
    )(x, gain, sc, sh, dh, dres, *(branch or ()))


def _mm_cols(x, gain, sc, sh, wg, layer, *, n_blocks, width, tn, act_map, w_map, out_dtype, name, tm=1024):
    k = x.shape[1]
    n_tiles = n_blocks * width // tn

    def body(x_ref, g_ref, sc_ref, sh_ref, w_ref, o_ref, h_ref):
        @pl.when(pl.program_id(1) == 0)
        def _():
            h_ref[...] = _norm_mod(x_ref, g_ref, sc_ref, sh_ref)

        o_ref[...] = _dot(h_ref[...], w_ref[...]).astype(o_ref.dtype)

    rows = pl.BlockSpec((tm, k), lambda i, t: (i, 0))
    return _pcall(
        body, name=name, grid=(SEQ // tm, n_tiles),
        in_specs=[rows] + _NORM_SPECS
        + [pl.BlockSpec((None, None, k, tn), lambda i, t: (w_map(t)[0], layer, 0, w_map(t)[1]))],
        out_specs=[pl.BlockSpec((None, tm, tn), lambda i, t: (act_map(t)[0], i, act_map(t)[1])), rows],
        out_shape=[_sds((n_blocks, SEQ, width), out_dtype), _sds((SEQ, k), BF16)],
        compiler_params=_cparams(("parallel", "arbitrary")),
    )(x, gain, sc, sh, wg)


def _mm_cols_bwd_a(dout, wg, layer, *, group, name, tm=1024, scatter=()):
    n_blocks, _, width = dout.shape
    k = wg.shape[2]
    n_s = len(scatter)
    n_rows = SEQ // tm
    n_steps = n_blocks // group

    def body(*refs):
        d_ref, w_ref = refs[:2]
        o_ref = refs[2 + n_s]
        if n_s:
            comm_start, comm_wait = _rs_chips(refs[2:2 + n_s], refs[3 + n_s:3 + 2 * n_s], *refs[3 + 2 * n_s:])
            pl.when((pl.program_id(0) == 0) & (pl.program_id(1) == 0))(comm_start)
        acc = _dot_nt(d_ref[0], w_ref[0])
        for b in range(1, group):
            acc += _dot_nt(d_ref[b], w_ref[b])
        if n_steps == 1:
            o_ref[...] = acc
        else:
            @pl.when(pl.program_id(1) == 0)
            def _():
                o_ref[...] = acc

            @pl.when(pl.program_id(1) > 0)
            def _():
                o_ref[...] += acc
        if n_s:
            pl.when((pl.program_id(0) == n_rows - 1) & (pl.program_id(1) == n_steps - 1))(comm_wait)

    sem = pltpu.SemaphoreType.DMA((max(n_s, 1), 3))
    res = _pcall(
        body, name=name, grid=(n_rows, n_steps),
        in_specs=[pl.BlockSpec((group, tm, width), lambda i, t: (t, i, 0)),
                  pl.BlockSpec((group, None, k, width), lambda i, t: (t, layer, 0, 0))] + [_ANY] * n_s,
        out_specs=[pl.BlockSpec((tm, k), lambda i, t: (i, 0))] + [_ANY] * n_s,
        out_shape=[_sds((SEQ, k), F32)] + _rs_chips_shapes(scatter),
        scratch_shapes=[sem, sem] if n_s else [],
        compiler_params=_cparams(("arbitrary", "arbitrary") if n_s else ("parallel", "arbitrary")),
    )(dout, wg, *scatter)
    return res if n_s else res[0]


def _mm_cols_bwd_w(a, dout, *, ns, tn, act_map, w_map, n_tiles, name, tm=1024, n_out=N_CHIP):
    k = a.shape[1]

    def body(a_ref, d_ref, o_ref):
        @pl.when(pl.program_id(1) == 0)
        def _():
            o_ref[...] = jnp.zeros_like(o_ref)

        o_ref[...] += _dot_tn(a_ref[...], d_ref[...])

    return _pcall(
        body, name=name, grid=(n_tiles, SEQ // tm),
        in_specs=[pl.BlockSpec((tm, k), lambda t, i: (i, 0)),
                  pl.BlockSpec((None, tm, tn), lambda t, i: (act_map(t)[0], i, act_map(t)[1]))],
        out_specs=pl.BlockSpec((None, k, tn), lambda t, i: (w_map(t)[0], 0, w_map(t)[1])),
        out_shape=_sds((n_out, k, ns), F32),
        compiler_params=_cparams(("parallel", "arbitrary")),
    )(a, dout)


def _retile_cols(src, *, n_out, width_out, tn, src_map, dst_map, n_tiles, name):
    k = src.shape[1]

    def body(s_ref, o_ref):
        o_ref[...] = s_ref[...]

    return _pcall(
        body, name=name, grid=(n_tiles,),
        in_specs=[pl.BlockSpec((None, k, tn), lambda t: (src_map(t)[0], 0, src_map(t)[1]))],
        out_specs=pl.BlockSpec((None, k, tn), lambda t: (dst_map(t)[0], 0, dst_map(t)[1])),
        out_shape=_sds((n_out, k, width_out), src.dtype),
        compiler_params=_cparams(("parallel",)),
    )(src)


def _mm_rows(a4, wg, layer, x, gate, name, tm=512):
    ks = a4.shape[2]
    n = wg.shape[3]

    def body(a_ref, w_ref, x_ref, g_ref, z_ref, xn_ref):
        z = _dot(a_ref[0], w_ref[0])
        for s in range(1, N_CHIP):
            z += _dot(a_ref[s], w_ref[s])
        z_ref[...] = z.astype(BF16)
        xn_ref[...] = x_ref[...] + g_ref[...] * z

    big = pl.BlockSpec((tm, n), lambda i: (i, 0))
    return _pcall(
        body, name=name, grid=(SEQ // tm,),
        in_specs=[pl.BlockSpec((N_CHIP, tm, ks), lambda i: (0, i, 0)),
                  pl.BlockSpec((N_CHIP, None, ks, n), lambda i: (0, layer, 0, 0)),
                  big, pl.BlockSpec((1, n), lambda i: (0, 0))],
        out_specs=[big, big],
        out_shape=[_sds((SEQ, n), BF16), _sds((SEQ, n), F32)],
        compiler_params=_cparams(("parallel",)),
    )(a4, wg, x, gate)


def _mm_rows_bwd_a(dz, wg, layer, name, tm=1024, halves=()):
    ks, n = wg.shape[2], wg.shape[3]
    n_h = len(halves)
    n_rows = SEQ // tm

    def body(*refs):
        dz_ref, w_ref = refs[:2]
        o_ref = refs[2 + n_h]
        if n_h:
            comm_start, comm_wait = _rs_halves(refs[2:2 + n_h], refs[3 + n_h:3 + 2 * n_h], *refs[3 + 2 * n_h:])
            pl.when((pl.program_id(0) == 0) & (pl.program_id(1) == 0))(comm_start)
        o_ref[...] = _dot_nt(dz_ref[...], w_ref[...])
        if n_h:
            pl.when((pl.program_id(0) == n_rows - 1) & (pl.program_id(1) == N_CHIP - 1))(comm_wait)

    sem = pltpu.SemaphoreType.DMA((max(n_h, 1),))
    res = _pcall(
        body, name=name, grid=(n_rows, N_CHIP),
        in_specs=[pl.BlockSpec((tm, n), lambda i, s: (i, 0)),
                  pl.BlockSpec((None, None, ks, n), lambda i, s: (s, layer, 0, 0))] + [_ANY] * n_h,
        out_specs=[pl.BlockSpec((None, tm, ks), lambda i, s: (s, i, 0))] + [_ANY] * n_h,
        out_shape=[_sds((N_CHIP, SEQ, ks), F32)] + _rs_halves_shapes(halves),
        scratch_shapes=[sem, sem] if n_h else [],
        compiler_params=_cparams(("arbitrary", "arbitrary") if n_h else ("parallel", "arbitrary")),
    )(dz, wg, *halves)
    return res if n_h else res[0]


def _mm_rows_bwd_w(a4, dz, name, tm=2048):
    return _mm_rows_bwd_w_multi([a4], dz, name, tm)[0]


def _mm_rows_bwd_w_multi(a4s, dz, name, tm=2048):
    n_a = len(a4s)
    ks = a4s[0].shape[2]
    n = dz.shape[1]

    def body(*refs):
        dz_ref = refs[n_a]

        for j in range(n_a):
            o_ref = refs[n_a + 1 + j]

            @pl.when(pl.program_id(1) == 0)
            def _(o_ref=o_ref):
                o_ref[...] = jnp.zeros_like(o_ref)

            o_ref[...] += _dot_tn(refs[j][...], dz_ref[...])

    return _pcall(
        body, name=name, grid=(N_CHIP, SEQ // tm),
        in_specs=[pl.BlockSpec((None, tm, ks), lambda s, i: (s, i, 0))] * n_a
        + [pl.BlockSpec((tm, n), lambda s, i: (i, 0))],
        out_specs=[pl.BlockSpec((None, ks, n), lambda s, i: (s, 0, 0))] * n_a,
        out_shape=[_sds((N_CHIP, ks, n), F32)] * n_a,
        compiler_params=_cparams(("parallel", "arbitrary")),
    )(*a4s, dz)


def _ffn_up(x, gain, sc, sh, w1g, w3g, layer, name, tm=1024):
    def body(x_ref, g_ref, sc_ref, sh_ref, w1_ref, w3_ref, a1_ref, a3_ref, u_ref, h_ref):
        @pl.when(pl.program_id(1) == 0)
        def _():
            h_ref[...] = _norm_mod(x_ref, g_ref, sc_ref, sh_ref)

        hv = h_ref[...]
        a1 = _dot_nt(hv, w1_ref[...])
        a3 = _dot_nt(hv, w3_ref[...])
        a1_ref[...] = a1.astype(BF16)
        a3_ref[...] = a3.astype(BF16)
        u_ref[...] = (_silu(a1) * a3).astype(BF16)

    wspec = pl.BlockSpec((None, None, FFN_SHARD, D_MODEL), lambda i, s: (s, layer, 0, 0))
    ospec = pl.BlockSpec((None, tm, FFN_SHARD), lambda i, s: (s, i, 0))
    shp = (N_CHIP, SEQ, FFN_SHARD)
    rows = pl.BlockSpec((tm, D_MODEL), lambda i, s: (i, 0))
    return _pcall(
        body, name=name, grid=(SEQ // tm, N_CHIP),
        in_specs=[rows] + _NORM_SPECS + [wspec, wspec],
        out_specs=[ospec, ospec, ospec, rows],
        out_shape=[_sds(shp, BF16), _sds(shp, BF16), _sds(shp, BF16), _sds((SEQ, D_MODEL), BF16)],
        compiler_params=_cparams(("parallel", "arbitrary")),
    )(x, gain, sc, sh, w1g, w3g)


def _ffn_up_bwd(da1, da3, w1g, w3g, layer, name, tm=512):
    def body(d1_ref, d3_ref, w1_ref, w3_ref, o_ref):
        acc = _dot(d1_ref[0], w1_ref[0]) + _dot(d3_ref[0], w3_ref[0])
        for s in range(1, N_CHIP):
            acc += _dot(d1_ref[s], w1_ref[s]) + _dot(d3_ref[s], w3_ref[s])
        o_ref[...] = acc

    wspec = pl.BlockSpec((N_CHIP, None, FFN_SHARD, D_MODEL), lambda i: (0, layer, 0, 0))
    dspec = pl.BlockSpec((N_CHIP, tm, FFN_SHARD), lambda i: (0, i, 0))
    return _pcall(
        body, name=name, grid=(SEQ // tm,),
        in_specs=[dspec, dspec, wspec, wspec],
        out_specs=pl.BlockSpec((tm, D_MODEL), lambda i: (i, 0)),
        out_shape=_sds((SEQ, D_MODEL), F32),
        compiler_params=_cparams(("parallel",)),
    )(da1, da3, w1g, w3g)


def _ffn_down_bwd(dz, w2g, layer, a1, a3, name, tm=1024, halves=()):
    n_h = len(halves)
    n_rows = SEQ // tm

    def body(*refs):
        dz_ref, w_ref, a1_ref, a3_ref = refs[:4]
        da1_ref, da3_ref = refs[4 + n_h:6 + n_h]
        if n_h:
            comm_start, comm_wait = _rs_halves(refs[4:4 + n_h], refs[6 + n_h:6 + 2 * n_h], *refs[6 + 2 * n_h:])
            pl.when((pl.program_id(0) == 0) & (pl.program_id(1) == 0))(comm_start)
        du = _dot_nt(dz_ref[...], w_ref[...])
        a1 = a1_ref[...].astype(F32)
        da1_ref[...] = (du * a3_ref[...].astype(F32) * _dsilu(a1)).astype(BF16)
        da3_ref[...] = (du * _silu(a1)).astype(BF16)
        if n_h:
            pl.when((pl.program_id(0) == n_rows - 1) & (pl.program_id(1) == N_CHIP - 1))(comm_wait)

    blk = pl.BlockSpec((None, tm, FFN_SHARD), lambda i, s: (s, i, 0))
    shp = (N_CHIP, SEQ, FFN_SHARD)
    sem = pltpu.SemaphoreType.DMA((max(n_h, 1),))
    return _pcall(
        body, name=name, grid=(n_rows, N_CHIP),
        in_specs=[pl.BlockSpec((tm, D_MODEL), lambda i, s: (i, 0)),
                  pl.BlockSpec((None, None, FFN_SHARD, D_MODEL), lambda i, s: (s, layer, 0, 0)),
                  blk, blk] + [_ANY] * n_h,
        out_specs=[blk, blk] + [_ANY] * n_h,
        out_shape=[_sds(shp, BF16), _sds(shp, BF16)] + _rs_halves_shapes(halves),
        scratch_shapes=[sem, sem] if n_h else [],
        compiler_params=_cparams(("arbitrary", "arbitrary") if n_h else ("parallel", "arbitrary")),
    )(dz, w2g, a1, a3, *halves)


def _loss_head(y, target, z, gate, name):
    tm = 512
    n_steps = SEQ // tm

    def body(y_ref, t_ref, z_ref, gate_ref, dy_ref, l_ref, dz_ref, dgate_ref, acc_ref):
        @pl.when(pl.program_id(0) == 0)
        def _():
            acc_ref[...] = jnp.zeros_like(acc_ref)
            dgate_ref[...] = jnp.zeros_like(dgate_ref)

        err = y_ref[...] - t_ref[...]
        dy = err * (1.0 / D_MODEL)
        dy_ref[...] = dy
        acc_ref[...] += jnp.sum(jnp.mean(err * err, axis=-1, keepdims=True), axis=0, keepdims=True)
        _gated_branch_bwd(dy, z_ref, gate_ref, dz_ref, dgate_ref)

        @pl.when(pl.program_id(0) == n_steps - 1)
        def _():
            l_ref[...] = 0.5 * acc_ref[...]

    big = pl.BlockSpec((tm, D_MODEL), lambda i: (i, 0))
    vec = pl.BlockSpec((1, D_MODEL), lambda i: (0, 0))
    return _pcall(
        body, name=name, grid=(n_steps,),
        in_specs=[big, big, big, vec],
        out_specs=[big, pl.BlockSpec((1, 1), lambda i: (0, 0)), big, vec],
        out_shape=[_sds((SEQ, D_MODEL), F32), _sds((1, 1), F32), _sds((SEQ, D_MODEL), BF16),
                   _sds((1, D_MODEL), F32)],
        scratch_shapes=[pltpu.VMEM((1, 1), F32)],
        compiler_params=_cparams(("arbitrary",)),
    )(y, target, z, gate)


def _attn_rows(base, d):
    if d == 1:
        return pl.ds(pl.multiple_of(base, ATT_BLK), ATT_BLK)
    return pl.ds(base, ATT_BLK, stride=d)


def _attn_block_index(i, d):
    nb = SEQ // (ATT_BLK * d)
    r = i // nb
    n = i % nb
    base = r + n * (ATT_BLK * d)
    pbase = jnp.maximum(base - ATT_BLK * d, r)
    return n, _attn_rows(base, d), _attn_rows(pbase, d)


def _attn_two_blocks(ref, prow, rows):
    return jnp.concatenate([ref[prow, :].astype(BF16), ref[rows, :].astype(BF16)], axis=0)


def _attn_block_bias(b_ref, n):
    b = b_ref[...]
    prev_half = lax.broadcasted_iota(jnp.int32, b.shape, 1) < ATT_BLK
    return jnp.where(prev_half & (n == 0), NEG, b)


def _qk_normed(x):
    rs = lax.rsqrt(jnp.mean(x * x, axis=-1, keepdims=True) + RMS_EPS)
    return x * rs, rs


def _attn_fwd(qkv9, qgain, kgain, bias, name, gather=()):
    n_g = len(gather)

    def body(*refs):
        q_ref, k_ref, v_ref, qg_ref, kg_ref, b_ref = refs[:6]
        o_ref, lse_ref = refs[6 + n_g:8 + n_g]
        qn_s, kn_s, acc_s, m_s, l_s = refs[8 + 2 * n_g:13 + 2 * n_g]
        g = pl.program_id(1)
        if n_g:
            comm_start, comm_wait = _gather_ici(refs[8 + n_g:8 + 2 * n_g], *refs[13 + 2 * n_g:])
            pl.when((pl.program_id(0) == 0) & (g == 0))(comm_start)

        @pl.when(g == 0)
        def _():
            m_s[...] = jnp.full_like(m_s, NEG)
            l_s[...] = jnp.zeros_like(l_s)
            acc_s[...] = jnp.zeros_like(acc_s)

        qn_s[...] = _qk_normed(q_ref[...])[0] * qg_ref[...]
        kn_s[...] = _qk_normed(k_ref[...])[0] * kg_ref[...]

        for gi, (_, d) in enumerate(GROUPS):
            @pl.when(g == gi)
            def _(d=d):
                def block(n, qb, kk, vv, m_old, l_old, acc_old):
                    s = _dot_nt(qb, kk) * ATT_SCALE + _attn_block_bias(b_ref, n)
                    m_new = jnp.maximum(m_old, jnp.max(s, axis=-1, keepdims=True))
                    alpha = jnp.exp(m_old - m_new)
                    p = jnp.exp(s - m_new)
                    l_new = alpha * l_old + jnp.sum(p, axis=-1, keepdims=True)
                    acc_new = alpha * acc_old + _dot(p.astype(BF16), vv)
                    return m_new, l_new, acc_new

                def it(i, carry):
                    where, loaded = [], []
                    for way in range(ATT_WAYS):
                        n, rows, prow = _attn_block_index(i + way * ATT_STEPS, d)
                        where.append(rows)
                        loaded.append((n, qn_s[rows, :].astype(BF16), _attn_two_blocks(kn_s, prow, rows),
                                       _attn_two_blocks(v_ref, prow, rows), m_s[rows, :], l_s[rows, :],
                                       acc_s[rows, :]))
                    results = [block(*vals) for vals in loaded]
                    for rows, (m_new, l_new, acc_new) in zip(where, results):
                        m_s[rows, :] = m_new
                        l_s[rows, :] = l_new
                        acc_s[rows, :] = acc_new
                    return carry

                lax.fori_loop(0, ATT_STEPS, it, 0)

        @pl.when(g == len(GROUPS) - 1)
        def _():
            o_ref[...] = (acc_s[...] / l_s[...]).astype(BF16)
            lse_ref[...] = m_s[...] + jnp.log(l_s[...])

        if n_g:
            pl.when((pl.program_id(0) == HEADS - 1) & (g == len(GROUPS) - 1))(comm_wait)

    def col(j):
        return pl.BlockSpec((None, SEQ, HEAD_DIM), lambda h, g: (g * 3 + j, 0, h))

    gspec = pl.BlockSpec((None, 1, HEAD_DIM), lambda h, g: (g, 0, 0))
    sem = pltpu.SemaphoreType.DMA((max(n_g, 1), 3))
    return _pcall(
        body, name=name, grid=(HEADS, len(GROUPS)),
        in_specs=[col(0), col(1), col(2), gspec, gspec,
                  pl.BlockSpec((None, None, ATT_BLK, 2 * ATT_BLK), lambda h, g: (g, h, 0, 0))] + [_ANY] * n_g,
        out_specs=[pl.BlockSpec((None, SEQ, HEAD_DIM), lambda h, g: (h // 2, 0, h % 2)),
                   pl.BlockSpec((None, SEQ, 1), lambda h, g: (h, 0, 0))] + [_ANY] * n_g,
        out_shape=[_sds((N_CHIP, SEQ, 2 * HEAD_DIM), BF16), _sds((HEADS, SEQ, 1), F32)]
        + [_sds(s.shape, s.dtype) for s in gather],
        input_output_aliases={6 + a: 2 + a for a in range(n_g)},
        scratch_shapes=[pltpu.VMEM((SEQ, HEAD_DIM), F32)] * 3 + [pltpu.VMEM((SEQ, 1), F32)] * 2
        + ([sem, sem] if n_g else []),
        compiler_params=_cparams(("arbitrary", "arbitrary")),
    )(qkv9, qkv9, qkv9, qgain, kgain, bias, *gather)


def _attn_bwd(qkv9, qgain, kgain, bias, do4, o4, lse, name, scatter=()):
    n_s = len(scatter)

    def body(*refs):
        q_ref, k_ref, v_ref, qg_ref, kg_ref, b_ref, do_ref, o_ref, lse_ref = refs[:9]
        dqkv_ref, dqg_ref, dkg_ref, db_ref = refs[9 + n_s:13 + n_s]
        qn_s, kn_s, dq_s, dk_s, dv_s, dl_s = refs[13 + 2 * n_s:19 + 2 * n_s]
        g = pl.program_id(1)
        if n_s:
            comm_start, comm_wait = _rs_chips(refs[9:9 + n_s], refs[13 + n_s:13 + 2 * n_s], *refs[19 + 2 * n_s:])
            pl.when((pl.program_id(0) == 0) & (g == 0))(comm_start)
        qh, rq = _qk_normed(q_ref[...])
        kh, rk = _qk_normed(k_ref[...])
        qn_s[...] = qh * qg_ref[...]
        kn_s[...] = kh * kg_ref[...]
        @pl.when(g == 0)
        def _():
            dl_s[...] = jnp.sum(do_ref[...] * o_ref[...].astype(F32), axis=-1, keepdims=True)

        dk_s[...] = jnp.zeros_like(dk_s)
        dv_s[...] = jnp.zeros_like(dv_s)
        db_ref[...] = jnp.zeros_like(db_ref)

        for gi, (_, d) in enumerate(GROUPS):
            @pl.when(g == gi)
            def _(d=d):
                def block(n, qb, kk, vv, dob, lse_b, dl):
                    s = _dot_nt(qb, kk) * ATT_SCALE + _attn_block_bias(b_ref, n)
                    p = jnp.exp(s - lse_b)
                    ds = p * (_dot_nt(dob, vv) - dl)
                    ds16 = ds.astype(BF16)
                    return (ds, _dot(ds16, kk) * ATT_SCALE, _dot_tn(ds16, qb) * ATT_SCALE,
                            _dot_tn(p.astype(BF16), dob))

                def it(i, carry):
                    where, loaded, old = [], [], []
                    for way in range(ATT_WAYS):
                        n, rows, prow = _attn_block_index(i + way * ATT_STEPS, d)
                        where.append((rows, prow))
                        loaded.append((n, qn_s[rows, :].astype(BF16), _attn_two_blocks(kn_s, prow, rows),
                                       _attn_two_blocks(v_ref, prow, rows), do_ref[rows, :].astype(BF16),
                                       lse_ref[rows, :], dl_s[rows, :]))
                        old.append((dk_s[rows, :], dk_s[prow, :], dv_s[rows, :], dv_s[prow, :]))
                    results = [block(*vals) for vals in loaded]
                    db_ref[...] += functools.reduce(lambda a, b: a + b, [r[0] for r in results])
                    for (rows, prow), (dk_c, dk_p, dv_c, dv_p), (_, dq, dkk, dvv) in zip(where, old, results):
                        dq_s[rows, :] = dq
                        dk_s[prow, :] = dk_p + dkk[:ATT_BLK]
                        dv_s[prow, :] = dv_p + dvv[:ATT_BLK]
                        dk_s[rows, :] = dk_c + dkk[ATT_BLK:]
                        dv_s[rows, :] = dv_c + dvv[ATT_BLK:]
                    return carry

                lax.fori_loop(0, ATT_STEPS, it, 0)

        def norm_bwd(dn, xh, rs, gain):
            dgain = jnp.sum(dn * xh, axis=0, keepdims=True)
            dxh = dn * gain
            return rs * (dxh - xh * jnp.mean(dxh * xh, axis=-1, keepdims=True)), dgain

        dq, dqg = norm_bwd(dq_s[...], qh, rq, qg_ref[...])
        dk, dkg = norm_bwd(dk_s[...], kh, rk, kg_ref[...])
        dqkv_ref[0] = dq.astype(BF16)
        dqkv_ref[1] = dk.astype(BF16)
        dqkv_ref[2] = dv_s[...].astype(BF16)
        dqg_ref[...] = dqg
        dkg_ref[...] = dkg
        if n_s:
            pl.when((pl.program_id(0) == HEADS - 1) & (g == len(GROUPS) - 1))(comm_wait)

    def col(j):
        return pl.BlockSpec((None, SEQ, HEAD_DIM), lambda h, g: (g * 3 + j, 0, h))

    gspec = pl.BlockSpec((None, 1, HEAD_DIM), lambda h, g: (g, 0, 0))
    bspec = pl.BlockSpec((None, None, ATT_BLK, 2 * ATT_BLK), lambda h, g: (g, h, 0, 0))
    hcol = pl.BlockSpec((None, SEQ, HEAD_DIM), lambda h, g: (h // 2, 0, h % 2))
    dgspec = pl.BlockSpec((None, None, 1, HEAD_DIM), lambda h, g: (h, g, 0, 0))
    ng = len(GROUPS)
    sem = pltpu.SemaphoreType.DMA((max(n_s, 1), 3))
    return _pcall(
        body, name=name, grid=(HEADS, ng),
        in_specs=[col(0), col(1), col(2), gspec, gspec, bspec, hcol, hcol,
                  pl.BlockSpec((None, SEQ, 1), lambda h, g: (h, 0, 0))] + [_ANY] * n_s,
        out_specs=[pl.BlockSpec((None, 3, SEQ, HEAD_DIM), lambda h, g: (g, 0, 0, h)), dgspec, dgspec, bspec]
        + [_ANY] * n_s,
        out_shape=[_sds((ng, 3, SEQ, D_MODEL), BF16), _sds((HEADS, ng, 1, HEAD_DIM), F32),
                   _sds((HEADS, ng, 1, HEAD_DIM), F32), _sds((ng, HEADS, ATT_BLK, 2 * ATT_BLK), F32)]
        + _rs_chips_shapes(scatter),
        scratch_shapes=[pltpu.VMEM((SEQ, HEAD_DIM), F32)] * 5 + [pltpu.VMEM((SEQ, 1), F32)]
        + ([sem, sem] if n_s else []),
        compiler_params=_cparams(("arbitrary", "arbitrary")),
    )(qkv9, qkv9, qkv9, qgain, kgain, bias, do4, o4, lse, *scatter)


def _relbias_bwd(dbias, bucket_idx, name):
    ng = len(GROUPS)

    def body(db_ref, idx_ref, o_ref):
        lane = lax.broadcasted_iota(jnp.int32, (HEADS, 128), 1)
        acc = jnp.zeros((HEADS, 128), F32)
        for g in range(ng):
            dbg = db_ref[g]
            idx = idx_ref[g]
            for b in range(NUM_BUCKETS):
                sel = jnp.where((idx == b)[None], dbg, 0.0)
                part = jnp.sum(sel, axis=1)
                val = jnp.sum(part, axis=-1, keepdims=True)
                acc = jnp.where(lane == g * NUM_BUCKETS + b, val, acc)
        o_ref[...] = acc

    return _pcall(body, name=name, out_shape=_sds((HEADS, 128), F32), compiler_params=_cparams())(dbias, bucket_idx)


def _scan16(x, reverse=False):
    row = lax.broadcasted_iota(jnp.int32, x.shape, 0)
    for sh in (1, 2, 4, 8):
        if reverse:
            x = x + jnp.where(row < HG_SUB - sh, pltpu.roll(x, HG_SUB - sh, 0), 0.0)
        else:
            x = x + jnp.where(row >= sh, pltpu.roll(x, sh, 0), 0.0)
    return x


def _hgrn_gates(qr, fr, lbv):
    q = _silu(qr)
    sig = _sigmoid(fr)
    fg = lbv + (1.0 - lbv) * sig
    lf = jnp.log(fg) * LOG2_E
    gcum = _scan16(lf)
    glast = jnp.sum(lf, axis=0, keepdims=True)
    return q, sig, fg, 1.0 - fg, gcum, glast


def _hgrn_intra(q, k, gcum, tri):
    e = jnp.exp2(jnp.where(tri, gcum[:, None, :] - gcum[None, :, :], NEG))
    a = jnp.sum(q[:, None, :] * k[None, :, :] * e, axis=-1, keepdims=True)
    return e, a


def _hgrn_fwd(proj4, lb, gain, name):
    nsub = HG_TC // HG_SUB
    wide = HG_HP * HEAD_DIM

    def body(p_ref, lb_ref, gn_ref, o_ref, y_ref, st_ref, state_s):
        @pl.when(pl.program_id(1) == 0)
        def _():
            state_s[...] = jnp.zeros_like(state_s)

        gnv = gn_ref[...]
        shp = (HG_SUB, HG_SUB, HEAD_DIM)
        tri = lax.broadcasted_iota(jnp.int32, shp, 0) >= lax.broadcasted_iota(jnp.int32, shp, 1)

        def head(qr, fr, vv, gr, lbv, st):
            q, _, _, k, gcum, glast = _hgrn_gates(qr, fr, lbv)
            _, a = _hgrn_intra(q, k, gcum, tri)
            o = jnp.sum(a * vv[None, :, :], axis=1) + _dot_nt((q * jnp.exp2(gcum)).astype(BF16), st.astype(BF16))
            kg = k * jnp.exp2(glast - gcum)
            st_new = st * jnp.exp2(glast) + _dot_tn(vv.astype(BF16), kg.astype(BF16))
            rs = lax.rsqrt(jnp.mean(o * o, axis=-1, keepdims=True) + RMS_EPS)
            return o, (o * rs * gnv * _silu(gr)).astype(BF16), st_new

        def it(i, carry):
            rows = pl.ds(pl.multiple_of(i * HG_SUB, HG_SUB), HG_SUB)
            loaded = []
            for hh in range(HG_HP):
                lanes = pl.ds(hh * HEAD_DIM, HEAD_DIM)
                loaded.append(([p_ref[j, rows, lanes] for j in range(4)], lb_ref[:, lanes], state_s[hh]))
            results = [head(blk[0], blk[1], blk[2], blk[3], lbv, st) for blk, lbv, st in loaded]
            for hh, ((_, _, st), (o, y, st_new)) in enumerate(zip(loaded, results)):
                lanes = pl.ds(hh * HEAD_DIM, HEAD_DIM)
                st_ref[hh, i] = st.astype(BF16)
                state_s[hh] = st_new
                o_ref[rows, lanes] = o
                y_ref[hh // 2, rows, pl.ds((hh % 2) * HEAD_DIM, HEAD_DIM)] = y
            return carry

        lax.fori_loop(0, nsub, it, 0)

    return _pcall(
        body, name=name, grid=(HEADS // HG_HP, SEQ // HG_TC),
        in_specs=[pl.BlockSpec((4, HG_TC, wide), lambda h, j: (0, j, h)),
                  pl.BlockSpec((1, wide), lambda h, j: (0, h)),
                  pl.BlockSpec((1, HEAD_DIM), lambda h, j: (0, 0))],
        out_specs=[pl.BlockSpec((HG_TC, wide), lambda h, j: (j, h)),
                   pl.BlockSpec((HG_HP // 2, HG_TC, 2 * HEAD_DIM), lambda h, j: (h, j, 0)),
                   pl.BlockSpec((HG_HP, nsub, HEAD_DIM, HEAD_DIM), lambda h, j: (h, j, 0, 0))],
        out_shape=[_sds((SEQ, D_MODEL), F32), _sds((N_CHIP, SEQ, 2 * HEAD_DIM), BF16),
                   _sds((HEADS, SEQ // HG_SUB, HEAD_DIM, HEAD_DIM), BF16)],
        scratch_shapes=[pltpu.VMEM((HG_HP, HEAD_DIM, HEAD_DIM), F32)],
        compiler_params=_cparams(("parallel", "arbitrary")),
    )(proj4, lb, gain)


def _hgrn_bwd(proj4, lb, gain, o_raw, dy4, states, name):
    nsub = HG_TC // HG_SUB
    nt = SEQ // HG_TC
    wide = HG_HP * HEAD_DIM

    def body(p_ref, lb_ref, gn_ref, o_ref, dy_ref, st_ref, dp_ref, dlb_ref, dgn_ref, dst_s):
        @pl.when(pl.program_id(1) == 0)
        def _():
            dst_s[...] = jnp.zeros_like(dst_s)
            dlb_ref[...] = jnp.zeros_like(dlb_ref)
            dgn_ref[...] = jnp.zeros_like(dgn_ref)

        gnv = gn_ref[...]
        shp = (HG_SUB, HG_SUB, HEAD_DIM)
        tri = lax.broadcasted_iota(jnp.int32, shp, 0) >= lax.broadcasted_iota(jnp.int32, shp, 1)

        def head(qr, fr, vv, gr, o, dy, lbv, st0, dst):
            q, sig, fg, k, gcum, glast = _hgrn_gates(qr, fr, lbv)
            rs = lax.rsqrt(jnp.mean(o * o, axis=-1, keepdims=True) + RMS_EPS)
            oh = o * rs
            don = dy * _silu(gr)
            dgn = jnp.sum(don * oh, axis=0, keepdims=True)
            dgr = dy * oh * gnv * _dsilu(gr)
            doh = don * gnv
            do = rs * (doh - oh * jnp.mean(doh * oh, axis=-1, keepdims=True))
            dst16 = dst.astype(BF16)
            do16 = do.astype(BF16)
            eg = jnp.exp2(gcum)
            eb = jnp.exp2(glast - gcum)
            e, a = _hgrn_intra(q, k, gcum, tri)
            da = jnp.sum(do[:, None, :] * vv[None, :, :], axis=-1, keepdims=True)
            dae = da * e
            dq = jnp.sum(dae * k[None, :, :], axis=1) + eg * _dot(do16, st0)
            dk_state = eb * _dot(vv.astype(BF16), dst16)
            dk = jnp.sum(dae * q[:, None, :], axis=0) + dk_state
            dv = jnp.sum(a * do[:, None, :], axis=0) + _dot_nt((k * eb).astype(BF16), dst16)
            eglast = jnp.exp2(glast)
            dst_new = dst * eglast + _dot_tn(do16, (q * eg).astype(BF16))
            dglast = jnp.sum(k * dk_state, axis=0, keepdims=True) \
                + eglast * jnp.sum(dst * st0.astype(F32), axis=0, keepdims=True)
            dlf = _scan16(q * dq - k * dk, reverse=True) + dglast
            dfg = dlf / fg - dk
            dlb = jnp.sum(dfg * (1.0 - sig), axis=0, keepdims=True)
            dproj = ((dq * _dsilu(qr)).astype(BF16), (dfg * (1.0 - lbv) * sig * (1.0 - sig)).astype(BF16),
                     dv.astype(BF16), dgr.astype(BF16))
            return dproj, dst_new, dlb, dgn

        def it(ii, carry):
            i = nsub - 1 - ii
            rows = pl.ds(pl.multiple_of(i * HG_SUB, HG_SUB), HG_SUB)
            results = []
            for hh in range(HG_HP):
                lanes = pl.ds(hh * HEAD_DIM, HEAD_DIM)
                blk = [p_ref[j, rows, lanes] for j in range(4)]
                dy = dy_ref[hh // 2, rows, pl.ds((hh % 2) * HEAD_DIM, HEAD_DIM)]
                results.append(head(blk[0], blk[1], blk[2], blk[3], o_ref[rows, lanes], dy,
                                    lb_ref[:, lanes], st_ref[hh, i], dst_s[hh]))
            new_carry = []
            for hh, (dproj, dst_new, dlb, dgn) in enumerate(results):
                lanes = pl.ds(hh * HEAD_DIM, HEAD_DIM)
                dst_s[hh] = dst_new
                for j in range(4):
                    dp_ref[j, rows, lanes] = dproj[j]
                new_carry.append((carry[hh][0] + dlb, carry[hh][1] + dgn))
            return tuple(new_carry)

        zero = jnp.zeros((1, HEAD_DIM), F32)
        sums = lax.fori_loop(0, nsub, it, tuple((zero, zero) for _ in range(HG_HP)))
        for hh in range(HG_HP):
            dlb_ref[hh] += sums[hh][0]
            dgn_ref[hh] += sums[hh][1]

    vspec = pl.BlockSpec((HG_HP, 1, HEAD_DIM), lambda h, j: (h, 0, 0))
    return _pcall(
        body, name=name, grid=(HEADS // HG_HP, nt),
        in_specs=[pl.BlockSpec((4, HG_TC, wide), lambda h, j: (0, nt - 1 - j, h)),
                  pl.BlockSpec((1, wide), lambda h, j: (0, h)),
                  pl.BlockSpec((1, HEAD_DIM), lambda h, j: (0, 0)),
                  pl.BlockSpec((HG_TC, wide), lambda h, j: (nt - 1 - j, h)),
                  pl.BlockSpec((HG_HP // 2, HG_TC, 2 * HEAD_DIM), lambda h, j: (h, nt - 1 - j, 0)),
                  pl.BlockSpec((HG_HP, nsub, HEAD_DIM, HEAD_DIM), lambda h, j: (h, nt - 1 - j, 0, 0))],
        out_specs=[pl.BlockSpec((4, HG_TC, wide), lambda h, j: (0, nt - 1 - j, h)), vspec, vspec],
        out_shape=[_sds((4, SEQ, D_MODEL), BF16), _sds((HEADS, 1, HEAD_DIM), F32), _sds((HEADS, 1, HEAD_DIM), F32)],
        scratch_shapes=[pltpu.VMEM((HG_HP, HEAD_DIM, HEAD_DIM), F32)],
        compiler_params=_cparams(("parallel", "arbitrary")),
    )(proj4, lb, gain, o_raw, dy4, states)


def _t5_bucket(dist):
    n = np.asarray(dist, dtype=np.int64)
    max_exact = NUM_BUCKETS // 2
    large = max_exact + (np.log(np.maximum(n, 1) / max_exact) / np.log(MAX_DISTANCE / max_exact)
                         * (NUM_BUCKETS - max_exact)).astype(np.int64)
    large = np.minimum(large, NUM_BUCKETS - 1)
    return np.where(n < max_exact, n, large).astype(np.int32)


def _bias_tables():
    qi = np.arange(ATT_BLK)[:, None]
    ki = np.arange(2 * ATT_BLK)[None, :]
    j = ATT_BLK + qi - ki
    valid = (j >= 0) & (j <= ATT_BLK)
    return np.stack([np.where(valid, _t5_bucket(np.clip(j, 0, ATT_BLK) * d), -1) for _, d in GROUPS]).astype(np.int32)


def _attn_bias(rel_bias, name):
    idx = _bias_tables()
    ng = len(GROUPS)
    buckets = [sorted(set(idx[g][idx[g] >= 0].tolist())) for g in range(ng)]

    def body(rb_ref, idx_ref, o_ref):
        h = pl.program_id(0)
        for g in range(ng):
            ig = idx_ref[g]
            acc = jnp.full(ig.shape, NEG, F32)
            for b in buckets[g]:
                acc = jnp.where(ig == b, rb_ref[b, g * HEADS + h], acc)
            o_ref[g] = acc

    return _pcall(
        body, name=name, grid=(HEADS,),
        in_specs=[pl.BlockSpec(memory_space=pltpu.SMEM),
                  pl.BlockSpec((ng, ATT_BLK, 2 * ATT_BLK), lambda h: (0, 0, 0))],
        out_specs=pl.BlockSpec((ng, None, ATT_BLK, 2 * ATT_BLK), lambda h: (0, h, 0, 0)),
        out_shape=_sds((ng, HEADS, ATT_BLK, 2 * ATT_BLK), F32),
        compiler_params=_cparams(("parallel",)),
    )(rel_bias, jnp.asarray(idx))


ADA_SHARD = 6 * D_MODEL // N_CHIP
ADA_TN = 512


def _ada_fwd(c_all, ada_w, ada_b_cols, name):
    def body(c_ref, w_ref, b_ref, o_ref):
        ca = _silu(c_ref[...]).astype(BF16)
        o_ref[...] = _dot(ca, w_ref[...].astype(BF16)) + b_ref[...]

    return _pcall(
        body, name=name, grid=(DEPTH, ADA_SHARD // ADA_TN),
        in_specs=[pl.BlockSpec((N_DEV, D_MODEL), lambda l, j: (0, 0)),
                  pl.BlockSpec((None, D_MODEL, ADA_TN), lambda l, j: (l, 0, j)),
                  pl.BlockSpec((None, 1, ADA_TN), lambda l, j: (l, 0, j))],
        out_specs=pl.BlockSpec((None, N_DEV, ADA_TN), lambda l, j: (l, 0, j)),
        out_shape=_sds((DEPTH, N_DEV, ADA_SHARD), F32),
        compiler_params=_cparams(("parallel", "parallel")),
    )(c_all, ada_w, ada_b_cols)


def _ada_bwd(c_all, dmod_cols, name):
    def body(c_ref, d_ref, o_ref):
        ca = _silu(c_ref[...]).astype(BF16)
        o_ref[...] = _dot_tn(ca, d_ref[...].astype(BF16))

    return _pcall(
        body, name=name, grid=(DEPTH, ADA_SHARD // ADA_TN),
        in_specs=[pl.BlockSpec((N_DEV, D_MODEL), lambda l, j: (0, 0)),
                  pl.BlockSpec((None, N_DEV, ADA_TN), lambda l, j: (l, 0, j))],
        out_specs=pl.BlockSpec((None, D_MODEL, ADA_TN), lambda l, j: (l, 0, j)),
        out_shape=_sds((DEPTH, D_MODEL, ADA_SHARD), F32),
        compiler_params=_cparams(("parallel", "parallel")),
    )(c_all, dmod_cols)


def _lower_bounds(logits, name):
    def body(l_ref, o_ref):
        l0 = l_ref[0:1, :]
        l1 = l_ref[1:2, :]
        mx = jnp.maximum(l0, l1)
        e0 = jnp.exp(l0 - mx)
        e1 = jnp.exp(l1 - mx)
        p0 = e0 / (e0 + e1)
        p1 = e1 / (e0 + e1)
        o_ref[0:1, :] = p0 - p0
        o_ref[1:2, :] = (p0 + p1) - p0

    return _pcall(body, name=name, out_shape=_sds((DEPTH, D_MODEL), F32), compiler_params=_cparams())(logits)


_R_DMOD = 0
_R_NMIX = 96
_R_NFFN = 112
_R_QG = 128
_R_KG = 152
_R_GN = 176
_R_LB = 184
_R_RB = 192
SMALL_ROWS = 200


def _small_totals(gathered, logits8, name):
    ng = len(GROUPS)

    def body(g_ref, l_ref, main_ref, gains_ref, dlb_ref, rb_ref):
        tot = g_ref[0]
        for dev in range(1, N_DEV):
            tot = tot + g_ref[dev]
        main_ref[...] = tot[0:_R_QG]
        gains_ref[...] = jnp.zeros_like(gains_ref)
        for g in range(ng):
            gains_ref[g:g + 1, :] = jnp.sum(tot[_R_QG + 8 * g:_R_QG + 8 * g + 8], axis=0, keepdims=True)
            gains_ref[ng + g:ng + g + 1, :] = jnp.sum(tot[_R_KG + 8 * g:_R_KG + 8 * g + 8], axis=0, keepdims=True)
        gains_ref[2 * ng:2 * ng + 1, :] = jnp.sum(tot[_R_GN:_R_GN + 8], axis=0, keepdims=True)
        rb_ref[...] = tot[_R_RB:_R_RB + 8]
        dlb1 = tot[_R_LB:_R_LB + 8]
        l0 = l_ref[0]
        l1 = l_ref[1]
        mx = jnp.maximum(l0, l1)
        e0 = jnp.exp(l0 - mx)
        e1 = jnp.exp(l1 - mx)
        p0 = e0 / (e0 + e1)
        p1 = e1 / (e0 + e1)
        dlb_ref[0] = -p0 * p1 * dlb1
        dlb_ref[1] = p1 * (1.0 - p1) * dlb1

    return _pcall(
        body, name=name,
        out_shape=[_sds((_R_QG, 128), F32), _sds((8, 128), F32), _sds((DEPTH, 8, 128), F32), _sds((8, 128), F32)],
        compiler_params=_cparams(),
    )(gathered, logits8)


def _row_tile(rows):
    return 128 if rows % 128 == 0 else rows


def _adamw(w, grads, m, v, name):
    nl, r, cdim = w.shape
    tr = _row_tile(r)

    def body(*refs):
        g_refs = refs[:nl]
        w_ref, m_ref, v_ref, go_ref, d_ref, mo_ref, vo_ref = refs[nl:]

        def step(g):
            m2 = ADAM_B1 * m_ref[...] + (1.0 - ADAM_B1) * g
            v2 = ADAM_B2 * v_ref[...] + (1.0 - ADAM_B2) * (g * g)
            m_hat = m2 / (1.0 - ADAM_B1 ** ADAM_STEP)
            v_hat = v2 / (1.0 - ADAM_B2 ** ADAM_STEP)
            go_ref[...] = g
            d_ref[...] = -ADAM_LR * (m_hat / (jnp.sqrt(v_hat) + ADAM_EPS) + ADAM_WD * w_ref[...])
            mo_ref[...] = m2
            vo_ref[...] = v2

        if nl == 1:
            step(g_refs[0][...])
        else:
            for layer in range(nl):
                @pl.when(pl.program_id(0) == layer)
                def _(layer=layer):
                    step(g_refs[layer][...])

    big = pl.BlockSpec((None, tr, cdim), lambda l, i: (l, i, 0))
    g_specs = [pl.BlockSpec((tr, cdim), lambda l, i, layer=layer: (jnp.where(l == layer, i, 0), 0))
               for layer in range(nl)]
    shp = _sds((nl, r, cdim), F32)
    return _pcall(
        body, name=name, grid=(nl, r // tr),
        in_specs=g_specs + [big, big, big],
        out_specs=[big, big, big, big],
        out_shape=[shp, shp, shp, shp],
        compiler_params=_cparams(("parallel", "parallel")),
    )(*grads, w, m, v)


def _cast_bf16(place, ws, name):
    n_a = len(ws)
    nl, r, cdim = ws[0].shape
    tr = _row_tile(r)

    def body(place_ref, *refs):
        for a in range(n_a):
            refs[n_a + a][...] = refs[a][...].astype(BF16)

    return _pcall(
        body, name=name,
        grid_spec=pltpu.PrefetchScalarGridSpec(
            num_scalar_prefetch=1, grid=(nl, r // tr),
            in_specs=[pl.BlockSpec((None, tr, cdim), lambda l, i, place_ref: (l, i, 0))] * n_a,
            out_specs=[pl.BlockSpec((None, None, tr, cdim),
                                    lambda l, i, place_ref: (place_ref[1], l, i, 0))] * n_a),
        out_shape=[_sds((N_CHIP, nl, r, cdim), BF16)] * n_a,
        compiler_params=_cparams(("parallel", "parallel")),
    )(place, *ws)


def _rs_add_cast(place, grads, recvs, name):
    n_a = len(grads)
    _, k, n = grads[0].shape
    kh = k // 2
    tr = _row_tile(kh)
    nb = kh // tr

    def body(place_ref, *refs):
        for a in range(n_a):
            refs[2 * n_a + a][...] = (refs[a][...] + refs[n_a + a][...]).astype(BF16)

    half = pl.BlockSpec((None, tr, n), lambda s, i, place_ref: (s, i, 0))
    mine = pl.BlockSpec((None, tr, n), lambda s, i, place_ref: (s, place_ref[0] * nb + i, 0))
    return _pcall(
        body, name=name,
        grid_spec=pltpu.PrefetchScalarGridSpec(
            num_scalar_prefetch=1, grid=(N_CHIP, nb),
            in_specs=[mine] * n_a + [half] * n_a,
            out_specs=[half] * n_a),
        out_shape=[_sds((N_CHIP, kh, n), BF16)] * n_a,
        compiler_params=_cparams(("parallel", "parallel")),
    )(place, *grads, *recvs)


def _rs_sum4(place, parts, gots, name):
    n_a = len(parts)
    _, kh, n = parts[0].shape
    tr = _row_tile(kh)
    nb = kh // tr

    def body(place_ref, *refs):
        for a in range(n_a):
            acc = refs[a][...].astype(F32)
            for j in range(N_CHIP - 1):
                acc = acc + refs[n_a + a][j].astype(F32)
            refs[2 * n_a + a][...] = acc

    return _pcall(
        body, name=name,
        grid_spec=pltpu.PrefetchScalarGridSpec(
            num_scalar_prefetch=1, grid=(nb,),
            in_specs=[pl.BlockSpec((None, tr, n), lambda i, place_ref: (place_ref[1], i, 0))] * n_a
            + [pl.BlockSpec((N_CHIP - 1, tr, n), lambda i, place_ref: (0, i, 0))] * n_a,
            out_specs=[pl.BlockSpec((tr, n), lambda i, place_ref: (place_ref[0] * nb + i, 0))] * n_a),
        out_shape=[_sds((2 * kh, n), F32)] * n_a,
        compiler_params=_cparams(("parallel",)),
    )(place, *parts, *gots)


_ANY = pl.BlockSpec(memory_space=pl.ANY)


def _position():
    return lax.axis_index("x"), lax.axis_index("y"), lax.axis_index("c")


def _other_chips(x, y):
    return [(1 - x, y), (x, 1 - y), (1 - x, 1 - y)]


def _remote(src, dst, send_sem, recv_sem, to):
    return pltpu.make_async_remote_copy(src_ref=src, dst_ref=dst, send_sem=send_sem, recv_sem=recv_sem,
                                        device_id=to, device_id_type=MESH)


def _small_allgather(v, name):
    r = v.shape[0]

    def body(x_ref, out_ref, send_sems, recv_sems, local_sem):
        x, y, c = _position()
        me, sibling = (x, y, c), (x, y, 1 - c)
        chips = _other_chips(x, y)

        def slab(px, py, pc):
            return out_ref.at[4 * px + 2 * py + pc]

        def copy(k, block, to, src=None):
            return _remote(slab(*block) if src is None else src, slab(*block), send_sems.at[k], recv_sems.at[k], to)

        mine = pltpu.make_async_copy(x_ref, slab(*me), local_sem)
        mine.start()
        first = [copy(0, me, sibling, src=x_ref)]
        first += [copy(1 + j, me, (*chip, c), src=x_ref) for j, chip in enumerate(chips)]
        for cp in first:
            cp.start()
        passed = [copy(4 + j, (*chip, c), sibling) for j, chip in enumerate(chips)]
        for j, chip in enumerate(chips):
            copy(1 + j, (*chip, c), me).wait_recv()
            passed[j].start()
        copy(0, sibling, me).wait_recv()
        for j, chip in enumerate(chips):
            copy(4 + j, (*chip, 1 - c), me).wait_recv()
        for cp in first + passed:
            cp.wait_send()
        mine.wait()

    return _pcall(
        body, name=name,
        out_shape=_sds((N_DEV, r, 128), F32),
        in_specs=[pl.BlockSpec(memory_space=pltpu.VMEM)],
        out_specs=pl.BlockSpec(memory_space=pltpu.VMEM),
        scratch_shapes=[pltpu.SemaphoreType.DMA((7,)), pltpu.SemaphoreType.DMA((7,)), pltpu.SemaphoreType.DMA],
        compiler_params=_cparams(),
    )(v)


def _half_rows(core, kh):
    return pl.ds(pl.multiple_of(core * kh, 8), kh)


def _slab_half(ref, chip, core):
    return ref.at[chip, :, _half_rows(core, ref.shape[2] // 2), :]


def _gather_ici(out, send_sems, recv_sems):
    def copies():
        x, y, c = _position()
        for a in range(len(out)):
            for j, (px, py) in enumerate(_other_chips(x, y)):
                mine = _slab_half(out[a], 2 * x + y, c)
                landed = _slab_half(out[a], 2 * px + py, c)
                yield (_remote(mine, mine, send_sems.at[a, j], recv_sems.at[a, j], (px, py, c)),
                       _remote(landed, landed, send_sems.at[a, j], recv_sems.at[a, j], (px, py, c)))

    def start():
        for send, _ in copies():
            send.start()

    def wait():
        for send, recv in copies():
            recv.wait_recv()
            send.wait_send()

    return start, wait


def _gather_d2d(out, send_sems, recv_sems):
    def copies():
        x, y, c = _position()
        for a in range(len(out)):
            for j, (px, py) in enumerate(_other_chips(x, y)):
                landed = _slab_half(out[a], 2 * px + py, c)
                other = _slab_half(out[a], 2 * px + py, 1 - c)
                yield (_remote(landed, landed, send_sems.at[a, j], recv_sems.at[a, j], (x, y, 1 - c)),
                       _remote(other, other, send_sems.at[a, j], recv_sems.at[a, j], (x, y, 1 - c)))

    def start():
        for send, _ in copies():
            send.start()

    def wait():
        for send, recv in copies():
            recv.wait_recv()
            send.wait_send()

    return start, wait


def _gather_weights(slabs, name, ici=True):
    n = len(slabs)

    def body(*refs):
        out = refs[n:2 * n]
        sems = refs[2 * n:]
        if ici:
            start, wait = _gather_ici(out, sems[2], sems[3])
            start()
            wait()
        start, wait = _gather_d2d(out, sems[0], sems[1])
        start()
        wait()

    sem = pltpu.SemaphoreType.DMA((n, 3))
    return _pcall(
        body, name=name,
        out_shape=[_sds(s.shape, BF16) for s in slabs],
        in_specs=[_ANY] * n, out_specs=[_ANY] * n,
        input_output_aliases={a: a for a in range(n)},
        scratch_shapes=[sem, sem] + ([sem, sem] if ici else []),
        compiler_params=_cparams(),
    )(*slabs)


def _rs_halves(grads, out, send_sems, recv_sems):
    def copies():
        x, y, c = _position()
        for a in range(len(grads)):
            kh = grads[a].shape[1] // 2
            yield _remote(grads[a].at[:, _half_rows(1 - c, kh), :], out[a], send_sems.at[a], recv_sems.at[a],
                          (x, y, 1 - c))

    def start():
        for cp in copies():
            cp.start()

    def wait():
        for cp in copies():
            cp.wait()

    return start, wait


def _rs_halves_shapes(grads):
    return [_sds((N_CHIP, g.shape[1] // 2, g.shape[2]), F32) for g in grads]


def _rs_exchange_halves(grads, name):
    n = len(grads)

    def body(*refs):
        start, wait = _rs_halves(refs[:n], refs[n:2 * n], *refs[2 * n:])
        start()
        wait()

    return _pcall(
        body, name=name,
        out_shape=_rs_halves_shapes(grads),
        in_specs=[_ANY] * n, out_specs=[_ANY] * n,
        scratch_shapes=[pltpu.SemaphoreType.DMA((n,)), pltpu.SemaphoreType.DMA((n,))],
        compiler_params=_cparams(),
    )(*grads)


def _rs_chips(parts, out, send_sems, recv_sems):
    def copies():
        x, y, c = _position()
        for a in range(len(parts)):
            for j, (px, py) in enumerate(_other_chips(x, y)):
                got = out[a].at[j]
                yield (_remote(parts[a].at[2 * px + py], got, send_sems.at[a, j], recv_sems.at[a, j], (px, py, c)),
                       _remote(got, got, send_sems.at[a, j], recv_sems.at[a, j], (px, py, c)))

    def start():
        for send, _ in copies():
            send.start()

    def wait():
        for send, recv in copies():
            recv.wait_recv()
            send.wait_send()

    return start, wait


def _rs_chips_shapes(parts):
    return [_sds((N_CHIP - 1,) + p.shape[1:], BF16) for p in parts]


def _rs_join(out, send_sems, recv_sems):
    def copies():
        x, y, c = _position()
        for a in range(len(out)):
            kh = out[a].shape[0] // 2
            mine = out[a].at[_half_rows(c, kh), :]
            theirs = out[a].at[_half_rows(1 - c, kh), :]
            yield (_remote(mine, mine, send_sems.at[a], recv_sems.at[a], (x, y, 1 - c)),
                   _remote(theirs, theirs, send_sems.at[a], recv_sems.at[a], (x, y, 1 - c)))

    def start():
        for send, _ in copies():
            send.start()

    def wait():
        for send, recv in copies():
            recv.wait_recv()
            send.wait_send()

    return start, wait


def _rs_join_halves(fulls, name):
    n = len(fulls)

    def body(*refs):
        start, wait = _rs_join(refs[n:2 * n], *refs[2 * n:])
        start()
        wait()

    return _pcall(
        body, name=name,
        out_shape=[_sds(f.shape, F32) for f in fulls],
        in_specs=[_ANY] * n, out_specs=[_ANY] * n,
        input_output_aliases={a: a for a in range(n)},
        scratch_shapes=[pltpu.SemaphoreType.DMA((n,)), pltpu.SemaphoreType.DMA((n,))],
        compiler_params=_cparams(),
    )(*fulls)


_SMALL_ORDER = ("rel_bias", "ada_b", "norm_mix", "norm_ffn", "attn_q_gain", "attn_k_gain", "hgrn_gnorm",
                "hgrn_lower_bounds")
_WEIGHT_ORDER = ("rel_bias", "ada_w", "ada_b", "norm_mix", "norm_ffn", "attn_w_qkv", "attn_w_out", "attn_q_gain",
                 "attn_k_gain", "hgrn_w_in", "hgrn_w_out", "hgrn_gnorm", "hgrn_lower_bounds", "ffn_w1", "ffn_w3",
                 "ffn_w2")


def _qkv_group_map(t):
    return t // 4, t % 4


def _qkv_chip_map(t):
    return t // 9, t % 9


def _block_map(t):
    return t, 0


def _pack_rows(parts):
    return jnp.concatenate([p.reshape(-1, 128) for p in parts], axis=0)


def kernel(x, c, rel_bias, ada_w, ada_b, norm_mix, norm_ffn, attn_w_qkv, attn_w_out, attn_q_gain, attn_k_gain, hgrn_w_in, hgrn_w_out, hgrn_gnorm, hgrn_lower_bounds, ffn_w1, ffn_w3, ffn_w2, loss_target, m_rel_bias, m_ada_w, m_ada_b, m_norm_mix, m_norm_ffn, m_attn_w_qkv, m_attn_w_out, m_attn_q_gain, m_attn_k_gain, m_hgrn_w_in, m_hgrn_w_out, m_hgrn_gnorm, m_hgrn_lower_bounds, m_ffn_w1, m_ffn_w3, m_ffn_w2, v_rel_bias, v_ada_w, v_ada_b, v_norm_mix, v_norm_ffn, v_attn_w_qkv, v_attn_w_out, v_attn_q_gain, v_attn_k_gain, v_hgrn_w_in, v_hgrn_w_out, v_hgrn_gnorm, v_hgrn_lower_bounds, v_ffn_w1, v_ffn_w3, v_ffn_w2):
    weights = dict(rel_bias=rel_bias, ada_w=ada_w, ada_b=ada_b, norm_mix=norm_mix, norm_ffn=norm_ffn,
                   attn_w_qkv=attn_w_qkv, attn_w_out=attn_w_out, attn_q_gain=attn_q_gain, attn_k_gain=attn_k_gain,
                   hgrn_w_in=hgrn_w_in, hgrn_w_out=hgrn_w_out, hgrn_gnorm=hgrn_gnorm,
                   hgrn_lower_bounds=hgrn_lower_bounds, ffn_w1=ffn_w1, ffn_w3=ffn_w3, ffn_w2=ffn_w2)
    mom1 = dict(rel_bias=m_rel_bias, ada_w=m_ada_w, ada_b=m_ada_b, norm_mix=m_norm_mix, norm_ffn=m_norm_ffn,
                attn_w_qkv=m_attn_w_qkv, attn_w_out=m_attn_w_out, attn_q_gain=m_attn_q_gain,
                attn_k_gain=m_attn_k_gain, hgrn_w_in=m_hgrn_w_in, hgrn_w_out=m_hgrn_w_out, hgrn_gnorm=m_hgrn_gnorm,
                hgrn_lower_bounds=m_hgrn_lower_bounds, ffn_w1=m_ffn_w1, ffn_w3=m_ffn_w3, ffn_w2=m_ffn_w2)
    mom2 = dict(rel_bias=v_rel_bias, ada_w=v_ada_w, ada_b=v_ada_b, norm_mix=v_norm_mix, norm_ffn=v_norm_ffn,
                attn_w_qkv=v_attn_w_qkv, attn_w_out=v_attn_w_out, attn_q_gain=v_attn_q_gain,
                attn_k_gain=v_attn_k_gain, hgrn_w_in=v_hgrn_w_in, hgrn_w_out=v_hgrn_w_out, hgrn_gnorm=v_hgrn_gnorm,
                hgrn_lower_bounds=v_hgrn_lower_bounds, ffn_w1=v_ffn_w1, ffn_w3=v_ffn_w3, ffn_w2=v_ffn_w2)

    transposed = ("ffn_w1", "ffn_w3")
    for group in (weights, mom1, mom2):
        for k in transposed:
            group[k] = jnp.transpose(group[k], (0, 2, 1))

    xi, yi, ci = _position()
    chip = 2 * xi + yi
    dev = 4 * xi + 2 * yi + ci
    place = jnp.stack([ci, chip]).astype(jnp.int32)
    d = D_MODEL

    big_names = ("attn_w_qkv", "attn_w_out", "hgrn_w_in", "hgrn_w_out", "ffn_w1", "ffn_w3", "ffn_w2")
    early_names, late_names = big_names[:1], big_names[1:]
    slabs16 = {}
    for group in (("attn_w_qkv",), ("attn_w_out", "hgrn_w_out"), ("hgrn_w_in",), ("ffn_w1", "ffn_w3", "ffn_w2")):
        slabs16.update(zip(group, _cast_bf16(place, [weights[k] for k in group], "cast_" + group[0])))
    wg = dict(zip(early_names, _gather_weights([slabs16[k] for k in early_names], "gather_early")))

    c_all = _small_allgather(c.reshape(8, 128), "gather_c").reshape(N_DEV, d)
    ada_b_cols = lax.dynamic_slice(ada_b, (0, chip * ADA_SHARD), (DEPTH, ADA_SHARD)).reshape(DEPTH, 1, ADA_SHARD)
    mod_shard = _ada_fwd(c_all, ada_w, ada_b_cols, "ada_fwd")
    mod_all = _small_allgather(mod_shard.reshape(-1, 128), "gather_mod").reshape(N_DEV, DEPTH, N_DEV, ADA_SHARD)
    mod_mine = lax.dynamic_index_in_dim(mod_all[0::2], dev, axis=2, keepdims=False)
    mod = jnp.transpose(mod_mine, (1, 0, 2)).reshape(DEPTH, 6 * d)

    def mods(layer):
        return [mod[layer:layer + 1, j * d:(j + 1) * d] for j in range(6)]

    x0 = x.reshape(SEQ, d)
    target = loss_target.reshape(SEQ, d)
    qg = attn_q_gain.reshape(len(GROUPS), 1, HEAD_DIM)
    kg = attn_k_gain.reshape(len(GROUPS), 1, HEAD_DIM)
    bias = _attn_bias(rel_bias, "attn_bias")
    lb1 = _lower_bounds(hgrn_lower_bounds, "lower_bounds")[1:2]

    def ffn_fwd(layer, x_in, sc2, sh2, g2):
        a1, a3, u, hf = _ffn_up(x_in, norm_ffn[layer:layer + 1], sc2, sh2, wg["ffn_w1"], wg["ffn_w3"], layer,
                                f"l{layer}_ffn_up")
        z, x_out = _mm_rows(u, wg["ffn_w2"], layer, x_in, g2, f"l{layer}_ffn_down")
        return x_out, (hf, a1, a3, u, z)

    def ffn_bwd(layer, dz, dg2, dx_out, x_in, sc2, sh2, saved, mixer_branch, halves=()):
        hf, a1, a3, u, _ = saved
        da1, da3, *recv = _ffn_down_bwd(dz, wg["ffn_w2"], layer, a1, a3, f"l{layer}_ffn_down_bwd", halves=halves)
        dw2 = _mm_rows_bwd_w(u, dz, f"l{layer}_dw2")
        dh = _ffn_up_bwd(da1, da3, wg["ffn_w1"], wg["ffn_w3"], layer, f"l{layer}_ffn_up_bwd")
        dw1, dw3 = _mm_rows_bwd_w_multi([da1, da3], hf, f"l{layer}_dw13")
        dx_in, dsc2, dsh2, dnf, dz_mix, dg_mix = _norm_mod_bwd(x_in, norm_ffn[layer:layer + 1], sc2, sh2, dh, dx_out,
                                                               f"l{layer}_norm_ffn_bwd", branch=mixer_branch)
        return dx_in, (dw1, dw3, dw2), (dsh2, dsc2, dg2), dnf, recv, dz_mix, dg_mix

    def rs_batched(fn, prefix, tags, *columns):
        out = [None] * len(tags)
        by_shape = {}
        for idx, arr in enumerate(columns[0]):
            by_shape.setdefault(arr.shape, []).append(idx)
        for idxs in by_shape.values():
            for lo in range(0, len(idxs), 3):
                sel = idxs[lo:lo + 3]
                k, layer = tags[sel[0]]
                res = fn(place, *[[col[i] for i in sel] for col in columns], f"{prefix}_{k}_{layer}_x{len(sel)}")
                for i, r in zip(sel, res):
                    out[i] = r
        return out

    def rs_add(tags, grads_in, recv):
        return rs_batched(_rs_add_cast, "rs_add", tags, grads_in, list(recv))

    sh1_0, sc1_0, g1_0, sh2_0, sc2_0, g2_0 = mods(0)
    w_qkv9 = _retile_cols(wg["attn_w_qkv"].reshape(N_CHIP, d, 2304), n_out=9, width_out=d, tn=256,
                          src_map=_qkv_chip_map, dst_map=_qkv_group_map, n_tiles=36,
                          name="regroup_w_qkv").reshape(9, 1, d, d)
    qkv9, h0 = _mm_cols(x0, norm_mix[0:1], sc1_0, sh1_0, w_qkv9, 0, n_blocks=9, width=d, tn=d,
                        act_map=_block_map, w_map=_block_map, out_dtype=F32, name="l0_qkv")
    o4, lse, *late = _attn_fwd(qkv9, qg, kg, bias, "l0_attn", gather=[slabs16[k] for k in late_names])
    wg.update(zip(late_names, _gather_weights(late, "gather_late_siblings", ici=False)))
    y0, x1 = _mm_rows(o4, wg["attn_w_out"], 0, x0, g1_0, "l0_attn_out")
    x2, ffn0 = ffn_fwd(0, x1, sc2_0, sh2_0, g2_0)

    sh1_1, sc1_1, g1_1, sh2_1, sc2_1, g2_1 = mods(1)
    proj4, h1 = _mm_cols(x2, norm_mix[1:2], sc1_1, sh1_1, wg["hgrn_w_in"], 0, n_blocks=4, width=d, tn=d,
                         act_map=_block_map, w_map=_block_map, out_dtype=F32, name="l1_hgrn_in")
    o_raw, yg4, states = _hgrn_fwd(proj4, lb1, hgrn_gnorm, "l1_hgrn")
    y1, x3 = _mm_rows(yg4, wg["hgrn_w_out"], 0, x2, g1_1, "l1_hgrn_out")
    x4, ffn1 = ffn_fwd(1, x3, sc2_1, sh2_1, g2_1)

    dx4, loss_part, dz_ffn1, dg2_1 = _loss_head(x4, target, ffn1[4], g2_1, "loss_head")
    loss = lax.psum(loss_part[0, 0], ("x", "y", "c"))

    dx3, (dw1_1, dw3_1, dw2_1), dmod2_1, dnf_1, _, dzm1, dg1_1 = ffn_bwd(
        1, dz_ffn1, dg2_1, dx4, x3, sc2_1, sh2_1, ffn1, (y1, g1_1))
    dyg4 = _mm_rows_bwd_a(dzm1, wg["hgrn_w_out"], 0, "l1_hgrn_out_bwd")
    dw_hout = _mm_rows_bwd_w(yg4, dzm1, "l1_dw_hgrn_out")
    dproj4, dlb_h, dgn_h = _hgrn_bwd(proj4, lb1, hgrn_gnorm, o_raw, dyg4, states, "l1_hgrn_bwd")
    dh1 = _mm_cols_bwd_a(dproj4, wg["hgrn_w_in"], 0, group=N_CHIP, name="l1_hgrn_in_bwd", tm=512)
    dw_hin = _mm_cols_bwd_w(h1, dproj4, ns=d, tn=d, act_map=_block_map, w_map=_block_map, n_tiles=N_CHIP,
                            name="l1_dw_hgrn_in", tm=2048)
    dx2, dsc1_1, dsh1_1, dnm_1, dz_ffn0, dg2_0 = _norm_mod_bwd(x2, norm_mix[1:2], sc1_1, sh1_1, dh1, dx3,
                                                               "l1_norm_mix_bwd", branch=(ffn0[4], g2_0))

    tags_1 = [("hgrn_w_in", 0), ("hgrn_w_out", 0), ("ffn_w1", 1), ("ffn_w3", 1), ("ffn_w2", 1)]
    grads_1 = [dw_hin, dw_hout, dw1_1, dw3_1, dw2_1]
    dx1, (dw1_0, dw3_0, dw2_0), dmod2_0, dnf_0, recv_1, dzm0, dg1_0 = ffn_bwd(
        0, dz_ffn0, dg2_0, dx2, x1, sc2_0, sh2_0, ffn0, (y0, g1_0), halves=grads_1)
    tags_0 = [("ffn_w1", 0), ("ffn_w3", 0), ("ffn_w2", 0)]
    grads_0 = [dw1_0, dw3_0, dw2_0]
    do4, *recv_0 = _mm_rows_bwd_a(dzm0, wg["attn_w_out"], 0, "l0_attn_out_bwd", halves=grads_0)
    dw_aout = _mm_rows_bwd_w(o4, dzm0, "l0_dw_attn_out")
    tags_a = tags_1 + tags_0
    parts_a = rs_add(tags_1, grads_1, recv_1) + rs_add(tags_0, grads_0, recv_0)
    dqkv, dqg_h, dkg_h, dbias, *got_a = _attn_bwd(qkv9, qg, kg, bias, do4, o4, lse, "l0_attn_bwd", scatter=parts_a)
    dqkv9 = dqkv.reshape(9, SEQ, d)
    dw_qkv9 = _mm_cols_bwd_w(h0, dqkv9, ns=d, tn=d, act_map=_block_map, w_map=_block_map, n_tiles=9,
                             name="l0_dw_qkv", tm=2048, n_out=9)
    dw_qkv = _retile_cols(dw_qkv9, n_out=N_CHIP, width_out=2304, tn=256, src_map=_qkv_group_map,
                          dst_map=_qkv_chip_map, n_tiles=36, name="regroup_dw_qkv")
    tags_b = [("attn_w_qkv", 0), ("attn_w_out", 0)]
    grads_b = [dw_qkv, dw_aout]
    parts_b = rs_add(tags_b, grads_b, _rs_exchange_halves(grads_b, "rs_exchange_halves_b"))
    dh0, *got_b = _mm_cols_bwd_a(dqkv9, w_qkv9, 0, group=3, name="l0_qkv_bwd", scatter=parts_b)
    dx0, dsc1_0, dsh1_0, dnm_0 = _norm_mod_bwd(x0, norm_mix[0:1], sc1_0, sh1_0, dh0, dx1, "l0_norm_mix_bwd")
    drb8 = _relbias_bwd(dbias, jnp.asarray(_bias_tables()), "rel_bias_bwd")

    small = _pack_rows([
        dsh1_0, dsc1_0, dg1_0, *dmod2_0, dsh1_1, dsc1_1, dg1_1, *dmod2_1,
        dnm_0, dnm_1, dnf_0, dnf_1,
        jnp.transpose(dqg_h, (1, 0, 2, 3)), jnp.transpose(dkg_h, (1, 0, 2, 3)), dgn_h, dlb_h, drb8])
    small_all = _small_allgather(small, "gather_small")
    main, gains, dlbnd, rbt = _small_totals(small_all, hgrn_lower_bounds.reshape(DEPTH, 8, 128), "small_totals")
    ng = len(GROUPS)
    grads = {
        "ada_b": main[_R_DMOD:_R_NMIX].reshape(DEPTH, 6 * d),
        "norm_mix": main[_R_NMIX:_R_NFFN].reshape(DEPTH, d),
        "norm_ffn": main[_R_NFFN:_R_QG].reshape(DEPTH, d),
        "attn_q_gain": gains[0:ng].reshape(1, ng, HEAD_DIM),
        "attn_k_gain": gains[ng:2 * ng].reshape(1, ng, HEAD_DIM),
        "hgrn_gnorm": gains[2 * ng:2 * ng + 1],
        "hgrn_lower_bounds": dlbnd.reshape(DEPTH, d),
        "rel_bias": jnp.transpose(rbt[:, :ng * NUM_BUCKETS].reshape(HEADS, ng, NUM_BUCKETS), (2, 1, 0))
                       .reshape(NUM_BUCKETS, ng * HEADS),
    }
    dmod_all = small_all[:, _R_DMOD:_R_NMIX].reshape(N_DEV, DEPTH, 6 * d)
    dmod_cols = jnp.transpose(lax.dynamic_slice(dmod_all, (0, 0, chip * ADA_SHARD), (N_DEV, DEPTH, ADA_SHARD)),
                              (1, 0, 2))
    grad_ada_w = _ada_bwd(c_all, dmod_cols, "ada_bwd")

    tags = tags_a + tags_b
    halves = rs_batched(_rs_sum4, "rs_sum", tags, parts_a + parts_b, list(got_a) + list(got_b))
    full = dict(zip(tags, _rs_join_halves(halves, "rs_join_halves")))

    out_g, out_d, out_m, out_v = {}, {}, {}, {}
    for k in big_names:
        gs = [full[(k, layer)] for layer in range(weights[k].shape[0])]
        out_g[k], out_d[k], out_m[k], out_v[k] = _adamw(weights[k], gs, mom1[k], mom2[k], "adamw_" + k)
    shp = (1, DEPTH * d, ADA_SHARD)
    res = _adamw(ada_w.reshape(shp), [grad_ada_w.reshape(shp[1:])], m_ada_w.reshape(shp), v_ada_w.reshape(shp),
                 "adamw_ada_w")
    out_g["ada_w"], out_d["ada_w"], out_m["ada_w"], out_v["ada_w"] = [r.reshape(ada_w.shape) for r in res]
    for k in _SMALL_ORDER:
        shp = (1, weights[k].size // weights[k].shape[-1], weights[k].shape[-1])
        res = _adamw(weights[k].reshape(shp), [grads[k].reshape(shp[1:])], mom1[k].reshape(shp),
                     mom2[k].reshape(shp), "adamw_" + k)
        out_g[k], out_d[k], out_m[k], out_v[k] = [r.reshape(weights[k].shape) for r in res]
    for dst in (out_g, out_d, out_m, out_v):
        for k in transposed:
            dst[k] = jnp.transpose(dst[k], (0, 2, 1))

    return (loss, dx0.reshape(x.shape), *[out_g[k] for k in _WEIGHT_ORDER], *[out_d[k] for k in _WEIGHT_ORDER],
            *[out_m[k] for k in _WEIGHT_ORDER], *[out_v[k] for k in _WEIGHT_ORDER])
```

```python
import functools

import numpy as np
import jax
import jax.numpy as jnp
from jax import lax
from jax.experimental import pallas as pl
from jax.experimental.pallas import tpu as pltpu

F32 = jnp.float32
BF16 = jnp.bfloat16

D_MODEL = 1024
SEQ = 4096
N_DEV = 8
N_CHIP = 4
DEPTH = 2
HEADS = 8
HEAD_DIM = 128
GROUPS = ((128, 1), (512, 4), (2048, 16))
ATT_BLK = 128
ATT_WAYS = 4
ATT_STEPS = SEQ // ATT_BLK // ATT_WAYS
NUM_BUCKETS = 32
MAX_DISTANCE = 2048
FFN_HIDDEN = 2816
FFN_SHARD = FFN_HIDDEN // N_CHIP
HG_SUB = 16
HG_TC = 512
HG_HP = 4
RMS_EPS = 1e-6
NEG = -1e30
ATT_SCALE = HEAD_DIM ** -0.5
LOG2_E = 1.4426950408889634
ADAM_LR, ADAM_B1, ADAM_B2, ADAM_EPS, ADAM_WD, ADAM_STEP = 0.001, 0.9, 0.999, 1e-08, 0.01, 10
VMEM_LIMIT = 56 * 1024 * 1024
MESH = pl.DeviceIdType.MESH


def _pcall(body, **kw):
    return pl.pallas_call(body, **kw)


def _cparams(sem=None):
    if sem is None:
        return pltpu.CompilerParams(vmem_limit_bytes=VMEM_LIMIT)
    return pltpu.CompilerParams(dimension_semantics=sem, vmem_limit_bytes=VMEM_LIMIT)


def _sds(shape, dtype):
    return jax.ShapeDtypeStruct(shape, dtype)


def _dot(a, b):
    return jnp.dot(a, b, preferred_element_type=F32)


def _dot_nt(a, b):
    return lax.dot_general(a, b, (((1,), (1,)), ((), ())), preferred_element_type=F32)


def _dot_tn(a, b):
    return lax.dot_general(a, b, (((0,), (0,)), ((), ())), preferred_element_type=F32)


def _sigmoid(x):
    return 1.0 / (1.0 + jnp.exp(-x))


def _silu(x):
    return x * _sigmoid(x)


def _dsilu(x):
    s = _sigmoid(x)
    return s * (1.0 + x * (1.0 - s))


def _norm_mod(x_ref, g_ref, sc_ref, sh_ref):
    xv = x_ref[...]
    rs = lax.rsqrt(jnp.mean(xv * xv, axis=-1, keepdims=True) + RMS_EPS)
    return ((xv * rs * g_ref[...]) * (1.0 + sc_ref[...]) + sh_ref[...]).astype(BF16)


_NORM_SPECS = [pl.BlockSpec((1, D_MODEL), lambda i, t: (0, 0))] * 3


def _gated_branch_bwd(dx, z_ref, gate_ref, dz_ref, dgate_ref):
    dz_ref[...] = (dx * gate_ref[...]).astype(BF16)
    dgate_ref[...] += jnp.sum(dx * z_ref[...], axis=0, keepdims=True)


def _norm_mod_bwd(x, gain, sc, sh, dh, dres, name, branch=None):
    tm = 512
    n_b = 2 if branch else 0

    def body(*refs):
        x_ref, g_ref, sc_ref, sh_ref, dh_ref, dres_ref = refs[:6]
        dx_ref, dsc_ref, dsh_ref, dg_ref = refs[6 + n_b:10 + n_b]

        @pl.when(pl.program_id(0) == 0)
        def _():
            dsc_ref[...] = jnp.zeros_like(dsc_ref)
            dsh_ref[...] = jnp.zeros_like(dsh_ref)
            dg_ref[...] = jnp.zeros_like(dg_ref)
            if branch:
                refs[11 + n_b][...] = jnp.zeros_like(refs[11 + n_b])

        xv = x_ref[...]
        dhv = dh_ref[...]
        rs = lax.rsqrt(jnp.mean(xv * xv, axis=-1, keepdims=True) + RMS_EPS)
        xh = xv * rs
        dsc_ref[...] += jnp.sum(dhv * (xh * g_ref[...]), axis=0, keepdims=True)
        dsh_ref[...] += jnp.sum(dhv, axis=0, keepdims=True)
        dhn = dhv * (1.0 + sc_ref[...])
        dg_ref[...] += jnp.sum(dhn * xh, axis=0, keepdims=True)
        dxh = dhn * g_ref[...]
        dx = dres_ref[...] + rs * (dxh - xh * jnp.mean(dxh * xh, axis=-1, keepdims=True))
        dx_ref[...] = dx
        if branch:
            _gated_branch_bwd(dx, refs[6], refs[7], refs[10 + n_b], refs[11 + n_b])

    vec = pl.BlockSpec((1, D_MODEL), lambda i: (0, 0))
    big = pl.BlockSpec((tm, D_MODEL), lambda i: (i, 0))
    return _pcall(
        body, name=name, grid=(SEQ // tm,),
        in_specs=[big, vec, vec, vec, big, big] + ([big, vec] if branch else []),
        out_specs=[big, vec, vec, vec] + ([big, vec] if branch else []),
        out_shape=[_sds((SEQ, D_MODEL), F32)] + [_sds((1, D_MODEL), F32)] * 3
        + ([_sds((SEQ, D_MODEL), BF16), _sds((1, D_MODEL), F32)] if branch else []),
        compiler_params=_cparams(("arbitrary",)),
    )(x, gain, sc, sh, dh, dres, *(branch or ()))


def _mm_cols(x, gain, sc, sh, wg, layer, *, n_blocks, width, tn, act_map, w_map, out_dtype, name, tm=1024):
    k = x.shape[1]
    n_tiles = n_blocks * width // tn

    def body(x_ref, g_ref, sc_ref, sh_ref, w_ref, o_ref, h_ref):
        @pl.when(pl.program_id(1) == 0)
        def _():
            h_ref[...] = _norm_mod(x_ref, g_ref, sc_ref, sh_ref)

        o_ref[...] = _dot(h_ref[...], w_ref[...]).astype(o_ref.dtype)

    rows = pl.BlockSpec((tm, k), lambda i, t: (i, 0))
    return _pcall(
        body, name=name, grid=(SEQ // tm, n_tiles),
        in_specs=[rows] + _NORM_SPECS
        + [pl.BlockSpec((None, None, k, tn), lambda i, t: (w_map(t)[0], layer, 0, w_map(t)[1]))],
        out_specs=[pl.BlockSpec((None, tm, tn), lambda i, t: (act_map(t)[0], i, act_map(t)[1])), rows],
        out_shape=[_sds((n_blocks, SEQ, width), out_dtype), _sds((SEQ, k), BF16)],
        compiler_params=_cparams(("parallel", "arbitrary")),
    )(x, gain, sc, sh, wg)


def _mm_cols_bwd_a(dout, wg, layer, *, group, name, tm=1024, scatter=()):
    n_blocks, _, width = dout.shape
    k = wg.shape[2]
    n_s = len(scatter)
    n_rows = SEQ // tm
    n_steps = n_blocks // group

    def body(*refs):
        d_ref, w_ref = refs[:2]
        o_ref = refs[2 + n_s]
        if n_s:
            comm_start, comm_wait = _rs_chips(refs[2:2 + n_s], refs[3 + n_s:3 + 2 * n_s], *refs[3 + 2 * n_s:])
            pl.when((pl.program_id(0) == 0) & (pl.program_id(1) == 0))(comm_start)
        acc = _dot_nt(d_ref[0], w_ref[0])
        for b in range(1, group):
            acc += _dot_nt(d_ref[b], w_ref[b])
        if n_steps == 1:
            o_ref[...] = acc
        else:
            @pl.when(pl.program_id(1) == 0)
            def _():
                o_ref[...] = acc

            @pl.when(pl.program_id(1) > 0)
            def _():
                o_ref[...] += acc
        if n_s:
            pl.when((pl.program_id(0) == n_rows - 1) & (pl.program_id(1) == n_steps - 1))(comm_wait)

    sem = pltpu.SemaphoreType.DMA((max(n_s, 1), 3))
    res = _pcall(
        body, name=name, grid=(n_rows, n_steps),
        in_specs=[pl.BlockSpec((group, tm, width), lambda i, t: (t, i, 0)),
                  pl.BlockSpec((group, None, k, width), lambda i, t: (t, layer, 0, 0))] + [_ANY] * n_s,
        out_specs=[pl.BlockSpec((tm, k), lambda i, t: (i, 0))] + [_ANY] * n_s,
        out_shape=[_sds((SEQ, k), F32)] + _rs_chips_shapes(scatter),
        scratch_shapes=[sem, sem] if n_s else [],
        compiler_params=_cparams(("arbitrary", "arbitrary") if n_s else ("parallel", "arbitrary")),
    )(dout, wg, *scatter)
    return res if n_s else res[0]


def _mm_cols_bwd_w(a, dout, *, ns, tn, act_map, w_map, n_tiles, name, tm=1024, n_out=N_CHIP):
    k = a.shape[1]

    def body(a_ref, d_ref, o_ref):
        @pl.when(pl.program_id(1) == 0)
        def _():
            o_ref[...] = jnp.zeros_like(o_ref)

        o_ref[...] += _dot_tn(a_ref[...], d_ref[...])

    return _pcall(
        body, name=name, grid=(n_tiles, SEQ // tm),
        in_specs=[pl.BlockSpec((tm, k), lambda t, i: (i, 0)),
                  pl.BlockSpec((None, tm, tn), lambda t, i: (act_map(t)[0], i, act_map(t)[1]))],
        out_specs=pl.BlockSpec((None, k, tn), lambda t, i: (w_map(t)[0], 0, w_map(t)[1])),
        out_shape=_sds((n_out, k, ns), F32),
        compiler_params=_cparams(("parallel", "arbitrary")),
    )(a, dout)


def _retile_cols(src, *, n_out, width_out, tn, src_map, dst_map, n_tiles, name):
    k = src.shape[1]

    def body(s_ref, o_ref):
        o_ref[...] = s_ref[...]

    return _pcall(
        body, name=name, grid=(n_tiles,),
        in_specs=[pl.BlockSpec((None, k, tn), lambda t: (src_map(t)[0], 0, src_map(t)[1]))],
        out_specs=pl.BlockSpec((None, k, tn), lambda t: (dst_map(t)[0], 0, dst_map(t)[1])),
        out_shape=_sds((n_out, k, width_out), src.dtype),
        compiler_params=_cparams(("parallel",)),
    )(src)


def _mm_rows(a4, wg, layer, x, gate, name, tm=512):
    ks = a4.shape[2]
    n = wg.shape[3]

    def body(a_ref, w_ref, x_ref, g_ref, z_ref, xn_ref):
        z = _dot(a_ref[0], w_ref[0])
        for s in range(1, N_CHIP):
            z += _dot(a_ref[s], w_ref[s])
        z_ref[...] = z.astype(BF16)
        xn_ref[...] = x_ref[...] + g_ref[...] * z

    big = pl.BlockSpec((tm, n), lambda i: (i, 0))
    return _pcall(
        body, name=name, grid=(SEQ // tm,),
        in_specs=[pl.BlockSpec((N_CHIP, tm, ks), lambda i: (0, i, 0)),
                  pl.BlockSpec((N_CHIP, None, ks, n), lambda i: (0, layer, 0, 0)),
                  big, pl.BlockSpec((1, n), lambda i: (0, 0))],
        out_specs=[big, big],
        out_shape=[_sds((SEQ, n), BF16), _sds((SEQ, n), F32)],
        compiler_params=_cparams(("parallel",)),
    )(a4, wg, x, gate)


def _mm_rows_bwd_a(dz, wg, layer, name, tm=1024, halves=()):
    ks, n = wg.shape[2], wg.shape[3]
    n_h = len(halves)
    n_rows = SEQ // tm

    def body(*refs):
        dz_ref, w_ref = refs[:2]
        o_ref = refs[2 + n_h]
        if n_h:
            comm_start, comm_wait = _rs_halves(refs[2:2 + n_h], refs[3 + n_h:3 + 2 * n_h], *refs[3 + 2 * n_h:])
            pl.when((pl.program_id(0) == 0) & (pl.program_id(1) == 0))(comm_start)
        o_ref[...] = _dot_nt(dz_ref[...], w_ref[...])
        if n_h:
            pl.when((pl.program_id(0) == n_rows - 1) & (pl.program_id(1) == N_CHIP - 1))(comm_wait)

    sem = pltpu.SemaphoreType.DMA((max(n_h, 1),))
    res = _pcall(
        body, name=name, grid=(n_rows, N_CHIP),
        in_specs=[pl.BlockSpec((tm, n), lambda i, s: (i, 0)),
                  pl.BlockSpec((None, None, ks, n), lambda i, s: (s, layer, 0, 0))] + [_ANY] * n_h,
        out_specs=[pl.BlockSpec((None, tm, ks), lambda i, s: (s, i, 0))] + [_ANY] * n_h,
        out_shape=[_sds((N_CHIP, SEQ, ks), F32)] + _rs_halves_shapes(halves),
        scratch_shapes=[sem, sem] if n_h else [],
        compiler_params=_cparams(("arbitrary", "arbitrary") if n_h else ("parallel", "arbitrary")),
    )(dz, wg, *halves)
    return res if n_h else res[0]


def _mm_rows_bwd_w(a4, dz, name, tm=2048):
    return _mm_rows_bwd_w_multi([a4], dz, name, tm)[0]


def _mm_rows_bwd_w_multi(a4s, dz, name, tm=2048):
    n_a = len(a4s)
    ks = a4s[0].shape[2]
    n = dz.shape[1]

    def body(*refs):
        dz_ref = refs[n_a]

        for j in range(n_a):
            o_ref = refs[n_a + 1 + j]

            @pl.when(pl.program_id(1) == 0)
            def _(o_ref=o_ref):
                o_ref[...] = jnp.zeros_like(o_ref)

            o_ref[...] += _dot_tn(refs[j][...], dz_ref[...])

    return _pcall(
        body, name=name, grid=(N_CHIP, SEQ // tm),
        in_specs=[pl.BlockSpec((None, tm, ks), lambda s, i: (s, i, 0))] * n_a
        + [pl.BlockSpec((tm, n), lambda s, i: (i, 0))],
        out_specs=[pl.BlockSpec((None, ks, n), lambda s, i: (s, 0, 0))] * n_a,
        out_shape=[_sds((N_CHIP, ks, n), F32)] * n_a,
        compiler_params=_cparams(("parallel", "arbitrary")),
    )(*a4s, dz)


def _ffn_up(x, gain, sc, sh, w1g, w3g, layer, name, tm=1024):
    def body(x_ref, g_ref, sc_ref, sh_ref, w1_ref, w3_ref, a1_ref, a3_ref, u_ref, h_ref):
        @pl.when(pl.program_id(1) == 0)
        def _():
            h_ref[...] = _norm_mod(x_ref, g_ref, sc_ref, sh_ref)

        hv = h_ref[...]
        a1 = _dot_nt(hv, w1_ref[...])
        a3 = _dot_nt(hv, w3_ref[...])
        a1_ref[...] = a1.astype(BF16)
        a3_ref[...] = a3.astype(BF16)
        u_ref[...] = (_silu(a1) * a3).astype(BF16)

    wspec = pl.BlockSpec((None, None, FFN_SHARD, D_MODEL), lambda i, s: (s, layer, 0, 0))
    ospec = pl.BlockSpec((None, tm, FFN_SHARD), lambda i, s: (s, i, 0))
    shp = (N_CHIP, SEQ, FFN_SHARD)
    rows = pl.BlockSpec((tm, D_MODEL), lambda i, s: (i, 0))
    return _pcall(
        body, name=name, grid=(SEQ // tm, N_CHIP),
        in_specs=[rows] + _NORM_SPECS + [wspec, wspec],
        out_specs=[ospec, ospec, ospec, rows],
        out_shape=[_sds(shp, BF16), _sds(shp, BF16), _sds(shp, BF16), _sds((SEQ, D_MODEL), BF16)],
        compiler_params=_cparams(("parallel", "arbitrary")),
    )(x, gain, sc, sh, w1g, w3g)


def _ffn_up_bwd(da1, da3, w1g, w3g, layer, name, tm=512):
    def body(d1_ref, d3_ref, w1_ref, w3_ref, o_ref):
        acc = _dot(d1_ref[0], w1_ref[0]) + _dot(d3_ref[0], w3_ref[0])
        for s in range(1, N_CHIP):
            acc += _dot(d1_ref[s], w1_ref[s]) + _dot(d3_ref[s], w3_ref[s])
        o_ref[...] = acc

    wspec = pl.BlockSpec((N_CHIP, None, FFN_SHARD, D_MODEL), lambda i: (0, layer, 0, 0))
    dspec = pl.BlockSpec((N_CHIP, tm, FFN_SHARD), lambda i: (0, i, 0))
    return _pcall(
        body, name=name, grid=(SEQ // tm,),
        in_specs=[dspec, dspec, wspec, wspec],
        out_specs=pl.BlockSpec((tm, D_MODEL), lambda i: (i, 0)),
        out_shape=_sds((SEQ, D_MODEL), F32),
        compiler_params=_cparams(("parallel",)),
    )(da1, da3, w1g, w3g)


def _ffn_down_bwd(dz, w2g, layer, a1, a3, name, tm=1024, halves=()):
    n_h = len(halves)
    n_rows = SEQ // tm

    def body(*refs):
        dz_ref, w_ref, a1_ref, a3_ref = refs[:4]
        da1_ref, da3_ref = refs[4 + n_h:6 + n_h]
        if n_h:
            comm_start, comm_wait = _rs_halves(refs[4:4 + n_h], refs[6 + n_h:6 + 2 * n_h], *refs[6 + 2 * n_h:])
            pl.when((pl.program_id(0) == 0) & (pl.program_id(1) == 0))(comm_start)
        du = _dot_nt(dz_ref[...], w_ref[...])
        a1 = a1_ref[...].astype(F32)
        da1_ref[...] = (du * a3_ref[...].astype(F32) * _dsilu(a1)).astype(BF16)
        da3_ref[...] = (du * _silu(a1)).astype(BF16)
        if n_h:
            pl.when((pl.program_id(0) == n_rows - 1) & (pl.program_id(1) == N_CHIP - 1))(comm_wait)

    blk = pl.BlockSpec((None, tm, FFN_SHARD), lambda i, s: (s, i, 0))
    shp = (N_CHIP, SEQ, FFN_SHARD)
    sem = pltpu.SemaphoreType.DMA((max(n_h, 1),))
    return _pcall(
        body, name=name, grid=(n_rows, N_CHIP),
        in_specs=[pl.BlockSpec((tm, D_MODEL), lambda i, s: (i, 0)),
                  pl.BlockSpec((None, None, FFN_SHARD, D_MODEL), lambda i, s: (s, layer, 0, 0)),
                  blk, blk] + [_ANY] * n_h,
        out_specs=[blk, blk] + [_ANY] * n_h,
        out_shape=[_sds(shp, BF16), _sds(shp, BF16)] + _rs_halves_shapes(halves),
        scratch_shapes=[sem, sem] if n_h else [],
        compiler_params=_cparams(("arbitrary", "arbitrary") if n_h else ("parallel", "arbitrary")),
    )(dz, w2g, a1, a3, *halves)


def _loss_head(y, target, z, gate, name):
    tm = 512
    n_steps = SEQ // tm

    def body(y_ref, t_ref, z_ref, gate_ref, dy_ref, l_ref, dz_ref, dgate_ref, acc_ref):
        @pl.when(pl.program_id(0) == 0)
        def _():
            acc_ref[...] = jnp.zeros_like(acc_ref)
            dgate_ref[...] = jnp.zeros_like(dgate_ref)

        err = y_ref[...] - t_ref[...]
        dy = err * (1.0 / D_MODEL)
        dy_ref[...] = dy
        acc_ref[...] += jnp.sum(jnp.mean(err * err, axis=-1, keepdims=True), axis=0, keepdims=True)
        _gated_branch_bwd(dy, z_ref, gate_ref, dz_ref, dgate_ref)

        @pl.when(pl.program_id(0) == n_steps - 1)
        def _():
            l_ref[...] = 0.5 * acc_ref[...]

    big = pl.BlockSpec((tm, D_MODEL), lambda i: (i, 0))
    vec = pl.BlockSpec((1, D_MODEL), lambda i: (0, 0))
    return _pcall(
        body, name=name, grid=(n_steps,),
        in_specs=[big, big, big, vec],
        out_specs=[big, pl.BlockSpec((1, 1), lambda i: (0, 0)), big, vec],
        out_shape=[_sds((SEQ, D_MODEL), F32), _sds((1, 1), F32), _sds((SEQ, D_MODEL), BF16),
                   _sds((1, D_MODEL), F32)],
        scratch_shapes=[pltpu.VMEM((1, 1), F32)],
        compiler_params=_cparams(("arbitrary",)),
    )(y, target, z, gate)


def _attn_rows(base, d):
    if d == 1:
        return pl.ds(pl.multiple_of(base, ATT_BLK), ATT_BLK)
    return pl.ds(base, ATT_BLK, stride=d)


def _attn_block_index(i, d):
    nb = SEQ // (ATT_BLK * d)
    r = i // nb
    n = i % nb
    base = r + n * (ATT_BLK * d)
    pbase = jnp.maximum(base - ATT_BLK * d, r)
    return n, _attn_rows(base, d), _attn_rows(pbase, d)


def _attn_two_blocks(ref, prow, rows):
    return jnp.concatenate([ref[prow, :].astype(BF16), ref[rows, :].astype(BF16)], axis=0)


def _attn_block_bias(b_ref, n):
    b = b_ref[...]
    prev_half = lax.broadcasted_iota(jnp.int32, b.shape, 1) < ATT_BLK
    return jnp.where(prev_half & (n == 0), NEG, b)


def _qk_normed(x):
    rs = lax.rsqrt(jnp.mean(x * x, axis=-1, keepdims=True) + RMS_EPS)
    return x * rs, rs


def _attn_fwd(qkv9, qgain, kgain, bias, name, gather=()):
    n_g = len(gather)

    def body(*refs):
        q_ref, k_ref, v_ref, qg_ref, kg_ref, b_ref = refs[:6]
        o_ref, lse_ref = refs[6 + n_g:8 + n_g]
        qn_s, kn_s, acc_s, m_s, l_s = refs[8 + 2 * n_g:13 + 2 * n_g]
        g = pl.program_id(1)
        if n_g:
            comm_start, comm_wait = _gather_ici(refs[8 + n_g:8 + 2 * n_g], *refs[13 + 2 * n_g:])
            pl.when((pl.program_id(0) == 0) & (g == 0))(comm_start)

        @pl.when(g == 0)
        def _():
            m_s[...] = jnp.full_like(m_s, NEG)
            l_s[...] = jnp.zeros_like(l_s)
            acc_s[...] = jnp.zeros_like(acc_s)

        qn_s[...] = _qk_normed(q_ref[...])[0] * qg_ref[...]
        kn_s[...] = _qk_normed(k_ref[...])[0] * kg_ref[...]

        for gi, (_, d) in enumerate(GROUPS):
            @pl.when(g == gi)
            def _(d=d):
                def block(n, qb, kk, vv, m_old, l_old, acc_old):
                    s = _dot_nt(qb, kk) * ATT_SCALE + _attn_block_bias(b_ref, n)
                    m_new = jnp.maximum(m_old, jnp.max(s, axis=-1, keepdims=True))
                    alpha = jnp.exp(m_old - m_new)
                    p = jnp.exp(s - m_new)
                    l_new = alpha * l_old + jnp.sum(p, axis=-1, keepdims=True)
                    acc_new = alpha * acc_old + _dot(p.astype(BF16), vv)
                    return m_new, l_new, acc_new

                def it(i, carry):
                    where, loaded = [], []
                    for way in range(ATT_WAYS):
                        n, rows, prow = _attn_block_index(i + way * ATT_STEPS, d)
                        where.append(rows)
                        loaded.append((n, qn_s[rows, :].astype(BF16), _attn_two_blocks(kn_s, prow, rows),
                                       _attn_two_blocks(v_ref, prow, rows), m_s[rows, :], l_s[rows, :],
                                       acc_s[rows, :]))
                    results = [block(*vals) for vals in loaded]
                    for rows, (m_new, l_new, acc_new) in zip(where, results):
                        m_s[rows, :] = m_new
                        l_s[rows, :] = l_new
                        acc_s[rows, :] = acc_new
                    return carry

                lax.fori_loop(0, ATT_STEPS, it, 0)

        @pl.when(g == len(GROUPS) - 1)
        def _():
            o_ref[...] = (acc_s[...] / l_s[...]).astype(BF16)
            lse_ref[...] = m_s[...] + jnp.log(l_s[...])

        if n_g:
            pl.when((pl.program_id(0) == HEADS - 1) & (g == len(GROUPS) - 1))(comm_wait)

    def col(j):
        return pl.BlockSpec((None, SEQ, HEAD_DIM), lambda h, g: (g * 3 + j, 0, h))

    gspec = pl.BlockSpec((None, 1, HEAD_DIM), lambda h, g: (g, 0, 0))
    sem = pltpu.SemaphoreType.DMA((max(n_g, 1), 3))
    return _pcall(
        body, name=name, grid=(HEADS, len(GROUPS)),
        in_specs=[col(0), col(1), col(2), gspec, gspec,
                  pl.BlockSpec((None, None, ATT_BLK, 2 * ATT_BLK), lambda h, g: (g, h, 0, 0))] + [_ANY] * n_g,
        out_specs=[pl.BlockSpec((None, SEQ, HEAD_DIM), lambda h, g: (h // 2, 0, h % 2)),
                   pl.BlockSpec((None, SEQ, 1), lambda h, g: (h, 0, 0))] + [_ANY] * n_g,
        out_shape=[_sds((N_CHIP, SEQ, 2 * HEAD_DIM), BF16), _sds((HEADS, SEQ, 1), F32)]
        + [_sds(s.shape, s.dtype) for s in gather],
        input_output_aliases={6 + a: 2 + a for a in range(n_g)},
        scratch_shapes=[pltpu.VMEM((SEQ, HEAD_DIM), F32)] * 3 + [pltpu.VMEM((SEQ, 1), F32)] * 2
        + ([sem, sem] if n_g else []),
        compiler_params=_cparams(("arbitrary", "arbitrary")),
    )(qkv9, qkv9, qkv9, qgain, kgain, bias, *gather)


def _attn_bwd(qkv9, qgain, kgain, bias, do4, o4, lse, name, scatter=()):
    n_s = len(scatter)

    def body(*refs):
        q_ref, k_ref, v_ref, qg_ref, kg_ref, b_ref, do_ref, o_ref, lse_ref = refs[:9]
        dqkv_ref, dqg_ref, dkg_ref, db_ref = refs[9 + n_s:13 + n_s]
        qn_s, kn_s, dq_s, dk_s, dv_s, dl_s = refs[13 + 2 * n_s:19 + 2 * n_s]
        g = pl.program_id(1)
        if n_s:
            comm_start, comm_wait = _rs_chips(refs[9:9 + n_s], refs[13 + n_s:13 + 2 * n_s], *refs[19 + 2 * n_s:])
            pl.when((pl.program_id(0) == 0) & (g == 0))(comm_start)
        qh, rq = _qk_normed(q_ref[...])
        kh, rk = _qk_normed(k_ref[...])
        qn_s[...] = qh * qg_ref[...]
        kn_s[...] = kh * kg_ref[...]
        @pl.when(g == 0)
        def _():
            dl_s[...] = jnp.sum(do_ref[...] * o_ref[...].astype(F32), axis=-1, keepdims=True)

        dk_s[...] = jnp.zeros_like(dk_s)
        dv_s[...] = jnp.zeros_like(dv_s)
        db_ref[...] = jnp.zeros_like(db_ref)

        for gi, (_, d) in enumerate(GROUPS):
            @pl.when(g == gi)
            def _(d=d):
                def block(n, qb, kk, vv, dob, lse_b, dl):
                    s = _dot_nt(qb, kk) * ATT_SCALE + _attn_block_bias(b_ref, n)
                    p = jnp.exp(s - lse_b)
                    ds = p * (_dot_nt(dob, vv) - dl)
                    ds16 = ds.astype(BF16)
                    return (ds, _dot(ds16, kk) * ATT_SCALE, _dot_tn(ds16, qb) * ATT_SCALE,
                            _dot_tn(p.astype(BF16), dob))

                def it(i, carry):
                    where, loaded, old = [], [], []
                    for way in range(ATT_WAYS):
                        n, rows, prow = _attn_block_index(i + way * ATT_STEPS, d)
                        where.append((rows, prow))
                        loaded.append((n, qn_s[rows, :].astype(BF16), _attn_two_blocks(kn_s, prow, rows),
                                       _attn_two_blocks(v_ref, prow, rows), do_ref[rows, :].astype(BF16),
                                       lse_ref[rows, :], dl_s[rows, :]))
                        old.append((dk_s[rows, :], dk_s[prow, :], dv_s[rows, :], dv_s[prow, :]))
                    results = [block(*vals) for vals in loaded]
                    db_ref[...] += functools.reduce(lambda a, b: a + b, [r[0] for r in results])
                    for (rows, prow), (dk_c, dk_p, dv_c, dv_p), (_, dq, dkk, dvv) in zip(where, old, results):
                        dq_s[rows, :] = dq
                        dk_s[prow, :] = dk_p + dkk[:ATT_BLK]
                        dv_s[prow, :] = dv_p + dvv[:ATT_BLK]
                        dk_s[rows, :] = dk_c + dkk[ATT_BLK:]
                        dv_s[rows, :] = dv_c + dvv[ATT_BLK:]
                    return carry

                lax.fori_loop(0, ATT_STEPS, it, 0)

        def norm_bwd(dn, xh, rs, gain):
            dgain = jnp.sum(dn * xh, axis=0, keepdims=True)
            dxh = dn * gain
            return rs * (dxh - xh * jnp.mean(dxh * xh, axis=-1, keepdims=True)), dgain

        dq, dqg = norm_bwd(dq_s[...], qh, rq, qg_ref[...])
        dk, dkg = norm_bwd(dk_s[...], kh, rk, kg_ref[...])
        dqkv_ref[0] = dq.astype(BF16)
        dqkv_ref[1] = dk.astype(BF16)
        dqkv_ref[2] = dv_s[...].astype(BF16)
        dqg_ref[...] = dqg
        dkg_ref[...] = dkg
        if n_s:
            pl.when((pl.program_id(0) == HEADS - 1) & (g == len(GROUPS) - 1))(comm_wait)

    def col(j):
        return pl.BlockSpec((None, SEQ, HEAD_DIM), lambda h, g: (g * 3 + j, 0, h))

    gspec = pl.BlockSpec((None, 1, HEAD_DIM), lambda h, g: (g, 0, 0))
    bspec = pl.BlockSpec((None, None, ATT_BLK, 2 * ATT_BLK), lambda h, g: (g, h, 0, 0))
    hcol = pl.BlockSpec((None, SEQ, HEAD_DIM), lambda h, g: (h // 2, 0, h % 2))
    dgspec = pl.BlockSpec((None, None, 1, HEAD_DIM), lambda h, g: (h, g, 0, 0))
    ng = len(GROUPS)
    sem = pltpu.SemaphoreType.DMA((max(n_s, 1), 3))
    return _pcall(
        body, name=name, grid=(HEADS, ng),
        in_specs=[col(0), col(1), col(2), gspec, gspec, bspec, hcol, hcol,
                  pl.BlockSpec((None, SEQ, 1), lambda h, g: (h, 0, 0))] + [_ANY] * n_s,
        out_specs=[pl.BlockSpec((None, 3, SEQ, HEAD_DIM), lambda h, g: (g, 0, 0, h)), dgspec, dgspec, bspec]
        + [_ANY] * n_s,
        out_shape=[_sds((ng, 3, SEQ, D_MODEL), BF16), _sds((HEADS, ng, 1, HEAD_DIM), F32),
                   _sds((HEADS, ng, 1, HEAD_DIM), F32), _sds((ng, HEADS, ATT_BLK, 2 * ATT_BLK), F32)]
        + _rs_chips_shapes(scatter),
        scratch_shapes=[pltpu.VMEM((SEQ, HEAD_DIM), F32)] * 5 + [pltpu.VMEM((SEQ, 1), F32)]
        + ([sem, sem] if n_s else []),
        compiler_params=_cparams(("arbitrary", "arbitrary")),
    )(qkv9, qkv9, qkv9, qgain, kgain, bias, do4, o4, lse, *scatter)


def _relbias_bwd(dbias, bucket_idx, name):
    ng = len(GROUPS)

    def body(db_ref, idx_ref, o_ref):
        lane = lax.broadcasted_iota(jnp.int32, (HEADS, 128), 1)
        acc = jnp.zeros((HEADS, 128), F32)
        for g in range(ng):
            dbg = db_ref[g]
            idx = idx_ref[g]
            for b in range(NUM_BUCKETS):
                sel = jnp.where((idx == b)[None], dbg, 0.0)
                part = jnp.sum(sel, axis=1)
                val = jnp.sum(part, axis=-1, keepdims=True)
                acc = jnp.where(lane == g * NUM_BUCKETS + b, val, acc)
        o_ref[...] = acc

    return _pcall(body, name=name, out_shape=_sds((HEADS, 128), F32), compiler_params=_cparams())(dbias, bucket_idx)


def _scan16(x, reverse=False):
    row = lax.broadcasted_iota(jnp.int32, x.shape, 0)
    for sh in (1, 2, 4, 8):
        if reverse:
            x = x + jnp.where(row < HG_SUB - sh, pltpu.roll(x, HG_SUB - sh, 0), 0.0)
        else:
            x = x + jnp.where(row >= sh, pltpu.roll(x, sh, 0), 0.0)
    return x


def _hgrn_gates(qr, fr, lbv):
    q = _silu(qr)
    sig = _sigmoid(fr)
    fg = lbv + (1.0 - lbv) * sig
    lf = jnp.log(fg) * LOG2_E
    gcum = _scan16(lf)
    glast = jnp.sum(lf, axis=0, keepdims=True)
    return q, sig, fg, 1.0 - fg, gcum, glast


def _hgrn_intra(q, k, gcum, tri):
    e = jnp.exp2(jnp.where(tri, gcum[:, None, :] - gcum[None, :, :], NEG))
    a = jnp.sum(q[:, None, :] * k[None, :, :] * e, axis=-1, keepdims=True)
    return e, a


def _hgrn_fwd(proj4, lb, gain, name):
    nsub = HG_TC // HG_SUB
    wide = HG_HP * HEAD_DIM

    def body(p_ref, lb_ref, gn_ref, o_ref, y_ref, st_ref, state_s):
        @pl.when(pl.program_id(1) == 0)
        def _():
            state_s[...] = jnp.zeros_like(state_s)

        gnv = gn_ref[...]
        shp = (HG_SUB, HG_SUB, HEAD_DIM)
        tri = lax.broadcasted_iota(jnp.int32, shp, 0) >= lax.broadcasted_iota(jnp.int32, shp, 1)

        def head(qr, fr, vv, gr, lbv, st):
            q, _, _, k, gcum, glast = _hgrn_gates(qr, fr, lbv)
            _, a = _hgrn_intra(q, k, gcum, tri)
            o = jnp.sum(a * vv[None, :, :], axis=1) + _dot_nt((q * jnp.exp2(gcum)).astype(BF16), st.astype(BF16))
            kg = k * jnp.exp2(glast - gcum)
            st_new = st * jnp.exp2(glast) + _dot_tn(vv.astype(BF16), kg.astype(BF16))
            rs = lax.rsqrt(jnp.mean(o * o, axis=-1, keepdims=True) + RMS_EPS)
            return o, (o * rs * gnv * _silu(gr)).astype(BF16), st_new

        def it(i, carry):
            rows = pl.ds(pl.multiple_of(i * HG_SUB, HG_SUB), HG_SUB)
            loaded = []
            for hh in range(HG_HP):
                lanes = pl.ds(hh * HEAD_DIM, HEAD_DIM)
                loaded.append(([p_ref[j, rows, lanes] for j in range(4)], lb_ref[:, lanes], state_s[hh]))
            results = [head(blk[0], blk[1], blk[2], blk[3], lbv, st) for blk, lbv, st in loaded]
            for hh, ((_, _, st), (o, y, st_new)) in enumerate(zip(loaded, results)):
                lanes = pl.ds(hh * HEAD_DIM, HEAD_DIM)
                st_ref[hh, i] = st.astype(BF16)
                state_s[hh] = st_new
                o_ref[rows, lanes] = o
                y_ref[hh // 2, rows, pl.ds((hh % 2) * HEAD_DIM, HEAD_DIM)] = y
            return carry

        lax.fori_loop(0, nsub, it, 0, unroll=2)

    return _pcall(
        body, name=name, grid=(HEADS // HG_HP, SEQ // HG_TC),
        in_specs=[pl.BlockSpec((4, HG_TC, wide), lambda h, j: (0, j, h)),
                  pl.BlockSpec((1, wide), lambda h, j: (0, h)),
                  pl.BlockSpec((1, HEAD_DIM), lambda h, j: (0, 0))],
        out_specs=[pl.BlockSpec((HG_TC, wide), lambda h, j: (j, h)),
                   pl.BlockSpec((HG_HP // 2, HG_TC, 2 * HEAD_DIM), lambda h, j: (h, j, 0)),
                   pl.BlockSpec((HG_HP, nsub, HEAD_DIM, HEAD_DIM), lambda h, j: (h, j, 0, 0))],
        out_shape=[_sds((SEQ, D_MODEL), F32), _sds((N_CHIP, SEQ, 2 * HEAD_DIM), BF16),
                   _sds((HEADS, SEQ // HG_SUB, HEAD_DIM, HEAD_DIM), BF16)],
        scratch_shapes=[pltpu.VMEM((HG_HP, HEAD_DIM, HEAD_DIM), F32)],
        compiler_params=_cparams(("parallel", "arbitrary")),
    )(proj4, lb, gain)


def _hgrn_bwd(proj4, lb, gain, o_raw, dy4, states, name):
    nsub = HG_TC // HG_SUB
    nt = SEQ // HG_TC
    wide = HG_HP * HEAD_DIM

    def body(p_ref, lb_ref, gn_ref, o_ref, dy_ref, st_ref, dp_ref, dlb_ref, dgn_ref, dst_s):
        @pl.when(pl.program_id(1) == 0)
        def _():
            dst_s[...] = jnp.zeros_like(dst_s)
            dlb_ref[...] = jnp.zeros_like(dlb_ref)
            dgn_ref[...] = jnp.zeros_like(dgn_ref)

        gnv = gn_ref[...]
        shp = (HG_SUB, HG_SUB, HEAD_DIM)
        tri = lax.broadcasted_iota(jnp.int32, shp, 0) >= lax.broadcasted_iota(jnp.int32, shp, 1)

        def head(qr, fr, vv, gr, o, dy, lbv, st0, dst):
            q, sig, fg, k, gcum, glast = _hgrn_gates(qr, fr, lbv)
            rs = lax.rsqrt(jnp.mean(o * o, axis=-1, keepdims=True) + RMS_EPS)
            oh = o * rs
            don = dy * _silu(gr)
            dgn = jnp.sum(don * oh, axis=0, keepdims=True)
            dgr = dy * oh * gnv * _dsilu(gr)
            doh = don * gnv
            do = rs * (doh - oh * jnp.mean(doh * oh, axis=-1, keepdims=True))
            dst16 = dst.astype(BF16)
            do16 = do.astype(BF16)
            eg = jnp.exp2(gcum)
            eb = jnp.exp2(glast - gcum)
            e, a = _hgrn_intra(q, k, gcum, tri)
            da = jnp.sum(do[:, None, :] * vv[None, :, :], axis=-1, keepdims=True)
            dae = da * e
            dq = jnp.sum(dae * k[None, :, :], axis=1) + eg * _dot(do16, st0)
            dk_state = eb * _dot(vv.astype(BF16), dst16)
            dk = jnp.sum(dae * q[:, None, :], axis=0) + dk_state
            dv = jnp.sum(a * do[:, None, :], axis=0) + _dot_nt((k * eb).astype(BF16), dst16)
            eglast = jnp.exp2(glast)
            dst_new = dst * eglast + _dot_tn(do16, (q * eg).astype(BF16))
            dglast = jnp.sum(k * dk_state, axis=0, keepdims=True) \
                + eglast * jnp.sum(dst * st0.astype(F32), axis=0, keepdims=True)
            dlf = _scan16(q * dq - k * dk, reverse=True) + dglast
            dfg = dlf / fg - dk
            dlb = jnp.sum(dfg * (1.0 - sig), axis=0, keepdims=True)
            dproj = ((dq * _dsilu(qr)).astype(BF16), (dfg * (1.0 - lbv) * sig * (1.0 - sig)).astype(BF16),
                     dv.astype(BF16), dgr.astype(BF16))
            return dproj, dst_new, dlb, dgn

        def it(ii, carry):
            i = nsub - 1 - ii
            rows = pl.ds(pl.multiple_of(i * HG_SUB, HG_SUB), HG_SUB)
            results = []
            for hh in range(HG_HP):
                lanes = pl.ds(hh * HEAD_DIM, HEAD_DIM)
                blk = [p_ref[j, rows, lanes] for j in range(4)]
                dy = dy_ref[hh // 2, rows, pl.ds((hh % 2) * HEAD_DIM, HEAD_DIM)]
                results.append(head(blk[0], blk[1], blk[2], blk[3], o_ref[rows, lanes], dy,
                                    lb_ref[:, lanes], st_ref[hh, i], dst_s[hh]))
            new_carry = []
            for hh, (dproj, dst_new, dlb, dgn) in enumerate(results):
                lanes = pl.ds(hh * HEAD_DIM, HEAD_DIM)
                dst_s[hh] = dst_new
                for j in range(4):
                    dp_ref[j, rows, lanes] = dproj[j]
                new_carry.append((carry[hh][0] + dlb, carry[hh][1] + dgn))
            return tuple(new_carry)

        zero = jnp.zeros((1, HEAD_DIM), F32)
        sums = lax.fori_loop(0, nsub, it, tuple((zero, zero) for _ in range(HG_HP)), unroll=2)
        for hh in range(HG_HP):
            dlb_ref[hh] += sums[hh][0]
            dgn_ref[hh] += sums[hh][1]

    vspec = pl.BlockSpec((HG_HP, 1, HEAD_DIM), lambda h, j: (h, 0, 0))
    return _pcall(
        body, name=name, grid=(HEADS // HG_HP, nt),
        in_specs=[pl.BlockSpec((4, HG_TC, wide), lambda h, j: (0, nt - 1 - j, h)),
                  pl.BlockSpec((1, wide), lambda h, j: (0, h)),
                  pl.BlockSpec((1, HEAD_DIM), lambda h, j: (0, 0)),
                  pl.BlockSpec((HG_TC, wide), lambda h, j: (nt - 1 - j, h)),
                  pl.BlockSpec((HG_HP // 2, HG_TC, 2 * HEAD_DIM), lambda h, j: (h, nt - 1 - j, 0)),
                  pl.BlockSpec((HG_HP, nsub, HEAD_DIM, HEAD_DIM), lambda h, j: (h, nt - 1 - j, 0, 0))],
        out_specs=[pl.BlockSpec((4, HG_TC, wide), lambda h, j: (0, nt - 1 - j, h)), vspec, vspec],
        out_shape=[_sds((4, SEQ, D_MODEL), BF16), _sds((HEADS, 1, HEAD_DIM), F32), _sds((HEADS, 1, HEAD_DIM), F32)],
        scratch_shapes=[pltpu.VMEM((HG_HP, HEAD_DIM, HEAD_DIM), F32)],
        compiler_params=_cparams(("parallel", "arbitrary")),
    )(proj4, lb, gain, o_raw, dy4, states)


def _t5_bucket(dist):
    n = np.asarray(dist, dtype=np.int64)
    max_exact = NUM_BUCKETS // 2
    large = max_exact + (np.log(np.maximum(n, 1) / max_exact) / np.log(MAX_DISTANCE / max_exact)
                         * (NUM_BUCKETS - max_exact)).astype(np.int64)
    large = np.minimum(large, NUM_BUCKETS - 1)
    return np.where(n < max_exact, n, large).astype(np.int32)


def _bias_tables():
    qi = np.arange(ATT_BLK)[:, None]
    ki = np.arange(2 * ATT_BLK)[None, :]
    j = ATT_BLK + qi - ki
    valid = (j >= 0) & (j <= ATT_BLK)
    return np.stack([np.where(valid, _t5_bucket(np.clip(j, 0, ATT_BLK) * d), -1) for _, d in GROUPS]).astype(np.int32)


def _attn_bias(rel_bias, name):
    idx = _bias_tables()
    ng = len(GROUPS)
    buckets = [sorted(set(idx[g][idx[g] >= 0].tolist())) for g in range(ng)]

    def body(rb_ref, idx_ref, o_ref):
        h = pl.program_id(0)
        for g in range(ng):
            ig = idx_ref[g]
            acc = jnp.full(ig.shape, NEG, F32)
            for b in buckets[g]:
                acc = jnp.where(ig == b, rb_ref[b, g * HEADS + h], acc)
            o_ref[g] = acc

    return _pcall(
        body, name=name, grid=(HEADS,),
        in_specs=[pl.BlockSpec(memory_space=pltpu.SMEM),
                  pl.BlockSpec((ng, ATT_BLK, 2 * ATT_BLK), lambda h: (0, 0, 0))],
        out_specs=pl.BlockSpec((ng, None, ATT_BLK, 2 * ATT_BLK), lambda h: (0, h, 0, 0)),
        out_shape=_sds((ng, HEADS, ATT_BLK, 2 * ATT_BLK), F32),
        compiler_params=_cparams(("parallel",)),
    )(rel_bias, jnp.asarray(idx))


ADA_SHARD = 6 * D_MODEL // N_CHIP
ADA_TN = 512


def _ada_fwd(c_all, ada_w, ada_b_cols, name):
    def body(c_ref, w_ref, b_ref, o_ref):
        ca = _silu(c_ref[...]).astype(BF16)
        o_ref[...] = _dot(ca, w_ref[...].astype(BF16)) + b_ref[...]

    return _pcall(
        body, name=name, grid=(DEPTH, ADA_SHARD // ADA_TN),
        in_specs=[pl.BlockSpec((N_DEV, D_MODEL), lambda l, j: (0, 0)),
                  pl.BlockSpec((None, D_MODEL, ADA_TN), lambda l, j: (l, 0, j)),
                  pl.BlockSpec((None, 1, ADA_TN), lambda l, j: (l, 0, j))],
        out_specs=pl.BlockSpec((None, N_DEV, ADA_TN), lambda l, j: (l, 0, j)),
        out_shape=_sds((DEPTH, N_DEV, ADA_SHARD), F32),
        compiler_params=_cparams(("parallel", "parallel")),
    )(c_all, ada_w, ada_b_cols)


def _ada_bwd(c_all, dmod_cols, name):
    def body(c_ref, d_ref, o_ref):
        ca = _silu(c_ref[...]).astype(BF16)
        o_ref[...] = _dot_tn(ca, d_ref[...].astype(BF16))

    return _pcall(
        body, name=name, grid=(DEPTH, ADA_SHARD // ADA_TN),
        in_specs=[pl.BlockSpec((N_DEV, D_MODEL), lambda l, j: (0, 0)),
                  pl.BlockSpec((None, N_DEV, ADA_TN), lambda l, j: (l, 0, j))],
        out_specs=pl.BlockSpec((None, D_MODEL, ADA_TN), lambda l, j: (l, 0, j)),
        out_shape=_sds((DEPTH, D_MODEL, ADA_SHARD), F32),
        compiler_params=_cparams(("parallel", "parallel")),
    )(c_all, dmod_cols)


def _lower_bounds(logits, name):
    def body(l_ref, o_ref):
        l0 = l_ref[0:1, :]
        l1 = l_ref[1:2, :]
        mx = jnp.maximum(l0, l1)
        e0 = jnp.exp(l0 - mx)
        e1 = jnp.exp(l1 - mx)
        p0 = e0 / (e0 + e1)
        p1 = e1 / (e0 + e1)
        o_ref[0:1, :] = p0 - p0
        o_ref[1:2, :] = (p0 + p1) - p0

    return _pcall(body, name=name, out_shape=_sds((DEPTH, D_MODEL), F32), compiler_params=_cparams())(logits)


_R_DMOD = 0
_R_NMIX = 96
_R_NFFN = 112
_R_QG = 128
_R_KG = 152
_R_GN = 176
_R_LB = 184
_R_RB = 192
SMALL_ROWS = 200


def _small_totals(gathered, logits8, name):
    ng = len(GROUPS)

    def body(g_ref, l_ref, main_ref, gains_ref, dlb_ref, rb_ref):
        tot = g_ref[0]
        for dev in range(1, N_DEV):
            tot = tot + g_ref[dev]
        main_ref[...] = tot[0:_R_QG]
        gains_ref[...] = jnp.zeros_like(gains_ref)
        for g in range(ng):
            gains_ref[g:g + 1, :] = jnp.sum(tot[_R_QG + 8 * g:_R_QG + 8 * g + 8], axis=0, keepdims=True)
            gains_ref[ng + g:ng + g + 1, :] = jnp.sum(tot[_R_KG + 8 * g:_R_KG + 8 * g + 8], axis=0, keepdims=True)
        gains_ref[2 * ng:2 * ng + 1, :] = jnp.sum(tot[_R_GN:_R_GN + 8], axis=0, keepdims=True)
        rb_ref[...] = tot[_R_RB:_R_RB + 8]
        dlb1 = tot[_R_LB:_R_LB + 8]
        l0 = l_ref[0]
        l1 = l_ref[1]
        mx = jnp.maximum(l0, l1)
        e0 = jnp.exp(l0 - mx)
        e1 = jnp.exp(l1 - mx)
        p0 = e0 / (e0 + e1)
        p1 = e1 / (e0 + e1)
        dlb_ref[0] = -p0 * p1 * dlb1
        dlb_ref[1] = p1 * (1.0 - p1) * dlb1

    return _pcall(
        body, name=name,
        out_shape=[_sds((_R_QG, 128), F32), _sds((8, 128), F32), _sds((DEPTH, 8, 128), F32), _sds((8, 128), F32)],
        compiler_params=_cparams(),
    )(gathered, logits8)


def _row_tile(rows):
    return 128 if rows % 128 == 0 else rows


def _adamw(w, grads, m, v, name):
    nl, r, cdim = w.shape
    tr = _row_tile(r)

    def body(*refs):
        g_refs = refs[:nl]
        w_ref, m_ref, v_ref, go_ref, d_ref, mo_ref, vo_ref = refs[nl:]

        def step(g):
            m2 = ADAM_B1 * m_ref[...] + (1.0 - ADAM_B1) * g
            v2 = ADAM_B2 * v_ref[...] + (1.0 - ADAM_B2) * (g * g)
            m_hat = m2 / (1.0 - ADAM_B1 ** ADAM_STEP)
            v_hat = v2 / (1.0 - ADAM_B2 ** ADAM_STEP)
            go_ref[...] = g
            d_ref[...] = -ADAM_LR * (m_hat / (jnp.sqrt(v_hat) + ADAM_EPS) + ADAM_WD * w_ref[...])
            mo_ref[...] = m2
            vo_ref[...] = v2

        if nl == 1:
            step(g_refs[0][...])
        else:
            for layer in range(nl):
                @pl.when(pl.program_id(0) == layer)
                def _(layer=layer):
                    step(g_refs[layer][...])

    big = pl.BlockSpec((None, tr, cdim), lambda l, i: (l, i, 0))
    g_specs = [pl.BlockSpec((tr, cdim), lambda l, i, layer=layer: (jnp.where(l == layer, i, 0), 0))
               for layer in range(nl)]
    shp = _sds((nl, r, cdim), F32)
    return _pcall(
        body, name=name, grid=(nl, r // tr),
        in_specs=g_specs + [big, big, big],
        out_specs=[big, big, big, big],
        out_shape=[shp, shp, shp, shp],
        compiler_params=_cparams(("parallel", "parallel")),
    )(*grads, w, m, v)


def _cast_bf16(place, ws, name):
    n_a = len(ws)
    nl, r, cdim = ws[0].shape
    tr = _row_tile(r)

    def body(place_ref, *refs):
        for a in range(n_a):
            refs[n_a + a][...] = refs[a][...].astype(BF16)

    return _pcall(
        body, name=name,
        grid_spec=pltpu.PrefetchScalarGridSpec(
            num_scalar_prefetch=1, grid=(nl, r // tr),
            in_specs=[pl.BlockSpec((None, tr, cdim), lambda l, i, place_ref: (l, i, 0))] * n_a,
            out_specs=[pl.BlockSpec((None, None, tr, cdim),
                                    lambda l, i, place_ref: (place_ref[1], l, i, 0))] * n_a),
        out_shape=[_sds((N_CHIP, nl, r, cdim), BF16)] * n_a,
        compiler_params=_cparams(("parallel", "parallel")),
    )(place, *ws)


def _rs_add_cast(place, grads, recvs, name):
    n_a = len(grads)
    _, k, n = grads[0].shape
    kh = k // 2
    tr = _row_tile(kh)
    nb = kh // tr

    def body(place_ref, *refs):
        for a in range(n_a):
            refs[2 * n_a + a][...] = (refs[a][...] + refs[n_a + a][...]).astype(BF16)

    half = pl.BlockSpec((None, tr, n), lambda s, i, place_ref: (s, i, 0))
    mine = pl.BlockSpec((None, tr, n), lambda s, i, place_ref: (s, place_ref[0] * nb + i, 0))
    return _pcall(
        body, name=name,
        grid_spec=pltpu.PrefetchScalarGridSpec(
            num_scalar_prefetch=1, grid=(N_CHIP, nb),
            in_specs=[mine] * n_a + [half] * n_a,
            out_specs=[half] * n_a),
        out_shape=[_sds((N_CHIP, kh, n), BF16)] * n_a,
        compiler_params=_cparams(("parallel", "parallel")),
    )(place, *grads, *recvs)


def _rs_sum4(place, parts, gots, name):
    n_a = len(parts)
    _, kh, n = parts[0].shape
    tr = _row_tile(kh)
    nb = kh // tr

    def body(place_ref, *refs):
        for a in range(n_a):
            acc = refs[a][...].astype(F32)
            for j in range(N_CHIP - 1):
                acc = acc + refs[n_a + a][j].astype(F32)
            refs[2 * n_a + a][...] = acc

    return _pcall(
        body, name=name,
        grid_spec=pltpu.PrefetchScalarGridSpec(
            num_scalar_prefetch=1, grid=(nb,),
            in_specs=[pl.BlockSpec((None, tr, n), lambda i, place_ref: (place_ref[1], i, 0))] * n_a
            + [pl.BlockSpec((N_CHIP - 1, tr, n), lambda i, place_ref: (0, i, 0))] * n_a,
            out_specs=[pl.BlockSpec((tr, n), lambda i, place_ref: (place_ref[0] * nb + i, 0))] * n_a),
        out_shape=[_sds((2 * kh, n), F32)] * n_a,
        compiler_params=_cparams(("parallel",)),
    )(place, *parts, *gots)


_ANY = pl.BlockSpec(memory_space=pl.ANY)


def _position():
    return lax.axis_index("x"), lax.axis_index("y"), lax.axis_index("c")


def _other_chips(x, y):
    return [(1 - x, y), (x, 1 - y), (1 - x, 1 - y)]


def _remote(src, dst, send_sem, recv_sem, to):
    return pltpu.make_async_remote_copy(src_ref=src, dst_ref=dst, send_sem=send_sem, recv_sem=recv_sem,
                                        device_id=to, device_id_type=MESH)


def _small_allgather(v, name):
    r = v.shape[0]

    def body(x_ref, out_ref, send_sems, recv_sems, local_sem):
        x, y, c = _position()
        me, sibling = (x, y, c), (x, y, 1 - c)
        chips = _other_chips(x, y)

        def slab(px, py, pc):
            return out_ref.at[4 * px + 2 * py + pc]

        def copy(k, block, to, src=None):
            return _remote(slab(*block) if src is None else src, slab(*block), send_sems.at[k], recv_sems.at[k], to)

        mine = pltpu.make_async_copy(x_ref, slab(*me), local_sem)
        mine.start()
        first = [copy(0, me, sibling, src=x_ref)]
        first += [copy(1 + j, me, (*chip, c), src=x_ref) for j, chip in enumerate(chips)]
        for cp in first:
            cp.start()
        passed = [copy(4 + j, (*chip, c), sibling) for j, chip in enumerate(chips)]
        for j, chip in enumerate(chips):
            copy(1 + j, (*chip, c), me).wait_recv()
            passed[j].start()
        copy(0, sibling, me).wait_recv()
        for j, chip in enumerate(chips):
            copy(4 + j, (*chip, 1 - c), me).wait_recv()
        for cp in first + passed:
            cp.wait_send()
        mine.wait()

    return _pcall(
        body, name=name,
        out_shape=_sds((N_DEV, r, 128), F32),
        in_specs=[pl.BlockSpec(memory_space=pltpu.VMEM)],
        out_specs=pl.BlockSpec(memory_space=pltpu.VMEM),
        scratch_shapes=[pltpu.SemaphoreType.DMA((7,)), pltpu.SemaphoreType.DMA((7,)), pltpu.SemaphoreType.DMA],
        compiler_params=_cparams(),
    )(v)


def _half_rows(core, kh):
    return pl.ds(pl.multiple_of(core * kh, 8), kh)


def _slab_half(ref, chip, core):
    return ref.at[chip, :, _half_rows(core, ref.shape[2] // 2), :]


def _gather_ici(out, send_sems, recv_sems):
    def copies():
        x, y, c = _position()
        for a in range(len(out)):
            for j, (px, py) in enumerate(_other_chips(x, y)):
                mine = _slab_half(out[a], 2 * x + y, c)
                landed = _slab_half(out[a], 2 * px + py, c)
                yield (_remote(mine, mine, send_sems.at[a, j], recv_sems.at[a, j], (px, py, c)),
                       _remote(landed, landed, send_sems.at[a, j], recv_sems.at[a, j], (px, py, c)))

    def start():
        for send, _ in copies():
            send.start()

    def wait():
        for send, recv in copies():
            recv.wait_recv()
            send.wait_send()

    return start, wait


def _gather_d2d(out, send_sems, recv_sems):
    def copies():
        x, y, c = _position()
        for a in range(len(out)):
            for j, (px, py) in enumerate(_other_chips(x, y)):
                landed = _slab_half(out[a], 2 * px + py, c)
                other = _slab_half(out[a], 2 * px + py, 1 - c)
                yield (_remote(landed, landed, send_sems.at[a, j], recv_sems.at[a, j], (x, y, 1 - c)),
                       _remote(other, other, send_sems.at[a, j], recv_sems.at[a, j], (x, y, 1 - c)))

    def start():
        for send, _ in copies():
            send.start()

    def wait():
        for send, recv in copies():
            recv.wait_recv()
            send.wait_send()

    return start, wait


def _gather_weights(slabs, name, ici=True):
    n = len(slabs)

    def body(*refs):
        out = refs[n:2 * n]
        sems = refs[2 * n:]
        if ici:
            start, wait = _gather_ici(out, sems[2], sems[3])
            start()
            wait()
        start, wait = _gather_d2d(out, sems[0], sems[1])
        start()
        wait()

    sem = pltpu.SemaphoreType.DMA((n, 3))
    return _pcall(
        body, name=name,
        out_shape=[_sds(s.shape, BF16) for s in slabs],
        in_specs=[_ANY] * n, out_specs=[_ANY] * n,
        input_output_aliases={a: a for a in range(n)},
        scratch_shapes=[sem, sem] + ([sem, sem] if ici else []),
        compiler_params=_cparams(),
    )(*slabs)


def _rs_halves(grads, out, send_sems, recv_sems):
    def copies():
        x, y, c = _position()
        for a in range(len(grads)):
            kh = grads[a].shape[1] // 2
            yield _remote(grads[a].at[:, _half_rows(1 - c, kh), :], out[a], send_sems.at[a], recv_sems.at[a],
                          (x, y, 1 - c))

    def start():
        for cp in copies():
            cp.start()

    def wait():
        for cp in copies():
            cp.wait()

    return start, wait


def _rs_halves_shapes(grads):
    return [_sds((N_CHIP, g.shape[1] // 2, g.shape[2]), F32) for g in grads]


def _rs_exchange_halves(grads, name):
    n = len(grads)

    def body(*refs):
        start, wait = _rs_halves(refs[:n], refs[n:2 * n], *refs[2 * n:])
        start()
        wait()

    return _pcall(
        body, name=name,
        out_shape=_rs_halves_shapes(grads),
        in_specs=[_ANY] * n, out_specs=[_ANY] * n,
        scratch_shapes=[pltpu.SemaphoreType.DMA((n,)), pltpu.SemaphoreType.DMA((n,))],
        compiler_params=_cparams(),
    )(*grads)


def _rs_chips(parts, out, send_sems, recv_sems):
    def copies():
        x, y, c = _position()
        for a in range(len(parts)):
            for j, (px, py) in enumerate(_other_chips(x, y)):
                got = out[a].at[j]
                yield (_remote(parts[a].at[2 * px + py], got, send_sems.at[a, j], recv_sems.at[a, j], (px, py, c)),
                       _remote(got, got, send_sems.at[a, j], recv_sems.at[a, j], (px, py, c)))

    def start():
        for send, _ in copies():
            send.start()

    def wait():
        for send, recv in copies():
            recv.wait_recv()
            send.wait_send()

    return start, wait


def _rs_chips_shapes(parts):
    return [_sds((N_CHIP - 1,) + p.shape[1:], BF16) for p in parts]


def _rs_join(out, send_sems, recv_sems):
    def copies():
        x, y, c = _position()
        for a in range(len(out)):
            kh = out[a].shape[0] // 2
            mine = out[a].at[_half_rows(c, kh), :]
            theirs = out[a].at[_half_rows(1 - c, kh), :]
            yield (_remote(mine, mine, send_sems.at[a], recv_sems.at[a], (x, y, 1 - c)),
                   _remote(theirs, theirs, send_sems.at[a], recv_sems.at[a], (x, y, 1 - c)))

    def start():
        for send, _ in copies():
            send.start()

    def wait():
        for send, recv in copies():
            recv.wait_recv()
            send.wait_send()

    return start, wait


def _rs_join_halves(fulls, name):
    n = len(fulls)

    def body(*refs):
        start, wait = _rs_join(refs[n:2 * n], *refs[2 * n:])
        start()
        wait()

    return _pcall(
        body, name=name,
        out_shape=[_sds(f.shape, F32) for f in fulls],
        in_specs=[_ANY] * n, out_specs=[_ANY] * n,
        input_output_aliases={a: a for a in range(n)},
        scratch_shapes=[pltpu.SemaphoreType.DMA((n,)), pltpu.SemaphoreType.DMA((n,))],
        compiler_params=_cparams(),
    )(*fulls)


_SMALL_ORDER = ("rel_bias", "ada_b", "norm_mix", "norm_ffn", "attn_q_gain", "attn_k_gain", "hgrn_gnorm",
                "hgrn_lower_bounds")
_WEIGHT_ORDER = ("rel_bias", "ada_w", "ada_b", "norm_mix", "norm_ffn", "attn_w_qkv", "attn_w_out", "attn_q_gain",
                 "attn_k_gain", "hgrn_w_in", "hgrn_w_out", "hgrn_gnorm", "hgrn_lower_bounds", "ffn_w1", "ffn_w3",
                 "ffn_w2")


def _qkv_group_map(t):
    return t // 4, t % 4


def _qkv_chip_map(t):
    return t // 9, t % 9


def _block_map(t):
    return t, 0


def _pack_rows(parts):
    return jnp.concatenate([p.reshape(-1, 128) for p in parts], axis=0)


def kernel(x, c, rel_bias, ada_w, ada_b, norm_mix, norm_ffn, attn_w_qkv, attn_w_out, attn_q_gain, attn_k_gain, hgrn_w_in, hgrn_w_out, hgrn_gnorm, hgrn_lower_bounds, ffn_w1, ffn_w3, ffn_w2, loss_target, m_rel_bias, m_ada_w, m_ada_b, m_norm_mix, m_norm_ffn, m_attn_w_qkv, m_attn_w_out, m_attn_q_gain, m_attn_k_gain, m_hgrn_w_in, m_hgrn_w_out, m_hgrn_gnorm, m_hgrn_lower_bounds, m_ffn_w1, m_ffn_w3, m_ffn_w2, v_rel_bias, v_ada_w, v_ada_b, v_norm_mix, v_norm_ffn, v_attn_w_qkv, v_attn_w_out, v_attn_q_gain, v_attn_k_gain, v_hgrn_w_in, v_hgrn_w_out, v_hgrn_gnorm, v_hgrn_lower_bounds, v_ffn_w1, v_ffn_w3, v_ffn_w2):
    weights = dict(rel_bias=rel_bias, ada_w=ada_w, ada_b=ada_b, norm_mix=norm_mix, norm_ffn=norm_ffn,
                   attn_w_qkv=attn_w_qkv, attn_w_out=attn_w_out, attn_q_gain=attn_q_gain, attn_k_gain=attn_k_gain,
                   hgrn_w_in=hgrn_w_in, hgrn_w_out=hgrn_w_out, hgrn_gnorm=hgrn_gnorm,
                   hgrn_lower_bounds=hgrn_lower_bounds, ffn_w1=ffn_w1, ffn_w3=ffn_w3, ffn_w2=ffn_w2)
    mom1 = dict(rel_bias=m_rel_bias, ada_w=m_ada_w, ada_b=m_ada_b, norm_mix=m_norm_mix, norm_ffn=m_norm_ffn,
                attn_w_qkv=m_attn_w_qkv, attn_w_out=m_attn_w_out, attn_q_gain=m_attn_q_gain,
                attn_k_gain=m_attn_k_gain, hgrn_w_in=m_hgrn_w_in, hgrn_w_out=m_hgrn_w_out, hgrn_gnorm=m_hgrn_gnorm,
                hgrn_lower_bounds=m_hgrn_lower_bounds, ffn_w1=m_ffn_w1, ffn_w3=m_ffn_w3, ffn_w2=m_ffn_w2)
    mom2 = dict(rel_bias=v_rel_bias, ada_w=v_ada_w, ada_b=v_ada_b, norm_mix=v_norm_mix, norm_ffn=v_norm_ffn,
                attn_w_qkv=v_attn_w_qkv, attn_w_out=v_attn_w_out, attn_q_gain=v_attn_q_gain,
                attn_k_gain=v_attn_k_gain, hgrn_w_in=v_hgrn_w_in, hgrn_w_out=v_hgrn_w_out, hgrn_gnorm=v_hgrn_gnorm,
                hgrn_lower_bounds=v_hgrn_lower_bounds, ffn_w1=v_ffn_w1, ffn_w3=v_ffn_w3, ffn_w2=v_ffn_w2)

    transposed = ("ffn_w1", "ffn_w3")
    for group in (weights, mom1, mom2):
        for k in transposed:
            group[k] = jnp.transpose(group[k], (0, 2, 1))

    xi, yi, ci = _position()
    chip = 2 * xi + yi
    dev = 4 * xi + 2 * yi + ci
    place = jnp.stack([ci, chip]).astype(jnp.int32)
    d = D_MODEL

    big_names = ("attn_w_qkv", "attn_w_out", "hgrn_w_in", "hgrn_w_out", "ffn_w1", "ffn_w3", "ffn_w2")
    early_names, late_names = big_names[:1], big_names[1:]
    slabs16 = {}
    for group in (("attn_w_qkv",), ("attn_w_out", "hgrn_w_out"), ("hgrn_w_in",), ("ffn_w1", "ffn_w3", "ffn_w2")):
        slabs16.update(zip(group, _cast_bf16(place, [weights[k] for k in group], "cast_" + group[0])))
    wg = dict(zip(early_names, _gather_weights([slabs16[k] for k in early_names], "gather_early")))

    c_all = _small_allgather(c.reshape(8, 128), "gather_c").reshape(N_DEV, d)
    ada_b_cols = lax.dynamic_slice(ada_b, (0, chip * ADA_SHARD), (DEPTH, ADA_SHARD)).reshape(DEPTH, 1, ADA_SHARD)
    mod_shard = _ada_fwd(c_all, ada_w, ada_b_cols, "ada_fwd")
    mod_all = _small_allgather(mod_shard.reshape(-1, 128), "gather_mod").reshape(N_DEV, DEPTH, N_DEV, ADA_SHARD)
    mod_mine = lax.dynamic_index_in_dim(mod_all[0::2], dev, axis=2, keepdims=False)
    mod = jnp.transpose(mod_mine, (1, 0, 2)).reshape(DEPTH, 6 * d)

    def mods(layer):
        return [mod[layer:layer + 1, j * d:(j + 1) * d] for j in range(6)]

    x0 = x.reshape(SEQ, d)
    target = loss_target.reshape(SEQ, d)
    qg = attn_q_gain.reshape(len(GROUPS), 1, HEAD_DIM)
    kg = attn_k_gain.reshape(len(GROUPS), 1, HEAD_DIM)
    bias = _attn_bias(rel_bias, "attn_bias")
    lb1 = _lower_bounds(hgrn_lower_bounds, "lower_bounds")[1:2]

    def ffn_fwd(layer, x_in, sc2, sh2, g2):
        a1, a3, u, hf = _ffn_up(x_in, norm_ffn[layer:layer + 1], sc2, sh2, wg["ffn_w1"], wg["ffn_w3"], layer,
                                f"l{layer}_ffn_up")
        z, x_out = _mm_rows(u, wg["ffn_w2"], layer, x_in, g2, f"l{layer}_ffn_down")
        return x_out, (hf, a1, a3, u, z)

    def ffn_bwd(layer, dz, dg2, dx_out, x_in, sc2, sh2, saved, mixer_branch, halves=()):
        hf, a1, a3, u, _ = saved
        da1, da3, *recv = _ffn_down_bwd(dz, wg["ffn_w2"], layer, a1, a3, f"l{layer}_ffn_down_bwd", halves=halves)
        dw2 = _mm_rows_bwd_w(u, dz, f"l{layer}_dw2")
        dh = _ffn_up_bwd(da1, da3, wg["ffn_w1"], wg["ffn_w3"], layer, f"l{layer}_ffn_up_bwd")
        dw1, dw3 = _mm_rows_bwd_w_multi([da1, da3], hf, f"l{layer}_dw13")
        dx_in, dsc2, dsh2, dnf, dz_mix, dg_mix = _norm_mod_bwd(x_in, norm_ffn[layer:layer + 1], sc2, sh2, dh, dx_out,
                                                               f"l{layer}_norm_ffn_bwd", branch=mixer_branch)
        return dx_in, (dw1, dw3, dw2), (dsh2, dsc2, dg2), dnf, recv, dz_mix, dg_mix

    def rs_batched(fn, prefix, tags, *columns):
        out = [None] * len(tags)
        by_shape = {}
        for idx, arr in enumerate(columns[0]):
            by_shape.setdefault(arr.shape, []).append(idx)
        for idxs in by_shape.values():
            for lo in range(0, len(idxs), 3):
                sel = idxs[lo:lo + 3]
                k, layer = tags[sel[0]]
                res = fn(place, *[[col[i] for i in sel] for col in columns], f"{prefix}_{k}_{layer}_x{len(sel)}")
                for i, r in zip(sel, res):
                    out[i] = r
        return out

    def rs_add(tags, grads_in, recv):
        return rs_batched(_rs_add_cast, "rs_add", tags, grads_in, list(recv))

    sh1_0, sc1_0, g1_0, sh2_0, sc2_0, g2_0 = mods(0)
    w_qkv9 = _retile_cols(wg["attn_w_qkv"].reshape(N_CHIP, d, 2304), n_out=9, width_out=d, tn=256,
                          src_map=_qkv_chip_map, dst_map=_qkv_group_map, n_tiles=36,
                          name="regroup_w_qkv").reshape(9, 1, d, d)
    qkv9, h0 = _mm_cols(x0, norm_mix[0:1], sc1_0, sh1_0, w_qkv9, 0, n_blocks=9, width=d, tn=d,
                        act_map=_block_map, w_map=_block_map, out_dtype=F32, name="l0_qkv")
    o4, lse, *late = _attn_fwd(qkv9, qg, kg, bias, "l0_attn", gather=[slabs16[k] for k in late_names])
    wg.update(zip(late_names, _gather_weights(late, "gather_late_siblings", ici=False)))
    y0, x1 = _mm_rows(o4, wg["attn_w_out"], 0, x0, g1_0, "l0_attn_out")
    x2, ffn0 = ffn_fwd(0, x1, sc2_0, sh2_0, g2_0)

    sh1_1, sc1_1, g1_1, sh2_1, sc2_1, g2_1 = mods(1)
    proj4, h1 = _mm_cols(x2, norm_mix[1:2], sc1_1, sh1_1, wg["hgrn_w_in"], 0, n_blocks=4, width=d, tn=d,
                         act_map=_block_map, w_map=_block_map, out_dtype=F32, name="l1_hgrn_in")
    o_raw, yg4, states = _hgrn_fwd(proj4, lb1, hgrn_gnorm, "l1_hgrn")
    y1, x3 = _mm_rows(yg4, wg["hgrn_w_out"], 0, x2, g1_1, "l1_hgrn_out")
    x4, ffn1 = ffn_fwd(1, x3, sc2_1, sh2_1, g2_1)

    dx4, loss_part, dz_ffn1, dg2_1 = _loss_head(x4, target, ffn1[4], g2_1, "loss_head")
    loss = lax.psum(loss_part[0, 0], ("x", "y", "c"))

    dx3, (dw1_1, dw3_1, dw2_1), dmod2_1, dnf_1, _, dzm1, dg1_1 = ffn_bwd(
        1, dz_ffn1, dg2_1, dx4, x3, sc2_1, sh2_1, ffn1, (y1, g1_1))
    dyg4 = _mm_rows_bwd_a(dzm1, wg["hgrn_w_out"], 0, "l1_hgrn_out_bwd")
    dw_hout = _mm_rows_bwd_w(yg4, dzm1, "l1_dw_hgrn_out")
    dproj4, dlb_h, dgn_h = _hgrn_bwd(proj4, lb1, hgrn_gnorm, o_raw, dyg4, states, "l1_hgrn_bwd")
    dh1 = _mm_cols_bwd_a(dproj4, wg["hgrn_w_in"], 0, group=N_CHIP, name="l1_hgrn_in_bwd", tm=512)
    dw_hin = _mm_cols_bwd_w(h1, dproj4, ns=d, tn=d, act_map=_block_map, w_map=_block_map, n_tiles=N_CHIP,
                            name="l1_dw_hgrn_in", tm=2048)
    dx2, dsc1_1, dsh1_1, dnm_1, dz_ffn0, dg2_0 = _norm_mod_bwd(x2, norm_mix[1:2], sc1_1, sh1_1, dh1, dx3,
                                                               "l1_norm_mix_bwd", branch=(ffn0[4], g2_0))

    tags_1 = [("hgrn_w_in", 0), ("hgrn_w_out", 0), ("ffn_w1", 1), ("ffn_w3", 1), ("ffn_w2", 1)]
    grads_1 = [dw_hin, dw_hout, dw1_1, dw3_1, dw2_1]
    dx1, (dw1_0, dw3_0, dw2_0), dmod2_0, dnf_0, recv_1, dzm0, dg1_0 = ffn_bwd(
        0, dz_ffn0, dg2_0, dx2, x1, sc2_0, sh2_0, ffn0, (y0, g1_0), halves=grads_1)
    tags_0 = [("ffn_w1", 0), ("ffn_w3", 0), ("ffn_w2", 0)]
    grads_0 = [dw1_0, dw3_0, dw2_0]
    do4, *recv_0 = _mm_rows_bwd_a(dzm0, wg["attn_w_out"], 0, "l0_attn_out_bwd", halves=grads_0)
    dw_aout = _mm_rows_bwd_w(o4, dzm0, "l0_dw_attn_out")
    tags_a = tags_1 + tags_0
    parts_a = rs_add(tags_1, grads_1, recv_1) + rs_add(tags_0, grads_0, recv_0)
    dqkv, dqg_h, dkg_h, dbias, *got_a = _attn_bwd(qkv9, qg, kg, bias, do4, o4, lse, "l0_attn_bwd", scatter=parts_a)
    dqkv9 = dqkv.reshape(9, SEQ, d)
    dw_qkv9 = _mm_cols_bwd_w(h0, dqkv9, ns=d, tn=d, act_map=_block_map, w_map=_block_map, n_tiles=9,
                             name="l0_dw_qkv", tm=2048, n_out=9)
    dw_qkv = _retile_cols(dw_qkv9, n_out=N_CHIP, width_out=2304, tn=256, src_map=_qkv_group_map,
                          dst_map=_qkv_chip_map, n_tiles=36, name="regroup_dw_qkv")
    tags_b = [("attn_w_qkv", 0), ("attn_w_out", 0)]
    grads_b = [dw_qkv, dw_aout]
    parts_b = rs_add(tags_b, grads_b, _rs_exchange_halves(grads_b, "rs_exchange_halves_b"))
    dh0, *got_b = _mm_cols_bwd_a(dqkv9, w_qkv9, 0, group=3, name="l0_qkv_bwd", scatter=parts_b)
    dx0, dsc1_0, dsh1_0, dnm_0 = _norm_mod_bwd(x0, norm_mix[0:1], sc1_0, sh1_0, dh0, dx1, "l0_norm_mix_bwd")
    drb8 = _relbias_bwd(dbias, jnp.asarray(_bias_tables()), "rel_bias_bwd")

    small = _pack_rows([
        dsh1_0, dsc1_0, dg1_0, *dmod2_0, dsh1_1, dsc1_1, dg1_1, *dmod2_1,
        dnm_0, dnm_1, dnf_0, dnf_1,
        jnp.transpose(dqg_h, (1, 0, 2, 3)), jnp.transpose(dkg_h, (1, 0, 2, 3)), dgn_h, dlb_h, drb8])
    small_all = _small_allgather(small, "gather_small")
    main, gains, dlbnd, rbt = _small_totals(small_all, hgrn_lower_bounds.reshape(DEPTH, 8, 128), "small_totals")
    ng = len(GROUPS)
    grads = {
        "ada_b": main[_R_DMOD:_R_NMIX].reshape(DEPTH, 6 * d),
        "norm_mix": main[_R_NMIX:_R_NFFN].reshape(DEPTH, d),
        "norm_ffn": main[_R_NFFN:_R_QG].reshape(DEPTH, d),
        "attn_q_gain": gains[0:ng].reshape(1, ng, HEAD_DIM),
        "attn_k_gain": gains[ng:2 * ng].reshape(1, ng, HEAD_DIM),
        "hgrn_gnorm": gains[2 * ng:2 * ng + 1],
        "hgrn_lower_bounds": dlbnd.reshape(DEPTH, d),
        "rel_bias": jnp.transpose(rbt[:, :ng * NUM_BUCKETS].reshape(HEADS, ng, NUM_BUCKETS), (2, 1, 0))
                       .reshape(NUM_BUCKETS, ng * HEADS),
    }
    dmod_all = small_all[:, _R_DMOD:_R_NMIX].reshape(N_DEV, DEPTH, 6 * d)
    dmod_cols = jnp.transpose(lax.dynamic_slice(dmod_all, (0, 0, chip * ADA_SHARD), (N_DEV, DEPTH, ADA_SHARD)),
                              (1, 0, 2))
    grad_ada_w = _ada_bwd(c_all, dmod_cols, "ada_bwd")

    tags = tags_a + tags_b
    halves = rs_batched(_rs_sum4, "rs_sum", tags, parts_a + parts_b, list(got_a) + list(got_b))
    full = dict(zip(tags, _rs_join_halves(halves, "rs_join_halves")))

    out_g, out_d, out_m, out_v = {}, {}, {}, {}
    for k in big_names:
        gs = [full[(k, layer)] for layer in range(weights[k].shape[0])]
        out_g[k], out_d[k], out_m[k], out_v[k] = _adamw(weights[k], gs, mom1[k], mom2[k], "adamw_" + k)
    shp = (1, DEPTH * d, ADA_SHARD)
    res = _adamw(ada_w.reshape(shp), [grad_ada_w.reshape(shp[1:])], m_ada_w.reshape(shp), v_ada_w.reshape(shp),
                 "adamw_ada_w")
    out_g["ada_w"], out_d["ada_w"], out_m["ada_w"], out_v["ada_w"] = [r.reshape(ada_w.shape) for r in res]
    for k in _SMALL_ORDER:
        shp = (1, weights[k].size // weights[k].shape[-1], weights[k].shape[-1])
        res = _adamw(weights[k].reshape(shp), [grads[k].reshape(shp[1:])], mom1[k].reshape(shp),
                     mom2[k].reshape(shp), "adamw_" + k)
        out_g[k], out_d[k], out_m[k], out_v[k] = [r.reshape(weights[k].shape) for r in res]
    for dst in (out_g, out_d, out_m, out_v):
        for k in transposed:
            dst[k] = jnp.transpose(dst[k], (0, 2, 1))

    return (loss, dx0.reshape(x.shape), *[out_g[k] for k in _WEIGHT_ORDER], *[out_d[k] for k in _WEIGHT_ORDER],
            *[out_m[k] for k in _WEIGHT_ORDER], *[out_v[k] for k in _WEIGHT_ORDER])
```
